```python
import math
import jax, jax.numpy as jnp
from jax import lax
import numpy as np

D_MODEL = 1024
BATCH = 8
SEQ = 8192
DEPTH = 2

CHUNK = 64
DN_HEADS = 4
DN_DK = 128
DN_DV = 128
DN_CONV = 4
SG_GROUPS = 4
SG_GROUP_DIM = 128
SG_BLOCK = 128
FFN_DIM = 2816
FFN_CONV = 3

LN_EPS = 1e-5
RMS_EPS = 1e-6
L2_EPS = 1e-6
DEEPNORM_ALPHA = (2 * DEPTH) ** 0.25
DEEPNORM_BETA = (8 * DEPTH) ** -0.25

QK_W = DN_HEADS * DN_DK
V_W = DN_HEADS * DN_DV
SG_W = SG_GROUPS * SG_GROUP_DIM
IN_SPLIT_SIZES = (2 * QK_W + V_W, V_W, DN_HEADS, DN_HEADS, 2 * SG_W, 2 * D_MODEL)
IN_COLS = sum(IN_SPLIT_SIZES)

kernel_name = "hybrid_deltanet_spatialgate_convffn_deepnorm"


def _split_points():
    pts, acc = [], 0
    for s in IN_SPLIT_SIZES[:-1]:
        acc += s
        pts.append(acc)
    return pts


def layer_norm(x, g, b):
    xf = x.astype(jnp.float32)
    mu = jnp.mean(xf, axis=-1, keepdims=True)
    var = jnp.mean(jnp.square(xf - mu), axis=-1, keepdims=True)
    y = (xf - mu) * lax.rsqrt(var + LN_EPS) * g.astype(jnp.float32) + b.astype(jnp.float32)
    return y.astype(x.dtype)


def causal_depthwise_conv(x, w):
    K, C = w.shape
    return lax.conv_general_dilated(
        x, w[:, None, :].astype(x.dtype), window_strides=(1,), padding=[(K - 1, 0)],
        dimension_numbers=("NWC", "WIO", "NWC"), feature_group_count=C)


def l2norm(x):
    return x * lax.rsqrt(jnp.sum(x * x, axis=-1, keepdims=True) + L2_EPS)


def _to_chunks(t, n):
    b = t.shape[0]
    t = t.reshape((b, n, CHUNK) + t.shape[2:])
    return jnp.swapaxes(t, 2, 3)


def gated_delta_rule(q, k, v, beta, g):
    B, S, H, Dk = q.shape
    Dv = v.shape[-1]
    n = S // CHUNK
    q, k, v = _to_chunks(q, n), _to_chunks(k, n), _to_chunks(v, n)
    beta, g = _to_chunks(beta, n), _to_chunks(g, n)
    G = jnp.cumsum(g, axis=-1)
    causal = jnp.tril(jnp.ones((CHUNK, CHUNK), dtype=bool))
    strict = jnp.tril(jnp.ones((CHUNK, CHUNK), dtype=bool), k=-1)
    diff = G[..., :, None] - G[..., None, :]
    decay = jnp.exp(jnp.where(causal, diff, -jnp.inf))
    kb = k * beta[..., None]
    L = jnp.where(strict, jnp.einsum("bnhid,bnhjd->bnhij", kb, k) * decay, 0.0)
    eye = jnp.eye(CHUNK, dtype=jnp.float32)
    T = lax.linalg.triangular_solve(eye + L, jnp.broadcast_to(eye, L.shape),
                                    left_side=True, lower=True)
    W = jnp.einsum("bnhij,bnhjd->bnhid", T, kb * jnp.exp(G)[..., None])
    U = jnp.einsum("bnhij,bnhjd->bnhid", T, v * beta[..., None])
    A_qk = jnp.einsum("bnhid,bnhjd->bnhij", q, k) * decay
    q_g = q * jnp.exp(G)[..., None]
    G_last = G[..., -1]
    k_d = k * jnp.exp(G_last[..., None] - G)[..., None]
    g_last = jnp.exp(G_last)

    def step(state, inp):
        qg, kd, w, u, aqk, gl = inp
        u_new = u - jnp.einsum("bhck,bhkv->bhcv", w, state)
        o = jnp.einsum("bhck,bhkv->bhcv", qg, state) + jnp.einsum("bhij,bhjv->bhiv", aqk, u_new)
        state = state * gl[..., None, None] + jnp.einsum("bhck,bhcv->bhkv", kd, u_new)
        return state, o

    xs = tuple(jnp.moveaxis(t, 1, 0) for t in (q_g, k_d, W, U, A_qk, g_last))
    s0 = jnp.zeros((B, H, Dk, Dv), jnp.float32)
    _, o = lax.scan(step, s0, xs)
    o = jnp.swapaxes(jnp.moveaxis(o, 0, 1), 2, 3)
    return o.reshape(B, S, H, Dv)


def deltanet_branch(qkv, z, beta_logit, a, conv_w, a_log, dt_bias, norm_w):
    B, S, _ = qkv.shape
    qkv = jax.nn.silu(causal_depthwise_conv(qkv, conv_w))
    q, k, v = jnp.split(qkv, [QK_W, 2 * QK_W], axis=-1)
    f32 = jnp.float32
    q = l2norm(q.reshape(B, S, DN_HEADS, DN_DK).astype(f32)) * (DN_DK ** -0.5)
    k = l2norm(k.reshape(B, S, DN_HEADS, DN_DK).astype(f32))
    v = v.reshape(B, S, DN_HEADS, DN_DV).astype(f32)
    beta = jax.nn.sigmoid(beta_logit.astype(f32))
    g = -jnp.exp(a_log.astype(f32)) * jax.nn.softplus(a.astype(f32) + dt_bias.astype(f32))
    o = gated_delta_rule(q, k, v, beta, g)
    o = o * lax.rsqrt(jnp.mean(o * o, axis=-1, keepdims=True) + RMS_EPS) * norm_w.astype(f32)
    o = o * jax.nn.silu(z.reshape(B, S, DN_HEADS, DN_DV).astype(f32))
    return o.reshape(B, S, V_W).astype(qkv.dtype)


def spatial_gating_branch(uv, ln_g, ln_b, w_s, b_s):
    B, S, _ = uv.shape
    u, v = jnp.split(jax.nn.gelu(uv), 2, axis=-1)
    v = layer_norm(v, ln_g, ln_b)
    n = S // SG_BLOCK
    v = v.reshape(B, n, SG_BLOCK, SG_GROUPS, SG_GROUP_DIM)
    mask = jnp.tril(jnp.ones((SG_BLOCK, SG_BLOCK), dtype=bool))
    w = jnp.where(mask, w_s, 0.0).astype(v.dtype)
    mixed = jnp.einsum("gpq,bnqgc->bnpgc", w, v) + b_s.T[:, :, None].astype(v.dtype)
    return u * mixed.reshape(B, S, SG_W)


def _fwd_setup_inputs(seed: int = 0) -> dict:
    key = jax.random.key(seed)
    ks = jax.random.split(key, 24)
    nrm = jax.random.normal
    D = D_MODEL
    x = nrm(ks[0], (BATCH, SEQ, D), jnp.float32)
    w_in = nrm(ks[1], (DEPTH, D, IN_COLS), jnp.float32) * D ** -0.5
    conv_qkv = nrm(ks[2], (DEPTH, DN_CONV, 2 * QK_W + V_W), jnp.float32) * DN_CONV ** -0.5
    a_log = jnp.log(jax.random.uniform(ks[3], (DEPTH, DN_HEADS), jnp.float32, 1.0, 16.0))
    dt = jnp.exp(jax.random.uniform(ks[4], (DEPTH, DN_HEADS), jnp.float32,
                                    math.log(1e-3), math.log(1e-1)))
    dt_bias = dt + jnp.log(-jnp.expm1(-dt))
    dn_norm_w = 1.0 + 0.02 * nrm(ks[5], (DEPTH, DN_DV), jnp.float32)
    w_branch_a = nrm(ks[6], (DEPTH, V_W, D), jnp.float32) * V_W ** -0.5 * DEEPNORM_BETA
    sg_ln_g = 1.0 + 0.02 * nrm(ks[7], (DEPTH, SG_W), jnp.float32)
    sg_ln_b = 0.02 * nrm(ks[8], (DEPTH, SG_W), jnp.float32)
    w_spatial = nrm(ks[9], (DEPTH, SG_GROUPS, SG_BLOCK, SG_BLOCK), jnp.float32) * SG_BLOCK ** -0.5
    b_spatial = 1.0 + 0.02 * nrm(ks[10], (DEPTH, SG_GROUPS, SG_BLOCK), jnp.float32)
    w_branch_b = nrm(ks[11], (DEPTH, SG_W, D), jnp.float32) * SG_W ** -0.5 * DEEPNORM_BETA
    w_out = nrm(ks[12], (DEPTH, D, D), jnp.float32) * D ** -0.5 * DEEPNORM_BETA
    ln1_g = 1.0 + 0.02 * nrm(ks[13], (DEPTH, D), jnp.float32)
    ln1_b = 0.02 * nrm(ks[14], (DEPTH, D), jnp.float32)
    w_up = nrm(ks[15], (DEPTH, D, 2 * FFN_DIM), jnp.float32) * D ** -0.5
    conv_ffn = nrm(ks[16], (DEPTH, FFN_CONV, 2 * FFN_DIM), jnp.float32) * FFN_CONV ** -0.5
    w_down = nrm(ks[17], (DEPTH, FFN_DIM, D), jnp.float32) * FFN_DIM ** -0.5 * DEEPNORM_BETA
    ln2_g = 1.0 + 0.02 * nrm(ks[18], (DEPTH, D), jnp.float32)
    ln2_b = 0.02 * nrm(ks[19], (DEPTH, D), jnp.float32)
    return {"x": x, "w_in": w_in, "conv_qkv": conv_qkv, "a_log": a_log, "dt_bias": dt_bias,
            "dn_norm_w": dn_norm_w, "w_branch_a": w_branch_a, "sg_ln_g": sg_ln_g,
            "sg_ln_b": sg_ln_b, "w_spatial": w_spatial, "b_spatial": b_spatial,
            "w_branch_b": w_branch_b, "w_out": w_out, "ln1_g": ln1_g, "ln1_b": ln1_b,
            "w_up": w_up, "conv_ffn": conv_ffn, "w_down": w_down, "ln2_g": ln2_g, "ln2_b": ln2_b}


def _fwd_reference(x, w_in, conv_qkv, a_log, dt_bias, dn_norm_w, w_branch_a, sg_ln_g, sg_ln_b,
              w_spatial, b_spatial, w_branch_b, w_out, ln1_g, ln1_b, w_up, conv_ffn, w_down,
              ln2_g, ln2_b):
    pts = _split_points()
    for l in range(DEPTH):
        proj = x @ w_in[l]
        qkv, z, beta_logit, a, sg_uv, gates = jnp.split(proj, pts, axis=-1)
        o_a = deltanet_branch(qkv, z, beta_logit, a, conv_qkv[l], a_log[l], dt_bias[l], dn_norm_w[l])
        o_b = spatial_gating_branch(sg_uv, sg_ln_g[l], sg_ln_b[l], w_spatial[l], b_spatial[l])
        gate_a, gate_b = jnp.split(jax.nn.sigmoid(gates), 2, axis=-1)
        h = gate_a * (o_a @ w_branch_a[l]) + gate_b * (o_b @ w_branch_b[l])
        x = layer_norm(DEEPNORM_ALPHA * x + h @ w_out[l], ln1_g[l], ln1_b[l])
        up = causal_depthwise_conv(x @ w_up[l], conv_ffn[l])
        a_ff, b_ff = jnp.split(up, 2, axis=-1)
        x = layer_norm(DEEPNORM_ALPHA * x + (jax.nn.silu(a_ff) * b_ff) @ w_down[l], ln2_g[l], ln2_b[l])
    return x


import jax as _jax
import jax.numpy as _jnp

TWIN_FORMAT = 'train_step'
FWD_PARAMS = ['x', 'w_in', 'conv_qkv', 'a_log', 'dt_bias', 'dn_norm_w', 'w_branch_a', 'sg_ln_g', 'sg_ln_b', 'w_spatial', 'b_spatial', 'w_branch_b', 'w_out', 'ln1_g', 'ln1_b', 'w_up', 'conv_ffn', 'w_down', 'ln2_g', 'ln2_b']
TWIN_WEIGHTS = ['w_in', 'conv_qkv', 'a_log', 'dt_bias', 'dn_norm_w', 'w_branch_a', 'sg_ln_g', 'sg_ln_b', 'w_spatial', 'b_spatial', 'w_branch_b', 'w_out', 'ln1_g', 'ln1_b', 'w_up', 'conv_ffn', 'w_down', 'ln2_g', 'ln2_b']
TWIN_DIFF_INPUT = 'x'
TWIN_INPUTS = ['x', 'w_in', 'conv_qkv', 'a_log', 'dt_bias', 'dn_norm_w', 'w_branch_a', 'sg_ln_g', 'sg_ln_b', 'w_spatial', 'b_spatial', 'w_branch_b', 'w_out', 'ln1_g', 'ln1_b', 'w_up', 'conv_ffn', 'w_down', 'ln2_g', 'ln2_b', 'loss_target', 'm_w_in', 'm_conv_qkv', 'm_a_log', 'm_dt_bias', 'm_dn_norm_w', 'm_w_branch_a', 'm_sg_ln_g', 'm_sg_ln_b', 'm_w_spatial', 'm_b_spatial', 'm_w_branch_b', 'm_w_out', 'm_ln1_g', 'm_ln1_b', 'm_w_up', 'm_conv_ffn', 'm_w_down', 'm_ln2_g', 'm_ln2_b', 'v_w_in', 'v_conv_qkv', 'v_a_log', 'v_dt_bias', 'v_dn_norm_w', 'v_w_branch_a', 'v_sg_ln_g', 'v_sg_ln_b', 'v_w_spatial', 'v_b_spatial', 'v_w_branch_b', 'v_w_out', 'v_ln1_g', 'v_ln1_b', 'v_w_up', 'v_conv_ffn', 'v_w_down', 'v_ln2_g', 'v_ln2_b']
TWIN_OUTPUTS = ['loss', 'grad_x', 'grad_w_in', 'grad_conv_qkv', 'grad_a_log', 'grad_dt_bias', 'grad_dn_norm_w', 'grad_w_branch_a', 'grad_sg_ln_g', 'grad_sg_ln_b', 'grad_w_spatial', 'grad_b_spatial', 'grad_w_branch_b', 'grad_w_out', 'grad_ln1_g', 'grad_ln1_b', 'grad_w_up', 'grad_conv_ffn', 'grad_w_down', 'grad_ln2_g', 'grad_ln2_b', 'delta_w_in', 'delta_conv_qkv', 'delta_a_log', 'delta_dt_bias', 'delta_dn_norm_w', 'delta_w_branch_a', 'delta_sg_ln_g', 'delta_sg_ln_b', 'delta_w_spatial', 'delta_b_spatial', 'delta_w_branch_b', 'delta_w_out', 'delta_ln1_g', 'delta_ln1_b', 'delta_w_up', 'delta_conv_ffn', 'delta_w_down', 'delta_ln2_g', 'delta_ln2_b', 'new_m_w_in', 'new_m_conv_qkv', 'new_m_a_log', 'new_m_dt_bias', 'new_m_dn_norm_w', 'new_m_w_branch_a', 'new_m_sg_ln_g', 'new_m_sg_ln_b', 'new_m_w_spatial', 'new_m_b_spatial', 'new_m_w_branch_b', 'new_m_w_out', 'new_m_ln1_g', 'new_m_ln1_b', 'new_m_w_up', 'new_m_conv_ffn', 'new_m_w_down', 'new_m_ln2_g', 'new_m_ln2_b', 'new_v_w_in', 'new_v_conv_qkv', 'new_v_a_log', 'new_v_dt_bias', 'new_v_dn_norm_w', 'new_v_w_branch_a', 'new_v_sg_ln_g', 'new_v_sg_ln_b', 'new_v_w_spatial', 'new_v_b_spatial', 'new_v_w_branch_b', 'new_v_w_out', 'new_v_ln1_g', 'new_v_ln1_b', 'new_v_w_up', 'new_v_conv_ffn', 'new_v_w_down', 'new_v_ln2_g', 'new_v_ln2_b']
TWIN_LEAF_KINDS = {'loss': 'loss', 'grad_x': 'grad_x', 'grad_w_in': 'grad_w', 'grad_conv_qkv': 'grad_w', 'grad_a_log': 'grad_w', 'grad_dt_bias': 'grad_w', 'grad_dn_norm_w': 'grad_w', 'grad_w_branch_a': 'grad_w', 'grad_sg_ln_g': 'grad_w', 'grad_sg_ln_b': 'grad_w', 'grad_w_spatial': 'grad_w', 'grad_b_spatial': 'grad_w', 'grad_w_branch_b': 'grad_w', 'grad_w_out': 'grad_w', 'grad_ln1_g': 'grad_w', 'grad_ln1_b': 'grad_w', 'grad_w_up': 'grad_w', 'grad_conv_ffn': 'grad_w', 'grad_w_down': 'grad_w', 'grad_ln2_g': 'grad_w', 'grad_ln2_b': 'grad_w', 'delta_w_in': 'delta_w', 'delta_conv_qkv': 'delta_w', 'delta_a_log': 'delta_w', 'delta_dt_bias': 'delta_w', 'delta_dn_norm_w': 'delta_w', 'delta_w_branch_a': 'delta_w', 'delta_sg_ln_g': 'delta_w', 'delta_sg_ln_b': 'delta_w', 'delta_w_spatial': 'delta_w', 'delta_b_spatial': 'delta_w', 'delta_w_branch_b': 'delta_w', 'delta_w_out': 'delta_w', 'delta_ln1_g': 'delta_w', 'delta_ln1_b': 'delta_w', 'delta_w_up': 'delta_w', 'delta_conv_ffn': 'delta_w', 'delta_w_down': 'delta_w', 'delta_ln2_g': 'delta_w', 'delta_ln2_b': 'delta_w', 'new_m_w_in': 'new_m', 'new_m_conv_qkv': 'new_m', 'new_m_a_log': 'new_m', 'new_m_dt_bias': 'new_m', 'new_m_dn_norm_w': 'new_m', 'new_m_w_branch_a': 'new_m', 'new_m_sg_ln_g': 'new_m', 'new_m_sg_ln_b': 'new_m', 'new_m_w_spatial': 'new_m', 'new_m_b_spatial': 'new_m', 'new_m_w_branch_b': 'new_m', 'new_m_w_out': 'new_m', 'new_m_ln1_g': 'new_m', 'new_m_ln1_b': 'new_m', 'new_m_w_up': 'new_m', 'new_m_conv_ffn': 'new_m', 'new_m_w_down': 'new_m', 'new_m_ln2_g': 'new_m', 'new_m_ln2_b': 'new_m', 'new_v_w_in': 'new_v', 'new_v_conv_qkv': 'new_v', 'new_v_a_log': 'new_v', 'new_v_dt_bias': 'new_v', 'new_v_dn_norm_w': 'new_v', 'new_v_w_branch_a': 'new_v', 'new_v_sg_ln_g': 'new_v', 'new_v_sg_ln_b': 'new_v', 'new_v_w_spatial': 'new_v', 'new_v_b_spatial': 'new_v', 'new_v_w_branch_b': 'new_v', 'new_v_w_out': 'new_v', 'new_v_ln1_g': 'new_v', 'new_v_ln1_b': 'new_v', 'new_v_w_up': 'new_v', 'new_v_conv_ffn': 'new_v', 'new_v_w_down': 'new_v', 'new_v_ln2_g': 'new_v', 'new_v_ln2_b': 'new_v'}


def _forward(args):
    return _fwd_reference(*[args[k] for k in FWD_PARAMS])


def _output_shape():
    def fwd():
        inp = _fwd_setup_inputs(0)
        return _fwd_reference(*[inp[k] for k in FWD_PARAMS])
    out = _jax.eval_shape(fwd)
    return out.shape, out.dtype

N_MICROBATCH = 1
ADAM_LR = 0.001
ADAM_B1 = 0.9
ADAM_B2 = 0.999
ADAM_EPS = 1e-08
ADAM_WD = 0.01
ADAM_STEP = 10
PER_EXAMPLE_BATCH_AXIS = {'x': 0, 'loss_target': 0}
SHARED_INPUTS = []
_WEIGHT_DTYPES = {'w_in': _jnp.float32, 'conv_qkv': _jnp.float32, 'a_log': _jnp.float32, 'dt_bias': _jnp.float32, 'dn_norm_w': _jnp.float32, 'w_branch_a': _jnp.float32, 'sg_ln_g': _jnp.float32, 'sg_ln_b': _jnp.float32, 'w_spatial': _jnp.float32, 'b_spatial': _jnp.float32, 'w_branch_b': _jnp.float32, 'w_out': _jnp.float32, 'ln1_g': _jnp.float32, 'ln1_b': _jnp.float32, 'w_up': _jnp.float32, 'conv_ffn': _jnp.float32, 'w_down': _jnp.float32, 'ln2_g': _jnp.float32, 'ln2_b': _jnp.float32}
MOMENT_SCALE = {'w_in': 1.623452e-02, 'conv_qkv': 1.627000e-02, 'a_log': 1.021654e-01, 'dt_bias': 9.914368e-02, 'dn_norm_w': 4.430501e-02, 'w_branch_a': 3.234795e-02, 'sg_ln_g': 1.591369e-02, 'sg_ln_b': 1.780735e-02, 'w_spatial': 1.611318e-02, 'b_spatial': 2.316008e-02, 'w_branch_b': 5.196347e-02, 'w_out': 6.215562e-02, 'ln1_g': 2.310440e+00, 'ln1_b': 9.501084e-01, 'w_up': 3.284289e-02, 'conv_ffn': 3.310066e-02, 'w_down': 1.072318e-01, 'ln2_g': 4.540764e+01, 'ln2_b': 2.245584e+00}


def _to_microbatches(a, axis):
    t = _jnp.moveaxis(a, axis, 0)
    t = t.reshape((N_MICROBATCH, t.shape[0] // N_MICROBATCH) + t.shape[1:])
    return _jnp.moveaxis(t, 1, axis + 1)


def setup_inputs(seed: int = 0) -> dict:
    inp = _fwd_setup_inputs(seed)
    key = _jax.random.fold_in(_jax.random.key(seed), 7919)
    shape, _ = _output_shape()
    out = dict(inp)
    out["loss_target"] = _jax.random.normal(_jax.random.fold_in(key, 0), shape, _jnp.float32)
    for i, name in enumerate(TWIN_WEIGHTS):
        w = inp[name].astype(_jnp.float32)
        if MOMENT_SCALE is None:
            s = _jnp.sqrt(_jnp.mean(_jnp.square(w)) + 1e-30)
        else:
            s = MOMENT_SCALE[name]
        km, kv = _jax.random.split(_jax.random.fold_in(key, i + 1))
        out[name] = w
        out["m_" + name] = s * _jax.random.normal(km, w.shape, _jnp.float32)
        out["v_" + name] = (s * s) * _jax.random.uniform(kv, w.shape, _jnp.float32, 0.5, 1.5)
    if N_MICROBATCH > 1:
        for name, axis in PER_EXAMPLE_BATCH_AXIS.items():
            out[name] = _to_microbatches(out[name], axis)
    return {'x': out['x'], 'w_in': out['w_in'], 'conv_qkv': out['conv_qkv'], 'a_log': out['a_log'], 'dt_bias': out['dt_bias'], 'dn_norm_w': out['dn_norm_w'], 'w_branch_a': out['w_branch_a'], 'sg_ln_g': out['sg_ln_g'], 'sg_ln_b': out['sg_ln_b'], 'w_spatial': out['w_spatial'], 'b_spatial': out['b_spatial'], 'w_branch_b': out['w_branch_b'], 'w_out': out['w_out'], 'ln1_g': out['ln1_g'], 'ln1_b': out['ln1_b'], 'w_up': out['w_up'], 'conv_ffn': out['conv_ffn'], 'w_down': out['w_down'], 'ln2_g': out['ln2_g'], 'ln2_b': out['ln2_b'], 'loss_target': out['loss_target'], 'm_w_in': out['m_w_in'], 'm_conv_qkv': out['m_conv_qkv'], 'm_a_log': out['m_a_log'], 'm_dt_bias': out['m_dt_bias'], 'm_dn_norm_w': out['m_dn_norm_w'], 'm_w_branch_a': out['m_w_branch_a'], 'm_sg_ln_g': out['m_sg_ln_g'], 'm_sg_ln_b': out['m_sg_ln_b'], 'm_w_spatial': out['m_w_spatial'], 'm_b_spatial': out['m_b_spatial'], 'm_w_branch_b': out['m_w_branch_b'], 'm_w_out': out['m_w_out'], 'm_ln1_g': out['m_ln1_g'], 'm_ln1_b': out['m_ln1_b'], 'm_w_up': out['m_w_up'], 'm_conv_ffn': out['m_conv_ffn'], 'm_w_down': out['m_w_down'], 'm_ln2_g': out['m_ln2_g'], 'm_ln2_b': out['m_ln2_b'], 'v_w_in': out['v_w_in'], 'v_conv_qkv': out['v_conv_qkv'], 'v_a_log': out['v_a_log'], 'v_dt_bias': out['v_dt_bias'], 'v_dn_norm_w': out['v_dn_norm_w'], 'v_w_branch_a': out['v_w_branch_a'], 'v_sg_ln_g': out['v_sg_ln_g'], 'v_sg_ln_b': out['v_sg_ln_b'], 'v_w_spatial': out['v_w_spatial'], 'v_b_spatial': out['v_b_spatial'], 'v_w_branch_b': out['v_w_branch_b'], 'v_w_out': out['v_w_out'], 'v_ln1_g': out['v_ln1_g'], 'v_ln1_b': out['v_ln1_b'], 'v_w_up': out['v_w_up'], 'v_conv_ffn': out['v_conv_ffn'], 'v_w_down': out['v_w_down'], 'v_ln2_g': out['v_ln2_g'], 'v_ln2_b': out['v_ln2_b']}


def _loss(weights, diff, rest, loss_target):
    with _jax.named_scope("forward"):
        args = {**rest, TWIN_DIFF_INPUT: diff, **{k: w.astype(_WEIGHT_DTYPES[k]) for k, w in weights.items()}}
        y = _forward(args)
    with _jax.named_scope("loss_head"):
        err = _jnp.square(y.astype(_jnp.float32) - loss_target)
        return 0.5 * _jnp.sum(_jnp.mean(err, axis=-1)) if err.ndim else 0.5 * err


def _adamw(w, g, m, v):
    m = ADAM_B1 * m + (1.0 - ADAM_B1) * g
    v = ADAM_B2 * v + (1.0 - ADAM_B2) * _jnp.square(g)
    m_hat = m / (1.0 - ADAM_B1 ** ADAM_STEP)
    v_hat = v / (1.0 - ADAM_B2 ** ADAM_STEP)
    delta = -ADAM_LR * (m_hat / (_jnp.sqrt(v_hat) + ADAM_EPS) + ADAM_WD * w)
    return delta, m, v


def reference(x, w_in, conv_qkv, a_log, dt_bias, dn_norm_w, w_branch_a, sg_ln_g, sg_ln_b, w_spatial, b_spatial, w_branch_b, w_out, ln1_g, ln1_b, w_up, conv_ffn, w_down, ln2_g, ln2_b, loss_target, m_w_in, m_conv_qkv, m_a_log, m_dt_bias, m_dn_norm_w, m_w_branch_a, m_sg_ln_g, m_sg_ln_b, m_w_spatial, m_b_spatial, m_w_branch_b, m_w_out, m_ln1_g, m_ln1_b, m_w_up, m_conv_ffn, m_w_down, m_ln2_g, m_ln2_b, v_w_in, v_conv_qkv, v_a_log, v_dt_bias, v_dn_norm_w, v_w_branch_a, v_sg_ln_g, v_sg_ln_b, v_w_spatial, v_b_spatial, v_w_branch_b, v_w_out, v_ln1_g, v_ln1_b, v_w_up, v_conv_ffn, v_w_down, v_ln2_g, v_ln2_b):
    given = dict(x=x, w_in=w_in, conv_qkv=conv_qkv, a_log=a_log, dt_bias=dt_bias, dn_norm_w=dn_norm_w, w_branch_a=w_branch_a, sg_ln_g=sg_ln_g, sg_ln_b=sg_ln_b, w_spatial=w_spatial, b_spatial=b_spatial, w_branch_b=w_branch_b, w_out=w_out, ln1_g=ln1_g, ln1_b=ln1_b, w_up=w_up, conv_ffn=conv_ffn, w_down=w_down, ln2_g=ln2_g, ln2_b=ln2_b, loss_target=loss_target, m_w_in=m_w_in, m_conv_qkv=m_conv_qkv, m_a_log=m_a_log, m_dt_bias=m_dt_bias, m_dn_norm_w=m_dn_norm_w, m_w_branch_a=m_w_branch_a, m_sg_ln_g=m_sg_ln_g, m_sg_ln_b=m_sg_ln_b, m_w_spatial=m_w_spatial, m_b_spatial=m_b_spatial, m_w_branch_b=m_w_branch_b, m_w_out=m_w_out, m_ln1_g=m_ln1_g, m_ln1_b=m_ln1_b, m_w_up=m_w_up, m_conv_ffn=m_conv_ffn, m_w_down=m_w_down, m_ln2_g=m_ln2_g, m_ln2_b=m_ln2_b, v_w_in=v_w_in, v_conv_qkv=v_conv_qkv, v_a_log=v_a_log, v_dt_bias=v_dt_bias, v_dn_norm_w=v_dn_norm_w, v_w_branch_a=v_w_branch_a, v_sg_ln_g=v_sg_ln_g, v_sg_ln_b=v_sg_ln_b, v_w_spatial=v_w_spatial, v_b_spatial=v_b_spatial, v_w_branch_b=v_w_branch_b, v_w_out=v_w_out, v_ln1_g=v_ln1_g, v_ln1_b=v_ln1_b, v_w_up=v_w_up, v_conv_ffn=v_conv_ffn, v_w_down=v_w_down, v_ln2_g=v_ln2_g, v_ln2_b=v_ln2_b)
    weights = {n: given[n] for n in TWIN_WEIGHTS}
    shared = {n: given[n] for n in SHARED_INPUTS}
    per_example = {n: given[n] for n in ['x']}
    grad_fn = _jax.value_and_grad(_loss, argnums=(0, 1))

    def one_microbatch(ex, loss_target):
        ex = dict(ex)
        diff = ex.pop(TWIN_DIFF_INPUT)
        return grad_fn(weights, diff, {**shared, **ex}, loss_target)

    if N_MICROBATCH == 1:
        loss, (grad_w, grad_x) = one_microbatch(per_example, given["loss_target"])
    else:
        def body(carry, xs):
            loss_sum, grad_sum = carry
            l_k, (gw_k, gx_k) = one_microbatch(xs[0], xs[1])
            with _jax.named_scope("update"):
                return (loss_sum + l_k, _jax.tree.map(_jnp.add, grad_sum, gw_k)), gx_k

        init = (_jnp.zeros((), _jnp.float32), _jax.tree.map(_jnp.zeros_like, weights))
        (loss, grad_w), grad_x = _jax.lax.scan(body, init, (per_example, given["loss_target"]))
    with _jax.named_scope("update"):
        delta_w, new_m, new_v = {}, {}, {}
        for n in TWIN_WEIGHTS:
            delta_w[n], new_m[n], new_v[n] = _adamw(weights[n], grad_w[n], given["m_" + n], given["v_" + n])
    return (loss, grad_x, *[grad_w[n] for n in TWIN_WEIGHTS], *[delta_w[n] for n in TWIN_WEIGHTS],
            *[new_m[n] for n in TWIN_WEIGHTS], *[new_v[n] for n in TWIN_WEIGHTS])
```

```python
import functools
import math

import jax
import jax.numpy as jnp
from jax import lax
from jax.experimental import pallas as pl
from jax.experimental.pallas import tpu as pltpu

F32 = jnp.float32
BF16 = jnp.bfloat16
HI = lax.Precision.HIGHEST
MESH = pl.DeviceIdType.MESH

D = 1024
DEPTH = 2
HEADS = 4
DK = 128
CHUNK = 64
QKV_W = 1536
Z_W = 512
SG_W = 512
FFN = 2816
FFN_HALF = FFN // 2
N_CHIPS = 4
LN_EPS = 1e-5
RMS_EPS = 1e-6
L2_EPS = 1e-6
ALPHA = (2 * DEPTH) ** 0.25
ADAM_LR, ADAM_B1, ADAM_B2, ADAM_EPS, ADAM_WD, ADAM_STEP = 0.001, 0.9, 0.999, 1e-08, 0.01, 10

HALO = 16
LANES = 128
IN_COLS_PAD = 5248
C_Z, C_UV, C_GA, C_GB, C_BA = 1536, 2048, 3072, 4096, 5120
VMEM_LIMIT = 56 * 1024 * 1024


def _params(n_grid=1):
    return pltpu.CompilerParams(dimension_semantics=("arbitrary",) * n_grid, vmem_limit_bytes=VMEM_LIMIT)


def _mm(a, b):
    return jnp.dot(a.astype(BF16), b.astype(BF16), preferred_element_type=F32)


def _mm_nt(a, b):
    return lax.dot_general(a.astype(BF16), b.astype(BF16), (((1,), (1,)), ((), ())), preferred_element_type=F32)


def _mm_tn(a, b):
    return lax.dot_general(a.astype(BF16), b.astype(BF16), (((0,), (0,)), ((), ())), preferred_element_type=F32)


def _bdot(a, b, prec=None):
    return lax.dot_general(a, b, (((2,), (1,)), ((0,), (0,))), precision=prec, preferred_element_type=F32)


def _bdot_nt(a, b, prec=None):
    return lax.dot_general(a, b, (((2,), (2,)), ((0,), (0,))), precision=prec, preferred_element_type=F32)


def _ln(x, g, b):
    mu = jnp.mean(x, axis=-1, keepdims=True)
    xc = x - mu
    var = jnp.mean(xc * xc, axis=-1, keepdims=True)
    return xc * lax.rsqrt(var + LN_EPS) * g + b


def _shift_rows(x, s):
    s = s % x.shape[0]
    return x if s == 0 else pltpu.roll(x, s, 0)


@jax.custom_vjp
def _conv(xcat, w):
    k_taps = len(w)
    y = None
    for k in range(k_taps):
        t = _shift_rows(xcat, k_taps - 1 - k)[HALO:] * w[k]
        y = t if y is None else y + t
    return y


def _conv_fwd(xcat, w):
    return _conv(xcat, w), (xcat, w)


def _conv_bwd(res, dy):
    xcat, w = res
    k_taps = len(w)
    dyp = jnp.concatenate([jnp.zeros((HALO, dy.shape[1]), dy.dtype), dy], axis=0)
    dx = None
    dws = []
    for k in range(k_taps):
        s = k_taps - 1 - k
        t = _shift_rows(dyp, -s) * w[k]
        dx = t if dx is None else dx + t
        dws.append(jnp.sum(_shift_rows(xcat, s)[HALO:] * dy, axis=0, keepdims=True))
    return dx, tuple(dws)


_conv.defvjp(_conv_fwd, _conv_bwd)


@jax.custom_vjp
def _tri_inv(l):
    n = l.shape[-1]
    r = lax.broadcasted_iota(jnp.int32, (n, n), 0)
    c = lax.broadcasted_iota(jnp.int32, (n, n), 1)
    eye = (r == c).astype(F32)
    p = eye - l
    lp = l
    steps = int(math.log2(n)) - 1
    for i in range(steps):
        lp = _bdot(lp, lp, HI)
        p = p + _bdot(p, lp, HI)
    return p


def _tri_inv_fwd(l):
    t = _tri_inv(l)
    return t, t


def _tri_inv_bwd(t, dt):
    tt = jnp.swapaxes(t, 1, 2)
    return (-_bdot(tt, _bdot(dt, tt, HI), HI),)


_tri_inv.defvjp(_tri_inv_fwd, _tri_inv_bwd)


def _dn_glue(qkvcat, z, ba, s_in, cw, a_row, dtb_row, nw_row):
    t_rows = z.shape[0]
    nc = t_rows // CHUNK
    nb = nc * HEADS

    qkv = jax.nn.silu(_conv(qkvcat, cw))

    def chunks(t, off):
        return jnp.stack([t[n * CHUNK:(n + 1) * CHUNK, off + h * DK: off + (h + 1) * DK]
                          for n in range(nc) for h in range(HEADS)])

    q = chunks(qkv, 0)
    k = chunks(qkv, 512)
    v = chunks(qkv, 1024)
    q = q * lax.rsqrt(jnp.sum(q * q, axis=-1, keepdims=True) + L2_EPS) * (DK ** -0.5)
    k = k * lax.rsqrt(jnp.sum(k * k, axis=-1, keepdims=True) + L2_EPS)

    lane = lax.broadcasted_iota(jnp.int32, (LANES, HEADS * DK), 0)
    head_of_col = lax.broadcasted_iota(jnp.int32, (LANES, HEADS * DK), 1) // DK
    e_beta = (head_of_col == lane).astype(F32)
    e_g = (head_of_col + HEADS == lane).astype(F32)
    beta_l = jax.nn.sigmoid(ba)
    g_l = -jnp.exp(a_row) * jax.nn.softplus(ba + dtb_row)
    beta = chunks(jnp.dot(beta_l, e_beta, precision=HI, preferred_element_type=F32), 0)
    g = chunks(jnp.dot(g_l, e_g, precision=HI, preferred_element_type=F32), 0)

    r = lax.broadcasted_iota(jnp.int32, (CHUNK, CHUNK), 0)
    c = lax.broadcasted_iota(jnp.int32, (CHUNK, CHUNK), 1)
    causal = r >= c
    strict = r > c
    tril_b = jnp.broadcast_to(causal.astype(F32), (nb, CHUNK, CHUNK))
    ones_b = jnp.ones((nb, CHUNK, CHUNK), F32)
    gi_b = _bdot(tril_b, g, HI)
    sel = (lax.broadcasted_iota(jnp.int32, (nb, CHUNK, DK), 2) == 0).astype(F32)
    gi = _bdot_nt(gi_b, sel, HI)
    gj = _bdot_nt(sel, gi_b, HI)
    decay = jnp.where(causal, jnp.exp(jnp.where(causal, gi - gj, 0.0)), 0.0)
    kb = k * beta
    l_mat = jnp.where(strict, _bdot_nt(kb, k) * decay, 0.0)
    t_mat = _tri_inv(l_mat)
    e_gi = jnp.exp(gi_b)
    w_mat = _bdot(t_mat, kb * e_gi)
    u_mat = _bdot(t_mat, v * beta)
    a_qk = _bdot_nt(q, k) * decay
    q_g = q * e_gi
    gl_b = _bdot(ones_b, g, HI)
    k_d = k * jnp.exp(gl_b - gi_b)
    g_last = jnp.exp(_bdot(jnp.ones((nb, DK, CHUNK), F32), g, HI))

    state = s_in
    rows = []
    for n in range(nc):
        sl = slice(n * HEADS, (n + 1) * HEADS)
        u_new = u_mat[sl] - _bdot(w_mat[sl], state)
        o_n = _bdot(q_g[sl], state) + _bdot(a_qk[sl], u_new)
        state = state * g_last[sl] + _bdot(jnp.swapaxes(k_d[sl], 1, 2), u_new)
        o_n = o_n * lax.rsqrt(jnp.mean(o_n * o_n, axis=-1, keepdims=True) + RMS_EPS) * nw_row
        z_n = jnp.stack([z[n * CHUNK:(n + 1) * CHUNK, h * DK:(h + 1) * DK] for h in range(HEADS)])
        o_n = o_n * jax.nn.silu(z_n)
        rows.append(jnp.concatenate([o_n[h] for h in range(HEADS)], axis=-1))
    return jnp.concatenate(rows, axis=0), state


def _sg_glue(uv, lng, lnb, w_s, bs_t):
    t_rows = uv.shape[0]
    y = jax.nn.gelu(uv)
    u = y[:, :SG_W]
    v = _ln(y[:, SG_W:], lng, lnb)
    r = lax.broadcasted_iota(jnp.int32, (LANES, LANES), 0)
    c = lax.broadcasted_iota(jnp.int32, (LANES, LANES), 1)
    wm = jnp.where(r >= c, w_s, 0.0)
    lane = lax.broadcasted_iota(jnp.int32, (LANES, SG_W), 0)
    group_of_col = lax.broadcasted_iota(jnp.int32, (LANES, SG_W), 1) // LANES
    e_grp = (group_of_col == lane).astype(F32)
    bias = jnp.dot(bs_t, e_grp, precision=HI, preferred_element_type=F32)
    outs = []
    for n in range(t_rows // LANES):
        vb = v[n * LANES:(n + 1) * LANES]
        vg = jnp.stack([vb[:, g * LANES:(g + 1) * LANES] for g in range(4)])
        mg = _bdot(wm, vg)
        mixed = jnp.concatenate([mg[g] for g in range(4)], axis=-1) + bias
        outs.append(u[n * LANES:(n + 1) * LANES] * mixed)
    return jnp.concatenate(outs, axis=0)


def _merge_glue(ga, gb, ya, yb):
    return jax.nn.sigmoid(ga) * ya + jax.nn.sigmoid(gb) * yb


def _res_ln_glue(x, r, g, b):
    return _ln(ALPHA * x + r, g, b)


def _ffn_glue(upcat, cw):
    y = _conv(upcat, cw)
    return jax.nn.silu(y[:, :FFN_HALF]) * y[:, FFN_HALF:]


def _row(t, c, col=0):
    return pl.BlockSpec((t, c), lambda i: (i, col))


def _row_rev(t, c, nt, col=0):
    return pl.BlockSpec((t, c), lambda i: (nt - 1 - i, col))


def _halo(t, c, nt=None):
    per = t // HALO
    if nt is None:
        return pl.BlockSpec((HALO, c), lambda i: (jnp.maximum(i * per - 1, 0), 0))
    return pl.BlockSpec((HALO, c), lambda i: (jnp.maximum((nt - 1 - i) * per - 1, 0), 0))


def _full(shape):
    nd = len(shape)
    return pl.BlockSpec(shape, lambda i: (0,) * nd)


ANY = pl.BlockSpec(memory_space=pl.ANY)


def _sds(shape, dtype=F32):
    return jax.ShapeDtypeStruct(shape, dtype)


def _tile(s):
    return 256 if s % 256 == 0 else 128


def proj_fwd(x, w, name):
    s = x.shape[0]
    t = _tile(s)
    segs = [(0, 2048), (2048, 3072), (3072, 4096), (4096, 5120), (5120, IN_COLS_PAD)]

    def body(x_ref, w_ref, p_ref):
        xb = x_ref[...].astype(BF16)
        for lo, hi in segs:
            p_ref[:, lo:hi] = jnp.dot(xb, w_ref[:, lo:hi], preferred_element_type=F32)

    return pl.pallas_call(
        body, grid=(s // t,), name=name,
        in_specs=[_row(t, D), _full((D, IN_COLS_PAD))],
        out_specs=_row(t, IN_COLS_PAD),
        out_shape=_sds((s, IN_COLS_PAD)), compiler_params=_params())(x, w)


def dn_fwd(p, cq, a_row, dtb_row, nw_row, name):
    s = p.shape[0]
    t = _tile(s)
    nt = s // t

    def body(qkv_ref, halo_ref, z_ref, ba_ref, cq_ref, a_ref, dtb_ref, nw_ref, o_ref, sst_ref, s_scr):
        i = pl.program_id(0)

        @pl.when(i == 0)
        def _():
            s_scr[...] = jnp.zeros_like(s_scr)

        halo = jnp.where(i == 0, 0.0, halo_ref[...])
        qkvcat = jnp.concatenate([halo, qkv_ref[...]], axis=0)
        cw = tuple(cq_ref[k:k + 1, :] for k in range(4))
        s_in = s_scr[...]
        sst_ref[0] = s_in
        o, s_out = _dn_glue(qkvcat, z_ref[...], ba_ref[...], s_in, cw, a_ref[...], dtb_ref[...], nw_ref[...])
        o_ref[...] = o.astype(BF16)
        s_scr[...] = s_out

    return pl.pallas_call(
        body, grid=(nt,), name=name,
        in_specs=[_row(t, QKV_W), _halo(t, QKV_W), _row(t, Z_W, C_Z // Z_W), _row(t, LANES, C_BA // LANES),
                  _full((4, QKV_W)), _full((1, LANES)), _full((1, LANES)), _full((1, LANES))],
        out_specs=[_row(t, Z_W), pl.BlockSpec((1, HEADS, DK, DK), lambda i: (i, 0, 0, 0))],
        out_shape=[_sds((s, Z_W), BF16), _sds((nt, HEADS, DK, DK))],
        scratch_shapes=[pltpu.VMEM((HEADS, DK, DK), F32)],
        compiler_params=_params())(p, p, p, p, cq, a_row, dtb_row, nw_row)


def sg_fwd(p, lng, lnb, w_s, bs_t, name):
    s = p.shape[0]
    t = _tile(s)

    def body(uv_ref, lng_ref, lnb_ref, ws_ref, bs_ref, o_ref):
        o_ref[...] = _sg_glue(uv_ref[...], lng_ref[...], lnb_ref[...], ws_ref[...], bs_ref[...]).astype(BF16)

    return pl.pallas_call(
        body, grid=(s // t,), name=name,
        in_specs=[_row(t, 2 * SG_W, C_UV // (2 * SG_W)), _full((1, SG_W)), _full((1, SG_W)),
                  _full((4, LANES, LANES)), _full((LANES, LANES))],
        out_specs=_row(t, SG_W), out_shape=_sds((s, SG_W), BF16), compiler_params=_params())(p, lng, lnb, w_s, bs_t)


def merge_fwd(oa, ob, p, x, wa, wb, wo, g1, b1, name):
    s = x.shape[0]
    t = _tile(s)

    def body(oa_ref, ob_ref, ga_ref, gb_ref, x_ref, wa_ref, wb_ref, wo_ref, g_ref, b_ref, x1_ref):
        ya = _mm(oa_ref[...], wa_ref[...])
        yb = _mm(ob_ref[...], wb_ref[...])
        h = _merge_glue(ga_ref[...], gb_ref[...], ya, yb)
        x1_ref[...] = _res_ln_glue(x_ref[...], _mm(h, wo_ref[...]), g_ref[...], b_ref[...])

    return pl.pallas_call(
        body, grid=(s // t,), name=name,
        in_specs=[_row(t, Z_W), _row(t, SG_W), _row(t, D, C_GA // D), _row(t, D, C_GB // D), _row(t, D),
                  _full((Z_W, D)), _full((SG_W, D)), _full((D, D)), _full((1, D)), _full((1, D))],
        out_specs=_row(t, D), out_shape=_sds((s, D)), compiler_params=_params())(oa, ob, p, p, x, wa, wb, wo, g1, b1)


def ffn_fwd(x1, wup2, cf2, wdn2, g2, b2, name):
    s = x1.shape[0]
    t = _tile(s)

    def body(x1_ref, halo_ref, wup_hbm, cf_ref, wdn_hbm, g_ref, b_ref, pre_ref, x2_ref, wup_v, wdn_v):
        i = pl.program_id(0)

        @pl.when(i == 0)
        def _():
            pltpu.sync_copy(wup_hbm, wup_v)
            pltpu.sync_copy(wdn_hbm, wdn_v)

        x1v = x1_ref[...]
        halo = jnp.where(i == 0, 0.0, halo_ref[...])
        x1cat = jnp.concatenate([halo, x1v], axis=0).astype(BF16)
        f = None
        for h in range(2):
            upcat = jnp.dot(x1cat, wup_v[h], preferred_element_type=F32)
            cw = tuple(cf_ref[h, k:k + 1, :] for k in range(3))
            act = _ffn_glue(upcat, cw)
            fh = _mm(act, wdn_v[h])
            f = fh if f is None else f + fh
        pre = ALPHA * x1v + f
        pre_ref[...] = pre
        x2_ref[...] = _ln(pre, g_ref[...], b_ref[...])

    return pl.pallas_call(
        body, grid=(s // t,), name=name,
        in_specs=[_row(t, D), _halo(t, D), ANY, _full((2, 3, FFN)), ANY, _full((1, D)), _full((1, D))],
        out_specs=[_row(t, D), _row(t, D)], out_shape=[_sds((s, D)), _sds((s, D))],
        scratch_shapes=[pltpu.VMEM((2, D, FFN), BF16), pltpu.VMEM((2, FFN_HALF, D), BF16)],
        compiler_params=_params())(x1, x1, wup2, cf2, wdn2, g2, b2)


def loss_call(y, tgt, name):
    s = y.shape[0]
    t = _tile(s)

    def body(y_ref, t_ref, dy_ref, loss_ref):
        @pl.when(pl.program_id(0) == 0)
        def _():
            loss_ref[...] = jnp.zeros_like(loss_ref)

        e = y_ref[...] - t_ref[...]
        dy_ref[...] = e * (1.0 / D)
        part = jnp.sum(jnp.sum(e * e, axis=1, keepdims=True), axis=0, keepdims=True) * (0.5 / D)
        loss_ref[...] += jnp.broadcast_to(part, loss_ref.shape)

    return pl.pallas_call(
        body, grid=(s // t,), name=name, in_specs=[_row(t, D), _row(t, D)],
        out_specs=[_row(t, D), _full((8, LANES))], out_shape=[_sds((s, D)), _sds((8, LANES))],
        compiler_params=_params())(y, tgt)


def _acc(ref, val, first):
    @pl.when(first)
    def _():
        ref[...] = val

    @pl.when(jnp.logical_not(first))
    def _():
        ref[...] += val


def _acc_tn(acc_ref, a, b, first, seg):
    n = b.shape[1]
    for lo in range(0, n, seg):
        hi = min(lo + seg, n)
        _acc(acc_ref.at[:, lo:hi], _mm_tn(a, b[:, lo:hi]), first)


def ln_bwd(pre, dy, g, b, name):
    s = pre.shape[0]
    t = _tile(s)

    def body(pre_ref, dy_ref, g_ref, b_ref, dpre_ref, dg_ref, db_ref):
        _, vjp = jax.vjp(_ln, pre_ref[...], g_ref[...], b_ref[...])
        dpre, dg, db = vjp(dy_ref[...])
        dpre_ref[...] = dpre
        first = pl.program_id(0) == 0
        _acc(dg_ref, dg, first)
        _acc(db_ref, db, first)

    return pl.pallas_call(
        body, grid=(s // t,), name=name, in_specs=[_row(t, D), _row(t, D), _full((1, D)), _full((1, D))],
        out_specs=[_row(t, D), _full((1, D)), _full((1, D))],
        out_shape=[_sds((s, D)), _sds((1, D)), _sds((1, D))], compiler_params=_params())(pre, dy, g, b)


def ffn_bwd(x1, df, acc_in, acc_scale, wup_h, cf_h, wdn_h, name):
    s = x1.shape[0]
    t = _tile(s)
    nt = s // t

    def body(x1_ref, halo_ref, df_ref, acc_ref, wup_hbm, cf_ref, wdn_hbm,
             dx1_ref, dwup_hbm, dcf_ref, dwdn_hbm, wup_ref, wdn_ref, dwup_v, dwdn_v, carry):
        i = pl.program_id(0)
        j = nt - 1 - i
        first = i == 0

        @pl.when(first)
        def _():
            pltpu.sync_copy(wup_hbm, wup_ref)
            pltpu.sync_copy(wdn_hbm, wdn_ref)
            carry[...] = jnp.zeros_like(carry)

        halo = jnp.where(j == 0, 0.0, halo_ref[...])
        x1cat = jnp.concatenate([halo, x1_ref[...]], axis=0).astype(BF16)
        upcat = jnp.dot(x1cat, wup_ref[...], preferred_element_type=F32)
        cw = tuple(cf_ref[k:k + 1, :] for k in range(3))
        act, vjp = jax.vjp(_ffn_glue, upcat, cw)
        dfb = df_ref[...].astype(BF16)
        dact = _mm_nt(dfb, wdn_ref[...])
        _acc_tn(dwdn_v, act.astype(BF16), dfb, first, 512)
        dupcat, dcw = vjp(dact)
        dupb = dupcat.astype(BF16)
        dx1cat = _mm_nt(dupb, wup_ref[...])
        _acc_tn(dwup_v, x1cat, dupb, first, FFN_HALF)
        for k in range(3):
            _acc(dcf_ref.at[k:k + 1, :], dcw[k], first)
        out = acc_scale * acc_ref[...] + dx1cat[HALO:]
        dx1_ref[...] = out
        dx1_ref[t - HALO:t, :] = out[t - HALO:] + carry[...]
        carry[...] = dx1cat[:HALO]

        @pl.when(i == nt - 1)
        def _():
            pltpu.sync_copy(dwup_v, dwup_hbm)
            pltpu.sync_copy(dwdn_v, dwdn_hbm)

    return pl.pallas_call(
        body, grid=(nt,), name=name,
        in_specs=[_row_rev(t, D, nt), _halo(t, D, nt), _row_rev(t, D, nt), _row_rev(t, D, nt),
                  ANY, _full((3, FFN)), ANY],
        out_specs=[_row_rev(t, D, nt), ANY, _full((3, FFN)), ANY],
        out_shape=[_sds((s, D)), _sds((D, FFN)), _sds((3, FFN)), _sds((FFN_HALF, D))],
        scratch_shapes=[pltpu.VMEM((D, FFN), BF16), pltpu.VMEM((FFN_HALF, D), BF16),
                        pltpu.VMEM((D, FFN), F32), pltpu.VMEM((FFN_HALF, D), F32), pltpu.VMEM((HALO, D), F32)],
        compiler_params=_params())(x1, x1, df, acc_in, wup_h, cf_h, wdn_h)


def merge_bwd(oa, ob, p, x, dx1, wa, wb, wo, g1, b1, name):
    s = x.shape[0]
    t = _tile(s)

    def body(oa_ref, ob_ref, ga_ref, gb_ref, x_ref, dx1_ref, wa_ref, wb_ref, wo_ref, g_ref, b_ref,
             doa_ref, dob_ref, dga_ref, dgb_ref, dx_ref, dwa_ref, dwb_ref, dwo_ref, dg_ref, db_ref):
        first = pl.program_id(0) == 0
        oa = oa_ref[...]
        ob = ob_ref[...]
        ya = _mm(oa, wa_ref[...])
        yb = _mm(ob, wb_ref[...])
        h, vjp1 = jax.vjp(_merge_glue, ga_ref[...], gb_ref[...], ya, yb)
        hb = h.astype(BF16)
        r = _mm(hb, wo_ref[...])
        _, vjp2 = jax.vjp(_res_ln_glue, x_ref[...], r, g_ref[...], b_ref[...])
        dx, dr, dg, db = vjp2(dx1_ref[...])
        dx_ref[...] = dx
        _acc(dg_ref, dg, first)
        _acc(db_ref, db, first)
        drb = dr.astype(BF16)
        dh = _mm_nt(drb, wo_ref[...])
        _acc(dwo_ref, _mm_tn(hb, drb), first)
        dga, dgb, dya, dyb = vjp1(dh)
        dga_ref[...] = dga.astype(BF16)
        dgb_ref[...] = dgb.astype(BF16)
        dyab = dya.astype(BF16)
        dybb = dyb.astype(BF16)
        doa_ref[...] = _mm_nt(dyab, wa_ref[...]).astype(BF16)
        dob_ref[...] = _mm_nt(dybb, wb_ref[...]).astype(BF16)
        _acc(dwa_ref, _mm_tn(oa, dyab), first)
        _acc(dwb_ref, _mm_tn(ob, dybb), first)

    return pl.pallas_call(
        body, grid=(s // t,), name=name,
        in_specs=[_row(t, Z_W), _row(t, SG_W), _row(t, D, C_GA // D), _row(t, D, C_GB // D), _row(t, D), _row(t, D),
                  _full((Z_W, D)), _full((SG_W, D)), _full((D, D)), _full((1, D)), _full((1, D))],
        out_specs=[_row(t, Z_W), _row(t, SG_W), _row(t, D), _row(t, D), _row(t, D),
                   _full((Z_W, D)), _full((SG_W, D)), _full((D, D)), _full((1, D)), _full((1, D))],
        out_shape=[_sds((s, Z_W), BF16), _sds((s, SG_W), BF16), _sds((s, D), BF16), _sds((s, D), BF16), _sds((s, D)),
                   _sds((Z_W, D)), _sds((SG_W, D)), _sds((D, D)), _sds((1, D)), _sds((1, D))],
        compiler_params=_params())(oa, ob, p, p, x, dx1, wa, wb, wo, g1, b1)


def sg_bwd(p, dob, lng, lnb, w_s, bs_t, name):
    s = p.shape[0]
    t = _tile(s)

    def body(uv_ref, dob_ref, lng_ref, lnb_ref, ws_ref, bs_ref, duv_ref, dlng_ref, dlnb_ref, dws_ref, dbs_ref):
        first = pl.program_id(0) == 0
        _, vjp = jax.vjp(_sg_glue, uv_ref[...], lng_ref[...], lnb_ref[...], ws_ref[...], bs_ref[...])
        duv, dlng, dlnb, dws, dbs = vjp(dob_ref[...].astype(F32))
        duv_ref[...] = duv.astype(BF16)
        _acc(dlng_ref, dlng, first)
        _acc(dlnb_ref, dlnb, first)
        _acc(dws_ref, dws, first)
        _acc(dbs_ref, dbs, first)

    return pl.pallas_call(
        body, grid=(s // t,), name=name,
        in_specs=[_row(t, 2 * SG_W, C_UV // (2 * SG_W)), _row(t, SG_W), _full((1, SG_W)), _full((1, SG_W)),
                  _full((4, LANES, LANES)), _full((LANES, LANES))],
        out_specs=[_row(t, 2 * SG_W), _full((1, SG_W)), _full((1, SG_W)), _full((4, LANES, LANES)), _full((LANES, LANES))],
        out_shape=[_sds((s, 2 * SG_W), BF16), _sds((1, SG_W)), _sds((1, SG_W)), _sds((4, LANES, LANES)), _sds((LANES, LANES))],
        compiler_params=_params())(p, dob, lng, lnb, w_s, bs_t)


def dn_bwd(p, sst, doa, cq, a_row, dtb_row, nw_row, name):
    s = p.shape[0]
    t = _tile(s)
    nt = s // t

    def body(qkv_ref, halo_ref, z_ref, ba_ref, sst_ref, doa_ref, cq_ref, a_ref, dtb_ref, nw_ref,
             dqkv_ref, dz_ref, dba_ref, dcq_ref, da_ref, ddtb_ref, dnw_ref, ds_scr, carry):
        i = pl.program_id(0)
        j = nt - 1 - i
        first = i == 0

        @pl.when(first)
        def _():
            ds_scr[...] = jnp.zeros_like(ds_scr)
            carry[...] = jnp.zeros_like(carry)

        halo = jnp.where(j == 0, 0.0, halo_ref[...])
        qkvcat = jnp.concatenate([halo, qkv_ref[...]], axis=0)
        cw = tuple(cq_ref[k:k + 1, :] for k in range(4))
        _, vjp = jax.vjp(_dn_glue, qkvcat, z_ref[...], ba_ref[...], sst_ref[0], cw, a_ref[...], dtb_ref[...], nw_ref[...])
        dqkvcat, dz, dba, ds_in, dcw, da, ddtb, dnw = vjp((doa_ref[...].astype(F32), ds_scr[...]))
        ds_scr[...] = ds_in
        dz_ref[...] = dz.astype(BF16)
        dba_ref[...] = dba.astype(BF16)
        dtile = dqkvcat[HALO:]
        dqkv_ref[...] = dtile.astype(BF16)
        dqkv_ref[t - HALO:t, :] = (dtile[t - HALO:] + carry[...]).astype(BF16)
        carry[...] = dqkvcat[:HALO]
        for k in range(4):
            _acc(dcq_ref.at[k:k + 1, :], dcw[k], first)
        _acc(da_ref, da, first)
        _acc(ddtb_ref, ddtb, first)
        _acc(dnw_ref, dnw, first)

    return pl.pallas_call(
        body, grid=(nt,), name=name,
        in_specs=[_row_rev(t, QKV_W, nt), _halo(t, QKV_W, nt), _row_rev(t, Z_W, nt, C_Z // Z_W),
                  _row_rev(t, LANES, nt, C_BA // LANES),
                  pl.BlockSpec((1, HEADS, DK, DK), lambda i: (nt - 1 - i, 0, 0, 0)), _row_rev(t, Z_W, nt),
                  _full((4, QKV_W)), _full((1, LANES)), _full((1, LANES)), _full((1, LANES))],
        out_specs=[_row_rev(t, QKV_W, nt), _row_rev(t, Z_W, nt), _row_rev(t, LANES, nt),
                   _full((4, QKV_W)), _full((1, LANES)), _full((1, LANES)), _full((1, LANES))],
        out_shape=[_sds((s, QKV_W), BF16), _sds((s, Z_W), BF16), _sds((s, LANES), BF16),
                   _sds((4, QKV_W)), _sds((1, LANES)), _sds((1, LANES)), _sds((1, LANES))],
        scratch_shapes=[pltpu.VMEM((HEADS, DK, DK), F32), pltpu.VMEM((HALO, QKV_W), F32)],
        compiler_params=_params())(p, p, p, p, sst, doa, cq, a_row, dtb_row, nw_row)


def proj_bwd(x, dqkv, dz, duv, dga, dgb, dba, dxd, w, name):
    s = x.shape[0]
    t = _tile(s)
    nt = s // t

    def body(x_ref, dqkv_ref, dz_ref, duv_ref, dga_ref, dgb_ref, dba_ref, dxd_ref, w_hbm,
             dx_ref, dw_hbm, w_v, dw_v):
        i = pl.program_id(0)
        first = i == 0

        @pl.when(first)
        def _():
            pltpu.sync_copy(w_hbm, w_v)

        dp = jnp.concatenate([dqkv_ref[...], dz_ref[...], duv_ref[...], dga_ref[...], dgb_ref[...], dba_ref[...]], axis=1)
        dx_ref[...] = dxd_ref[...] + _mm_nt(dp, w_v[...])
        _acc_tn(dw_v, x_ref[...].astype(BF16), dp, first, 1024)

        @pl.when(i == nt - 1)
        def _():
            pltpu.sync_copy(dw_v, dw_hbm)

    return pl.pallas_call(
        body, grid=(nt,), name=name,
        in_specs=[_row(t, D), _row(t, QKV_W), _row(t, Z_W), _row(t, 2 * SG_W), _row(t, D), _row(t, D), _row(t, LANES),
                  _row(t, D), ANY],
        out_specs=[_row(t, D), ANY], out_shape=[_sds((s, D)), _sds((D, IN_COLS_PAD))],
        scratch_shapes=[pltpu.VMEM((D, IN_COLS_PAD), BF16), pltpu.VMEM((D, IN_COLS_PAD), F32)],
        compiler_params=_params())(x, dqkv, dz, duv, dga, dgb, dba, dxd, w)


def _rows_block(rows, cols):
    cap = max(8, (2 * 1024 * 1024) // (cols * 4))
    for cand in range(min(rows, cap) // 8 * 8, 7, -8):
        if rows % cand == 0:
            return cand
    return rows


def adam_call(w, g, m, v, name):
    rows, cols = w.shape
    tr = _rows_block(rows, cols)
    c1 = 1.0 - ADAM_B1 ** ADAM_STEP
    c2 = 1.0 - ADAM_B2 ** ADAM_STEP

    def body(w_ref, g_ref, m_ref, v_ref, d_ref, nm_ref, nv_ref):
        gv = g_ref[...]
        nm = ADAM_B1 * m_ref[...] + (1.0 - ADAM_B1) * gv
        nv = ADAM_B2 * v_ref[...] + (1.0 - ADAM_B2) * (gv * gv)
        d_ref[...] = -ADAM_LR * ((nm / c1) / (jnp.sqrt(nv / c2) + ADAM_EPS) + ADAM_WD * w_ref[...])
        nm_ref[...] = nm
        nv_ref[...] = nv

    spec = pl.BlockSpec((tr, cols), lambda i: (i, 0))
    return pl.pallas_call(
        body, grid=(rows // tr,), name=name, in_specs=[spec] * 4, out_specs=[spec] * 3,
        out_shape=[_sds((rows, cols))] * 3, compiler_params=_params())(w, g, m, v)


def add_call(parts, name):
    rows, cols = parts[0].shape
    tr = _rows_block(rows, cols)

    def body(*refs):
        acc = refs[0][...]
        for r in refs[1:-1]:
            acc = acc + r[...]
        refs[-1][...] = acc

    spec = pl.BlockSpec((tr, cols), lambda i: (i, 0))
    return pl.pallas_call(
        body, grid=(rows // tr,), name=name, in_specs=[spec] * len(parts), out_specs=spec,
        out_shape=_sds((rows, cols)), compiler_params=_params())(*parts)


def _place():
    return lax.axis_index("x"), lax.axis_index("y"), lax.axis_index("c")


def _other_chips(x, y):
    return [(1 - x, y), (x, 1 - y), (1 - x, 1 - y)]


def gather_chips(xs, name):
    _, rh, lanes = xs.shape

    def body(x_ref, out_ref, send_sems, recv_sems, local_sem):
        x, y, c = _place()
        me = 2 * x + y
        sibling = (x, y, 1 - c)
        chips = _other_chips(x, y)

        def copy(k, slot, half, to, src=None):
            dst = out_ref.at[slot, half]
            return pltpu.make_async_remote_copy(
                src_ref=dst if src is None else src, dst_ref=dst, send_sem=send_sems.at[k], recv_sem=recv_sems.at[k],
                device_id=to, device_id_type=MESH)

        mine = pltpu.make_async_copy(x_ref, out_ref.at[me], local_sem)
        mine.start()
        first = [copy(k, me, c, (cx, cy, c), src=x_ref.at[c]) for k, (cx, cy) in enumerate(chips)]
        for cp in first:
            cp.start()
        passed = [copy(3 + k, 2 * cx + cy, c, sibling) for k, (cx, cy) in enumerate(chips)]
        for k, (cx, cy) in enumerate(chips):
            copy(k, 2 * cx + cy, c, (x, y, c)).wait_recv()
            passed[k].start()
        for k, (cx, cy) in enumerate(chips):
            copy(3 + k, 2 * cx + cy, 1 - c, (x, y, c)).wait_recv()
        for cp in first + passed:
            cp.wait_send()
        mine.wait()

    return pl.pallas_call(
        body, name=name, in_specs=[ANY], out_specs=ANY, out_shape=_sds((N_CHIPS, 2, rh, lanes), xs.dtype),
        scratch_shapes=[pltpu.SemaphoreType.DMA((6,)), pltpu.SemaphoreType.DMA((6,)), pltpu.SemaphoreType.DMA],
    )(xs)


def pair_split(g2, name):
    _, n, rh, lanes = g2.shape

    def body(g_ref, own_ref, got_ref, send_sem, recv_sem, local_sem):
        x, y, c = _place()
        mine = pltpu.make_async_copy(g_ref.at[c], own_ref, local_sem)
        mine.start()
        cp = pltpu.make_async_remote_copy(
            src_ref=g_ref.at[1 - c], dst_ref=got_ref, send_sem=send_sem, recv_sem=recv_sem,
            device_id=(x, y, 1 - c), device_id_type=MESH)
        cp.start()
        cp.wait()
        mine.wait()

    return pl.pallas_call(
        body, name=name, in_specs=[ANY], out_specs=[ANY, ANY],
        out_shape=[_sds((n, rh, lanes)), _sds((n, rh, lanes))],
        scratch_shapes=[pltpu.SemaphoreType.DMA, pltpu.SemaphoreType.DMA, pltpu.SemaphoreType.DMA],
    )(g2)


def scatter_chips(ps, name):
    n, rh, lanes = ps.shape

    def body(p_ref, out_ref, send_sems, recv_sems, local_sem):
        x, y, c = _place()
        me = 2 * x + y
        chips = _other_chips(x, y)
        mine = pltpu.make_async_copy(p_ref.at[me], out_ref.at[me], local_sem)
        mine.start()
        sends = [pltpu.make_async_remote_copy(
            src_ref=p_ref.at[2 * cx + cy], dst_ref=out_ref.at[me], send_sem=send_sems.at[k], recv_sem=recv_sems.at[k],
            device_id=(cx, cy, c), device_id_type=MESH) for k, (cx, cy) in enumerate(chips)]
        for cp in sends:
            cp.start()
        for k, (cx, cy) in enumerate(chips):
            pltpu.make_async_remote_copy(
                src_ref=p_ref.at[me], dst_ref=out_ref.at[2 * cx + cy], send_sem=send_sems.at[k],
                recv_sem=recv_sems.at[k], device_id=(x, y, c), device_id_type=MESH).wait_recv()
        for cp in sends:
            cp.wait_send()
        mine.wait()

    return pl.pallas_call(
        body, name=name, in_specs=[ANY], out_specs=ANY, out_shape=_sds((n, rh, lanes)),
        scratch_shapes=[pltpu.SemaphoreType.DMA((3,)), pltpu.SemaphoreType.DMA((3,)), pltpu.SemaphoreType.DMA],
    )(ps)


def pair_join(half, name):
    rh, lanes = half.shape

    def body(h_ref, out_ref, send_sem, recv_sem, local_sem):
        x, y, c = _place()
        mine = pltpu.make_async_copy(h_ref, out_ref.at[c], local_sem)
        mine.start()
        cp = pltpu.make_async_remote_copy(
            src_ref=h_ref, dst_ref=out_ref.at[c], send_sem=send_sem, recv_sem=recv_sem,
            device_id=(x, y, 1 - c), device_id_type=MESH)
        cp.start()
        cp.wait_send()
        pltpu.make_async_remote_copy(
            src_ref=h_ref, dst_ref=out_ref.at[1 - c], send_sem=send_sem, recv_sem=recv_sem,
            device_id=(x, y, c), device_id_type=MESH).wait_recv()
        mine.wait()

    return pl.pallas_call(
        body, name=name, in_specs=[ANY], out_specs=ANY, out_shape=_sds((2, rh, lanes)),
        scratch_shapes=[pltpu.SemaphoreType.DMA, pltpu.SemaphoreType.DMA, pltpu.SemaphoreType.DMA],
    )(half)


def allsum_small(v, name):
    rows, lanes = v.shape
    n_dev = 8

    def body(v_ref, out_ref, buf, send_sems, recv_sems):
        x, y, c = _place()
        me, sibling = (x, y, c), (x, y, 1 - c)
        chips = _other_chips(x, y)

        def slot(px, py, pc):
            return buf.at[4 * px + 2 * py + pc]

        def copy(k, block, to, src=None):
            return pltpu.make_async_remote_copy(
                src_ref=slot(*block) if src is None else src, dst_ref=slot(*block),
                send_sem=send_sems.at[k], recv_sem=recv_sems.at[k], device_id=to, device_id_type=MESH)

        slot(*me)[...] = v_ref[...]
        first = [copy(0, me, sibling, src=v_ref)]
        first += [copy(1 + k, me, (*chip, c), src=v_ref) for k, chip in enumerate(chips)]
        for cp in first:
            cp.start()
        passed = [copy(4 + k, (*chip, c), sibling) for k, chip in enumerate(chips)]
        for k, chip in enumerate(chips):
            copy(1 + k, (*chip, c), me).wait_recv()
            passed[k].start()
        copy(0, sibling, me).wait_recv()
        for k, chip in enumerate(chips):
            copy(4 + k, (*chip, 1 - c), me).wait_recv()
        for cp in first + passed:
            cp.wait_send()
        acc = buf[0]
        for d in range(1, n_dev):
            acc = acc + buf[d]
        out_ref[...] = acc

    vm = pl.BlockSpec(memory_space=pltpu.VMEM)
    return pl.pallas_call(
        body, name=name, in_specs=[vm], out_specs=vm, out_shape=_sds((rows, lanes)),
        scratch_shapes=[pltpu.VMEM((n_dev, rows, lanes), F32), pltpu.SemaphoreType.DMA((7,)), pltpu.SemaphoreType.DMA((7,))],
        compiler_params=pltpu.CompilerParams(vmem_limit_bytes=VMEM_LIMIT),
    )(v)


BIG = (("w_in", 2), ("w_branch_a", 2), ("w_branch_b", 2), ("w_out", 1), ("w_up", 2), ("w_down", 1))
CONV = (("conv_qkv", 2), ("conv_ffn", 2))
REPL = ("a_log", "dt_bias", "dn_norm_w", "sg_ln_g", "sg_ln_b", "w_spatial", "b_spatial", "ln1_g", "ln1_b", "ln2_g", "ln2_b")


def _pad_rows(flat, mult):
    n = flat.shape[0]
    unit = mult * LANES
    total = -(-n // unit) * unit
    return jnp.pad(flat, (0, total - n)).reshape(total // LANES, LANES)


def _pack(arrs, mult):
    return _pad_rows(jnp.concatenate([a.reshape(-1) for a in arrs]), mult)


def _unpack(flat, shapes):
    out, off = [], 0
    for shp in shapes:
        n = math.prod(shp)
        out.append(flat[off:off + n].reshape(shp))
        off += n
    return out


def _gather_full(shards, axes, dtype, mult, name):
    packed = _pack([a.astype(dtype) for a in shards], 2 * mult)
    rows = packed.shape[0]
    got = gather_chips(packed.reshape(2, rows // 2, LANES), name).reshape(N_CHIPS, rows * LANES)
    per_chip = [_unpack(got[k], [a.shape for a in shards]) for k in range(N_CHIPS)]
    return [jnp.concatenate([per_chip[k][i] for k in range(N_CHIPS)], axis=ax) for i, ax in enumerate(axes)]


def kernel(x, w_in, conv_qkv, a_log, dt_bias, dn_norm_w, w_branch_a, sg_ln_g, sg_ln_b, w_spatial, b_spatial, w_branch_b, w_out, ln1_g, ln1_b, w_up, conv_ffn, w_down, ln2_g, ln2_b, loss_target, m_w_in, m_conv_qkv, m_a_log, m_dt_bias, m_dn_norm_w, m_w_branch_a, m_sg_ln_g, m_sg_ln_b, m_w_spatial, m_b_spatial, m_w_branch_b, m_w_out, m_ln1_g, m_ln1_b, m_w_up, m_conv_ffn, m_w_down, m_ln2_g, m_ln2_b, v_w_in, v_conv_qkv, v_a_log, v_dt_bias, v_dn_norm_w, v_w_branch_a, v_sg_ln_g, v_sg_ln_b, v_w_spatial, v_b_spatial, v_w_branch_b, v_w_out, v_ln1_g, v_ln1_b, v_w_up, v_conv_ffn, v_w_down, v_ln2_g, v_ln2_b):
    names = ("w_in", "conv_qkv", "a_log", "dt_bias", "dn_norm_w", "w_branch_a", "sg_ln_g", "sg_ln_b", "w_spatial",
             "b_spatial", "w_branch_b", "w_out", "ln1_g", "ln1_b", "w_up", "conv_ffn", "w_down", "ln2_g", "ln2_b")
    w = dict(zip(names, (w_in, conv_qkv, a_log, dt_bias, dn_norm_w, w_branch_a, sg_ln_g, sg_ln_b, w_spatial,
                         b_spatial, w_branch_b, w_out, ln1_g, ln1_b, w_up, conv_ffn, w_down, ln2_g, ln2_b)))
    m = dict(zip(names, (m_w_in, m_conv_qkv, m_a_log, m_dt_bias, m_dn_norm_w, m_w_branch_a, m_sg_ln_g, m_sg_ln_b,
                         m_w_spatial, m_b_spatial, m_w_branch_b, m_w_out, m_ln1_g, m_ln1_b, m_w_up, m_conv_ffn,
                         m_w_down, m_ln2_g, m_ln2_b)))
    v = dict(zip(names, (v_w_in, v_conv_qkv, v_a_log, v_dt_bias, v_dn_norm_w, v_w_branch_a, v_sg_ln_g, v_sg_ln_b,
                         v_w_spatial, v_b_spatial, v_w_branch_b, v_w_out, v_ln1_g, v_ln1_b, v_w_up, v_conv_ffn,
                         v_w_down, v_ln2_g, v_ln2_b)))
    chip = 2 * lax.axis_index("x") + lax.axis_index("y")
    s = x.shape[1]
    xs = x.reshape(s, D)
    tgt = loss_target.reshape(s, D)

    big_names = [n for n, _ in BIG]
    big_axes = [a for _, a in BIG]
    full = dict(zip(big_names, _gather_full([w[n] for n in big_names], big_axes, BF16, 16, "gather_big")))
    conv_names = [n for n, _ in CONV]
    full.update(zip(conv_names, _gather_full([w[n] for n in conv_names], [a for _, a in CONV], F32, 8, "gather_conv")))

    def lane_row(vec, off):
        return jnp.zeros((1, LANES), F32).at[0, off:off + vec.shape[0]].set(vec)

    layers = []
    for l in range(DEPTH):
        wi = full["w_in"][l]
        wup, cf, wdn = full["w_up"][l], full["conv_ffn"][l], full["w_down"][l]
        halves = lambda a: [jnp.concatenate([a[:, h * FFN_HALF:(h + 1) * FFN_HALF],
                                             a[:, FFN + h * FFN_HALF:FFN + (h + 1) * FFN_HALF]], axis=1) for h in range(2)]
        layers.append(dict(
            w_in=jnp.concatenate([wi[:, :2048], wi[:, 2056:3080], wi[:, 3080:5128], wi[:, 2048:2056],
                                  jnp.zeros((D, IN_COLS_PAD - 5128), BF16)], axis=1),
            cq=full["conv_qkv"][l],
            a_row=lane_row(w["a_log"][l], HEADS), dtb_row=lane_row(w["dt_bias"][l], HEADS),
            nw_row=w["dn_norm_w"][l].reshape(1, DK),
            lng=w["sg_ln_g"][l].reshape(1, SG_W), lnb=w["sg_ln_b"][l].reshape(1, SG_W),
            w_s=w["w_spatial"][l], bs_t=jnp.zeros((LANES, LANES), F32).at[:, :4].set(w["b_spatial"][l].T),
            wa=full["w_branch_a"][l], wb=full["w_branch_b"][l], wo=full["w_out"][l],
            g1=w["ln1_g"][l].reshape(1, D), b1=w["ln1_b"][l].reshape(1, D),
            wup=halves(wup), cf=halves(cf), wdn=[wdn[:FFN_HALF], wdn[FFN_HALF:]],
            g2=w["ln2_g"][l].reshape(1, D), b2=w["ln2_b"][l].reshape(1, D)))

    saved = []
    h_in = xs
    for l, p in enumerate(layers):
        proj = proj_fwd(h_in, p["w_in"], f"proj_fwd{l}")
        oa, sst = dn_fwd(proj, p["cq"], p["a_row"], p["dtb_row"], p["nw_row"], f"dn_fwd{l}")
        ob = sg_fwd(proj, p["lng"], p["lnb"], p["w_s"], p["bs_t"], f"sg_fwd{l}")
        x1 = merge_fwd(oa, ob, proj, h_in, p["wa"], p["wb"], p["wo"], p["g1"], p["b1"], f"merge_fwd{l}")
        pre2, x2 = ffn_fwd(x1, jnp.stack(p["wup"]), jnp.stack(p["cf"]), jnp.stack(p["wdn"]), p["g2"], p["b2"], f"ffn_fwd{l}")
        saved.append(dict(x=h_in, proj=proj, oa=oa, ob=ob, sst=sst, x1=x1, pre2=pre2))
        h_in = x2

    dy, loss_part = loss_call(h_in, tgt, "loss")
    loss = lax.psum(loss_part[0, 0], ("x", "y", "c"))

    grads = {n: [None] * DEPTH for n in names}
    for l in reversed(range(DEPTH)):
        p, a = layers[l], saved[l]
        dpre2, dg2, db2 = ln_bwd(a["pre2"], dy, p["g2"], p["b2"], f"ln2_bwd{l}")
        dx1, dwup0, dcf0, dwdn0 = ffn_bwd(a["x1"], dpre2, dpre2, ALPHA, p["wup"][0], p["cf"][0], p["wdn"][0], f"ffn_bwd{l}a")
        dx1, dwup1, dcf1, dwdn1 = ffn_bwd(a["x1"], dpre2, dx1, 1.0, p["wup"][1], p["cf"][1], p["wdn"][1], f"ffn_bwd{l}b")
        doa, dob, dga, dgb, dxd, dwa, dwb, dwo, dg1, db1 = merge_bwd(
            a["oa"], a["ob"], a["proj"], a["x"], dx1, p["wa"], p["wb"], p["wo"], p["g1"], p["b1"], f"merge_bwd{l}")
        duv, dlng, dlnb, dws, dbs = sg_bwd(a["proj"], dob, p["lng"], p["lnb"], p["w_s"], p["bs_t"], f"sg_bwd{l}")
        dqkv, dz, dba, dcq, da, ddtb, dnw = dn_bwd(a["proj"], a["sst"], doa, p["cq"], p["a_row"], p["dtb_row"],
                                                   p["nw_row"], f"dn_bwd{l}")
        dy, dwi = proj_bwd(a["x"], dqkv, dz, duv, dga, dgb, dba, dxd, p["w_in"], f"proj_bwd{l}")

        unhalve = lambda h0, h1: jnp.concatenate([h0[:, :FFN_HALF], h1[:, :FFN_HALF], h0[:, FFN_HALF:], h1[:, FFN_HALF:]], axis=1)
        grads["w_in"][l] = jnp.concatenate([dwi[:, :2048], dwi[:, C_BA:C_BA + 8], dwi[:, 2048:3072], dwi[:, 3072:5120]], axis=1)
        grads["conv_qkv"][l] = dcq
        grads["a_log"][l] = da[0, HEADS:2 * HEADS]
        grads["dt_bias"][l] = ddtb[0, HEADS:2 * HEADS]
        grads["dn_norm_w"][l] = dnw[0]
        grads["w_branch_a"][l] = dwa
        grads["sg_ln_g"][l] = dlng[0]
        grads["sg_ln_b"][l] = dlnb[0]
        grads["w_spatial"][l] = dws
        grads["b_spatial"][l] = dbs[:, :4].T
        grads["w_branch_b"][l] = dwb
        grads["w_out"][l] = dwo
        grads["ln1_g"][l] = dg1[0]
        grads["ln1_b"][l] = db1[0]
        grads["w_up"][l] = unhalve(dwup0, dwup1)
        grads["conv_ffn"][l] = unhalve(dcf0, dcf1)
        grads["w_down"][l] = jnp.concatenate([dwdn0, dwdn1], axis=0)
        grads["ln2_g"][l] = dg2[0]
        grads["ln2_b"][l] = db2[0]
    grad_x = dy.reshape(x.shape)
    g_full = {n: jnp.stack(grads[n]) for n in names}

    slots = []
    for k in range(N_CHIPS):
        pieces = [jnp.split(g_full[n], N_CHIPS, axis=ax)[k] for n, ax in BIG]
        slots.append(_pack(pieces, 32))
    rows = slots[0].shape[0]
    rh = rows // 2
    g2 = jnp.stack([sl.reshape(2, rh, LANES) for sl in slots], axis=1)
    own, theirs = pair_split(g2, "reduce_pair")
    pair = add_call([own.reshape(N_CHIPS * rh, LANES), theirs.reshape(N_CHIPS * rh, LANES)], "reduce_pair_add")
    from_chips = scatter_chips(pair.reshape(N_CHIPS, rh, LANES), "reduce_chips")
    half_sum = add_call([from_chips[k] for k in range(N_CHIPS)], "reduce_chips_add")
    red = pair_join(half_sum, "reduce_join").reshape(rows * LANES)
    g_shard = dict(zip(big_names, _unpack(red, [w[n].shape for n in big_names])))

    small_names = conv_names + list(REPL)
    small = allsum_small(_pack([g_full[n] for n in small_names], 8), "reduce_small").reshape(-1)
    small_full = dict(zip(small_names, _unpack(small, [g_full[n].shape for n in small_names])))
    for n in conv_names:
        width = w[n].shape[2]
        g_shard[n] = lax.dynamic_slice_in_dim(small_full[n], chip * width, width, axis=2)
    for n in REPL:
        g_shard[n] = small_full[n]

    delta, new_m, new_v = {}, {}, {}
    for n in big_names:
        shp = w[n].shape
        two_d = (shp[0] * shp[1], shp[2])
        d_, m_, v_ = adam_call(w[n].reshape(two_d), g_shard[n].reshape(two_d), m[n].reshape(two_d), v[n].reshape(two_d), f"adam_{n}")
        delta[n], new_m[n], new_v[n] = d_.reshape(shp), m_.reshape(shp), v_.reshape(shp)
    shapes = [w[n].shape for n in small_names]
    packs = [_pack([src[n] for n in small_names], 8) for src in (w, g_shard, m, v)]
    outs = adam_call(*packs, "adam_small")
    for dst, o in zip((delta, new_m, new_v), outs):
        dst.update(zip(small_names, _unpack(o.reshape(-1), shapes)))

    return (loss, grad_x, *[g_shard[n] for n in names], *[delta[n] for n in names],
            *[new_m[n] for n in names], *[new_v[n] for n in names])
```

```python
import functools
import math

import jax
import jax.numpy as jnp
from jax import lax
from jax.experimental import pallas as pl
from jax.experimental.pallas import tpu as pltpu

F32 = jnp.float32
BF16 = jnp.bfloat16
HI = lax.Precision.HIGHEST
MESH = pl.DeviceIdType.MESH

D = 1024
DEPTH = 2
HEADS = 4
DK = 128
CHUNK = 64
QKV_W = 1536
Z_W = 512
SG_W = 512
FFN = 2816
FFN_HALF = FFN // 2
N_CHIPS = 4
LN_EPS = 1e-5
RMS_EPS = 1e-6
L2_EPS = 1e-6
ALPHA = (2 * DEPTH) ** 0.25
ADAM_LR, ADAM_B1, ADAM_B2, ADAM_EPS, ADAM_WD, ADAM_STEP = 0.001, 0.9, 0.999, 1e-08, 0.01, 10

HALO = 16
LANES = 128
IN_COLS_PAD = 5248
C_Z, C_UV, C_GA, C_GB, C_BA = 1536, 2048, 3072, 4096, 5120
VMEM_LIMIT = 56 * 1024 * 1024
LOCAL_PIECES = 3


def _params(n_grid=1):
    return pltpu.CompilerParams(dimension_semantics=("arbitrary",) * n_grid, vmem_limit_bytes=VMEM_LIMIT)


def _mm(a, b):
    return jnp.dot(a.astype(BF16), b.astype(BF16), preferred_element_type=F32)


def _mm_nt(a, b):
    return lax.dot_general(a.astype(BF16), b.astype(BF16), (((1,), (1,)), ((), ())), preferred_element_type=F32)


def _mm_tn(a, b):
    return lax.dot_general(a.astype(BF16), b.astype(BF16), (((0,), (0,)), ((), ())), preferred_element_type=F32)


def _bdot(a, b, prec=None):
    return lax.dot_general(a, b, (((2,), (1,)), ((0,), (0,))), precision=prec, preferred_element_type=F32)


def _bdot_nt(a, b, prec=None):
    return lax.dot_general(a, b, (((2,), (2,)), ((0,), (0,))), precision=prec, preferred_element_type=F32)


def _ln(x, g, b):
    mu = jnp.mean(x, axis=-1, keepdims=True)
    xc = x - mu
    var = jnp.mean(xc * xc, axis=-1, keepdims=True)
    return xc * lax.rsqrt(var + LN_EPS) * g + b


def _shift_rows(x, s):
    s = s % x.shape[0]
    return x if s == 0 else pltpu.roll(x, s, 0)


@jax.custom_vjp
def _conv(xcat, w):
    k_taps = len(w)
    y = None
    for k in range(k_taps):
        t = _shift_rows(xcat, k_taps - 1 - k)[HALO:] * w[k]
        y = t if y is None else y + t
    return y


def _conv_fwd(xcat, w):
    return _conv(xcat, w), (xcat, w)


def _conv_bwd(res, dy):
    xcat, w = res
    k_taps = len(w)
    dyp = jnp.concatenate([jnp.zeros((HALO, dy.shape[1]), dy.dtype), dy], axis=0)
    dx = None
    dws = []
    for k in range(k_taps):
        s = k_taps - 1 - k
        t = _shift_rows(dyp, -s) * w[k]
        dx = t if dx is None else dx + t
        dws.append(jnp.sum(_shift_rows(xcat, s)[HALO:] * dy, axis=0, keepdims=True))
    return dx, tuple(dws)


_conv.defvjp(_conv_fwd, _conv_bwd)


@jax.custom_vjp
def _tri_inv(l):
    n = l.shape[-1]
    r = lax.broadcasted_iota(jnp.int32, (n, n), 0)
    c = lax.broadcasted_iota(jnp.int32, (n, n), 1)
    eye = (r == c).astype(F32)
    p = eye - l
    lp = l
    steps = int(math.log2(n)) - 1
    for i in range(steps):
        lp = _bdot(lp, lp, HI)
        p = p + _bdot(p, lp, HI)
    return p


def _tri_inv_fwd(l):
    t = _tri_inv(l)
    return t, t


def _tri_inv_bwd(t, dt):
    tt = jnp.swapaxes(t, 1, 2)
    return (-_bdot(tt, _bdot(dt, tt, HI), HI),)


_tri_inv.defvjp(_tri_inv_fwd, _tri_inv_bwd)


def _dn_glue(qkvcat, z, ba, s_in, cw, a_row, dtb_row, nw_row):
    t_rows = z.shape[0]
    nc = t_rows // CHUNK
    nb = nc * HEADS

    qkv = jax.nn.silu(_conv(qkvcat, cw))

    def chunks(t, off):
        return jnp.stack([t[n * CHUNK:(n + 1) * CHUNK, off + h * DK: off + (h + 1) * DK]
                          for n in range(nc) for h in range(HEADS)])

    q = chunks(qkv, 0)
    k = chunks(qkv, 512)
    v = chunks(qkv, 1024)
    q = q * lax.rsqrt(jnp.sum(q * q, axis=-1, keepdims=True) + L2_EPS) * (DK ** -0.5)
    k = k * lax.rsqrt(jnp.sum(k * k, axis=-1, keepdims=True) + L2_EPS)

    lane = lax.broadcasted_iota(jnp.int32, (LANES, HEADS * DK), 0)
    head_of_col = lax.broadcasted_iota(jnp.int32, (LANES, HEADS * DK), 1) // DK
    e_beta = (head_of_col == lane).astype(F32)
    e_g = (head_of_col + HEADS == lane).astype(F32)
    beta_l = jax.nn.sigmoid(ba)
    g_l = -jnp.exp(a_row) * jax.nn.softplus(ba + dtb_row)
    beta = chunks(jnp.dot(beta_l, e_beta, precision=HI, preferred_element_type=F32), 0)
    g = chunks(jnp.dot(g_l, e_g, precision=HI, preferred_element_type=F32), 0)

    r = lax.broadcasted_iota(jnp.int32, (CHUNK, CHUNK), 0)
    c = lax.broadcasted_iota(jnp.int32, (CHUNK, CHUNK), 1)
    causal = r >= c
    strict = r > c
    tril_b = jnp.broadcast_to(causal.astype(F32), (nb, CHUNK, CHUNK))
    ones_b = jnp.ones((nb, CHUNK, CHUNK), F32)
    gi_b = _bdot(tril_b, g, HI)
    sel = (lax.broadcasted_iota(jnp.int32, (nb, CHUNK, DK), 2) == 0).astype(F32)
    gi = _bdot_nt(gi_b, sel, HI)
    gj = _bdot_nt(sel, gi_b, HI)
    decay = jnp.where(causal, jnp.exp(jnp.where(causal, gi - gj, 0.0)), 0.0)
    kb = k * beta
    l_mat = jnp.where(strict, _bdot_nt(kb, k) * decay, 0.0)
    t_mat = _tri_inv(l_mat)
    e_gi = jnp.exp(gi_b)
    w_mat = _bdot(t_mat, kb * e_gi)
    u_mat = _bdot(t_mat, v * beta)
    a_qk = _bdot_nt(q, k) * decay
    q_g = q * e_gi
    gl_b = _bdot(ones_b, g, HI)
    k_d = k * jnp.exp(gl_b - gi_b)
    g_last = jnp.exp(_bdot(jnp.ones((nb, DK, CHUNK), F32), g, HI))

    state = s_in
    rows = []
    for n in range(nc):
        sl = slice(n * HEADS, (n + 1) * HEADS)
        u_new = u_mat[sl] - _bdot(w_mat[sl], state)
        o_n = _bdot(q_g[sl], state) + _bdot(a_qk[sl], u_new)
        state = state * g_last[sl] + _bdot(jnp.swapaxes(k_d[sl], 1, 2), u_new)
        o_n = o_n * lax.rsqrt(jnp.mean(o_n * o_n, axis=-1, keepdims=True) + RMS_EPS) * nw_row
        z_n = jnp.stack([z[n * CHUNK:(n + 1) * CHUNK, h * DK:(h + 1) * DK] for h in range(HEADS)])
        o_n = o_n * jax.nn.silu(z_n)
        rows.append(jnp.concatenate([o_n[h] for h in range(HEADS)], axis=-1))
    return jnp.concatenate(rows, axis=0), state


def _sg_glue(uv, lng, lnb, w_s, bs_t):
    t_rows = uv.shape[0]
    y = jax.nn.gelu(uv)
    u = y[:, :SG_W]
    v = _ln(y[:, SG_W:], lng, lnb)
    r = lax.broadcasted_iota(jnp.int32, (LANES, LANES), 0)
    c = lax.broadcasted_iota(jnp.int32, (LANES, LANES), 1)
    wm = jnp.where(r >= c, w_s, 0.0)
    lane = lax.broadcasted_iota(jnp.int32, (LANES, SG_W), 0)
    group_of_col = lax.broadcasted_iota(jnp.int32, (LANES, SG_W), 1) // LANES
    e_grp = (group_of_col == lane).astype(F32)
    bias = jnp.dot(bs_t, e_grp, precision=HI, preferred_element_type=F32)
    outs = []
    for n in range(t_rows // LANES):
        vb = v[n * LANES:(n + 1) * LANES]
        vg = jnp.stack([vb[:, g * LANES:(g + 1) * LANES] for g in range(4)])
        mg = _bdot(wm, vg)
        mixed = jnp.concatenate([mg[g] for g in range(4)], axis=-1) + bias
        outs.append(u[n * LANES:(n + 1) * LANES] * mixed)
    return jnp.concatenate(outs, axis=0)


def _merge_glue(ga, gb, ya, yb):
    return jax.nn.sigmoid(ga) * ya + jax.nn.sigmoid(gb) * yb


def _res_ln_glue(x, r, g, b):
    return _ln(ALPHA * x + r, g, b)


def _ffn_glue(upcat, cw):
    y = _conv(upcat, cw)
    return jax.nn.silu(y[:, :FFN_HALF]) * y[:, FFN_HALF:]


def _row(t, c, col=0):
    return pl.BlockSpec((t, c), lambda i: (i, col))


def _row_rev(t, c, nt, col=0):
    return pl.BlockSpec((t, c), lambda i: (nt - 1 - i, col))


def _halo(t, c, nt=None):
    per = t // HALO
    if nt is None:
        return pl.BlockSpec((HALO, c), lambda i: (jnp.maximum(i * per - 1, 0), 0))
    return pl.BlockSpec((HALO, c), lambda i: (jnp.maximum((nt - 1 - i) * per - 1, 0), 0))


def _full(shape):
    nd = len(shape)
    return pl.BlockSpec(shape, lambda i: (0,) * nd)


ANY = pl.BlockSpec(memory_space=pl.ANY)


def _sds(shape, dtype=F32):
    return jax.ShapeDtypeStruct(shape, dtype)


def _tile(s):
    return 256 if s % 256 == 0 else 128


def proj_fwd(x, w, name):
    s = x.shape[0]
    t = _tile(s)
    segs = [(0, 2048), (2048, 3072), (3072, 4096), (4096, 5120), (5120, IN_COLS_PAD)]

    def body(x_ref, w_ref, p_ref):
        xb = x_ref[...].astype(BF16)
        for lo, hi in segs:
            p_ref[:, lo:hi] = jnp.dot(xb, w_ref[:, lo:hi], preferred_element_type=F32)

    return pl.pallas_call(
        body, grid=(s // t,), name=name,
        in_specs=[_row(t, D), _full((D, IN_COLS_PAD))],
        out_specs=_row(t, IN_COLS_PAD),
        out_shape=_sds((s, IN_COLS_PAD)), compiler_params=_params())(x, w)


def dn_fwd(p, cq, a_row, dtb_row, nw_row, name):
    s = p.shape[0]
    t = _tile(s)
    nt = s // t

    def body(qkv_ref, halo_ref, z_ref, ba_ref, cq_ref, a_ref, dtb_ref, nw_ref, o_ref, sst_ref, s_scr):
        i = pl.program_id(0)

        @pl.when(i == 0)
        def _():
            s_scr[...] = jnp.zeros_like(s_scr)

        halo = jnp.where(i == 0, 0.0, halo_ref[...])
        qkvcat = jnp.concatenate([halo, qkv_ref[...]], axis=0)
        cw = tuple(cq_ref[k:k + 1, :] for k in range(4))
        s_in = s_scr[...]
        sst_ref[0] = s_in
        o, s_out = _dn_glue(qkvcat, z_ref[...], ba_ref[...], s_in, cw, a_ref[...], dtb_ref[...], nw_ref[...])
        o_ref[...] = o.astype(BF16)
        s_scr[...] = s_out

    return pl.pallas_call(
        body, grid=(nt,), name=name,
        in_specs=[_row(t, QKV_W), _halo(t, QKV_W), _row(t, Z_W, C_Z // Z_W), _row(t, LANES, C_BA // LANES),
                  _full((4, QKV_W)), _full((1, LANES)), _full((1, LANES)), _full((1, LANES))],
        out_specs=[_row(t, Z_W), pl.BlockSpec((1, HEADS, DK, DK), lambda i: (i, 0, 0, 0))],
        out_shape=[_sds((s, Z_W), BF16), _sds((nt, HEADS, DK, DK))],
        scratch_shapes=[pltpu.VMEM((HEADS, DK, DK), F32)],
        compiler_params=_params())(p, p, p, p, cq, a_row, dtb_row, nw_row)


def sg_fwd(p, lng, lnb, w_s, bs_t, name):
    s = p.shape[0]
    t = _tile(s)

    def body(uv_ref, lng_ref, lnb_ref, ws_ref, bs_ref, o_ref):
        o_ref[...] = _sg_glue(uv_ref[...], lng_ref[...], lnb_ref[...], ws_ref[...], bs_ref[...]).astype(BF16)

    return pl.pallas_call(
        body, grid=(s // t,), name=name,
        in_specs=[_row(t, 2 * SG_W, C_UV // (2 * SG_W)), _full((1, SG_W)), _full((1, SG_W)),
                  _full((4, LANES, LANES)), _full((LANES, LANES))],
        out_specs=_row(t, SG_W), out_shape=_sds((s, SG_W), BF16), compiler_params=_params())(p, lng, lnb, w_s, bs_t)


def merge_fwd(oa, ob, p, x, wa, wb, wo, g1, b1, name):
    s = x.shape[0]
    t = _tile(s)

    def body(oa_ref, ob_ref, ga_ref, gb_ref, x_ref, wa_ref, wb_ref, wo_ref, g_ref, b_ref, x1_ref):
        ya = _mm(oa_ref[...], wa_ref[...])
        yb = _mm(ob_ref[...], wb_ref[...])
        h = _merge_glue(ga_ref[...], gb_ref[...], ya, yb)
        x1_ref[...] = _res_ln_glue(x_ref[...], _mm(h, wo_ref[...]), g_ref[...], b_ref[...])

    return pl.pallas_call(
        body, grid=(s // t,), name=name,
        in_specs=[_row(t, Z_W), _row(t, SG_W), _row(t, D, C_GA // D), _row(t, D, C_GB // D), _row(t, D),
                  _full((Z_W, D)), _full((SG_W, D)), _full((D, D)), _full((1, D)), _full((1, D))],
        out_specs=_row(t, D), out_shape=_sds((s, D)), compiler_params=_params())(oa, ob, p, p, x, wa, wb, wo, g1, b1)


def ffn_fwd(x1, wup2, cf2, wdn2, g2, b2, name):
    s = x1.shape[0]
    t = _tile(s)

    def body(x1_ref, halo_ref, wup_hbm, cf_ref, wdn_hbm, g_ref, b_ref, pre_ref, x2_ref, wup_v, wdn_v):
        i = pl.program_id(0)

        @pl.when(i == 0)
        def _():
            pltpu.sync_copy(wup_hbm, wup_v)
            pltpu.sync_copy(wdn_hbm, wdn_v)

        x1v = x1_ref[...]
        halo = jnp.where(i == 0, 0.0, halo_ref[...])
        x1cat = jnp.concatenate([halo, x1v], axis=0).astype(BF16)
        f = None
        for h in range(2):
            upcat = jnp.dot(x1cat, wup_v[h], preferred_element_type=F32)
            cw = tuple(cf_ref[h, k:k + 1, :] for k in range(3))
            act = _ffn_glue(upcat, cw)
            fh = _mm(act, wdn_v[h])
            f = fh if f is None else f + fh
        pre = ALPHA * x1v + f
        pre_ref[...] = pre
        x2_ref[...] = _ln(pre, g_ref[...], b_ref[...])

    return pl.pallas_call(
        body, grid=(s // t,), name=name,
        in_specs=[_row(t, D), _halo(t, D), ANY, _full((2, 3, FFN)), ANY, _full((1, D)), _full((1, D))],
        out_specs=[_row(t, D), _row(t, D)], out_shape=[_sds((s, D)), _sds((s, D))],
        scratch_shapes=[pltpu.VMEM((2, D, FFN), BF16), pltpu.VMEM((2, FFN_HALF, D), BF16)],
        compiler_params=_params())(x1, x1, wup2, cf2, wdn2, g2, b2)


def loss_call(y, tgt, name):
    s = y.shape[0]
    t = _tile(s)

    def body(y_ref, t_ref, dy_ref, loss_ref):
        @pl.when(pl.program_id(0) == 0)
        def _():
            loss_ref[...] = jnp.zeros_like(loss_ref)

        e = y_ref[...] - t_ref[...]
        dy_ref[...] = e * (1.0 / D)
        part = jnp.sum(jnp.sum(e * e, axis=1, keepdims=True), axis=0, keepdims=True) * (0.5 / D)
        loss_ref[...] += jnp.broadcast_to(part, loss_ref.shape)

    return pl.pallas_call(
        body, grid=(s // t,), name=name, in_specs=[_row(t, D), _row(t, D)],
        out_specs=[_row(t, D), _full((8, LANES))], out_shape=[_sds((s, D)), _sds((8, LANES))],
        compiler_params=_params())(y, tgt)


def _acc(ref, val, first):
    @pl.when(first)
    def _():
        ref[...] = val

    @pl.when(jnp.logical_not(first))
    def _():
        ref[...] += val


def _acc_tn(acc_ref, a, b, first, seg):
    n = b.shape[1]
    for lo in range(0, n, seg):
        hi = min(lo + seg, n)
        _acc(acc_ref.at[:, lo:hi], _mm_tn(a, b[:, lo:hi]), first)


def ln_bwd(pre, dy, g, b, name):
    s = pre.shape[0]
    t = _tile(s)

    def body(pre_ref, dy_ref, g_ref, b_ref, dpre_ref, dg_ref, db_ref):
        _, vjp = jax.vjp(_ln, pre_ref[...], g_ref[...], b_ref[...])
        dpre, dg, db = vjp(dy_ref[...])
        dpre_ref[...] = dpre
        first = pl.program_id(0) == 0
        _acc(dg_ref, dg, first)
        _acc(db_ref, db, first)

    return pl.pallas_call(
        body, grid=(s // t,), name=name, in_specs=[_row(t, D), _row(t, D), _full((1, D)), _full((1, D))],
        out_specs=[_row(t, D), _full((1, D)), _full((1, D))],
        out_shape=[_sds((s, D)), _sds((1, D)), _sds((1, D))], compiler_params=_params())(pre, dy, g, b)


def ffn_bwd(x1, df, acc_in, acc_scale, wup_h, cf_h, wdn_h, name):
    s = x1.shape[0]
    t = _tile(s)
    nt = s // t

    def body(x1_ref, halo_ref, df_ref, acc_ref, wup_hbm, cf_ref, wdn_hbm,
             dx1_ref, dwup_hbm, dcf_ref, dwdn_hbm, wup_ref, wdn_ref, dwup_v, dwdn_v, carry):
        i = pl.program_id(0)
        j = nt - 1 - i
        first = i == 0

        @pl.when(first)
        def _():
            pltpu.sync_copy(wup_hbm, wup_ref)
            pltpu.sync_copy(wdn_hbm, wdn_ref)
            carry[...] = jnp.zeros_like(carry)

        halo = jnp.where(j == 0, 0.0, halo_ref[...])
        x1cat = jnp.concatenate([halo, x1_ref[...]], axis=0).astype(BF16)
        upcat = jnp.dot(x1cat, wup_ref[...], preferred_element_type=F32)
        cw = tuple(cf_ref[k:k + 1, :] for k in range(3))
        act, vjp = jax.vjp(_ffn_glue, upcat, cw)
        dfb = df_ref[...].astype(BF16)
        dact = _mm_nt(dfb, wdn_ref[...])
        _acc_tn(dwdn_v, act.astype(BF16), dfb, first, 512)
        dupcat, dcw = vjp(dact)
        dupb = dupcat.astype(BF16)
        dx1cat = _mm_nt(dupb, wup_ref[...])
        _acc_tn(dwup_v, x1cat, dupb, first, FFN_HALF)
        for k in range(3):
            _acc(dcf_ref.at[k:k + 1, :], dcw[k], first)
        out = acc_scale * acc_ref[...] + dx1cat[HALO:]
        dx1_ref[...] = out
        dx1_ref[t - HALO:t, :] = out[t - HALO:] + carry[...]
        carry[...] = dx1cat[:HALO]

        @pl.when(i == nt - 1)
        def _():
            pltpu.sync_copy(dwup_v, dwup_hbm)
            pltpu.sync_copy(dwdn_v, dwdn_hbm)

    return pl.pallas_call(
        body, grid=(nt,), name=name,
        in_specs=[_row_rev(t, D, nt), _halo(t, D, nt), _row_rev(t, D, nt), _row_rev(t, D, nt),
                  ANY, _full((3, FFN)), ANY],
        out_specs=[_row_rev(t, D, nt), ANY, _full((3, FFN)), ANY],
        out_shape=[_sds((s, D)), _sds((D, FFN)), _sds((3, FFN)), _sds((FFN_HALF, D))],
        scratch_shapes=[pltpu.VMEM((D, FFN), BF16), pltpu.VMEM((FFN_HALF, D), BF16),
                        pltpu.VMEM((D, FFN), F32), pltpu.VMEM((FFN_HALF, D), F32), pltpu.VMEM((HALO, D), F32)],
        compiler_params=_params())(x1, x1, df, acc_in, wup_h, cf_h, wdn_h)


def merge_bwd(oa, ob, p, x, dx1, wa, wb, wo, g1, b1, name):
    s = x.shape[0]
    t = _tile(s)

    def body(oa_ref, ob_ref, ga_ref, gb_ref, x_ref, dx1_ref, wa_ref, wb_ref, wo_ref, g_ref, b_ref,
             doa_ref, dob_ref, dga_ref, dgb_ref, dx_ref, dwa_ref, dwb_ref, dwo_ref, dg_ref, db_ref):
        first = pl.program_id(0) == 0
        oa = oa_ref[...]
        ob = ob_ref[...]
        ya = _mm(oa, wa_ref[...])
        yb = _mm(ob, wb_ref[...])
        h, vjp1 = jax.vjp(_merge_glue, ga_ref[...], gb_ref[...], ya, yb)
        hb = h.astype(BF16)
        r = _mm(hb, wo_ref[...])
        _, vjp2 = jax.vjp(_res_ln_glue, x_ref[...], r, g_ref[...], b_ref[...])
        dx, dr, dg, db = vjp2(dx1_ref[...])
        dx_ref[...] = dx
        _acc(dg_ref, dg, first)
        _acc(db_ref, db, first)
        drb = dr.astype(BF16)
        dh = _mm_nt(drb, wo_ref[...])
        _acc(dwo_ref, _mm_tn(hb, drb), first)
        dga, dgb, dya, dyb = vjp1(dh)
        dga_ref[...] = dga.astype(BF16)
        dgb_ref[...] = dgb.astype(BF16)
        dyab = dya.astype(BF16)
        dybb = dyb.astype(BF16)
        doa_ref[...] = _mm_nt(dyab, wa_ref[...]).astype(BF16)
        dob_ref[...] = _mm_nt(dybb, wb_ref[...]).astype(BF16)
        _acc(dwa_ref, _mm_tn(oa, dyab), first)
        _acc(dwb_ref, _mm_tn(ob, dybb), first)

    return pl.pallas_call(
        body, grid=(s // t,), name=name,
        in_specs=[_row(t, Z_W), _row(t, SG_W), _row(t, D, C_GA // D), _row(t, D, C_GB // D), _row(t, D), _row(t, D),
                  _full((Z_W, D)), _full((SG_W, D)), _full((D, D)), _full((1, D)), _full((1, D))],
        out_specs=[_row(t, Z_W), _row(t, SG_W), _row(t, D), _row(t, D), _row(t, D),
                   _full((Z_W, D)), _full((SG_W, D)), _full((D, D)), _full((1, D)), _full((1, D))],
        out_shape=[_sds((s, Z_W), BF16), _sds((s, SG_W), BF16), _sds((s, D), BF16), _sds((s, D), BF16), _sds((s, D)),
                   _sds((Z_W, D)), _sds((SG_W, D)), _sds((D, D)), _sds((1, D)), _sds((1, D))],
        compiler_params=_params())(oa, ob, p, p, x, dx1, wa, wb, wo, g1, b1)


def sg_bwd(p, dob, lng, lnb, w_s, bs_t, name):
    s = p.shape[0]
    t = _tile(s)

    def body(uv_ref, dob_ref, lng_ref, lnb_ref, ws_ref, bs_ref, duv_ref, dlng_ref, dlnb_ref, dws_ref, dbs_ref):
        first = pl.program_id(0) == 0
        _, vjp = jax.vjp(_sg_glue, uv_ref[...], lng_ref[...], lnb_ref[...], ws_ref[...], bs_ref[...])
        duv, dlng, dlnb, dws, dbs = vjp(dob_ref[...].astype(F32))
        duv_ref[...] = duv.astype(BF16)
        _acc(dlng_ref, dlng, first)
        _acc(dlnb_ref, dlnb, first)
        _acc(dws_ref, dws, first)
        _acc(dbs_ref, dbs, first)

    return pl.pallas_call(
        body, grid=(s // t,), name=name,
        in_specs=[_row(t, 2 * SG_W, C_UV // (2 * SG_W)), _row(t, SG_W), _full((1, SG_W)), _full((1, SG_W)),
                  _full((4, LANES, LANES)), _full((LANES, LANES))],
        out_specs=[_row(t, 2 * SG_W), _full((1, SG_W)), _full((1, SG_W)), _full((4, LANES, LANES)), _full((LANES, LANES))],
        out_shape=[_sds((s, 2 * SG_W), BF16), _sds((1, SG_W)), _sds((1, SG_W)), _sds((4, LANES, LANES)), _sds((LANES, LANES))],
        compiler_params=_params())(p, dob, lng, lnb, w_s, bs_t)


def dn_bwd(p, sst, doa, cq, a_row, dtb_row, nw_row, name):
    s = p.shape[0]
    t = _tile(s)
    nt = s // t

    def body(qkv_ref, halo_ref, z_ref, ba_ref, sst_ref, doa_ref, cq_ref, a_ref, dtb_ref, nw_ref,
             dqkv_ref, dz_ref, dba_ref, dcq_ref, da_ref, ddtb_ref, dnw_ref, ds_scr, carry):
        i = pl.program_id(0)
        j = nt - 1 - i
        first = i == 0

        @pl.when(first)
        def _():
            ds_scr[...] = jnp.zeros_like(ds_scr)
            carry[...] = jnp.zeros_like(carry)

        halo = jnp.where(j == 0, 0.0, halo_ref[...])
        qkvcat = jnp.concatenate([halo, qkv_ref[...]], axis=0)
        cw = tuple(cq_ref[k:k + 1, :] for k in range(4))
        _, vjp = jax.vjp(_dn_glue, qkvcat, z_ref[...], ba_ref[...], sst_ref[0], cw, a_ref[...], dtb_ref[...], nw_ref[...])
        dqkvcat, dz, dba, ds_in, dcw, da, ddtb, dnw = vjp((doa_ref[...].astype(F32), ds_scr[...]))
        ds_scr[...] = ds_in
        dz_ref[...] = dz.astype(BF16)
        dba_ref[...] = dba.astype(BF16)
        dtile = dqkvcat[HALO:]
        dqkv_ref[...] = dtile.astype(BF16)
        dqkv_ref[t - HALO:t, :] = (dtile[t - HALO:] + carry[...]).astype(BF16)
        carry[...] = dqkvcat[:HALO]
        for k in range(4):
            _acc(dcq_ref.at[k:k + 1, :], dcw[k], first)
        _acc(da_ref, da, first)
        _acc(ddtb_ref, ddtb, first)
        _acc(dnw_ref, dnw, first)

    return pl.pallas_call(
        body, grid=(nt,), name=name,
        in_specs=[_row_rev(t, QKV_W, nt), _halo(t, QKV_W, nt), _row_rev(t, Z_W, nt, C_Z // Z_W),
                  _row_rev(t, LANES, nt, C_BA // LANES),
                  pl.BlockSpec((1, HEADS, DK, DK), lambda i: (nt - 1 - i, 0, 0, 0)), _row_rev(t, Z_W, nt),
                  _full((4, QKV_W)), _full((1, LANES)), _full((1, LANES)), _full((1, LANES))],
        out_specs=[_row_rev(t, QKV_W, nt), _row_rev(t, Z_W, nt), _row_rev(t, LANES, nt),
                   _full((4, QKV_W)), _full((1, LANES)), _full((1, LANES)), _full((1, LANES))],
        out_shape=[_sds((s, QKV_W), BF16), _sds((s, Z_W), BF16), _sds((s, LANES), BF16),
                   _sds((4, QKV_W)), _sds((1, LANES)), _sds((1, LANES)), _sds((1, LANES))],
        scratch_shapes=[pltpu.VMEM((HEADS, DK, DK), F32), pltpu.VMEM((HALO, QKV_W), F32)],
        compiler_params=_params())(p, p, p, p, sst, doa, cq, a_row, dtb_row, nw_row)


def proj_bwd(x, dqkv, dz, duv, dga, dgb, dba, dxd, w, name):
    s = x.shape[0]
    t = _tile(s)
    nt = s // t

    def body(x_ref, dqkv_ref, dz_ref, duv_ref, dga_ref, dgb_ref, dba_ref, dxd_ref, w_hbm,
             dx_ref, dw_hbm, w_v, dw_v):
        i = pl.program_id(0)
        first = i == 0

        @pl.when(first)
        def _():
            pltpu.sync_copy(w_hbm, w_v)

        dp = jnp.concatenate([dqkv_ref[...], dz_ref[...], duv_ref[...], dga_ref[...], dgb_ref[...], dba_ref[...]], axis=1)
        dx_ref[...] = dxd_ref[...] + _mm_nt(dp, w_v[...])
        _acc_tn(dw_v, x_ref[...].astype(BF16), dp, first, 1024)

        @pl.when(i == nt - 1)
        def _():
            pltpu.sync_copy(dw_v, dw_hbm)

    return pl.pallas_call(
        body, grid=(nt,), name=name,
        in_specs=[_row(t, D), _row(t, QKV_W), _row(t, Z_W), _row(t, 2 * SG_W), _row(t, D), _row(t, D), _row(t, LANES),
                  _row(t, D), ANY],
        out_specs=[_row(t, D), ANY], out_shape=[_sds((s, D)), _sds((D, IN_COLS_PAD))],
        scratch_shapes=[pltpu.VMEM((D, IN_COLS_PAD), BF16), pltpu.VMEM((D, IN_COLS_PAD), F32)],
        compiler_params=_params())(x, dqkv, dz, duv, dga, dgb, dba, dxd, w)


def _rows_block(rows, cols):
    cap = max(8, (2 * 1024 * 1024) // (cols * 4))
    for cand in range(min(rows, cap) // 8 * 8, 7, -8):
        if rows % cand == 0:
            return cand
    return rows


def adam_call(w, g, m, v, name):
    rows, cols = w.shape
    tr = _rows_block(rows, cols)
    c1 = 1.0 - ADAM_B1 ** ADAM_STEP
    c2 = 1.0 - ADAM_B2 ** ADAM_STEP

    def body(w_ref, g_ref, m_ref, v_ref, d_ref, nm_ref, nv_ref):
        gv = g_ref[...]
        nm = ADAM_B1 * m_ref[...] + (1.0 - ADAM_B1) * gv
        nv = ADAM_B2 * v_ref[...] + (1.0 - ADAM_B2) * (gv * gv)
        d_ref[...] = -ADAM_LR * ((nm / c1) / (jnp.sqrt(nv / c2) + ADAM_EPS) + ADAM_WD * w_ref[...])
        nm_ref[...] = nm
        nv_ref[...] = nv

    spec = pl.BlockSpec((tr, cols), lambda i: (i, 0))
    return pl.pallas_call(
        body, grid=(rows // tr,), name=name, in_specs=[spec] * 4, out_specs=[spec] * 3,
        out_shape=[_sds((rows, cols))] * 3, compiler_params=_params())(w, g, m, v)


def _place():
    return lax.axis_index("x"), lax.axis_index("y"), lax.axis_index("c")


def _other_chips(x, y):
    return [(1 - x, y), (x, 1 - y), (1 - x, 1 - y)]


def gather_chips(xs, name):
    _, rh, lanes = xs.shape

    def body(x_ref, out_ref, send_sems, recv_sems, local_sems):
        x, y, c = _place()
        me = 2 * x + y
        sibling = (x, y, 1 - c)
        chips = _other_chips(x, y)

        def copy(k, slot, half, to, src=None):
            dst = out_ref.at[slot, half]
            return pltpu.make_async_remote_copy(
                src_ref=dst if src is None else src, dst_ref=dst, send_sem=send_sems.at[k], recv_sem=recv_sems.at[k],
                device_id=to, device_id_type=MESH)

        step = rh // LOCAL_PIECES
        mine = [pltpu.make_async_copy(x_ref.at[h, pl.ds(q * step, step)], out_ref.at[me, h, pl.ds(q * step, step)],
                                      local_sems.at[h * LOCAL_PIECES + q]) for h in range(2) for q in range(LOCAL_PIECES)]
        for cp in mine:
            cp.start()
        first = [copy(k, me, c, (cx, cy, c), src=x_ref.at[c]) for k, (cx, cy) in enumerate(chips)]
        for cp in first:
            cp.start()
        passed = [copy(3 + k, 2 * cx + cy, c, sibling) for k, (cx, cy) in enumerate(chips)]
        for k, (cx, cy) in enumerate(chips):
            copy(k, 2 * cx + cy, c, (x, y, c)).wait_recv()
            passed[k].start()
        for k, (cx, cy) in enumerate(chips):
            copy(3 + k, 2 * cx + cy, 1 - c, (x, y, c)).wait_recv()
        for cp in first + passed:
            cp.wait_send()
        for cp in mine:
            cp.wait()

    return pl.pallas_call(
        body, name=name, in_specs=[ANY], out_specs=ANY, out_shape=_sds((N_CHIPS, 2, rh, lanes), xs.dtype),
        scratch_shapes=[pltpu.SemaphoreType.DMA((6,)), pltpu.SemaphoreType.DMA((6,)),
                        pltpu.SemaphoreType.DMA((2 * LOCAL_PIECES,))],
    )(xs)


def pair_split(g2, name):
    _, n, rh, lanes = g2.shape

    def body(g_ref, got_ref, send_sem, recv_sem):
        x, y, c = _place()
        cp = pltpu.make_async_remote_copy(
            src_ref=g_ref.at[1 - c], dst_ref=got_ref, send_sem=send_sem, recv_sem=recv_sem,
            device_id=(x, y, 1 - c), device_id_type=MESH)
        cp.start()
        cp.wait()

    return pl.pallas_call(
        body, name=name, in_specs=[ANY], out_specs=ANY, out_shape=_sds((n, rh, lanes)),
        scratch_shapes=[pltpu.SemaphoreType.DMA, pltpu.SemaphoreType.DMA],
    )(g2)


def scatter_chips(ps, name):
    n, rh, lanes = ps.shape

    def body(p_ref, out_ref, send_sems, recv_sems):
        x, y, c = _place()
        me = 2 * x + y
        chips = _other_chips(x, y)
        sends = [pltpu.make_async_remote_copy(
            src_ref=p_ref.at[2 * cx + cy], dst_ref=out_ref.at[me], send_sem=send_sems.at[k], recv_sem=recv_sems.at[k],
            device_id=(cx, cy, c), device_id_type=MESH) for k, (cx, cy) in enumerate(chips)]
        for cp in sends:
            cp.start()
        for k, (cx, cy) in enumerate(chips):
            pltpu.make_async_remote_copy(
                src_ref=p_ref.at[me], dst_ref=out_ref.at[2 * cx + cy], send_sem=send_sems.at[k],
                recv_sem=recv_sems.at[k], device_id=(x, y, c), device_id_type=MESH).wait_recv()
        for cp in sends:
            cp.wait_send()

    return pl.pallas_call(
        body, name=name, in_specs=[ANY], out_specs=ANY, out_shape=_sds((n, rh, lanes)),
        scratch_shapes=[pltpu.SemaphoreType.DMA((3,)), pltpu.SemaphoreType.DMA((3,))],
    )(ps)


def pair_join(halves, name):
    _, rh, lanes = halves.shape

    def body(h_ref, out_ref, send_sem, recv_sem):
        x, y, c = _place()
        cp = pltpu.make_async_remote_copy(
            src_ref=h_ref.at[c], dst_ref=out_ref.at[c], send_sem=send_sem, recv_sem=recv_sem,
            device_id=(x, y, 1 - c), device_id_type=MESH)
        cp.start()
        cp.wait_send()
        pltpu.make_async_remote_copy(
            src_ref=h_ref.at[c], dst_ref=out_ref.at[1 - c], send_sem=send_sem, recv_sem=recv_sem,
            device_id=(x, y, c), device_id_type=MESH).wait_recv()

    return pl.pallas_call(
        body, name=name, in_specs=[ANY], out_specs=ANY, out_shape=_sds((2, rh, lanes)),
        input_output_aliases={0: 0},
        scratch_shapes=[pltpu.SemaphoreType.DMA, pltpu.SemaphoreType.DMA],
    )(halves)


def pair_add(g2, theirs, c_vec, name):
    _, rows, lanes = g2.shape
    tr = _rows_block(rows, lanes)

    def body(c_ref, a_ref, b_ref, o_ref):
        o_ref[...] = a_ref[...] + b_ref[...]

    grid_spec = pltpu.PrefetchScalarGridSpec(
        num_scalar_prefetch=1, grid=(rows // tr,),
        in_specs=[pl.BlockSpec((None, tr, lanes), lambda i, c: (c[0], i, 0)), pl.BlockSpec((tr, lanes), lambda i, c: (i, 0))],
        out_specs=pl.BlockSpec((tr, lanes), lambda i, c: (i, 0)))
    return pl.pallas_call(body, grid_spec=grid_spec, name=name, out_shape=_sds((rows, lanes)),
                          compiler_params=_params())(c_vec, g2, theirs)


def chips_add(pair, from_chips, place_vec, name):
    n, rh, lanes = pair.shape
    tr = _rows_block(rh, lanes)

    def body(s_ref, mine_ref, *refs):
        me = s_ref[0]
        acc = None
        for k in range(n):
            term = jnp.where(me == k, mine_ref[...], refs[k][...])
            acc = term if acc is None else acc + term
        refs[n][...] = acc

    def other(k):
        return pl.BlockSpec((None, tr, lanes), lambda i, s: (jnp.where(s[0] == k, (k + 1) % n, k), i, 0))

    grid_spec = pltpu.PrefetchScalarGridSpec(
        num_scalar_prefetch=1, grid=(rh // tr,),
        in_specs=[pl.BlockSpec((None, tr, lanes), lambda i, s: (s[0], i, 0))] + [other(k) for k in range(n)],
        out_specs=pl.BlockSpec((None, tr, lanes), lambda i, s: (s[1], i, 0)))
    return pl.pallas_call(body, grid_spec=grid_spec, name=name, out_shape=_sds((2, rh, lanes)),
                          compiler_params=_params())(place_vec, pair, *([from_chips] * n))


def allsum_small(v, name):
    rows, lanes = v.shape
    n_dev = 8

    def body(v_ref, out_ref, buf, send_sems, recv_sems):
        x, y, c = _place()
        me, sibling = (x, y, c), (x, y, 1 - c)
        chips = _other_chips(x, y)

        def slot(px, py, pc):
            return buf.at[4 * px + 2 * py + pc]

        def copy(k, block, to, src=None):
            return pltpu.make_async_remote_copy(
                src_ref=slot(*block) if src is None else src, dst_ref=slot(*block),
                send_sem=send_sems.at[k], recv_sem=recv_sems.at[k], device_id=to, device_id_type=MESH)

        slot(*me)[...] = v_ref[...]
        first = [copy(0, me, sibling, src=v_ref)]
        first += [copy(1 + k, me, (*chip, c), src=v_ref) for k, chip in enumerate(chips)]
        for cp in first:
            cp.start()
        passed = [copy(4 + k, (*chip, c), sibling) for k, chip in enumerate(chips)]
        for k, chip in enumerate(chips):
            copy(1 + k, (*chip, c), me).wait_recv()
            passed[k].start()
        copy(0, sibling, me).wait_recv()
        for k, chip in enumerate(chips):
            copy(4 + k, (*chip, 1 - c), me).wait_recv()
        for cp in first + passed:
            cp.wait_send()
        acc = buf[0]
        for d in range(1, n_dev):
            acc = acc + buf[d]
        out_ref[...] = acc

    vm = pl.BlockSpec(memory_space=pltpu.VMEM)
    return pl.pallas_call(
        body, name=name, in_specs=[vm], out_specs=vm, out_shape=_sds((rows, lanes)),
        scratch_shapes=[pltpu.VMEM((n_dev, rows, lanes), F32), pltpu.SemaphoreType.DMA((7,)), pltpu.SemaphoreType.DMA((7,))],
        compiler_params=pltpu.CompilerParams(vmem_limit_bytes=VMEM_LIMIT),
    )(v)


BIG = (("w_in", 2), ("w_branch_a", 2), ("w_branch_b", 2), ("w_out", 1), ("w_up", 2), ("w_down", 1))
CONV = (("conv_qkv", 2), ("conv_ffn", 2))
REPL = ("a_log", "dt_bias", "dn_norm_w", "sg_ln_g", "sg_ln_b", "w_spatial", "b_spatial", "ln1_g", "ln1_b", "ln2_g", "ln2_b")


def _pad_rows(flat, mult):
    n = flat.shape[0]
    unit = mult * LANES
    total = -(-n // unit) * unit
    return jnp.pad(flat, (0, total - n)).reshape(total // LANES, LANES)


def _pack(arrs, mult):
    return _pad_rows(jnp.concatenate([a.reshape(-1) for a in arrs]), mult)


def _unpack(flat, shapes):
    out, off = [], 0
    for shp in shapes:
        n = math.prod(shp)
        out.append(flat[off:off + n].reshape(shp))
        off += n
    return out


def _gather_full(shards, axes, dtype, mult, name):
    packed = _pack([a.astype(dtype) for a in shards], 2 * mult)
    rows = packed.shape[0]
    got = gather_chips(packed.reshape(2, rows // 2, LANES), name).reshape(N_CHIPS, rows * LANES)
    per_chip = [_unpack(got[k], [a.shape for a in shards]) for k in range(N_CHIPS)]
    return [jnp.concatenate([per_chip[k][i] for k in range(N_CHIPS)], axis=ax) for i, ax in enumerate(axes)]


def kernel(x, w_in, conv_qkv, a_log, dt_bias, dn_norm_w, w_branch_a, sg_ln_g, sg_ln_b, w_spatial, b_spatial, w_branch_b, w_out, ln1_g, ln1_b, w_up, conv_ffn, w_down, ln2_g, ln2_b, loss_target, m_w_in, m_conv_qkv, m_a_log, m_dt_bias, m_dn_norm_w, m_w_branch_a, m_sg_ln_g, m_sg_ln_b, m_w_spatial, m_b_spatial, m_w_branch_b, m_w_out, m_ln1_g, m_ln1_b, m_w_up, m_conv_ffn, m_w_down, m_ln2_g, m_ln2_b, v_w_in, v_conv_qkv, v_a_log, v_dt_bias, v_dn_norm_w, v_w_branch_a, v_sg_ln_g, v_sg_ln_b, v_w_spatial, v_b_spatial, v_w_branch_b, v_w_out, v_ln1_g, v_ln1_b, v_w_up, v_conv_ffn, v_w_down, v_ln2_g, v_ln2_b):
    names = ("w_in", "conv_qkv", "a_log", "dt_bias", "dn_norm_w", "w_branch_a", "sg_ln_g", "sg_ln_b", "w_spatial",
             "b_spatial", "w_branch_b", "w_out", "ln1_g", "ln1_b", "w_up", "conv_ffn", "w_down", "ln2_g", "ln2_b")
    w = dict(zip(names, (w_in, conv_qkv, a_log, dt_bias, dn_norm_w, w_branch_a, sg_ln_g, sg_ln_b, w_spatial,
                         b_spatial, w_branch_b, w_out, ln1_g, ln1_b, w_up, conv_ffn, w_down, ln2_g, ln2_b)))
    m = dict(zip(names, (m_w_in, m_conv_qkv, m_a_log, m_dt_bias, m_dn_norm_w, m_w_branch_a, m_sg_ln_g, m_sg_ln_b,
                         m_w_spatial, m_b_spatial, m_w_branch_b, m_w_out, m_ln1_g, m_ln1_b, m_w_up, m_conv_ffn,
                         m_w_down, m_ln2_g, m_ln2_b)))
    v = dict(zip(names, (v_w_in, v_conv_qkv, v_a_log, v_dt_bias, v_dn_norm_w, v_w_branch_a, v_sg_ln_g, v_sg_ln_b,
                         v_w_spatial, v_b_spatial, v_w_branch_b, v_w_out, v_ln1_g, v_ln1_b, v_w_up, v_conv_ffn,
                         v_w_down, v_ln2_g, v_ln2_b)))
    chip = 2 * lax.axis_index("x") + lax.axis_index("y")
    s = x.shape[1]
    xs = x.reshape(s, D)
    tgt = loss_target.reshape(s, D)

    big_names = [n for n, _ in BIG]
    big_axes = [a for _, a in BIG]
    full = dict(zip(big_names, _gather_full([w[n] for n in big_names], big_axes, BF16, 16, "gather_big")))
    conv_names = [n for n, _ in CONV]
    full.update(zip(conv_names, _gather_full([w[n] for n in conv_names], [a for _, a in CONV], F32, 8, "gather_conv")))

    def lane_row(vec, off):
        return jnp.zeros((1, LANES), F32).at[0, off:off + vec.shape[0]].set(vec)

    layers = []
    for l in range(DEPTH):
        wi = full["w_in"][l]
        wup, cf, wdn = full["w_up"][l], full["conv_ffn"][l], full["w_down"][l]
        halves = lambda a: [jnp.concatenate([a[:, h * FFN_HALF:(h + 1) * FFN_HALF],
                                             a[:, FFN + h * FFN_HALF:FFN + (h + 1) * FFN_HALF]], axis=1) for h in range(2)]
        layers.append(dict(
            w_in=jnp.concatenate([wi[:, :2048], wi[:, 2056:3080], wi[:, 3080:5128], wi[:, 2048:2056],
                                  jnp.zeros((D, IN_COLS_PAD - 5128), BF16)], axis=1),
            cq=full["conv_qkv"][l],
            a_row=lane_row(w["a_log"][l], HEADS), dtb_row=lane_row(w["dt_bias"][l], HEADS),
            nw_row=w["dn_norm_w"][l].reshape(1, DK),
            lng=w["sg_ln_g"][l].reshape(1, SG_W), lnb=w["sg_ln_b"][l].reshape(1, SG_W),
            w_s=w["w_spatial"][l], bs_t=jnp.zeros((LANES, LANES), F32).at[:, :4].set(w["b_spatial"][l].T),
            wa=full["w_branch_a"][l], wb=full["w_branch_b"][l], wo=full["w_out"][l],
            g1=w["ln1_g"][l].reshape(1, D), b1=w["ln1_b"][l].reshape(1, D),
            wup=halves(wup), cf=halves(cf), wdn=[wdn[:FFN_HALF], wdn[FFN_HALF:]],
            g2=w["ln2_g"][l].reshape(1, D), b2=w["ln2_b"][l].reshape(1, D)))

    saved = []
    h_in = xs
    for l, p in enumerate(layers):
        proj = proj_fwd(h_in, p["w_in"], f"proj_fwd{l}")
        oa, sst = dn_fwd(proj, p["cq"], p["a_row"], p["dtb_row"], p["nw_row"], f"dn_fwd{l}")
        ob = sg_fwd(proj, p["lng"], p["lnb"], p["w_s"], p["bs_t"], f"sg_fwd{l}")
        x1 = merge_fwd(oa, ob, proj, h_in, p["wa"], p["wb"], p["wo"], p["g1"], p["b1"], f"merge_fwd{l}")
        pre2, x2 = ffn_fwd(x1, jnp.stack(p["wup"]), jnp.stack(p["cf"]), jnp.stack(p["wdn"]), p["g2"], p["b2"], f"ffn_fwd{l}")
        saved.append(dict(x=h_in, proj=proj, oa=oa, ob=ob, sst=sst, x1=x1, pre2=pre2))
        h_in = x2

    dy, loss_part = loss_call(h_in, tgt, "loss")
    loss = lax.psum(loss_part[0, 0], ("x", "y", "c"))

    grads = {n: [None] * DEPTH for n in names}
    for l in reversed(range(DEPTH)):
        p, a = layers[l], saved[l]
        dpre2, dg2, db2 = ln_bwd(a["pre2"], dy, p["g2"], p["b2"], f"ln2_bwd{l}")
        dx1, dwup0, dcf0, dwdn0 = ffn_bwd(a["x1"], dpre2, dpre2, ALPHA, p["wup"][0], p["cf"][0], p["wdn"][0], f"ffn_bwd{l}a")
        dx1, dwup1, dcf1, dwdn1 = ffn_bwd(a["x1"], dpre2, dx1, 1.0, p["wup"][1], p["cf"][1], p["wdn"][1], f"ffn_bwd{l}b")
        doa, dob, dga, dgb, dxd, dwa, dwb, dwo, dg1, db1 = merge_bwd(
            a["oa"], a["ob"], a["proj"], a["x"], dx1, p["wa"], p["wb"], p["wo"], p["g1"], p["b1"], f"merge_bwd{l}")
        duv, dlng, dlnb, dws, dbs = sg_bwd(a["proj"], dob, p["lng"], p["lnb"], p["w_s"], p["bs_t"], f"sg_bwd{l}")
        dqkv, dz, dba, dcq, da, ddtb, dnw = dn_bwd(a["proj"], a["sst"], doa, p["cq"], p["a_row"], p["dtb_row"],
                                                   p["nw_row"], f"dn_bwd{l}")
        dy, dwi = proj_bwd(a["x"], dqkv, dz, duv, dga, dgb, dba, dxd, p["w_in"], f"proj_bwd{l}")

        unhalve = lambda h0, h1: jnp.concatenate([h0[:, :FFN_HALF], h1[:, :FFN_HALF], h0[:, FFN_HALF:], h1[:, FFN_HALF:]], axis=1)
        grads["w_in"][l] = jnp.concatenate([dwi[:, :2048], dwi[:, C_BA:C_BA + 8], dwi[:, 2048:3072], dwi[:, 3072:5120]], axis=1)
        grads["conv_qkv"][l] = dcq
        grads["a_log"][l] = da[0, HEADS:2 * HEADS]
        grads["dt_bias"][l] = ddtb[0, HEADS:2 * HEADS]
        grads["dn_norm_w"][l] = dnw[0]
        grads["w_branch_a"][l] = dwa
        grads["sg_ln_g"][l] = dlng[0]
        grads["sg_ln_b"][l] = dlnb[0]
        grads["w_spatial"][l] = dws
        grads["b_spatial"][l] = dbs[:, :4].T
        grads["w_branch_b"][l] = dwb
        grads["w_out"][l] = dwo
        grads["ln1_g"][l] = dg1[0]
        grads["ln1_b"][l] = db1[0]
        grads["w_up"][l] = unhalve(dwup0, dwup1)
        grads["conv_ffn"][l] = unhalve(dcf0, dcf1)
        grads["w_down"][l] = jnp.concatenate([dwdn0, dwdn1], axis=0)
        grads["ln2_g"][l] = dg2[0]
        grads["ln2_b"][l] = db2[0]
    grad_x = dy.reshape(x.shape)
    g_full = {n: jnp.stack(grads[n]) for n in names}

    slots = []
    for k in range(N_CHIPS):
        pieces = [jnp.split(g_full[n], N_CHIPS, axis=ax)[k] for n, ax in BIG]
        slots.append(_pack(pieces, 32))
    rows = slots[0].shape[0]
    rh = rows // 2
    g2 = jnp.stack([sl.reshape(2, rh, LANES) for sl in slots], axis=1)
    core = lax.axis_index("c")
    theirs = pair_split(g2, "reduce_pair")
    pair = pair_add(g2.reshape(2, N_CHIPS * rh, LANES), theirs.reshape(N_CHIPS * rh, LANES),
                    jnp.stack([core]).astype(jnp.int32), "reduce_pair_add").reshape(N_CHIPS, rh, LANES)
    from_chips = scatter_chips(pair, "reduce_chips")
    halves = chips_add(pair, from_chips, jnp.stack([chip, core]).astype(jnp.int32), "reduce_chips_add")
    red = pair_join(halves, "reduce_join").reshape(rows * LANES)
    g_shard = dict(zip(big_names, _unpack(red, [w[n].shape for n in big_names])))

    small_names = conv_names + list(REPL)
    small = allsum_small(_pack([g_full[n] for n in small_names], 8), "reduce_small").reshape(-1)
    small_full = dict(zip(small_names, _unpack(small, [g_full[n].shape for n in small_names])))
    for n in conv_names:
        width = w[n].shape[2]
        g_shard[n] = lax.dynamic_slice_in_dim(small_full[n], chip * width, width, axis=2)
    for n in REPL:
        g_shard[n] = small_full[n]

    delta, new_m, new_v = {}, {}, {}
    for n in big_names:
        shp = w[n].shape
        two_d = (shp[0] * shp[1], shp[2])
        d_, m_, v_ = adam_call(w[n].reshape(two_d), g_shard[n].reshape(two_d), m[n].reshape(two_d), v[n].reshape(two_d), f"adam_{n}")
        delta[n], new_m[n], new_v[n] = d_.reshape(shp), m_.reshape(shp), v_.reshape(shp)
    shapes = [w[n].shape for n in small_names]
    packs = [_pack([src[n] for n in small_names], 8) for src in (w, g_shard, m, v)]
    outs = adam_call(*packs, "adam_small")
    for dst, o in zip((delta, new_m, new_v), outs):
        dst.update(zip(small_names, _unpack(o.reshape(-1), shapes)))

    return (loss, grad_x, *[g_shard[n] for n in names], *[delta[n] for n in names],
            *[new_m[n] for n in names], *[new_v[n] for n in names])
```

```python
import functools
import math

import jax
import jax.numpy as jnp
from jax import lax
from jax.experimental import pallas as pl
from jax.experimental.pallas import tpu as pltpu

F32 = jnp.float32
BF16 = jnp.bfloat16
HI = lax.Precision.HIGHEST
MID = lax.Precision.HIGH
MESH = pl.DeviceIdType.MESH

D = 1024
DEPTH = 2
HEADS = 4
DK = 128
CHUNK = 64
QKV_W = 1536
Z_W = 512
SG_W = 512
FFN = 2816
FFN_HALF = FFN // 2
N_CHIPS = 4
DN_SHARD = FFN // N_CHIPS
LN_EPS = 1e-5
RMS_EPS = 1e-6
L2_EPS = 1e-6
ALPHA = (2 * DEPTH) ** 0.25
ADAM_LR, ADAM_B1, ADAM_B2, ADAM_EPS, ADAM_WD, ADAM_STEP = 0.001, 0.9, 0.999, 1e-08, 0.01, 10

HALO = 16
LANES = 128
IN_COLS_PAD = 5248
C_Z, C_UV, C_GA, C_GB, C_BA = 1536, 2048, 3072, 4096, 5120
VMEM_LIMIT = 56 * 1024 * 1024


def _params(n_grid=1):
    return pltpu.CompilerParams(dimension_semantics=("arbitrary",) * n_grid, vmem_limit_bytes=VMEM_LIMIT)


def _mm(a, b):
    return jnp.dot(a.astype(BF16), b.astype(BF16), preferred_element_type=F32)


def _mm_nt(a, b):
    return lax.dot_general(a.astype(BF16), b.astype(BF16), (((1,), (1,)), ((), ())), preferred_element_type=F32)


def _mm_tn(a, b):
    return lax.dot_general(a.astype(BF16), b.astype(BF16), (((0,), (0,)), ((), ())), preferred_element_type=F32)


def _bdot(a, b, prec=MID):
    return lax.dot_general(a, b, (((2,), (1,)), ((0,), (0,))), precision=prec, preferred_element_type=F32)


def _bdot_nt(a, b, prec=MID):
    return lax.dot_general(a, b, (((2,), (2,)), ((0,), (0,))), precision=prec, preferred_element_type=F32)


def _bf16_dot(a, b, contract):
    return lax.dot_general(a.astype(BF16), b.astype(BF16), (contract, ((0,), (0,))), preferred_element_type=F32)


@jax.custom_vjp
def _fdot(a, b):
    return _bf16_dot(a, b, ((2,), (1,)))


def _fdot_fwd(a, b):
    return _fdot(a, b), (a, b)


def _fdot_bwd(res, ct):
    a, b = res
    return _bf16_dot(ct, b, ((2,), (2,))), _bf16_dot(a, ct, ((1,), (1,)))


_fdot.defvjp(_fdot_fwd, _fdot_bwd)


@jax.custom_vjp
def _fdot_nt(a, b):
    return _bf16_dot(a, b, ((2,), (2,)))


def _fdot_nt_fwd(a, b):
    return _fdot_nt(a, b), (a, b)


def _fdot_nt_bwd(res, ct):
    a, b = res
    return _bf16_dot(ct, b, ((2,), (1,))), _bf16_dot(ct, a, ((1,), (1,)))


_fdot_nt.defvjp(_fdot_nt_fwd, _fdot_nt_bwd)


@jax.custom_vjp
def _fdot_tn(a, b):
    return _bf16_dot(a, b, ((1,), (1,)))


def _fdot_tn_fwd(a, b):
    return _fdot_tn(a, b), (a, b)


def _fdot_tn_bwd(res, ct):
    a, b = res
    return _bf16_dot(b, ct, ((2,), (2,))), _bf16_dot(a, ct, ((2,), (1,)))


_fdot_tn.defvjp(_fdot_tn_fwd, _fdot_tn_bwd)


def _ln(x, g, b):
    mu = jnp.mean(x, axis=-1, keepdims=True)
    xc = x - mu
    var = jnp.mean(xc * xc, axis=-1, keepdims=True)
    return xc * lax.rsqrt(var + LN_EPS) * g + b


def _shift_rows(x, s):
    s = s % x.shape[0]
    return x if s == 0 else pltpu.roll(x, s, 0)


@jax.custom_vjp
def _conv(xcat, w):
    k_taps = len(w)
    y = None
    for k in range(k_taps):
        t = _shift_rows(xcat, k_taps - 1 - k)[HALO:] * w[k]
        y = t if y is None else y + t
    return y


def _conv_fwd(xcat, w):
    return _conv(xcat, w), (xcat, w)


def _conv_bwd(res, dy):
    xcat, w = res
    k_taps = len(w)
    dyp = jnp.concatenate([jnp.zeros((HALO, dy.shape[1]), dy.dtype), dy], axis=0)
    dx = None
    dws = []
    for k in range(k_taps):
        s = k_taps - 1 - k
        t = _shift_rows(dyp, -s) * w[k]
        dx = t if dx is None else dx + t
        dws.append(jnp.sum(_shift_rows(xcat, s)[HALO:] * dy, axis=0, keepdims=True))
    return dx, tuple(dws)


_conv.defvjp(_conv_fwd, _conv_bwd)


@jax.custom_vjp
def _tri_inv(l):
    n = l.shape[-1]
    r = lax.broadcasted_iota(jnp.int32, (n, n), 0)
    c = lax.broadcasted_iota(jnp.int32, (n, n), 1)
    eye = (r == c).astype(F32)
    p = eye - l
    lp = l
    steps = int(math.log2(n)) - 1
    for i in range(steps):
        dot = _bdot if i < 2 else functools.partial(_bf16_dot, contract=((2,), (1,)))
        lp = dot(lp, lp)
        p = p + dot(p, lp)
    return p


def _tri_inv_fwd(l):
    t = _tri_inv(l)
    return t, t


def _tri_inv_bwd(t, dt):
    tt = jnp.swapaxes(t, 1, 2)
    return (-_bdot(tt, _bdot(dt, tt)),)


_tri_inv.defvjp(_tri_inv_fwd, _tri_inv_bwd)


def _dn_glue(qkvcat, z, ba, s_in, cw, a_row, dtb_row, nw_row):
    t_rows = z.shape[0]
    nc = t_rows // CHUNK
    nb = nc * HEADS

    qkv = jax.nn.silu(_conv(qkvcat, cw))

    def chunks(t, off):
        return jnp.stack([t[n * CHUNK:(n + 1) * CHUNK, off + h * DK: off + (h + 1) * DK]
                          for n in range(nc) for h in range(HEADS)])

    q = chunks(qkv, 0)
    k = chunks(qkv, 512)
    v = chunks(qkv, 1024)
    q = q * lax.rsqrt(jnp.sum(q * q, axis=-1, keepdims=True) + L2_EPS) * (DK ** -0.5)
    k = k * lax.rsqrt(jnp.sum(k * k, axis=-1, keepdims=True) + L2_EPS)

    lane = lax.broadcasted_iota(jnp.int32, (LANES, HEADS * DK), 0)
    head_of_col = lax.broadcasted_iota(jnp.int32, (LANES, HEADS * DK), 1) // DK
    e_beta = (head_of_col == lane).astype(F32)
    e_g = (head_of_col + HEADS == lane).astype(F32)
    beta_l = jax.nn.sigmoid(ba)
    g_l = -jnp.exp(a_row) * jax.nn.softplus(ba + dtb_row)
    beta = chunks(jnp.dot(beta_l, e_beta, precision=HI, preferred_element_type=F32), 0)
    g = chunks(jnp.dot(g_l, e_g, precision=HI, preferred_element_type=F32), 0)

    r = lax.broadcasted_iota(jnp.int32, (CHUNK, CHUNK), 0)
    c = lax.broadcasted_iota(jnp.int32, (CHUNK, CHUNK), 1)
    causal = r >= c
    strict = r > c
    tril_b = jnp.broadcast_to(causal.astype(F32), (nb, CHUNK, CHUNK))
    gi_b = _bdot(tril_b, g, HI)
    gi = gi_b[:, :, :CHUNK]
    gj = jnp.swapaxes(gi, 1, 2)
    decay = jnp.where(causal, jnp.exp(jnp.where(causal, gi - gj, 0.0)), 0.0)
    kb = k * beta
    l_mat = jnp.where(strict, _fdot_nt(kb, k) * decay, 0.0)
    t_mat = _tri_inv(l_mat)
    e_gi = jnp.exp(gi_b)
    w_mat = _fdot(t_mat, kb * e_gi)
    u_mat = _fdot(t_mat, v * beta)
    a_qk = _fdot_nt(q, k) * decay
    q_g = q * e_gi
    gl_b = jnp.broadcast_to(jnp.sum(g, axis=1, keepdims=True), g.shape)
    k_d = k * jnp.exp(gl_b - gi_b)
    e_gl = jnp.exp(gl_b)
    g_last = jnp.concatenate([e_gl, e_gl], axis=1)

    state = s_in
    rows = []
    for n in range(nc):
        sl = slice(n * HEADS, (n + 1) * HEADS)
        u_new = u_mat[sl] - _fdot(w_mat[sl], state)
        o_n = _fdot(q_g[sl], state) + _fdot(a_qk[sl], u_new)
        state = state * g_last[sl] + _fdot_tn(k_d[sl], u_new)
        o_n = o_n * lax.rsqrt(jnp.mean(o_n * o_n, axis=-1, keepdims=True) + RMS_EPS) * nw_row
        z_n = jnp.stack([z[n * CHUNK:(n + 1) * CHUNK, h * DK:(h + 1) * DK] for h in range(HEADS)])
        o_n = o_n * jax.nn.silu(z_n)
        rows.append(jnp.concatenate([o_n[h] for h in range(HEADS)], axis=-1))
    return jnp.concatenate(rows, axis=0), state


def _sg_glue(uv, lng, lnb, w_s, bs_t):
    t_rows = uv.shape[0]
    y = jax.nn.gelu(uv)
    u = y[:, :SG_W]
    v = _ln(y[:, SG_W:], lng, lnb)
    r = lax.broadcasted_iota(jnp.int32, (LANES, LANES), 0)
    c = lax.broadcasted_iota(jnp.int32, (LANES, LANES), 1)
    wm = jnp.where(r >= c, w_s, 0.0)
    lane = lax.broadcasted_iota(jnp.int32, (LANES, SG_W), 0)
    group_of_col = lax.broadcasted_iota(jnp.int32, (LANES, SG_W), 1) // LANES
    e_grp = (group_of_col == lane).astype(F32)
    bias = jnp.dot(bs_t, e_grp, precision=HI, preferred_element_type=F32)
    outs = []
    for n in range(t_rows // LANES):
        vb = v[n * LANES:(n + 1) * LANES]
        vg = jnp.stack([vb[:, g * LANES:(g + 1) * LANES] for g in range(4)])
        mg = _fdot(wm, vg)
        mixed = jnp.concatenate([mg[g] for g in range(4)], axis=-1) + bias
        outs.append(u[n * LANES:(n + 1) * LANES] * mixed)
    return jnp.concatenate(outs, axis=0)


def _merge_glue(ga, gb, ya, yb):
    return jax.nn.sigmoid(ga) * ya + jax.nn.sigmoid(gb) * yb


def _res_ln_glue(x, r, g, b):
    return _ln(ALPHA * x + r, g, b)


def _ffn_glue(ua, ub, cwa, cwb):
    return jax.nn.silu(_conv(ua, cwa)) * _conv(ub, cwb)


def _row(t, c, col=0):
    return pl.BlockSpec((t, c), lambda i: (i, col))


def _row_rev(t, c, nt, col=0):
    return pl.BlockSpec((t, c), lambda i: (nt - 1 - i, col))


def _halo(t, c, nt=None):
    per = t // HALO
    if nt is None:
        return pl.BlockSpec((HALO, c), lambda i: (jnp.maximum(i * per - 1, 0), 0))
    return pl.BlockSpec((HALO, c), lambda i: (jnp.maximum((nt - 1 - i) * per - 1, 0), 0))


def _full(shape):
    nd = len(shape)
    return pl.BlockSpec(shape, lambda i: (0,) * nd)


ANY = pl.BlockSpec(memory_space=pl.ANY)


def _sds(shape, dtype=F32):
    return jax.ShapeDtypeStruct(shape, dtype)


def _tile(s):
    return 256 if s % 256 == 0 else 128


def proj_fwd(x, w, name):
    s = x.shape[0]
    t = _tile(s)
    segs = [(0, 2048), (2048, 3072), (3072, 4096), (4096, 5120), (5120, IN_COLS_PAD)]

    def body(x_ref, w_ref, p_ref):
        xb = x_ref[...].astype(BF16)
        for lo, hi in segs:
            p_ref[:, lo:hi] = jnp.dot(xb, w_ref[:, lo:hi], preferred_element_type=F32)

    return pl.pallas_call(
        body, grid=(s // t,), name=name,
        in_specs=[_row(t, D), _full((D, IN_COLS_PAD))],
        out_specs=_row(t, IN_COLS_PAD),
        out_shape=_sds((s, IN_COLS_PAD)), compiler_params=_params())(x, w)


def dn_fwd(p, cq, a_row, dtb_row, nw_row, name):
    s = p.shape[0]
    t = _tile(s)
    nt = s // t

    def body(qkv_ref, halo_ref, z_ref, ba_ref, cq_ref, a_ref, dtb_ref, nw_ref, o_ref, sst_ref, s_scr):
        i = pl.program_id(0)

        @pl.when(i == 0)
        def _():
            s_scr[...] = jnp.zeros_like(s_scr)

        halo = jnp.where(i == 0, 0.0, halo_ref[...])
        qkvcat = jnp.concatenate([halo, qkv_ref[...]], axis=0)
        cw = tuple(cq_ref[k:k + 1, :] for k in range(4))
        s_in = s_scr[...]
        sst_ref[0] = s_in
        o, s_out = _dn_glue(qkvcat, z_ref[...], ba_ref[...], s_in, cw, a_ref[...], dtb_ref[...], nw_ref[...])
        o_ref[...] = o.astype(BF16)
        s_scr[...] = s_out

    return pl.pallas_call(
        body, grid=(nt,), name=name,
        in_specs=[_row(t, QKV_W), _halo(t, QKV_W), _row(t, Z_W, C_Z // Z_W), _row(t, LANES, C_BA // LANES),
                  _full((4, QKV_W)), _full((1, LANES)), _full((1, LANES)), _full((1, LANES))],
        out_specs=[_row(t, Z_W), pl.BlockSpec((1, HEADS, DK, DK), lambda i: (i, 0, 0, 0))],
        out_shape=[_sds((s, Z_W), BF16), _sds((nt, HEADS, DK, DK))],
        scratch_shapes=[pltpu.VMEM((HEADS, DK, DK), F32)],
        compiler_params=_params())(p, p, p, p, cq, a_row, dtb_row, nw_row)


def sg_fwd(p, lng, lnb, w_s, bs_t, name):
    s = p.shape[0]
    t = _tile(s)

    def body(uv_ref, lng_ref, lnb_ref, ws_ref, bs_ref, o_ref):
        o_ref[...] = _sg_glue(uv_ref[...], lng_ref[...], lnb_ref[...], ws_ref[...], bs_ref[...]).astype(BF16)

    return pl.pallas_call(
        body, grid=(s // t,), name=name,
        in_specs=[_row(t, 2 * SG_W, C_UV // (2 * SG_W)), _full((1, SG_W)), _full((1, SG_W)),
                  _full((4, LANES, LANES)), _full((LANES, LANES))],
        out_specs=_row(t, SG_W), out_shape=_sds((s, SG_W), BF16), compiler_params=_params())(p, lng, lnb, w_s, bs_t)


def merge_fwd(oa, ob, p, x, wa, wb, wo, g1, b1, name):
    s = x.shape[0]
    t = _tile(s)

    def body(oa_ref, ob_ref, ga_ref, gb_ref, x_ref, wa_ref, wb_ref, wo_ref, g_ref, b_ref, x1_ref):
        ya = _mm(oa_ref[...], wa_ref[...])
        yb = _mm(ob_ref[...], wb_ref[...])
        h = _merge_glue(ga_ref[...], gb_ref[...], ya, yb)
        x1_ref[...] = _res_ln_glue(x_ref[...], _mm(h, wo_ref[...]), g_ref[...], b_ref[...])

    return pl.pallas_call(
        body, grid=(s // t,), name=name,
        in_specs=[_row(t, Z_W), _row(t, SG_W), _row(t, D, C_GA // D), _row(t, D, C_GB // D), _row(t, D),
                  _full((Z_W, D)), _full((SG_W, D)), _full((D, D)), _full((1, D)), _full((1, D))],
        out_specs=_row(t, D), out_shape=_sds((s, D)), compiler_params=_params())(oa, ob, p, p, x, wa, wb, wo, g1, b1)


def _load_ffn_weights(wup_hbm, wdn_hbm, wup_v, wdn_v, layer, up_slots, dn_slots):
    for n, k in enumerate(up_slots):
        pltpu.sync_copy(wup_hbm.at[k, layer], wup_v.at[n])
    for n, k in enumerate(dn_slots):
        pltpu.sync_copy(wdn_hbm.at[k, layer], wdn_v.at[pl.ds(n * DN_SHARD, DN_SHARD)])


def ffn_fwd(x1, wup4, cf4, wdn4, layer, g2, b2, name):
    s = x1.shape[0]
    t = _tile(s)

    def body(x1_ref, halo_ref, wup_hbm, cf_ref, wdn_hbm, g_ref, b_ref, pre_ref, x2_ref, wup_v, wdn_v):
        i = pl.program_id(0)

        @pl.when(i == 0)
        def _():
            _load_ffn_weights(wup_hbm, wdn_hbm, wup_v, wdn_v, layer, range(4), range(4))

        x1v = x1_ref[...]
        halo = jnp.where(i == 0, 0.0, halo_ref[...])
        x1cat = jnp.concatenate([halo, x1v], axis=0).astype(BF16)
        f = None
        for h in range(2):
            ua = jnp.dot(x1cat, wup_v[h], preferred_element_type=F32)
            ub = jnp.dot(x1cat, wup_v[2 + h], preferred_element_type=F32)
            cwa = tuple(cf_ref[h, k:k + 1, :] for k in range(3))
            cwb = tuple(cf_ref[2 + h, k:k + 1, :] for k in range(3))
            act = _ffn_glue(ua, ub, cwa, cwb)
            fh = _mm(act, wdn_v[h * FFN_HALF:(h + 1) * FFN_HALF, :])
            f = fh if f is None else f + fh
        pre = ALPHA * x1v + f
        pre_ref[...] = pre
        x2_ref[...] = _ln(pre, g_ref[...], b_ref[...])

    return pl.pallas_call(
        body, grid=(s // t,), name=name,
        in_specs=[_row(t, D), _halo(t, D), ANY, _full((4, 3, FFN_HALF)), ANY, _full((1, D)), _full((1, D))],
        out_specs=[_row(t, D), _row(t, D)], out_shape=[_sds((s, D)), _sds((s, D))],
        scratch_shapes=[pltpu.VMEM((4, D, FFN_HALF), BF16), pltpu.VMEM((FFN, D), BF16)],
        compiler_params=_params())(x1, x1, wup4, cf4, wdn4, g2, b2)


def loss_call(y, tgt, name):
    s = y.shape[0]
    t = _tile(s)

    def body(y_ref, t_ref, dy_ref, loss_ref):
        @pl.when(pl.program_id(0) == 0)
        def _():
            loss_ref[...] = jnp.zeros_like(loss_ref)

        e = y_ref[...] - t_ref[...]
        dy_ref[...] = e * (1.0 / D)
        part = jnp.sum(jnp.sum(e * e, axis=1, keepdims=True), axis=0, keepdims=True) * (0.5 / D)
        loss_ref[...] += jnp.broadcast_to(part, loss_ref.shape)

    return pl.pallas_call(
        body, grid=(s // t,), name=name, in_specs=[_row(t, D), _row(t, D)],
        out_specs=[_row(t, D), _full((8, LANES))], out_shape=[_sds((s, D)), _sds((8, LANES))],
        compiler_params=_params())(y, tgt)


def _acc(ref, val, first):
    @pl.when(first)
    def _():
        ref[...] = val

    @pl.when(jnp.logical_not(first))
    def _():
        ref[...] += val


def _acc_tn(acc_ref, a, b, first, seg):
    n = b.shape[1]
    for lo in range(0, n, seg):
        hi = min(lo + seg, n)
        _acc(acc_ref.at[:, lo:hi], _mm_tn(a, b[:, lo:hi]), first)


def ln_bwd(pre, dy, g, b, name):
    s = pre.shape[0]
    t = _tile(s)

    def body(pre_ref, dy_ref, g_ref, b_ref, dpre_ref, dg_ref, db_ref):
        _, vjp = jax.vjp(_ln, pre_ref[...], g_ref[...], b_ref[...])
        dpre, dg, db = vjp(dy_ref[...])
        dpre_ref[...] = dpre
        first = pl.program_id(0) == 0
        _acc(dg_ref, dg, first)
        _acc(db_ref, db, first)

    return pl.pallas_call(
        body, grid=(s // t,), name=name, in_specs=[_row(t, D), _row(t, D), _full((1, D)), _full((1, D))],
        out_specs=[_row(t, D), _full((1, D)), _full((1, D))],
        out_shape=[_sds((s, D)), _sds((1, D)), _sds((1, D))], compiler_params=_params())(pre, dy, g, b)


def ffn_bwd(x1, df, acc_in, acc_scale, wup4, cf4, wdn4, layer, h, name):
    s = x1.shape[0]
    t = _tile(s)
    nt = s // t

    def body(x1_ref, halo_ref, df_ref, acc_ref, wup_hbm, cf_ref, wdn_hbm,
             dx1_ref, dwup_hbm, dcf_ref, dwdn_hbm, wup_v, wdn_v, dwup_v, dwdn_v, carry):
        i = pl.program_id(0)
        j = nt - 1 - i
        first = i == 0

        @pl.when(first)
        def _():
            _load_ffn_weights(wup_hbm, wdn_hbm, wup_v, wdn_v, layer, (h, 2 + h), (2 * h, 2 * h + 1))
            carry[...] = jnp.zeros_like(carry)

        halo = jnp.where(j == 0, 0.0, halo_ref[...])
        x1cat = jnp.concatenate([halo, x1_ref[...]], axis=0).astype(BF16)
        ua = jnp.dot(x1cat, wup_v[0], preferred_element_type=F32)
        ub = jnp.dot(x1cat, wup_v[1], preferred_element_type=F32)
        cwa = tuple(cf_ref[h, k:k + 1, :] for k in range(3))
        cwb = tuple(cf_ref[2 + h, k:k + 1, :] for k in range(3))
        act, vjp = jax.vjp(_ffn_glue, ua, ub, cwa, cwb)
        dfb = df_ref[...].astype(BF16)
        dact = _mm_nt(dfb, wdn_v[...])
        _acc_tn(dwdn_v, act.astype(BF16), dfb, first, 512)
        dua, dub, dcwa, dcwb = vjp(dact)
        duab = dua.astype(BF16)
        dubb = dub.astype(BF16)
        dx1cat = _mm_nt(duab, wup_v[0]) + _mm_nt(dubb, wup_v[1])
        _acc(dwup_v.at[0], _mm_tn(x1cat, duab), first)
        _acc(dwup_v.at[1], _mm_tn(x1cat, dubb), first)
        for k in range(3):
            _acc(dcf_ref.at[0, k:k + 1, :], dcwa[k], first)
            _acc(dcf_ref.at[1, k:k + 1, :], dcwb[k], first)
        out = acc_scale * acc_ref[...] + dx1cat[HALO:]
        dx1_ref[...] = out
        dx1_ref[t - HALO:t, :] = out[t - HALO:] + carry[...]
        carry[...] = dx1cat[:HALO]

        @pl.when(i == nt - 1)
        def _():
            pltpu.sync_copy(dwup_v, dwup_hbm)
            pltpu.sync_copy(dwdn_v, dwdn_hbm)

    return pl.pallas_call(
        body, grid=(nt,), name=name,
        in_specs=[_row_rev(t, D, nt), _halo(t, D, nt), _row_rev(t, D, nt), _row_rev(t, D, nt),
                  ANY, _full((4, 3, FFN_HALF)), ANY],
        out_specs=[_row_rev(t, D, nt), ANY, _full((2, 3, FFN_HALF)), ANY],
        out_shape=[_sds((s, D)), _sds((2, D, FFN_HALF)), _sds((2, 3, FFN_HALF)), _sds((FFN_HALF, D))],
        scratch_shapes=[pltpu.VMEM((2, D, FFN_HALF), BF16), pltpu.VMEM((FFN_HALF, D), BF16),
                        pltpu.VMEM((2, D, FFN_HALF), F32), pltpu.VMEM((FFN_HALF, D), F32), pltpu.VMEM((HALO, D), F32)],
        compiler_params=_params())(x1, x1, df, acc_in, wup4, cf4, wdn4)


def merge_bwd(oa, ob, p, x, dx1, wa, wb, wo, g1, b1, name):
    s = x.shape[0]
    t = _tile(s)

    def body(oa_ref, ob_ref, ga_ref, gb_ref, x_ref, dx1_ref, wa_ref, wb_ref, wo_ref, g_ref, b_ref,
             doa_ref, dob_ref, dga_ref, dgb_ref, dx_ref, dwa_ref, dwb_ref, dwo_ref, dg_ref, db_ref):
        first = pl.program_id(0) == 0
        oa = oa_ref[...]
        ob = ob_ref[...]
        ya = _mm(oa, wa_ref[...])
        yb = _mm(ob, wb_ref[...])
        h, vjp1 = jax.vjp(_merge_glue, ga_ref[...], gb_ref[...], ya, yb)
        hb = h.astype(BF16)
        r = _mm(hb, wo_ref[...])
        _, vjp2 = jax.vjp(_res_ln_glue, x_ref[...], r, g_ref[...], b_ref[...])
        dx, dr, dg, db = vjp2(dx1_ref[...])
        dx_ref[...] = dx
        _acc(dg_ref, dg, first)
        _acc(db_ref, db, first)
        drb = dr.astype(BF16)
        dh = _mm_nt(drb, wo_ref[...])
        _acc(dwo_ref, _mm_tn(hb, drb), first)
        dga, dgb, dya, dyb = vjp1(dh)
        dga_ref[...] = dga.astype(BF16)
        dgb_ref[...] = dgb.astype(BF16)
        dyab = dya.astype(BF16)
        dybb = dyb.astype(BF16)
        doa_ref[...] = _mm_nt(dyab, wa_ref[...]).astype(BF16)
        dob_ref[...] = _mm_nt(dybb, wb_ref[...]).astype(BF16)
        _acc(dwa_ref, _mm_tn(oa, dyab), first)
        _acc(dwb_ref, _mm_tn(ob, dybb), first)

    return pl.pallas_call(
        body, grid=(s // t,), name=name,
        in_specs=[_row(t, Z_W), _row(t, SG_W), _row(t, D, C_GA // D), _row(t, D, C_GB // D), _row(t, D), _row(t, D),
                  _full((Z_W, D)), _full((SG_W, D)), _full((D, D)), _full((1, D)), _full((1, D))],
        out_specs=[_row(t, Z_W), _row(t, SG_W), _row(t, D), _row(t, D), _row(t, D),
                   _full((Z_W, D)), _full((SG_W, D)), _full((D, D)), _full((1, D)), _full((1, D))],
        out_shape=[_sds((s, Z_W), BF16), _sds((s, SG_W), BF16), _sds((s, D), BF16), _sds((s, D), BF16), _sds((s, D)),
                   _sds((Z_W, D)), _sds((SG_W, D)), _sds((D, D)), _sds((1, D)), _sds((1, D))],
        compiler_params=_params())(oa, ob, p, p, x, dx1, wa, wb, wo, g1, b1)


def sg_bwd(p, dob, lng, lnb, w_s, bs_t, name):
    s = p.shape[0]
    t = _tile(s)

    def body(uv_ref, dob_ref, lng_ref, lnb_ref, ws_ref, bs_ref, duv_ref, dlng_ref, dlnb_ref, dws_ref, dbs_ref):
        first = pl.program_id(0) == 0
        _, vjp = jax.vjp(_sg_glue, uv_ref[...], lng_ref[...], lnb_ref[...], ws_ref[...], bs_ref[...])
        duv, dlng, dlnb, dws, dbs = vjp(dob_ref[...].astype(F32))
        duv_ref[...] = duv.astype(BF16)
        _acc(dlng_ref, dlng, first)
        _acc(dlnb_ref, dlnb, first)
        _acc(dws_ref, dws, first)
        _acc(dbs_ref, dbs, first)

    return pl.pallas_call(
        body, grid=(s // t,), name=name,
        in_specs=[_row(t, 2 * SG_W, C_UV // (2 * SG_W)), _row(t, SG_W), _full((1, SG_W)), _full((1, SG_W)),
                  _full((4, LANES, LANES)), _full((LANES, LANES))],
        out_specs=[_row(t, 2 * SG_W), _full((1, SG_W)), _full((1, SG_W)), _full((4, LANES, LANES)), _full((LANES, LANES))],
        out_shape=[_sds((s, 2 * SG_W), BF16), _sds((1, SG_W)), _sds((1, SG_W)), _sds((4, LANES, LANES)), _sds((LANES, LANES))],
        compiler_params=_params())(p, dob, lng, lnb, w_s, bs_t)


def dn_bwd(p, sst, doa, cq, a_row, dtb_row, nw_row, name):
    s = p.shape[0]
    t = _tile(s)
    nt = s // t

    def body(qkv_ref, halo_ref, z_ref, ba_ref, sst_ref, doa_ref, cq_ref, a_ref, dtb_ref, nw_ref,
             dqkv_ref, dz_ref, dba_ref, dcq_ref, da_ref, ddtb_ref, dnw_ref, ds_scr, carry):
        i = pl.program_id(0)
        j = nt - 1 - i
        first = i == 0

        @pl.when(first)
        def _():
            ds_scr[...] = jnp.zeros_like(ds_scr)
            carry[...] = jnp.zeros_like(carry)

        halo = jnp.where(j == 0, 0.0, halo_ref[...])
        qkvcat = jnp.concatenate([halo, qkv_ref[...]], axis=0)
        cw = tuple(cq_ref[k:k + 1, :] for k in range(4))
        _, vjp = jax.vjp(_dn_glue, qkvcat, z_ref[...], ba_ref[...], sst_ref[0], cw, a_ref[...], dtb_ref[...], nw_ref[...])
        dqkvcat, dz, dba, ds_in, dcw, da, ddtb, dnw = vjp((doa_ref[...].astype(F32), ds_scr[...]))
        ds_scr[...] = ds_in
        dz_ref[...] = dz.astype(BF16)
        dba_ref[...] = dba.astype(BF16)
        dtile = dqkvcat[HALO:]
        dqkv_ref[...] = dtile.astype(BF16)
        dqkv_ref[t - HALO:t, :] = (dtile[t - HALO:] + carry[...]).astype(BF16)
        carry[...] = dqkvcat[:HALO]
        for k in range(4):
            _acc(dcq_ref.at[k:k + 1, :], dcw[k], first)
        _acc(da_ref, da, first)
        _acc(ddtb_ref, ddtb, first)
        _acc(dnw_ref, dnw, first)

    return pl.pallas_call(
        body, grid=(nt,), name=name,
        in_specs=[_row_rev(t, QKV_W, nt), _halo(t, QKV_W, nt), _row_rev(t, Z_W, nt, C_Z // Z_W),
                  _row_rev(t, LANES, nt, C_BA // LANES),
                  pl.BlockSpec((1, HEADS, DK, DK), lambda i: (nt - 1 - i, 0, 0, 0)), _row_rev(t, Z_W, nt),
                  _full((4, QKV_W)), _full((1, LANES)), _full((1, LANES)), _full((1, LANES))],
        out_specs=[_row_rev(t, QKV_W, nt), _row_rev(t, Z_W, nt), _row_rev(t, LANES, nt),
                   _full((4, QKV_W)), _full((1, LANES)), _full((1, LANES)), _full((1, LANES))],
        out_shape=[_sds((s, QKV_W), BF16), _sds((s, Z_W), BF16), _sds((s, LANES), BF16),
                   _sds((4, QKV_W)), _sds((1, LANES)), _sds((1, LANES)), _sds((1, LANES))],
        scratch_shapes=[pltpu.VMEM((HEADS, DK, DK), F32), pltpu.VMEM((HALO, QKV_W), F32)],
        compiler_params=_params())(p, p, p, p, sst, doa, cq, a_row, dtb_row, nw_row)


def proj_bwd(x, dqkv, dz, duv, dga, dgb, dba, dxd, w, name):
    s = x.shape[0]
    t = _tile(s)
    nt = s // t

    def body(x_ref, dqkv_ref, dz_ref, duv_ref, dga_ref, dgb_ref, dba_ref, dxd_ref, w_hbm,
             dx_ref, dw_hbm, w_v, dw_v):
        i = pl.program_id(0)
        first = i == 0

        @pl.when(first)
        def _():
            pltpu.sync_copy(w_hbm, w_v)

        dp = jnp.concatenate([dqkv_ref[...], dz_ref[...], duv_ref[...], dga_ref[...], dgb_ref[...], dba_ref[...]], axis=1)
        dx_ref[...] = dxd_ref[...] + _mm_nt(dp, w_v[...])
        _acc_tn(dw_v, x_ref[...].astype(BF16), dp, first, 1024)

        @pl.when(i == nt - 1)
        def _():
            pltpu.sync_copy(dw_v, dw_hbm)

    return pl.pallas_call(
        body, grid=(nt,), name=name,
        in_specs=[_row(t, D), _row(t, QKV_W), _row(t, Z_W), _row(t, 2 * SG_W), _row(t, D), _row(t, D), _row(t, LANES),
                  _row(t, D), ANY],
        out_specs=[_row(t, D), ANY], out_shape=[_sds((s, D)), _sds((D, IN_COLS_PAD))],
        scratch_shapes=[pltpu.VMEM((D, IN_COLS_PAD), BF16), pltpu.VMEM((D, IN_COLS_PAD), F32)],
        compiler_params=_params())(x, dqkv, dz, duv, dga, dgb, dba, dxd, w)


def _rows_block(rows, cols):
    cap = max(8, (2 * 1024 * 1024) // (cols * 4))
    for cand in range(min(rows, cap) // 8 * 8, 7, -8):
        if rows % cand == 0:
            return cand
    return rows


def adam_call(w, g, m, v, name):
    rows, cols = w.shape
    tr = _rows_block(rows, cols)
    c1 = 1.0 - ADAM_B1 ** ADAM_STEP
    c2 = 1.0 - ADAM_B2 ** ADAM_STEP

    def body(w_ref, g_ref, m_ref, v_ref, d_ref, nm_ref, nv_ref):
        gv = g_ref[...]
        nm = ADAM_B1 * m_ref[...] + (1.0 - ADAM_B1) * gv
        nv = ADAM_B2 * v_ref[...] + (1.0 - ADAM_B2) * (gv * gv)
        d_ref[...] = -ADAM_LR * ((nm / c1) / (jnp.sqrt(nv / c2) + ADAM_EPS) + ADAM_WD * w_ref[...])
        nm_ref[...] = nm
        nv_ref[...] = nv

    spec = pl.BlockSpec((tr, cols), lambda i: (i, 0))
    return pl.pallas_call(
        body, grid=(rows // tr,), name=name, in_specs=[spec] * 4, out_specs=[spec] * 3,
        out_shape=[_sds((rows, cols))] * 3, compiler_params=_params())(w, g, m, v)


def _place():
    return lax.axis_index("x"), lax.axis_index("y"), lax.axis_index("c")


def _other_chips(x, y):
    return [(1 - x, y), (x, 1 - y), (1 - x, 1 - y)]


def _remote(src, dst, send_sem, recv_sem, to):
    return pltpu.make_async_remote_copy(src_ref=src, dst_ref=dst, send_sem=send_sem, recv_sem=recv_sem,
                                        device_id=to, device_id_type=MESH)


def gather_weights(shards, name):
    n = len(shards)

    def body(*refs):
        ins, outs = refs[:n], refs[n:2 * n]
        send_sems, recv_sems, local_sems = refs[2 * n:]
        x, y, c = _place()
        me = 2 * x + y
        sibling = (x, y, 1 - c)
        chips = _other_chips(x, y)

        def copy(t, k, slot, layer, to, src=None):
            dst = outs[t].at[slot, layer]
            return _remote(dst if src is None else src, dst, send_sems.at[6 * t + k], recv_sems.at[6 * t + k], to)

        mine = [pltpu.make_async_copy(ins[t].at[l], outs[t].at[me, l], local_sems.at[2 * t + l])
                for t in range(n) for l in range(2)]
        for cp in mine:
            cp.start()
        first = [copy(t, k, me, c, (cx, cy, c), src=ins[t].at[c]) for k, (cx, cy) in enumerate(chips) for t in range(n)]
        for cp in first:
            cp.start()
        passed = []
        for k, (cx, cy) in enumerate(chips):
            for t in range(n):
                copy(t, k, 2 * cx + cy, c, (x, y, c)).wait_recv()
                passed.append(copy(t, 3 + k, 2 * cx + cy, c, sibling))
                passed[-1].start()
        for k, (cx, cy) in enumerate(chips):
            for t in range(n):
                copy(t, 3 + k, 2 * cx + cy, 1 - c, (x, y, c)).wait_recv()
        for cp in first + passed:
            cp.wait_send()
        for cp in mine:
            cp.wait()

    return pl.pallas_call(
        body, name=name, in_specs=[ANY] * n, out_specs=[ANY] * n,
        out_shape=[_sds((N_CHIPS,) + a.shape, a.dtype) for a in shards],
        scratch_shapes=[pltpu.SemaphoreType.DMA((6 * n,)), pltpu.SemaphoreType.DMA((6 * n,)),
                        pltpu.SemaphoreType.DMA((2 * n,))],
    )(*shards)


def pair_exchange(layer0, layer1, name):
    n = len(layer0)

    def body(*refs):
        srcs = (refs[:n], refs[n:2 * n])
        outs = refs[2 * n:3 * n]
        send_sems, recv_sems = refs[3 * n:]
        x, y, c = _place()
        for core in range(2):
            @pl.when(c == core)
            def _(core=core):
                cps = [_remote(srcs[1 - core][t], outs[t], send_sems.at[t], recv_sems.at[t], (x, y, 1 - c))
                       for t in range(n)]
                for cp in cps:
                    cp.start()
                for cp in cps:
                    cp.wait()

    return pl.pallas_call(
        body, name=name, in_specs=[ANY] * (2 * n), out_specs=[ANY] * n,
        out_shape=[_sds(a.shape) for a in layer0],
        scratch_shapes=[pltpu.SemaphoreType.DMA((n,)), pltpu.SemaphoreType.DMA((n,))],
    )(*layer0, *layer1)


def pair_add(a0, a1, theirs, c_vec, name):
    rows, cols = a0.shape
    tr = _rows_block(rows, cols)

    def body(c_ref, a0_ref, a1_ref, b_ref, o_ref):
        o_ref[...] = jnp.where(c_ref[0] == 0, a0_ref[...], a1_ref[...]) + b_ref[...]

    def of_core(core):
        return pl.BlockSpec((tr, cols), lambda i, c: (jnp.where(c[0] == core, i, 0), 0))

    spec = pl.BlockSpec((tr, cols), lambda i, c: (i, 0))
    grid_spec = pltpu.PrefetchScalarGridSpec(
        num_scalar_prefetch=1, grid=(rows // tr,), in_specs=[of_core(0), of_core(1), spec], out_specs=spec)
    return pl.pallas_call(body, grid_spec=grid_spec, name=name, out_shape=_sds((rows, cols)),
                          compiler_params=_params())(c_vec, a0, a1, theirs)


def scatter_chips(srcs, pieces, shard_shapes, name):
    n_src, n_t = len(srcs), len(pieces)

    def body(*refs):
        src_refs = refs[:n_src]
        outs = refs[n_src:n_src + n_t]
        send_sems, recv_sems, local_sems = refs[n_src + n_t:]
        x, y, c = _place()
        me = 2 * x + y

        def piece(t, k):
            idx, lead, rows, cols = pieces[t][k]
            ref = src_refs[idx]
            if lead is not None:
                ref = ref.at[lead]
            if rows is not None:
                ref = ref.at[pl.ds(rows[0], rows[1]), :]
            if cols is not None:
                ref = ref.at[:, pl.ds(cols[0], cols[1])]
            return ref

        def local(t, k):
            return pltpu.make_async_copy(piece(t, k), outs[t].at[k], local_sems.at[t])

        for k in range(N_CHIPS):
            @pl.when(me == k)
            def _(k=k):
                for t in range(n_t):
                    local(t, k).start()

            @pl.when(me != k)
            def _(k=k):
                for t in range(n_t):
                    _remote(piece(t, k), outs[t].at[me], send_sems.at[N_CHIPS * t + k],
                            recv_sems.at[N_CHIPS * t + me], (k // 2, k % 2, c)).start()

        for k in range(N_CHIPS):
            @pl.when(me != k)
            def _(k=k):
                for t in range(n_t):
                    cp = _remote(piece(t, k), outs[t].at[k], send_sems.at[N_CHIPS * t + k],
                                 recv_sems.at[N_CHIPS * t + k], (x, y, c))
                    cp.wait_recv()
                    cp.wait_send()

            @pl.when(me == k)
            def _(k=k):
                for t in range(n_t):
                    local(t, k).wait()

    return pl.pallas_call(
        body, name=name, in_specs=[ANY] * n_src, out_specs=[ANY] * n_t,
        out_shape=[_sds((N_CHIPS,) + tuple(shp)) for shp in shard_shapes],
        scratch_shapes=[pltpu.SemaphoreType.DMA((N_CHIPS * n_t,)), pltpu.SemaphoreType.DMA((N_CHIPS * n_t,)),
                        pltpu.SemaphoreType.DMA((n_t,))],
    )(*srcs)


def chips_add(recv, c_vec, name):
    n, a, b = recv.shape
    tr = _rows_block(a, b)

    def body(c_ref, r0, r1, r2, r3, o_ref):
        o_ref[...] = ((r0[...] + r1[...]) + r2[...]) + r3[...]

    grid_spec = pltpu.PrefetchScalarGridSpec(
        num_scalar_prefetch=1, grid=(a // tr,),
        in_specs=[pl.BlockSpec((None, tr, b), lambda i, c, k=k: (k, i, 0)) for k in range(n)],
        out_specs=pl.BlockSpec((None, tr, b), lambda i, c: (c[0], i, 0)))
    return pl.pallas_call(body, grid_spec=grid_spec, name=name, out_shape=_sds((2, a, b)),
                          compiler_params=_params())(c_vec, *([recv] * n))


def pair_join(bufs, name):
    n = len(bufs)

    def body(*refs):
        ins, outs = refs[:n], refs[n:2 * n]
        send_sems, recv_sems = refs[2 * n:]
        x, y, c = _place()
        cps = [_remote(ins[t].at[c], outs[t].at[c], send_sems.at[t], recv_sems.at[t], (x, y, 1 - c)) for t in range(n)]
        for cp in cps:
            cp.start()
        for t in range(n):
            cps[t].wait_send()
            _remote(ins[t].at[c], outs[t].at[1 - c], send_sems.at[t], recv_sems.at[t], (x, y, c)).wait_recv()

    return pl.pallas_call(
        body, name=name, in_specs=[ANY] * n, out_specs=[ANY] * n, out_shape=[_sds(a.shape) for a in bufs],
        input_output_aliases={t: t for t in range(n)},
        scratch_shapes=[pltpu.SemaphoreType.DMA((n,)), pltpu.SemaphoreType.DMA((n,))],
    )(*bufs)


def allsum_small(v, name):
    rows, lanes = v.shape
    n_dev = 8

    def body(v_ref, out_ref, buf, send_sems, recv_sems):
        x, y, c = _place()
        me, sibling = (x, y, c), (x, y, 1 - c)
        chips = _other_chips(x, y)

        def slot(px, py, pc):
            return buf.at[4 * px + 2 * py + pc]

        def copy(k, block, to, src=None):
            return pltpu.make_async_remote_copy(
                src_ref=slot(*block) if src is None else src, dst_ref=slot(*block),
                send_sem=send_sems.at[k], recv_sem=recv_sems.at[k], device_id=to, device_id_type=MESH)

        slot(*me)[...] = v_ref[...]
        first = [copy(0, me, sibling, src=v_ref)]
        first += [copy(1 + k, me, (*chip, c), src=v_ref) for k, chip in enumerate(chips)]
        for cp in first:
            cp.start()
        passed = [copy(4 + k, (*chip, c), sibling) for k, chip in enumerate(chips)]
        for k, chip in enumerate(chips):
            copy(1 + k, (*chip, c), me).wait_recv()
            passed[k].start()
        copy(0, sibling, me).wait_recv()
        for k, chip in enumerate(chips):
            copy(4 + k, (*chip, 1 - c), me).wait_recv()
        for cp in first + passed:
            cp.wait_send()
        acc = buf[0]
        for d in range(1, n_dev):
            acc = acc + buf[d]
        out_ref[...] = acc

    vm = pl.BlockSpec(memory_space=pltpu.VMEM)
    return pl.pallas_call(
        body, name=name, in_specs=[vm], out_specs=vm, out_shape=_sds((rows, lanes)),
        scratch_shapes=[pltpu.VMEM((n_dev, rows, lanes), F32), pltpu.SemaphoreType.DMA((7,)), pltpu.SemaphoreType.DMA((7,))],
        compiler_params=pltpu.CompilerParams(vmem_limit_bytes=VMEM_LIMIT),
    )(v)


BIG = ("w_in", "w_branch_a", "w_branch_b", "w_out", "w_up", "w_down")
CONV = ("conv_qkv", "conv_ffn")
REPL =("a_log", "dt_bias", "dn_norm_w", "sg_ln_g", "sg_ln_b", "w_spatial", "b_spatial", "ln1_g", "ln1_b", "ln2_g", "ln2_b")


def _pad_rows(flat, mult):
    n = flat.shape[0]
    unit = mult * LANES
    total = -(-n // unit) * unit
    return jnp.pad(flat, (0, total - n)).reshape(total // LANES, LANES)


def _pack(arrs, mult):
    return _pad_rows(jnp.concatenate([a.reshape(-1) for a in arrs]), mult)


def _unpack(flat, shapes):
    out, off = [], 0
    for shp in shapes:
        n = math.prod(shp)
        out.append(flat[off:off + n].reshape(shp))
        off += n
    return out


def kernel(x, w_in, conv_qkv, a_log, dt_bias, dn_norm_w, w_branch_a, sg_ln_g, sg_ln_b, w_spatial, b_spatial, w_branch_b, w_out, ln1_g, ln1_b, w_up, conv_ffn, w_down, ln2_g, ln2_b, loss_target, m_w_in, m_conv_qkv, m_a_log, m_dt_bias, m_dn_norm_w, m_w_branch_a, m_sg_ln_g, m_sg_ln_b, m_w_spatial, m_b_spatial, m_w_branch_b, m_w_out, m_ln1_g, m_ln1_b, m_w_up, m_conv_ffn, m_w_down, m_ln2_g, m_ln2_b, v_w_in, v_conv_qkv, v_a_log, v_dt_bias, v_dn_norm_w, v_w_branch_a, v_sg_ln_g, v_sg_ln_b, v_w_spatial, v_b_spatial, v_w_branch_b, v_w_out, v_ln1_g, v_ln1_b, v_w_up, v_conv_ffn, v_w_down, v_ln2_g, v_ln2_b):
    names = ("w_in", "conv_qkv", "a_log", "dt_bias", "dn_norm_w", "w_branch_a", "sg_ln_g", "sg_ln_b", "w_spatial",
             "b_spatial", "w_branch_b", "w_out", "ln1_g", "ln1_b", "w_up", "conv_ffn", "w_down", "ln2_g", "ln2_b")
    w = dict(zip(names, (w_in, conv_qkv, a_log, dt_bias, dn_norm_w, w_branch_a, sg_ln_g, sg_ln_b, w_spatial,
                         b_spatial, w_branch_b, w_out, ln1_g, ln1_b, w_up, conv_ffn, w_down, ln2_g, ln2_b)))
    m = dict(zip(names, (m_w_in, m_conv_qkv, m_a_log, m_dt_bias, m_dn_norm_w, m_w_branch_a, m_sg_ln_g, m_sg_ln_b,
                         m_w_spatial, m_b_spatial, m_w_branch_b, m_w_out, m_ln1_g, m_ln1_b, m_w_up, m_conv_ffn,
                         m_w_down, m_ln2_g, m_ln2_b)))
    v = dict(zip(names, (v_w_in, v_conv_qkv, v_a_log, v_dt_bias, v_dn_norm_w, v_w_branch_a, v_sg_ln_g, v_sg_ln_b,
                         v_w_spatial, v_b_spatial, v_w_branch_b, v_w_out, v_ln1_g, v_ln1_b, v_w_up, v_conv_ffn,
                         v_w_down, v_ln2_g, v_ln2_b)))
    chip = 2 * lax.axis_index("x") + lax.axis_index("y")
    s = x.shape[1]
    xs = x.reshape(s, D)
    tgt = loss_target.reshape(s, D)

    big_names, conv_names = list(BIG), list(CONV)
    got = dict(zip(big_names + conv_names, gather_weights(
        [w[n].astype(BF16) for n in big_names] + [w[n] for n in conv_names], "gather_weights")))

    def lane_row(vec, off):
        return jnp.zeros((1, LANES), F32).at[0, off:off + vec.shape[0]].set(vec)

    def side_by_side(name, l):
        return jnp.concatenate([got[name][k, l] for k in range(N_CHIPS)], axis=1)

    layers = []
    for l in range(DEPTH):
        wi = side_by_side("w_in", l)
        layers.append(dict(
            w_in=jnp.concatenate([wi[:, :2048], wi[:, 2056:3080], wi[:, 3080:5128], wi[:, 2048:2056],
                                  jnp.zeros((D, IN_COLS_PAD - 5128), BF16)], axis=1),
            cq=side_by_side("conv_qkv", l),
            a_row=lane_row(w["a_log"][l], HEADS), dtb_row=lane_row(w["dt_bias"][l], HEADS),
            nw_row=w["dn_norm_w"][l].reshape(1, DK),
            lng=w["sg_ln_g"][l].reshape(1, SG_W), lnb=w["sg_ln_b"][l].reshape(1, SG_W),
            w_s=w["w_spatial"][l], bs_t=jnp.zeros((LANES, LANES), F32).at[:, :4].set(w["b_spatial"][l].T),
            wa=side_by_side("w_branch_a", l), wb=side_by_side("w_branch_b", l),
            wo=got["w_out"][:, l].reshape(D, D),
            g1=w["ln1_g"][l].reshape(1, D), b1=w["ln1_b"][l].reshape(1, D),
            cf=got["conv_ffn"][:, l],
            g2=w["ln2_g"][l].reshape(1, D), b2=w["ln2_b"][l].reshape(1, D)))

    saved = []
    h_in = xs
    for l, p in enumerate(layers):
        proj = proj_fwd(h_in, p["w_in"], f"proj_fwd{l}")
        oa, sst = dn_fwd(proj, p["cq"], p["a_row"], p["dtb_row"], p["nw_row"], f"dn_fwd{l}")
        ob = sg_fwd(proj, p["lng"], p["lnb"], p["w_s"], p["bs_t"], f"sg_fwd{l}")
        x1 = merge_fwd(oa, ob, proj, h_in, p["wa"], p["wb"], p["wo"], p["g1"], p["b1"], f"merge_fwd{l}")
        pre2, x2 = ffn_fwd(x1, got["w_up"], p["cf"], got["w_down"], l, p["g2"], p["b2"], f"ffn_fwd{l}")
        saved.append(dict(x=h_in, proj=proj, oa=oa, ob=ob, sst=sst, x1=x1, pre2=pre2))
        h_in = x2

    dy, loss_part = loss_call(h_in, tgt, "loss")
    loss = lax.psum(loss_part[0, 0], ("x", "y", "c"))

    small_names = conv_names + list(REPL)
    grads = {n: [None] * DEPTH for n in small_names}
    big_grads = [None] * DEPTH
    for l in reversed(range(DEPTH)):
        p, a = layers[l], saved[l]
        dpre2, dg2, db2 = ln_bwd(a["pre2"], dy, p["g2"], p["b2"], f"ln2_bwd{l}")
        dx1, dwup0, dcf0, dwdn0 = ffn_bwd(a["x1"], dpre2, dpre2, ALPHA, got["w_up"], p["cf"], got["w_down"], l, 0, f"ffn_bwd{l}a")
        dx1, dwup1, dcf1, dwdn1 = ffn_bwd(a["x1"], dpre2, dx1, 1.0, got["w_up"], p["cf"], got["w_down"], l, 1, f"ffn_bwd{l}b")
        doa, dob, dga, dgb, dxd, dwa, dwb, dwo, dg1, db1 = merge_bwd(
            a["oa"], a["ob"], a["proj"], a["x"], dx1, p["wa"], p["wb"], p["wo"], p["g1"], p["b1"], f"merge_bwd{l}")
        duv, dlng, dlnb, dws, dbs = sg_bwd(a["proj"], dob, p["lng"], p["lnb"], p["w_s"], p["bs_t"], f"sg_bwd{l}")
        dqkv, dz, dba, dcq, da, ddtb, dnw = dn_bwd(a["proj"], a["sst"], doa, p["cq"], p["a_row"], p["dtb_row"],
                                                   p["nw_row"], f"dn_bwd{l}")
        dy, dwi = proj_bwd(a["x"], dqkv, dz, duv, dga, dgb, dba, dxd, p["w_in"], f"proj_bwd{l}")

        big_grads[l] = [dwi, dwa, dwb, dwo, dwup0.reshape(2 * D, FFN_HALF), dwup1.reshape(2 * D, FFN_HALF), dwdn0, dwdn1]
        grads["conv_qkv"][l] = dcq
        grads["conv_ffn"][l] = jnp.concatenate([dcf0[0], dcf1[0], dcf0[1], dcf1[1]], axis=1)
        grads["a_log"][l] = da[0, HEADS:2 * HEADS]
        grads["dt_bias"][l] = ddtb[0, HEADS:2 * HEADS]
        grads["dn_norm_w"][l] = dnw[0]
        grads["sg_ln_g"][l] = dlng[0]
        grads["sg_ln_b"][l] = dlnb[0]
        grads["w_spatial"][l] = dws
        grads["b_spatial"][l] = dbs[:, :4].T
        grads["ln1_g"][l] = dg1[0]
        grads["ln1_b"][l] = db1[0]
        grads["ln2_g"][l] = dg2[0]
        grads["ln2_b"][l] = db2[0]
    grad_x = dy.reshape(x.shape)
    g_full = {n: jnp.stack(grads[n]) for n in small_names}

    c_vec = jnp.stack([lax.axis_index("c")]).astype(jnp.int32)
    theirs = pair_exchange(big_grads[0], big_grads[1], "reduce_pair")
    tags = ("w_in", "w_a", "w_b", "w_out", "w_up0", "w_up1", "w_dn0", "w_dn1")
    pin, pa, pb, po, pup0, pup1, pdn0, pdn1 = [
        pair_add(a0, a1, th, c_vec, f"reduce_pair_add_{tag}")
        for tag, a0, a1, th in zip(tags, big_grads[0], big_grads[1], theirs)]
    natural = jnp.concatenate([pin[:, :2048], pin[:, C_BA:C_BA + 8], pin[:, 2048:C_BA]], axis=1)
    srcs = [jnp.stack(jnp.split(natural, N_CHIPS, axis=1)), pa, pb, po,
            pup0.reshape(2, D, FFN_HALF), pup1.reshape(2, D, FFN_HALF), pdn0, pdn1]
    ab_cols, out_rows = w["w_branch_a"].shape[2], w["w_out"].shape[1]
    pieces = [
        [(0, k, None, None) for k in range(N_CHIPS)],
        [(1, None, None, (k * ab_cols, ab_cols)) for k in range(N_CHIPS)],
        [(2, None, None, (k * ab_cols, ab_cols)) for k in range(N_CHIPS)],
        [(3, None, (k * out_rows, out_rows), None) for k in range(N_CHIPS)],
        [(4 + k % 2, k // 2, None, None) for k in range(N_CHIPS)],
        [(6 + k // 2, None, ((k % 2) * DN_SHARD, DN_SHARD), None) for k in range(N_CHIPS)],
    ]
    recv = scatter_chips(srcs, pieces, [w[n].shape[1:] for n in big_names], "reduce_chips")
    sums = [chips_add(r, c_vec, f"reduce_chips_add_{n}") for n, r in zip(big_names, recv)]
    g_shard = dict(zip(big_names, pair_join(sums, "reduce_join")))

    small = allsum_small(_pack([g_full[n] for n in small_names], 8), "reduce_small").reshape(-1)
    small_full = dict(zip(small_names, _unpack(small, [g_full[n].shape for n in small_names])))
    for n in conv_names:
        width = w[n].shape[2]
        g_shard[n] = lax.dynamic_slice_in_dim(small_full[n], chip * width, width, axis=2)
    for n in REPL:
        g_shard[n] = small_full[n]

    delta, new_m, new_v = {}, {}, {}
    for n in big_names:
        shp = w[n].shape
        two_d = (shp[0] * shp[1], shp[2])
        d_, m_, v_ = adam_call(w[n].reshape(two_d), g_shard[n].reshape(two_d), m[n].reshape(two_d), v[n].reshape(two_d), f"adam_{n}")
        delta[n], new_m[n], new_v[n] = d_.reshape(shp), m_.reshape(shp), v_.reshape(shp)
    shapes = [w[n].shape for n in small_names]
    packs = [_pack([src[n] for n in small_names], 8) for src in (w, g_shard, m, v)]
    outs = adam_call(*packs, "adam_small")
    for dst, o in zip((delta, new_m, new_v), outs):
        dst.update(zip(small_names, _unpack(o.reshape(-1), shapes)))

    return (loss, grad_x, *[g_shard[n] for n in names], *[delta[n] for n in names],
            *[new_m[n] for n in names], *[new_v[n] for n in names])
```

```python
import functools
import math

import jax
import jax.numpy as jnp
from jax import lax
from jax.experimental import pallas as pl
from jax.experimental.pallas import tpu as pltpu

F32 = jnp.float32
BF16 = jnp.bfloat16
HI = lax.Precision.HIGHEST
MID = lax.Precision.HIGH
MESH = pl.DeviceIdType.MESH

D = 1024
DEPTH = 2
HEADS = 4
DK = 128
CHUNK = 64
QKV_W = 1536
Z_W = 512
SG_W = 512
FFN = 2816
FFN_HALF = FFN // 2
N_CHIPS = 4
DN_SHARD = FFN // N_CHIPS
LN_EPS = 1e-5
RMS_EPS = 1e-6
L2_EPS = 1e-6
ALPHA = (2 * DEPTH) ** 0.25
ADAM_LR, ADAM_B1, ADAM_B2, ADAM_EPS, ADAM_WD, ADAM_STEP = 0.001, 0.9, 0.999, 1e-08, 0.01, 10

HALO = 16
LANES = 128
IN_COLS_PAD = 5248
C_Z, C_UV, C_GA, C_GB, C_BA = 1536, 2048, 3072, 4096, 5120
VMEM_LIMIT = 56 * 1024 * 1024


def _params(n_grid=1):
    return pltpu.CompilerParams(dimension_semantics=("arbitrary",) * n_grid, vmem_limit_bytes=VMEM_LIMIT)


def _mm(a, b):
    return jnp.dot(a.astype(BF16), b.astype(BF16), preferred_element_type=F32)


def _mm_nt(a, b):
    return lax.dot_general(a.astype(BF16), b.astype(BF16), (((1,), (1,)), ((), ())), preferred_element_type=F32)


def _mm_tn(a, b):
    return lax.dot_general(a.astype(BF16), b.astype(BF16), (((0,), (0,)), ((), ())), preferred_element_type=F32)


def _bdot(a, b, prec=MID):
    return lax.dot_general(a, b, (((2,), (1,)), ((0,), (0,))), precision=prec, preferred_element_type=F32)


def _bdot_nt(a, b, prec=MID):
    return lax.dot_general(a, b, (((2,), (2,)), ((0,), (0,))), precision=prec, preferred_element_type=F32)


def _bf16_dot(a, b, contract):
    return lax.dot_general(a.astype(BF16), b.astype(BF16), (contract, ((0,), (0,))), preferred_element_type=F32)


@jax.custom_vjp
def _fdot(a, b):
    return _bf16_dot(a, b, ((2,), (1,)))


def _fdot_fwd(a, b):
    return _fdot(a, b), (a, b)


def _fdot_bwd(res, ct):
    a, b = res
    return _bf16_dot(ct, b, ((2,), (2,))), _bf16_dot(a, ct, ((1,), (1,)))


_fdot.defvjp(_fdot_fwd, _fdot_bwd)


@jax.custom_vjp
def _fdot_nt(a, b):
    return _bf16_dot(a, b, ((2,), (2,)))


def _fdot_nt_fwd(a, b):
    return _fdot_nt(a, b), (a, b)


def _fdot_nt_bwd(res, ct):
    a, b = res
    return _bf16_dot(ct, b, ((2,), (1,))), _bf16_dot(ct, a, ((1,), (1,)))


_fdot_nt.defvjp(_fdot_nt_fwd, _fdot_nt_bwd)


@jax.custom_vjp
def _fdot_tn(a, b):
    return _bf16_dot(a, b, ((1,), (1,)))


def _fdot_tn_fwd(a, b):
    return _fdot_tn(a, b), (a, b)


def _fdot_tn_bwd(res, ct):
    a, b = res
    return _bf16_dot(b, ct, ((2,), (2,))), _bf16_dot(a, ct, ((2,), (1,)))


_fdot_tn.defvjp(_fdot_tn_fwd, _fdot_tn_bwd)


def _ln(x, g, b):
    mu = jnp.mean(x, axis=-1, keepdims=True)
    xc = x - mu
    var = jnp.mean(xc * xc, axis=-1, keepdims=True)
    return xc * lax.rsqrt(var + LN_EPS) * g + b


def _shift_rows(x, s):
    s = s % x.shape[0]
    return x if s == 0 else pltpu.roll(x, s, 0)


@jax.custom_vjp
def _conv(xcat, w):
    k_taps = len(w)
    y = None
    for k in range(k_taps):
        t = _shift_rows(xcat, k_taps - 1 - k)[HALO:] * w[k]
        y = t if y is None else y + t
    return y


def _conv_fwd(xcat, w):
    return _conv(xcat, w), (xcat, w)


def _conv_bwd(res, dy):
    xcat, w = res
    k_taps = len(w)
    dyp = jnp.concatenate([jnp.zeros((HALO, dy.shape[1]), dy.dtype), dy], axis=0)
    dx = None
    dws = []
    for k in range(k_taps):
        s = k_taps - 1 - k
        t = _shift_rows(dyp, -s) * w[k]
        dx = t if dx is None else dx + t
        dws.append(jnp.sum(_shift_rows(xcat, s)[HALO:] * dy, axis=0, keepdims=True))
    return dx, tuple(dws)


_conv.defvjp(_conv_fwd, _conv_bwd)


@jax.custom_vjp
def _tri_inv(l):
    n = l.shape[-1]
    r = lax.broadcasted_iota(jnp.int32, (n, n), 0)
    c = lax.broadcasted_iota(jnp.int32, (n, n), 1)
    eye = (r == c).astype(F32)
    p = eye - l
    lp = l
    steps = int(math.log2(n)) - 1
    for i in range(steps):
        dot = _bdot if i < 2 else functools.partial(_bf16_dot, contract=((2,), (1,)))
        lp = dot(lp, lp)
        p = p + dot(p, lp)
    return p


def _tri_inv_fwd(l):
    t = _tri_inv(l)
    return t, t


def _tri_inv_bwd(t, dt):
    tt = jnp.swapaxes(t, 1, 2)
    return (-_bdot(tt, _bdot(dt, tt)),)


_tri_inv.defvjp(_tri_inv_fwd, _tri_inv_bwd)


def _dn_glue(qkvcat, z, ba, s_in, cw, a_row, dtb_row, nw_row):
    t_rows = z.shape[0]
    nc = t_rows // CHUNK
    nb = nc * HEADS

    qkv = jax.nn.silu(_conv(qkvcat, cw))

    def chunks(t, off):
        return jnp.stack([t[n * CHUNK:(n + 1) * CHUNK, off + h * DK: off + (h + 1) * DK]
                          for n in range(nc) for h in range(HEADS)])

    q = chunks(qkv, 0)
    k = chunks(qkv, 512)
    v = chunks(qkv, 1024)
    q = q * lax.rsqrt(jnp.sum(q * q, axis=-1, keepdims=True) + L2_EPS) * (DK ** -0.5)
    k = k * lax.rsqrt(jnp.sum(k * k, axis=-1, keepdims=True) + L2_EPS)

    lane = lax.broadcasted_iota(jnp.int32, (LANES, HEADS * DK), 0)
    head_of_col = lax.broadcasted_iota(jnp.int32, (LANES, HEADS * DK), 1) // DK
    e_beta = (head_of_col == lane).astype(F32)
    e_g = (head_of_col + HEADS == lane).astype(F32)
    beta_l = jax.nn.sigmoid(ba)
    g_l = -jnp.exp(a_row) * jax.nn.softplus(ba + dtb_row)
    beta = chunks(jnp.dot(beta_l, e_beta, precision=HI, preferred_element_type=F32), 0)
    g = chunks(jnp.dot(g_l, e_g, precision=HI, preferred_element_type=F32), 0)

    r = lax.broadcasted_iota(jnp.int32, (CHUNK, CHUNK), 0)
    c = lax.broadcasted_iota(jnp.int32, (CHUNK, CHUNK), 1)
    causal = r >= c
    strict = r > c
    tril_b = jnp.broadcast_to(causal.astype(F32), (nb, CHUNK, CHUNK))
    gi_b = _bdot(tril_b, g, HI)
    gi = gi_b[:, :, :CHUNK]
    gj = jnp.swapaxes(gi, 1, 2)
    decay = jnp.where(causal, jnp.exp(jnp.where(causal, gi - gj, 0.0)), 0.0)
    kb = k * beta
    l_mat = jnp.where(strict, _fdot_nt(kb, k) * decay, 0.0)
    t_mat = _tri_inv(l_mat)
    e_gi = jnp.exp(gi_b)
    w_mat = _fdot(t_mat, kb * e_gi)
    u_mat = _fdot(t_mat, v * beta)
    a_qk = _fdot_nt(q, k) * decay
    q_g = q * e_gi
    gl_b = jnp.broadcast_to(jnp.sum(g, axis=1, keepdims=True), g.shape)
    k_d = k * jnp.exp(gl_b - gi_b)
    e_gl = jnp.exp(gl_b)
    g_last = jnp.concatenate([e_gl, e_gl], axis=1)

    state = s_in
    rows = []
    for n in range(nc):
        sl = slice(n * HEADS, (n + 1) * HEADS)
        u_new = u_mat[sl] - _fdot(w_mat[sl], state)
        o_n = _fdot(q_g[sl], state) + _fdot(a_qk[sl], u_new)
        state = state * g_last[sl] + _fdot_tn(k_d[sl], u_new)
        o_n = o_n * lax.rsqrt(jnp.mean(o_n * o_n, axis=-1, keepdims=True) + RMS_EPS) * nw_row
        z_n = jnp.stack([z[n * CHUNK:(n + 1) * CHUNK, h * DK:(h + 1) * DK] for h in range(HEADS)])
        o_n = o_n * jax.nn.silu(z_n)
        rows.append(jnp.concatenate([o_n[h] for h in range(HEADS)], axis=-1))
    return jnp.concatenate(rows, axis=0), state


def _sg_glue(uv, lng, lnb, w_s, bs_t):
    t_rows = uv.shape[0]
    y = jax.nn.gelu(uv)
    u = y[:, :SG_W]
    v = _ln(y[:, SG_W:], lng, lnb)
    r = lax.broadcasted_iota(jnp.int32, (LANES, LANES), 0)
    c = lax.broadcasted_iota(jnp.int32, (LANES, LANES), 1)
    wm = jnp.where(r >= c, w_s, 0.0)
    lane = lax.broadcasted_iota(jnp.int32, (LANES, SG_W), 0)
    group_of_col = lax.broadcasted_iota(jnp.int32, (LANES, SG_W), 1) // LANES
    e_grp = (group_of_col == lane).astype(F32)
    bias = jnp.dot(bs_t, e_grp, precision=HI, preferred_element_type=F32)
    outs = []
    for n in range(t_rows // LANES):
        vb = v[n * LANES:(n + 1) * LANES]
        vg = jnp.stack([vb[:, g * LANES:(g + 1) * LANES] for g in range(4)])
        mg = _fdot(wm, vg)
        mixed = jnp.concatenate([mg[g] for g in range(4)], axis=-1) + bias
        outs.append(u[n * LANES:(n + 1) * LANES] * mixed)
    return jnp.concatenate(outs, axis=0)


def _merge_glue(ga, gb, ya, yb):
    return jax.nn.sigmoid(ga) * ya + jax.nn.sigmoid(gb) * yb


def _res_ln_glue(x, r, g, b):
    return _ln(ALPHA * x + r, g, b)


def _ffn_glue(ua, ub, cwa, cwb):
    return jax.nn.silu(_conv(ua, cwa)) * _conv(ub, cwb)


def _row(t, c, col=0):
    return pl.BlockSpec((t, c), lambda i: (i, col))


def _row_rev(t, c, nt, col=0):
    return pl.BlockSpec((t, c), lambda i: (nt - 1 - i, col))


def _halo(t, c, nt=None):
    per = t // HALO
    if nt is None:
        return pl.BlockSpec((HALO, c), lambda i: (jnp.maximum(i * per - 1, 0), 0))
    return pl.BlockSpec((HALO, c), lambda i: (jnp.maximum((nt - 1 - i) * per - 1, 0), 0))


def _full(shape):
    nd = len(shape)
    return pl.BlockSpec(shape, lambda i: (0,) * nd)


ANY = pl.BlockSpec(memory_space=pl.ANY)


def _sds(shape, dtype=F32):
    return jax.ShapeDtypeStruct(shape, dtype)


def _tile(s, want=256):
    for t in (want, 256, 128):
        if s % t == 0:
            return t
    raise ValueError(f"sequence length {s} is not a multiple of 128")


def proj_fwd(x, w, name):
    s = x.shape[0]
    t = _tile(s)
    segs = [(0, 2048), (2048, 3072), (3072, 4096), (4096, 5120), (5120, IN_COLS_PAD)]

    def body(x_ref, w_ref, p_ref):
        xb = x_ref[...].astype(BF16)
        for lo, hi in segs:
            p_ref[:, lo:hi] = jnp.dot(xb, w_ref[:, lo:hi], preferred_element_type=F32)

    return pl.pallas_call(
        body, grid=(s // t,), name=name,
        in_specs=[_row(t, D), _full((D, IN_COLS_PAD))],
        out_specs=_row(t, IN_COLS_PAD),
        out_shape=_sds((s, IN_COLS_PAD)), compiler_params=_params())(x, w)


def dn_fwd(p, cq, a_row, dtb_row, nw_row, name):
    s = p.shape[0]
    t = _tile(s)
    nt = s // t

    def body(qkv_ref, halo_ref, z_ref, ba_ref, cq_ref, a_ref, dtb_ref, nw_ref, o_ref, sst_ref, s_scr):
        i = pl.program_id(0)

        @pl.when(i == 0)
        def _():
            s_scr[...] = jnp.zeros_like(s_scr)

        halo = jnp.where(i == 0, 0.0, halo_ref[...])
        qkvcat = jnp.concatenate([halo, qkv_ref[...]], axis=0)
        cw = tuple(cq_ref[k:k + 1, :] for k in range(4))
        s_in = s_scr[...]
        sst_ref[0] = s_in
        o, s_out = _dn_glue(qkvcat, z_ref[...], ba_ref[...], s_in, cw, a_ref[...], dtb_ref[...], nw_ref[...])
        o_ref[...] = o.astype(BF16)
        s_scr[...] = s_out

    return pl.pallas_call(
        body, grid=(nt,), name=name,
        in_specs=[_row(t, QKV_W), _halo(t, QKV_W), _row(t, Z_W, C_Z // Z_W), _row(t, LANES, C_BA // LANES),
                  _full((4, QKV_W)), _full((1, LANES)), _full((1, LANES)), _full((1, LANES))],
        out_specs=[_row(t, Z_W), pl.BlockSpec((1, HEADS, DK, DK), lambda i: (i, 0, 0, 0))],
        out_shape=[_sds((s, Z_W), BF16), _sds((nt, HEADS, DK, DK))],
        scratch_shapes=[pltpu.VMEM((HEADS, DK, DK), F32)],
        compiler_params=_params())(p, p, p, p, cq, a_row, dtb_row, nw_row)


def sg_fwd(p, lng, lnb, w_s, bs_t, name):
    s = p.shape[0]
    t = _tile(s)

    def body(uv_ref, lng_ref, lnb_ref, ws_ref, bs_ref, o_ref):
        o_ref[...] = _sg_glue(uv_ref[...], lng_ref[...], lnb_ref[...], ws_ref[...], bs_ref[...]).astype(BF16)

    return pl.pallas_call(
        body, grid=(s // t,), name=name,
        in_specs=[_row(t, 2 * SG_W, C_UV // (2 * SG_W)), _full((1, SG_W)), _full((1, SG_W)),
                  _full((4, LANES, LANES)), _full((LANES, LANES))],
        out_specs=_row(t, SG_W), out_shape=_sds((s, SG_W), BF16), compiler_params=_params())(p, lng, lnb, w_s, bs_t)


def merge_fwd(oa, ob, p, x, wa, wb, wo, g1, b1, name):
    s = x.shape[0]
    t = _tile(s)

    def body(oa_ref, ob_ref, ga_ref, gb_ref, x_ref, wa_ref, wb_ref, wo_ref, g_ref, b_ref, x1_ref):
        ya = _mm(oa_ref[...], wa_ref[...])
        yb = _mm(ob_ref[...], wb_ref[...])
        h = _merge_glue(ga_ref[...], gb_ref[...], ya, yb)
        x1_ref[...] = _res_ln_glue(x_ref[...], _mm(h, wo_ref[...]), g_ref[...], b_ref[...])

    return pl.pallas_call(
        body, grid=(s // t,), name=name,
        in_specs=[_row(t, Z_W), _row(t, SG_W), _row(t, D, C_GA // D), _row(t, D, C_GB // D), _row(t, D),
                  _full((Z_W, D)), _full((SG_W, D)), _full((D, D)), _full((1, D)), _full((1, D))],
        out_specs=_row(t, D), out_shape=_sds((s, D)), compiler_params=_params())(oa, ob, p, p, x, wa, wb, wo, g1, b1)


def _load_ffn_weights(wup_hbm, wdn_hbm, wup_v, wdn_v, layer, up_slots, dn_slots):
    for n, k in enumerate(up_slots):
        pltpu.sync_copy(wup_hbm.at[k, layer], wup_v.at[n])
    for n, k in enumerate(dn_slots):
        pltpu.sync_copy(wdn_hbm.at[k, layer], wdn_v.at[pl.ds(n * DN_SHARD, DN_SHARD)])


def ffn_fwd(x1, wup4, cf4, wdn4, layer, g2, b2, name):
    s = x1.shape[0]
    t = _tile(s)

    def body(x1_ref, halo_ref, wup_hbm, cf_ref, wdn_hbm, g_ref, b_ref, pre_ref, x2_ref, wup_v, wdn_v):
        i = pl.program_id(0)

        @pl.when(i == 0)
        def _():
            _load_ffn_weights(wup_hbm, wdn_hbm, wup_v, wdn_v, layer, range(4), range(4))

        x1v = x1_ref[...]
        halo = jnp.where(i == 0, 0.0, halo_ref[...])
        x1cat = jnp.concatenate([halo, x1v], axis=0).astype(BF16)
        f = None
        for h in range(2):
            ua = jnp.dot(x1cat, wup_v[h], preferred_element_type=F32)
            ub = jnp.dot(x1cat, wup_v[2 + h], preferred_element_type=F32)
            cwa = tuple(cf_ref[h, k:k + 1, :] for k in range(3))
            cwb = tuple(cf_ref[2 + h, k:k + 1, :] for k in range(3))
            act = _ffn_glue(ua, ub, cwa, cwb)
            fh = _mm(act, wdn_v[h * FFN_HALF:(h + 1) * FFN_HALF, :])
            f = fh if f is None else f + fh
        pre = ALPHA * x1v + f
        pre_ref[...] = pre
        x2_ref[...] = _ln(pre, g_ref[...], b_ref[...])

    return pl.pallas_call(
        body, grid=(s // t,), name=name,
        in_specs=[_row(t, D), _halo(t, D), ANY, _full((4, 3, FFN_HALF)), ANY, _full((1, D)), _full((1, D))],
        out_specs=[_row(t, D), _row(t, D)], out_shape=[_sds((s, D)), _sds((s, D))],
        scratch_shapes=[pltpu.VMEM((4, D, FFN_HALF), BF16), pltpu.VMEM((FFN, D), BF16)],
        compiler_params=_params())(x1, x1, wup4, cf4, wdn4, g2, b2)


def loss_call(y, tgt, name):
    s = y.shape[0]
    t = _tile(s)

    def body(y_ref, t_ref, dy_ref, loss_ref):
        @pl.when(pl.program_id(0) == 0)
        def _():
            loss_ref[...] = jnp.zeros_like(loss_ref)

        e = y_ref[...] - t_ref[...]
        dy_ref[...] = e * (1.0 / D)
        part = jnp.sum(jnp.sum(e * e, axis=1, keepdims=True), axis=0, keepdims=True) * (0.5 / D)
        loss_ref[...] += jnp.broadcast_to(part, loss_ref.shape)

    return pl.pallas_call(
        body, grid=(s // t,), name=name, in_specs=[_row(t, D), _row(t, D)],
        out_specs=[_row(t, D), _full((8, LANES))], out_shape=[_sds((s, D)), _sds((8, LANES))],
        compiler_params=_params())(y, tgt)


def _acc(ref, val, first):
    @pl.when(first)
    def _():
        ref[...] = val

    @pl.when(jnp.logical_not(first))
    def _():
        ref[...] += val


def _acc_tn(acc_ref, a, b, first, seg):
    n = b.shape[1]
    for lo in range(0, n, seg):
        hi = min(lo + seg, n)
        _acc(acc_ref.at[:, lo:hi], _mm_tn(a, b[:, lo:hi]), first)


def ln_bwd(pre, dy, g, b, name):
    s = pre.shape[0]
    t = _tile(s)

    def body(pre_ref, dy_ref, g_ref, b_ref, dpre_ref, dg_ref, db_ref):
        _, vjp = jax.vjp(_ln, pre_ref[...], g_ref[...], b_ref[...])
        dpre, dg, db = vjp(dy_ref[...])
        dpre_ref[...] = dpre
        first = pl.program_id(0) == 0
        _acc(dg_ref, dg, first)
        _acc(db_ref, db, first)

    return pl.pallas_call(
        body, grid=(s // t,), name=name, in_specs=[_row(t, D), _row(t, D), _full((1, D)), _full((1, D))],
        out_specs=[_row(t, D), _full((1, D)), _full((1, D))],
        out_shape=[_sds((s, D)), _sds((1, D)), _sds((1, D))], compiler_params=_params())(pre, dy, g, b)


def ffn_bwd(x1, df, acc_in, acc_scale, wup4, cf4, wdn4, layer, h, name):
    s = x1.shape[0]
    t = _tile(s)
    nt = s // t

    def body(x1_ref, halo_ref, df_ref, acc_ref, wup_hbm, cf_ref, wdn_hbm,
             dx1_ref, dwup_hbm, dcf_ref, dwdn_hbm, wup_v, wdn_v, dwup_v, dwdn_v, carry):
        i = pl.program_id(0)
        j = nt - 1 - i
        first = i == 0

        @pl.when(first)
        def _():
            _load_ffn_weights(wup_hbm, wdn_hbm, wup_v, wdn_v, layer, (h, 2 + h), (2 * h, 2 * h + 1))
            carry[...] = jnp.zeros_like(carry)

        halo = jnp.where(j == 0, 0.0, halo_ref[...])
        x1cat = jnp.concatenate([halo, x1_ref[...]], axis=0).astype(BF16)
        ua = jnp.dot(x1cat, wup_v[0], preferred_element_type=F32)
        ub = jnp.dot(x1cat, wup_v[1], preferred_element_type=F32)
        cwa = tuple(cf_ref[h, k:k + 1, :] for k in range(3))
        cwb = tuple(cf_ref[2 + h, k:k + 1, :] for k in range(3))
        act, vjp = jax.vjp(_ffn_glue, ua, ub, cwa, cwb)
        dfb = df_ref[...].astype(BF16)
        dact = _mm_nt(dfb, wdn_v[...])
        _acc_tn(dwdn_v, act.astype(BF16), dfb, first, 512)
        dua, dub, dcwa, dcwb = vjp(dact)
        x1b = x1cat[HALO:]
        dups = []
        for n, du in enumerate((dua, dub)):
            dups.append(jnp.concatenate([du[HALO:t], du[t:] + carry[n]], axis=0).astype(BF16))
            carry[n] = du[:HALO]
            _acc(dwup_v.at[n], _mm_tn(x1b, dups[n]), first)
        for k in range(3):
            _acc(dcf_ref.at[0, k:k + 1, :], dcwa[k], first)
            _acc(dcf_ref.at[1, k:k + 1, :], dcwb[k], first)
        dx1_ref[...] = acc_scale * acc_ref[...] + _mm_nt(dups[0], wup_v[0]) + _mm_nt(dups[1], wup_v[1])

        @pl.when(i == nt - 1)
        def _():
            pltpu.sync_copy(dwup_v, dwup_hbm)
            pltpu.sync_copy(dwdn_v, dwdn_hbm)

    return pl.pallas_call(
        body, grid=(nt,), name=name,
        in_specs=[_row_rev(t, D, nt), _halo(t, D, nt), _row_rev(t, D, nt), _row_rev(t, D, nt),
                  ANY, _full((4, 3, FFN_HALF)), ANY],
        out_specs=[_row_rev(t, D, nt), ANY, _full((2, 3, FFN_HALF)), ANY],
        out_shape=[_sds((s, D)), _sds((2, D, FFN_HALF)), _sds((2, 3, FFN_HALF)), _sds((FFN_HALF, D))],
        scratch_shapes=[pltpu.VMEM((2, D, FFN_HALF), BF16), pltpu.VMEM((FFN_HALF, D), BF16),
                        pltpu.VMEM((2, D, FFN_HALF), F32), pltpu.VMEM((FFN_HALF, D), F32),
                        pltpu.VMEM((2, HALO, FFN_HALF), F32)],
        compiler_params=_params())(x1, x1, df, acc_in, wup4, cf4, wdn4)


def merge_bwd(oa, ob, p, x, dx1, wa, wb, wo, g1, b1, name):
    s = x.shape[0]
    t = _tile(s)

    def body(oa_ref, ob_ref, ga_ref, gb_ref, x_ref, dx1_ref, wa_ref, wb_ref, wo_ref, g_ref, b_ref,
             doa_ref, dob_ref, dga_ref, dgb_ref, dx_ref, dwa_ref, dwb_ref, dwo_ref, dg_ref, db_ref):
        first = pl.program_id(0) == 0
        oa = oa_ref[...]
        ob = ob_ref[...]
        ya = _mm(oa, wa_ref[...])
        yb = _mm(ob, wb_ref[...])
        h, vjp1 = jax.vjp(_merge_glue, ga_ref[...], gb_ref[...], ya, yb)
        hb = h.astype(BF16)
        r = _mm(hb, wo_ref[...])
        _, vjp2 = jax.vjp(_res_ln_glue, x_ref[...], r, g_ref[...], b_ref[...])
        dx, dr, dg, db = vjp2(dx1_ref[...])
        dx_ref[...] = dx
        _acc(dg_ref, dg, first)
        _acc(db_ref, db, first)
        drb = dr.astype(BF16)
        dh = _mm_nt(drb, wo_ref[...])
        _acc(dwo_ref, _mm_tn(hb, drb), first)
        dga, dgb, dya, dyb = vjp1(dh)
        dga_ref[...] = dga.astype(BF16)
        dgb_ref[...] = dgb.astype(BF16)
        dyab = dya.astype(BF16)
        dybb = dyb.astype(BF16)
        doa_ref[...] = _mm_nt(dyab, wa_ref[...]).astype(BF16)
        dob_ref[...] = _mm_nt(dybb, wb_ref[...]).astype(BF16)
        _acc(dwa_ref, _mm_tn(oa, dyab), first)
        _acc(dwb_ref, _mm_tn(ob, dybb), first)

    return pl.pallas_call(
        body, grid=(s // t,), name=name,
        in_specs=[_row(t, Z_W), _row(t, SG_W), _row(t, D, C_GA // D), _row(t, D, C_GB // D), _row(t, D), _row(t, D),
                  _full((Z_W, D)), _full((SG_W, D)), _full((D, D)), _full((1, D)), _full((1, D))],
        out_specs=[_row(t, Z_W), _row(t, SG_W), _row(t, D), _row(t, D), _row(t, D),
                   _full((Z_W, D)), _full((SG_W, D)), _full((D, D)), _full((1, D)), _full((1, D))],
        out_shape=[_sds((s, Z_W), BF16), _sds((s, SG_W), BF16), _sds((s, D), BF16), _sds((s, D), BF16), _sds((s, D)),
                   _sds((Z_W, D)), _sds((SG_W, D)), _sds((D, D)), _sds((1, D)), _sds((1, D))],
        compiler_params=_params())(oa, ob, p, p, x, dx1, wa, wb, wo, g1, b1)


def sg_bwd(p, dob, lng, lnb, w_s, bs_t, name):
    s = p.shape[0]
    t = _tile(s)

    def body(uv_ref, dob_ref, lng_ref, lnb_ref, ws_ref, bs_ref, duv_ref, dlng_ref, dlnb_ref, dws_ref, dbs_ref):
        first = pl.program_id(0) == 0
        _, vjp = jax.vjp(_sg_glue, uv_ref[...], lng_ref[...], lnb_ref[...], ws_ref[...], bs_ref[...])
        duv, dlng, dlnb, dws, dbs = vjp(dob_ref[...].astype(F32))
        duv_ref[...] = duv.astype(BF16)
        _acc(dlng_ref, dlng, first)
        _acc(dlnb_ref, dlnb, first)
        _acc(dws_ref, dws, first)
        _acc(dbs_ref, dbs, first)

    return pl.pallas_call(
        body, grid=(s // t,), name=name,
        in_specs=[_row(t, 2 * SG_W, C_UV // (2 * SG_W)), _row(t, SG_W), _full((1, SG_W)), _full((1, SG_W)),
                  _full((4, LANES, LANES)), _full((LANES, LANES))],
        out_specs=[_row(t, 2 * SG_W), _full((1, SG_W)), _full((1, SG_W)), _full((4, LANES, LANES)), _full((LANES, LANES))],
        out_shape=[_sds((s, 2 * SG_W), BF16), _sds((1, SG_W)), _sds((1, SG_W)), _sds((4, LANES, LANES)), _sds((LANES, LANES))],
        compiler_params=_params())(p, dob, lng, lnb, w_s, bs_t)


def dn_bwd(p, sst, doa, cq, a_row, dtb_row, nw_row, name):
    s = p.shape[0]
    t = _tile(s)
    nt = s // t

    def body(qkv_ref, halo_ref, z_ref, ba_ref, sst_ref, doa_ref, cq_ref, a_ref, dtb_ref, nw_ref,
             dqkv_ref, dz_ref, dba_ref, dcq_ref, da_ref, ddtb_ref, dnw_ref, ds_scr, carry):
        i = pl.program_id(0)
        j = nt - 1 - i
        first = i == 0

        @pl.when(first)
        def _():
            ds_scr[...] = jnp.zeros_like(ds_scr)
            carry[...] = jnp.zeros_like(carry)

        halo = jnp.where(j == 0, 0.0, halo_ref[...])
        qkvcat = jnp.concatenate([halo, qkv_ref[...]], axis=0)
        cw = tuple(cq_ref[k:k + 1, :] for k in range(4))
        _, vjp = jax.vjp(_dn_glue, qkvcat, z_ref[...], ba_ref[...], sst_ref[0], cw, a_ref[...], dtb_ref[...], nw_ref[...])
        dqkvcat, dz, dba, ds_in, dcw, da, ddtb, dnw = vjp((doa_ref[...].astype(F32), ds_scr[...]))
        ds_scr[...] = ds_in
        dz_ref[...] = dz.astype(BF16)
        dba_ref[...] = dba.astype(BF16)
        dtile = dqkvcat[HALO:]
        dqkv_ref[...] = dtile.astype(BF16)
        dqkv_ref[t - HALO:t, :] = (dtile[t - HALO:] + carry[...]).astype(BF16)
        carry[...] = dqkvcat[:HALO]
        for k in range(4):
            _acc(dcq_ref.at[k:k + 1, :], dcw[k], first)
        _acc(da_ref, da, first)
        _acc(ddtb_ref, ddtb, first)
        _acc(dnw_ref, dnw, first)

    return pl.pallas_call(
        body, grid=(nt,), name=name,
        in_specs=[_row_rev(t, QKV_W, nt), _halo(t, QKV_W, nt), _row_rev(t, Z_W, nt, C_Z // Z_W),
                  _row_rev(t, LANES, nt, C_BA // LANES),
                  pl.BlockSpec((1, HEADS, DK, DK), lambda i: (nt - 1 - i, 0, 0, 0)), _row_rev(t, Z_W, nt),
                  _full((4, QKV_W)), _full((1, LANES)), _full((1, LANES)), _full((1, LANES))],
        out_specs=[_row_rev(t, QKV_W, nt), _row_rev(t, Z_W, nt), _row_rev(t, LANES, nt),
                   _full((4, QKV_W)), _full((1, LANES)), _full((1, LANES)), _full((1, LANES))],
        out_shape=[_sds((s, QKV_W), BF16), _sds((s, Z_W), BF16), _sds((s, LANES), BF16),
                   _sds((4, QKV_W)), _sds((1, LANES)), _sds((1, LANES)), _sds((1, LANES))],
        scratch_shapes=[pltpu.VMEM((HEADS, DK, DK), F32), pltpu.VMEM((HALO, QKV_W), F32)],
        compiler_params=_params())(p, p, p, p, sst, doa, cq, a_row, dtb_row, nw_row)


def proj_bwd(x, dqkv, dz, duv, dga, dgb, dba, dxd, w, name):
    s = x.shape[0]
    t = _tile(s)
    nt = s // t

    def body(x_ref, dqkv_ref, dz_ref, duv_ref, dga_ref, dgb_ref, dba_ref, dxd_ref, w_hbm,
             dx_ref, dw_hbm, w_v, dw_v):
        i = pl.program_id(0)
        first = i == 0

        @pl.when(first)
        def _():
            pltpu.sync_copy(w_hbm, w_v)

        dp = jnp.concatenate([dqkv_ref[...], dz_ref[...], duv_ref[...], dga_ref[...], dgb_ref[...], dba_ref[...]], axis=1)
        dx_ref[...] = dxd_ref[...] + _mm_nt(dp, w_v[...])
        _acc_tn(dw_v, x_ref[...].astype(BF16), dp, first, 1024)

        @pl.when(i == nt - 1)
        def _():
            pltpu.sync_copy(dw_v, dw_hbm)

    return pl.pallas_call(
        body, grid=(nt,), name=name,
        in_specs=[_row(t, D), _row(t, QKV_W), _row(t, Z_W), _row(t, 2 * SG_W), _row(t, D), _row(t, D), _row(t, LANES),
                  _row(t, D), ANY],
        out_specs=[_row(t, D), ANY], out_shape=[_sds((s, D)), _sds((D, IN_COLS_PAD))],
        scratch_shapes=[pltpu.VMEM((D, IN_COLS_PAD), BF16), pltpu.VMEM((D, IN_COLS_PAD), F32)],
        compiler_params=_params())(x, dqkv, dz, duv, dga, dgb, dba, dxd, w)


def _rows_block(rows, cols):
    cap = max(HALO, (2 * 1024 * 1024) // (cols * 4))
    for cand in range(min(rows, cap) // HALO * HALO, HALO - 1, -HALO):
        if rows % cand == 0:
            return cand
    return rows


def adam_call(w, g, m, v, name):
    rows, cols = w.shape
    tr = _rows_block(rows, cols)
    c1 = 1.0 - ADAM_B1 ** ADAM_STEP
    c2 = 1.0 - ADAM_B2 ** ADAM_STEP

    def body(w_ref, g_ref, m_ref, v_ref, d_ref, nm_ref, nv_ref):
        gv = g_ref[...]
        nm = ADAM_B1 * m_ref[...] + (1.0 - ADAM_B1) * gv
        nv = ADAM_B2 * v_ref[...] + (1.0 - ADAM_B2) * (gv * gv)
        d_ref[...] = -ADAM_LR * ((nm / c1) / (jnp.sqrt(nv / c2) + ADAM_EPS) + ADAM_WD * w_ref[...])
        nm_ref[...] = nm
        nv_ref[...] = nv

    spec = pl.BlockSpec((tr, cols), lambda i: (i, 0))
    return pl.pallas_call(
        body, grid=(rows // tr,), name=name, in_specs=[spec] * 4, out_specs=[spec] * 3,
        out_shape=[_sds((rows, cols))] * 3, compiler_params=_params())(w, g, m, v)


def _place():
    return lax.axis_index("x"), lax.axis_index("y"), lax.axis_index("c")


def _other_chips(x, y):
    return [(1 - x, y), (x, 1 - y), (1 - x, 1 - y)]


def _remote(src, dst, send_sem, recv_sem, to):
    return pltpu.make_async_remote_copy(src_ref=src, dst_ref=dst, send_sem=send_sem, recv_sem=recv_sem,
                                        device_id=to, device_id_type=MESH)


def gather_weights(shards, name):
    n = len(shards)

    def body(*refs):
        ins, outs = refs[:n], refs[n:2 * n]
        send_sems, recv_sems, local_sems = refs[2 * n:]
        x, y, c = _place()
        me = 2 * x + y
        sibling = (x, y, 1 - c)
        chips = _other_chips(x, y)

        def copy(t, k, slot, layer, to, src=None):
            dst = outs[t].at[slot, layer]
            return _remote(dst if src is None else src, dst, send_sems.at[6 * t + k], recv_sems.at[6 * t + k], to)

        mine = [pltpu.make_async_copy(ins[t].at[l], outs[t].at[me, l], local_sems.at[2 * t + l])
                for t in range(n) for l in range(2)]
        for cp in mine:
            cp.start()
        first = [copy(t, k, me, c, (cx, cy, c), src=ins[t].at[c]) for k, (cx, cy) in enumerate(chips) for t in range(n)]
        for cp in first:
            cp.start()
        passed = []
        for k, (cx, cy) in enumerate(chips):
            for t in range(n):
                copy(t, k, 2 * cx + cy, c, (x, y, c)).wait_recv()
                passed.append(copy(t, 3 + k, 2 * cx + cy, c, sibling))
                passed[-1].start()
        for k, (cx, cy) in enumerate(chips):
            for t in range(n):
                copy(t, 3 + k, 2 * cx + cy, 1 - c, (x, y, c)).wait_recv()
        for cp in first + passed:
            cp.wait_send()
        for cp in mine:
            cp.wait()

    return pl.pallas_call(
        body, name=name, in_specs=[ANY] * n, out_specs=[ANY] * n,
        out_shape=[_sds((N_CHIPS,) + a.shape, a.dtype) for a in shards],
        scratch_shapes=[pltpu.SemaphoreType.DMA((6 * n,)), pltpu.SemaphoreType.DMA((6 * n,)),
                        pltpu.SemaphoreType.DMA((2 * n,))],
    )(*shards)


def pair_exchange(layer0, layer1, name):
    n = len(layer0)

    def body(*refs):
        srcs = (refs[:n], refs[n:2 * n])
        outs = refs[2 * n:3 * n]
        send_sems, recv_sems = refs[3 * n:]
        x, y, c = _place()
        for core in range(2):
            @pl.when(c == core)
            def _(core=core):
                cps = [_remote(srcs[1 - core][t], outs[t], send_sems.at[t], recv_sems.at[t], (x, y, 1 - c))
                       for t in range(n)]
                for cp in cps:
                    cp.start()
                for cp in cps:
                    cp.wait()

    return pl.pallas_call(
        body, name=name, in_specs=[ANY] * (2 * n), out_specs=[ANY] * n,
        out_shape=[_sds(a.shape) for a in layer0],
        scratch_shapes=[pltpu.SemaphoreType.DMA((n,)), pltpu.SemaphoreType.DMA((n,))],
    )(*layer0, *layer1)


def pair_add(a0, a1, theirs, c_vec, name):
    rows, cols = a0.shape
    tr = _rows_block(rows, cols)

    def body(c_ref, a0_ref, a1_ref, b_ref, o_ref):
        o_ref[...] = (jnp.where(c_ref[0] == 0, a0_ref[...], a1_ref[...]) + b_ref[...]).astype(BF16)

    def of_core(core):
        return pl.BlockSpec((tr, cols), lambda i, c: (jnp.where(c[0] == core, i, 0), 0))

    spec = pl.BlockSpec((tr, cols), lambda i, c: (i, 0))
    grid_spec = pltpu.PrefetchScalarGridSpec(
        num_scalar_prefetch=1, grid=(rows // tr,), in_specs=[of_core(0), of_core(1), spec], out_specs=spec)
    return pl.pallas_call(body, grid_spec=grid_spec, name=name, out_shape=_sds((rows, cols), BF16),
                          compiler_params=_params())(c_vec, a0, a1, theirs)


def scatter_chips(srcs, pieces, shard_shapes, name):
    n_src, n_t = len(srcs), len(pieces)

    def body(*refs):
        src_refs = refs[:n_src]
        outs = refs[n_src:n_src + n_t]
        send_sems, recv_sems, local_sems = refs[n_src + n_t:]
        x, y, c = _place()
        me = 2 * x + y

        def piece(t, k):
            idx, lead, rows, cols = pieces[t][k]
            ref = src_refs[idx]
            if lead is not None:
                ref = ref.at[lead]
            if rows is not None:
                ref = ref.at[pl.ds(rows[0], rows[1]), :]
            if cols is not None:
                ref = ref.at[:, pl.ds(cols[0], cols[1])]
            return ref

        def local(t, k):
            return pltpu.make_async_copy(piece(t, k), outs[t].at[k], local_sems.at[t])

        for k in range(N_CHIPS):
            @pl.when(me == k)
            def _(k=k):
                for t in range(n_t):
                    local(t, k).start()

            @pl.when(me != k)
            def _(k=k):
                for t in range(n_t):
                    _remote(piece(t, k), outs[t].at[me], send_sems.at[N_CHIPS * t + k],
                            recv_sems.at[N_CHIPS * t + me], (k // 2, k % 2, c)).start()

        for k in range(N_CHIPS):
            @pl.when(me != k)
            def _(k=k):
                for t in range(n_t):
                    cp = _remote(piece(t, k), outs[t].at[k], send_sems.at[N_CHIPS * t + k],
                                 recv_sems.at[N_CHIPS * t + k], (x, y, c))
                    cp.wait_recv()
                    cp.wait_send()

            @pl.when(me == k)
            def _(k=k):
                for t in range(n_t):
                    local(t, k).wait()

    return pl.pallas_call(
        body, name=name, in_specs=[ANY] * n_src, out_specs=[ANY] * n_t,
        out_shape=[_sds((N_CHIPS,) + tuple(shp), srcs[0].dtype) for shp in shard_shapes],
        scratch_shapes=[pltpu.SemaphoreType.DMA((N_CHIPS * n_t,)), pltpu.SemaphoreType.DMA((N_CHIPS * n_t,)),
                        pltpu.SemaphoreType.DMA((n_t,))],
    )(*srcs)


def chips_add(recv, c_vec, name):
    n, a, b = recv.shape
    tr = _rows_block(a, b)

    def body(c_ref, r0, r1, r2, r3, o_ref):
        o_ref[...] = ((r0[...].astype(F32) + r1[...].astype(F32)) + r2[...].astype(F32)) + r3[...].astype(F32)

    grid_spec = pltpu.PrefetchScalarGridSpec(
        num_scalar_prefetch=1, grid=(a // tr,),
        in_specs=[pl.BlockSpec((None, tr, b), lambda i, c, k=k: (k, i, 0)) for k in range(n)],
        out_specs=pl.BlockSpec((None, tr, b), lambda i, c: (c[0], i, 0)))
    return pl.pallas_call(body, grid_spec=grid_spec, name=name, out_shape=_sds((2, a, b)),
                          compiler_params=_params())(c_vec, *([recv] * n))


def pair_join(bufs, name):
    n = len(bufs)

    def body(*refs):
        ins, outs = refs[:n], refs[n:2 * n]
        send_sems, recv_sems = refs[2 * n:]
        x, y, c = _place()
        cps = [_remote(ins[t].at[c], outs[t].at[c], send_sems.at[t], recv_sems.at[t], (x, y, 1 - c)) for t in range(n)]
        for cp in cps:
            cp.start()
        for t in range(n):
            cps[t].wait_send()
            _remote(ins[t].at[c], outs[t].at[1 - c], send_sems.at[t], recv_sems.at[t], (x, y, c)).wait_recv()

    return pl.pallas_call(
        body, name=name, in_specs=[ANY] * n, out_specs=[ANY] * n, out_shape=[_sds(a.shape) for a in bufs],
        input_output_aliases={t: t for t in range(n)},
        scratch_shapes=[pltpu.SemaphoreType.DMA((n,)), pltpu.SemaphoreType.DMA((n,))],
    )(*bufs)


def allsum_small(v, name):
    rows, lanes = v.shape
    n_dev = 8

    def body(v_ref, out_ref, buf, send_sems, recv_sems):
        x, y, c = _place()
        me, sibling = (x, y, c), (x, y, 1 - c)
        chips = _other_chips(x, y)

        def slot(px, py, pc):
            return buf.at[4 * px + 2 * py + pc]

        def copy(k, block, to, src=None):
            return pltpu.make_async_remote_copy(
                src_ref=slot(*block) if src is None else src, dst_ref=slot(*block),
                send_sem=send_sems.at[k], recv_sem=recv_sems.at[k], device_id=to, device_id_type=MESH)

        slot(*me)[...] = v_ref[...]
        first = [copy(0, me, sibling, src=v_ref)]
        first += [copy(1 + k, me, (*chip, c), src=v_ref) for k, chip in enumerate(chips)]
        for cp in first:
            cp.start()
        passed = [copy(4 + k, (*chip, c), sibling) for k, chip in enumerate(chips)]
        for k, chip in enumerate(chips):
            copy(1 + k, (*chip, c), me).wait_recv()
            passed[k].start()
        copy(0, sibling, me).wait_recv()
        for k, chip in enumerate(chips):
            copy(4 + k, (*chip, 1 - c), me).wait_recv()
        for cp in first + passed:
            cp.wait_send()
        acc = buf[0]
        for d in range(1, n_dev):
            acc = acc + buf[d]
        out_ref[...] = acc

    vm = pl.BlockSpec(memory_space=pltpu.VMEM)
    return pl.pallas_call(
        body, name=name, in_specs=[vm], out_specs=vm, out_shape=_sds((rows, lanes)),
        scratch_shapes=[pltpu.VMEM((n_dev, rows, lanes), F32), pltpu.SemaphoreType.DMA((7,)), pltpu.SemaphoreType.DMA((7,))],
        compiler_params=pltpu.CompilerParams(vmem_limit_bytes=VMEM_LIMIT),
    )(v)


BIG = ("w_in", "w_branch_a", "w_branch_b", "w_out", "w_up", "w_down")
CONV = ("conv_qkv", "conv_ffn")
REPL =("a_log", "dt_bias", "dn_norm_w", "sg_ln_g", "sg_ln_b", "w_spatial", "b_spatial", "ln1_g", "ln1_b", "ln2_g", "ln2_b")


def _pad_rows(flat, mult):
    n = flat.shape[0]
    unit = mult * LANES
    total = -(-n // unit) * unit
    return jnp.pad(flat, (0, total - n)).reshape(total // LANES, LANES)


def _pack(arrs, mult):
    return _pad_rows(jnp.concatenate([a.reshape(-1) for a in arrs]), mult)


def _unpack(flat, shapes):
    out, off = [], 0
    for shp in shapes:
        n = math.prod(shp)
        out.append(flat[off:off + n].reshape(shp))
        off += n
    return out


def kernel(x, w_in, conv_qkv, a_log, dt_bias, dn_norm_w, w_branch_a, sg_ln_g, sg_ln_b, w_spatial, b_spatial, w_branch_b, w_out, ln1_g, ln1_b, w_up, conv_ffn, w_down, ln2_g, ln2_b, loss_target, m_w_in, m_conv_qkv, m_a_log, m_dt_bias, m_dn_norm_w, m_w_branch_a, m_sg_ln_g, m_sg_ln_b, m_w_spatial, m_b_spatial, m_w_branch_b, m_w_out, m_ln1_g, m_ln1_b, m_w_up, m_conv_ffn, m_w_down, m_ln2_g, m_ln2_b, v_w_in, v_conv_qkv, v_a_log, v_dt_bias, v_dn_norm_w, v_w_branch_a, v_sg_ln_g, v_sg_ln_b, v_w_spatial, v_b_spatial, v_w_branch_b, v_w_out, v_ln1_g, v_ln1_b, v_w_up, v_conv_ffn, v_w_down, v_ln2_g, v_ln2_b):
    names = ("w_in", "conv_qkv", "a_log", "dt_bias", "dn_norm_w", "w_branch_a", "sg_ln_g", "sg_ln_b", "w_spatial",
             "b_spatial", "w_branch_b", "w_out", "ln1_g", "ln1_b", "w_up", "conv_ffn", "w_down", "ln2_g", "ln2_b")
    w = dict(zip(names, (w_in, conv_qkv, a_log, dt_bias, dn_norm_w, w_branch_a, sg_ln_g, sg_ln_b, w_spatial,
                         b_spatial, w_branch_b, w_out, ln1_g, ln1_b, w_up, conv_ffn, w_down, ln2_g, ln2_b)))
    m = dict(zip(names, (m_w_in, m_conv_qkv, m_a_log, m_dt_bias, m_dn_norm_w, m_w_branch_a, m_sg_ln_g, m_sg_ln_b,
                         m_w_spatial, m_b_spatial, m_w_branch_b, m_w_out, m_ln1_g, m_ln1_b, m_w_up, m_conv_ffn,
                         m_w_down, m_ln2_g, m_ln2_b)))
    v = dict(zip(names, (v_w_in, v_conv_qkv, v_a_log, v_dt_bias, v_dn_norm_w, v_w_branch_a, v_sg_ln_g, v_sg_ln_b,
                         v_w_spatial, v_b_spatial, v_w_branch_b, v_w_out, v_ln1_g, v_ln1_b, v_w_up, v_conv_ffn,
                         v_w_down, v_ln2_g, v_ln2_b)))
    chip = 2 * lax.axis_index("x") + lax.axis_index("y")
    s = x.shape[1]
    xs = x.reshape(s, D)
    tgt = loss_target.reshape(s, D)

    big_names, conv_names = list(BIG), list(CONV)
    got = dict(zip(big_names + conv_names, gather_weights(
        [w[n].astype(BF16) for n in big_names] + [w[n] for n in conv_names], "gather_weights")))

    def lane_row(vec, off):
        return jnp.zeros((1, LANES), F32).at[0, off:off + vec.shape[0]].set(vec)

    def side_by_side(name, l):
        return jnp.concatenate([got[name][k, l] for k in range(N_CHIPS)], axis=1)

    layers = []
    for l in range(DEPTH):
        wi = side_by_side("w_in", l)
        layers.append(dict(
            w_in=jnp.concatenate([wi[:, :2048], wi[:, 2056:3080], wi[:, 3080:5128], wi[:, 2048:2056],
                                  jnp.zeros((D, IN_COLS_PAD - 5128), BF16)], axis=1),
            cq=side_by_side("conv_qkv", l),
            a_row=lane_row(w["a_log"][l], HEADS), dtb_row=lane_row(w["dt_bias"][l], HEADS),
            nw_row=w["dn_norm_w"][l].reshape(1, DK),
            lng=w["sg_ln_g"][l].reshape(1, SG_W), lnb=w["sg_ln_b"][l].reshape(1, SG_W),
            w_s=w["w_spatial"][l], bs_t=jnp.zeros((LANES, LANES), F32).at[:, :4].set(w["b_spatial"][l].T),
            wa=side_by_side("w_branch_a", l), wb=side_by_side("w_branch_b", l),
            wo=got["w_out"][:, l].reshape(D, D),
            g1=w["ln1_g"][l].reshape(1, D), b1=w["ln1_b"][l].reshape(1, D),
            cf=got["conv_ffn"][:, l],
            g2=w["ln2_g"][l].reshape(1, D), b2=w["ln2_b"][l].reshape(1, D)))

    saved = []
    h_in = xs
    for l, p in enumerate(layers):
        proj = proj_fwd(h_in, p["w_in"], f"proj_fwd{l}")
        oa, sst = dn_fwd(proj, p["cq"], p["a_row"], p["dtb_row"], p["nw_row"], f"dn_fwd{l}")
        ob = sg_fwd(proj, p["lng"], p["lnb"], p["w_s"], p["bs_t"], f"sg_fwd{l}")
        x1 = merge_fwd(oa, ob, proj, h_in, p["wa"], p["wb"], p["wo"], p["g1"], p["b1"], f"merge_fwd{l}")
        pre2, x2 = ffn_fwd(x1, got["w_up"], p["cf"], got["w_down"], l, p["g2"], p["b2"], f"ffn_fwd{l}")
        saved.append(dict(x=h_in, proj=proj, oa=oa, ob=ob, sst=sst, x1=x1, pre2=pre2))
        h_in = x2

    dy, loss_part = loss_call(h_in, tgt, "loss")
    loss = lax.psum(loss_part[0, 0], ("x", "y", "c"))

    small_names = conv_names + list(REPL)
    grads = {n: [None] * DEPTH for n in small_names}
    big_grads = [None] * DEPTH
    for l in reversed(range(DEPTH)):
        p, a = layers[l], saved[l]
        dpre2, dg2, db2 = ln_bwd(a["pre2"], dy, p["g2"], p["b2"], f"ln2_bwd{l}")
        dx1, dwup0, dcf0, dwdn0 = ffn_bwd(a["x1"], dpre2, dpre2, ALPHA, got["w_up"], p["cf"], got["w_down"], l, 0, f"ffn_bwd{l}a")
        dx1, dwup1, dcf1, dwdn1 = ffn_bwd(a["x1"], dpre2, dx1, 1.0, got["w_up"], p["cf"], got["w_down"], l, 1, f"ffn_bwd{l}b")
        doa, dob, dga, dgb, dxd, dwa, dwb, dwo, dg1, db1 = merge_bwd(
            a["oa"], a["ob"], a["proj"], a["x"], dx1, p["wa"], p["wb"], p["wo"], p["g1"], p["b1"], f"merge_bwd{l}")
        duv, dlng, dlnb, dws, dbs = sg_bwd(a["proj"], dob, p["lng"], p["lnb"], p["w_s"], p["bs_t"], f"sg_bwd{l}")
        dqkv, dz, dba, dcq, da, ddtb, dnw = dn_bwd(a["proj"], a["sst"], doa, p["cq"], p["a_row"], p["dtb_row"],
                                                   p["nw_row"], f"dn_bwd{l}")
        dy, dwi = proj_bwd(a["x"], dqkv, dz, duv, dga, dgb, dba, dxd, p["w_in"], f"proj_bwd{l}")

        big_grads[l] = [dwi, dwa, dwb, dwo, dwup0.reshape(2 * D, FFN_HALF), dwup1.reshape(2 * D, FFN_HALF), dwdn0, dwdn1]
        grads["conv_qkv"][l] = dcq
        grads["conv_ffn"][l] = jnp.concatenate([dcf0[0], dcf1[0], dcf0[1], dcf1[1]], axis=1)
        grads["a_log"][l] = da[0, HEADS:2 * HEADS]
        grads["dt_bias"][l] = ddtb[0, HEADS:2 * HEADS]
        grads["dn_norm_w"][l] = dnw[0]
        grads["sg_ln_g"][l] = dlng[0]
        grads["sg_ln_b"][l] = dlnb[0]
        grads["w_spatial"][l] = dws
        grads["b_spatial"][l] = dbs[:, :4].T
        grads["ln1_g"][l] = dg1[0]
        grads["ln1_b"][l] = db1[0]
        grads["ln2_g"][l] = dg2[0]
        grads["ln2_b"][l] = db2[0]
    grad_x = dy.reshape(x.shape)
    g_full = {n: jnp.stack(grads[n]) for n in small_names}

    c_vec = jnp.stack([lax.axis_index("c")]).astype(jnp.int32)
    theirs = pair_exchange(big_grads[0], big_grads[1], "reduce_pair")
    tags = ("w_in", "w_a", "w_b", "w_out", "w_up0", "w_up1", "w_dn0", "w_dn1")
    pin, pa, pb, po, pup0, pup1, pdn0, pdn1 = [
        pair_add(a0, a1, th, c_vec, f"reduce_pair_add_{tag}")
        for tag, a0, a1, th in zip(tags, big_grads[0], big_grads[1], theirs)]
    natural = jnp.concatenate([pin[:, :2048], pin[:, C_BA:C_BA + 8], pin[:, 2048:C_BA]], axis=1)
    srcs = [jnp.stack(jnp.split(natural, N_CHIPS, axis=1)), pa, pb, po,
            pup0.reshape(2, D, FFN_HALF), pup1.reshape(2, D, FFN_HALF), pdn0, pdn1]
    ab_cols, out_rows = w["w_branch_a"].shape[2], w["w_out"].shape[1]
    pieces = [
        [(0, k, None, None) for k in range(N_CHIPS)],
        [(1, None, None, (k * ab_cols, ab_cols)) for k in range(N_CHIPS)],
        [(2, None, None, (k * ab_cols, ab_cols)) for k in range(N_CHIPS)],
        [(3, None, (k * out_rows, out_rows), None) for k in range(N_CHIPS)],
        [(4 + k % 2, k // 2, None, None) for k in range(N_CHIPS)],
        [(6 + k // 2, None, ((k % 2) * DN_SHARD, DN_SHARD), None) for k in range(N_CHIPS)],
    ]
    recv = scatter_chips(srcs, pieces, [w[n].shape[1:] for n in big_names], "reduce_chips")
    sums = [chips_add(r, c_vec, f"reduce_chips_add_{n}") for n, r in zip(big_names, recv)]
    g_shard = dict(zip(big_names, pair_join(sums, "reduce_join")))

    small = allsum_small(_pack([g_full[n] for n in small_names], 8), "reduce_small").reshape(-1)
    small_full = dict(zip(small_names, _unpack(small, [g_full[n].shape for n in small_names])))
    for n in conv_names:
        width = w[n].shape[2]
        g_shard[n] = lax.dynamic_slice_in_dim(small_full[n], chip * width, width, axis=2)
    for n in REPL:
        g_shard[n] = small_full[n]

    delta, new_m, new_v = {}, {}, {}
    for n in big_names:
        shp = w[n].shape
        two_d = (shp[0] * shp[1], shp[2])
        d_, m_, v_ = adam_call(w[n].reshape(two_d), g_shard[n].reshape(two_d), m[n].reshape(two_d), v[n].reshape(two_d), f"adam_{n}")
        delta[n], new_m[n], new_v[n] = d_.reshape(shp), m_.reshape(shp), v_.reshape(shp)
    shapes = [w[n].shape for n in small_names]
    packs = [_pack([src[n] for n in small_names], 8) for src in (w, g_shard, m, v)]
    outs = adam_call(*packs, "adam_small")
    for dst, o in zip((delta, new_m, new_v), outs):
        dst.update(zip(small_names, _unpack(o.reshape(-1), shapes)))

    return (loss, grad_x, *[g_shard[n] for n in names], *[delta[n] for n in names],
            *[new_m[n] for n in names], *[new_v[n] for n in names])
```

```python
import functools
import math

import jax
import jax.numpy as jnp
from jax import lax
from jax.experimental import pallas as pl
from jax.experimental.pallas import tpu as pltpu

F32 = jnp.float32
BF16 = jnp.bfloat16
HI = lax.Precision.HIGHEST
MID = lax.Precision.HIGH
MESH = pl.DeviceIdType.MESH

D = 1024
DEPTH = 2
HEADS = 4
DK = 128
CHUNK = 64
QKV_W = 1536
Z_W = 512
SG_W = 512
FFN = 2816
FFN_HALF = FFN // 2
N_CHIPS = 4
DN_SHARD = FFN // N_CHIPS
LN_EPS = 1e-5
RMS_EPS = 1e-6
L2_EPS = 1e-6
ALPHA = (2 * DEPTH) ** 0.25
ADAM_LR, ADAM_B1, ADAM_B2, ADAM_EPS, ADAM_WD, ADAM_STEP = 0.001, 0.9, 0.999, 1e-08, 0.01, 10

HALO = 16
LANES = 128
IN_COLS_PAD = 5248
C_Z, C_UV, C_GA, C_GB, C_BA = 1536, 2048, 3072, 4096, 5120
VMEM_LIMIT = 56 * 1024 * 1024


def _params(n_grid=1):
    return pltpu.CompilerParams(dimension_semantics=("arbitrary",) * n_grid, vmem_limit_bytes=VMEM_LIMIT)


def _mm(a, b):
    return jnp.dot(a.astype(BF16), b.astype(BF16), preferred_element_type=F32)


def _mm_nt(a, b):
    return lax.dot_general(a.astype(BF16), b.astype(BF16), (((1,), (1,)), ((), ())), preferred_element_type=F32)


def _mm_tn(a, b):
    return lax.dot_general(a.astype(BF16), b.astype(BF16), (((0,), (0,)), ((), ())), preferred_element_type=F32)


def _bdot(a, b, prec=MID):
    return lax.dot_general(a, b, (((2,), (1,)), ((0,), (0,))), precision=prec, preferred_element_type=F32)


def _bdot_nt(a, b, prec=MID):
    return lax.dot_general(a, b, (((2,), (2,)), ((0,), (0,))), precision=prec, preferred_element_type=F32)


def _bf16_dot(a, b, contract):
    return lax.dot_general(a.astype(BF16), b.astype(BF16), (contract, ((0,), (0,))), preferred_element_type=F32)


@jax.custom_vjp
def _fdot(a, b):
    return _bf16_dot(a, b, ((2,), (1,)))


def _fdot_fwd(a, b):
    return _fdot(a, b), (a, b)


def _fdot_bwd(res, ct):
    a, b = res
    return _bf16_dot(ct, b, ((2,), (2,))), _bf16_dot(a, ct, ((1,), (1,)))


_fdot.defvjp(_fdot_fwd, _fdot_bwd)


@jax.custom_vjp
def _fdot_nt(a, b):
    return _bf16_dot(a, b, ((2,), (2,)))


def _fdot_nt_fwd(a, b):
    return _fdot_nt(a, b), (a, b)


def _fdot_nt_bwd(res, ct):
    a, b = res
    return _bf16_dot(ct, b, ((2,), (1,))), _bf16_dot(ct, a, ((1,), (1,)))


_fdot_nt.defvjp(_fdot_nt_fwd, _fdot_nt_bwd)


@jax.custom_vjp
def _fdot_tn(a, b):
    return _bf16_dot(a, b, ((1,), (1,)))


def _fdot_tn_fwd(a, b):
    return _fdot_tn(a, b), (a, b)


def _fdot_tn_bwd(res, ct):
    a, b = res
    return _bf16_dot(b, ct, ((2,), (2,))), _bf16_dot(a, ct, ((2,), (1,)))


_fdot_tn.defvjp(_fdot_tn_fwd, _fdot_tn_bwd)


def _ln(x, g, b):
    mu = jnp.mean(x, axis=-1, keepdims=True)
    xc = x - mu
    var = jnp.mean(xc * xc, axis=-1, keepdims=True)
    return xc * lax.rsqrt(var + LN_EPS) * g + b


def _shift_rows(x, s):
    s = s % x.shape[0]
    return x if s == 0 else pltpu.roll(x, s, 0)


@jax.custom_vjp
def _conv(xcat, w):
    k_taps = len(w)
    y = None
    for k in range(k_taps):
        t = _shift_rows(xcat, k_taps - 1 - k)[HALO:] * w[k]
        y = t if y is None else y + t
    return y


def _conv_fwd(xcat, w):
    return _conv(xcat, w), (xcat, w)


def _conv_bwd(res, dy):
    xcat, w = res
    k_taps = len(w)
    dyp = jnp.concatenate([jnp.zeros((HALO, dy.shape[1]), dy.dtype), dy], axis=0)
    dx = None
    dws = []
    for k in range(k_taps):
        s = k_taps - 1 - k
        t = _shift_rows(dyp, -s) * w[k]
        dx = t if dx is None else dx + t
        dws.append(jnp.sum(_shift_rows(xcat, s)[HALO:] * dy, axis=0, keepdims=True))
    return dx, tuple(dws)


_conv.defvjp(_conv_fwd, _conv_bwd)


@jax.custom_vjp
def _tri_inv(l):
    n = l.shape[-1]
    r = lax.broadcasted_iota(jnp.int32, (n, n), 0)
    c = lax.broadcasted_iota(jnp.int32, (n, n), 1)
    eye = (r == c).astype(F32)
    p = eye - l
    lp = l
    steps = int(math.log2(n)) - 1
    for i in range(steps):
        dot = _bdot if i < 2 else functools.partial(_bf16_dot, contract=((2,), (1,)))
        lp = dot(lp, lp)
        p = p + dot(p, lp)
    return p


def _tri_inv_fwd(l):
    t = _tri_inv(l)
    return t, t


def _tri_inv_bwd(t, dt):
    tt = jnp.swapaxes(t, 1, 2)
    return (-_bdot(tt, _bdot(dt, tt)),)


_tri_inv.defvjp(_tri_inv_fwd, _tri_inv_bwd)


def _dn_glue(qkvcat, z, ba, s_in, cw, a_row, dtb_row, nw_row):
    t_rows = z.shape[0]
    nc = t_rows // CHUNK
    nb = nc * HEADS

    qkv = jax.nn.silu(_conv(qkvcat, cw))

    def chunks(t, off):
        return jnp.stack([t[n * CHUNK:(n + 1) * CHUNK, off + h * DK: off + (h + 1) * DK]
                          for n in range(nc) for h in range(HEADS)])

    q = chunks(qkv, 0)
    k = chunks(qkv, 512)
    v = chunks(qkv, 1024)
    q = q * lax.rsqrt(jnp.sum(q * q, axis=-1, keepdims=True) + L2_EPS) * (DK ** -0.5)
    k = k * lax.rsqrt(jnp.sum(k * k, axis=-1, keepdims=True) + L2_EPS)

    lane = lax.broadcasted_iota(jnp.int32, (LANES, HEADS * DK), 0)
    head_of_col = lax.broadcasted_iota(jnp.int32, (LANES, HEADS * DK), 1) // DK
    e_beta = (head_of_col == lane).astype(F32)
    e_g = (head_of_col + HEADS == lane).astype(F32)
    beta_l = jax.nn.sigmoid(ba)
    g_l = -jnp.exp(a_row) * jax.nn.softplus(ba + dtb_row)
    beta = chunks(jnp.dot(beta_l, e_beta, precision=HI, preferred_element_type=F32), 0)
    g = chunks(jnp.dot(g_l, e_g, precision=HI, preferred_element_type=F32), 0)

    r = lax.broadcasted_iota(jnp.int32, (CHUNK, CHUNK), 0)
    c = lax.broadcasted_iota(jnp.int32, (CHUNK, CHUNK), 1)
    causal = r >= c
    strict = r > c
    tril_b = jnp.broadcast_to(causal.astype(F32), (nb, CHUNK, CHUNK))
    gi_b = _bdot(tril_b, g, HI)
    gi = gi_b[:, :, :CHUNK]
    gj = jnp.swapaxes(gi, 1, 2)
    decay = jnp.where(causal, jnp.exp(jnp.where(causal, gi - gj, 0.0)), 0.0)
    kb = k * beta
    l_mat = jnp.where(strict, _fdot_nt(kb, k) * decay, 0.0)
    t_mat = _tri_inv(l_mat)
    e_gi = jnp.exp(gi_b)
    w_mat = _fdot(t_mat, kb * e_gi)
    u_mat = _fdot(t_mat, v * beta)
    a_qk = _fdot_nt(q, k) * decay
    q_g = q * e_gi
    gl_b = jnp.broadcast_to(jnp.sum(g, axis=1, keepdims=True), g.shape)
    k_d = k * jnp.exp(gl_b - gi_b)
    e_gl = jnp.exp(gl_b)
    g_last = jnp.concatenate([e_gl, e_gl], axis=1)

    state = s_in
    rows = []
    for n in range(nc):
        sl = slice(n * HEADS, (n + 1) * HEADS)
        u_new = u_mat[sl] - _fdot(w_mat[sl], state)
        o_n = _fdot(q_g[sl], state) + _fdot(a_qk[sl], u_new)
        state = state * g_last[sl] + _fdot_tn(k_d[sl], u_new)
        o_n = o_n * lax.rsqrt(jnp.mean(o_n * o_n, axis=-1, keepdims=True) + RMS_EPS) * nw_row
        z_n = jnp.stack([z[n * CHUNK:(n + 1) * CHUNK, h * DK:(h + 1) * DK] for h in range(HEADS)])
        o_n = o_n * jax.nn.silu(z_n)
        rows.append(jnp.concatenate([o_n[h] for h in range(HEADS)], axis=-1))
    return jnp.concatenate(rows, axis=0), state


def _sg_glue(uv, lng, lnb, w_s, bs_t):
    t_rows = uv.shape[0]
    y = jax.nn.gelu(uv)
    u = y[:, :SG_W]
    v = _ln(y[:, SG_W:], lng, lnb)
    r = lax.broadcasted_iota(jnp.int32, (LANES, LANES), 0)
    c = lax.broadcasted_iota(jnp.int32, (LANES, LANES), 1)
    wm = jnp.where(r >= c, w_s, 0.0)
    lane = lax.broadcasted_iota(jnp.int32, (LANES, SG_W), 0)
    group_of_col = lax.broadcasted_iota(jnp.int32, (LANES, SG_W), 1) // LANES
    e_grp = (group_of_col == lane).astype(F32)
    bias = jnp.dot(bs_t, e_grp, precision=HI, preferred_element_type=F32)
    outs = []
    for n in range(t_rows // LANES):
        vb = v[n * LANES:(n + 1) * LANES]
        vg = jnp.stack([vb[:, g * LANES:(g + 1) * LANES] for g in range(4)])
        mg = _fdot(wm, vg)
        mixed = jnp.concatenate([mg[g] for g in range(4)], axis=-1) + bias
        outs.append(u[n * LANES:(n + 1) * LANES] * mixed)
    return jnp.concatenate(outs, axis=0)


def _merge_glue(ga, gb, ya, yb):
    return jax.nn.sigmoid(ga) * ya + jax.nn.sigmoid(gb) * yb


def _res_ln_glue(x, r, g, b):
    return _ln(ALPHA * x + r, g, b)


def _ffn_glue(ua, ub, cwa, cwb):
    return jax.nn.silu(_conv(ua, cwa)) * _conv(ub, cwb)


def _row(t, c, col=0):
    return pl.BlockSpec((t, c), lambda i: (i, col))


def _row_rev(t, c, nt, col=0):
    return pl.BlockSpec((t, c), lambda i: (nt - 1 - i, col))


def _halo(t, c, nt=None):
    per = t // HALO
    if nt is None:
        return pl.BlockSpec((HALO, c), lambda i: (jnp.maximum(i * per - 1, 0), 0))
    return pl.BlockSpec((HALO, c), lambda i: (jnp.maximum((nt - 1 - i) * per - 1, 0), 0))


def _full(shape):
    nd = len(shape)
    return pl.BlockSpec(shape, lambda i: (0,) * nd)


ANY = pl.BlockSpec(memory_space=pl.ANY)


def _sds(shape, dtype=F32):
    return jax.ShapeDtypeStruct(shape, dtype)


def _tile(s, want=256):
    for t in (want, 256, 128):
        if s % t == 0:
            return t
    raise ValueError(f"sequence length {s} is not a multiple of 128")


def proj_fwd(x, w, name):
    s = x.shape[0]
    t = _tile(s)
    segs = [(0, 2048), (2048, 3072), (3072, 4096), (4096, 5120), (5120, IN_COLS_PAD)]

    def body(x_ref, w_ref, p_ref):
        xb = x_ref[...].astype(BF16)
        for lo, hi in segs:
            p_ref[:, lo:hi] = jnp.dot(xb, w_ref[:, lo:hi], preferred_element_type=F32)

    return pl.pallas_call(
        body, grid=(s // t,), name=name,
        in_specs=[_row(t, D), _full((D, IN_COLS_PAD))],
        out_specs=_row(t, IN_COLS_PAD),
        out_shape=_sds((s, IN_COLS_PAD)), compiler_params=_params())(x, w)


def dn_fwd(p, cq, a_row, dtb_row, nw_row, name, gather=()):
    s = p.shape[0]
    t = _tile(s)
    nt = s // t

    def body(qkv_ref, halo_ref, z_ref, ba_ref, cq_ref, a_ref, dtb_ref, nw_ref, o_ref, sst_ref, s_scr):
        i = pl.program_id(0)

        @pl.when(i == 0)
        def _():
            s_scr[...] = jnp.zeros_like(s_scr)

        halo = jnp.where(i == 0, 0.0, halo_ref[...])
        qkvcat = jnp.concatenate([halo, qkv_ref[...]], axis=0)
        cw = tuple(cq_ref[k:k + 1, :] for k in range(4))
        s_in = s_scr[...]
        sst_ref[0] = s_in
        o, s_out = _dn_glue(qkvcat, z_ref[...], ba_ref[...], s_in, cw, a_ref[...], dtb_ref[...], nw_ref[...])
        o_ref[...] = o.astype(BF16)
        s_scr[...] = s_out

    n = len(gather)
    return pl.pallas_call(
        _carrying_gather(body, 8, 2, gather, nt) if gather else body, grid=(nt,), name=name,
        in_specs=[_row(t, QKV_W), _halo(t, QKV_W), _row(t, Z_W, C_Z // Z_W), _row(t, LANES, C_BA // LANES),
                  _full((4, QKV_W)), _full((1, LANES)), _full((1, LANES)), _full((1, LANES))] + [ANY] * n,
        out_specs=[_row(t, Z_W), pl.BlockSpec((1, HEADS, DK, DK), lambda i: (i, 0, 0, 0))] + [ANY] * n,
        out_shape=[_sds((s, Z_W), BF16), _sds((nt, HEADS, DK, DK))] + _gathered(gather),
        scratch_shapes=[pltpu.VMEM((HEADS, DK, DK), F32)] + (_gather_scratch(n) if gather else []),
        compiler_params=_params())(p, p, p, p, cq, a_row, dtb_row, nw_row, *gather)


def sg_fwd(p, lng, lnb, w_s, bs_t, name):
    s = p.shape[0]
    t = _tile(s)

    def body(uv_ref, lng_ref, lnb_ref, ws_ref, bs_ref, o_ref):
        o_ref[...] = _sg_glue(uv_ref[...], lng_ref[...], lnb_ref[...], ws_ref[...], bs_ref[...]).astype(BF16)

    return pl.pallas_call(
        body, grid=(s // t,), name=name,
        in_specs=[_row(t, 2 * SG_W, C_UV // (2 * SG_W)), _full((1, SG_W)), _full((1, SG_W)),
                  _full((4, LANES, LANES)), _full((LANES, LANES))],
        out_specs=_row(t, SG_W), out_shape=_sds((s, SG_W), BF16), compiler_params=_params())(p, lng, lnb, w_s, bs_t)


def merge_fwd(oa, ob, p, x, wa, wb, wo, g1, b1, name):
    s = x.shape[0]
    t = _tile(s)

    def body(oa_ref, ob_ref, ga_ref, gb_ref, x_ref, wa_ref, wb_ref, wo_ref, g_ref, b_ref, x1_ref):
        ya = _mm(oa_ref[...], wa_ref[...])
        yb = _mm(ob_ref[...], wb_ref[...])
        h = _merge_glue(ga_ref[...], gb_ref[...], ya, yb)
        x1_ref[...] = _res_ln_glue(x_ref[...], _mm(h, wo_ref[...]), g_ref[...], b_ref[...])

    return pl.pallas_call(
        body, grid=(s // t,), name=name,
        in_specs=[_row(t, Z_W), _row(t, SG_W), _row(t, D, C_GA // D), _row(t, D, C_GB // D), _row(t, D),
                  _full((Z_W, D)), _full((SG_W, D)), _full((D, D)), _full((1, D)), _full((1, D))],
        out_specs=_row(t, D), out_shape=_sds((s, D)), compiler_params=_params())(oa, ob, p, p, x, wa, wb, wo, g1, b1)


def _load_ffn_weights(wup_hbm, wdn_hbm, wup_v, wdn_v, up_slots, dn_slots):
    for n, k in enumerate(up_slots):
        pltpu.sync_copy(wup_hbm.at[k], wup_v.at[n])
    for n, k in enumerate(dn_slots):
        pltpu.sync_copy(wdn_hbm.at[k], wdn_v.at[pl.ds(n * DN_SHARD, DN_SHARD)])


def ffn_fwd(x1, wup4, cf4, wdn4, g2, b2, name, gather=()):
    s = x1.shape[0]
    t = _tile(s)
    nt = s // t

    def body(x1_ref, halo_ref, wup_hbm, cf_ref, wdn_hbm, g_ref, b_ref, pre_ref, x2_ref, wup_v, wdn_v):
        i = pl.program_id(0)

        @pl.when(i == 0)
        def _():
            _load_ffn_weights(wup_hbm, wdn_hbm, wup_v, wdn_v, range(4), range(4))

        x1v = x1_ref[...]
        halo = jnp.where(i == 0, 0.0, halo_ref[...])
        x1cat = jnp.concatenate([halo, x1v], axis=0).astype(BF16)
        f = None
        for h in range(2):
            ua = jnp.dot(x1cat, wup_v[h], preferred_element_type=F32)
            ub = jnp.dot(x1cat, wup_v[2 + h], preferred_element_type=F32)
            cwa = tuple(cf_ref[h, k:k + 1, :] for k in range(3))
            cwb = tuple(cf_ref[2 + h, k:k + 1, :] for k in range(3))
            act = _ffn_glue(ua, ub, cwa, cwb)
            fh = _mm(act, wdn_v[h * FFN_HALF:(h + 1) * FFN_HALF, :])
            f = fh if f is None else f + fh
        pre = ALPHA * x1v + f
        pre_ref[...] = pre
        x2_ref[...] = _ln(pre, g_ref[...], b_ref[...])

    n = len(gather)
    return pl.pallas_call(
        _carrying_gather(body, 7, 2, gather, nt) if gather else body, grid=(nt,), name=name,
        in_specs=[_row(t, D), _halo(t, D), ANY, _full((4, 3, FFN_HALF)), ANY, _full((1, D)), _full((1, D))] + [ANY] * n,
        out_specs=[_row(t, D), _row(t, D)] + [ANY] * n, out_shape=[_sds((s, D)), _sds((s, D))] + _gathered(gather),
        scratch_shapes=[pltpu.VMEM((4, D, FFN_HALF), BF16), pltpu.VMEM((FFN, D), BF16)]
        + (_gather_scratch(n) if gather else []),
        compiler_params=_params())(x1, x1, wup4, cf4, wdn4, g2, b2, *gather)


def loss_call(y, tgt, name):
    s = y.shape[0]
    t = _tile(s)

    def body(y_ref, t_ref, dy_ref, loss_ref):
        @pl.when(pl.program_id(0) == 0)
        def _():
            loss_ref[...] = jnp.zeros_like(loss_ref)

        e = y_ref[...] - t_ref[...]
        dy_ref[...] = e * (1.0 / D)
        part = jnp.sum(jnp.sum(e * e, axis=1, keepdims=True), axis=0, keepdims=True) * (0.5 / D)
        loss_ref[...] += jnp.broadcast_to(part, loss_ref.shape)

    return pl.pallas_call(
        body, grid=(s // t,), name=name, in_specs=[_row(t, D), _row(t, D)],
        out_specs=[_row(t, D), _full((8, LANES))], out_shape=[_sds((s, D)), _sds((8, LANES))],
        compiler_params=_params())(y, tgt)


def _acc(ref, val, first):
    @pl.when(first)
    def _():
        ref[...] = val

    @pl.when(jnp.logical_not(first))
    def _():
        ref[...] += val


def _acc_tn(acc_ref, a, b, first, seg):
    n = b.shape[1]
    for lo in range(0, n, seg):
        hi = min(lo + seg, n)
        _acc(acc_ref.at[:, lo:hi], _mm_tn(a, b[:, lo:hi]), first)


def ln_bwd(pre, dy, g, b, name):
    s = pre.shape[0]
    t = _tile(s)

    def body(pre_ref, dy_ref, g_ref, b_ref, dpre_ref, dg_ref, db_ref):
        _, vjp = jax.vjp(_ln, pre_ref[...], g_ref[...], b_ref[...])
        dpre, dg, db = vjp(dy_ref[...])
        dpre_ref[...] = dpre
        first = pl.program_id(0) == 0
        _acc(dg_ref, dg, first)
        _acc(db_ref, db, first)

    return pl.pallas_call(
        body, grid=(s // t,), name=name, in_specs=[_row(t, D), _row(t, D), _full((1, D)), _full((1, D))],
        out_specs=[_row(t, D), _full((1, D)), _full((1, D))],
        out_shape=[_sds((s, D)), _sds((1, D)), _sds((1, D))], compiler_params=_params())(pre, dy, g, b)


def ffn_bwd(x1, df, acc_in, acc_scale, wup4, cf4, wdn4, h, name):
    s = x1.shape[0]
    t = _tile(s)
    nt = s // t

    def body(x1_ref, halo_ref, df_ref, acc_ref, wup_hbm, cf_ref, wdn_hbm,
             dx1_ref, dwup_hbm, dcf_ref, dwdn_hbm, wup_v, wdn_v, dwup_v, dwdn_v, carry):
        i = pl.program_id(0)
        j = nt - 1 - i
        first = i == 0

        @pl.when(first)
        def _():
            _load_ffn_weights(wup_hbm, wdn_hbm, wup_v, wdn_v, (h, 2 + h), (2 * h, 2 * h + 1))
            carry[...] = jnp.zeros_like(carry)

        halo = jnp.where(j == 0, 0.0, halo_ref[...])
        x1cat = jnp.concatenate([halo, x1_ref[...]], axis=0).astype(BF16)
        ua = jnp.dot(x1cat, wup_v[0], preferred_element_type=F32)
        ub = jnp.dot(x1cat, wup_v[1], preferred_element_type=F32)
        cwa = tuple(cf_ref[h, k:k + 1, :] for k in range(3))
        cwb = tuple(cf_ref[2 + h, k:k + 1, :] for k in range(3))
        act, vjp = jax.vjp(_ffn_glue, ua, ub, cwa, cwb)
        dfb = df_ref[...].astype(BF16)
        dact = _mm_nt(dfb, wdn_v[...])
        _acc_tn(dwdn_v, act.astype(BF16), dfb, first, 512)
        dua, dub, dcwa, dcwb = vjp(dact)
        x1b = x1cat[HALO:]
        dups = []
        for n, du in enumerate((dua, dub)):
            dups.append(jnp.concatenate([du[HALO:t], du[t:] + carry[n]], axis=0).astype(BF16))
            carry[n] = du[:HALO]
            _acc(dwup_v.at[n], _mm_tn(x1b, dups[n]), first)
        for k in range(3):
            _acc(dcf_ref.at[0, k:k + 1, :], dcwa[k], first)
            _acc(dcf_ref.at[1, k:k + 1, :], dcwb[k], first)
        dx1_ref[...] = acc_scale * acc_ref[...] + _mm_nt(dups[0], wup_v[0]) + _mm_nt(dups[1], wup_v[1])

        @pl.when(i == nt - 1)
        def _():
            pltpu.sync_copy(dwup_v, dwup_hbm)
            pltpu.sync_copy(dwdn_v, dwdn_hbm)

    return pl.pallas_call(
        body, grid=(nt,), name=name,
        in_specs=[_row_rev(t, D, nt), _halo(t, D, nt), _row_rev(t, D, nt), _row_rev(t, D, nt),
                  ANY, _full((4, 3, FFN_HALF)), ANY],
        out_specs=[_row_rev(t, D, nt), ANY, _full((2, 3, FFN_HALF)), ANY],
        out_shape=[_sds((s, D)), _sds((2, D, FFN_HALF)), _sds((2, 3, FFN_HALF)), _sds((FFN_HALF, D))],
        scratch_shapes=[pltpu.VMEM((2, D, FFN_HALF), BF16), pltpu.VMEM((FFN_HALF, D), BF16),
                        pltpu.VMEM((2, D, FFN_HALF), F32), pltpu.VMEM((FFN_HALF, D), F32),
                        pltpu.VMEM((2, HALO, FFN_HALF), F32)],
        compiler_params=_params())(x1, x1, df, acc_in, wup4, cf4, wdn4)


def merge_bwd(oa, ob, p, x, dx1, wa, wb, wo, g1, b1, name):
    s = x.shape[0]
    t = _tile(s)

    def body(oa_ref, ob_ref, ga_ref, gb_ref, x_ref, dx1_ref, wa_ref, wb_ref, wo_ref, g_ref, b_ref,
             doa_ref, dob_ref, dga_ref, dgb_ref, dx_ref, dwa_ref, dwb_ref, dwo_ref, dg_ref, db_ref):
        first = pl.program_id(0) == 0
        oa = oa_ref[...]
        ob = ob_ref[...]
        ya = _mm(oa, wa_ref[...])
        yb = _mm(ob, wb_ref[...])
        h, vjp1 = jax.vjp(_merge_glue, ga_ref[...], gb_ref[...], ya, yb)
        hb = h.astype(BF16)
        r = _mm(hb, wo_ref[...])
        _, vjp2 = jax.vjp(_res_ln_glue, x_ref[...], r, g_ref[...], b_ref[...])
        dx, dr, dg, db = vjp2(dx1_ref[...])
        dx_ref[...] = dx
        _acc(dg_ref, dg, first)
        _acc(db_ref, db, first)
        drb = dr.astype(BF16)
        dh = _mm_nt(drb, wo_ref[...])
        _acc(dwo_ref, _mm_tn(hb, drb), first)
        dga, dgb, dya, dyb = vjp1(dh)
        dga_ref[...] = dga.astype(BF16)
        dgb_ref[...] = dgb.astype(BF16)
        dyab = dya.astype(BF16)
        dybb = dyb.astype(BF16)
        doa_ref[...] = _mm_nt(dyab, wa_ref[...]).astype(BF16)
        dob_ref[...] = _mm_nt(dybb, wb_ref[...]).astype(BF16)
        _acc(dwa_ref, _mm_tn(oa, dyab), first)
        _acc(dwb_ref, _mm_tn(ob, dybb), first)

    return pl.pallas_call(
        body, grid=(s // t,), name=name,
        in_specs=[_row(t, Z_W), _row(t, SG_W), _row(t, D, C_GA // D), _row(t, D, C_GB // D), _row(t, D), _row(t, D),
                  _full((Z_W, D)), _full((SG_W, D)), _full((D, D)), _full((1, D)), _full((1, D))],
        out_specs=[_row(t, Z_W), _row(t, SG_W), _row(t, D), _row(t, D), _row(t, D),
                   _full((Z_W, D)), _full((SG_W, D)), _full((D, D)), _full((1, D)), _full((1, D))],
        out_shape=[_sds((s, Z_W), BF16), _sds((s, SG_W), BF16), _sds((s, D), BF16), _sds((s, D), BF16), _sds((s, D)),
                   _sds((Z_W, D)), _sds((SG_W, D)), _sds((D, D)), _sds((1, D)), _sds((1, D))],
        compiler_params=_params())(oa, ob, p, p, x, dx1, wa, wb, wo, g1, b1)


def sg_bwd(p, dob, lng, lnb, w_s, bs_t, name):
    s = p.shape[0]
    t = _tile(s)

    def body(uv_ref, dob_ref, lng_ref, lnb_ref, ws_ref, bs_ref, duv_ref, dlng_ref, dlnb_ref, dws_ref, dbs_ref):
        first = pl.program_id(0) == 0
        _, vjp = jax.vjp(_sg_glue, uv_ref[...], lng_ref[...], lnb_ref[...], ws_ref[...], bs_ref[...])
        duv, dlng, dlnb, dws, dbs = vjp(dob_ref[...].astype(F32))
        duv_ref[...] = duv.astype(BF16)
        _acc(dlng_ref, dlng, first)
        _acc(dlnb_ref, dlnb, first)
        _acc(dws_ref, dws, first)
        _acc(dbs_ref, dbs, first)

    return pl.pallas_call(
        body, grid=(s // t,), name=name,
        in_specs=[_row(t, 2 * SG_W, C_UV // (2 * SG_W)), _row(t, SG_W), _full((1, SG_W)), _full((1, SG_W)),
                  _full((4, LANES, LANES)), _full((LANES, LANES))],
        out_specs=[_row(t, 2 * SG_W), _full((1, SG_W)), _full((1, SG_W)), _full((4, LANES, LANES)), _full((LANES, LANES))],
        out_shape=[_sds((s, 2 * SG_W), BF16), _sds((1, SG_W)), _sds((1, SG_W)), _sds((4, LANES, LANES)), _sds((LANES, LANES))],
        compiler_params=_params())(p, dob, lng, lnb, w_s, bs_t)


def dn_bwd(p, sst, doa, cq, a_row, dtb_row, nw_row, name):
    s = p.shape[0]
    t = _tile(s)
    nt = s // t

    def body(qkv_ref, halo_ref, z_ref, ba_ref, sst_ref, doa_ref, cq_ref, a_ref, dtb_ref, nw_ref,
             dqkv_ref, dz_ref, dba_ref, dcq_ref, da_ref, ddtb_ref, dnw_ref, ds_scr, carry):
        i = pl.program_id(0)
        j = nt - 1 - i
        first = i == 0

        @pl.when(first)
        def _():
            ds_scr[...] = jnp.zeros_like(ds_scr)
            carry[...] = jnp.zeros_like(carry)

        halo = jnp.where(j == 0, 0.0, halo_ref[...])
        qkvcat = jnp.concatenate([halo, qkv_ref[...]], axis=0)
        cw = tuple(cq_ref[k:k + 1, :] for k in range(4))
        _, vjp = jax.vjp(_dn_glue, qkvcat, z_ref[...], ba_ref[...], sst_ref[0], cw, a_ref[...], dtb_ref[...], nw_ref[...])
        dqkvcat, dz, dba, ds_in, dcw, da, ddtb, dnw = vjp((doa_ref[...].astype(F32), ds_scr[...]))
        ds_scr[...] = ds_in
        dz_ref[...] = dz.astype(BF16)
        dba_ref[...] = dba.astype(BF16)
        dtile = dqkvcat[HALO:]
        dqkv_ref[...] = dtile.astype(BF16)
        dqkv_ref[t - HALO:t, :] = (dtile[t - HALO:] + carry[...]).astype(BF16)
        carry[...] = dqkvcat[:HALO]
        for k in range(4):
            _acc(dcq_ref.at[k:k + 1, :], dcw[k], first)
        _acc(da_ref, da, first)
        _acc(ddtb_ref, ddtb, first)
        _acc(dnw_ref, dnw, first)

    return pl.pallas_call(
        body, grid=(nt,), name=name,
        in_specs=[_row_rev(t, QKV_W, nt), _halo(t, QKV_W, nt), _row_rev(t, Z_W, nt, C_Z // Z_W),
                  _row_rev(t, LANES, nt, C_BA // LANES),
                  pl.BlockSpec((1, HEADS, DK, DK), lambda i: (nt - 1 - i, 0, 0, 0)), _row_rev(t, Z_W, nt),
                  _full((4, QKV_W)), _full((1, LANES)), _full((1, LANES)), _full((1, LANES))],
        out_specs=[_row_rev(t, QKV_W, nt), _row_rev(t, Z_W, nt), _row_rev(t, LANES, nt),
                   _full((4, QKV_W)), _full((1, LANES)), _full((1, LANES)), _full((1, LANES))],
        out_shape=[_sds((s, QKV_W), BF16), _sds((s, Z_W), BF16), _sds((s, LANES), BF16),
                   _sds((4, QKV_W)), _sds((1, LANES)), _sds((1, LANES)), _sds((1, LANES))],
        scratch_shapes=[pltpu.VMEM((HEADS, DK, DK), F32), pltpu.VMEM((HALO, QKV_W), F32)],
        compiler_params=_params())(p, p, p, p, sst, doa, cq, a_row, dtb_row, nw_row)


def proj_bwd(x, dqkv, dz, duv, dga, dgb, dba, dxd, w, name):
    s = x.shape[0]
    t = _tile(s)
    nt = s // t

    def body(x_ref, dqkv_ref, dz_ref, duv_ref, dga_ref, dgb_ref, dba_ref, dxd_ref, w_hbm,
             dx_ref, dw_hbm, w_v, dw_v):
        i = pl.program_id(0)
        first = i == 0

        @pl.when(first)
        def _():
            pltpu.sync_copy(w_hbm, w_v)

        dp = jnp.concatenate([dqkv_ref[...], dz_ref[...], duv_ref[...], dga_ref[...], dgb_ref[...], dba_ref[...]], axis=1)
        dx_ref[...] = dxd_ref[...] + _mm_nt(dp, w_v[...])
        _acc_tn(dw_v, x_ref[...].astype(BF16), dp, first, 1024)

        @pl.when(i == nt - 1)
        def _():
            pltpu.sync_copy(dw_v, dw_hbm)

    return pl.pallas_call(
        body, grid=(nt,), name=name,
        in_specs=[_row(t, D), _row(t, QKV_W), _row(t, Z_W), _row(t, 2 * SG_W), _row(t, D), _row(t, D), _row(t, LANES),
                  _row(t, D), ANY],
        out_specs=[_row(t, D), ANY], out_shape=[_sds((s, D)), _sds((D, IN_COLS_PAD))],
        scratch_shapes=[pltpu.VMEM((D, IN_COLS_PAD), BF16), pltpu.VMEM((D, IN_COLS_PAD), F32)],
        compiler_params=_params())(x, dqkv, dz, duv, dga, dgb, dba, dxd, w)


def _rows_block(rows, cols):
    cap = max(HALO, (2 * 1024 * 1024) // (cols * 4))
    for cand in range(min(rows, cap) // HALO * HALO, HALO - 1, -HALO):
        if rows % cand == 0:
            return cand
    return rows


def adam_call(w, g, m, v, name):
    rows, cols = w.shape
    tr = _rows_block(rows, cols)
    c1 = 1.0 - ADAM_B1 ** ADAM_STEP
    c2 = 1.0 - ADAM_B2 ** ADAM_STEP

    def body(w_ref, g_ref, m_ref, v_ref, d_ref, nm_ref, nv_ref):
        gv = g_ref[...]
        nm = ADAM_B1 * m_ref[...] + (1.0 - ADAM_B1) * gv
        nv = ADAM_B2 * v_ref[...] + (1.0 - ADAM_B2) * (gv * gv)
        d_ref[...] = -ADAM_LR * ((nm / c1) / (jnp.sqrt(nv / c2) + ADAM_EPS) + ADAM_WD * w_ref[...])
        nm_ref[...] = nm
        nv_ref[...] = nv

    spec = pl.BlockSpec((tr, cols), lambda i: (i, 0))
    return pl.pallas_call(
        body, grid=(rows // tr,), name=name, in_specs=[spec] * 4, out_specs=[spec] * 3,
        out_shape=[_sds((rows, cols))] * 3, compiler_params=_params())(w, g, m, v)


def _place():
    return lax.axis_index("x"), lax.axis_index("y"), lax.axis_index("c")


def _other_chips(x, y):
    return [(1 - x, y), (x, 1 - y), (1 - x, 1 - y)]


def _remote(src, dst, send_sem, recv_sem, to):
    return pltpu.make_async_remote_copy(src_ref=src, dst_ref=dst, send_sem=send_sem, recv_sem=recv_sem,
                                        device_id=to, device_id_type=MESH)


class _Gather:
    def __init__(self, ins, outs, send_sems, recv_sems, local_sems):
        self.ins, self.outs, self.n = ins, outs, len(ins)
        self.send_sems, self.recv_sems, self.local_sems = send_sems, recv_sems, local_sems
        self.x, self.y, self.c = _place()
        self.me = 2 * self.x + self.y
        self.chips = _other_chips(self.x, self.y)

    def _copy(self, t, k, slot, part, to, src=None):
        dst = self.outs[t].at[slot, part]
        return _remote(dst if src is None else src, dst, self.send_sems.at[6 * t + k], self.recv_sems.at[6 * t + k], to)

    def _mine(self):
        return [pltpu.make_async_copy(self.ins[t].at[p], self.outs[t].at[self.me, p], self.local_sems.at[2 * t + p])
                for t in range(self.n) for p in range(2)]

    def _first(self):
        return [self._copy(t, k, self.me, self.c, (cx, cy, self.c), src=self.ins[t].at[self.c])
                for k, (cx, cy) in enumerate(self.chips) for t in range(self.n)]

    def start(self):
        for cp in self._mine() + self._first():
            cp.start()

    def finish(self):
        x, y, c = self.x, self.y, self.c
        passed = []
        for k, (cx, cy) in enumerate(self.chips):
            for t in range(self.n):
                self._copy(t, k, 2 * cx + cy, c, (x, y, c)).wait_recv()
                passed.append(self._copy(t, 3 + k, 2 * cx + cy, c, (x, y, 1 - c)))
                passed[-1].start()
        for k, (cx, cy) in enumerate(self.chips):
            for t in range(self.n):
                self._copy(t, 3 + k, 2 * cx + cy, 1 - c, (x, y, c)).wait_recv()
        for cp in self._first() + passed:
            cp.wait_send()
        for cp in self._mine():
            cp.wait()


def _gather_scratch(n):
    return [pltpu.SemaphoreType.DMA((6 * n,)), pltpu.SemaphoreType.DMA((6 * n,)), pltpu.SemaphoreType.DMA((2 * n,))]


def _gathered(shards):
    return [_sds((N_CHIPS,) + a.shape, a.dtype) for a in shards]


def gather_weights(shards, name):
    n = len(shards)

    def body(*refs):
        gather = _Gather(refs[:n], refs[n:2 * n], *refs[2 * n:])
        gather.start()
        gather.finish()

    return pl.pallas_call(
        body, name=name, in_specs=[ANY] * n, out_specs=[ANY] * n, out_shape=_gathered(shards),
        scratch_shapes=_gather_scratch(n))(*shards)


def _carrying_gather(body, n_in, n_out, shards, steps):
    n = len(shards)

    def both(*refs):
        ins, parts = refs[:n_in], refs[n_in:n_in + n]
        outs = refs[n_in + n:n_in + n + n_out]
        landed = refs[n_in + n + n_out:n_in + 2 * n + n_out]
        scratch, sems = refs[n_in + 2 * n + n_out:-3], refs[-3:]
        gather = _Gather(parts, landed, *sems)
        pl.when(pl.program_id(0) == 0)(gather.start)
        body(*ins, *outs, *scratch)
        pl.when(pl.program_id(0) == steps - 1)(gather.finish)

    return both


def pair_exchange(layer0, layer1, name):
    n = len(layer0)

    def body(*refs):
        srcs = (refs[:n], refs[n:2 * n])
        outs = refs[2 * n:3 * n]
        send_sems, recv_sems = refs[3 * n:]
        x, y, c = _place()
        for core in range(2):
            @pl.when(c == core)
            def _(core=core):
                cps = [_remote(srcs[1 - core][t], outs[t], send_sems.at[t], recv_sems.at[t], (x, y, 1 - c))
                       for t in range(n)]
                for cp in cps:
                    cp.start()
                for cp in cps:
                    cp.wait()

    return pl.pallas_call(
        body, name=name, in_specs=[ANY] * (2 * n), out_specs=[ANY] * n,
        out_shape=[_sds(a.shape) for a in layer0],
        scratch_shapes=[pltpu.SemaphoreType.DMA((n,)), pltpu.SemaphoreType.DMA((n,))],
    )(*layer0, *layer1)


def pair_add(a0, a1, theirs, c_vec, name):
    rows, cols = a0.shape
    tr = _rows_block(rows, cols)

    def body(c_ref, a0_ref, a1_ref, b_ref, o_ref):
        o_ref[...] = (jnp.where(c_ref[0] == 0, a0_ref[...], a1_ref[...]) + b_ref[...]).astype(BF16)

    def of_core(core):
        return pl.BlockSpec((tr, cols), lambda i, c: (jnp.where(c[0] == core, i, 0), 0))

    spec = pl.BlockSpec((tr, cols), lambda i, c: (i, 0))
    grid_spec = pltpu.PrefetchScalarGridSpec(
        num_scalar_prefetch=1, grid=(rows // tr,), in_specs=[of_core(0), of_core(1), spec], out_specs=spec)
    return pl.pallas_call(body, grid_spec=grid_spec, name=name, out_shape=_sds((rows, cols), BF16),
                          compiler_params=_params())(c_vec, a0, a1, theirs)


def scatter_chips(srcs, pieces, shard_shapes, name):
    n_src, n_t = len(srcs), len(pieces)

    def body(*refs):
        src_refs = refs[:n_src]
        outs = refs[n_src:n_src + n_t]
        send_sems, recv_sems, local_sems = refs[n_src + n_t:]
        x, y, c = _place()
        me = 2 * x + y

        def piece(t, k):
            idx, lead, rows, cols = pieces[t][k]
            ref = src_refs[idx]
            if lead is not None:
                ref = ref.at[lead]
            if rows is not None:
                ref = ref.at[pl.ds(rows[0], rows[1]), :]
            if cols is not None:
                ref = ref.at[:, pl.ds(cols[0], cols[1])]
            return ref

        def local(t, k):
            return pltpu.make_async_copy(piece(t, k), outs[t].at[k], local_sems.at[t])

        for k in range(N_CHIPS):
            @pl.when(me == k)
            def _(k=k):
                for t in range(n_t):
                    local(t, k).start()

            @pl.when(me != k)
            def _(k=k):
                for t in range(n_t):
                    _remote(piece(t, k), outs[t].at[me], send_sems.at[N_CHIPS * t + k],
                            recv_sems.at[N_CHIPS * t + me], (k // 2, k % 2, c)).start()

        for k in range(N_CHIPS):
            @pl.when(me != k)
            def _(k=k):
                for t in range(n_t):
                    cp = _remote(piece(t, k), outs[t].at[k], send_sems.at[N_CHIPS * t + k],
                                 recv_sems.at[N_CHIPS * t + k], (x, y, c))
                    cp.wait_recv()
                    cp.wait_send()

            @pl.when(me == k)
            def _(k=k):
                for t in range(n_t):
                    local(t, k).wait()

    return pl.pallas_call(
        body, name=name, in_specs=[ANY] * n_src, out_specs=[ANY] * n_t,
        out_shape=[_sds((N_CHIPS,) + tuple(shp), srcs[0].dtype) for shp in shard_shapes],
        scratch_shapes=[pltpu.SemaphoreType.DMA((N_CHIPS * n_t,)), pltpu.SemaphoreType.DMA((N_CHIPS * n_t,)),
                        pltpu.SemaphoreType.DMA((n_t,))],
    )(*srcs)


def chips_add(recv, c_vec, name):
    n, a, b = recv.shape
    tr = _rows_block(a, b)

    def body(c_ref, r0, r1, r2, r3, o_ref):
        o_ref[...] = ((r0[...].astype(F32) + r1[...].astype(F32)) + r2[...].astype(F32)) + r3[...].astype(F32)

    grid_spec = pltpu.PrefetchScalarGridSpec(
        num_scalar_prefetch=1, grid=(a // tr,),
        in_specs=[pl.BlockSpec((None, tr, b), lambda i, c, k=k: (k, i, 0)) for k in range(n)],
        out_specs=pl.BlockSpec((None, tr, b), lambda i, c: (c[0], i, 0)))
    return pl.pallas_call(body, grid_spec=grid_spec, name=name, out_shape=_sds((2, a, b)),
                          compiler_params=_params())(c_vec, *([recv] * n))


def pair_join(bufs, name):
    n = len(bufs)

    def body(*refs):
        ins, outs = refs[:n], refs[n:2 * n]
        send_sems, recv_sems = refs[2 * n:]
        x, y, c = _place()
        cps = [_remote(ins[t].at[c], outs[t].at[c], send_sems.at[t], recv_sems.at[t], (x, y, 1 - c)) for t in range(n)]
        for cp in cps:
            cp.start()
        for t in range(n):
            cps[t].wait_send()
            _remote(ins[t].at[c], outs[t].at[1 - c], send_sems.at[t], recv_sems.at[t], (x, y, c)).wait_recv()

    return pl.pallas_call(
        body, name=name, in_specs=[ANY] * n, out_specs=[ANY] * n, out_shape=[_sds(a.shape) for a in bufs],
        input_output_aliases={t: t for t in range(n)},
        scratch_shapes=[pltpu.SemaphoreType.DMA((n,)), pltpu.SemaphoreType.DMA((n,))],
    )(*bufs)


def allsum_small(v, name):
    rows, lanes = v.shape
    n_dev = 8

    def body(v_ref, out_ref, buf, send_sems, recv_sems):
        x, y, c = _place()
        me, sibling = (x, y, c), (x, y, 1 - c)
        chips = _other_chips(x, y)

        def slot(px, py, pc):
            return buf.at[4 * px + 2 * py + pc]

        def copy(k, block, to, src=None):
            return pltpu.make_async_remote_copy(
                src_ref=slot(*block) if src is None else src, dst_ref=slot(*block),
                send_sem=send_sems.at[k], recv_sem=recv_sems.at[k], device_id=to, device_id_type=MESH)

        slot(*me)[...] = v_ref[...]
        first = [copy(0, me, sibling, src=v_ref)]
        first += [copy(1 + k, me, (*chip, c), src=v_ref) for k, chip in enumerate(chips)]
        for cp in first:
            cp.start()
        passed = [copy(4 + k, (*chip, c), sibling) for k, chip in enumerate(chips)]
        for k, chip in enumerate(chips):
            copy(1 + k, (*chip, c), me).wait_recv()
            passed[k].start()
        copy(0, sibling, me).wait_recv()
        for k, chip in enumerate(chips):
            copy(4 + k, (*chip, 1 - c), me).wait_recv()
        for cp in first + passed:
            cp.wait_send()
        acc = buf[0]
        for d in range(1, n_dev):
            acc = acc + buf[d]
        out_ref[...] = acc

    vm = pl.BlockSpec(memory_space=pltpu.VMEM)
    return pl.pallas_call(
        body, name=name, in_specs=[vm], out_specs=vm, out_shape=_sds((rows, lanes)),
        scratch_shapes=[pltpu.VMEM((n_dev, rows, lanes), F32), pltpu.SemaphoreType.DMA((7,)), pltpu.SemaphoreType.DMA((7,))],
        compiler_params=pltpu.CompilerParams(vmem_limit_bytes=VMEM_LIMIT),
    )(v)


BIG = ("w_in", "w_branch_a", "w_branch_b", "w_out", "w_up", "w_down")
CONV = ("conv_qkv", "conv_ffn")
REPL =("a_log", "dt_bias", "dn_norm_w", "sg_ln_g", "sg_ln_b", "w_spatial", "b_spatial", "ln1_g", "ln1_b", "ln2_g", "ln2_b")


def _pad_rows(flat, mult):
    n = flat.shape[0]
    unit = mult * LANES
    total = -(-n // unit) * unit
    return jnp.pad(flat, (0, total - n)).reshape(total // LANES, LANES)


def _pack(arrs, mult):
    return _pad_rows(jnp.concatenate([a.reshape(-1) for a in arrs]), mult)


def _unpack(flat, shapes):
    out, off = [], 0
    for shp in shapes:
        n = math.prod(shp)
        out.append(flat[off:off + n].reshape(shp))
        off += n
    return out


def kernel(x, w_in, conv_qkv, a_log, dt_bias, dn_norm_w, w_branch_a, sg_ln_g, sg_ln_b, w_spatial, b_spatial, w_branch_b, w_out, ln1_g, ln1_b, w_up, conv_ffn, w_down, ln2_g, ln2_b, loss_target, m_w_in, m_conv_qkv, m_a_log, m_dt_bias, m_dn_norm_w, m_w_branch_a, m_sg_ln_g, m_sg_ln_b, m_w_spatial, m_b_spatial, m_w_branch_b, m_w_out, m_ln1_g, m_ln1_b, m_w_up, m_conv_ffn, m_w_down, m_ln2_g, m_ln2_b, v_w_in, v_conv_qkv, v_a_log, v_dt_bias, v_dn_norm_w, v_w_branch_a, v_sg_ln_g, v_sg_ln_b, v_w_spatial, v_b_spatial, v_w_branch_b, v_w_out, v_ln1_g, v_ln1_b, v_w_up, v_conv_ffn, v_w_down, v_ln2_g, v_ln2_b):
    names = ("w_in", "conv_qkv", "a_log", "dt_bias", "dn_norm_w", "w_branch_a", "sg_ln_g", "sg_ln_b", "w_spatial",
             "b_spatial", "w_branch_b", "w_out", "ln1_g", "ln1_b", "w_up", "conv_ffn", "w_down", "ln2_g", "ln2_b")
    w = dict(zip(names, (w_in, conv_qkv, a_log, dt_bias, dn_norm_w, w_branch_a, sg_ln_g, sg_ln_b, w_spatial,
                         b_spatial, w_branch_b, w_out, ln1_g, ln1_b, w_up, conv_ffn, w_down, ln2_g, ln2_b)))
    m = dict(zip(names, (m_w_in, m_conv_qkv, m_a_log, m_dt_bias, m_dn_norm_w, m_w_branch_a, m_sg_ln_g, m_sg_ln_b,
                         m_w_spatial, m_b_spatial, m_w_branch_b, m_w_out, m_ln1_g, m_ln1_b, m_w_up, m_conv_ffn,
                         m_w_down, m_ln2_g, m_ln2_b)))
    v = dict(zip(names, (v_w_in, v_conv_qkv, v_a_log, v_dt_bias, v_dn_norm_w, v_w_branch_a, v_sg_ln_g, v_sg_ln_b,
                         v_w_spatial, v_b_spatial, v_w_branch_b, v_w_out, v_ln1_g, v_ln1_b, v_w_up, v_conv_ffn,
                         v_w_down, v_ln2_g, v_ln2_b)))
    chip = 2 * lax.axis_index("x") + lax.axis_index("y")
    s = x.shape[1]
    xs = x.reshape(s, D)
    tgt = loss_target.reshape(s, D)

    big_names, conv_names = list(BIG), list(CONV)

    def in_two(name, l):
        rows, cols = w[name].shape[1:]
        return w[name][l].astype(BF16).reshape(2, rows // 2, cols)

    def whole(name, landed):
        rows, cols = w[name].shape[1:]
        return landed.reshape(N_CHIPS, rows, cols)

    first = gather_weights([in_two(n, 0) for n in big_names] + [w[n] for n in conv_names], "gather_layer0")
    got = [dict(zip(big_names, [whole(n, a) for n, a in zip(big_names, first)])), {}]
    conv_taps = dict(zip(conv_names, first[len(big_names):]))
    early, late = big_names[:4], big_names[4:]

    def lane_row(vec, off):
        return jnp.zeros((1, LANES), F32).at[0, off:off + vec.shape[0]].set(vec)

    def layer_params(l):
        side_by_side = lambda blocks: jnp.concatenate([blocks[k] for k in range(N_CHIPS)], axis=1)
        wi = side_by_side(got[l]["w_in"])
        return dict(
            w_in=jnp.concatenate([wi[:, :2048], wi[:, 2056:3080], wi[:, 3080:5128], wi[:, 2048:2056],
                                  jnp.zeros((D, IN_COLS_PAD - 5128), BF16)], axis=1),
            cq=side_by_side(conv_taps["conv_qkv"][:, l]),
            a_row=lane_row(w["a_log"][l], HEADS), dtb_row=lane_row(w["dt_bias"][l], HEADS),
            nw_row=w["dn_norm_w"][l].reshape(1, DK),
            lng=w["sg_ln_g"][l].reshape(1, SG_W), lnb=w["sg_ln_b"][l].reshape(1, SG_W),
            w_s=w["w_spatial"][l], bs_t=jnp.zeros((LANES, LANES), F32).at[:, :4].set(w["b_spatial"][l].T),
            wa=side_by_side(got[l]["w_branch_a"]), wb=side_by_side(got[l]["w_branch_b"]),
            wo=got[l]["w_out"].reshape(D, D),
            g1=w["ln1_g"][l].reshape(1, D), b1=w["ln1_b"][l].reshape(1, D),
            cf=conv_taps["conv_ffn"][:, l],
            g2=w["ln2_g"][l].reshape(1, D), b2=w["ln2_b"][l].reshape(1, D))

    layers, saved = [], []
    h_in = xs
    for l in range(DEPTH):
        p = layer_params(l)
        layers.append(p)
        carry = l == 0
        proj = proj_fwd(h_in, p["w_in"], f"proj_fwd{l}")
        oa, sst, *landed = dn_fwd(proj, p["cq"], p["a_row"], p["dtb_row"], p["nw_row"], f"dn_fwd{l}",
                                  gather=[in_two(n, 1) for n in early] if carry else ())
        got[1].update({n: whole(n, a) for n, a in zip(early, landed)})
        ob = sg_fwd(proj, p["lng"], p["lnb"], p["w_s"], p["bs_t"], f"sg_fwd{l}")
        x1 = merge_fwd(oa, ob, proj, h_in, p["wa"], p["wb"], p["wo"], p["g1"], p["b1"], f"merge_fwd{l}")
        pre2, x2, *landed = ffn_fwd(x1, got[l]["w_up"], p["cf"], got[l]["w_down"], p["g2"], p["b2"], f"ffn_fwd{l}",
                                    gather=[in_two(n, 1) for n in late] if carry else ())
        got[1].update({n: whole(n, a) for n, a in zip(late, landed)})
        saved.append(dict(x=h_in, proj=proj, oa=oa, ob=ob, sst=sst, x1=x1, pre2=pre2))
        h_in = x2

    dy, loss_part = loss_call(h_in, tgt, "loss")
    loss = lax.psum(loss_part[0, 0], ("x", "y", "c"))

    small_names = conv_names + list(REPL)
    grads = {n: [None] * DEPTH for n in small_names}
    big_grads = [None] * DEPTH
    for l in reversed(range(DEPTH)):
        p, a = layers[l], saved[l]
        dpre2, dg2, db2 = ln_bwd(a["pre2"], dy, p["g2"], p["b2"], f"ln2_bwd{l}")
        dx1, dwup0, dcf0, dwdn0 = ffn_bwd(a["x1"], dpre2, dpre2, ALPHA, got[l]["w_up"], p["cf"], got[l]["w_down"], 0, f"ffn_bwd{l}a")
        dx1, dwup1, dcf1, dwdn1 = ffn_bwd(a["x1"], dpre2, dx1, 1.0, got[l]["w_up"], p["cf"], got[l]["w_down"], 1, f"ffn_bwd{l}b")
        doa, dob, dga, dgb, dxd, dwa, dwb, dwo, dg1, db1 = merge_bwd(
            a["oa"], a["ob"], a["proj"], a["x"], dx1, p["wa"], p["wb"], p["wo"], p["g1"], p["b1"], f"merge_bwd{l}")
        duv, dlng, dlnb, dws, dbs = sg_bwd(a["proj"], dob, p["lng"], p["lnb"], p["w_s"], p["bs_t"], f"sg_bwd{l}")
        dqkv, dz, dba, dcq, da, ddtb, dnw = dn_bwd(a["proj"], a["sst"], doa, p["cq"], p["a_row"], p["dtb_row"],
                                                   p["nw_row"], f"dn_bwd{l}")
        dy, dwi = proj_bwd(a["x"], dqkv, dz, duv, dga, dgb, dba, dxd, p["w_in"], f"proj_bwd{l}")

        big_grads[l] = [dwi, dwa, dwb, dwo, dwup0.reshape(2 * D, FFN_HALF), dwup1.reshape(2 * D, FFN_HALF), dwdn0, dwdn1]
        grads["conv_qkv"][l] = dcq
        grads["conv_ffn"][l] = jnp.concatenate([dcf0[0], dcf1[0], dcf0[1], dcf1[1]], axis=1)
        grads["a_log"][l] = da[0, HEADS:2 * HEADS]
        grads["dt_bias"][l] = ddtb[0, HEADS:2 * HEADS]
        grads["dn_norm_w"][l] = dnw[0]
        grads["sg_ln_g"][l] = dlng[0]
        grads["sg_ln_b"][l] = dlnb[0]
        grads["w_spatial"][l] = dws
        grads["b_spatial"][l] = dbs[:, :4].T
        grads["ln1_g"][l] = dg1[0]
        grads["ln1_b"][l] = db1[0]
        grads["ln2_g"][l] = dg2[0]
        grads["ln2_b"][l] = db2[0]
    grad_x = dy.reshape(x.shape)
    g_full = {n: jnp.stack(grads[n]) for n in small_names}

    c_vec = jnp.stack([lax.axis_index("c")]).astype(jnp.int32)
    theirs = pair_exchange(big_grads[0], big_grads[1], "reduce_pair")
    tags = ("w_in", "w_a", "w_b", "w_out", "w_up0", "w_up1", "w_dn0", "w_dn1")
    pin, pa, pb, po, pup0, pup1, pdn0, pdn1 = [
        pair_add(a0, a1, th, c_vec, f"reduce_pair_add_{tag}")
        for tag, a0, a1, th in zip(tags, big_grads[0], big_grads[1], theirs)]
    natural = jnp.concatenate([pin[:, :2048], pin[:, C_BA:C_BA + 8], pin[:, 2048:C_BA]], axis=1)
    srcs = [jnp.stack(jnp.split(natural, N_CHIPS, axis=1)), pa, pb, po,
            pup0.reshape(2, D, FFN_HALF), pup1.reshape(2, D, FFN_HALF), pdn0, pdn1]
    ab_cols, out_rows = w["w_branch_a"].shape[2], w["w_out"].shape[1]
    pieces = [
        [(0, k, None, None) for k in range(N_CHIPS)],
        [(1, None, None, (k * ab_cols, ab_cols)) for k in range(N_CHIPS)],
        [(2, None, None, (k * ab_cols, ab_cols)) for k in range(N_CHIPS)],
        [(3, None, (k * out_rows, out_rows), None) for k in range(N_CHIPS)],
        [(4 + k % 2, k // 2, None, None) for k in range(N_CHIPS)],
        [(6 + k // 2, None, ((k % 2) * DN_SHARD, DN_SHARD), None) for k in range(N_CHIPS)],
    ]
    recv = scatter_chips(srcs, pieces, [w[n].shape[1:] for n in big_names], "reduce_chips")
    sums = [chips_add(r, c_vec, f"reduce_chips_add_{n}") for n, r in zip(big_names, recv)]
    g_shard = dict(zip(big_names, pair_join(sums, "reduce_join")))

    small = allsum_small(_pack([g_full[n] for n in small_names], 8), "reduce_small").reshape(-1)
    small_full = dict(zip(small_names, _unpack(small, [g_full[n].shape for n in small_names])))
    for n in conv_names:
        width = w[n].shape[2]
        g_shard[n] = lax.dynamic_slice_in_dim(small_full[n], chip * width, width, axis=2)
    for n in REPL:
        g_shard[n] = small_full[n]

    delta, new_m, new_v = {}, {}, {}
    for n in big_names:
        shp = w[n].shape
        two_d = (shp[0] * shp[1], shp[2])
        d_, m_, v_ = adam_call(w[n].reshape(two_d), g_shard[n].reshape(two_d), m[n].reshape(two_d), v[n].reshape(two_d), f"adam_{n}")
        delta[n], new_m[n], new_v[n] = d_.reshape(shp), m_.reshape(shp), v_.reshape(shp)
    shapes = [w[n].shape for n in small_names]
    packs = [_pack([src[n] for n in small_names], 8) for src in (w, g_shard, m, v)]
    outs = adam_call(*packs, "adam_small")
    for dst, o in zip((delta, new_m, new_v), outs):
        dst.update(zip(small_names, _unpack(o.reshape(-1), shapes)))

    return (loss, grad_x, *[g_shard[n] for n in names], *[delta[n] for n in names],
            *[new_m[n] for n in names], *[new_v[n] for n in names])
```

```python
import functools
import math

import jax
import jax.numpy as jnp
from jax import lax
from jax.experimental import pallas as pl
from jax.experimental.pallas import tpu as pltpu

F32 = jnp.float32
BF16 = jnp.bfloat16
HI = lax.Precision.HIGHEST
MID = lax.Precision.HIGH
MESH = pl.DeviceIdType.MESH

D = 1024
DEPTH = 2
HEADS = 4
DK = 128
CHUNK = 64
QKV_W = 1536
Z_W = 512
SG_W = 512
FFN = 2816
FFN_HALF = FFN // 2
N_CHIPS = 4
DN_SHARD = FFN // N_CHIPS
LN_EPS = 1e-5
RMS_EPS = 1e-6
L2_EPS = 1e-6
ALPHA = (2 * DEPTH) ** 0.25
ADAM_LR, ADAM_B1, ADAM_B2, ADAM_EPS, ADAM_WD, ADAM_STEP = 0.001, 0.9, 0.999, 1e-08, 0.01, 10

HALO = 16
LANES = 128
IN_COLS_PAD = 5248
C_Z, C_UV, C_GA, C_GB, C_BA = 1536, 2048, 3072, 4096, 5120
VMEM_LIMIT = 56 * 1024 * 1024


def _params(n_grid=1):
    return pltpu.CompilerParams(dimension_semantics=("arbitrary",) * n_grid, vmem_limit_bytes=VMEM_LIMIT)


def _mm(a, b):
    return jnp.dot(a.astype(BF16), b.astype(BF16), preferred_element_type=F32)


def _mm_nt(a, b):
    return lax.dot_general(a.astype(BF16), b.astype(BF16), (((1,), (1,)), ((), ())), preferred_element_type=F32)


def _mm_tn(a, b):
    return lax.dot_general(a.astype(BF16), b.astype(BF16), (((0,), (0,)), ((), ())), preferred_element_type=F32)


def _bdot(a, b, prec=MID):
    return lax.dot_general(a, b, (((2,), (1,)), ((0,), (0,))), precision=prec, preferred_element_type=F32)


def _bdot_nt(a, b, prec=MID):
    return lax.dot_general(a, b, (((2,), (2,)), ((0,), (0,))), precision=prec, preferred_element_type=F32)


def _bf16_dot(a, b, contract):
    return lax.dot_general(a.astype(BF16), b.astype(BF16), (contract, ((0,), (0,))), preferred_element_type=F32)


@jax.custom_vjp
def _fdot(a, b):
    return _bf16_dot(a, b, ((2,), (1,)))


def _fdot_fwd(a, b):
    return _fdot(a, b), (a, b)


def _fdot_bwd(res, ct):
    a, b = res
    return _bf16_dot(ct, b, ((2,), (2,))), _bf16_dot(a, ct, ((1,), (1,)))


_fdot.defvjp(_fdot_fwd, _fdot_bwd)


@jax.custom_vjp
def _fdot_nt(a, b):
    return _bf16_dot(a, b, ((2,), (2,)))


def _fdot_nt_fwd(a, b):
    return _fdot_nt(a, b), (a, b)


def _fdot_nt_bwd(res, ct):
    a, b = res
    return _bf16_dot(ct, b, ((2,), (1,))), _bf16_dot(ct, a, ((1,), (1,)))


_fdot_nt.defvjp(_fdot_nt_fwd, _fdot_nt_bwd)


@jax.custom_vjp
def _fdot_tn(a, b):
    return _bf16_dot(a, b, ((1,), (1,)))


def _fdot_tn_fwd(a, b):
    return _fdot_tn(a, b), (a, b)


def _fdot_tn_bwd(res, ct):
    a, b = res
    return _bf16_dot(b, ct, ((2,), (2,))), _bf16_dot(a, ct, ((2,), (1,)))


_fdot_tn.defvjp(_fdot_tn_fwd, _fdot_tn_bwd)


def _ln(x, g, b):
    mu = jnp.mean(x, axis=-1, keepdims=True)
    xc = x - mu
    var = jnp.mean(xc * xc, axis=-1, keepdims=True)
    return xc * lax.rsqrt(var + LN_EPS) * g + b


def _shift_rows(x, s):
    s = s % x.shape[0]
    return x if s == 0 else pltpu.roll(x, s, 0)


@jax.custom_vjp
def _conv(xcat, w):
    k_taps = len(w)
    y = None
    for k in range(k_taps):
        t = _shift_rows(xcat, k_taps - 1 - k)[HALO:] * w[k]
        y = t if y is None else y + t
    return y


def _conv_fwd(xcat, w):
    return _conv(xcat, w), (xcat, w)


def _conv_bwd(res, dy):
    xcat, w = res
    k_taps = len(w)
    dyp = jnp.concatenate([jnp.zeros((HALO, dy.shape[1]), dy.dtype), dy], axis=0)
    dx = None
    dws = []
    for k in range(k_taps):
        s = k_taps - 1 - k
        t = _shift_rows(dyp, -s) * w[k]
        dx = t if dx is None else dx + t
        dws.append(jnp.sum(_shift_rows(xcat, s)[HALO:] * dy, axis=0, keepdims=True))
    return dx, tuple(dws)


_conv.defvjp(_conv_fwd, _conv_bwd)


@jax.custom_vjp
def _tri_inv(l):
    n = l.shape[-1]
    r = lax.broadcasted_iota(jnp.int32, (n, n), 0)
    c = lax.broadcasted_iota(jnp.int32, (n, n), 1)
    eye = (r == c).astype(F32)
    p = eye - l
    lp = l
    steps = int(math.log2(n)) - 1
    for i in range(steps):
        dot = _bdot if i < 2 else functools.partial(_bf16_dot, contract=((2,), (1,)))
        lp = dot(lp, lp)
        p = p + dot(p, lp)
    return p


def _tri_inv_fwd(l):
    t = _tri_inv(l)
    return t, t


def _tri_inv_bwd(t, dt):
    tt = jnp.swapaxes(t, 1, 2)
    return (-_bdot(tt, _bdot(dt, tt)),)


_tri_inv.defvjp(_tri_inv_fwd, _tri_inv_bwd)


def _dn_glue(qkvcat, z, ba, s_in, cw, a_row, dtb_row, nw_row):
    t_rows = z.shape[0]
    nc = t_rows // CHUNK
    nb = nc * HEADS

    qkv = jax.nn.silu(_conv(qkvcat, cw))

    def chunks(t, off):
        return jnp.stack([t[n * CHUNK:(n + 1) * CHUNK, off + h * DK: off + (h + 1) * DK]
                          for n in range(nc) for h in range(HEADS)])

    q = chunks(qkv, 0)
    k = chunks(qkv, 512)
    v = chunks(qkv, 1024)
    q = q * lax.rsqrt(jnp.sum(q * q, axis=-1, keepdims=True) + L2_EPS) * (DK ** -0.5)
    k = k * lax.rsqrt(jnp.sum(k * k, axis=-1, keepdims=True) + L2_EPS)

    lane = lax.broadcasted_iota(jnp.int32, (LANES, HEADS * DK), 0)
    head_of_col = lax.broadcasted_iota(jnp.int32, (LANES, HEADS * DK), 1) // DK
    e_beta = (head_of_col == lane).astype(F32)
    e_g = (head_of_col + HEADS == lane).astype(F32)
    beta_l = jax.nn.sigmoid(ba)
    g_l = -jnp.exp(a_row) * jax.nn.softplus(ba + dtb_row)
    beta = chunks(jnp.dot(beta_l, e_beta, precision=MID, preferred_element_type=F32), 0)
    g = chunks(jnp.dot(g_l, e_g, precision=MID, preferred_element_type=F32), 0)

    r = lax.broadcasted_iota(jnp.int32, (CHUNK, CHUNK), 0)
    c = lax.broadcasted_iota(jnp.int32, (CHUNK, CHUNK), 1)
    causal = r >= c
    strict = r > c
    tril_b = jnp.broadcast_to(causal.astype(F32), (nb, CHUNK, CHUNK))
    gi_b = _bdot(tril_b, g, HI)
    gi = gi_b[:, :, :CHUNK]
    gj = jnp.swapaxes(gi, 1, 2)
    decay = jnp.where(causal, jnp.exp(jnp.where(causal, gi - gj, 0.0)), 0.0)
    kb = k * beta
    l_mat = jnp.where(strict, _fdot_nt(kb, k) * decay, 0.0)
    t_mat = _tri_inv(l_mat)
    e_gi = jnp.exp(gi_b)
    w_mat = _fdot(t_mat, kb * e_gi)
    u_mat = _fdot(t_mat, v * beta)
    a_qk = _fdot_nt(q, k) * decay
    q_g = q * e_gi
    gl_b = jnp.broadcast_to(jnp.sum(g, axis=1, keepdims=True), g.shape)
    k_d = k * jnp.exp(gl_b - gi_b)
    e_gl = jnp.exp(gl_b)
    g_last = jnp.concatenate([e_gl, e_gl], axis=1)

    state = s_in
    rows = []
    for n in range(nc):
        sl = slice(n * HEADS, (n + 1) * HEADS)
        u_new = u_mat[sl] - _fdot(w_mat[sl], state)
        o_n = _fdot(q_g[sl], state) + _fdot(a_qk[sl], u_new)
        state = state * g_last[sl] + _fdot_tn(k_d[sl], u_new)
        o_n = o_n * lax.rsqrt(jnp.mean(o_n * o_n, axis=-1, keepdims=True) + RMS_EPS) * nw_row
        z_n = jnp.stack([z[n * CHUNK:(n + 1) * CHUNK, h * DK:(h + 1) * DK] for h in range(HEADS)])
        o_n = o_n * jax.nn.silu(z_n)
        rows.append(jnp.concatenate([o_n[h] for h in range(HEADS)], axis=-1))
    return jnp.concatenate(rows, axis=0), state


def _sg_glue(uv, lng, lnb, w_s, bs_t):
    t_rows = uv.shape[0]
    y = jax.nn.gelu(uv)
    u = y[:, :SG_W]
    v = _ln(y[:, SG_W:], lng, lnb)
    r = lax.broadcasted_iota(jnp.int32, (LANES, LANES), 0)
    c = lax.broadcasted_iota(jnp.int32, (LANES, LANES), 1)
    wm = jnp.where(r >= c, w_s, 0.0)
    lane = lax.broadcasted_iota(jnp.int32, (LANES, SG_W), 0)
    group_of_col = lax.broadcasted_iota(jnp.int32, (LANES, SG_W), 1) // LANES
    e_grp = (group_of_col == lane).astype(F32)
    bias = jnp.dot(bs_t, e_grp, precision=HI, preferred_element_type=F32)
    outs = []
    for n in range(t_rows // LANES):
        vb = v[n * LANES:(n + 1) * LANES]
        vg = jnp.stack([vb[:, g * LANES:(g + 1) * LANES] for g in range(4)])
        mg = _fdot(wm, vg)
        mixed = jnp.concatenate([mg[g] for g in range(4)], axis=-1) + bias
        outs.append(u[n * LANES:(n + 1) * LANES] * mixed)
    return jnp.concatenate(outs, axis=0)


def _merge_glue(ga, gb, ya, yb):
    return jax.nn.sigmoid(ga) * ya + jax.nn.sigmoid(gb) * yb


def _res_ln_glue(x, r, g, b):
    return _ln(ALPHA * x + r, g, b)


def _ffn_glue(ua, ub, cwa, cwb):
    return jax.nn.silu(_conv(ua, cwa)) * _conv(ub, cwb)


def _row(t, c, col=0):
    return pl.BlockSpec((t, c), lambda i: (i, col))


def _row_rev(t, c, nt, col=0):
    return pl.BlockSpec((t, c), lambda i: (nt - 1 - i, col))


def _halo(t, c, nt=None):
    per = t // HALO
    if nt is None:
        return pl.BlockSpec((HALO, c), lambda i: (jnp.maximum(i * per - 1, 0), 0))
    return pl.BlockSpec((HALO, c), lambda i: (jnp.maximum((nt - 1 - i) * per - 1, 0), 0))


def _full(shape):
    nd = len(shape)
    return pl.BlockSpec(shape, lambda i: (0,) * nd)


ANY = pl.BlockSpec(memory_space=pl.ANY)


def _sds(shape, dtype=F32):
    return jax.ShapeDtypeStruct(shape, dtype)


def _tile(s, want=256):
    for t in (want, 256, 128):
        if s % t == 0:
            return t
    raise ValueError(f"sequence length {s} is not a multiple of 128")


def proj_fwd(x, w, name, gather=()):
    s = x.shape[0]
    t = _tile(s)
    nt = s // t
    segs = [(0, 2048), (2048, 3072), (3072, 4096), (4096, 5120), (5120, IN_COLS_PAD)]

    def body(x_ref, w_ref, p_ref):
        xb = x_ref[...].astype(BF16)
        for lo, hi in segs:
            p_ref[:, lo:hi] = jnp.dot(xb, w_ref[:, lo:hi], preferred_element_type=F32)

    n = len(gather)
    return pl.pallas_call(
        _carrying_gather(body, 2, 1, gather, nt) if gather else body, grid=(nt,), name=name,
        in_specs=[_row(t, D), _full((D, IN_COLS_PAD))] + [ANY] * n,
        out_specs=[_row(t, IN_COLS_PAD)] + [ANY] * n,
        out_shape=[_sds((s, IN_COLS_PAD))] + _gathered(gather),
        scratch_shapes=_gather_scratch(n) if gather else [], compiler_params=_params())(x, w, *gather)


def dn_fwd(p, cq, a_row, dtb_row, nw_row, name, gather=()):
    s = p.shape[0]
    t = _tile(s)
    nt = s // t

    def body(qkv_ref, halo_ref, z_ref, ba_ref, cq_ref, a_ref, dtb_ref, nw_ref, o_ref, sst_ref, s_scr):
        i = pl.program_id(0)

        @pl.when(i == 0)
        def _():
            s_scr[...] = jnp.zeros_like(s_scr)

        halo = jnp.where(i == 0, 0.0, halo_ref[...])
        qkvcat = jnp.concatenate([halo, qkv_ref[...]], axis=0)
        cw = tuple(cq_ref[k:k + 1, :] for k in range(4))
        s_in = s_scr[...]
        sst_ref[0] = s_in
        o, s_out = _dn_glue(qkvcat, z_ref[...], ba_ref[...], s_in, cw, a_ref[...], dtb_ref[...], nw_ref[...])
        o_ref[...] = o.astype(BF16)
        s_scr[...] = s_out

    n = len(gather)
    return pl.pallas_call(
        _carrying_gather(body, 8, 2, gather, nt) if gather else body, grid=(nt,), name=name,
        in_specs=[_row(t, QKV_W), _halo(t, QKV_W), _row(t, Z_W, C_Z // Z_W), _row(t, LANES, C_BA // LANES),
                  _full((4, QKV_W)), _full((1, LANES)), _full((1, LANES)), _full((1, LANES))] + [ANY] * n,
        out_specs=[_row(t, Z_W), pl.BlockSpec((1, HEADS, DK, DK), lambda i: (i, 0, 0, 0))] + [ANY] * n,
        out_shape=[_sds((s, Z_W), BF16), _sds((nt, HEADS, DK, DK))] + _gathered(gather),
        scratch_shapes=[pltpu.VMEM((HEADS, DK, DK), F32)] + (_gather_scratch(n) if gather else []),
        compiler_params=_params())(p, p, p, p, cq, a_row, dtb_row, nw_row, *gather)


def sg_fwd(p, lng, lnb, w_s, bs_t, name):
    s = p.shape[0]
    t = _tile(s)

    def body(uv_ref, lng_ref, lnb_ref, ws_ref, bs_ref, o_ref):
        o_ref[...] = _sg_glue(uv_ref[...], lng_ref[...], lnb_ref[...], ws_ref[...], bs_ref[...]).astype(BF16)

    return pl.pallas_call(
        body, grid=(s // t,), name=name,
        in_specs=[_row(t, 2 * SG_W, C_UV // (2 * SG_W)), _full((1, SG_W)), _full((1, SG_W)),
                  _full((4, LANES, LANES)), _full((LANES, LANES))],
        out_specs=_row(t, SG_W), out_shape=_sds((s, SG_W), BF16), compiler_params=_params())(p, lng, lnb, w_s, bs_t)


def merge_fwd(oa, ob, p, x, wa, wb, wo, g1, b1, name):
    s = x.shape[0]
    t = _tile(s)

    def body(oa_ref, ob_ref, ga_ref, gb_ref, x_ref, wa_ref, wb_ref, wo_ref, g_ref, b_ref, x1_ref):
        ya = _mm(oa_ref[...], wa_ref[...])
        yb = _mm(ob_ref[...], wb_ref[...])
        h = _merge_glue(ga_ref[...], gb_ref[...], ya, yb)
        x1_ref[...] = _res_ln_glue(x_ref[...], _mm(h, wo_ref[...]), g_ref[...], b_ref[...])

    return pl.pallas_call(
        body, grid=(s // t,), name=name,
        in_specs=[_row(t, Z_W), _row(t, SG_W), _row(t, D, C_GA // D), _row(t, D, C_GB // D), _row(t, D),
                  _full((Z_W, D)), _full((SG_W, D)), _full((D, D)), _full((1, D)), _full((1, D))],
        out_specs=_row(t, D), out_shape=_sds((s, D)), compiler_params=_params())(oa, ob, p, p, x, wa, wb, wo, g1, b1)


def _load_ffn_weights(wup_hbm, wdn_hbm, wup_v, wdn_v, up_slots, dn_slots):
    for n, k in enumerate(up_slots):
        pltpu.sync_copy(wup_hbm.at[k], wup_v.at[n])
    for n, k in enumerate(dn_slots):
        pltpu.sync_copy(wdn_hbm.at[k], wdn_v.at[pl.ds(n * DN_SHARD, DN_SHARD)])


def ffn_fwd(x1, wup4, cf4, wdn4, g2, b2, name, gather=()):
    s = x1.shape[0]
    t = _tile(s)
    nt = s // t

    def body(x1_ref, halo_ref, wup_hbm, cf_ref, wdn_hbm, g_ref, b_ref, pre_ref, x2_ref, wup_v, wdn_v):
        i = pl.program_id(0)

        @pl.when(i == 0)
        def _():
            _load_ffn_weights(wup_hbm, wdn_hbm, wup_v, wdn_v, range(4), range(4))

        x1v = x1_ref[...]
        halo = jnp.where(i == 0, 0.0, halo_ref[...])
        x1cat = jnp.concatenate([halo, x1v], axis=0).astype(BF16)
        f = None
        for h in range(2):
            ua = jnp.dot(x1cat, wup_v[h], preferred_element_type=F32)
            ub = jnp.dot(x1cat, wup_v[2 + h], preferred_element_type=F32)
            cwa = tuple(cf_ref[h, k:k + 1, :] for k in range(3))
            cwb = tuple(cf_ref[2 + h, k:k + 1, :] for k in range(3))
            act = _ffn_glue(ua, ub, cwa, cwb)
            fh = _mm(act, wdn_v[h * FFN_HALF:(h + 1) * FFN_HALF, :])
            f = fh if f is None else f + fh
        pre = ALPHA * x1v + f
        pre_ref[...] = pre
        x2_ref[...] = _ln(pre, g_ref[...], b_ref[...])

    n = len(gather)
    return pl.pallas_call(
        _carrying_gather(body, 7, 2, gather, nt) if gather else body, grid=(nt,), name=name,
        in_specs=[_row(t, D), _halo(t, D), ANY, _full((4, 3, FFN_HALF)), ANY, _full((1, D)), _full((1, D))] + [ANY] * n,
        out_specs=[_row(t, D), _row(t, D)] + [ANY] * n, out_shape=[_sds((s, D)), _sds((s, D))] + _gathered(gather),
        scratch_shapes=[pltpu.VMEM((4, D, FFN_HALF), BF16), pltpu.VMEM((FFN, D), BF16)]
        + (_gather_scratch(n) if gather else []),
        compiler_params=_params())(x1, x1, wup4, cf4, wdn4, g2, b2, *gather)


def loss_call(y, tgt, name):
    s = y.shape[0]
    t = _tile(s)

    def body(y_ref, t_ref, dy_ref, loss_ref):
        @pl.when(pl.program_id(0) == 0)
        def _():
            loss_ref[...] = jnp.zeros_like(loss_ref)

        e = y_ref[...] - t_ref[...]
        dy_ref[...] = e * (1.0 / D)
        part = jnp.sum(jnp.sum(e * e, axis=1, keepdims=True), axis=0, keepdims=True) * (0.5 / D)
        loss_ref[...] += jnp.broadcast_to(part, loss_ref.shape)

    return pl.pallas_call(
        body, grid=(s // t,), name=name, in_specs=[_row(t, D), _row(t, D)],
        out_specs=[_row(t, D), _full((8, LANES))], out_shape=[_sds((s, D)), _sds((8, LANES))],
        compiler_params=_params())(y, tgt)


def _acc(ref, val, first):
    @pl.when(first)
    def _():
        ref[...] = val

    @pl.when(jnp.logical_not(first))
    def _():
        ref[...] += val


def _acc_tn(acc_ref, a, b, first, seg):
    n = b.shape[1]
    for lo in range(0, n, seg):
        hi = min(lo + seg, n)
        _acc(acc_ref.at[:, lo:hi], _mm_tn(a, b[:, lo:hi]), first)


def ln_bwd(pre, dy, g, b, name):
    s = pre.shape[0]
    t = _tile(s)

    def body(pre_ref, dy_ref, g_ref, b_ref, dpre_ref, dg_ref, db_ref):
        _, vjp = jax.vjp(_ln, pre_ref[...], g_ref[...], b_ref[...])
        dpre, dg, db = vjp(dy_ref[...])
        dpre_ref[...] = dpre
        first = pl.program_id(0) == 0
        _acc(dg_ref, dg, first)
        _acc(db_ref, db, first)

    return pl.pallas_call(
        body, grid=(s // t,), name=name, in_specs=[_row(t, D), _row(t, D), _full((1, D)), _full((1, D))],
        out_specs=[_row(t, D), _full((1, D)), _full((1, D))],
        out_shape=[_sds((s, D)), _sds((1, D)), _sds((1, D))], compiler_params=_params())(pre, dy, g, b)


def ffn_bwd(x1, df, acc_in, acc_scale, wup4, cf4, wdn4, h, name):
    s = x1.shape[0]
    t = _tile(s)
    nt = s // t

    def body(x1_ref, halo_ref, df_ref, acc_ref, wup_hbm, cf_ref, wdn_hbm,
             dx1_ref, dwup_hbm, dcf_ref, dwdn_hbm, wup_v, wdn_v, dwup_v, dwdn_v, carry):
        i = pl.program_id(0)
        j = nt - 1 - i
        first = i == 0

        @pl.when(first)
        def _():
            _load_ffn_weights(wup_hbm, wdn_hbm, wup_v, wdn_v, (h, 2 + h), (2 * h, 2 * h + 1))
            carry[...] = jnp.zeros_like(carry)

        halo = jnp.where(j == 0, 0.0, halo_ref[...])
        x1cat = jnp.concatenate([halo, x1_ref[...]], axis=0).astype(BF16)
        ua = jnp.dot(x1cat, wup_v[0], preferred_element_type=F32)
        ub = jnp.dot(x1cat, wup_v[1], preferred_element_type=F32)
        cwa = tuple(cf_ref[h, k:k + 1, :] for k in range(3))
        cwb = tuple(cf_ref[2 + h, k:k + 1, :] for k in range(3))
        act, vjp = jax.vjp(_ffn_glue, ua, ub, cwa, cwb)
        dfb = df_ref[...].astype(BF16)
        dact = _mm_nt(dfb, wdn_v[...])
        _acc_tn(dwdn_v, act.astype(BF16), dfb, first, 512)
        dua, dub, dcwa, dcwb = vjp(dact)
        x1b = x1cat[HALO:]
        dups = []
        for n, du in enumerate((dua, dub)):
            dups.append(jnp.concatenate([du[HALO:t], du[t:] + carry[n]], axis=0).astype(BF16))
            carry[n] = du[:HALO]
            _acc(dwup_v.at[n], _mm_tn(x1b, dups[n]), first)
        for k in range(3):
            _acc(dcf_ref.at[0, k:k + 1, :], dcwa[k], first)
            _acc(dcf_ref.at[1, k:k + 1, :], dcwb[k], first)
        dx1_ref[...] = acc_scale * acc_ref[...] + _mm_nt(dups[0], wup_v[0]) + _mm_nt(dups[1], wup_v[1])

        @pl.when(i == nt - 1)
        def _():
            pltpu.sync_copy(dwup_v, dwup_hbm)
            pltpu.sync_copy(dwdn_v, dwdn_hbm)

    return pl.pallas_call(
        body, grid=(nt,), name=name,
        in_specs=[_row_rev(t, D, nt), _halo(t, D, nt), _row_rev(t, D, nt), _row_rev(t, D, nt),
                  ANY, _full((4, 3, FFN_HALF)), ANY],
        out_specs=[_row_rev(t, D, nt), ANY, _full((2, 3, FFN_HALF)), ANY],
        out_shape=[_sds((s, D)), _sds((2, D, FFN_HALF)), _sds((2, 3, FFN_HALF)), _sds((FFN_HALF, D))],
        scratch_shapes=[pltpu.VMEM((2, D, FFN_HALF), BF16), pltpu.VMEM((FFN_HALF, D), BF16),
                        pltpu.VMEM((2, D, FFN_HALF), F32), pltpu.VMEM((FFN_HALF, D), F32),
                        pltpu.VMEM((2, HALO, FFN_HALF), F32)],
        compiler_params=_params())(x1, x1, df, acc_in, wup4, cf4, wdn4)


def merge_bwd(oa, ob, p, x, dx1, wa, wb, wo, g1, b1, name):
    s = x.shape[0]
    t = _tile(s)

    def body(oa_ref, ob_ref, ga_ref, gb_ref, x_ref, dx1_ref, wa_ref, wb_ref, wo_ref, g_ref, b_ref,
             doa_ref, dob_ref, dga_ref, dgb_ref, dx_ref, dwa_ref, dwb_ref, dwo_ref, dg_ref, db_ref):
        first = pl.program_id(0) == 0
        oa = oa_ref[...]
        ob = ob_ref[...]
        ya = _mm(oa, wa_ref[...])
        yb = _mm(ob, wb_ref[...])
        h, vjp1 = jax.vjp(_merge_glue, ga_ref[...], gb_ref[...], ya, yb)
        hb = h.astype(BF16)
        r = _mm(hb, wo_ref[...])
        _, vjp2 = jax.vjp(_res_ln_glue, x_ref[...], r, g_ref[...], b_ref[...])
        dx, dr, dg, db = vjp2(dx1_ref[...])
        dx_ref[...] = dx
        _acc(dg_ref, dg, first)
        _acc(db_ref, db, first)
        drb = dr.astype(BF16)
        dh = _mm_nt(drb, wo_ref[...])
        _acc(dwo_ref, _mm_tn(hb, drb), first)
        dga, dgb, dya, dyb = vjp1(dh)
        dga_ref[...] = dga.astype(BF16)
        dgb_ref[...] = dgb.astype(BF16)
        dyab = dya.astype(BF16)
        dybb = dyb.astype(BF16)
        doa_ref[...] = _mm_nt(dyab, wa_ref[...]).astype(BF16)
        dob_ref[...] = _mm_nt(dybb, wb_ref[...]).astype(BF16)
        _acc(dwa_ref, _mm_tn(oa, dyab), first)
        _acc(dwb_ref, _mm_tn(ob, dybb), first)

    return pl.pallas_call(
        body, grid=(s // t,), name=name,
        in_specs=[_row(t, Z_W), _row(t, SG_W), _row(t, D, C_GA // D), _row(t, D, C_GB // D), _row(t, D), _row(t, D),
                  _full((Z_W, D)), _full((SG_W, D)), _full((D, D)), _full((1, D)), _full((1, D))],
        out_specs=[_row(t, Z_W), _row(t, SG_W), _row(t, D), _row(t, D), _row(t, D),
                   _full((Z_W, D)), _full((SG_W, D)), _full((D, D)), _full((1, D)), _full((1, D))],
        out_shape=[_sds((s, Z_W), BF16), _sds((s, SG_W), BF16), _sds((s, D), BF16), _sds((s, D), BF16), _sds((s, D)),
                   _sds((Z_W, D)), _sds((SG_W, D)), _sds((D, D)), _sds((1, D)), _sds((1, D))],
        compiler_params=_params())(oa, ob, p, p, x, dx1, wa, wb, wo, g1, b1)


def sg_bwd(p, dob, lng, lnb, w_s, bs_t, name):
    s = p.shape[0]
    t = _tile(s)

    def body(uv_ref, dob_ref, lng_ref, lnb_ref, ws_ref, bs_ref, duv_ref, dlng_ref, dlnb_ref, dws_ref, dbs_ref):
        first = pl.program_id(0) == 0
        _, vjp = jax.vjp(_sg_glue, uv_ref[...], lng_ref[...], lnb_ref[...], ws_ref[...], bs_ref[...])
        duv, dlng, dlnb, dws, dbs = vjp(dob_ref[...].astype(F32))
        duv_ref[...] = duv.astype(BF16)
        _acc(dlng_ref, dlng, first)
        _acc(dlnb_ref, dlnb, first)
        _acc(dws_ref, dws, first)
        _acc(dbs_ref, dbs, first)

    return pl.pallas_call(
        body, grid=(s // t,), name=name,
        in_specs=[_row(t, 2 * SG_W, C_UV // (2 * SG_W)), _row(t, SG_W), _full((1, SG_W)), _full((1, SG_W)),
                  _full((4, LANES, LANES)), _full((LANES, LANES))],
        out_specs=[_row(t, 2 * SG_W), _full((1, SG_W)), _full((1, SG_W)), _full((4, LANES, LANES)), _full((LANES, LANES))],
        out_shape=[_sds((s, 2 * SG_W), BF16), _sds((1, SG_W)), _sds((1, SG_W)), _sds((4, LANES, LANES)), _sds((LANES, LANES))],
        compiler_params=_params())(p, dob, lng, lnb, w_s, bs_t)


def dn_bwd(p, sst, doa, cq, a_row, dtb_row, nw_row, name):
    s = p.shape[0]
    t = _tile(s)
    nt = s // t

    def body(qkv_ref, halo_ref, z_ref, ba_ref, sst_ref, doa_ref, cq_ref, a_ref, dtb_ref, nw_ref,
             dqkv_ref, dz_ref, dba_ref, dcq_ref, da_ref, ddtb_ref, dnw_ref, ds_scr, carry):
        i = pl.program_id(0)
        j = nt - 1 - i
        first = i == 0

        @pl.when(first)
        def _():
            ds_scr[...] = jnp.zeros_like(ds_scr)
            carry[...] = jnp.zeros_like(carry)

        halo = jnp.where(j == 0, 0.0, halo_ref[...])
        qkvcat = jnp.concatenate([halo, qkv_ref[...]], axis=0)
        cw = tuple(cq_ref[k:k + 1, :] for k in range(4))
        _, vjp = jax.vjp(_dn_glue, qkvcat, z_ref[...], ba_ref[...], sst_ref[0], cw, a_ref[...], dtb_ref[...], nw_ref[...])
        dqkvcat, dz, dba, ds_in, dcw, da, ddtb, dnw = vjp((doa_ref[...].astype(F32), ds_scr[...]))
        ds_scr[...] = ds_in
        dz_ref[...] = dz.astype(BF16)
        dba_ref[...] = dba.astype(BF16)
        dtile = dqkvcat[HALO:]
        dqkv_ref[...] = dtile.astype(BF16)
        dqkv_ref[t - HALO:t, :] = (dtile[t - HALO:] + carry[...]).astype(BF16)
        carry[...] = dqkvcat[:HALO]
        for k in range(4):
            _acc(dcq_ref.at[k:k + 1, :], dcw[k], first)
        _acc(da_ref, da, first)
        _acc(ddtb_ref, ddtb, first)
        _acc(dnw_ref, dnw, first)

    return pl.pallas_call(
        body, grid=(nt,), name=name,
        in_specs=[_row_rev(t, QKV_W, nt), _halo(t, QKV_W, nt), _row_rev(t, Z_W, nt, C_Z // Z_W),
                  _row_rev(t, LANES, nt, C_BA // LANES),
                  pl.BlockSpec((1, HEADS, DK, DK), lambda i: (nt - 1 - i, 0, 0, 0)), _row_rev(t, Z_W, nt),
                  _full((4, QKV_W)), _full((1, LANES)), _full((1, LANES)), _full((1, LANES))],
        out_specs=[_row_rev(t, QKV_W, nt), _row_rev(t, Z_W, nt), _row_rev(t, LANES, nt),
                   _full((4, QKV_W)), _full((1, LANES)), _full((1, LANES)), _full((1, LANES))],
        out_shape=[_sds((s, QKV_W), BF16), _sds((s, Z_W), BF16), _sds((s, LANES), BF16),
                   _sds((4, QKV_W)), _sds((1, LANES)), _sds((1, LANES)), _sds((1, LANES))],
        scratch_shapes=[pltpu.VMEM((HEADS, DK, DK), F32), pltpu.VMEM((HALO, QKV_W), F32)],
        compiler_params=_params())(p, p, p, p, sst, doa, cq, a_row, dtb_row, nw_row)


def proj_bwd(dps, dxd, w, name):
    s = dxd.shape[0]
    t = _tile(s, 512)
    n = len(dps)

    def body(*refs):
        dp_refs, dxd_ref, w_hbm, dx_ref, w_v = refs[:n], refs[n], refs[n + 1], refs[n + 2], refs[n + 3]

        @pl.when(pl.program_id(0) == 0)
        def _():
            pltpu.sync_copy(w_hbm, w_v)

        dp = jnp.concatenate([r[...] for r in dp_refs], axis=1)
        dx_ref[...] = dxd_ref[...] + _mm_nt(dp, w_v[...])

    return pl.pallas_call(
        body, grid=(s // t,), name=name,
        in_specs=[_row(t, dp.shape[1]) for dp in dps] + [_row(t, D), ANY],
        out_specs=_row(t, D), out_shape=_sds((s, D)),
        scratch_shapes=[pltpu.VMEM((D, IN_COLS_PAD), BF16)],
        compiler_params=_params())(*dps, dxd, w)


def wgrad(x, dp, col, into, name):
    s, n = dp.shape
    tk = _tile(s, 1024)
    tn = next(c for c in (1024, 768, 512, 256, 128) if n % c == 0 and col % c == 0)
    block = col // tn

    def body(x_ref, dp_ref, *rest):
        o_ref = rest[-1]
        _acc(o_ref, _mm_tn(x_ref[...], dp_ref[...]), pl.program_id(1) == 0)

    operands = (x, dp) if into is None else (x, dp, into)
    return pl.pallas_call(
        body, grid=(n // tn, s // tk), name=name,
        in_specs=[pl.BlockSpec((tk, D), lambda j, k: (k, 0)), pl.BlockSpec((tk, tn), lambda j, k: (k, j))]
        + ([] if into is None else [ANY]),
        out_specs=pl.BlockSpec((D, tn), lambda j, k: (0, block + j)), out_shape=_sds((D, IN_COLS_PAD)),
        input_output_aliases={} if into is None else {2: 0},
        compiler_params=_params(2))(*operands)


def _rows_block(rows, cols):
    cap = max(HALO, (2 * 1024 * 1024) // (cols * 4))
    for cand in range(min(rows, cap) // HALO * HALO, HALO - 1, -HALO):
        if rows % cand == 0:
            return cand
    return rows


def adam_call(w, g, m, v, name):
    rows, cols = w.shape
    tr = _rows_block(rows, cols)
    c1 = 1.0 - ADAM_B1 ** ADAM_STEP
    c2 = 1.0 - ADAM_B2 ** ADAM_STEP

    def body(w_ref, g_ref, m_ref, v_ref, d_ref, nm_ref, nv_ref):
        gv = g_ref[...]
        nm = ADAM_B1 * m_ref[...] + (1.0 - ADAM_B1) * gv
        nv = ADAM_B2 * v_ref[...] + (1.0 - ADAM_B2) * (gv * gv)
        d_ref[...] = -ADAM_LR * ((nm / c1) / (jnp.sqrt(nv / c2) + ADAM_EPS) + ADAM_WD * w_ref[...])
        nm_ref[...] = nm
        nv_ref[...] = nv

    spec = pl.BlockSpec((tr, cols), lambda i: (i, 0))
    return pl.pallas_call(
        body, grid=(rows // tr,), name=name, in_specs=[spec] * 4, out_specs=[spec] * 3,
        out_shape=[_sds((rows, cols))] * 3, compiler_params=_params())(w, g, m, v)


def _place():
    return lax.axis_index("x"), lax.axis_index("y"), lax.axis_index("c")


def _other_chips(x, y):
    return [(1 - x, y), (x, 1 - y), (1 - x, 1 - y)]


def _remote(src, dst, send_sem, recv_sem, to):
    return pltpu.make_async_remote_copy(src_ref=src, dst_ref=dst, send_sem=send_sem, recv_sem=recv_sem,
                                        device_id=to, device_id_type=MESH)


class _Gather:
    def __init__(self, ins, outs, send_sems, recv_sems, local_sems):
        self.ins, self.outs, self.n = ins, outs, len(ins)
        self.send_sems, self.recv_sems, self.local_sems = send_sems, recv_sems, local_sems
        self.x, self.y, self.c = _place()
        self.me = 2 * self.x + self.y
        self.chips = _other_chips(self.x, self.y)

    def _copy(self, t, k, slot, part, to, src=None):
        dst = self.outs[t].at[slot, part]
        return _remote(dst if src is None else src, dst, self.send_sems.at[6 * t + k], self.recv_sems.at[6 * t + k], to)

    def _mine(self):
        return [pltpu.make_async_copy(self.ins[t].at[p], self.outs[t].at[self.me, p], self.local_sems.at[2 * t + p])
                for t in range(self.n) for p in range(2)]

    def _first(self):
        return [self._copy(t, k, self.me, self.c, (cx, cy, self.c), src=self.ins[t].at[self.c])
                for k, (cx, cy) in enumerate(self.chips) for t in range(self.n)]

    def start(self):
        for cp in self._mine() + self._first():
            cp.start()

    def finish(self):
        x, y, c = self.x, self.y, self.c
        passed = []
        for k, (cx, cy) in enumerate(self.chips):
            for t in range(self.n):
                self._copy(t, k, 2 * cx + cy, c, (x, y, c)).wait_recv()
                passed.append(self._copy(t, 3 + k, 2 * cx + cy, c, (x, y, 1 - c)))
                passed[-1].start()
        for k, (cx, cy) in enumerate(self.chips):
            for t in range(self.n):
                self._copy(t, 3 + k, 2 * cx + cy, 1 - c, (x, y, c)).wait_recv()
        for cp in self._first() + passed:
            cp.wait_send()
        for cp in self._mine():
            cp.wait()


def _gather_scratch(n):
    return [pltpu.SemaphoreType.DMA((6 * n,)), pltpu.SemaphoreType.DMA((6 * n,)), pltpu.SemaphoreType.DMA((2 * n,))]


def _gathered(shards):
    return [_sds((N_CHIPS,) + a.shape, a.dtype) for a in shards]


def gather_weights(shards, name):
    n = len(shards)

    def body(*refs):
        gather = _Gather(refs[:n], refs[n:2 * n], *refs[2 * n:])
        gather.start()
        gather.finish()

    return pl.pallas_call(
        body, name=name, in_specs=[ANY] * n, out_specs=[ANY] * n, out_shape=_gathered(shards),
        scratch_shapes=_gather_scratch(n))(*shards)


def _carrying_gather(body, n_in, n_out, shards, steps):
    n = len(shards)

    def both(*refs):
        ins, parts = refs[:n_in], refs[n_in:n_in + n]
        outs = refs[n_in + n:n_in + n + n_out]
        landed = refs[n_in + n + n_out:n_in + 2 * n + n_out]
        scratch, sems = refs[n_in + 2 * n + n_out:-3], refs[-3:]
        gather = _Gather(parts, landed, *sems)
        pl.when(pl.program_id(0) == 0)(gather.start)
        body(*ins, *outs, *scratch)
        pl.when(pl.program_id(0) == steps - 1)(gather.finish)

    return both


def pair_exchange(layer0, layer1, name):
    n = len(layer0)

    def body(*refs):
        srcs = (refs[:n], refs[n:2 * n])
        outs = refs[2 * n:3 * n]
        send_sems, recv_sems = refs[3 * n:]
        x, y, c = _place()
        for core in range(2):
            @pl.when(c == core)
            def _(core=core):
                cps = [_remote(srcs[1 - core][t], outs[t], send_sems.at[t], recv_sems.at[t], (x, y, 1 - c))
                       for t in range(n)]
                for cp in cps:
                    cp.start()
                for cp in cps:
                    cp.wait()

    return pl.pallas_call(
        body, name=name, in_specs=[ANY] * (2 * n), out_specs=[ANY] * n,
        out_shape=[_sds(a.shape) for a in layer0],
        scratch_shapes=[pltpu.SemaphoreType.DMA((n,)), pltpu.SemaphoreType.DMA((n,))],
    )(*layer0, *layer1)


def pair_add(a0, a1, theirs, c_vec, name):
    rows, cols = a0.shape
    tr = _rows_block(rows, cols)

    def body(c_ref, a0_ref, a1_ref, b_ref, o_ref):
        o_ref[...] = (jnp.where(c_ref[0] == 0, a0_ref[...], a1_ref[...]) + b_ref[...]).astype(BF16)

    def of_core(core):
        return pl.BlockSpec((tr, cols), lambda i, c: (jnp.where(c[0] == core, i, 0), 0))

    spec = pl.BlockSpec((tr, cols), lambda i, c: (i, 0))
    grid_spec = pltpu.PrefetchScalarGridSpec(
        num_scalar_prefetch=1, grid=(rows // tr,), in_specs=[of_core(0), of_core(1), spec], out_specs=spec)
    return pl.pallas_call(body, grid_spec=grid_spec, name=name, out_shape=_sds((rows, cols), BF16),
                          compiler_params=_params())(c_vec, a0, a1, theirs)


def scatter_chips(srcs, pieces, shard_shapes, name):
    n_src, n_t = len(srcs), len(pieces)

    def body(*refs):
        src_refs = refs[:n_src]
        outs = refs[n_src:n_src + n_t]
        send_sems, recv_sems, local_sems = refs[n_src + n_t:]
        x, y, c = _place()
        me = 2 * x + y

        def piece(t, k):
            idx, lead, rows, cols = pieces[t][k]
            ref = src_refs[idx]
            if lead is not None:
                ref = ref.at[lead]
            if rows is not None:
                ref = ref.at[pl.ds(rows[0], rows[1]), :]
            if cols is not None:
                ref = ref.at[:, pl.ds(cols[0], cols[1])]
            return ref

        def local(t, k):
            return pltpu.make_async_copy(piece(t, k), outs[t].at[k], local_sems.at[t])

        for k in range(N_CHIPS):
            @pl.when(me == k)
            def _(k=k):
                for t in range(n_t):
                    local(t, k).start()

            @pl.when(me != k)
            def _(k=k):
                for t in range(n_t):
                    _remote(piece(t, k), outs[t].at[me], send_sems.at[N_CHIPS * t + k],
                            recv_sems.at[N_CHIPS * t + me], (k // 2, k % 2, c)).start()

        for k in range(N_CHIPS):
            @pl.when(me != k)
            def _(k=k):
                for t in range(n_t):
                    cp = _remote(piece(t, k), outs[t].at[k], send_sems.at[N_CHIPS * t + k],
                                 recv_sems.at[N_CHIPS * t + k], (x, y, c))
                    cp.wait_recv()
                    cp.wait_send()

            @pl.when(me == k)
            def _(k=k):
                for t in range(n_t):
                    local(t, k).wait()

    return pl.pallas_call(
        body, name=name, in_specs=[ANY] * n_src, out_specs=[ANY] * n_t,
        out_shape=[_sds((N_CHIPS,) + tuple(shp), srcs[0].dtype) for shp in shard_shapes],
        scratch_shapes=[pltpu.SemaphoreType.DMA((N_CHIPS * n_t,)), pltpu.SemaphoreType.DMA((N_CHIPS * n_t,)),
                        pltpu.SemaphoreType.DMA((n_t,))],
    )(*srcs)


def chips_add(recv, c_vec, name):
    n, a, b = recv.shape
    tr = _rows_block(a, b)

    def body(c_ref, r0, r1, r2, r3, o_ref):
        o_ref[...] = ((r0[...].astype(F32) + r1[...].astype(F32)) + r2[...].astype(F32)) + r3[...].astype(F32)

    grid_spec = pltpu.PrefetchScalarGridSpec(
        num_scalar_prefetch=1, grid=(a // tr,),
        in_specs=[pl.BlockSpec((None, tr, b), lambda i, c, k=k: (k, i, 0)) for k in range(n)],
        out_specs=pl.BlockSpec((None, tr, b), lambda i, c: (c[0], i, 0)))
    return pl.pallas_call(body, grid_spec=grid_spec, name=name, out_shape=_sds((2, a, b)),
                          compiler_params=_params())(c_vec, *([recv] * n))


def pair_join(bufs, name):
    n = len(bufs)

    def body(*refs):
        ins, outs = refs[:n], refs[n:2 * n]
        send_sems, recv_sems = refs[2 * n:]
        x, y, c = _place()
        cps = [_remote(ins[t].at[c], outs[t].at[c], send_sems.at[t], recv_sems.at[t], (x, y, 1 - c)) for t in range(n)]
        for cp in cps:
            cp.start()
        for t in range(n):
            cps[t].wait_send()
            _remote(ins[t].at[c], outs[t].at[1 - c], send_sems.at[t], recv_sems.at[t], (x, y, c)).wait_recv()

    return pl.pallas_call(
        body, name=name, in_specs=[ANY] * n, out_specs=[ANY] * n, out_shape=[_sds(a.shape) for a in bufs],
        input_output_aliases={t: t for t in range(n)},
        scratch_shapes=[pltpu.SemaphoreType.DMA((n,)), pltpu.SemaphoreType.DMA((n,))],
    )(*bufs)


def allsum_small(v, name):
    rows, lanes = v.shape
    n_dev = 8

    def body(v_ref, out_ref, buf, send_sems, recv_sems):
        x, y, c = _place()
        me, sibling = (x, y, c), (x, y, 1 - c)
        chips = _other_chips(x, y)

        def slot(px, py, pc):
            return buf.at[4 * px + 2 * py + pc]

        def copy(k, block, to, src=None):
            return pltpu.make_async_remote_copy(
                src_ref=slot(*block) if src is None else src, dst_ref=slot(*block),
                send_sem=send_sems.at[k], recv_sem=recv_sems.at[k], device_id=to, device_id_type=MESH)

        slot(*me)[...] = v_ref[...]
        first = [copy(0, me, sibling, src=v_ref)]
        first += [copy(1 + k, me, (*chip, c), src=v_ref) for k, chip in enumerate(chips)]
        for cp in first:
            cp.start()
        passed = [copy(4 + k, (*chip, c), sibling) for k, chip in enumerate(chips)]
        for k, chip in enumerate(chips):
            copy(1 + k, (*chip, c), me).wait_recv()
            passed[k].start()
        copy(0, sibling, me).wait_recv()
        for k, chip in enumerate(chips):
            copy(4 + k, (*chip, 1 - c), me).wait_recv()
        for cp in first + passed:
            cp.wait_send()
        acc = buf[0]
        for d in range(1, n_dev):
            acc = acc + buf[d]
        out_ref[...] = acc

    vm = pl.BlockSpec(memory_space=pltpu.VMEM)
    return pl.pallas_call(
        body, name=name, in_specs=[vm], out_specs=vm, out_shape=_sds((rows, lanes)),
        scratch_shapes=[pltpu.VMEM((n_dev, rows, lanes), F32), pltpu.SemaphoreType.DMA((7,)), pltpu.SemaphoreType.DMA((7,))],
        compiler_params=pltpu.CompilerParams(vmem_limit_bytes=VMEM_LIMIT),
    )(v)


BIG = ("w_in", "w_branch_a", "w_branch_b", "w_out", "w_up", "w_down")
CONV = ("conv_qkv", "conv_ffn")
REPL =("a_log", "dt_bias", "dn_norm_w", "sg_ln_g", "sg_ln_b", "w_spatial", "b_spatial", "ln1_g", "ln1_b", "ln2_g", "ln2_b")


def _pad_rows(flat, mult):
    n = flat.shape[0]
    unit = mult * LANES
    total = -(-n // unit) * unit
    return jnp.pad(flat, (0, total - n)).reshape(total // LANES, LANES)


def _pack(arrs, mult):
    return _pad_rows(jnp.concatenate([a.reshape(-1) for a in arrs]), mult)


def _unpack(flat, shapes):
    out, off = [], 0
    for shp in shapes:
        n = math.prod(shp)
        out.append(flat[off:off + n].reshape(shp))
        off += n
    return out


def kernel(x, w_in, conv_qkv, a_log, dt_bias, dn_norm_w, w_branch_a, sg_ln_g, sg_ln_b, w_spatial, b_spatial, w_branch_b, w_out, ln1_g, ln1_b, w_up, conv_ffn, w_down, ln2_g, ln2_b, loss_target, m_w_in, m_conv_qkv, m_a_log, m_dt_bias, m_dn_norm_w, m_w_branch_a, m_sg_ln_g, m_sg_ln_b, m_w_spatial, m_b_spatial, m_w_branch_b, m_w_out, m_ln1_g, m_ln1_b, m_w_up, m_conv_ffn, m_w_down, m_ln2_g, m_ln2_b, v_w_in, v_conv_qkv, v_a_log, v_dt_bias, v_dn_norm_w, v_w_branch_a, v_sg_ln_g, v_sg_ln_b, v_w_spatial, v_b_spatial, v_w_branch_b, v_w_out, v_ln1_g, v_ln1_b, v_w_up, v_conv_ffn, v_w_down, v_ln2_g, v_ln2_b):
    names = ("w_in", "conv_qkv", "a_log", "dt_bias", "dn_norm_w", "w_branch_a", "sg_ln_g", "sg_ln_b", "w_spatial",
             "b_spatial", "w_branch_b", "w_out", "ln1_g", "ln1_b", "w_up", "conv_ffn", "w_down", "ln2_g", "ln2_b")
    w = dict(zip(names, (w_in, conv_qkv, a_log, dt_bias, dn_norm_w, w_branch_a, sg_ln_g, sg_ln_b, w_spatial,
                         b_spatial, w_branch_b, w_out, ln1_g, ln1_b, w_up, conv_ffn, w_down, ln2_g, ln2_b)))
    m = dict(zip(names, (m_w_in, m_conv_qkv, m_a_log, m_dt_bias, m_dn_norm_w, m_w_branch_a, m_sg_ln_g, m_sg_ln_b,
                         m_w_spatial, m_b_spatial, m_w_branch_b, m_w_out, m_ln1_g, m_ln1_b, m_w_up, m_conv_ffn,
                         m_w_down, m_ln2_g, m_ln2_b)))
    v = dict(zip(names, (v_w_in, v_conv_qkv, v_a_log, v_dt_bias, v_dn_norm_w, v_w_branch_a, v_sg_ln_g, v_sg_ln_b,
                         v_w_spatial, v_b_spatial, v_w_branch_b, v_w_out, v_ln1_g, v_ln1_b, v_w_up, v_conv_ffn,
                         v_w_down, v_ln2_g, v_ln2_b)))
    chip = 2 * lax.axis_index("x") + lax.axis_index("y")
    s = x.shape[1]
    xs = x.reshape(s, D)
    tgt = loss_target.reshape(s, D)

    big_names, conv_names = list(BIG), list(CONV)

    def in_two(name, l):
        rows, cols = w[name].shape[1:]
        return w[name][l].astype(BF16).reshape(2, rows // 2, cols)

    def whole(name, landed):
        rows, cols = w[name].shape[1:]
        return landed.reshape(N_CHIPS, rows, cols)

    first = gather_weights([in_two("w_in", 0)] + [w[n] for n in conv_names], "gather_first")
    got = [{"w_in": whole("w_in", first[0])}, {}]
    conv_taps = dict(zip(conv_names, first[1:]))
    narrow, wide = ["w_branch_a", "w_branch_b", "w_out"], ["w_up", "w_down"]
    carried = {"proj_fwd0": (0, narrow), "dn_fwd0": (0, wide), "ffn_fwd0": (1, ["w_in"] + narrow), "dn_fwd1": (1, wide)}

    def carry(call):
        l, which = carried.get(call, (0, []))
        return [in_two(n, l) for n in which]

    def land(call, landed):
        l, which = carried.get(call, (0, []))
        got[l].update({n: whole(n, a) for n, a in zip(which, landed)})

    def lane_row(vec, off):
        return jnp.zeros((1, LANES), F32).at[0, off:off + vec.shape[0]].set(vec)

    def side_by_side(blocks):
        return jnp.concatenate([blocks[k] for k in range(N_CHIPS)], axis=1)

    def small_params(l):
        return dict(
            cq=side_by_side(conv_taps["conv_qkv"][:, l]),
            a_row=lane_row(w["a_log"][l], HEADS), dtb_row=lane_row(w["dt_bias"][l], HEADS),
            nw_row=w["dn_norm_w"][l].reshape(1, DK),
            lng=w["sg_ln_g"][l].reshape(1, SG_W), lnb=w["sg_ln_b"][l].reshape(1, SG_W),
            w_s=w["w_spatial"][l], bs_t=jnp.zeros((LANES, LANES), F32).at[:, :4].set(w["b_spatial"][l].T),
            g1=w["ln1_g"][l].reshape(1, D), b1=w["ln1_b"][l].reshape(1, D),
            cf=conv_taps["conv_ffn"][:, l],
            g2=w["ln2_g"][l].reshape(1, D), b2=w["ln2_b"][l].reshape(1, D))

    layers, saved = [], []
    h_in = xs
    for l in range(DEPTH):
        p = small_params(l)
        wi = side_by_side(got[l]["w_in"])
        p["w_in"] = jnp.concatenate([wi[:, :2048], wi[:, 2056:3080], wi[:, 3080:5128], wi[:, 2048:2056],
                                     jnp.zeros((D, IN_COLS_PAD - 5128), BF16)], axis=1)
        proj, *landed = proj_fwd(h_in, p["w_in"], f"proj_fwd{l}", gather=carry(f"proj_fwd{l}"))
        land(f"proj_fwd{l}", landed)
        oa, sst, *landed = dn_fwd(proj, p["cq"], p["a_row"], p["dtb_row"], p["nw_row"], f"dn_fwd{l}",
                                  gather=carry(f"dn_fwd{l}"))
        land(f"dn_fwd{l}", landed)
        ob = sg_fwd(proj, p["lng"], p["lnb"], p["w_s"], p["bs_t"], f"sg_fwd{l}")
        p.update(wa=side_by_side(got[l]["w_branch_a"]), wb=side_by_side(got[l]["w_branch_b"]),
                 wo=got[l]["w_out"].reshape(D, D))
        x1 = merge_fwd(oa, ob, proj, h_in, p["wa"], p["wb"], p["wo"], p["g1"], p["b1"], f"merge_fwd{l}")
        pre2, x2, *landed = ffn_fwd(x1, got[l]["w_up"], p["cf"], got[l]["w_down"], p["g2"], p["b2"], f"ffn_fwd{l}",
                                    gather=carry(f"ffn_fwd{l}"))
        land(f"ffn_fwd{l}", landed)
        layers.append(p)
        saved.append(dict(x=h_in, proj=proj, oa=oa, ob=ob, sst=sst, x1=x1, pre2=pre2))
        h_in = x2

    dy, loss_part = loss_call(h_in, tgt, "loss")
    loss = lax.psum(loss_part[0, 0], ("x", "y", "c"))

    small_names = conv_names + list(REPL)
    grads = {n: [None] * DEPTH for n in small_names}
    big_grads = [None] * DEPTH
    for l in reversed(range(DEPTH)):
        p, a = layers[l], saved[l]
        dpre2, dg2, db2 = ln_bwd(a["pre2"], dy, p["g2"], p["b2"], f"ln2_bwd{l}")
        dx1, dwup0, dcf0, dwdn0 = ffn_bwd(a["x1"], dpre2, dpre2, ALPHA, got[l]["w_up"], p["cf"], got[l]["w_down"], 0, f"ffn_bwd{l}a")
        dx1, dwup1, dcf1, dwdn1 = ffn_bwd(a["x1"], dpre2, dx1, 1.0, got[l]["w_up"], p["cf"], got[l]["w_down"], 1, f"ffn_bwd{l}b")
        doa, dob, dga, dgb, dxd, dwa, dwb, dwo, dg1, db1 = merge_bwd(
            a["oa"], a["ob"], a["proj"], a["x"], dx1, p["wa"], p["wb"], p["wo"], p["g1"], p["b1"], f"merge_bwd{l}")
        duv, dlng, dlnb, dws, dbs = sg_bwd(a["proj"], dob, p["lng"], p["lnb"], p["w_s"], p["bs_t"], f"sg_bwd{l}")
        dqkv, dz, dba, dcq, da, ddtb, dnw = dn_bwd(a["proj"], a["sst"], doa, p["cq"], p["a_row"], p["dtb_row"],
                                                   p["nw_row"], f"dn_bwd{l}")
        dy = proj_bwd([dqkv, dz, duv, dga, dgb, dba], dxd, p["w_in"], f"proj_bwd{l}")
        dwi = None
        for tag, dp, col in (("qkv", dqkv, 0), ("z", dz, C_Z), ("uv", duv, C_UV), ("ga", dga, C_GA), ("gb", dgb, C_GB),
                             ("ba", dba, C_BA)):
            dwi = wgrad(a["x"], dp, col, dwi, f"wgrad_in{l}_{tag}")

        big_grads[l] = [dwi, dwa, dwb, dwo, dwup0.reshape(2 * D, FFN_HALF), dwup1.reshape(2 * D, FFN_HALF), dwdn0, dwdn1]
        grads["conv_qkv"][l] = dcq
        grads["conv_ffn"][l] = jnp.concatenate([dcf0[0], dcf1[0], dcf0[1], dcf1[1]], axis=1)
        grads["a_log"][l] = da[0, HEADS:2 * HEADS]
        grads["dt_bias"][l] = ddtb[0, HEADS:2 * HEADS]
        grads["dn_norm_w"][l] = dnw[0]
        grads["sg_ln_g"][l] = dlng[0]
        grads["sg_ln_b"][l] = dlnb[0]
        grads["w_spatial"][l] = dws
        grads["b_spatial"][l] = dbs[:, :4].T
        grads["ln1_g"][l] = dg1[0]
        grads["ln1_b"][l] = db1[0]
        grads["ln2_g"][l] = dg2[0]
        grads["ln2_b"][l] = db2[0]
    grad_x = dy.reshape(x.shape)
    g_full = {n: jnp.stack(grads[n]) for n in small_names}

    c_vec = jnp.stack([lax.axis_index("c")]).astype(jnp.int32)
    theirs = pair_exchange(big_grads[0], big_grads[1], "reduce_pair")
    tags = ("w_in", "w_a", "w_b", "w_out", "w_up0", "w_up1", "w_dn0", "w_dn1")
    pin, pa, pb, po, pup0, pup1, pdn0, pdn1 = [
        pair_add(a0, a1, th, c_vec, f"reduce_pair_add_{tag}")
        for tag, a0, a1, th in zip(tags, big_grads[0], big_grads[1], theirs)]
    natural = jnp.concatenate([pin[:, :2048], pin[:, C_BA:C_BA + 8], pin[:, 2048:C_BA]], axis=1)
    srcs = [jnp.stack(jnp.split(natural, N_CHIPS, axis=1)), pa, pb, po,
            pup0.reshape(2, D, FFN_HALF), pup1.reshape(2, D, FFN_HALF), pdn0, pdn1]
    ab_cols, out_rows = w["w_branch_a"].shape[2], w["w_out"].shape[1]
    pieces = [
        [(0, k, None, None) for k in range(N_CHIPS)],
        [(1, None, None, (k * ab_cols, ab_cols)) for k in range(N_CHIPS)],
        [(2, None, None, (k * ab_cols, ab_cols)) for k in range(N_CHIPS)],
        [(3, None, (k * out_rows, out_rows), None) for k in range(N_CHIPS)],
        [(4 + k % 2, k // 2, None, None) for k in range(N_CHIPS)],
        [(6 + k // 2, None, ((k % 2) * DN_SHARD, DN_SHARD), None) for k in range(N_CHIPS)],
    ]
    recv = scatter_chips(srcs, pieces, [w[n].shape[1:] for n in big_names], "reduce_chips")
    sums = [chips_add(r, c_vec, f"reduce_chips_add_{n}") for n, r in zip(big_names, recv)]
    g_shard = dict(zip(big_names, pair_join(sums, "reduce_join")))

    small = allsum_small(_pack([g_full[n] for n in small_names], 8), "reduce_small").reshape(-1)
    small_full = dict(zip(small_names, _unpack(small, [g_full[n].shape for n in small_names])))
    for n in conv_names:
        width = w[n].shape[2]
        g_shard[n] = lax.dynamic_slice_in_dim(small_full[n], chip * width, width, axis=2)
    for n in REPL:
        g_shard[n] = small_full[n]

    delta, new_m, new_v = {}, {}, {}
    for n in big_names:
        shp = w[n].shape
        two_d = (shp[0] * shp[1], shp[2])
        d_, m_, v_ = adam_call(w[n].reshape(two_d), g_shard[n].reshape(two_d), m[n].reshape(two_d), v[n].reshape(two_d), f"adam_{n}")
        delta[n], new_m[n], new_v[n] = d_.reshape(shp), m_.reshape(shp), v_.reshape(shp)
    shapes = [w[n].shape for n in small_names]
    packs = [_pack([src[n] for n in small_names], 8) for src in (w, g_shard, m, v)]
    outs = adam_call(*packs, "adam_small")
    for dst, o in zip((delta, new_m, new_v), outs):
        dst.update(zip(small_names, _unpack(o.reshape(-1), shapes)))

    return (loss, grad_x, *[g_shard[n] for n in names], *[delta[n] for n in names],
            *[new_m[n] for n in names], *[new_v[n] for n in names])
```

```python
import functools
import math

import jax
import jax.numpy as jnp
from jax import lax
from jax.experimental import pallas as pl
from jax.experimental.pallas import tpu as pltpu

F32 = jnp.float32
BF16 = jnp.bfloat16
HI = lax.Precision.HIGHEST
MID = lax.Precision.HIGH
MESH = pl.DeviceIdType.MESH

D = 1024
DEPTH = 2
HEADS = 4
DK = 128
CHUNK = 64
QKV_W = 1536
Z_W = 512
SG_W = 512
FFN = 2816
FFN_HALF = FFN // 2
N_CHIPS = 4
DN_SHARD = FFN // N_CHIPS
LN_EPS = 1e-5
RMS_EPS = 1e-6
L2_EPS = 1e-6
ALPHA = (2 * DEPTH) ** 0.25
ADAM_LR, ADAM_B1, ADAM_B2, ADAM_EPS, ADAM_WD, ADAM_STEP = 0.001, 0.9, 0.999, 1e-08, 0.01, 10

HALO = 16
LANES = 128
IN_COLS_PAD = 5248
C_Z, C_UV, C_GA, C_GB, C_BA = 1536, 2048, 3072, 4096, 5120
VMEM_LIMIT = 56 * 1024 * 1024


def _params(n_grid=1):
    return pltpu.CompilerParams(dimension_semantics=("arbitrary",) * n_grid, vmem_limit_bytes=VMEM_LIMIT)


def _mm(a, b):
    return jnp.dot(a.astype(BF16), b.astype(BF16), preferred_element_type=F32)


def _mm_nt(a, b):
    return lax.dot_general(a.astype(BF16), b.astype(BF16), (((1,), (1,)), ((), ())), preferred_element_type=F32)


def _mm_tn(a, b):
    return lax.dot_general(a.astype(BF16), b.astype(BF16), (((0,), (0,)), ((), ())), preferred_element_type=F32)


def _bdot(a, b, prec=MID):
    return lax.dot_general(a, b, (((2,), (1,)), ((0,), (0,))), precision=prec, preferred_element_type=F32)


def _bdot_nt(a, b, prec=MID):
    return lax.dot_general(a, b, (((2,), (2,)), ((0,), (0,))), precision=prec, preferred_element_type=F32)


def _bf16_dot(a, b, contract):
    return lax.dot_general(a.astype(BF16), b.astype(BF16), (contract, ((0,), (0,))), preferred_element_type=F32)


@jax.custom_vjp
def _fdot(a, b):
    return _bf16_dot(a, b, ((2,), (1,)))


def _fdot_fwd(a, b):
    return _fdot(a, b), (a, b)


def _fdot_bwd(res, ct):
    a, b = res
    return _bf16_dot(ct, b, ((2,), (2,))), _bf16_dot(a, ct, ((1,), (1,)))


_fdot.defvjp(_fdot_fwd, _fdot_bwd)


@jax.custom_vjp
def _fdot_nt(a, b):
    return _bf16_dot(a, b, ((2,), (2,)))


def _fdot_nt_fwd(a, b):
    return _fdot_nt(a, b), (a, b)


def _fdot_nt_bwd(res, ct):
    a, b = res
    return _bf16_dot(ct, b, ((2,), (1,))), _bf16_dot(ct, a, ((1,), (1,)))


_fdot_nt.defvjp(_fdot_nt_fwd, _fdot_nt_bwd)


@jax.custom_vjp
def _fdot_tn(a, b):
    return _bf16_dot(a, b, ((1,), (1,)))


def _fdot_tn_fwd(a, b):
    return _fdot_tn(a, b), (a, b)


def _fdot_tn_bwd(res, ct):
    a, b = res
    return _bf16_dot(b, ct, ((2,), (2,))), _bf16_dot(a, ct, ((2,), (1,)))


_fdot_tn.defvjp(_fdot_tn_fwd, _fdot_tn_bwd)


def _ln(x, g, b):
    mu = jnp.mean(x, axis=-1, keepdims=True)
    xc = x - mu
    var = jnp.mean(xc * xc, axis=-1, keepdims=True)
    return xc * lax.rsqrt(var + LN_EPS) * g + b


def _shift_rows(x, s):
    s = s % x.shape[0]
    return x if s == 0 else pltpu.roll(x, s, 0)


@jax.custom_vjp
def _conv(xcat, w):
    k_taps = len(w)
    y = None
    for k in range(k_taps):
        t = _shift_rows(xcat, k_taps - 1 - k)[HALO:] * w[k]
        y = t if y is None else y + t
    return y


def _conv_fwd(xcat, w):
    return _conv(xcat, w), (xcat, w)


def _conv_bwd(res, dy):
    xcat, w = res
    k_taps = len(w)
    dyp = jnp.concatenate([jnp.zeros((HALO, dy.shape[1]), dy.dtype), dy], axis=0)
    dx = None
    dws = []
    for k in range(k_taps):
        s = k_taps - 1 - k
        t = _shift_rows(dyp, -s) * w[k]
        dx = t if dx is None else dx + t
        dws.append(jnp.sum(_shift_rows(xcat, s)[HALO:] * dy, axis=0, keepdims=True))
    return dx, tuple(dws)


_conv.defvjp(_conv_fwd, _conv_bwd)


@jax.custom_vjp
def _tri_inv(l):
    n = l.shape[-1]
    r = lax.broadcasted_iota(jnp.int32, (n, n), 0)
    c = lax.broadcasted_iota(jnp.int32, (n, n), 1)
    eye = (r == c).astype(F32)
    p = eye - l
    lp = l
    steps = int(math.log2(n)) - 1
    for i in range(steps):
        dot = _bdot if i < 2 else functools.partial(_bf16_dot, contract=((2,), (1,)))
        lp = dot(lp, lp)
        p = p + dot(p, lp)
    return p


def _tri_inv_fwd(l):
    t = _tri_inv(l)
    return t, t


def _tri_inv_bwd(t, dt):
    tt = jnp.swapaxes(t, 1, 2)
    return (-_bdot(tt, _bdot(dt, tt)),)


_tri_inv.defvjp(_tri_inv_fwd, _tri_inv_bwd)


def _dn_glue(qkvcat, z, ba, s_in, cw, a_row, dtb_row, nw_row):
    t_rows = z.shape[0]
    nc = t_rows // CHUNK
    nb = nc * HEADS

    qkv = jax.nn.silu(_conv(qkvcat, cw))

    def chunks(t, off):
        return jnp.stack([t[n * CHUNK:(n + 1) * CHUNK, off + h * DK: off + (h + 1) * DK]
                          for n in range(nc) for h in range(HEADS)])

    q = chunks(qkv, 0)
    k = chunks(qkv, 512)
    v = chunks(qkv, 1024)
    q = q * lax.rsqrt(jnp.sum(q * q, axis=-1, keepdims=True) + L2_EPS) * (DK ** -0.5)
    k = k * lax.rsqrt(jnp.sum(k * k, axis=-1, keepdims=True) + L2_EPS)

    lane = lax.broadcasted_iota(jnp.int32, (LANES, HEADS * DK), 0)
    head_of_col = lax.broadcasted_iota(jnp.int32, (LANES, HEADS * DK), 1) // DK
    e_beta = (head_of_col == lane).astype(F32)
    e_g = (head_of_col + HEADS == lane).astype(F32)
    beta_l = jax.nn.sigmoid(ba)
    g_l = -jnp.exp(a_row) * jax.nn.softplus(ba + dtb_row)
    beta = chunks(jnp.dot(beta_l, e_beta, precision=MID, preferred_element_type=F32), 0)
    g = chunks(jnp.dot(g_l, e_g, precision=MID, preferred_element_type=F32), 0)

    r = lax.broadcasted_iota(jnp.int32, (CHUNK, CHUNK), 0)
    c = lax.broadcasted_iota(jnp.int32, (CHUNK, CHUNK), 1)
    causal = r >= c
    strict = r > c
    tril_b = jnp.broadcast_to(causal.astype(F32), (nb, CHUNK, CHUNK))
    gi_b = _bdot(tril_b, g, HI)
    gi = gi_b[:, :, :CHUNK]
    gj = jnp.swapaxes(gi, 1, 2)
    decay = jnp.where(causal, jnp.exp(jnp.where(causal, gi - gj, 0.0)), 0.0)
    kb = k * beta
    l_mat = jnp.where(strict, _fdot_nt(kb, k) * decay, 0.0)
    t_mat = _tri_inv(l_mat)
    e_gi = jnp.exp(gi_b)
    w_mat = _fdot(t_mat, kb * e_gi)
    u_mat = _fdot(t_mat, v * beta)
    a_qk = _fdot_nt(q, k) * decay
    q_g = q * e_gi
    gl_b = jnp.broadcast_to(jnp.sum(g, axis=1, keepdims=True), g.shape)
    k_d = k * jnp.exp(gl_b - gi_b)
    e_gl = jnp.exp(gl_b)
    g_last = jnp.concatenate([e_gl, e_gl], axis=1)

    state = s_in
    rows = []
    for n in range(nc):
        sl = slice(n * HEADS, (n + 1) * HEADS)
        u_new = u_mat[sl] - _fdot(w_mat[sl], state)
        o_n = _fdot(q_g[sl], state) + _fdot(a_qk[sl], u_new)
        state = state * g_last[sl] + _fdot_tn(k_d[sl], u_new)
        o_n = o_n * lax.rsqrt(jnp.mean(o_n * o_n, axis=-1, keepdims=True) + RMS_EPS) * nw_row
        z_n = jnp.stack([z[n * CHUNK:(n + 1) * CHUNK, h * DK:(h + 1) * DK] for h in range(HEADS)])
        o_n = o_n * jax.nn.silu(z_n)
        rows.append(jnp.concatenate([o_n[h] for h in range(HEADS)], axis=-1))
    return jnp.concatenate(rows, axis=0), state


def _sg_glue(uv, lng, lnb, w_s, bs_t):
    t_rows = uv.shape[0]
    y = jax.nn.gelu(uv)
    u = y[:, :SG_W]
    v = _ln(y[:, SG_W:], lng, lnb)
    r = lax.broadcasted_iota(jnp.int32, (LANES, LANES), 0)
    c = lax.broadcasted_iota(jnp.int32, (LANES, LANES), 1)
    wm = jnp.where(r >= c, w_s, 0.0)
    lane = lax.broadcasted_iota(jnp.int32, (LANES, SG_W), 0)
    group_of_col = lax.broadcasted_iota(jnp.int32, (LANES, SG_W), 1) // LANES
    e_grp = (group_of_col == lane).astype(F32)
    bias = jnp.dot(bs_t, e_grp, precision=HI, preferred_element_type=F32)
    outs = []
    for n in range(t_rows // LANES):
        vb = v[n * LANES:(n + 1) * LANES]
        vg = jnp.stack([vb[:, g * LANES:(g + 1) * LANES] for g in range(4)])
        mg = _fdot(wm, vg)
        mixed = jnp.concatenate([mg[g] for g in range(4)], axis=-1) + bias
        outs.append(u[n * LANES:(n + 1) * LANES] * mixed)
    return jnp.concatenate(outs, axis=0)


def _merge_glue(ga, gb, ya, yb):
    return jax.nn.sigmoid(ga) * ya + jax.nn.sigmoid(gb) * yb


def _res_ln_glue(x, r, g, b):
    return _ln(ALPHA * x + r, g, b)


def _ffn_glue(ua, ub, cwa, cwb):
    return jax.nn.silu(_conv(ua, cwa)) * _conv(ub, cwb)


def _row(t, c, col=0):
    return pl.BlockSpec((t, c), lambda i: (i, col))


def _row_rev(t, c, nt, col=0):
    return pl.BlockSpec((t, c), lambda i: (nt - 1 - i, col))


def _halo(t, c, nt=None):
    per = t // HALO
    if nt is None:
        return pl.BlockSpec((HALO, c), lambda i: (jnp.maximum(i * per - 1, 0), 0))
    return pl.BlockSpec((HALO, c), lambda i: (jnp.maximum((nt - 1 - i) * per - 1, 0), 0))


def _full(shape):
    nd = len(shape)
    return pl.BlockSpec(shape, lambda i: (0,) * nd)


ANY = pl.BlockSpec(memory_space=pl.ANY)


def _sds(shape, dtype=F32):
    return jax.ShapeDtypeStruct(shape, dtype)


def _tile(s, want=256):
    for t in (want, 256, 128):
        if s % t == 0:
            return t
    raise ValueError(f"sequence length {s} is not a multiple of 128")


def proj_fwd(x, w, name, gather=()):
    s = x.shape[0]
    t = _tile(s)
    nt = s // t
    segs = [(0, 2048), (2048, 3072), (3072, 4096), (4096, 5120), (5120, IN_COLS_PAD)]

    def body(x_ref, w_ref, p_ref):
        xb = x_ref[...].astype(BF16)
        for lo, hi in segs:
            p_ref[:, lo:hi] = jnp.dot(xb, w_ref[:, lo:hi], preferred_element_type=F32)

    n = len(gather)
    return pl.pallas_call(
        _carrying_gather(body, 2, 1, gather, nt) if gather else body, grid=(nt,), name=name,
        in_specs=[_row(t, D), _full((D, IN_COLS_PAD))] + [ANY] * n,
        out_specs=[_row(t, IN_COLS_PAD)] + [ANY] * n,
        out_shape=[_sds((s, IN_COLS_PAD))] + _gathered(gather),
        scratch_shapes=_gather_scratch(n) if gather else [], compiler_params=_params())(x, w, *gather)


def dn_fwd(p, cq, a_row, dtb_row, nw_row, name, gather=()):
    s = p.shape[0]
    t = _tile(s)
    nt = s // t

    def body(qkv_ref, halo_ref, z_ref, ba_ref, cq_ref, a_ref, dtb_ref, nw_ref, o_ref, sst_ref, s_scr):
        i = pl.program_id(0)

        @pl.when(i == 0)
        def _():
            s_scr[...] = jnp.zeros_like(s_scr)

        halo = jnp.where(i == 0, 0.0, halo_ref[...])
        qkvcat = jnp.concatenate([halo, qkv_ref[...]], axis=0)
        cw = tuple(cq_ref[k:k + 1, :] for k in range(4))
        s_in = s_scr[...]
        sst_ref[0] = s_in
        o, s_out = _dn_glue(qkvcat, z_ref[...], ba_ref[...], s_in, cw, a_ref[...], dtb_ref[...], nw_ref[...])
        o_ref[...] = o.astype(BF16)
        s_scr[...] = s_out

    n = len(gather)
    return pl.pallas_call(
        _carrying_gather(body, 8, 2, gather, nt) if gather else body, grid=(nt,), name=name,
        in_specs=[_row(t, QKV_W), _halo(t, QKV_W), _row(t, Z_W, C_Z // Z_W), _row(t, LANES, C_BA // LANES),
                  _full((4, QKV_W)), _full((1, LANES)), _full((1, LANES)), _full((1, LANES))] + [ANY] * n,
        out_specs=[_row(t, Z_W), pl.BlockSpec((1, HEADS, DK, DK), lambda i: (i, 0, 0, 0))] + [ANY] * n,
        out_shape=[_sds((s, Z_W), BF16), _sds((nt, HEADS, DK, DK))] + _gathered(gather),
        scratch_shapes=[pltpu.VMEM((HEADS, DK, DK), F32)] + (_gather_scratch(n) if gather else []),
        compiler_params=_params())(p, p, p, p, cq, a_row, dtb_row, nw_row, *gather)


def sg_fwd(p, lng, lnb, w_s, bs_t, name):
    s = p.shape[0]
    t = _tile(s, 512)

    def body(uv_ref, lng_ref, lnb_ref, ws_ref, bs_ref, o_ref):
        o_ref[...] = _sg_glue(uv_ref[...], lng_ref[...], lnb_ref[...], ws_ref[...], bs_ref[...]).astype(BF16)

    return pl.pallas_call(
        body, grid=(s // t,), name=name,
        in_specs=[_row(t, 2 * SG_W, C_UV // (2 * SG_W)), _full((1, SG_W)), _full((1, SG_W)),
                  _full((4, LANES, LANES)), _full((LANES, LANES))],
        out_specs=_row(t, SG_W), out_shape=_sds((s, SG_W), BF16), compiler_params=_params())(p, lng, lnb, w_s, bs_t)


def merge_fwd(oa, ob, p, x, wa, wb, wo, g1, b1, name):
    s = x.shape[0]
    t = _tile(s, 512)

    def body(oa_ref, ob_ref, ga_ref, gb_ref, x_ref, wa_ref, wb_ref, wo_ref, g_ref, b_ref, x1_ref):
        ya = _mm(oa_ref[...], wa_ref[...])
        yb = _mm(ob_ref[...], wb_ref[...])
        h = _merge_glue(ga_ref[...], gb_ref[...], ya, yb)
        x1_ref[...] = _res_ln_glue(x_ref[...], _mm(h, wo_ref[...]), g_ref[...], b_ref[...])

    return pl.pallas_call(
        body, grid=(s // t,), name=name,
        in_specs=[_row(t, Z_W), _row(t, SG_W), _row(t, D, C_GA // D), _row(t, D, C_GB // D), _row(t, D),
                  _full((Z_W, D)), _full((SG_W, D)), _full((D, D)), _full((1, D)), _full((1, D))],
        out_specs=_row(t, D), out_shape=_sds((s, D)), compiler_params=_params())(oa, ob, p, p, x, wa, wb, wo, g1, b1)


def _load_ffn_weights(wup_hbm, wdn_hbm, wup_v, wdn_v, up_slots, dn_slots):
    for n, k in enumerate(up_slots):
        pltpu.sync_copy(wup_hbm.at[k], wup_v.at[n])
    for n, k in enumerate(dn_slots):
        pltpu.sync_copy(wdn_hbm.at[k], wdn_v.at[pl.ds(n * DN_SHARD, DN_SHARD)])


def ffn_fwd(x1, wup4, cf4, wdn4, g2, b2, name, gather=()):
    s = x1.shape[0]
    t = _tile(s, 512)
    nt = s // t

    def body(x1_ref, halo_ref, wup_hbm, cf_ref, wdn_hbm, g_ref, b_ref, pre_ref, x2_ref, wup_v, wdn_v):
        i = pl.program_id(0)

        @pl.when(i == 0)
        def _():
            _load_ffn_weights(wup_hbm, wdn_hbm, wup_v, wdn_v, range(4), range(4))

        x1v = x1_ref[...]
        halo = jnp.where(i == 0, 0.0, halo_ref[...])
        x1cat = jnp.concatenate([halo, x1v], axis=0).astype(BF16)
        f = None
        for h in range(2):
            ua = jnp.dot(x1cat, wup_v[h], preferred_element_type=F32)
            ub = jnp.dot(x1cat, wup_v[2 + h], preferred_element_type=F32)
            cwa = tuple(cf_ref[h, k:k + 1, :] for k in range(3))
            cwb = tuple(cf_ref[2 + h, k:k + 1, :] for k in range(3))
            act = _ffn_glue(ua, ub, cwa, cwb)
            fh = _mm(act, wdn_v[h * FFN_HALF:(h + 1) * FFN_HALF, :])
            f = fh if f is None else f + fh
        pre = ALPHA * x1v + f
        pre_ref[...] = pre
        x2_ref[...] = _ln(pre, g_ref[...], b_ref[...])

    n = len(gather)
    return pl.pallas_call(
        _carrying_gather(body, 7, 2, gather, nt) if gather else body, grid=(nt,), name=name,
        in_specs=[_row(t, D), _halo(t, D), ANY, _full((4, 3, FFN_HALF)), ANY, _full((1, D)), _full((1, D))] + [ANY] * n,
        out_specs=[_row(t, D), _row(t, D)] + [ANY] * n, out_shape=[_sds((s, D)), _sds((s, D))] + _gathered(gather),
        scratch_shapes=[pltpu.VMEM((4, D, FFN_HALF), BF16), pltpu.VMEM((FFN, D), BF16)]
        + (_gather_scratch(n) if gather else []),
        compiler_params=_params())(x1, x1, wup4, cf4, wdn4, g2, b2, *gather)


def loss_ln_bwd(pre, tgt, g, b, name):
    s = pre.shape[0]
    t = _tile(s, 512)

    def body(pre_ref, t_ref, g_ref, b_ref, dpre_ref, dg_ref, db_ref, loss_ref):
        first = pl.program_id(0) == 0
        y, vjp = jax.vjp(_ln, pre_ref[...], g_ref[...], b_ref[...])
        e = y - t_ref[...]
        dpre, dg, db = vjp(e * (1.0 / D))
        dpre_ref[...] = dpre
        _acc(dg_ref, dg, first)
        _acc(db_ref, db, first)
        part = jnp.sum(jnp.sum(e * e, axis=1, keepdims=True), axis=0, keepdims=True) * (0.5 / D)
        _acc(loss_ref, jnp.broadcast_to(part, loss_ref.shape), first)

    return pl.pallas_call(
        body, grid=(s // t,), name=name, in_specs=[_row(t, D), _row(t, D), _full((1, D)), _full((1, D))],
        out_specs=[_row(t, D), _full((1, D)), _full((1, D)), _full((8, LANES))],
        out_shape=[_sds((s, D)), _sds((1, D)), _sds((1, D)), _sds((8, LANES))], compiler_params=_params())(pre, tgt, g, b)


def _acc(ref, val, first):
    @pl.when(first)
    def _():
        ref[...] = val

    @pl.when(jnp.logical_not(first))
    def _():
        ref[...] += val


def _acc_tn(acc_ref, a, b, first, seg):
    n = b.shape[1]
    for lo in range(0, n, seg):
        hi = min(lo + seg, n)
        _acc(acc_ref.at[:, lo:hi], _mm_tn(a, b[:, lo:hi]), first)


def ln_bwd(pre, dy, g, b, name):
    s = pre.shape[0]
    t = _tile(s, 512)

    def body(pre_ref, dy_ref, g_ref, b_ref, dpre_ref, dg_ref, db_ref):
        _, vjp = jax.vjp(_ln, pre_ref[...], g_ref[...], b_ref[...])
        dpre, dg, db = vjp(dy_ref[...])
        dpre_ref[...] = dpre
        first = pl.program_id(0) == 0
        _acc(dg_ref, dg, first)
        _acc(db_ref, db, first)

    return pl.pallas_call(
        body, grid=(s // t,), name=name, in_specs=[_row(t, D), _row(t, D), _full((1, D)), _full((1, D))],
        out_specs=[_row(t, D), _full((1, D)), _full((1, D))],
        out_shape=[_sds((s, D)), _sds((1, D)), _sds((1, D))], compiler_params=_params())(pre, dy, g, b)


def ffn_bwd(x1, df, acc_in, acc_scale, wup4, cf4, wdn4, h, name):
    s = x1.shape[0]
    t = _tile(s)
    nt = s // t

    def body(x1_ref, halo_ref, df_ref, acc_ref, wup_hbm, cf_ref, wdn_hbm,
             dx1_ref, dwup_hbm, dcf_ref, dwdn_hbm, wup_v, wdn_v, dwup_v, dwdn_v, carry):
        i = pl.program_id(0)
        j = nt - 1 - i
        first = i == 0

        @pl.when(first)
        def _():
            _load_ffn_weights(wup_hbm, wdn_hbm, wup_v, wdn_v, (h, 2 + h), (2 * h, 2 * h + 1))
            carry[...] = jnp.zeros_like(carry)

        halo = jnp.where(j == 0, 0.0, halo_ref[...])
        x1cat = jnp.concatenate([halo, x1_ref[...]], axis=0).astype(BF16)
        ua = jnp.dot(x1cat, wup_v[0], preferred_element_type=F32)
        ub = jnp.dot(x1cat, wup_v[1], preferred_element_type=F32)
        cwa = tuple(cf_ref[h, k:k + 1, :] for k in range(3))
        cwb = tuple(cf_ref[2 + h, k:k + 1, :] for k in range(3))
        act, vjp = jax.vjp(_ffn_glue, ua, ub, cwa, cwb)
        dfb = df_ref[...].astype(BF16)
        dact = _mm_nt(dfb, wdn_v[...])
        _acc_tn(dwdn_v, act.astype(BF16), dfb, first, 512)
        dua, dub, dcwa, dcwb = vjp(dact)
        x1b = x1cat[HALO:]
        dups = []
        for n, du in enumerate((dua, dub)):
            dups.append(jnp.concatenate([du[HALO:t], du[t:] + carry[n]], axis=0).astype(BF16))
            carry[n] = du[:HALO]
            _acc(dwup_v.at[n], _mm_tn(x1b, dups[n]), first)
        for k in range(3):
            _acc(dcf_ref.at[0, k:k + 1, :], dcwa[k], first)
            _acc(dcf_ref.at[1, k:k + 1, :], dcwb[k], first)
        dx1_ref[...] = acc_scale * acc_ref[...] + _mm_nt(dups[0], wup_v[0]) + _mm_nt(dups[1], wup_v[1])

        @pl.when(i == nt - 1)
        def _():
            pltpu.sync_copy(dwup_v, dwup_hbm)
            pltpu.sync_copy(dwdn_v, dwdn_hbm)

    return pl.pallas_call(
        body, grid=(nt,), name=name,
        in_specs=[_row_rev(t, D, nt), _halo(t, D, nt), _row_rev(t, D, nt), _row_rev(t, D, nt),
                  ANY, _full((4, 3, FFN_HALF)), ANY],
        out_specs=[_row_rev(t, D, nt), ANY, _full((2, 3, FFN_HALF)), ANY],
        out_shape=[_sds((s, D)), _sds((2, D, FFN_HALF)), _sds((2, 3, FFN_HALF)), _sds((FFN_HALF, D))],
        scratch_shapes=[pltpu.VMEM((2, D, FFN_HALF), BF16), pltpu.VMEM((FFN_HALF, D), BF16),
                        pltpu.VMEM((2, D, FFN_HALF), F32), pltpu.VMEM((FFN_HALF, D), F32),
                        pltpu.VMEM((2, HALO, FFN_HALF), F32)],
        compiler_params=_params())(x1, x1, df, acc_in, wup4, cf4, wdn4)


def merge_bwd(oa, ob, p, x, dx1, wa, wb, wo, g1, b1, name):
    s = x.shape[0]
    t = _tile(s)

    def body(oa_ref, ob_ref, ga_ref, gb_ref, x_ref, dx1_ref, wa_ref, wb_ref, wo_ref, g_ref, b_ref,
             doa_ref, dob_ref, dga_ref, dgb_ref, dx_ref, dwa_ref, dwb_ref, dwo_ref, dg_ref, db_ref):
        first = pl.program_id(0) == 0
        oa = oa_ref[...]
        ob = ob_ref[...]
        ya = _mm(oa, wa_ref[...])
        yb = _mm(ob, wb_ref[...])
        h, vjp1 = jax.vjp(_merge_glue, ga_ref[...], gb_ref[...], ya, yb)
        hb = h.astype(BF16)
        r = _mm(hb, wo_ref[...])
        _, vjp2 = jax.vjp(_res_ln_glue, x_ref[...], r, g_ref[...], b_ref[...])
        dx, dr, dg, db = vjp2(dx1_ref[...])
        dx_ref[...] = dx
        _acc(dg_ref, dg, first)
        _acc(db_ref, db, first)
        drb = dr.astype(BF16)
        dh = _mm_nt(drb, wo_ref[...])
        _acc(dwo_ref, _mm_tn(hb, drb), first)
        dga, dgb, dya, dyb = vjp1(dh)
        dga_ref[...] = dga.astype(BF16)
        dgb_ref[...] = dgb.astype(BF16)
        dyab = dya.astype(BF16)
        dybb = dyb.astype(BF16)
        doa_ref[...] = _mm_nt(dyab, wa_ref[...]).astype(BF16)
        dob_ref[...] = _mm_nt(dybb, wb_ref[...]).astype(BF16)
        _acc(dwa_ref, _mm_tn(oa, dyab), first)
        _acc(dwb_ref, _mm_tn(ob, dybb), first)

    return pl.pallas_call(
        body, grid=(s // t,), name=name,
        in_specs=[_row(t, Z_W), _row(t, SG_W), _row(t, D, C_GA // D), _row(t, D, C_GB // D), _row(t, D), _row(t, D),
                  _full((Z_W, D)), _full((SG_W, D)), _full((D, D)), _full((1, D)), _full((1, D))],
        out_specs=[_row(t, Z_W), _row(t, SG_W), _row(t, D), _row(t, D), _row(t, D),
                   _full((Z_W, D)), _full((SG_W, D)), _full((D, D)), _full((1, D)), _full((1, D))],
        out_shape=[_sds((s, Z_W), BF16), _sds((s, SG_W), BF16), _sds((s, D), BF16), _sds((s, D), BF16), _sds((s, D)),
                   _sds((Z_W, D)), _sds((SG_W, D)), _sds((D, D)), _sds((1, D)), _sds((1, D))],
        compiler_params=_params())(oa, ob, p, p, x, dx1, wa, wb, wo, g1, b1)


def sg_bwd(p, dob, lng, lnb, w_s, bs_t, name):
    s = p.shape[0]
    t = _tile(s, 512)

    def body(uv_ref, dob_ref, lng_ref, lnb_ref, ws_ref, bs_ref, duv_ref, dlng_ref, dlnb_ref, dws_ref, dbs_ref):
        first = pl.program_id(0) == 0
        _, vjp = jax.vjp(_sg_glue, uv_ref[...], lng_ref[...], lnb_ref[...], ws_ref[...], bs_ref[...])
        duv, dlng, dlnb, dws, dbs = vjp(dob_ref[...].astype(F32))
        duv_ref[...] = duv.astype(BF16)
        _acc(dlng_ref, dlng, first)
        _acc(dlnb_ref, dlnb, first)
        _acc(dws_ref, dws, first)
        _acc(dbs_ref, dbs, first)

    return pl.pallas_call(
        body, grid=(s // t,), name=name,
        in_specs=[_row(t, 2 * SG_W, C_UV // (2 * SG_W)), _row(t, SG_W), _full((1, SG_W)), _full((1, SG_W)),
                  _full((4, LANES, LANES)), _full((LANES, LANES))],
        out_specs=[_row(t, 2 * SG_W), _full((1, SG_W)), _full((1, SG_W)), _full((4, LANES, LANES)), _full((LANES, LANES))],
        out_shape=[_sds((s, 2 * SG_W), BF16), _sds((1, SG_W)), _sds((1, SG_W)), _sds((4, LANES, LANES)), _sds((LANES, LANES))],
        compiler_params=_params())(p, dob, lng, lnb, w_s, bs_t)


def dn_bwd(p, sst, doa, cq, a_row, dtb_row, nw_row, name):
    s = p.shape[0]
    t = _tile(s)
    nt = s // t

    def body(qkv_ref, halo_ref, z_ref, ba_ref, sst_ref, doa_ref, cq_ref, a_ref, dtb_ref, nw_ref,
             dqkv_ref, dz_ref, dba_ref, dcq_ref, da_ref, ddtb_ref, dnw_ref, ds_scr, carry):
        i = pl.program_id(0)
        j = nt - 1 - i
        first = i == 0

        @pl.when(first)
        def _():
            ds_scr[...] = jnp.zeros_like(ds_scr)
            carry[...] = jnp.zeros_like(carry)

        halo = jnp.where(j == 0, 0.0, halo_ref[...])
        qkvcat = jnp.concatenate([halo, qkv_ref[...]], axis=0)
        cw = tuple(cq_ref[k:k + 1, :] for k in range(4))
        _, vjp = jax.vjp(_dn_glue, qkvcat, z_ref[...], ba_ref[...], sst_ref[0], cw, a_ref[...], dtb_ref[...], nw_ref[...])
        dqkvcat, dz, dba, ds_in, dcw, da, ddtb, dnw = vjp((doa_ref[...].astype(F32), ds_scr[...]))
        ds_scr[...] = ds_in
        dz_ref[...] = dz.astype(BF16)
        dba_ref[...] = dba.astype(BF16)
        dtile = dqkvcat[HALO:]
        dqkv_ref[...] = dtile.astype(BF16)
        dqkv_ref[t - HALO:t, :] = (dtile[t - HALO:] + carry[...]).astype(BF16)
        carry[...] = dqkvcat[:HALO]
        for k in range(4):
            _acc(dcq_ref.at[k:k + 1, :], dcw[k], first)
        _acc(da_ref, da, first)
        _acc(ddtb_ref, ddtb, first)
        _acc(dnw_ref, dnw, first)

    return pl.pallas_call(
        body, grid=(nt,), name=name,
        in_specs=[_row_rev(t, QKV_W, nt), _halo(t, QKV_W, nt), _row_rev(t, Z_W, nt, C_Z // Z_W),
                  _row_rev(t, LANES, nt, C_BA // LANES),
                  pl.BlockSpec((1, HEADS, DK, DK), lambda i: (nt - 1 - i, 0, 0, 0)), _row_rev(t, Z_W, nt),
                  _full((4, QKV_W)), _full((1, LANES)), _full((1, LANES)), _full((1, LANES))],
        out_specs=[_row_rev(t, QKV_W, nt), _row_rev(t, Z_W, nt), _row_rev(t, LANES, nt),
                   _full((4, QKV_W)), _full((1, LANES)), _full((1, LANES)), _full((1, LANES))],
        out_shape=[_sds((s, QKV_W), BF16), _sds((s, Z_W), BF16), _sds((s, LANES), BF16),
                   _sds((4, QKV_W)), _sds((1, LANES)), _sds((1, LANES)), _sds((1, LANES))],
        scratch_shapes=[pltpu.VMEM((HEADS, DK, DK), F32), pltpu.VMEM((HALO, QKV_W), F32)],
        compiler_params=_params())(p, p, p, p, sst, doa, cq, a_row, dtb_row, nw_row)


def proj_bwd(dps, dxd, w, name):
    s = dxd.shape[0]
    t = _tile(s, 512)
    n = len(dps)

    def body(*refs):
        dp_refs, dxd_ref, w_hbm, dx_ref, w_v = refs[:n], refs[n], refs[n + 1], refs[n + 2], refs[n + 3]

        @pl.when(pl.program_id(0) == 0)
        def _():
            pltpu.sync_copy(w_hbm, w_v)

        dp = jnp.concatenate([r[...] for r in dp_refs], axis=1)
        dx_ref[...] = dxd_ref[...] + _mm_nt(dp, w_v[...])

    return pl.pallas_call(
        body, grid=(s // t,), name=name,
        in_specs=[_row(t, dp.shape[1]) for dp in dps] + [_row(t, D), ANY],
        out_specs=_row(t, D), out_shape=_sds((s, D)),
        scratch_shapes=[pltpu.VMEM((D, IN_COLS_PAD), BF16)],
        compiler_params=_params())(*dps, dxd, w)


def wgrad(x, dp, col, into, name):
    s, n = dp.shape
    tk = _tile(s, 1024)
    tn = next(c for c in (1024, 768, 512, 256, 128) if n % c == 0 and col % c == 0)
    block = col // tn

    def body(x_ref, dp_ref, *rest):
        o_ref = rest[-1]
        _acc(o_ref, _mm_tn(x_ref[...], dp_ref[...]), pl.program_id(1) == 0)

    operands = (x, dp) if into is None else (x, dp, into)
    return pl.pallas_call(
        body, grid=(n // tn, s // tk), name=name,
        in_specs=[pl.BlockSpec((tk, D), lambda j, k: (k, 0)), pl.BlockSpec((tk, tn), lambda j, k: (k, j))]
        + ([] if into is None else [ANY]),
        out_specs=pl.BlockSpec((D, tn), lambda j, k: (0, block + j)), out_shape=_sds((D, IN_COLS_PAD)),
        input_output_aliases={} if into is None else {2: 0},
        compiler_params=_params(2))(*operands)


def _rows_block(rows, cols):
    cap = max(HALO, (2 * 1024 * 1024) // (cols * 4))
    for cand in range(min(rows, cap) // HALO * HALO, HALO - 1, -HALO):
        if rows % cand == 0:
            return cand
    return rows


def adam_call(w, g, m, v, name):
    rows, cols = w.shape
    tr = _rows_block(rows, cols)
    c1 = 1.0 - ADAM_B1 ** ADAM_STEP
    c2 = 1.0 - ADAM_B2 ** ADAM_STEP

    def body(w_ref, g_ref, m_ref, v_ref, go_ref, d_ref, nm_ref, nv_ref):
        gv = g_ref[...]
        go_ref[...] = gv
        nm = ADAM_B1 * m_ref[...] + (1.0 - ADAM_B1) * gv
        nv = ADAM_B2 * v_ref[...] + (1.0 - ADAM_B2) * (gv * gv)
        d_ref[...] = -ADAM_LR * ((nm / c1) / (jnp.sqrt(nv / c2) + ADAM_EPS) + ADAM_WD * w_ref[...])
        nm_ref[...] = nm
        nv_ref[...] = nv

    spec = pl.BlockSpec((tr, cols), lambda i: (i, 0))
    return pl.pallas_call(
        body, grid=(rows // tr,), name=name, in_specs=[spec] * 4, out_specs=[spec] * 4,
        out_shape=[_sds((rows, cols))] * 4, compiler_params=_params())(w, g, m, v)


def _place():
    return lax.axis_index("x"), lax.axis_index("y"), lax.axis_index("c")


def _other_chips(x, y):
    return [(1 - x, y), (x, 1 - y), (1 - x, 1 - y)]


def _remote(src, dst, send_sem, recv_sem, to):
    return pltpu.make_async_remote_copy(src_ref=src, dst_ref=dst, send_sem=send_sem, recv_sem=recv_sem,
                                        device_id=to, device_id_type=MESH)


class _Gather:
    def __init__(self, ins, outs, send_sems, recv_sems, local_sems):
        self.ins, self.outs, self.n = ins, outs, len(ins)
        self.send_sems, self.recv_sems, self.local_sems = send_sems, recv_sems, local_sems
        self.x, self.y, self.c = _place()
        self.me = 2 * self.x + self.y
        self.chips = _other_chips(self.x, self.y)

    def _copy(self, t, k, slot, part, to, src=None):
        dst = self.outs[t].at[slot, part]
        return _remote(dst if src is None else src, dst, self.send_sems.at[6 * t + k], self.recv_sems.at[6 * t + k], to)

    def _mine(self):
        return [pltpu.make_async_copy(self.ins[t].at[p], self.outs[t].at[self.me, p], self.local_sems.at[2 * t + p])
                for t in range(self.n) for p in range(2)]

    def _first(self):
        return [self._copy(t, k, self.me, self.c, (cx, cy, self.c), src=self.ins[t].at[self.c])
                for k, (cx, cy) in enumerate(self.chips) for t in range(self.n)]

    def start(self):
        for cp in self._mine() + self._first():
            cp.start()

    def finish(self):
        x, y, c = self.x, self.y, self.c
        passed = []
        for k, (cx, cy) in enumerate(self.chips):
            for t in range(self.n):
                self._copy(t, k, 2 * cx + cy, c, (x, y, c)).wait_recv()
                passed.append(self._copy(t, 3 + k, 2 * cx + cy, c, (x, y, 1 - c)))
                passed[-1].start()
        for k, (cx, cy) in enumerate(self.chips):
            for t in range(self.n):
                self._copy(t, 3 + k, 2 * cx + cy, 1 - c, (x, y, c)).wait_recv()
        for cp in self._first() + passed:
            cp.wait_send()
        for cp in self._mine():
            cp.wait()


def _gather_scratch(n):
    return [pltpu.SemaphoreType.DMA((6 * n,)), pltpu.SemaphoreType.DMA((6 * n,)), pltpu.SemaphoreType.DMA((2 * n,))]


def _gathered(shards):
    return [_sds((N_CHIPS,) + a.shape, a.dtype) for a in shards]


def gather_weights(shards, name):
    n = len(shards)

    def body(*refs):
        gather = _Gather(refs[:n], refs[n:2 * n], *refs[2 * n:])
        gather.start()
        gather.finish()

    return pl.pallas_call(
        body, name=name, in_specs=[ANY] * n, out_specs=[ANY] * n, out_shape=_gathered(shards),
        scratch_shapes=_gather_scratch(n))(*shards)


def _carrying_gather(body, n_in, n_out, shards, steps):
    n = len(shards)

    def both(*refs):
        ins, parts = refs[:n_in], refs[n_in:n_in + n]
        outs = refs[n_in + n:n_in + n + n_out]
        landed = refs[n_in + n + n_out:n_in + 2 * n + n_out]
        scratch, sems = refs[n_in + 2 * n + n_out:-3], refs[-3:]
        gather = _Gather(parts, landed, *sems)
        pl.when(pl.program_id(0) == 0)(gather.start)
        body(*ins, *outs, *scratch)
        pl.when(pl.program_id(0) == steps - 1)(gather.finish)

    return both


def pair_exchange(layer0, layer1, name):
    n = len(layer0)

    def body(*refs):
        srcs = (refs[:n], refs[n:2 * n])
        outs = refs[2 * n:3 * n]
        send_sems, recv_sems = refs[3 * n:]
        x, y, c = _place()
        for core in range(2):
            @pl.when(c == core)
            def _(core=core):
                cps = [_remote(srcs[1 - core][t], outs[t], send_sems.at[t], recv_sems.at[t], (x, y, 1 - c))
                       for t in range(n)]
                for cp in cps:
                    cp.start()
                for cp in cps:
                    cp.wait()

    return pl.pallas_call(
        body, name=name, in_specs=[ANY] * (2 * n), out_specs=[ANY] * n,
        out_shape=[_sds(a.shape) for a in layer0],
        scratch_shapes=[pltpu.SemaphoreType.DMA((n,)), pltpu.SemaphoreType.DMA((n,))],
    )(*layer0, *layer1)


def pair_add(a0, a1, theirs, c_vec, name):
    rows, cols = a0.shape
    tr = _rows_block(rows, cols)

    def body(c_ref, a0_ref, a1_ref, b_ref, o_ref):
        o_ref[...] = (jnp.where(c_ref[0] == 0, a0_ref[...], a1_ref[...]) + b_ref[...]).astype(BF16)

    def of_core(core):
        return pl.BlockSpec((tr, cols), lambda i, c: (jnp.where(c[0] == core, i, 0), 0))

    spec = pl.BlockSpec((tr, cols), lambda i, c: (i, 0))
    grid_spec = pltpu.PrefetchScalarGridSpec(
        num_scalar_prefetch=1, grid=(rows // tr,), in_specs=[of_core(0), of_core(1), spec], out_specs=spec)
    return pl.pallas_call(body, grid_spec=grid_spec, name=name, out_shape=_sds((rows, cols), BF16),
                          compiler_params=_params())(c_vec, a0, a1, theirs)


def scatter_chips(srcs, pieces, shard_shapes, name):
    n_src, n_t = len(srcs), len(pieces)

    def body(*refs):
        src_refs = refs[:n_src]
        outs = refs[n_src:n_src + n_t]
        send_sems, recv_sems, local_sems = refs[n_src + n_t:]
        x, y, c = _place()
        me = 2 * x + y

        def piece(t, k):
            idx, lead, rows, cols = pieces[t][k]
            ref = src_refs[idx]
            if lead is not None:
                ref = ref.at[lead]
            if rows is not None:
                ref = ref.at[pl.ds(rows[0], rows[1]), :]
            if cols is not None:
                ref = ref.at[:, pl.ds(cols[0], cols[1])]
            return ref

        def local(t, k):
            return pltpu.make_async_copy(piece(t, k), outs[t].at[k], local_sems.at[t])

        for k in range(N_CHIPS):
            @pl.when(me == k)
            def _(k=k):
                for t in range(n_t):
                    local(t, k).start()

            @pl.when(me != k)
            def _(k=k):
                for t in range(n_t):
                    _remote(piece(t, k), outs[t].at[me], send_sems.at[N_CHIPS * t + k],
                            recv_sems.at[N_CHIPS * t + me], (k // 2, k % 2, c)).start()

        for k in range(N_CHIPS):
            @pl.when(me != k)
            def _(k=k):
                for t in range(n_t):
                    cp = _remote(piece(t, k), outs[t].at[k], send_sems.at[N_CHIPS * t + k],
                                 recv_sems.at[N_CHIPS * t + k], (x, y, c))
                    cp.wait_recv()
                    cp.wait_send()

            @pl.when(me == k)
            def _(k=k):
                for t in range(n_t):
                    local(t, k).wait()

    return pl.pallas_call(
        body, name=name, in_specs=[ANY] * n_src, out_specs=[ANY] * n_t,
        out_shape=[_sds((N_CHIPS,) + tuple(shp), srcs[0].dtype) for shp in shard_shapes],
        scratch_shapes=[pltpu.SemaphoreType.DMA((N_CHIPS * n_t,)), pltpu.SemaphoreType.DMA((N_CHIPS * n_t,)),
                        pltpu.SemaphoreType.DMA((n_t,))],
    )(*srcs)


def chips_add(recv, c_vec, name):
    n, a, b = recv.shape
    tr = _rows_block(a, b)

    def body(c_ref, r0, r1, r2, r3, o_ref):
        o_ref[...] = ((r0[...].astype(F32) + r1[...].astype(F32)) + r2[...].astype(F32)) + r3[...].astype(F32)

    grid_spec = pltpu.PrefetchScalarGridSpec(
        num_scalar_prefetch=1, grid=(a // tr,),
        in_specs=[pl.BlockSpec((None, tr, b), lambda i, c, k=k: (k, i, 0)) for k in range(n)],
        out_specs=pl.BlockSpec((None, tr, b), lambda i, c: (c[0], i, 0)))
    return pl.pallas_call(body, grid_spec=grid_spec, name=name, out_shape=_sds((2, a, b)),
                          compiler_params=_params())(c_vec, *([recv] * n))


def pair_join(bufs, name):
    n = len(bufs)

    def body(*refs):
        ins, outs = refs[:n], refs[n:2 * n]
        send_sems, recv_sems = refs[2 * n:]
        x, y, c = _place()
        cps = [_remote(ins[t].at[c], outs[t].at[c], send_sems.at[t], recv_sems.at[t], (x, y, 1 - c)) for t in range(n)]
        for cp in cps:
            cp.start()
        for t in range(n):
            cps[t].wait_send()
            _remote(ins[t].at[c], outs[t].at[1 - c], send_sems.at[t], recv_sems.at[t], (x, y, c)).wait_recv()

    return pl.pallas_call(
        body, name=name, in_specs=[ANY] * n, out_specs=[ANY] * n, out_shape=[_sds(a.shape) for a in bufs],
        input_output_aliases={t: t for t in range(n)},
        scratch_shapes=[pltpu.SemaphoreType.DMA((n,)), pltpu.SemaphoreType.DMA((n,))],
    )(*bufs)


def allsum_small(v, name):
    rows, lanes = v.shape
    n_dev = 8

    def body(v_ref, out_ref, buf, send_sems, recv_sems):
        x, y, c = _place()
        me, sibling = (x, y, c), (x, y, 1 - c)
        chips = _other_chips(x, y)

        def slot(px, py, pc):
            return buf.at[4 * px + 2 * py + pc]

        def copy(k, block, to, src=None):
            return pltpu.make_async_remote_copy(
                src_ref=slot(*block) if src is None else src, dst_ref=slot(*block),
                send_sem=send_sems.at[k], recv_sem=recv_sems.at[k], device_id=to, device_id_type=MESH)

        slot(*me)[...] = v_ref[...]
        first = [copy(0, me, sibling, src=v_ref)]
        first += [copy(1 + k, me, (*chip, c), src=v_ref) for k, chip in enumerate(chips)]
        for cp in first:
            cp.start()
        passed = [copy(4 + k, (*chip, c), sibling) for k, chip in enumerate(chips)]
        for k, chip in enumerate(chips):
            copy(1 + k, (*chip, c), me).wait_recv()
            passed[k].start()
        copy(0, sibling, me).wait_recv()
        for k, chip in enumerate(chips):
            copy(4 + k, (*chip, 1 - c), me).wait_recv()
        for cp in first + passed:
            cp.wait_send()
        acc = buf[0]
        for d in range(1, n_dev):
            acc = acc + buf[d]
        out_ref[...] = acc

    vm = pl.BlockSpec(memory_space=pltpu.VMEM)
    return pl.pallas_call(
        body, name=name, in_specs=[vm], out_specs=vm, out_shape=_sds((rows, lanes)),
        scratch_shapes=[pltpu.VMEM((n_dev, rows, lanes), F32), pltpu.SemaphoreType.DMA((7,)), pltpu.SemaphoreType.DMA((7,))],
        compiler_params=pltpu.CompilerParams(vmem_limit_bytes=VMEM_LIMIT),
    )(v)


BIG = ("w_in", "w_branch_a", "w_branch_b", "w_out", "w_up", "w_down")
CONV = ("conv_qkv", "conv_ffn")
REPL =("a_log", "dt_bias", "dn_norm_w", "sg_ln_g", "sg_ln_b", "w_spatial", "b_spatial", "ln1_g", "ln1_b", "ln2_g", "ln2_b")


def _pad_rows(flat, mult):
    n = flat.shape[0]
    unit = mult * LANES
    total = -(-n // unit) * unit
    return jnp.pad(flat, (0, total - n)).reshape(total // LANES, LANES)


def _pack(arrs, mult):
    return _pad_rows(jnp.concatenate([a.reshape(-1) for a in arrs]), mult)


def _unpack(flat, shapes):
    out, off = [], 0
    for shp in shapes:
        n = math.prod(shp)
        out.append(flat[off:off + n].reshape(shp))
        off += n
    return out


def kernel(x, w_in, conv_qkv, a_log, dt_bias, dn_norm_w, w_branch_a, sg_ln_g, sg_ln_b, w_spatial, b_spatial, w_branch_b, w_out, ln1_g, ln1_b, w_up, conv_ffn, w_down, ln2_g, ln2_b, loss_target, m_w_in, m_conv_qkv, m_a_log, m_dt_bias, m_dn_norm_w, m_w_branch_a, m_sg_ln_g, m_sg_ln_b, m_w_spatial, m_b_spatial, m_w_branch_b, m_w_out, m_ln1_g, m_ln1_b, m_w_up, m_conv_ffn, m_w_down, m_ln2_g, m_ln2_b, v_w_in, v_conv_qkv, v_a_log, v_dt_bias, v_dn_norm_w, v_w_branch_a, v_sg_ln_g, v_sg_ln_b, v_w_spatial, v_b_spatial, v_w_branch_b, v_w_out, v_ln1_g, v_ln1_b, v_w_up, v_conv_ffn, v_w_down, v_ln2_g, v_ln2_b):
    names = ("w_in", "conv_qkv", "a_log", "dt_bias", "dn_norm_w", "w_branch_a", "sg_ln_g", "sg_ln_b", "w_spatial",
             "b_spatial", "w_branch_b", "w_out", "ln1_g", "ln1_b", "w_up", "conv_ffn", "w_down", "ln2_g", "ln2_b")
    w = dict(zip(names, (w_in, conv_qkv, a_log, dt_bias, dn_norm_w, w_branch_a, sg_ln_g, sg_ln_b, w_spatial,
                         b_spatial, w_branch_b, w_out, ln1_g, ln1_b, w_up, conv_ffn, w_down, ln2_g, ln2_b)))
    m = dict(zip(names, (m_w_in, m_conv_qkv, m_a_log, m_dt_bias, m_dn_norm_w, m_w_branch_a, m_sg_ln_g, m_sg_ln_b,
                         m_w_spatial, m_b_spatial, m_w_branch_b, m_w_out, m_ln1_g, m_ln1_b, m_w_up, m_conv_ffn,
                         m_w_down, m_ln2_g, m_ln2_b)))
    v = dict(zip(names, (v_w_in, v_conv_qkv, v_a_log, v_dt_bias, v_dn_norm_w, v_w_branch_a, v_sg_ln_g, v_sg_ln_b,
                         v_w_spatial, v_b_spatial, v_w_branch_b, v_w_out, v_ln1_g, v_ln1_b, v_w_up, v_conv_ffn,
                         v_w_down, v_ln2_g, v_ln2_b)))
    chip = 2 * lax.axis_index("x") + lax.axis_index("y")
    s = x.shape[1]
    xs = x.reshape(s, D)
    tgt = loss_target.reshape(s, D)

    big_names, conv_names = list(BIG), list(CONV)

    def in_two(name, l):
        rows, cols = w[name].shape[1:]
        return w[name][l].astype(BF16).reshape(2, rows // 2, cols)

    def whole(name, landed):
        rows, cols = w[name].shape[1:]
        return landed.reshape(N_CHIPS, rows, cols)

    first = gather_weights([in_two("w_in", 0)] + [w[n] for n in conv_names], "gather_first")
    got = [{"w_in": whole("w_in", first[0])}, {}]
    conv_taps = dict(zip(conv_names, first[1:]))
    narrow, wide = ["w_branch_a", "w_branch_b", "w_out"], ["w_up", "w_down"]
    carried = {"proj_fwd0": (0, narrow), "dn_fwd0": (0, wide), "ffn_fwd0": (1, ["w_in"] + narrow), "dn_fwd1": (1, wide)}

    def carry(call):
        l, which = carried.get(call, (0, []))
        return [in_two(n, l) for n in which]

    def land(call, landed):
        l, which = carried.get(call, (0, []))
        got[l].update({n: whole(n, a) for n, a in zip(which, landed)})

    def lane_row(vec, off):
        return jnp.zeros((1, LANES), F32).at[0, off:off + vec.shape[0]].set(vec)

    def side_by_side(blocks):
        return jnp.concatenate([blocks[k] for k in range(N_CHIPS)], axis=1)

    def small_params(l):
        return dict(
            cq=side_by_side(conv_taps["conv_qkv"][:, l]),
            a_row=lane_row(w["a_log"][l], HEADS), dtb_row=lane_row(w["dt_bias"][l], HEADS),
            nw_row=w["dn_norm_w"][l].reshape(1, DK),
            lng=w["sg_ln_g"][l].reshape(1, SG_W), lnb=w["sg_ln_b"][l].reshape(1, SG_W),
            w_s=w["w_spatial"][l], bs_t=jnp.zeros((LANES, LANES), F32).at[:, :4].set(w["b_spatial"][l].T),
            g1=w["ln1_g"][l].reshape(1, D), b1=w["ln1_b"][l].reshape(1, D),
            cf=conv_taps["conv_ffn"][:, l],
            g2=w["ln2_g"][l].reshape(1, D), b2=w["ln2_b"][l].reshape(1, D))

    layers, saved = [], []
    h_in = xs
    for l in range(DEPTH):
        p = small_params(l)
        wi = side_by_side(got[l]["w_in"])
        p["w_in"] = jnp.concatenate([wi[:, :2048], wi[:, 2056:3080], wi[:, 3080:5128], wi[:, 2048:2056],
                                     jnp.zeros((D, IN_COLS_PAD - 5128), BF16)], axis=1)
        proj, *landed = proj_fwd(h_in, p["w_in"], f"proj_fwd{l}", gather=carry(f"proj_fwd{l}"))
        land(f"proj_fwd{l}", landed)
        oa, sst, *landed = dn_fwd(proj, p["cq"], p["a_row"], p["dtb_row"], p["nw_row"], f"dn_fwd{l}",
                                  gather=carry(f"dn_fwd{l}"))
        land(f"dn_fwd{l}", landed)
        ob = sg_fwd(proj, p["lng"], p["lnb"], p["w_s"], p["bs_t"], f"sg_fwd{l}")
        p.update(wa=side_by_side(got[l]["w_branch_a"]), wb=side_by_side(got[l]["w_branch_b"]),
                 wo=got[l]["w_out"].reshape(D, D))
        x1 = merge_fwd(oa, ob, proj, h_in, p["wa"], p["wb"], p["wo"], p["g1"], p["b1"], f"merge_fwd{l}")
        pre2, x2, *landed = ffn_fwd(x1, got[l]["w_up"], p["cf"], got[l]["w_down"], p["g2"], p["b2"], f"ffn_fwd{l}",
                                    gather=carry(f"ffn_fwd{l}"))
        land(f"ffn_fwd{l}", landed)
        layers.append(p)
        saved.append(dict(x=h_in, proj=proj, oa=oa, ob=ob, sst=sst, x1=x1, pre2=pre2))
        h_in = x2


    small_names = conv_names + list(REPL)
    grads = {n: [None] * DEPTH for n in small_names}
    big_grads = [None] * DEPTH
    for l in reversed(range(DEPTH)):
        p, a = layers[l], saved[l]
        if l == DEPTH - 1:
            dpre2, dg2, db2, loss_part = loss_ln_bwd(a["pre2"], tgt, p["g2"], p["b2"], "loss_ln2_bwd")
            loss = lax.psum(loss_part[0, 0], ("x", "y", "c"))
        else:
            dpre2, dg2, db2 = ln_bwd(a["pre2"], dy, p["g2"], p["b2"], f"ln2_bwd{l}")
        dx1, dwup0, dcf0, dwdn0 = ffn_bwd(a["x1"], dpre2, dpre2, ALPHA, got[l]["w_up"], p["cf"], got[l]["w_down"], 0, f"ffn_bwd{l}a")
        dx1, dwup1, dcf1, dwdn1 = ffn_bwd(a["x1"], dpre2, dx1, 1.0, got[l]["w_up"], p["cf"], got[l]["w_down"], 1, f"ffn_bwd{l}b")
        doa, dob, dga, dgb, dxd, dwa, dwb, dwo, dg1, db1 = merge_bwd(
            a["oa"], a["ob"], a["proj"], a["x"], dx1, p["wa"], p["wb"], p["wo"], p["g1"], p["b1"], f"merge_bwd{l}")
        duv, dlng, dlnb, dws, dbs = sg_bwd(a["proj"], dob, p["lng"], p["lnb"], p["w_s"], p["bs_t"], f"sg_bwd{l}")
        dqkv, dz, dba, dcq, da, ddtb, dnw = dn_bwd(a["proj"], a["sst"], doa, p["cq"], p["a_row"], p["dtb_row"],
                                                   p["nw_row"], f"dn_bwd{l}")
        dy = proj_bwd([dqkv, dz, duv, dga, dgb, dba], dxd, p["w_in"], f"proj_bwd{l}")
        dwi = None
        for tag, dp, col in (("qkv", dqkv, 0), ("z", dz, C_Z), ("uv", duv, C_UV), ("ga", dga, C_GA), ("gb", dgb, C_GB),
                             ("ba", dba, C_BA)):
            dwi = wgrad(a["x"], dp, col, dwi, f"wgrad_in{l}_{tag}")

        big_grads[l] = [dwi, dwa, dwb, dwo, dwup0.reshape(2 * D, FFN_HALF), dwup1.reshape(2 * D, FFN_HALF), dwdn0, dwdn1]
        grads["conv_qkv"][l] = dcq
        grads["conv_ffn"][l] = jnp.concatenate([dcf0[0], dcf1[0], dcf0[1], dcf1[1]], axis=1)
        grads["a_log"][l] = da[0, HEADS:2 * HEADS]
        grads["dt_bias"][l] = ddtb[0, HEADS:2 * HEADS]
        grads["dn_norm_w"][l] = dnw[0]
        grads["sg_ln_g"][l] = dlng[0]
        grads["sg_ln_b"][l] = dlnb[0]
        grads["w_spatial"][l] = dws
        grads["b_spatial"][l] = dbs[:, :4].T
        grads["ln1_g"][l] = dg1[0]
        grads["ln1_b"][l] = db1[0]
        grads["ln2_g"][l] = dg2[0]
        grads["ln2_b"][l] = db2[0]
    grad_x = dy.reshape(x.shape)
    g_full = {n: jnp.stack(grads[n]) for n in small_names}

    c_vec = jnp.stack([lax.axis_index("c")]).astype(jnp.int32)
    theirs = pair_exchange(big_grads[0], big_grads[1], "reduce_pair")
    tags = ("w_in", "w_a", "w_b", "w_out", "w_up0", "w_up1", "w_dn0", "w_dn1")
    pin, pa, pb, po, pup0, pup1, pdn0, pdn1 = [
        pair_add(a0, a1, th, c_vec, f"reduce_pair_add_{tag}")
        for tag, a0, a1, th in zip(tags, big_grads[0], big_grads[1], theirs)]
    natural = jnp.concatenate([pin[:, :2048], pin[:, C_BA:C_BA + 8], pin[:, 2048:C_BA]], axis=1)
    srcs = [jnp.stack(jnp.split(natural, N_CHIPS, axis=1)), pa, pb, po,
            pup0.reshape(2, D, FFN_HALF), pup1.reshape(2, D, FFN_HALF), pdn0, pdn1]
    ab_cols, out_rows = w["w_branch_a"].shape[2], w["w_out"].shape[1]
    pieces = [
        [(0, k, None, None) for k in range(N_CHIPS)],
        [(1, None, None, (k * ab_cols, ab_cols)) for k in range(N_CHIPS)],
        [(2, None, None, (k * ab_cols, ab_cols)) for k in range(N_CHIPS)],
        [(3, None, (k * out_rows, out_rows), None) for k in range(N_CHIPS)],
        [(4 + k % 2, k // 2, None, None) for k in range(N_CHIPS)],
        [(6 + k // 2, None, ((k % 2) * DN_SHARD, DN_SHARD), None) for k in range(N_CHIPS)],
    ]
    recv = scatter_chips(srcs, pieces, [w[n].shape[1:] for n in big_names], "reduce_chips")
    sums = [chips_add(r, c_vec, f"reduce_chips_add_{n}") for n, r in zip(big_names, recv)]
    g_shard = dict(zip(big_names, pair_join(sums, "reduce_join")))

    small = allsum_small(_pack([g_full[n] for n in small_names], 8), "reduce_small").reshape(-1)
    small_full = dict(zip(small_names, _unpack(small, [g_full[n].shape for n in small_names])))
    for n in conv_names:
        width = w[n].shape[2]
        g_shard[n] = lax.dynamic_slice_in_dim(small_full[n], chip * width, width, axis=2)
    for n in REPL:
        g_shard[n] = small_full[n]

    delta, new_m, new_v = {}, {}, {}
    for n in big_names:
        shp = w[n].shape
        two_d = (shp[0] * shp[1], shp[2])
        g_, d_, m_, v_ = adam_call(w[n].reshape(two_d), g_shard[n].reshape(two_d), m[n].reshape(two_d), v[n].reshape(two_d), f"adam_{n}")
        g_shard[n], delta[n], new_m[n], new_v[n] = g_.reshape(shp), d_.reshape(shp), m_.reshape(shp), v_.reshape(shp)
    shapes = [w[n].shape for n in small_names]
    packs = [_pack([src[n] for n in small_names], 8) for src in (w, g_shard, m, v)]
    outs = adam_call(*packs, "adam_small")
    for dst, o in zip((delta, new_m, new_v), outs[1:]):
        dst.update(zip(small_names, _unpack(o.reshape(-1), shapes)))

    return (loss, grad_x, *[g_shard[n] for n in names], *[delta[n] for n in names],
            *[new_m[n] for n in names], *[new_v[n] for n in names])
```

```python
import functools
import math
from typing import Callable, NamedTuple

import jax
import jax.numpy as jnp
from jax import lax
from jax.experimental import pallas as pl
from jax.experimental.pallas import tpu as pltpu

F32 = jnp.float32
BF16 = jnp.bfloat16
HI = lax.Precision.HIGHEST
MID = lax.Precision.HIGH
MESH = pl.DeviceIdType.MESH

D = 1024
DEPTH = 2
HEADS = 4
DK = 128
CHUNK = 64
QKV_W = 1536
Z_W = 512
SG_W = 512
FFN = 2816
FFN_HALF = FFN // 2
N_CHIPS = 4
DN_SHARD = FFN // N_CHIPS
LN_EPS = 1e-5
RMS_EPS = 1e-6
L2_EPS = 1e-6
ALPHA = (2 * DEPTH) ** 0.25
ADAM_LR, ADAM_B1, ADAM_B2, ADAM_EPS, ADAM_WD, ADAM_STEP = 0.001, 0.9, 0.999, 1e-08, 0.01, 10

HALO = 16
LANES = 128
IN_COLS_PAD = 5248
C_Z, C_UV, C_GA, C_GB, C_BA = 1536, 2048, 3072, 4096, 5120
VMEM_LIMIT = 56 * 1024 * 1024


def _params(n_grid=1):
    return pltpu.CompilerParams(dimension_semantics=("arbitrary",) * n_grid, vmem_limit_bytes=VMEM_LIMIT)


def _mm(a, b):
    return jnp.dot(a.astype(BF16), b.astype(BF16), preferred_element_type=F32)


def _mm_nt(a, b):
    return lax.dot_general(a.astype(BF16), b.astype(BF16), (((1,), (1,)), ((), ())), preferred_element_type=F32)


def _mm_tn(a, b):
    return lax.dot_general(a.astype(BF16), b.astype(BF16), (((0,), (0,)), ((), ())), preferred_element_type=F32)


def _bdot(a, b, prec=MID):
    return lax.dot_general(a, b, (((2,), (1,)), ((0,), (0,))), precision=prec, preferred_element_type=F32)


def _bdot_nt(a, b, prec=MID):
    return lax.dot_general(a, b, (((2,), (2,)), ((0,), (0,))), precision=prec, preferred_element_type=F32)


def _bf16_dot(a, b, contract):
    return lax.dot_general(a.astype(BF16), b.astype(BF16), (contract, ((0,), (0,))), preferred_element_type=F32)


@jax.custom_vjp
def _fdot(a, b):
    return _bf16_dot(a, b, ((2,), (1,)))


def _fdot_fwd(a, b):
    return _fdot(a, b), (a, b)


def _fdot_bwd(res, ct):
    a, b = res
    return _bf16_dot(ct, b, ((2,), (2,))), _bf16_dot(a, ct, ((1,), (1,)))


_fdot.defvjp(_fdot_fwd, _fdot_bwd)


@jax.custom_vjp
def _fdot_nt(a, b):
    return _bf16_dot(a, b, ((2,), (2,)))


def _fdot_nt_fwd(a, b):
    return _fdot_nt(a, b), (a, b)


def _fdot_nt_bwd(res, ct):
    a, b = res
    return _bf16_dot(ct, b, ((2,), (1,))), _bf16_dot(ct, a, ((1,), (1,)))


_fdot_nt.defvjp(_fdot_nt_fwd, _fdot_nt_bwd)


@jax.custom_vjp
def _fdot_tn(a, b):
    return _bf16_dot(a, b, ((1,), (1,)))


def _fdot_tn_fwd(a, b):
    return _fdot_tn(a, b), (a, b)


def _fdot_tn_bwd(res, ct):
    a, b = res
    return _bf16_dot(b, ct, ((2,), (2,))), _bf16_dot(a, ct, ((2,), (1,)))


_fdot_tn.defvjp(_fdot_tn_fwd, _fdot_tn_bwd)


def _ln(x, g, b):
    mu = jnp.mean(x, axis=-1, keepdims=True)
    xc = x - mu
    var = jnp.mean(xc * xc, axis=-1, keepdims=True)
    return xc * lax.rsqrt(var + LN_EPS) * g + b


def _shift_rows(x, s):
    s = s % x.shape[0]
    return x if s == 0 else pltpu.roll(x, s, 0)


@jax.custom_vjp
def _conv(xcat, w):
    k_taps = len(w)
    y = None
    for k in range(k_taps):
        t = _shift_rows(xcat, k_taps - 1 - k)[HALO:] * w[k]
        y = t if y is None else y + t
    return y


def _conv_fwd(xcat, w):
    return _conv(xcat, w), (xcat, w)


def _conv_bwd(res, dy):
    xcat, w = res
    k_taps = len(w)
    dyp = jnp.concatenate([jnp.zeros((HALO, dy.shape[1]), dy.dtype), dy], axis=0)
    dx = None
    dws = []
    for k in range(k_taps):
        s = k_taps - 1 - k
        t = _shift_rows(dyp, -s) * w[k]
        dx = t if dx is None else dx + t
        dws.append(jnp.sum(_shift_rows(xcat, s)[HALO:] * dy, axis=0, keepdims=True))
    return dx, tuple(dws)


_conv.defvjp(_conv_fwd, _conv_bwd)


@jax.custom_vjp
def _tri_inv(l):
    n = l.shape[-1]
    r = lax.broadcasted_iota(jnp.int32, (n, n), 0)
    c = lax.broadcasted_iota(jnp.int32, (n, n), 1)
    eye = (r == c).astype(F32)
    p = eye - l
    lp = l
    steps = int(math.log2(n)) - 1
    for i in range(steps):
        dot = _bdot if i < 2 else functools.partial(_bf16_dot, contract=((2,), (1,)))
        lp = dot(lp, lp)
        p = p + dot(p, lp)
    return p


def _tri_inv_fwd(l):
    t = _tri_inv(l)
    return t, t


def _tri_inv_bwd(t, dt):
    tt = jnp.swapaxes(t, 1, 2)
    return (-_bdot(tt, _bdot(dt, tt)),)


_tri_inv.defvjp(_tri_inv_fwd, _tri_inv_bwd)


def _dn_glue(qkvcat, z, ba, s_in, cw, a_row, dtb_row, nw_row):
    t_rows = z.shape[0]
    nc = t_rows // CHUNK
    nb = nc * HEADS

    qkv = jax.nn.silu(_conv(qkvcat, cw))

    def chunks(t, off):
        return jnp.stack([t[n * CHUNK:(n + 1) * CHUNK, off + h * DK: off + (h + 1) * DK]
                          for n in range(nc) for h in range(HEADS)])

    q = chunks(qkv, 0)
    k = chunks(qkv, 512)
    v = chunks(qkv, 1024)
    q = q * lax.rsqrt(jnp.sum(q * q, axis=-1, keepdims=True) + L2_EPS) * (DK ** -0.5)
    k = k * lax.rsqrt(jnp.sum(k * k, axis=-1, keepdims=True) + L2_EPS)

    lane = lax.broadcasted_iota(jnp.int32, (LANES, HEADS * DK), 0)
    head_of_col = lax.broadcasted_iota(jnp.int32, (LANES, HEADS * DK), 1) // DK
    e_beta = (head_of_col == lane).astype(F32)
    e_g = (head_of_col + HEADS == lane).astype(F32)
    beta_l = jax.nn.sigmoid(ba)
    g_l = -jnp.exp(a_row) * jax.nn.softplus(ba + dtb_row)
    beta = chunks(jnp.dot(beta_l, e_beta, precision=MID, preferred_element_type=F32), 0)
    g = chunks(jnp.dot(g_l, e_g, precision=MID, preferred_element_type=F32), 0)

    r = lax.broadcasted_iota(jnp.int32, (CHUNK, CHUNK), 0)
    c = lax.broadcasted_iota(jnp.int32, (CHUNK, CHUNK), 1)
    causal = r >= c
    strict = r > c
    tril_b = jnp.broadcast_to(causal.astype(F32), (nb, CHUNK, CHUNK))
    gi_b = _bdot(tril_b, g, HI)
    gi = gi_b[:, :, :CHUNK]
    gj = jnp.swapaxes(gi, 1, 2)
    decay = jnp.where(causal, jnp.exp(jnp.where(causal, gi - gj, 0.0)), 0.0)
    kb = k * beta
    l_mat = jnp.where(strict, _fdot_nt(kb, k) * decay, 0.0)
    t_mat = _tri_inv(l_mat)
    e_gi = jnp.exp(gi_b)
    w_mat = _fdot(t_mat, kb * e_gi)
    u_mat = _fdot(t_mat, v * beta)
    a_qk = _fdot_nt(q, k) * decay
    q_g = q * e_gi
    gl_b = jnp.broadcast_to(jnp.sum(g, axis=1, keepdims=True), g.shape)
    k_d = k * jnp.exp(gl_b - gi_b)
    e_gl = jnp.exp(gl_b)
    g_last = jnp.concatenate([e_gl, e_gl], axis=1)

    state = s_in
    rows = []
    for n in range(nc):
        sl = slice(n * HEADS, (n + 1) * HEADS)
        u_new = u_mat[sl] - _fdot(w_mat[sl], state)
        o_n = _fdot(q_g[sl], state) + _fdot(a_qk[sl], u_new)
        state = state * g_last[sl] + _fdot_tn(k_d[sl], u_new)
        o_n = o_n * lax.rsqrt(jnp.mean(o_n * o_n, axis=-1, keepdims=True) + RMS_EPS) * nw_row
        z_n = jnp.stack([z[n * CHUNK:(n + 1) * CHUNK, h * DK:(h + 1) * DK] for h in range(HEADS)])
        o_n = o_n * jax.nn.silu(z_n)
        rows.append(jnp.concatenate([o_n[h] for h in range(HEADS)], axis=-1))
    return jnp.concatenate(rows, axis=0), state


def _sg_glue(uv, lng, lnb, w_s, bs_t):
    t_rows = uv.shape[0]
    y = jax.nn.gelu(uv)
    u = y[:, :SG_W]
    v = _ln(y[:, SG_W:], lng, lnb)
    r = lax.broadcasted_iota(jnp.int32, (LANES, LANES), 0)
    c = lax.broadcasted_iota(jnp.int32, (LANES, LANES), 1)
    wm = jnp.where(r >= c, w_s, 0.0)
    lane = lax.broadcasted_iota(jnp.int32, (LANES, SG_W), 0)
    group_of_col = lax.broadcasted_iota(jnp.int32, (LANES, SG_W), 1) // LANES
    e_grp = (group_of_col == lane).astype(F32)
    bias = jnp.dot(bs_t, e_grp, precision=HI, preferred_element_type=F32)
    outs = []
    for n in range(t_rows // LANES):
        vb = v[n * LANES:(n + 1) * LANES]
        vg = jnp.stack([vb[:, g * LANES:(g + 1) * LANES] for g in range(4)])
        mg = _fdot(wm, vg)
        mixed = jnp.concatenate([mg[g] for g in range(4)], axis=-1) + bias
        outs.append(u[n * LANES:(n + 1) * LANES] * mixed)
    return jnp.concatenate(outs, axis=0)


def _merge_glue(ga, gb, ya, yb):
    return jax.nn.sigmoid(ga) * ya + jax.nn.sigmoid(gb) * yb


def _res_ln_glue(x, r, g, b):
    return _ln(ALPHA * x + r, g, b)


def _ffn_glue(ua, ub, cwa, cwb):
    return jax.nn.silu(_conv(ua, cwa)) * _conv(ub, cwb)


def _row(t, c, col=0):
    return pl.BlockSpec((t, c), lambda i: (i, col))


def _row_rev(t, c, nt, col=0):
    return pl.BlockSpec((t, c), lambda i: (nt - 1 - i, col))


def _halo(t, c, nt=None):
    per = t // HALO
    if nt is None:
        return pl.BlockSpec((HALO, c), lambda i: (jnp.maximum(i * per - 1, 0), 0))
    return pl.BlockSpec((HALO, c), lambda i: (jnp.maximum((nt - 1 - i) * per - 1, 0), 0))


def _full(shape):
    nd = len(shape)
    return pl.BlockSpec(shape, lambda i: (0,) * nd)


ANY = pl.BlockSpec(memory_space=pl.ANY)


def _sds(shape, dtype=F32):
    return jax.ShapeDtypeStruct(shape, dtype)


def _tile(s, want=256):
    for t in (want, 256, 128):
        if s % t == 0:
            return t
    raise ValueError(f"sequence length {s} is not a multiple of 128")


def proj_fwd(x, w, name, carry=None):
    s = x.shape[0]
    t = _tile(s)
    nt = s // t
    segs = [(0, 2048), (2048, 3072), (3072, 4096), (4096, 5120), (5120, IN_COLS_PAD)]

    def body(x_ref, w_ref, p_ref):
        xb = x_ref[...].astype(BF16)
        for lo, hi in segs:
            p_ref[:, lo:hi] = jnp.dot(xb, w_ref[:, lo:hi], preferred_element_type=F32)

    return _host_call(
        body, carry, nt, grid=(nt,), name=name, in_specs=[_row(t, D), _full((D, IN_COLS_PAD))],
        out_specs=[_row(t, IN_COLS_PAD)], out_shape=[_sds((s, IN_COLS_PAD))], scratch_shapes=[], operands=(x, w))


def dn_fwd(p, cq, a_row, dtb_row, nw_row, name, carry=None):
    s = p.shape[0]
    t = _tile(s)
    nt = s // t

    def body(qkv_ref, halo_ref, z_ref, ba_ref, cq_ref, a_ref, dtb_ref, nw_ref, o_ref, sst_ref, s_scr):
        i = pl.program_id(0)

        @pl.when(i == 0)
        def _():
            s_scr[...] = jnp.zeros_like(s_scr)

        halo = jnp.where(i == 0, 0.0, halo_ref[...])
        qkvcat = jnp.concatenate([halo, qkv_ref[...]], axis=0)
        cw = tuple(cq_ref[k:k + 1, :] for k in range(4))
        s_in = s_scr[...]
        sst_ref[0] = s_in
        o, s_out = _dn_glue(qkvcat, z_ref[...], ba_ref[...], s_in, cw, a_ref[...], dtb_ref[...], nw_ref[...])
        o_ref[...] = o.astype(BF16)
        s_scr[...] = s_out

    return _host_call(
        body, carry, nt, grid=(nt,), name=name,
        in_specs=[_row(t, QKV_W), _halo(t, QKV_W), _row(t, Z_W, C_Z // Z_W), _row(t, LANES, C_BA // LANES),
                  _full((4, QKV_W)), _full((1, LANES)), _full((1, LANES)), _full((1, LANES))],
        out_specs=[_row(t, Z_W), pl.BlockSpec((1, HEADS, DK, DK), lambda i: (i, 0, 0, 0))],
        out_shape=[_sds((s, Z_W), BF16), _sds((nt, HEADS, DK, DK))],
        scratch_shapes=[pltpu.VMEM((HEADS, DK, DK), F32)], operands=(p, p, p, p, cq, a_row, dtb_row, nw_row))


def sg_fwd(p, lng, lnb, w_s, bs_t, name):
    s = p.shape[0]
    t = _tile(s, 512)

    def body(uv_ref, lng_ref, lnb_ref, ws_ref, bs_ref, o_ref):
        o_ref[...] = _sg_glue(uv_ref[...], lng_ref[...], lnb_ref[...], ws_ref[...], bs_ref[...]).astype(BF16)

    return pl.pallas_call(
        body, grid=(s // t,), name=name,
        in_specs=[_row(t, 2 * SG_W, C_UV // (2 * SG_W)), _full((1, SG_W)), _full((1, SG_W)),
                  _full((4, LANES, LANES)), _full((LANES, LANES))],
        out_specs=_row(t, SG_W), out_shape=_sds((s, SG_W), BF16), compiler_params=_params())(p, lng, lnb, w_s, bs_t)


def merge_fwd(oa, ob, p, x, wa, wb, wo, g1, b1, name):
    s = x.shape[0]
    t = _tile(s, 512)

    def body(oa_ref, ob_ref, ga_ref, gb_ref, x_ref, wa_ref, wb_ref, wo_ref, g_ref, b_ref, x1_ref):
        ya = _mm(oa_ref[...], wa_ref[...])
        yb = _mm(ob_ref[...], wb_ref[...])
        h = _merge_glue(ga_ref[...], gb_ref[...], ya, yb)
        x1_ref[...] = _res_ln_glue(x_ref[...], _mm(h, wo_ref[...]), g_ref[...], b_ref[...])

    return pl.pallas_call(
        body, grid=(s // t,), name=name,
        in_specs=[_row(t, Z_W), _row(t, SG_W), _row(t, D, C_GA // D), _row(t, D, C_GB // D), _row(t, D),
                  _full((Z_W, D)), _full((SG_W, D)), _full((D, D)), _full((1, D)), _full((1, D))],
        out_specs=_row(t, D), out_shape=_sds((s, D)), compiler_params=_params())(oa, ob, p, p, x, wa, wb, wo, g1, b1)


def _load_ffn_weights(wup_hbm, wdn_hbm, wup_v, wdn_v, up_slots, dn_slots):
    for n, k in enumerate(up_slots):
        pltpu.sync_copy(wup_hbm.at[k], wup_v.at[n])
    for n, k in enumerate(dn_slots):
        pltpu.sync_copy(wdn_hbm.at[k], wdn_v.at[pl.ds(n * DN_SHARD, DN_SHARD)])


def ffn_fwd(x1, wup4, cf4, wdn4, g2, b2, name, carry=None):
    s = x1.shape[0]
    t = _tile(s, 512)
    nt = s // t

    def body(x1_ref, halo_ref, wup_hbm, cf_ref, wdn_hbm, g_ref, b_ref, pre_ref, x2_ref, wup_v, wdn_v):
        i = pl.program_id(0)

        @pl.when(i == 0)
        def _():
            _load_ffn_weights(wup_hbm, wdn_hbm, wup_v, wdn_v, range(4), range(4))

        x1v = x1_ref[...]
        halo = jnp.where(i == 0, 0.0, halo_ref[...])
        x1cat = jnp.concatenate([halo, x1v], axis=0).astype(BF16)
        f = None
        for h in range(2):
            ua = jnp.dot(x1cat, wup_v[h], preferred_element_type=F32)
            ub = jnp.dot(x1cat, wup_v[2 + h], preferred_element_type=F32)
            cwa = tuple(cf_ref[h, k:k + 1, :] for k in range(3))
            cwb = tuple(cf_ref[2 + h, k:k + 1, :] for k in range(3))
            act = _ffn_glue(ua, ub, cwa, cwb)
            fh = _mm(act, wdn_v[h * FFN_HALF:(h + 1) * FFN_HALF, :])
            f = fh if f is None else f + fh
        pre = ALPHA * x1v + f
        pre_ref[...] = pre
        x2_ref[...] = _ln(pre, g_ref[...], b_ref[...])

    return _host_call(
        body, carry, nt, grid=(nt,), name=name,
        in_specs=[_row(t, D), _halo(t, D), ANY, _full((4, 3, FFN_HALF)), ANY, _full((1, D)), _full((1, D))],
        out_specs=[_row(t, D), _row(t, D)], out_shape=[_sds((s, D)), _sds((s, D))],
        scratch_shapes=[pltpu.VMEM((4, D, FFN_HALF), BF16), pltpu.VMEM((FFN, D), BF16)],
        operands=(x1, x1, wup4, cf4, wdn4, g2, b2))


def loss_ln_bwd(pre, tgt, g, b, name):
    s = pre.shape[0]
    t = _tile(s, 512)

    def body(pre_ref, t_ref, g_ref, b_ref, dpre_ref, dg_ref, db_ref, loss_ref):
        first = pl.program_id(0) == 0
        y, vjp = jax.vjp(_ln, pre_ref[...], g_ref[...], b_ref[...])
        e = y - t_ref[...]
        dpre, dg, db = vjp(e * (1.0 / D))
        dpre_ref[...] = dpre
        _acc(dg_ref, dg, first)
        _acc(db_ref, db, first)
        part = jnp.sum(jnp.sum(e * e, axis=1, keepdims=True), axis=0, keepdims=True) * (0.5 / D)
        _acc(loss_ref, jnp.broadcast_to(part, loss_ref.shape), first)

    return pl.pallas_call(
        body, grid=(s // t,), name=name, in_specs=[_row(t, D), _row(t, D), _full((1, D)), _full((1, D))],
        out_specs=[_row(t, D), _full((1, D)), _full((1, D)), _full((8, LANES))],
        out_shape=[_sds((s, D)), _sds((1, D)), _sds((1, D)), _sds((8, LANES))], compiler_params=_params())(pre, tgt, g, b)


def _acc(ref, val, first):
    @pl.when(first)
    def _():
        ref[...] = val

    @pl.when(jnp.logical_not(first))
    def _():
        ref[...] += val


def _acc_tn(acc_ref, a, b, first, seg):
    n = b.shape[1]
    for lo in range(0, n, seg):
        hi = min(lo + seg, n)
        _acc(acc_ref.at[:, lo:hi], _mm_tn(a, b[:, lo:hi]), first)


def ln_bwd(pre, dy, g, b, name):
    s = pre.shape[0]
    t = _tile(s, 512)

    def body(pre_ref, dy_ref, g_ref, b_ref, dpre_ref, dg_ref, db_ref):
        _, vjp = jax.vjp(_ln, pre_ref[...], g_ref[...], b_ref[...])
        dpre, dg, db = vjp(dy_ref[...])
        dpre_ref[...] = dpre
        first = pl.program_id(0) == 0
        _acc(dg_ref, dg, first)
        _acc(db_ref, db, first)

    return pl.pallas_call(
        body, grid=(s // t,), name=name, in_specs=[_row(t, D), _row(t, D), _full((1, D)), _full((1, D))],
        out_specs=[_row(t, D), _full((1, D)), _full((1, D))],
        out_shape=[_sds((s, D)), _sds((1, D)), _sds((1, D))], compiler_params=_params())(pre, dy, g, b)


def ffn_bwd(x1, df, acc_in, acc_scale, wup4, cf4, wdn4, h, name, carry=None):
    s = x1.shape[0]
    t = _tile(s)
    nt = s // t

    def body(x1_ref, halo_ref, df_ref, acc_ref, wup_hbm, cf_ref, wdn_hbm,
             dx1_ref, dwup_hbm, dcf_ref, dwdn_hbm, wup_v, wdn_v, dwup_v, dwdn_v, carry):
        i = pl.program_id(0)
        j = nt - 1 - i
        first = i == 0

        @pl.when(first)
        def _():
            _load_ffn_weights(wup_hbm, wdn_hbm, wup_v, wdn_v, (h, 2 + h), (2 * h, 2 * h + 1))
            carry[...] = jnp.zeros_like(carry)

        halo = jnp.where(j == 0, 0.0, halo_ref[...])
        x1cat = jnp.concatenate([halo, x1_ref[...]], axis=0).astype(BF16)
        ua = jnp.dot(x1cat, wup_v[0], preferred_element_type=F32)
        ub = jnp.dot(x1cat, wup_v[1], preferred_element_type=F32)
        cwa = tuple(cf_ref[h, k:k + 1, :] for k in range(3))
        cwb = tuple(cf_ref[2 + h, k:k + 1, :] for k in range(3))
        act, vjp = jax.vjp(_ffn_glue, ua, ub, cwa, cwb)
        dfb = df_ref[...].astype(BF16)
        dact = _mm_nt(dfb, wdn_v[...])
        _acc_tn(dwdn_v, act.astype(BF16), dfb, first, 512)
        dua, dub, dcwa, dcwb = vjp(dact)
        x1b = x1cat[HALO:]
        dups = []
        for n, du in enumerate((dua, dub)):
            dups.append(jnp.concatenate([du[HALO:t], du[t:] + carry[n]], axis=0).astype(BF16))
            carry[n] = du[:HALO]
            _acc(dwup_v.at[n], _mm_tn(x1b, dups[n]), first)
        for k in range(3):
            _acc(dcf_ref.at[0, k:k + 1, :], dcwa[k], first)
            _acc(dcf_ref.at[1, k:k + 1, :], dcwb[k], first)
        dx1_ref[...] = acc_scale * acc_ref[...] + _mm_nt(dups[0], wup_v[0]) + _mm_nt(dups[1], wup_v[1])

        @pl.when(i == nt - 1)
        def _():
            pltpu.sync_copy(dwup_v, dwup_hbm)
            pltpu.sync_copy(dwdn_v, dwdn_hbm)

    return _host_call(
        body, carry, nt, grid=(nt,), name=name,
        in_specs=[_row_rev(t, D, nt), _halo(t, D, nt), _row_rev(t, D, nt), _row_rev(t, D, nt),
                  ANY, _full((4, 3, FFN_HALF)), ANY],
        out_specs=[_row_rev(t, D, nt), ANY, _full((2, 3, FFN_HALF)), ANY],
        out_shape=[_sds((s, D)), _sds((2, D, FFN_HALF)), _sds((2, 3, FFN_HALF)), _sds((FFN_HALF, D))],
        scratch_shapes=[pltpu.VMEM((2, D, FFN_HALF), BF16), pltpu.VMEM((FFN_HALF, D), BF16),
                        pltpu.VMEM((2, D, FFN_HALF), F32), pltpu.VMEM((FFN_HALF, D), F32),
                        pltpu.VMEM((2, HALO, FFN_HALF), F32)],
        operands=(x1, x1, df, acc_in, wup4, cf4, wdn4))


def merge_bwd(oa, ob, p, x, dx1, wa, wb, wo, g1, b1, name, carry=None):
    s = x.shape[0]
    t = _tile(s)

    def body(oa_ref, ob_ref, ga_ref, gb_ref, x_ref, dx1_ref, wa_ref, wb_ref, wo_ref, g_ref, b_ref,
             doa_ref, dob_ref, dga_ref, dgb_ref, dx_ref, dwa_ref, dwb_ref, dwo_ref, dg_ref, db_ref):
        first = pl.program_id(0) == 0
        oa = oa_ref[...]
        ob = ob_ref[...]
        ya = _mm(oa, wa_ref[...])
        yb = _mm(ob, wb_ref[...])
        h, vjp1 = jax.vjp(_merge_glue, ga_ref[...], gb_ref[...], ya, yb)
        hb = h.astype(BF16)
        r = _mm(hb, wo_ref[...])
        _, vjp2 = jax.vjp(_res_ln_glue, x_ref[...], r, g_ref[...], b_ref[...])
        dx, dr, dg, db = vjp2(dx1_ref[...])
        dx_ref[...] = dx
        _acc(dg_ref, dg, first)
        _acc(db_ref, db, first)
        drb = dr.astype(BF16)
        dh = _mm_nt(drb, wo_ref[...])
        _acc(dwo_ref, _mm_tn(hb, drb), first)
        dga, dgb, dya, dyb = vjp1(dh)
        dga_ref[...] = dga.astype(BF16)
        dgb_ref[...] = dgb.astype(BF16)
        dyab = dya.astype(BF16)
        dybb = dyb.astype(BF16)
        doa_ref[...] = _mm_nt(dyab, wa_ref[...]).astype(BF16)
        dob_ref[...] = _mm_nt(dybb, wb_ref[...]).astype(BF16)
        _acc(dwa_ref, _mm_tn(oa, dyab), first)
        _acc(dwb_ref, _mm_tn(ob, dybb), first)

    return _host_call(
        body, carry, s // t, grid=(s // t,), name=name,
        in_specs=[_row(t, Z_W), _row(t, SG_W), _row(t, D, C_GA // D), _row(t, D, C_GB // D), _row(t, D), _row(t, D),
                  _full((Z_W, D)), _full((SG_W, D)), _full((D, D)), _full((1, D)), _full((1, D))],
        out_specs=[_row(t, Z_W), _row(t, SG_W), _row(t, D), _row(t, D), _row(t, D),
                   _full((Z_W, D)), _full((SG_W, D)), _full((D, D)), _full((1, D)), _full((1, D))],
        out_shape=[_sds((s, Z_W), BF16), _sds((s, SG_W), BF16), _sds((s, D), BF16), _sds((s, D), BF16), _sds((s, D)),
                   _sds((Z_W, D)), _sds((SG_W, D)), _sds((D, D)), _sds((1, D)), _sds((1, D))],
        scratch_shapes=[], operands=(oa, ob, p, p, x, dx1, wa, wb, wo, g1, b1))


def sg_bwd(p, dob, lng, lnb, w_s, bs_t, name):
    s = p.shape[0]
    t = _tile(s, 512)

    def body(uv_ref, dob_ref, lng_ref, lnb_ref, ws_ref, bs_ref, duv_ref, dlng_ref, dlnb_ref, dws_ref, dbs_ref):
        first = pl.program_id(0) == 0
        _, vjp = jax.vjp(_sg_glue, uv_ref[...], lng_ref[...], lnb_ref[...], ws_ref[...], bs_ref[...])
        duv, dlng, dlnb, dws, dbs = vjp(dob_ref[...].astype(F32))
        duv_ref[...] = duv.astype(BF16)
        _acc(dlng_ref, dlng, first)
        _acc(dlnb_ref, dlnb, first)
        _acc(dws_ref, dws, first)
        _acc(dbs_ref, dbs, first)

    return pl.pallas_call(
        body, grid=(s // t,), name=name,
        in_specs=[_row(t, 2 * SG_W, C_UV // (2 * SG_W)), _row(t, SG_W), _full((1, SG_W)), _full((1, SG_W)),
                  _full((4, LANES, LANES)), _full((LANES, LANES))],
        out_specs=[_row(t, 2 * SG_W), _full((1, SG_W)), _full((1, SG_W)), _full((4, LANES, LANES)), _full((LANES, LANES))],
        out_shape=[_sds((s, 2 * SG_W), BF16), _sds((1, SG_W)), _sds((1, SG_W)), _sds((4, LANES, LANES)), _sds((LANES, LANES))],
        compiler_params=_params())(p, dob, lng, lnb, w_s, bs_t)


def dn_bwd(p, sst, doa, cq, a_row, dtb_row, nw_row, name):
    s = p.shape[0]
    t = _tile(s)
    nt = s // t

    def body(qkv_ref, halo_ref, z_ref, ba_ref, sst_ref, doa_ref, cq_ref, a_ref, dtb_ref, nw_ref,
             dqkv_ref, dz_ref, dba_ref, dcq_ref, da_ref, ddtb_ref, dnw_ref, ds_scr, carry):
        i = pl.program_id(0)
        j = nt - 1 - i
        first = i == 0

        @pl.when(first)
        def _():
            ds_scr[...] = jnp.zeros_like(ds_scr)
            carry[...] = jnp.zeros_like(carry)

        halo = jnp.where(j == 0, 0.0, halo_ref[...])
        qkvcat = jnp.concatenate([halo, qkv_ref[...]], axis=0)
        cw = tuple(cq_ref[k:k + 1, :] for k in range(4))
        _, vjp = jax.vjp(_dn_glue, qkvcat, z_ref[...], ba_ref[...], sst_ref[0], cw, a_ref[...], dtb_ref[...], nw_ref[...])
        dqkvcat, dz, dba, ds_in, dcw, da, ddtb, dnw = vjp((doa_ref[...].astype(F32), ds_scr[...]))
        ds_scr[...] = ds_in
        dz_ref[...] = dz.astype(BF16)
        dba_ref[...] = dba.astype(BF16)
        dtile = dqkvcat[HALO:]
        dqkv_ref[...] = dtile.astype(BF16)
        dqkv_ref[t - HALO:t, :] = (dtile[t - HALO:] + carry[...]).astype(BF16)
        carry[...] = dqkvcat[:HALO]
        for k in range(4):
            _acc(dcq_ref.at[k:k + 1, :], dcw[k], first)
        _acc(da_ref, da, first)
        _acc(ddtb_ref, ddtb, first)
        _acc(dnw_ref, dnw, first)

    return pl.pallas_call(
        body, grid=(nt,), name=name,
        in_specs=[_row_rev(t, QKV_W, nt), _halo(t, QKV_W, nt), _row_rev(t, Z_W, nt, C_Z // Z_W),
                  _row_rev(t, LANES, nt, C_BA // LANES),
                  pl.BlockSpec((1, HEADS, DK, DK), lambda i: (nt - 1 - i, 0, 0, 0)), _row_rev(t, Z_W, nt),
                  _full((4, QKV_W)), _full((1, LANES)), _full((1, LANES)), _full((1, LANES))],
        out_specs=[_row_rev(t, QKV_W, nt), _row_rev(t, Z_W, nt), _row_rev(t, LANES, nt),
                   _full((4, QKV_W)), _full((1, LANES)), _full((1, LANES)), _full((1, LANES))],
        out_shape=[_sds((s, QKV_W), BF16), _sds((s, Z_W), BF16), _sds((s, LANES), BF16),
                   _sds((4, QKV_W)), _sds((1, LANES)), _sds((1, LANES)), _sds((1, LANES))],
        scratch_shapes=[pltpu.VMEM((HEADS, DK, DK), F32), pltpu.VMEM((HALO, QKV_W), F32)],
        compiler_params=_params())(p, p, p, p, sst, doa, cq, a_row, dtb_row, nw_row)


def proj_bwd(dps, dxd, w, name):
    s = dxd.shape[0]
    t = _tile(s, 512)
    n = len(dps)

    def body(*refs):
        dp_refs, dxd_ref, w_hbm, dx_ref, w_v = refs[:n], refs[n], refs[n + 1], refs[n + 2], refs[n + 3]

        @pl.when(pl.program_id(0) == 0)
        def _():
            pltpu.sync_copy(w_hbm, w_v)

        dp = jnp.concatenate([r[...] for r in dp_refs], axis=1)
        dx_ref[...] = dxd_ref[...] + _mm_nt(dp, w_v[...])

    return pl.pallas_call(
        body, grid=(s // t,), name=name,
        in_specs=[_row(t, dp.shape[1]) for dp in dps] + [_row(t, D), ANY],
        out_specs=_row(t, D), out_shape=_sds((s, D)),
        scratch_shapes=[pltpu.VMEM((D, IN_COLS_PAD), BF16)],
        compiler_params=_params())(*dps, dxd, w)


def wgrad(x, dp, col, into, name):
    s, n = dp.shape
    tk = _tile(s, 1024)
    tn = next(c for c in (1024, 768, 512, 256, 128) if n % c == 0 and col % c == 0)
    block = col // tn

    def body(x_ref, dp_ref, *rest):
        o_ref = rest[-1]
        _acc(o_ref, _mm_tn(x_ref[...], dp_ref[...]), pl.program_id(1) == 0)

    operands = (x, dp) if into is None else (x, dp, into)
    return pl.pallas_call(
        body, grid=(n // tn, s // tk), name=name,
        in_specs=[pl.BlockSpec((tk, D), lambda j, k: (k, 0)), pl.BlockSpec((tk, tn), lambda j, k: (k, j))]
        + ([] if into is None else [ANY]),
        out_specs=pl.BlockSpec((D, tn), lambda j, k: (0, block + j)), out_shape=_sds((D, IN_COLS_PAD)),
        input_output_aliases={} if into is None else {2: 0},
        compiler_params=_params(2))(*operands)


def _rows_block(rows, cols):
    cap = max(HALO, (2 * 1024 * 1024) // (cols * 4))
    for cand in range(min(rows, cap) // HALO * HALO, HALO - 1, -HALO):
        if rows % cand == 0:
            return cand
    return rows


def adam_call(w, g, m, v, name):
    rows, cols = w.shape
    tr = _rows_block(rows, cols)
    c1 = 1.0 - ADAM_B1 ** ADAM_STEP
    c2 = 1.0 - ADAM_B2 ** ADAM_STEP

    def body(w_ref, g_ref, m_ref, v_ref, go_ref, d_ref, nm_ref, nv_ref):
        gv = g_ref[...]
        go_ref[...] = gv
        nm = ADAM_B1 * m_ref[...] + (1.0 - ADAM_B1) * gv
        nv = ADAM_B2 * v_ref[...] + (1.0 - ADAM_B2) * (gv * gv)
        d_ref[...] = -ADAM_LR * ((nm / c1) / (jnp.sqrt(nv / c2) + ADAM_EPS) + ADAM_WD * w_ref[...])
        nm_ref[...] = nm
        nv_ref[...] = nv

    spec = pl.BlockSpec((tr, cols), lambda i: (i, 0))
    return pl.pallas_call(
        body, grid=(rows // tr,), name=name, in_specs=[spec] * 4, out_specs=[spec] * 4,
        out_shape=[_sds((rows, cols))] * 4, compiler_params=_params())(w, g, m, v)


def _place():
    return lax.axis_index("x"), lax.axis_index("y"), lax.axis_index("c")


def _other_chips(x, y):
    return [(1 - x, y), (x, 1 - y), (1 - x, 1 - y)]


def _remote(src, dst, send_sem, recv_sem, to):
    return pltpu.make_async_remote_copy(src_ref=src, dst_ref=dst, send_sem=send_sem, recv_sem=recv_sem,
                                        device_id=to, device_id_type=MESH)


class _Gather:
    def __init__(self, ins, outs, send_sems, recv_sems, local_sems):
        self.ins, self.outs, self.n = ins, outs, len(ins)
        self.send_sems, self.recv_sems, self.local_sems = send_sems, recv_sems, local_sems
        self.x, self.y, self.c = _place()
        self.me = 2 * self.x + self.y
        self.chips = _other_chips(self.x, self.y)

    def _copy(self, t, k, slot, part, to, src=None):
        dst = self.outs[t].at[slot, part]
        return _remote(dst if src is None else src, dst, self.send_sems.at[6 * t + k], self.recv_sems.at[6 * t + k], to)

    def _mine(self):
        return [pltpu.make_async_copy(self.ins[t].at[p], self.outs[t].at[self.me, p], self.local_sems.at[2 * t + p])
                for t in range(self.n) for p in range(2)]

    def _first(self):
        return [self._copy(t, k, self.me, self.c, (cx, cy, self.c), src=self.ins[t].at[self.c])
                for k, (cx, cy) in enumerate(self.chips) for t in range(self.n)]

    def start(self):
        for cp in self._mine() + self._first():
            cp.start()

    def finish(self):
        x, y, c = self.x, self.y, self.c
        passed = []
        for k, (cx, cy) in enumerate(self.chips):
            for t in range(self.n):
                self._copy(t, k, 2 * cx + cy, c, (x, y, c)).wait_recv()
                passed.append(self._copy(t, 3 + k, 2 * cx + cy, c, (x, y, 1 - c)))
                passed[-1].start()
        for k, (cx, cy) in enumerate(self.chips):
            for t in range(self.n):
                self._copy(t, 3 + k, 2 * cx + cy, 1 - c, (x, y, c)).wait_recv()
        for cp in self._first() + passed:
            cp.wait_send()
        for cp in self._mine():
            cp.wait()


class _Carried(NamedTuple):
    ins: tuple
    out_shapes: tuple
    scratch: tuple
    make: Callable
    aliases: dict


def _host_call(body, carry, steps, *, grid, name, in_specs, out_specs, out_shape, scratch_shapes, operands):
    in_specs, out_specs, out_shape, scratch_shapes = list(in_specs), list(out_specs), list(out_shape), list(scratch_shapes)
    aliases = {}
    if carry is not None:
        n_in, n_out, n_scr = len(in_specs), len(out_specs), len(scratch_shapes)
        n_ci, n_co = len(carry.ins), len(carry.out_shapes)
        plain = body

        def body(*refs):
            ins, cins = refs[:n_in], refs[n_in:n_in + n_ci]
            outs = refs[n_in + n_ci:n_in + n_ci + n_out]
            couts = refs[n_in + n_ci + n_out:n_in + n_ci + n_out + n_co]
            rest = refs[n_in + n_ci + n_out + n_co:]
            exchange = carry.make(cins, couts, *rest[n_scr:])
            pl.when(pl.program_id(0) == 0)(exchange.start)
            plain(*ins, *outs, *rest[:n_scr])
            pl.when(pl.program_id(0) == steps - 1)(exchange.finish)

        aliases = {n_in + i: n_out + j for i, j in carry.aliases.items()}
        in_specs += [ANY] * n_ci
        out_specs += [ANY] * n_co
        out_shape += list(carry.out_shapes)
        scratch_shapes += list(carry.scratch)
        operands = tuple(operands) + tuple(carry.ins)
    return pl.pallas_call(
        body, grid=grid, name=name, in_specs=in_specs, out_specs=out_specs, out_shape=out_shape,
        scratch_shapes=scratch_shapes, input_output_aliases=aliases, compiler_params=_params(len(grid)))(*operands)


def exchange(carry, name):
    n_i, n_o = len(carry.ins), len(carry.out_shapes)

    def body(*refs):
        ex = carry.make(refs[:n_i], refs[n_i:n_i + n_o], *refs[n_i + n_o:])
        ex.start()
        ex.finish()

    return pl.pallas_call(
        body, name=name, in_specs=[ANY] * n_i, out_specs=[ANY] * n_o, out_shape=list(carry.out_shapes),
        scratch_shapes=list(carry.scratch), input_output_aliases=dict(carry.aliases))(*carry.ins)


def _dma_sems(*counts):
    return tuple(pltpu.SemaphoreType.DMA((n,)) for n in counts)


def carried_gather(shards):
    n = len(shards)
    return _Carried(tuple(shards), tuple(_sds((N_CHIPS,) + a.shape, a.dtype) for a in shards),
                    _dma_sems(6 * n, 6 * n, 2 * n), _Gather, {})


class _PairSwap:
    def __init__(self, ins, outs, send_sems, recv_sems):
        x, y, c = _place()
        self.copies = [_remote(ins[t].at[:, 1 - c], outs[t], send_sems.at[t], recv_sems.at[t], (x, y, 1 - c))
                       for t in range(len(ins))]

    def start(self):
        for cp in self.copies:
            cp.start()

    def finish(self):
        for cp in self.copies:
            cp.wait()


def carried_pair_swap(views):
    n = len(views)
    return _Carried(tuple(views), tuple(_sds((v.shape[0],) + v.shape[2:]) for v in views), _dma_sems(n, n), _PairSwap, {})


class _Scatter:
    def __init__(self, srcs, outs, send_sems, recv_sems, local_sems, pieces):
        self.srcs, self.outs, self.pieces, self.n = srcs, outs, pieces, len(pieces)
        self.send_sems, self.recv_sems, self.local_sems = send_sems, recv_sems, local_sems
        self.x, self.y, self.c = _place()
        self.me = 2 * self.x + self.y

    def _piece(self, t, k):
        idx, lead, cols = self.pieces[t][k]
        ref = self.srcs[idx].at[lead]
        return ref if cols is None else ref.at[:, pl.ds(cols[0], cols[1])]

    def _local(self, t, k):
        return pltpu.make_async_copy(self._piece(t, k), self.outs[t].at[k], self.local_sems.at[t])

    def _each_chip(self, mine, others):
        for k in range(N_CHIPS):
            pl.when(self.me == k)(functools.partial(mine, k))
            pl.when(self.me != k)(functools.partial(others, k))

    def start(self):
        def mine(k):
            for t in range(self.n):
                self._local(t, k).start()

        def others(k):
            for t in range(self.n):
                _remote(self._piece(t, k), self.outs[t].at[self.me], self.send_sems.at[N_CHIPS * t + k],
                        self.recv_sems.at[N_CHIPS * t + self.me], (k // 2, k % 2, self.c)).start()

        self._each_chip(mine, others)

    def finish(self):
        def mine(k):
            for t in range(self.n):
                self._local(t, k).wait()

        def others(k):
            for t in range(self.n):
                cp = _remote(self._piece(t, k), self.outs[t].at[k], self.send_sems.at[N_CHIPS * t + k],
                             self.recv_sems.at[N_CHIPS * t + k], (self.x, self.y, self.c))
                cp.wait_recv()
                cp.wait_send()

        self._each_chip(mine, others)


def carried_scatter(srcs, pieces, part_shapes):
    n = len(pieces)
    return _Carried(tuple(srcs), tuple(_sds((N_CHIPS,) + tuple(shp), srcs[0].dtype) for shp in part_shapes),
                    _dma_sems(N_CHIPS * n, N_CHIPS * n, n), functools.partial(_Scatter, pieces=pieces), {})


class _PairJoin:
    def __init__(self, ins, outs, send_sems, recv_sems, layer):
        self.ins, self.outs, self.layer, self.n = ins, outs, layer, len(ins)
        self.send_sems, self.recv_sems = send_sems, recv_sems
        self.x, self.y, self.c = _place()

    def _copy(self, t, half, to):
        return _remote(self.ins[t].at[self.layer, self.c], self.outs[t].at[self.layer, half],
                       self.send_sems.at[t], self.recv_sems.at[t], to)

    def start(self):
        for t in range(self.n):
            self._copy(t, self.c, (self.x, self.y, 1 - self.c)).start()

    def finish(self):
        for t in range(self.n):
            self._copy(t, self.c, (self.x, self.y, 1 - self.c)).wait_send()
            self._copy(t, 1 - self.c, (self.x, self.y, self.c)).wait_recv()


def carried_join(bufs, layer):
    n = len(bufs)
    return _Carried(tuple(bufs), tuple(_sds(b.shape) for b in bufs), _dma_sems(n, n),
                    functools.partial(_PairJoin, layer=layer), {t: t for t in range(n)})


def pair_add_half(mine, theirs, c_vec, name):
    g, _, h, b = mine.shape
    tr = _rows_block(h, b)

    def body(c_ref, a_ref, b_ref, o_ref):
        o_ref[...] = (a_ref[...] + b_ref[...]).astype(BF16)

    part = pl.BlockSpec((None, tr, b), lambda j, i, c: (j, i, 0))
    grid_spec = pltpu.PrefetchScalarGridSpec(
        num_scalar_prefetch=1, grid=(g, h // tr),
        in_specs=[pl.BlockSpec((None, None, tr, b), lambda j, i, c: (j, c[0], i, 0)), part], out_specs=part)
    return pl.pallas_call(body, grid_spec=grid_spec, name=name, out_shape=_sds((g, h, b), BF16),
                          compiler_params=_params(2))(c_vec, mine, theirs)


def chips_add_into(recv, into, layer, c_vec, name):
    n, h, b = recv.shape
    tr = _rows_block(h, b)

    def body(c_ref, r0, r1, r2, r3, *rest):
        rest[-1][...] = ((r0[...].astype(F32) + r1[...].astype(F32)) + r2[...].astype(F32)) + r3[...].astype(F32)

    grid_spec = pltpu.PrefetchScalarGridSpec(
        num_scalar_prefetch=1, grid=(h // tr,),
        in_specs=[pl.BlockSpec((None, tr, b), lambda i, c, k=k: (k, i, 0)) for k in range(n)]
        + ([] if into is None else [ANY]),
        out_specs=pl.BlockSpec((None, None, tr, b), lambda i, c: (layer, c[0], i, 0)))
    return pl.pallas_call(
        body, grid_spec=grid_spec, name=name, out_shape=_sds((2, 2, h, b)),
        input_output_aliases={} if into is None else {1 + n: 0},
        compiler_params=_params())(c_vec, *([recv] * n), *(() if into is None else (into,)))


def allsum_small(v, name):
    rows, lanes = v.shape
    n_dev = 8

    def body(v_ref, out_ref, buf, send_sems, recv_sems):
        x, y, c = _place()
        me, sibling = (x, y, c), (x, y, 1 - c)
        chips = _other_chips(x, y)

        def slot(px, py, pc):
            return buf.at[4 * px + 2 * py + pc]

        def copy(k, block, to, src=None):
            return pltpu.make_async_remote_copy(
                src_ref=slot(*block) if src is None else src, dst_ref=slot(*block),
                send_sem=send_sems.at[k], recv_sem=recv_sems.at[k], device_id=to, device_id_type=MESH)

        slot(*me)[...] = v_ref[...]
        first = [copy(0, me, sibling, src=v_ref)]
        first += [copy(1 + k, me, (*chip, c), src=v_ref) for k, chip in enumerate(chips)]
        for cp in first:
            cp.start()
        passed = [copy(4 + k, (*chip, c), sibling) for k, chip in enumerate(chips)]
        for k, chip in enumerate(chips):
            copy(1 + k, (*chip, c), me).wait_recv()
            passed[k].start()
        copy(0, sibling, me).wait_recv()
        for k, chip in enumerate(chips):
            copy(4 + k, (*chip, 1 - c), me).wait_recv()
        for cp in first + passed:
            cp.wait_send()
        acc = buf[0]
        for d in range(1, n_dev):
            acc = acc + buf[d]
        out_ref[...] = acc

    vm = pl.BlockSpec(memory_space=pltpu.VMEM)
    return pl.pallas_call(
        body, name=name, in_specs=[vm], out_specs=vm, out_shape=_sds((rows, lanes)),
        scratch_shapes=[pltpu.VMEM((n_dev, rows, lanes), F32), pltpu.SemaphoreType.DMA((7,)), pltpu.SemaphoreType.DMA((7,))],
        compiler_params=pltpu.CompilerParams(vmem_limit_bytes=VMEM_LIMIT),
    )(v)


BIG = ("w_in", "w_branch_a", "w_branch_b", "w_out", "w_up", "w_down")
CONV = ("conv_qkv", "conv_ffn")
REPL =("a_log", "dt_bias", "dn_norm_w", "sg_ln_g", "sg_ln_b", "w_spatial", "b_spatial", "ln1_g", "ln1_b", "ln2_g", "ln2_b")


def _pad_rows(flat, mult):
    n = flat.shape[0]
    unit = mult * LANES
    total = -(-n // unit) * unit
    return jnp.pad(flat, (0, total - n)).reshape(total // LANES, LANES)


def _pack(arrs, mult):
    return _pad_rows(jnp.concatenate([a.reshape(-1) for a in arrs]), mult)


def _unpack(flat, shapes):
    out, off = [], 0
    for shp in shapes:
        n = math.prod(shp)
        out.append(flat[off:off + n].reshape(shp))
        off += n
    return out


def kernel(x, w_in, conv_qkv, a_log, dt_bias, dn_norm_w, w_branch_a, sg_ln_g, sg_ln_b, w_spatial, b_spatial, w_branch_b, w_out, ln1_g, ln1_b, w_up, conv_ffn, w_down, ln2_g, ln2_b, loss_target, m_w_in, m_conv_qkv, m_a_log, m_dt_bias, m_dn_norm_w, m_w_branch_a, m_sg_ln_g, m_sg_ln_b, m_w_spatial, m_b_spatial, m_w_branch_b, m_w_out, m_ln1_g, m_ln1_b, m_w_up, m_conv_ffn, m_w_down, m_ln2_g, m_ln2_b, v_w_in, v_conv_qkv, v_a_log, v_dt_bias, v_dn_norm_w, v_w_branch_a, v_sg_ln_g, v_sg_ln_b, v_w_spatial, v_b_spatial, v_w_branch_b, v_w_out, v_ln1_g, v_ln1_b, v_w_up, v_conv_ffn, v_w_down, v_ln2_g, v_ln2_b):
    names = ("w_in", "conv_qkv", "a_log", "dt_bias", "dn_norm_w", "w_branch_a", "sg_ln_g", "sg_ln_b", "w_spatial",
             "b_spatial", "w_branch_b", "w_out", "ln1_g", "ln1_b", "w_up", "conv_ffn", "w_down", "ln2_g", "ln2_b")
    w = dict(zip(names, (w_in, conv_qkv, a_log, dt_bias, dn_norm_w, w_branch_a, sg_ln_g, sg_ln_b, w_spatial,
                         b_spatial, w_branch_b, w_out, ln1_g, ln1_b, w_up, conv_ffn, w_down, ln2_g, ln2_b)))
    m = dict(zip(names, (m_w_in, m_conv_qkv, m_a_log, m_dt_bias, m_dn_norm_w, m_w_branch_a, m_sg_ln_g, m_sg_ln_b,
                         m_w_spatial, m_b_spatial, m_w_branch_b, m_w_out, m_ln1_g, m_ln1_b, m_w_up, m_conv_ffn,
                         m_w_down, m_ln2_g, m_ln2_b)))
    v = dict(zip(names, (v_w_in, v_conv_qkv, v_a_log, v_dt_bias, v_dn_norm_w, v_w_branch_a, v_sg_ln_g, v_sg_ln_b,
                         v_w_spatial, v_b_spatial, v_w_branch_b, v_w_out, v_ln1_g, v_ln1_b, v_w_up, v_conv_ffn,
                         v_w_down, v_ln2_g, v_ln2_b)))
    chip = 2 * lax.axis_index("x") + lax.axis_index("y")
    s = x.shape[1]
    xs = x.reshape(s, D)
    tgt = loss_target.reshape(s, D)

    big_names, conv_names = list(BIG), list(CONV)

    def in_two(name, l):
        rows, cols = w[name].shape[1:]
        return w[name][l].astype(BF16).reshape(2, rows // 2, cols)

    def whole(name, landed):
        rows, cols = w[name].shape[1:]
        return landed.reshape(N_CHIPS, rows, cols)

    first = exchange(carried_gather([in_two("w_in", 0)] + [w[n] for n in conv_names]), "gather_first")
    got = [{"w_in": whole("w_in", first[0])}, {}]
    conv_taps = dict(zip(conv_names, first[1:]))
    narrow, wide = ["w_branch_a", "w_branch_b", "w_out"], ["w_up", "w_down"]
    carried = {"proj_fwd0": (0, narrow), "dn_fwd0": (0, wide), "ffn_fwd0": (1, ["w_in"] + narrow), "dn_fwd1": (1, wide)}

    def carry(call):
        if call not in carried:
            return None
        l, which = carried[call]
        return carried_gather([in_two(n, l) for n in which])

    def land(call, landed):
        l, which = carried.get(call, (0, []))
        got[l].update({n: whole(n, a) for n, a in zip(which, landed)})

    def lane_row(vec, off):
        return jnp.zeros((1, LANES), F32).at[0, off:off + vec.shape[0]].set(vec)

    def side_by_side(blocks):
        return jnp.concatenate([blocks[k] for k in range(N_CHIPS)], axis=1)

    def small_params(l):
        return dict(
            cq=side_by_side(conv_taps["conv_qkv"][:, l]),
            a_row=lane_row(w["a_log"][l], HEADS), dtb_row=lane_row(w["dt_bias"][l], HEADS),
            nw_row=w["dn_norm_w"][l].reshape(1, DK),
            lng=w["sg_ln_g"][l].reshape(1, SG_W), lnb=w["sg_ln_b"][l].reshape(1, SG_W),
            w_s=w["w_spatial"][l], bs_t=jnp.zeros((LANES, LANES), F32).at[:, :4].set(w["b_spatial"][l].T),
            g1=w["ln1_g"][l].reshape(1, D), b1=w["ln1_b"][l].reshape(1, D),
            cf=conv_taps["conv_ffn"][:, l],
            g2=w["ln2_g"][l].reshape(1, D), b2=w["ln2_b"][l].reshape(1, D))

    layers, saved = [], []
    h_in = xs
    for l in range(DEPTH):
        p = small_params(l)
        wi = side_by_side(got[l]["w_in"])
        p["w_in"] = jnp.concatenate([wi[:, :2048], wi[:, 2056:3080], wi[:, 3080:5128], wi[:, 2048:2056],
                                     jnp.zeros((D, IN_COLS_PAD - 5128), BF16)], axis=1)
        proj, *landed = proj_fwd(h_in, p["w_in"], f"proj_fwd{l}", carry=carry(f"proj_fwd{l}"))
        land(f"proj_fwd{l}", landed)
        oa, sst, *landed = dn_fwd(proj, p["cq"], p["a_row"], p["dtb_row"], p["nw_row"], f"dn_fwd{l}",
                                  carry=carry(f"dn_fwd{l}"))
        land(f"dn_fwd{l}", landed)
        ob = sg_fwd(proj, p["lng"], p["lnb"], p["w_s"], p["bs_t"], f"sg_fwd{l}")
        p.update(wa=side_by_side(got[l]["w_branch_a"]), wb=side_by_side(got[l]["w_branch_b"]),
                 wo=got[l]["w_out"].reshape(D, D))
        x1 = merge_fwd(oa, ob, proj, h_in, p["wa"], p["wb"], p["wo"], p["g1"], p["b1"], f"merge_fwd{l}")
        pre2, x2, *landed = ffn_fwd(x1, got[l]["w_up"], p["cf"], got[l]["w_down"], p["g2"], p["b2"], f"ffn_fwd{l}",
                                    carry=carry(f"ffn_fwd{l}"))
        land(f"ffn_fwd{l}", landed)
        layers.append(p)
        saved.append(dict(x=h_in, proj=proj, oa=oa, ob=ob, sst=sst, x1=x1, pre2=pre2))
        h_in = x2


    small_names = conv_names + list(REPL)
    grads = {n: [None] * DEPTH for n in small_names}
    c_vec = jnp.stack([lax.axis_index("c")]).astype(jnp.int32)
    tags = ("w_in", "w_a", "w_b", "w_out", "w_up0", "w_up1", "w_dn0", "w_dn1")
    groups = (1, 1, 1, N_CHIPS, 2, 2, 2, 2)
    ab_cols = w["w_branch_a"].shape[2]
    pieces = [
        [(0, (k,), None) for k in range(N_CHIPS)],
        [(1, (0,), (k * ab_cols, ab_cols)) for k in range(N_CHIPS)],
        [(2, (0,), (k * ab_cols, ab_cols)) for k in range(N_CHIPS)],
        [(3, (k,), None) for k in range(N_CHIPS)],
        [(4 + k % 2, (k // 2,), None) for k in range(N_CHIPS)],
        [(6 + k // 2, (k % 2,), None) for k in range(N_CHIPS)],
    ]
    part_shapes = [(w[n].shape[1] // 2, w[n].shape[2]) for n in big_names]

    def views(arrs):
        return [a.reshape(g, 2, a.size // a.shape[-1] // (2 * g), a.shape[-1]) for a, g in zip(arrs, groups)]

    def pair_sums(l, mine, theirs):
        sums = [pair_add_half(m_, t_, c_vec, f"reduce_pair_add{l}_{tag}") for tag, m_, t_ in zip(tags, mine, theirs)]
        pin = sums[0][0]
        natural = jnp.concatenate([pin[:, :2048], pin[:, C_BA:C_BA + 8], pin[:, 2048:C_BA]], axis=1)
        return [jnp.stack(jnp.split(natural, N_CHIPS, axis=1))] + sums[1:]

    def chip_sums(l, recv, bufs):
        return [chips_add_into(r, None if bufs is None else bufs[i], l, c_vec, f"reduce_chips_add{l}_{n}")
                for i, (n, r) in enumerate(zip(big_names, recv))]

    above, bufs = None, None
    for l in reversed(range(DEPTH)):
        p, a = layers[l], saved[l]
        if l == DEPTH - 1:
            dpre2, dg2, db2, loss_part = loss_ln_bwd(a["pre2"], tgt, p["g2"], p["b2"], "loss_ln2_bwd")
            loss = lax.psum(loss_part[0, 0], ("x", "y", "c"))
        else:
            dpre2, dg2, db2 = ln_bwd(a["pre2"], dy, p["g2"], p["b2"], f"ln2_bwd{l}")
        dx1, dwup0, dcf0, dwdn0, *theirs = ffn_bwd(
            a["x1"], dpre2, dpre2, ALPHA, got[l]["w_up"], p["cf"], got[l]["w_down"], 0, f"ffn_bwd{l}a",
            carry=carried_pair_swap(above) if above else None)
        srcs = pair_sums(l + 1, above, theirs) if above else None
        dx1, dwup1, dcf1, dwdn1, *recv = ffn_bwd(
            a["x1"], dpre2, dx1, 1.0, got[l]["w_up"], p["cf"], got[l]["w_down"], 1, f"ffn_bwd{l}b",
            carry=carried_scatter(srcs, pieces, part_shapes) if above else None)
        bufs = chip_sums(l + 1, recv, bufs) if above else bufs
        doa, dob, dga, dgb, dxd, dwa, dwb, dwo, dg1, db1, *joined = merge_bwd(
            a["oa"], a["ob"], a["proj"], a["x"], dx1, p["wa"], p["wb"], p["wo"], p["g1"], p["b1"], f"merge_bwd{l}",
            carry=carried_join(bufs, l + 1) if above else None)
        bufs = joined if above else bufs
        duv, dlng, dlnb, dws, dbs = sg_bwd(a["proj"], dob, p["lng"], p["lnb"], p["w_s"], p["bs_t"], f"sg_bwd{l}")
        dqkv, dz, dba, dcq, da, ddtb, dnw = dn_bwd(a["proj"], a["sst"], doa, p["cq"], p["a_row"], p["dtb_row"],
                                                   p["nw_row"], f"dn_bwd{l}")
        dy = proj_bwd([dqkv, dz, duv, dga, dgb, dba], dxd, p["w_in"], f"proj_bwd{l}")
        dwi = None
        for tag, dp, col in (("qkv", dqkv, 0), ("z", dz, C_Z), ("uv", duv, C_UV), ("ga", dga, C_GA), ("gb", dgb, C_GB),
                             ("ba", dba, C_BA)):
            dwi = wgrad(a["x"], dp, col, dwi, f"wgrad_in{l}_{tag}")

        above = views([dwi, dwa, dwb, dwo, dwup0, dwup1, dwdn0, dwdn1])
        grads["conv_qkv"][l] = dcq
        grads["conv_ffn"][l] = jnp.concatenate([dcf0[0], dcf1[0], dcf0[1], dcf1[1]], axis=1)
        grads["a_log"][l] = da[0, HEADS:2 * HEADS]
        grads["dt_bias"][l] = ddtb[0, HEADS:2 * HEADS]
        grads["dn_norm_w"][l] = dnw[0]
        grads["sg_ln_g"][l] = dlng[0]
        grads["sg_ln_b"][l] = dlnb[0]
        grads["w_spatial"][l] = dws
        grads["b_spatial"][l] = dbs[:, :4].T
        grads["ln1_g"][l] = dg1[0]
        grads["ln1_b"][l] = db1[0]
        grads["ln2_g"][l] = dg2[0]
        grads["ln2_b"][l] = db2[0]
    grad_x = dy.reshape(x.shape)
    g_full = {n: jnp.stack(grads[n]) for n in small_names}

    theirs = exchange(carried_pair_swap(above), "reduce_pair")
    recv = exchange(carried_scatter(pair_sums(0, above, theirs), pieces, part_shapes), "reduce_chips")
    bufs = exchange(carried_join(chip_sums(0, recv, bufs), 0), "reduce_join")
    g_shard = {n: b.reshape(w[n].shape) for n, b in zip(big_names, bufs)}

    small = allsum_small(_pack([g_full[n] for n in small_names], 8), "reduce_small").reshape(-1)
    small_full = dict(zip(small_names, _unpack(small, [g_full[n].shape for n in small_names])))
    for n in conv_names:
        width = w[n].shape[2]
        g_shard[n] = lax.dynamic_slice_in_dim(small_full[n], chip * width, width, axis=2)
    for n in REPL:
        g_shard[n] = small_full[n]

    delta, new_m, new_v = {}, {}, {}
    for n in big_names:
        shp = w[n].shape
        two_d = (shp[0] * shp[1], shp[2])
        g_, d_, m_, v_ = adam_call(w[n].reshape(two_d), g_shard[n].reshape(two_d), m[n].reshape(two_d), v[n].reshape(two_d), f"adam_{n}")
        g_shard[n], delta[n], new_m[n], new_v[n] = g_.reshape(shp), d_.reshape(shp), m_.reshape(shp), v_.reshape(shp)
    shapes = [w[n].shape for n in small_names]
    packs = [_pack([src[n] for n in small_names], 8) for src in (w, g_shard, m, v)]
    outs = adam_call(*packs, "adam_small")
    for dst, o in zip((delta, new_m, new_v), outs[1:]):
        dst.update(zip(small_names, _unpack(o.reshape(-1), shapes)))

    return (loss, grad_x, *[g_shard[n] for n in names], *[delta[n] for n in names],
            *[new_m[n] for n in names], *[new_v[n] for n in names])
```

```python
import functools
import math
from typing import Callable, NamedTuple

import jax
import jax.numpy as jnp
from jax import lax
from jax.experimental import pallas as pl
from jax.experimental.pallas import tpu as pltpu

F32 = jnp.float32
BF16 = jnp.bfloat16
HI = lax.Precision.HIGHEST
MID = lax.Precision.HIGH
MESH = pl.DeviceIdType.MESH

D = 1024
DEPTH = 2
HEADS = 4
DK = 128
CHUNK = 64
QKV_W = 1536
Z_W = 512
SG_W = 512
FFN = 2816
FFN_HALF = FFN // 2
N_CHIPS = 4
DN_SHARD = FFN // N_CHIPS
LN_EPS = 1e-5
RMS_EPS = 1e-6
L2_EPS = 1e-6
ALPHA = (2 * DEPTH) ** 0.25
ADAM_LR, ADAM_B1, ADAM_B2, ADAM_EPS, ADAM_WD, ADAM_STEP = 0.001, 0.9, 0.999, 1e-08, 0.01, 10

HALO = 16
LANES = 128
IN_COLS_PAD = 5248
C_Z, C_UV, C_GA, C_GB, C_BA = 1536, 2048, 3072, 4096, 5120
VMEM_LIMIT = 56 * 1024 * 1024


def _params(n_grid=1):
    return pltpu.CompilerParams(dimension_semantics=("arbitrary",) * n_grid, vmem_limit_bytes=VMEM_LIMIT)


def _mm(a, b):
    return jnp.dot(a.astype(BF16), b.astype(BF16), preferred_element_type=F32)


def _mm_nt(a, b):
    return lax.dot_general(a.astype(BF16), b.astype(BF16), (((1,), (1,)), ((), ())), preferred_element_type=F32)


def _mm_tn(a, b):
    return lax.dot_general(a.astype(BF16), b.astype(BF16), (((0,), (0,)), ((), ())), preferred_element_type=F32)


def _bdot(a, b, prec=MID):
    return lax.dot_general(a, b, (((2,), (1,)), ((0,), (0,))), precision=prec, preferred_element_type=F32)


def _bdot_nt(a, b, prec=MID):
    return lax.dot_general(a, b, (((2,), (2,)), ((0,), (0,))), precision=prec, preferred_element_type=F32)


def _bf16_dot(a, b, contract):
    return lax.dot_general(a.astype(BF16), b.astype(BF16), (contract, ((0,), (0,))), preferred_element_type=F32)


@jax.custom_vjp
def _fdot(a, b):
    return _bf16_dot(a, b, ((2,), (1,)))


def _fdot_fwd(a, b):
    return _fdot(a, b), (a, b)


def _fdot_bwd(res, ct):
    a, b = res
    return _bf16_dot(ct, b, ((2,), (2,))), _bf16_dot(a, ct, ((1,), (1,)))


_fdot.defvjp(_fdot_fwd, _fdot_bwd)


@jax.custom_vjp
def _fdot_nt(a, b):
    return _bf16_dot(a, b, ((2,), (2,)))


def _fdot_nt_fwd(a, b):
    return _fdot_nt(a, b), (a, b)


def _fdot_nt_bwd(res, ct):
    a, b = res
    return _bf16_dot(ct, b, ((2,), (1,))), _bf16_dot(ct, a, ((1,), (1,)))


_fdot_nt.defvjp(_fdot_nt_fwd, _fdot_nt_bwd)


@jax.custom_vjp
def _fdot_tn(a, b):
    return _bf16_dot(a, b, ((1,), (1,)))


def _fdot_tn_fwd(a, b):
    return _fdot_tn(a, b), (a, b)


def _fdot_tn_bwd(res, ct):
    a, b = res
    return _bf16_dot(b, ct, ((2,), (2,))), _bf16_dot(a, ct, ((2,), (1,)))


_fdot_tn.defvjp(_fdot_tn_fwd, _fdot_tn_bwd)


def _stack(parts):
    return jnp.concatenate([p[None] for p in parts], axis=0)


def _ln(x, g, b):
    mu = jnp.mean(x, axis=-1, keepdims=True)
    xc = x - mu
    var = jnp.mean(xc * xc, axis=-1, keepdims=True)
    return xc * lax.rsqrt(var + LN_EPS) * g + b


def _shift_rows(x, s):
    s = s % x.shape[0]
    return x if s == 0 else pltpu.roll(x, s, 0)


@jax.custom_vjp
def _conv(xcat, w):
    k_taps = len(w)
    y = None
    for k in range(k_taps):
        t = _shift_rows(xcat, k_taps - 1 - k)[HALO:] * w[k]
        y = t if y is None else y + t
    return y


def _conv_fwd(xcat, w):
    return _conv(xcat, w), (xcat, w)


def _conv_bwd(res, dy):
    xcat, w = res
    k_taps = len(w)
    dyp = jnp.concatenate([jnp.zeros((HALO, dy.shape[1]), dy.dtype), dy], axis=0)
    dx = None
    dws = []
    for k in range(k_taps):
        s = k_taps - 1 - k
        t = _shift_rows(dyp, -s) * w[k]
        dx = t if dx is None else dx + t
        dws.append(jnp.sum(_shift_rows(xcat, s)[HALO:] * dy, axis=0, keepdims=True))
    return dx, tuple(dws)


_conv.defvjp(_conv_fwd, _conv_bwd)


@jax.custom_vjp
def _tri_inv(l):
    n = l.shape[-1]
    r = lax.broadcasted_iota(jnp.int32, (n, n), 0)
    c = lax.broadcasted_iota(jnp.int32, (n, n), 1)
    eye = (r == c).astype(F32)
    p = eye - l
    lp = l
    steps = int(math.log2(n)) - 1
    for i in range(steps):
        dot = _bdot if i < 2 else functools.partial(_bf16_dot, contract=((2,), (1,)))
        lp = dot(lp, lp)
        p = p + dot(p, lp)
    return p


def _tri_inv_fwd(l):
    t = _tri_inv(l)
    return t, t


def _tri_inv_bwd(t, dt):
    tt = jnp.swapaxes(t, 1, 2)
    return (-_bdot(tt, _bdot(dt, tt)),)


_tri_inv.defvjp(_tri_inv_fwd, _tri_inv_bwd)


def _dn_glue(qkvcat, z, ba, s_in, cw, a_row, dtb_row, nw_row):
    t_rows = z.shape[0]
    nc = t_rows // CHUNK
    nb = nc * HEADS

    qkv = jax.nn.silu(_conv(qkvcat, cw))

    def chunks(t, off):
        return _stack([t[n * CHUNK:(n + 1) * CHUNK, off + h * DK: off + (h + 1) * DK]
                       for n in range(nc) for h in range(HEADS)])

    q = chunks(qkv, 0)
    k = chunks(qkv, 512)
    v = chunks(qkv, 1024)
    q = q * lax.rsqrt(jnp.sum(q * q, axis=-1, keepdims=True) + L2_EPS) * (DK ** -0.5)
    k = k * lax.rsqrt(jnp.sum(k * k, axis=-1, keepdims=True) + L2_EPS)

    lane = lax.broadcasted_iota(jnp.int32, (LANES, HEADS * DK), 0)
    head_of_col = lax.broadcasted_iota(jnp.int32, (LANES, HEADS * DK), 1) // DK
    e_beta = (head_of_col == lane).astype(F32)
    e_g = (head_of_col + HEADS == lane).astype(F32)
    beta_l = jax.nn.sigmoid(ba)
    g_l = -jnp.exp(a_row) * jax.nn.softplus(ba + dtb_row)
    beta = chunks(jnp.dot(beta_l, e_beta, precision=MID, preferred_element_type=F32), 0)
    g = chunks(jnp.dot(g_l, e_g, precision=MID, preferred_element_type=F32), 0)

    r = lax.broadcasted_iota(jnp.int32, (CHUNK, CHUNK), 0)
    c = lax.broadcasted_iota(jnp.int32, (CHUNK, CHUNK), 1)
    causal = r >= c
    strict = r > c
    tril_b = jnp.broadcast_to(causal.astype(F32), (nb, CHUNK, CHUNK))
    gi_b = _bdot(tril_b, g, HI)
    gi = gi_b[:, :, :CHUNK]
    gj = jnp.swapaxes(gi, 1, 2)
    decay = jnp.where(causal, jnp.exp(jnp.where(causal, gi - gj, 0.0)), 0.0)
    kb = k * beta
    l_mat = jnp.where(strict, _fdot_nt(kb, k) * decay, 0.0)
    t_mat = _tri_inv(l_mat)
    e_gi = jnp.exp(gi_b)
    w_mat = _fdot(t_mat, kb * e_gi)
    u_mat = _fdot(t_mat, v * beta)
    a_qk = _fdot_nt(q, k) * decay
    q_g = q * e_gi
    gl_b = jnp.broadcast_to(jnp.sum(g, axis=1, keepdims=True), g.shape)
    k_d = k * jnp.exp(gl_b - gi_b)
    e_gl = jnp.exp(gl_b)
    g_last = jnp.concatenate([e_gl, e_gl], axis=1)

    state = s_in
    rows = []
    for n in range(nc):
        sl = slice(n * HEADS, (n + 1) * HEADS)
        u_new = u_mat[sl] - _fdot(w_mat[sl], state)
        o_n = _fdot(q_g[sl], state) + _fdot(a_qk[sl], u_new)
        state = state * g_last[sl] + _fdot_tn(k_d[sl], u_new)
        o_n = o_n * lax.rsqrt(jnp.mean(o_n * o_n, axis=-1, keepdims=True) + RMS_EPS) * nw_row
        z_n = _stack([z[n * CHUNK:(n + 1) * CHUNK, h * DK:(h + 1) * DK] for h in range(HEADS)])
        o_n = o_n * jax.nn.silu(z_n)
        rows.append(jnp.concatenate([o_n[h] for h in range(HEADS)], axis=-1))
    return jnp.concatenate(rows, axis=0), state


def _sg_glue(uv, lng, lnb, w_s, bs_t):
    t_rows = uv.shape[0]
    y = jax.nn.gelu(uv)
    u = y[:, :SG_W]
    v = _ln(y[:, SG_W:], lng, lnb)
    r = lax.broadcasted_iota(jnp.int32, (LANES, LANES), 0)
    c = lax.broadcasted_iota(jnp.int32, (LANES, LANES), 1)
    wm = jnp.where(r >= c, w_s, 0.0)
    lane = lax.broadcasted_iota(jnp.int32, (LANES, SG_W), 0)
    group_of_col = lax.broadcasted_iota(jnp.int32, (LANES, SG_W), 1) // LANES
    e_grp = (group_of_col == lane).astype(F32)
    bias = jnp.dot(bs_t, e_grp, precision=HI, preferred_element_type=F32)
    outs = []
    for n in range(t_rows // LANES):
        vb = v[n * LANES:(n + 1) * LANES]
        vg = _stack([vb[:, g * LANES:(g + 1) * LANES] for g in range(4)])
        mg = _fdot(wm, vg)
        mixed = jnp.concatenate([mg[g] for g in range(4)], axis=-1) + bias
        outs.append(u[n * LANES:(n + 1) * LANES] * mixed)
    return jnp.concatenate(outs, axis=0)


def _merge_glue(ga, gb, ya, yb):
    return jax.nn.sigmoid(ga) * ya + jax.nn.sigmoid(gb) * yb


def _res_ln_glue(x, r, g, b):
    return _ln(ALPHA * x + r, g, b)


def _ffn_glue(ua, ub, cwa, cwb):
    return jax.nn.silu(_conv(ua, cwa)) * _conv(ub, cwb)


def _row(t, c, col=0):
    return pl.BlockSpec((t, c), lambda i: (i, col))


def _row_rev(t, c, nt, col=0):
    return pl.BlockSpec((t, c), lambda i: (nt - 1 - i, col))


def _halo(t, c, nt=None):
    per = t // HALO
    if nt is None:
        return pl.BlockSpec((HALO, c), lambda i: (jnp.maximum(i * per - 1, 0), 0))
    return pl.BlockSpec((HALO, c), lambda i: (jnp.maximum((nt - 1 - i) * per - 1, 0), 0))


def _full(shape):
    nd = len(shape)
    return pl.BlockSpec(shape, lambda i: (0,) * nd)


ANY = pl.BlockSpec(memory_space=pl.ANY)


def _sds(shape, dtype=F32):
    return jax.ShapeDtypeStruct(shape, dtype)


def _tile(s, want=256):
    for t in (want, 256, 128):
        if s % t == 0:
            return t
    raise ValueError(f"sequence length {s} is not a multiple of 128")


def proj_fwd(x, w, name, carry=None):
    s = x.shape[0]
    t = _tile(s)
    nt = s // t
    segs = [(0, 2048), (2048, 3072), (3072, 4096), (4096, 5120), (5120, IN_COLS_PAD)]

    def body(x_ref, w_ref, p_ref):
        xb = x_ref[...].astype(BF16)
        for lo, hi in segs:
            p_ref[:, lo:hi] = jnp.dot(xb, w_ref[:, lo:hi], preferred_element_type=F32)

    return _host_call(
        body, carry, nt, grid=(nt,), name=name, in_specs=[_row(t, D), _full((D, IN_COLS_PAD))],
        out_specs=[_row(t, IN_COLS_PAD)], out_shape=[_sds((s, IN_COLS_PAD))], scratch_shapes=[], operands=(x, w))


def dn_fwd(p, cq, a_row, dtb_row, nw_row, name, carry=None):
    s = p.shape[0]
    t = _tile(s)
    nt = s // t

    def body(qkv_ref, halo_ref, z_ref, ba_ref, cq_ref, a_ref, dtb_ref, nw_ref, o_ref, sst_ref, s_scr):
        i = pl.program_id(0)

        @pl.when(i == 0)
        def _():
            s_scr[...] = jnp.zeros_like(s_scr)

        halo = jnp.where(i == 0, 0.0, halo_ref[...])
        qkvcat = jnp.concatenate([halo, qkv_ref[...]], axis=0)
        cw = tuple(cq_ref[k:k + 1, :] for k in range(4))
        s_in = s_scr[...]
        sst_ref[0] = s_in
        o, s_out = _dn_glue(qkvcat, z_ref[...], ba_ref[...], s_in, cw, a_ref[...], dtb_ref[...], nw_ref[...])
        o_ref[...] = o.astype(BF16)
        s_scr[...] = s_out

    return _host_call(
        body, carry, nt, grid=(nt,), name=name,
        in_specs=[_row(t, QKV_W), _halo(t, QKV_W), _row(t, Z_W, C_Z // Z_W), _row(t, LANES, C_BA // LANES),
                  _full((4, QKV_W)), _full((1, LANES)), _full((1, LANES)), _full((1, LANES))],
        out_specs=[_row(t, Z_W), pl.BlockSpec((1, HEADS, DK, DK), lambda i: (i, 0, 0, 0))],
        out_shape=[_sds((s, Z_W), BF16), _sds((nt, HEADS, DK, DK))],
        scratch_shapes=[pltpu.VMEM((HEADS, DK, DK), F32)], operands=(p, p, p, p, cq, a_row, dtb_row, nw_row))


def sg_fwd(p, lng, lnb, w_s, bs_t, name):
    s = p.shape[0]
    t = _tile(s, 512)

    def body(uv_ref, lng_ref, lnb_ref, ws_ref, bs_ref, o_ref):
        o_ref[...] = _sg_glue(uv_ref[...], lng_ref[...], lnb_ref[...], ws_ref[...], bs_ref[...]).astype(BF16)

    return pl.pallas_call(
        body, grid=(s // t,), name=name,
        in_specs=[_row(t, 2 * SG_W, C_UV // (2 * SG_W)), _full((1, SG_W)), _full((1, SG_W)),
                  _full((4, LANES, LANES)), _full((LANES, LANES))],
        out_specs=_row(t, SG_W), out_shape=_sds((s, SG_W), BF16), compiler_params=_params())(p, lng, lnb, w_s, bs_t)


def merge_fwd(oa, ob, p, x, wa, wb, wo, g1, b1, name):
    s = x.shape[0]
    t = _tile(s, 512)

    def body(oa_ref, ob_ref, ga_ref, gb_ref, x_ref, wa_ref, wb_ref, wo_ref, g_ref, b_ref, x1_ref):
        ya = _mm(oa_ref[...], wa_ref[...])
        yb = _mm(ob_ref[...], wb_ref[...])
        h = _merge_glue(ga_ref[...], gb_ref[...], ya, yb)
        x1_ref[...] = _res_ln_glue(x_ref[...], _mm(h, wo_ref[...]), g_ref[...], b_ref[...])

    return pl.pallas_call(
        body, grid=(s // t,), name=name,
        in_specs=[_row(t, Z_W), _row(t, SG_W), _row(t, D, C_GA // D), _row(t, D, C_GB // D), _row(t, D),
                  _full((Z_W, D)), _full((SG_W, D)), _full((D, D)), _full((1, D)), _full((1, D))],
        out_specs=_row(t, D), out_shape=_sds((s, D)), compiler_params=_params())(oa, ob, p, p, x, wa, wb, wo, g1, b1)


def _load_ffn_weights(wup_hbm, wdn_hbm, wup_v, wdn_v, up_slots, dn_slots):
    for n, k in enumerate(up_slots):
        pltpu.sync_copy(wup_hbm.at[k], wup_v.at[n])
    for n, k in enumerate(dn_slots):
        pltpu.sync_copy(wdn_hbm.at[k], wdn_v.at[pl.ds(n * DN_SHARD, DN_SHARD)])


def ffn_fwd(x1, wup4, cf4, wdn4, g2, b2, name, carry=None):
    s = x1.shape[0]
    t = _tile(s, 512)
    nt = s // t

    def body(x1_ref, halo_ref, wup_hbm, cf_ref, wdn_hbm, g_ref, b_ref, pre_ref, x2_ref, wup_v, wdn_v):
        i = pl.program_id(0)

        @pl.when(i == 0)
        def _():
            _load_ffn_weights(wup_hbm, wdn_hbm, wup_v, wdn_v, range(4), range(4))

        x1v = x1_ref[...]
        halo = jnp.where(i == 0, 0.0, halo_ref[...])
        x1cat = jnp.concatenate([halo, x1v], axis=0).astype(BF16)
        f = None
        for h in range(2):
            ua = jnp.dot(x1cat, wup_v[h], preferred_element_type=F32)
            ub = jnp.dot(x1cat, wup_v[2 + h], preferred_element_type=F32)
            cwa = tuple(cf_ref[h, k:k + 1, :] for k in range(3))
            cwb = tuple(cf_ref[2 + h, k:k + 1, :] for k in range(3))
            act = _ffn_glue(ua, ub, cwa, cwb)
            fh = _mm(act, wdn_v[h * FFN_HALF:(h + 1) * FFN_HALF, :])
            f = fh if f is None else f + fh
        pre = ALPHA * x1v + f
        pre_ref[...] = pre
        x2_ref[...] = _ln(pre, g_ref[...], b_ref[...])

    return _host_call(
        body, carry, nt, grid=(nt,), name=name,
        in_specs=[_row(t, D), _halo(t, D), ANY, _full((4, 3, FFN_HALF)), ANY, _full((1, D)), _full((1, D))],
        out_specs=[_row(t, D), _row(t, D)], out_shape=[_sds((s, D)), _sds((s, D))],
        scratch_shapes=[pltpu.VMEM((4, D, FFN_HALF), BF16), pltpu.VMEM((FFN, D), BF16)],
        operands=(x1, x1, wup4, cf4, wdn4, g2, b2))


def loss_ln_bwd(pre, tgt, g, b, name):
    s = pre.shape[0]
    t = _tile(s, 512)

    def body(pre_ref, t_ref, g_ref, b_ref, dpre_ref, dg_ref, db_ref, loss_ref):
        first = pl.program_id(0) == 0
        y, vjp = jax.vjp(_ln, pre_ref[...], g_ref[...], b_ref[...])
        e = y - t_ref[...]
        dpre, dg, db = vjp(e * (1.0 / D))
        dpre_ref[...] = dpre
        _acc(dg_ref, dg, first)
        _acc(db_ref, db, first)
        part = jnp.sum(jnp.sum(e * e, axis=1, keepdims=True), axis=0, keepdims=True) * (0.5 / D)
        _acc(loss_ref, jnp.broadcast_to(part, loss_ref.shape), first)

    return pl.pallas_call(
        body, grid=(s // t,), name=name, in_specs=[_row(t, D), _row(t, D), _full((1, D)), _full((1, D))],
        out_specs=[_row(t, D), _full((1, D)), _full((1, D)), _full((8, LANES))],
        out_shape=[_sds((s, D)), _sds((1, D)), _sds((1, D)), _sds((8, LANES))], compiler_params=_params())(pre, tgt, g, b)


def _acc(ref, val, first):
    @pl.when(first)
    def _():
        ref[...] = val

    @pl.when(jnp.logical_not(first))
    def _():
        ref[...] += val


def _acc_tn(acc_ref, a, b, first, seg):
    n = b.shape[1]
    for lo in range(0, n, seg):
        hi = min(lo + seg, n)
        _acc(acc_ref.at[:, lo:hi], _mm_tn(a, b[:, lo:hi]), first)


def ln_bwd(pre, dy, g, b, name):
    s = pre.shape[0]
    t = _tile(s, 512)

    def body(pre_ref, dy_ref, g_ref, b_ref, dpre_ref, dg_ref, db_ref):
        _, vjp = jax.vjp(_ln, pre_ref[...], g_ref[...], b_ref[...])
        dpre, dg, db = vjp(dy_ref[...])
        dpre_ref[...] = dpre
        first = pl.program_id(0) == 0
        _acc(dg_ref, dg, first)
        _acc(db_ref, db, first)

    return pl.pallas_call(
        body, grid=(s // t,), name=name, in_specs=[_row(t, D), _row(t, D), _full((1, D)), _full((1, D))],
        out_specs=[_row(t, D), _full((1, D)), _full((1, D))],
        out_shape=[_sds((s, D)), _sds((1, D)), _sds((1, D))], compiler_params=_params())(pre, dy, g, b)


def ffn_bwd(x1, df, acc_in, acc_scale, wup4, cf4, wdn4, h, name, carry=None):
    s = x1.shape[0]
    t = _tile(s)
    nt = s // t

    def body(x1_ref, halo_ref, df_ref, acc_ref, wup_hbm, cf_ref, wdn_hbm,
             dx1_ref, dwup_hbm, dcf_ref, dwdn_hbm, wup_v, wdn_v, dwup_v, dwdn_v, carry):
        i = pl.program_id(0)
        j = nt - 1 - i
        first = i == 0

        @pl.when(first)
        def _():
            _load_ffn_weights(wup_hbm, wdn_hbm, wup_v, wdn_v, (h, 2 + h), (2 * h, 2 * h + 1))
            carry[...] = jnp.zeros_like(carry)

        halo = jnp.where(j == 0, 0.0, halo_ref[...])
        x1cat = jnp.concatenate([halo, x1_ref[...]], axis=0).astype(BF16)
        ua = jnp.dot(x1cat, wup_v[0], preferred_element_type=F32)
        ub = jnp.dot(x1cat, wup_v[1], preferred_element_type=F32)
        cwa = tuple(cf_ref[h, k:k + 1, :] for k in range(3))
        cwb = tuple(cf_ref[2 + h, k:k + 1, :] for k in range(3))
        act, vjp = jax.vjp(_ffn_glue, ua, ub, cwa, cwb)
        dfb = df_ref[...].astype(BF16)
        dact = _mm_nt(dfb, wdn_v[...])
        _acc_tn(dwdn_v, act.astype(BF16), dfb, first, 512)
        dua, dub, dcwa, dcwb = vjp(dact)
        x1b = x1cat[HALO:]
        dups = []
        for n, du in enumerate((dua, dub)):
            dups.append(jnp.concatenate([du[HALO:t], du[t:] + carry[n]], axis=0).astype(BF16))
            carry[n] = du[:HALO]
            _acc(dwup_v.at[n], _mm_tn(x1b, dups[n]), first)
        for k in range(3):
            _acc(dcf_ref.at[0, k:k + 1, :], dcwa[k], first)
            _acc(dcf_ref.at[1, k:k + 1, :], dcwb[k], first)
        dx1_ref[...] = acc_scale * acc_ref[...] + _mm_nt(dups[0], wup_v[0]) + _mm_nt(dups[1], wup_v[1])

        @pl.when(i == nt - 1)
        def _():
            pltpu.sync_copy(dwup_v, dwup_hbm)
            pltpu.sync_copy(dwdn_v, dwdn_hbm)

    return _host_call(
        body, carry, nt, grid=(nt,), name=name,
        in_specs=[_row_rev(t, D, nt), _halo(t, D, nt), _row_rev(t, D, nt), _row_rev(t, D, nt),
                  ANY, _full((4, 3, FFN_HALF)), ANY],
        out_specs=[_row_rev(t, D, nt), ANY, _full((2, 3, FFN_HALF)), ANY],
        out_shape=[_sds((s, D)), _sds((2, D, FFN_HALF)), _sds((2, 3, FFN_HALF)), _sds((FFN_HALF, D))],
        scratch_shapes=[pltpu.VMEM((2, D, FFN_HALF), BF16), pltpu.VMEM((FFN_HALF, D), BF16),
                        pltpu.VMEM((2, D, FFN_HALF), F32), pltpu.VMEM((FFN_HALF, D), F32),
                        pltpu.VMEM((2, HALO, FFN_HALF), F32)],
        operands=(x1, x1, df, acc_in, wup4, cf4, wdn4))


def merge_bwd(oa, ob, p, x, dx1, wa, wb, wo, g1, b1, name, carry=None):
    s = x.shape[0]
    t = _tile(s)

    def body(oa_ref, ob_ref, ga_ref, gb_ref, x_ref, dx1_ref, wa_ref, wb_ref, wo_ref, g_ref, b_ref,
             doa_ref, dob_ref, dga_ref, dgb_ref, dx_ref, dwa_ref, dwb_ref, dwo_ref, dg_ref, db_ref):
        first = pl.program_id(0) == 0
        oa = oa_ref[...]
        ob = ob_ref[...]
        ya = _mm(oa, wa_ref[...])
        yb = _mm(ob, wb_ref[...])
        h, vjp1 = jax.vjp(_merge_glue, ga_ref[...], gb_ref[...], ya, yb)
        hb = h.astype(BF16)
        r = _mm(hb, wo_ref[...])
        _, vjp2 = jax.vjp(_res_ln_glue, x_ref[...], r, g_ref[...], b_ref[...])
        dx, dr, dg, db = vjp2(dx1_ref[...])
        dx_ref[...] = dx
        _acc(dg_ref, dg, first)
        _acc(db_ref, db, first)
        drb = dr.astype(BF16)
        dh = _mm_nt(drb, wo_ref[...])
        _acc(dwo_ref, _mm_tn(hb, drb), first)
        dga, dgb, dya, dyb = vjp1(dh)
        dga_ref[...] = dga.astype(BF16)
        dgb_ref[...] = dgb.astype(BF16)
        dyab = dya.astype(BF16)
        dybb = dyb.astype(BF16)
        doa_ref[...] = _mm_nt(dyab, wa_ref[...]).astype(BF16)
        dob_ref[...] = _mm_nt(dybb, wb_ref[...]).astype(BF16)
        _acc(dwa_ref, _mm_tn(oa, dyab), first)
        _acc(dwb_ref, _mm_tn(ob, dybb), first)

    return _host_call(
        body, carry, s // t, grid=(s // t,), name=name,
        in_specs=[_row(t, Z_W), _row(t, SG_W), _row(t, D, C_GA // D), _row(t, D, C_GB // D), _row(t, D), _row(t, D),
                  _full((Z_W, D)), _full((SG_W, D)), _full((D, D)), _full((1, D)), _full((1, D))],
        out_specs=[_row(t, Z_W), _row(t, SG_W), _row(t, D), _row(t, D), _row(t, D),
                   _full((Z_W, D)), _full((SG_W, D)), _full((D, D)), _full((1, D)), _full((1, D))],
        out_shape=[_sds((s, Z_W), BF16), _sds((s, SG_W), BF16), _sds((s, D), BF16), _sds((s, D), BF16), _sds((s, D)),
                   _sds((Z_W, D)), _sds((SG_W, D)), _sds((D, D)), _sds((1, D)), _sds((1, D))],
        scratch_shapes=[], operands=(oa, ob, p, p, x, dx1, wa, wb, wo, g1, b1))


def sg_bwd(p, dob, lng, lnb, w_s, bs_t, name, carry=None):
    s = p.shape[0]
    t = _tile(s, 512)

    def body(uv_ref, dob_ref, lng_ref, lnb_ref, ws_ref, bs_ref, duv_ref, dlng_ref, dlnb_ref, dws_ref, dbs_ref):
        first = pl.program_id(0) == 0
        _, vjp = jax.vjp(_sg_glue, uv_ref[...], lng_ref[...], lnb_ref[...], ws_ref[...], bs_ref[...])
        duv, dlng, dlnb, dws, dbs = vjp(dob_ref[...].astype(F32))
        duv_ref[...] = duv.astype(BF16)
        _acc(dlng_ref, dlng, first)
        _acc(dlnb_ref, dlnb, first)
        _acc(dws_ref, dws, first)
        _acc(dbs_ref, dbs, first)

    return _host_call(
        body, carry, s // t, grid=(s // t,), name=name,
        in_specs=[_row(t, 2 * SG_W, C_UV // (2 * SG_W)), _row(t, SG_W), _full((1, SG_W)), _full((1, SG_W)),
                  _full((4, LANES, LANES)), _full((LANES, LANES))],
        out_specs=[_row(t, 2 * SG_W), _full((1, SG_W)), _full((1, SG_W)), _full((4, LANES, LANES)), _full((LANES, LANES))],
        out_shape=[_sds((s, 2 * SG_W), BF16), _sds((1, SG_W)), _sds((1, SG_W)), _sds((4, LANES, LANES)), _sds((LANES, LANES))],
        scratch_shapes=[], operands=(p, dob, lng, lnb, w_s, bs_t))


def dn_bwd(p, sst, doa, cq, a_row, dtb_row, nw_row, name, carry=None):
    s = p.shape[0]
    t = _tile(s)
    nt = s // t

    def body(qkv_ref, halo_ref, z_ref, ba_ref, sst_ref, doa_ref, cq_ref, a_ref, dtb_ref, nw_ref,
             dqkv_ref, dz_ref, dba_ref, dcq_ref, da_ref, ddtb_ref, dnw_ref, ds_scr, carry):
        i = pl.program_id(0)
        j = nt - 1 - i
        first = i == 0

        @pl.when(first)
        def _():
            ds_scr[...] = jnp.zeros_like(ds_scr)
            carry[...] = jnp.zeros_like(carry)

        halo = jnp.where(j == 0, 0.0, halo_ref[...])
        qkvcat = jnp.concatenate([halo, qkv_ref[...]], axis=0)
        cw = tuple(cq_ref[k:k + 1, :] for k in range(4))
        _, vjp = jax.vjp(_dn_glue, qkvcat, z_ref[...], ba_ref[...], sst_ref[0], cw, a_ref[...], dtb_ref[...], nw_ref[...])
        dqkvcat, dz, dba, ds_in, dcw, da, ddtb, dnw = vjp((doa_ref[...].astype(F32), ds_scr[...]))
        ds_scr[...] = ds_in
        dz_ref[...] = dz.astype(BF16)
        dba_ref[...] = dba.astype(BF16)
        dtile = dqkvcat[HALO:]
        dqkv_ref[...] = dtile.astype(BF16)
        dqkv_ref[t - HALO:t, :] = (dtile[t - HALO:] + carry[...]).astype(BF16)
        carry[...] = dqkvcat[:HALO]
        for k in range(4):
            _acc(dcq_ref.at[k:k + 1, :], dcw[k], first)
        _acc(da_ref, da, first)
        _acc(ddtb_ref, ddtb, first)
        _acc(dnw_ref, dnw, first)

    return _host_call(
        body, carry, nt, grid=(nt,), name=name,
        in_specs=[_row_rev(t, QKV_W, nt), _halo(t, QKV_W, nt), _row_rev(t, Z_W, nt, C_Z // Z_W),
                  _row_rev(t, LANES, nt, C_BA // LANES),
                  pl.BlockSpec((1, HEADS, DK, DK), lambda i: (nt - 1 - i, 0, 0, 0)), _row_rev(t, Z_W, nt),
                  _full((4, QKV_W)), _full((1, LANES)), _full((1, LANES)), _full((1, LANES))],
        out_specs=[_row_rev(t, QKV_W, nt), _row_rev(t, Z_W, nt), _row_rev(t, LANES, nt),
                   _full((4, QKV_W)), _full((1, LANES)), _full((1, LANES)), _full((1, LANES))],
        out_shape=[_sds((s, QKV_W), BF16), _sds((s, Z_W), BF16), _sds((s, LANES), BF16),
                   _sds((4, QKV_W)), _sds((1, LANES)), _sds((1, LANES)), _sds((1, LANES))],
        scratch_shapes=[pltpu.VMEM((HEADS, DK, DK), F32), pltpu.VMEM((HALO, QKV_W), F32)],
        operands=(p, p, p, p, sst, doa, cq, a_row, dtb_row, nw_row))


def proj_bwd(dps, dxd, w, name, carry=None):
    s = dxd.shape[0]
    t = _tile(s, 512)
    n = len(dps)

    def body(*refs):
        dp_refs, dxd_ref, w_hbm, dx_ref, w_v = refs[:n], refs[n], refs[n + 1], refs[n + 2], refs[n + 3]

        @pl.when(pl.program_id(0) == 0)
        def _():
            pltpu.sync_copy(w_hbm, w_v)

        dp = jnp.concatenate([r[...] for r in dp_refs], axis=1)
        dx_ref[...] = dxd_ref[...] + _mm_nt(dp, w_v[...])

    return _host_call(
        body, carry, s // t, grid=(s // t,), name=name,
        in_specs=[_row(t, dp.shape[1]) for dp in dps] + [_row(t, D), ANY],
        out_specs=[_row(t, D)], out_shape=[_sds((s, D))],
        scratch_shapes=[pltpu.VMEM((D, IN_COLS_PAD), BF16)], operands=(*dps, dxd, w))


def wgrad(x, dp, col, into, name):
    s, n = dp.shape
    tk = _tile(s, 1024)
    tn = next(c for c in (1024, 768, 512, 256, 128) if n % c == 0 and col % c == 0)
    block = col // tn

    def body(x_ref, dp_ref, *rest):
        o_ref = rest[-1]
        _acc(o_ref, _mm_tn(x_ref[...], dp_ref[...]), pl.program_id(1) == 0)

    operands = (x, dp) if into is None else (x, dp, into)
    return pl.pallas_call(
        body, grid=(n // tn, s // tk), name=name,
        in_specs=[pl.BlockSpec((tk, D), lambda j, k: (k, 0)), pl.BlockSpec((tk, tn), lambda j, k: (k, j))]
        + ([] if into is None else [ANY]),
        out_specs=pl.BlockSpec((D, tn), lambda j, k: (0, block + j)), out_shape=_sds((D, IN_COLS_PAD)),
        input_output_aliases={} if into is None else {2: 0},
        compiler_params=_params(2))(*operands)


def _rows_block(rows, cols):
    cap = max(HALO, (2 * 1024 * 1024) // (cols * 4))
    for cand in range(min(rows, cap) // HALO * HALO, HALO - 1, -HALO):
        if rows % cand == 0:
            return cand
    return rows


def adam_call(w, g, m, v, name):
    rows, cols = w.shape
    tr = _rows_block(rows, cols)
    c1 = 1.0 - ADAM_B1 ** ADAM_STEP
    c2 = 1.0 - ADAM_B2 ** ADAM_STEP

    def body(w_ref, g_ref, m_ref, v_ref, go_ref, d_ref, nm_ref, nv_ref):
        gv = g_ref[...]
        go_ref[...] = gv
        nm = ADAM_B1 * m_ref[...] + (1.0 - ADAM_B1) * gv
        nv = ADAM_B2 * v_ref[...] + (1.0 - ADAM_B2) * (gv * gv)
        d_ref[...] = -ADAM_LR * ((nm / c1) / (jnp.sqrt(nv / c2) + ADAM_EPS) + ADAM_WD * w_ref[...])
        nm_ref[...] = nm
        nv_ref[...] = nv

    spec = pl.BlockSpec((tr, cols), lambda i: (i, 0))
    return pl.pallas_call(
        body, grid=(rows // tr,), name=name, in_specs=[spec] * 4, out_specs=[spec] * 4,
        out_shape=[_sds((rows, cols))] * 4, compiler_params=_params())(w, g, m, v)


def _place():
    return lax.axis_index("x"), lax.axis_index("y"), lax.axis_index("c")


def _other_chips(x, y):
    return [(1 - x, y), (x, 1 - y), (1 - x, 1 - y)]


def _remote(src, dst, send_sem, recv_sem, to):
    return pltpu.make_async_remote_copy(src_ref=src, dst_ref=dst, send_sem=send_sem, recv_sem=recv_sem,
                                        device_id=to, device_id_type=MESH)


class _Gather:
    def __init__(self, ins, outs, send_sems, recv_sems, local_sems):
        self.ins, self.outs, self.n = ins, outs, len(ins)
        self.send_sems, self.recv_sems, self.local_sems = send_sems, recv_sems, local_sems
        self.x, self.y, self.c = _place()
        self.me = 2 * self.x + self.y
        self.chips = _other_chips(self.x, self.y)

    def _copy(self, t, k, slot, part, to, src=None):
        dst = self.outs[t].at[slot, part]
        return _remote(dst if src is None else src, dst, self.send_sems.at[6 * t + k], self.recv_sems.at[6 * t + k], to)

    def _mine(self):
        return [pltpu.make_async_copy(self.ins[t].at[p], self.outs[t].at[self.me, p], self.local_sems.at[2 * t + p])
                for t in range(self.n) for p in range(2)]

    def _first(self):
        return [self._copy(t, k, self.me, self.c, (cx, cy, self.c), src=self.ins[t].at[self.c])
                for k, (cx, cy) in enumerate(self.chips) for t in range(self.n)]

    def start(self):
        for cp in self._mine() + self._first():
            cp.start()

    def finish(self):
        x, y, c = self.x, self.y, self.c
        passed = []
        for k, (cx, cy) in enumerate(self.chips):
            for t in range(self.n):
                self._copy(t, k, 2 * cx + cy, c, (x, y, c)).wait_recv()
                passed.append(self._copy(t, 3 + k, 2 * cx + cy, c, (x, y, 1 - c)))
                passed[-1].start()
        for k, (cx, cy) in enumerate(self.chips):
            for t in range(self.n):
                self._copy(t, 3 + k, 2 * cx + cy, 1 - c, (x, y, c)).wait_recv()
        for cp in self._first() + passed:
            cp.wait_send()
        for cp in self._mine():
            cp.wait()


class _Carried(NamedTuple):
    ins: tuple
    out_shapes: tuple
    scratch: tuple
    make: Callable
    aliases: dict


def _host_call(body, carry, steps, *, grid, name, in_specs, out_specs, out_shape, scratch_shapes, operands):
    in_specs, out_specs, out_shape, scratch_shapes = list(in_specs), list(out_specs), list(out_shape), list(scratch_shapes)
    aliases = {}
    if carry is not None:
        n_in, n_out, n_scr = len(in_specs), len(out_specs), len(scratch_shapes)
        n_ci, n_co = len(carry.ins), len(carry.out_shapes)
        plain = body

        def body(*refs):
            ins, cins = refs[:n_in], refs[n_in:n_in + n_ci]
            outs = refs[n_in + n_ci:n_in + n_ci + n_out]
            couts = refs[n_in + n_ci + n_out:n_in + n_ci + n_out + n_co]
            rest = refs[n_in + n_ci + n_out + n_co:]
            exchange = carry.make(cins, couts, *rest[n_scr:])
            pl.when(pl.program_id(0) == 0)(exchange.start)
            plain(*ins, *outs, *rest[:n_scr])
            pl.when(pl.program_id(0) == steps - 1)(exchange.finish)

        aliases = {n_in + i: n_out + j for i, j in carry.aliases.items()}
        in_specs += [ANY] * n_ci
        out_specs += [ANY] * n_co
        out_shape += list(carry.out_shapes)
        scratch_shapes += list(carry.scratch)
        operands = tuple(operands) + tuple(carry.ins)
    return pl.pallas_call(
        body, grid=grid, name=name, in_specs=in_specs, out_specs=out_specs, out_shape=out_shape,
        scratch_shapes=scratch_shapes, input_output_aliases=aliases, compiler_params=_params(len(grid)))(*operands)


def exchange(carry, name):
    n_i, n_o = len(carry.ins), len(carry.out_shapes)

    def body(*refs):
        ex = carry.make(refs[:n_i], refs[n_i:n_i + n_o], *refs[n_i + n_o:])
        ex.start()
        ex.finish()

    return pl.pallas_call(
        body, name=name, in_specs=[ANY] * n_i, out_specs=[ANY] * n_o, out_shape=list(carry.out_shapes),
        scratch_shapes=list(carry.scratch), input_output_aliases=dict(carry.aliases))(*carry.ins)


def _dma_sems(*counts):
    return tuple(pltpu.SemaphoreType.DMA((n,)) for n in counts)


def carried_gather(shards):
    n = len(shards)
    return _Carried(tuple(shards), tuple(_sds((N_CHIPS,) + a.shape, a.dtype) for a in shards),
                    _dma_sems(6 * n, 6 * n, 2 * n), _Gather, {})


class _PairSwap:
    def __init__(self, ins, outs, send_sems, recv_sems):
        x, y, c = _place()
        self.copies = [_remote(ins[t].at[:, 1 - c], outs[t], send_sems.at[t], recv_sems.at[t], (x, y, 1 - c))
                       for t in range(len(ins))]

    def start(self):
        for cp in self.copies:
            cp.start()

    def finish(self):
        for cp in self.copies:
            cp.wait()


def carried_pair_swap(views):
    n = len(views)
    return _Carried(tuple(views), tuple(_sds((v.shape[0],) + v.shape[2:]) for v in views), _dma_sems(n, n), _PairSwap, {})


class _Scatter:
    def __init__(self, srcs, outs, send_sems, recv_sems, local_sems, pieces):
        self.srcs, self.outs, self.pieces, self.n = srcs, outs, pieces, len(pieces)
        self.send_sems, self.recv_sems, self.local_sems = send_sems, recv_sems, local_sems
        self.x, self.y, self.c = _place()
        self.me = 2 * self.x + self.y

    def _piece(self, t, k):
        idx, lead, cols = self.pieces[t][k]
        ref = self.srcs[idx].at[lead]
        return ref if cols is None else ref.at[:, pl.ds(cols[0], cols[1])]

    def _local(self, t, k):
        return pltpu.make_async_copy(self._piece(t, k), self.outs[t].at[k], self.local_sems.at[t])

    def _each_chip(self, mine, others):
        for k in range(N_CHIPS):
            pl.when(self.me == k)(functools.partial(mine, k))
            pl.when(self.me != k)(functools.partial(others, k))

    def start(self):
        def mine(k):
            for t in range(self.n):
                self._local(t, k).start()

        def others(k):
            for t in range(self.n):
                _remote(self._piece(t, k), self.outs[t].at[self.me], self.send_sems.at[N_CHIPS * t + k],
                        self.recv_sems.at[N_CHIPS * t + self.me], (k // 2, k % 2, self.c)).start()

        self._each_chip(mine, others)

    def finish(self):
        def mine(k):
            for t in range(self.n):
                self._local(t, k).wait()

        def others(k):
            for t in range(self.n):
                cp = _remote(self._piece(t, k), self.outs[t].at[k], self.send_sems.at[N_CHIPS * t + k],
                             self.recv_sems.at[N_CHIPS * t + k], (self.x, self.y, self.c))
                cp.wait_recv()
                cp.wait_send()

        self._each_chip(mine, others)


def carried_scatter(srcs, pieces, part_shapes):
    n = len(pieces)
    return _Carried(tuple(srcs), tuple(_sds((N_CHIPS,) + tuple(shp), srcs[0].dtype) for shp in part_shapes),
                    _dma_sems(N_CHIPS * n, N_CHIPS * n, n), functools.partial(_Scatter, pieces=pieces), {})


class _PairJoin:
    def __init__(self, ins, outs, send_sems, recv_sems, layer):
        self.ins, self.outs, self.layer, self.n = ins, outs, layer, len(ins)
        self.send_sems, self.recv_sems = send_sems, recv_sems
        self.x, self.y, self.c = _place()

    def _copy(self, t, half, to):
        return _remote(self.ins[t].at[self.layer, self.c], self.outs[t].at[self.layer, half],
                       self.send_sems.at[t], self.recv_sems.at[t], to)

    def start(self):
        for t in range(self.n):
            self._copy(t, self.c, (self.x, self.y, 1 - self.c)).start()

    def finish(self):
        for t in range(self.n):
            self._copy(t, self.c, (self.x, self.y, 1 - self.c)).wait_send()
            self._copy(t, 1 - self.c, (self.x, self.y, self.c)).wait_recv()


def carried_join(bufs, layer):
    n = len(bufs)
    return _Carried(tuple(bufs), tuple(_sds(b.shape) for b in bufs), _dma_sems(n, n),
                    functools.partial(_PairJoin, layer=layer), {t: t for t in range(n)})


def pair_add_half(mine, theirs, c_vec, name):
    g, _, h, b = mine.shape
    tr = _rows_block(h, b)

    def body(c_ref, a_ref, b_ref, o_ref):
        o_ref[...] = (a_ref[...] + b_ref[...]).astype(BF16)

    part = pl.BlockSpec((None, tr, b), lambda j, i, c: (j, i, 0))
    grid_spec = pltpu.PrefetchScalarGridSpec(
        num_scalar_prefetch=1, grid=(g, h // tr),
        in_specs=[pl.BlockSpec((None, None, tr, b), lambda j, i, c: (j, c[0], i, 0)), part], out_specs=part)
    return pl.pallas_call(body, grid_spec=grid_spec, name=name, out_shape=_sds((g, h, b), BF16),
                          compiler_params=_params(2))(c_vec, mine, theirs)


def chips_add_into(recv, into, layer, c_vec, name):
    n, h, b = recv.shape
    tr = _rows_block(h, b)

    def body(c_ref, r0, r1, r2, r3, *rest):
        rest[-1][...] = ((r0[...].astype(F32) + r1[...].astype(F32)) + r2[...].astype(F32)) + r3[...].astype(F32)

    grid_spec = pltpu.PrefetchScalarGridSpec(
        num_scalar_prefetch=1, grid=(h // tr,),
        in_specs=[pl.BlockSpec((None, tr, b), lambda i, c, k=k: (k, i, 0)) for k in range(n)]
        + ([] if into is None else [ANY]),
        out_specs=pl.BlockSpec((None, None, tr, b), lambda i, c: (layer, c[0], i, 0)))
    return pl.pallas_call(
        body, grid_spec=grid_spec, name=name, out_shape=_sds((2, 2, h, b)),
        input_output_aliases={} if into is None else {1 + n: 0},
        compiler_params=_params())(c_vec, *([recv] * n), *(() if into is None else (into,)))


def allsum_small(v, name):
    rows, lanes = v.shape
    n_dev = 8

    def body(v_ref, out_ref, buf, send_sems, recv_sems):
        x, y, c = _place()
        me, sibling = (x, y, c), (x, y, 1 - c)
        chips = _other_chips(x, y)

        def slot(px, py, pc):
            return buf.at[4 * px + 2 * py + pc]

        def copy(k, block, to, src=None):
            return pltpu.make_async_remote_copy(
                src_ref=slot(*block) if src is None else src, dst_ref=slot(*block),
                send_sem=send_sems.at[k], recv_sem=recv_sems.at[k], device_id=to, device_id_type=MESH)

        slot(*me)[...] = v_ref[...]
        first = [copy(0, me, sibling, src=v_ref)]
        first += [copy(1 + k, me, (*chip, c), src=v_ref) for k, chip in enumerate(chips)]
        for cp in first:
            cp.start()
        passed = [copy(4 + k, (*chip, c), sibling) for k, chip in enumerate(chips)]
        for k, chip in enumerate(chips):
            copy(1 + k, (*chip, c), me).wait_recv()
            passed[k].start()
        copy(0, sibling, me).wait_recv()
        for k, chip in enumerate(chips):
            copy(4 + k, (*chip, 1 - c), me).wait_recv()
        for cp in first + passed:
            cp.wait_send()
        acc = buf[0]
        for d in range(1, n_dev):
            acc = acc + buf[d]
        out_ref[...] = acc

    vm = pl.BlockSpec(memory_space=pltpu.VMEM)
    return pl.pallas_call(
        body, name=name, in_specs=[vm], out_specs=vm, out_shape=_sds((rows, lanes)),
        scratch_shapes=[pltpu.VMEM((n_dev, rows, lanes), F32), pltpu.SemaphoreType.DMA((7,)), pltpu.SemaphoreType.DMA((7,))],
        compiler_params=pltpu.CompilerParams(vmem_limit_bytes=VMEM_LIMIT),
    )(v)


BIG = ("w_in", "w_branch_a", "w_branch_b", "w_out", "w_up", "w_down")
CONV = ("conv_qkv", "conv_ffn")
REPL =("a_log", "dt_bias", "dn_norm_w", "sg_ln_g", "sg_ln_b", "w_spatial", "b_spatial", "ln1_g", "ln1_b", "ln2_g", "ln2_b")


def _pad_rows(flat, mult):
    n = flat.shape[0]
    unit = mult * LANES
    total = -(-n // unit) * unit
    return jnp.pad(flat, (0, total - n)).reshape(total // LANES, LANES)


def _pack(arrs, mult):
    return _pad_rows(jnp.concatenate([a.reshape(-1) for a in arrs]), mult)


def _unpack(flat, shapes):
    out, off = [], 0
    for shp in shapes:
        n = math.prod(shp)
        out.append(flat[off:off + n].reshape(shp))
        off += n
    return out


def kernel(x, w_in, conv_qkv, a_log, dt_bias, dn_norm_w, w_branch_a, sg_ln_g, sg_ln_b, w_spatial, b_spatial, w_branch_b, w_out, ln1_g, ln1_b, w_up, conv_ffn, w_down, ln2_g, ln2_b, loss_target, m_w_in, m_conv_qkv, m_a_log, m_dt_bias, m_dn_norm_w, m_w_branch_a, m_sg_ln_g, m_sg_ln_b, m_w_spatial, m_b_spatial, m_w_branch_b, m_w_out, m_ln1_g, m_ln1_b, m_w_up, m_conv_ffn, m_w_down, m_ln2_g, m_ln2_b, v_w_in, v_conv_qkv, v_a_log, v_dt_bias, v_dn_norm_w, v_w_branch_a, v_sg_ln_g, v_sg_ln_b, v_w_spatial, v_b_spatial, v_w_branch_b, v_w_out, v_ln1_g, v_ln1_b, v_w_up, v_conv_ffn, v_w_down, v_ln2_g, v_ln2_b):
    names = ("w_in", "conv_qkv", "a_log", "dt_bias", "dn_norm_w", "w_branch_a", "sg_ln_g", "sg_ln_b", "w_spatial",
             "b_spatial", "w_branch_b", "w_out", "ln1_g", "ln1_b", "w_up", "conv_ffn", "w_down", "ln2_g", "ln2_b")
    w = dict(zip(names, (w_in, conv_qkv, a_log, dt_bias, dn_norm_w, w_branch_a, sg_ln_g, sg_ln_b, w_spatial,
                         b_spatial, w_branch_b, w_out, ln1_g, ln1_b, w_up, conv_ffn, w_down, ln2_g, ln2_b)))
    m = dict(zip(names, (m_w_in, m_conv_qkv, m_a_log, m_dt_bias, m_dn_norm_w, m_w_branch_a, m_sg_ln_g, m_sg_ln_b,
                         m_w_spatial, m_b_spatial, m_w_branch_b, m_w_out, m_ln1_g, m_ln1_b, m_w_up, m_conv_ffn,
                         m_w_down, m_ln2_g, m_ln2_b)))
    v = dict(zip(names, (v_w_in, v_conv_qkv, v_a_log, v_dt_bias, v_dn_norm_w, v_w_branch_a, v_sg_ln_g, v_sg_ln_b,
                         v_w_spatial, v_b_spatial, v_w_branch_b, v_w_out, v_ln1_g, v_ln1_b, v_w_up, v_conv_ffn,
                         v_w_down, v_ln2_g, v_ln2_b)))
    chip = 2 * lax.axis_index("x") + lax.axis_index("y")
    s = x.shape[1]
    xs = x.reshape(s, D)
    tgt = loss_target.reshape(s, D)

    big_names, conv_names = list(BIG), list(CONV)

    def in_two(name, l):
        rows, cols = w[name].shape[1:]
        return w[name][l].astype(BF16).reshape(2, rows // 2, cols)

    def whole(name, landed):
        rows, cols = w[name].shape[1:]
        return landed.reshape(N_CHIPS, rows, cols)

    first = exchange(carried_gather([in_two("w_in", 0)] + [w[n] for n in conv_names]), "gather_first")
    got = [{"w_in": whole("w_in", first[0])}, {}]
    conv_taps = dict(zip(conv_names, first[1:]))
    narrow, wide = ["w_branch_a", "w_branch_b", "w_out"], ["w_up", "w_down"]
    carried = {"proj_fwd0": (0, narrow), "dn_fwd0": (0, wide), "ffn_fwd0": (1, ["w_in"] + narrow), "dn_fwd1": (1, wide)}

    def carry(call):
        if call not in carried:
            return None
        l, which = carried[call]
        return carried_gather([in_two(n, l) for n in which])

    def land(call, landed):
        l, which = carried.get(call, (0, []))
        got[l].update({n: whole(n, a) for n, a in zip(which, landed)})

    def lane_row(vec, off):
        return jnp.zeros((1, LANES), F32).at[0, off:off + vec.shape[0]].set(vec)

    def side_by_side(blocks):
        return jnp.concatenate([blocks[k] for k in range(N_CHIPS)], axis=1)

    def small_params(l):
        return dict(
            cq=side_by_side(conv_taps["conv_qkv"][:, l]),
            a_row=lane_row(w["a_log"][l], HEADS), dtb_row=lane_row(w["dt_bias"][l], HEADS),
            nw_row=w["dn_norm_w"][l].reshape(1, DK),
            lng=w["sg_ln_g"][l].reshape(1, SG_W), lnb=w["sg_ln_b"][l].reshape(1, SG_W),
            w_s=w["w_spatial"][l], bs_t=jnp.zeros((LANES, LANES), F32).at[:, :4].set(w["b_spatial"][l].T),
            g1=w["ln1_g"][l].reshape(1, D), b1=w["ln1_b"][l].reshape(1, D),
            cf=conv_taps["conv_ffn"][:, l],
            g2=w["ln2_g"][l].reshape(1, D), b2=w["ln2_b"][l].reshape(1, D))

    layers, saved = [], []
    h_in = xs
    for l in range(DEPTH):
        p = small_params(l)
        wi = side_by_side(got[l]["w_in"])
        p["w_in"] = jnp.concatenate([wi[:, :2048], wi[:, 2056:3080], wi[:, 3080:5128], wi[:, 2048:2056],
                                     jnp.zeros((D, IN_COLS_PAD - 5128), BF16)], axis=1)
        proj, *landed = proj_fwd(h_in, p["w_in"], f"proj_fwd{l}", carry=carry(f"proj_fwd{l}"))
        land(f"proj_fwd{l}", landed)
        oa, sst, *landed = dn_fwd(proj, p["cq"], p["a_row"], p["dtb_row"], p["nw_row"], f"dn_fwd{l}",
                                  carry=carry(f"dn_fwd{l}"))
        land(f"dn_fwd{l}", landed)
        ob = sg_fwd(proj, p["lng"], p["lnb"], p["w_s"], p["bs_t"], f"sg_fwd{l}")
        p.update(wa=side_by_side(got[l]["w_branch_a"]), wb=side_by_side(got[l]["w_branch_b"]),
                 wo=got[l]["w_out"].reshape(D, D))
        x1 = merge_fwd(oa, ob, proj, h_in, p["wa"], p["wb"], p["wo"], p["g1"], p["b1"], f"merge_fwd{l}")
        pre2, x2, *landed = ffn_fwd(x1, got[l]["w_up"], p["cf"], got[l]["w_down"], p["g2"], p["b2"], f"ffn_fwd{l}",
                                    carry=carry(f"ffn_fwd{l}"))
        land(f"ffn_fwd{l}", landed)
        layers.append(p)
        saved.append(dict(x=h_in, proj=proj, oa=oa, ob=ob, sst=sst, x1=x1, pre2=pre2))
        h_in = x2


    small_names = conv_names + list(REPL)
    grads = {n: [None] * DEPTH for n in small_names}
    c_vec = jnp.stack([lax.axis_index("c")]).astype(jnp.int32)
    tags = ("w_in", "w_a", "w_b", "w_out", "w_up0", "w_up1", "w_dn0", "w_dn1")
    groups = (1, 1, 1, N_CHIPS, 2, 2, 2, 2)
    ab_cols = w["w_branch_a"].shape[2]
    pieces = [
        [(0, (k,), None) for k in range(N_CHIPS)],
        [(1, (0,), (k * ab_cols, ab_cols)) for k in range(N_CHIPS)],
        [(2, (0,), (k * ab_cols, ab_cols)) for k in range(N_CHIPS)],
        [(3, (k,), None) for k in range(N_CHIPS)],
        [(4 + k % 2, (k // 2,), None) for k in range(N_CHIPS)],
        [(6 + k // 2, (k % 2,), None) for k in range(N_CHIPS)],
    ]
    part_shapes = [(w[n].shape[1] // 2, w[n].shape[2]) for n in big_names]
    arrays_of = ((0,), (1,), (2,), (3,), (4, 5), (6, 7))
    rest, ffn_part = (0, 1, 2, 3), (4, 5)
    bufs = {}

    def arrays(which):
        return [i for t in which for i in arrays_of[t]]

    def views(which, arrs):
        return [a.reshape(groups[i], 2, a.size // a.shape[-1] // (2 * groups[i]), a.shape[-1])
                for i, a in zip(arrays(which), arrs)]

    def pair_sums(l, which, mine, theirs):
        ids = arrays(which)
        sums = [pair_add_half(m_, t_, c_vec, f"reduce_pair_add{l}_{tags[i]}") for i, m_, t_ in zip(ids, mine, theirs)]
        if ids[0] == 0:
            pin = sums[0][0]
            natural = jnp.concatenate([pin[:, :2048], pin[:, C_BA:C_BA + 8], pin[:, 2048:C_BA]], axis=1)
            sums[0] = jnp.stack(jnp.split(natural, N_CHIPS, axis=1))
        return sums

    def scatter_of(which, srcs):
        place = {i: j for j, i in enumerate(arrays(which))}
        return carried_scatter(srcs, [[(place[i], lead, cols) for i, lead, cols in pieces[t]] for t in which],
                               [part_shapes[t] for t in which])

    def chip_sums(l, which, recv):
        for t, r in zip(which, recv):
            n = big_names[t]
            bufs[n] = chips_add_into(r, bufs.get(n), l, c_vec, f"reduce_chips_add{l}_{n}")
        return [bufs[big_names[t]] for t in which]

    def keep(which, joined):
        bufs.update({big_names[t]: b for t, b in zip(which, joined)})

    above = None
    for l in reversed(range(DEPTH)):
        p, a = layers[l], saved[l]
        if l == DEPTH - 1:
            dpre2, dg2, db2, loss_part = loss_ln_bwd(a["pre2"], tgt, p["g2"], p["b2"], "loss_ln2_bwd")
            loss = lax.psum(loss_part[0, 0], ("x", "y", "c"))
        else:
            dpre2, dg2, db2 = ln_bwd(a["pre2"], dy, p["g2"], p["b2"], f"ln2_bwd{l}")
        dx1, dwup0, dcf0, dwdn0, *theirs = ffn_bwd(
            a["x1"], dpre2, dpre2, ALPHA, got[l]["w_up"], p["cf"], got[l]["w_down"], 0, f"ffn_bwd{l}a",
            carry=carried_pair_swap(above) if above else None)
        srcs = pair_sums(l + 1, rest, above, theirs) if above else None
        dx1, dwup1, dcf1, dwdn1, *recv = ffn_bwd(
            a["x1"], dpre2, dx1, 1.0, got[l]["w_up"], p["cf"], got[l]["w_down"], 1, f"ffn_bwd{l}b",
            carry=scatter_of(rest, srcs) if above else None)
        summed = chip_sums(l + 1, rest, recv) if above else None
        doa, dob, dga, dgb, dxd, dwa, dwb, dwo, dg1, db1, *joined = merge_bwd(
            a["oa"], a["ob"], a["proj"], a["x"], dx1, p["wa"], p["wb"], p["wo"], p["g1"], p["b1"], f"merge_bwd{l}",
            carry=carried_join(summed, l + 1) if above else None)
        keep(rest, joined)
        mine = views(ffn_part, [dwup0, dwup1, dwdn0, dwdn1])
        duv, dlng, dlnb, dws, dbs, *theirs = sg_bwd(a["proj"], dob, p["lng"], p["lnb"], p["w_s"], p["bs_t"], f"sg_bwd{l}",
                                                    carry=carried_pair_swap(mine))
        srcs = pair_sums(l, ffn_part, mine, theirs)
        dqkv, dz, dba, dcq, da, ddtb, dnw, *recv = dn_bwd(a["proj"], a["sst"], doa, p["cq"], p["a_row"], p["dtb_row"],
                                                          p["nw_row"], f"dn_bwd{l}", carry=scatter_of(ffn_part, srcs))
        summed = chip_sums(l, ffn_part, recv)
        dy, *joined = proj_bwd([dqkv, dz, duv, dga, dgb, dba], dxd, p["w_in"], f"proj_bwd{l}",
                               carry=carried_join(summed, l))
        keep(ffn_part, joined)
        dwi = None
        for tag, dp, col in (("qkv", dqkv, 0), ("z", dz, C_Z), ("uv", duv, C_UV), ("ga", dga, C_GA), ("gb", dgb, C_GB),
                             ("ba", dba, C_BA)):
            dwi = wgrad(a["x"], dp, col, dwi, f"wgrad_in{l}_{tag}")

        above = views(rest, [dwi, dwa, dwb, dwo])
        grads["conv_qkv"][l] = dcq
        grads["conv_ffn"][l] = jnp.concatenate([dcf0[0], dcf1[0], dcf0[1], dcf1[1]], axis=1)
        grads["a_log"][l] = da[0, HEADS:2 * HEADS]
        grads["dt_bias"][l] = ddtb[0, HEADS:2 * HEADS]
        grads["dn_norm_w"][l] = dnw[0]
        grads["sg_ln_g"][l] = dlng[0]
        grads["sg_ln_b"][l] = dlnb[0]
        grads["w_spatial"][l] = dws
        grads["b_spatial"][l] = dbs[:, :4].T
        grads["ln1_g"][l] = dg1[0]
        grads["ln1_b"][l] = db1[0]
        grads["ln2_g"][l] = dg2[0]
        grads["ln2_b"][l] = db2[0]
    grad_x = dy.reshape(x.shape)
    g_full = {n: jnp.stack(grads[n]) for n in small_names}

    theirs = exchange(carried_pair_swap(above), "reduce_pair")
    recv = exchange(scatter_of(rest, pair_sums(0, rest, above, theirs)), "reduce_chips")
    keep(rest, exchange(carried_join(chip_sums(0, rest, recv), 0), "reduce_join"))
    g_shard = {n: bufs[n].reshape(w[n].shape) for n in big_names}

    small = allsum_small(_pack([g_full[n] for n in small_names], 8), "reduce_small").reshape(-1)
    small_full = dict(zip(small_names, _unpack(small, [g_full[n].shape for n in small_names])))
    for n in conv_names:
        width = w[n].shape[2]
        g_shard[n] = lax.dynamic_slice_in_dim(small_full[n], chip * width, width, axis=2)
    for n in REPL:
        g_shard[n] = small_full[n]

    delta, new_m, new_v = {}, {}, {}
    for n in big_names:
        shp = w[n].shape
        two_d = (shp[0] * shp[1], shp[2])
        g_, d_, m_, v_ = adam_call(w[n].reshape(two_d), g_shard[n].reshape(two_d), m[n].reshape(two_d), v[n].reshape(two_d), f"adam_{n}")
        g_shard[n], delta[n], new_m[n], new_v[n] = g_.reshape(shp), d_.reshape(shp), m_.reshape(shp), v_.reshape(shp)
    shapes = [w[n].shape for n in small_names]
    packs = [_pack([src[n] for n in small_names], 8) for src in (w, g_shard, m, v)]
    outs = adam_call(*packs, "adam_small")
    for dst, o in zip((delta, new_m, new_v), outs[1:]):
        dst.update(zip(small_names, _unpack(o.reshape(-1), shapes)))

    return (loss, grad_x, *[g_shard[n] for n in names], *[delta[n] for n in names],
            *[new_m[n] for n in names], *[new_v[n] for n in names])
```

```python
import functools
import math
from typing import Callable, NamedTuple

import jax
import jax.numpy as jnp
from jax import lax
from jax.experimental import pallas as pl
from jax.experimental.pallas import tpu as pltpu

F32 = jnp.float32
BF16 = jnp.bfloat16
HI = lax.Precision.HIGHEST
MID = lax.Precision.HIGH
MESH = pl.DeviceIdType.MESH

D = 1024
DEPTH = 2
HEADS = 4
DK = 128
CHUNK = 64
QKV_W = 1536
Z_W = 512
SG_W = 512
FFN = 2816
FFN_HALF = FFN // 2
N_CHIPS = 4
DN_SHARD = FFN // N_CHIPS
LN_EPS = 1e-5
RMS_EPS = 1e-6
L2_EPS = 1e-6
ALPHA = (2 * DEPTH) ** 0.25
ADAM_LR, ADAM_B1, ADAM_B2, ADAM_EPS, ADAM_WD, ADAM_STEP = 0.001, 0.9, 0.999, 1e-08, 0.01, 10

HALO = 16
LANES = 128
IN_COLS_PAD = 5248
C_Z, C_UV, C_GA, C_GB, C_BA = 1536, 2048, 3072, 4096, 5120
VMEM_LIMIT = 56 * 1024 * 1024
DN_TILE = 256


def _params(n_grid=1):
    return pltpu.CompilerParams(dimension_semantics=("arbitrary",) * n_grid, vmem_limit_bytes=VMEM_LIMIT)


def _mm(a, b):
    return jnp.dot(a.astype(BF16), b.astype(BF16), preferred_element_type=F32)


def _mm_nt(a, b):
    return lax.dot_general(a.astype(BF16), b.astype(BF16), (((1,), (1,)), ((), ())), preferred_element_type=F32)


def _mm_tn(a, b):
    return lax.dot_general(a.astype(BF16), b.astype(BF16), (((0,), (0,)), ((), ())), preferred_element_type=F32)


def _bdot(a, b, prec=MID):
    return lax.dot_general(a, b, (((2,), (1,)), ((0,), (0,))), precision=prec, preferred_element_type=F32)


def _bdot_nt(a, b, prec=MID):
    return lax.dot_general(a, b, (((2,), (2,)), ((0,), (0,))), precision=prec, preferred_element_type=F32)


def _bf16_dot(a, b, contract):
    return lax.dot_general(a.astype(BF16), b.astype(BF16), (contract, ((0,), (0,))), preferred_element_type=F32)


@jax.custom_vjp
def _fdot(a, b):
    return _bf16_dot(a, b, ((2,), (1,)))


def _fdot_fwd(a, b):
    return _fdot(a, b), (a, b)


def _fdot_bwd(res, ct):
    a, b = res
    return _bf16_dot(ct, b, ((2,), (2,))), _bf16_dot(a, ct, ((1,), (1,)))


_fdot.defvjp(_fdot_fwd, _fdot_bwd)


@jax.custom_vjp
def _fdot_nt(a, b):
    return _bf16_dot(a, b, ((2,), (2,)))


def _fdot_nt_fwd(a, b):
    return _fdot_nt(a, b), (a, b)


def _fdot_nt_bwd(res, ct):
    a, b = res
    return _bf16_dot(ct, b, ((2,), (1,))), _bf16_dot(ct, a, ((1,), (1,)))


_fdot_nt.defvjp(_fdot_nt_fwd, _fdot_nt_bwd)


@jax.custom_vjp
def _fdot_tn(a, b):
    return _bf16_dot(a, b, ((1,), (1,)))


def _fdot_tn_fwd(a, b):
    return _fdot_tn(a, b), (a, b)


def _fdot_tn_bwd(res, ct):
    a, b = res
    return _bf16_dot(b, ct, ((2,), (2,))), _bf16_dot(a, ct, ((2,), (1,)))


_fdot_tn.defvjp(_fdot_tn_fwd, _fdot_tn_bwd)


def _stack(parts):
    return jnp.concatenate([p[None] for p in parts], axis=0)


def _ln(x, g, b):
    mu = jnp.mean(x, axis=-1, keepdims=True)
    xc = x - mu
    var = jnp.mean(xc * xc, axis=-1, keepdims=True)
    return xc * lax.rsqrt(var + LN_EPS) * g + b


def _shift_rows(x, s):
    s = s % x.shape[0]
    return x if s == 0 else pltpu.roll(x, s, 0)


@jax.custom_vjp
def _conv(xcat, w):
    k_taps = len(w)
    y = None
    for k in range(k_taps):
        t = _shift_rows(xcat, k_taps - 1 - k)[HALO:] * w[k]
        y = t if y is None else y + t
    return y


def _conv_fwd(xcat, w):
    return _conv(xcat, w), (xcat, w)


def _conv_bwd(res, dy):
    xcat, w = res
    k_taps = len(w)
    dyp = jnp.concatenate([jnp.zeros((HALO, dy.shape[1]), dy.dtype), dy], axis=0)
    dx = None
    dws = []
    for k in range(k_taps):
        s = k_taps - 1 - k
        t = _shift_rows(dyp, -s) * w[k]
        dx = t if dx is None else dx + t
        dws.append(jnp.sum(_shift_rows(xcat, s)[HALO:] * dy, axis=0, keepdims=True))
    return dx, tuple(dws)


_conv.defvjp(_conv_fwd, _conv_bwd)


@jax.custom_vjp
def _tri_inv(l):
    n = l.shape[-1]
    r = lax.broadcasted_iota(jnp.int32, (n, n), 0)
    c = lax.broadcasted_iota(jnp.int32, (n, n), 1)
    eye = (r == c).astype(F32)
    p = eye - l
    lp = l
    steps = int(math.log2(n)) - 1
    for i in range(steps):
        dot = _bdot if i < 2 else functools.partial(_bf16_dot, contract=((2,), (1,)))
        lp = dot(lp, lp)
        p = p + dot(p, lp)
    return p


def _tri_inv_fwd(l):
    t = _tri_inv(l)
    return t, t


def _tri_inv_bwd(t, dt):
    tt = jnp.swapaxes(t, 1, 2)
    return (-_bdot(tt, _bdot(dt, tt)),)


_tri_inv.defvjp(_tri_inv_fwd, _tri_inv_bwd)


def _dn_glue(qkvcat, z, ba, s_in, cw, a_row, dtb_row, nw_row):
    t_rows = z.shape[0]
    nc = t_rows // CHUNK
    nb = nc * HEADS

    qkv = jax.nn.silu(_conv(qkvcat, cw))

    def chunks(t, off):
        return _stack([t[n * CHUNK:(n + 1) * CHUNK, off + h * DK: off + (h + 1) * DK]
                       for n in range(nc) for h in range(HEADS)])

    q = chunks(qkv, 0)
    k = chunks(qkv, 512)
    v = chunks(qkv, 1024)
    q = q * lax.rsqrt(jnp.sum(q * q, axis=-1, keepdims=True) + L2_EPS) * (DK ** -0.5)
    k = k * lax.rsqrt(jnp.sum(k * k, axis=-1, keepdims=True) + L2_EPS)

    lane = lax.broadcasted_iota(jnp.int32, (LANES, HEADS * DK), 0)
    head_of_col = lax.broadcasted_iota(jnp.int32, (LANES, HEADS * DK), 1) // DK
    e_beta = (head_of_col == lane).astype(F32)
    e_g = (head_of_col + HEADS == lane).astype(F32)
    beta_l = jax.nn.sigmoid(ba)
    g_l = -jnp.exp(a_row) * jax.nn.softplus(ba + dtb_row)
    beta = chunks(jnp.dot(beta_l, e_beta, precision=MID, preferred_element_type=F32), 0)
    g = chunks(jnp.dot(g_l, e_g, precision=MID, preferred_element_type=F32), 0)

    r = lax.broadcasted_iota(jnp.int32, (CHUNK, CHUNK), 0)
    c = lax.broadcasted_iota(jnp.int32, (CHUNK, CHUNK), 1)
    causal = r >= c
    strict = r > c
    tril_b = jnp.broadcast_to(causal.astype(F32), (nb, CHUNK, CHUNK))
    gi_b = _bdot(tril_b, g, HI)
    gi = gi_b[:, :, :CHUNK]
    gj = jnp.swapaxes(gi, 1, 2)
    decay = jnp.where(causal, jnp.exp(jnp.where(causal, gi - gj, 0.0)), 0.0)
    kb = k * beta
    l_mat = jnp.where(strict, _fdot_nt(kb, k) * decay, 0.0)
    t_mat = _tri_inv(l_mat)
    e_gi = jnp.exp(gi_b)
    w_mat = _fdot(t_mat, kb * e_gi)
    u_mat = _fdot(t_mat, v * beta)
    a_qk = _fdot_nt(q, k) * decay
    q_g = q * e_gi
    gl_b = jnp.broadcast_to(jnp.sum(g, axis=1, keepdims=True), g.shape)
    k_d = k * jnp.exp(gl_b - gi_b)
    e_gl = jnp.exp(gl_b)
    g_last = jnp.concatenate([e_gl, e_gl], axis=1)

    state = s_in
    rows = []
    for n in range(nc):
        sl = slice(n * HEADS, (n + 1) * HEADS)
        u_new = u_mat[sl] - _fdot(w_mat[sl], state)
        o_n = _fdot(q_g[sl], state) + _fdot(a_qk[sl], u_new)
        state = state * g_last[sl] + _fdot_tn(k_d[sl], u_new)
        o_n = o_n * lax.rsqrt(jnp.mean(o_n * o_n, axis=-1, keepdims=True) + RMS_EPS) * nw_row
        z_n = _stack([z[n * CHUNK:(n + 1) * CHUNK, h * DK:(h + 1) * DK] for h in range(HEADS)])
        o_n = o_n * jax.nn.silu(z_n)
        rows.append(jnp.concatenate([o_n[h] for h in range(HEADS)], axis=-1))
    return jnp.concatenate(rows, axis=0), state


def _sg_glue(uv, lng, lnb, w_s, bs_t):
    t_rows = uv.shape[0]
    y = jax.nn.gelu(uv)
    u = y[:, :SG_W]
    v = _ln(y[:, SG_W:], lng, lnb)
    r = lax.broadcasted_iota(jnp.int32, (LANES, LANES), 0)
    c = lax.broadcasted_iota(jnp.int32, (LANES, LANES), 1)
    wm = jnp.where(r >= c, w_s, 0.0)
    lane = lax.broadcasted_iota(jnp.int32, (LANES, SG_W), 0)
    group_of_col = lax.broadcasted_iota(jnp.int32, (LANES, SG_W), 1) // LANES
    e_grp = (group_of_col == lane).astype(F32)
    bias = jnp.dot(bs_t, e_grp, precision=HI, preferred_element_type=F32)
    outs = []
    for n in range(t_rows // LANES):
        vb = v[n * LANES:(n + 1) * LANES]
        vg = _stack([vb[:, g * LANES:(g + 1) * LANES] for g in range(4)])
        mg = _fdot(wm, vg)
        mixed = jnp.concatenate([mg[g] for g in range(4)], axis=-1) + bias
        outs.append(u[n * LANES:(n + 1) * LANES] * mixed)
    return jnp.concatenate(outs, axis=0)


def _merge_glue(ga, gb, ya, yb):
    return jax.nn.sigmoid(ga) * ya + jax.nn.sigmoid(gb) * yb


def _res_ln_glue(x, r, g, b):
    return _ln(ALPHA * x + r, g, b)


def _ffn_glue(ua, ub, cwa, cwb):
    return jax.nn.silu(_conv(ua, cwa)) * _conv(ub, cwb)


def _row(t, c, col=0):
    return pl.BlockSpec((t, c), lambda i: (i, col))


def _row_rev(t, c, nt, col=0):
    return pl.BlockSpec((t, c), lambda i: (nt - 1 - i, col))


def _halo(t, c, nt=None):
    per = t // HALO
    if nt is None:
        return pl.BlockSpec((HALO, c), lambda i: (jnp.maximum(i * per - 1, 0), 0))
    return pl.BlockSpec((HALO, c), lambda i: (jnp.maximum((nt - 1 - i) * per - 1, 0), 0))


def _full(shape):
    nd = len(shape)
    return pl.BlockSpec(shape, lambda i: (0,) * nd)


ANY = pl.BlockSpec(memory_space=pl.ANY)


def _sds(shape, dtype=F32):
    return jax.ShapeDtypeStruct(shape, dtype)


def _tile(s, want=256):
    for t in (want, 256, 128):
        if s % t == 0:
            return t
    raise ValueError(f"sequence length {s} is not a multiple of 128")


def proj_fwd(x, w, name, carry=None):
    s = x.shape[0]
    t = _tile(s, 512)
    nt = s // t
    segs = [(0, 2048), (2048, 3072), (3072, 4096), (4096, 5120), (5120, IN_COLS_PAD)]

    def body(x_ref, w_ref, p_ref):
        xb = x_ref[...].astype(BF16)
        for lo, hi in segs:
            p_ref[:, lo:hi] = jnp.dot(xb, w_ref[:, lo:hi], preferred_element_type=F32)

    return _host_call(
        body, carry, nt, grid=(nt,), name=name, in_specs=[_row(t, D), _full((D, IN_COLS_PAD))],
        out_specs=[_row(t, IN_COLS_PAD)], out_shape=[_sds((s, IN_COLS_PAD))], scratch_shapes=[], operands=(x, w))


def dn_fwd(p, cq, a_row, dtb_row, nw_row, name, carry=None):
    s = p.shape[0]
    t = _tile(s, DN_TILE)
    nt = s // t

    def body(qkv_ref, halo_ref, z_ref, ba_ref, cq_ref, a_ref, dtb_ref, nw_ref, o_ref, sst_ref, s_scr):
        i = pl.program_id(0)

        @pl.when(i == 0)
        def _():
            s_scr[...] = jnp.zeros_like(s_scr)

        halo = jnp.where(i == 0, 0.0, halo_ref[...])
        qkvcat = jnp.concatenate([halo, qkv_ref[...]], axis=0)
        cw = tuple(cq_ref[k:k + 1, :] for k in range(4))
        s_in = s_scr[...]
        sst_ref[0] = s_in
        o, s_out = _dn_glue(qkvcat, z_ref[...], ba_ref[...], s_in, cw, a_ref[...], dtb_ref[...], nw_ref[...])
        o_ref[...] = o.astype(BF16)
        s_scr[...] = s_out

    return _host_call(
        body, carry, nt, grid=(nt,), name=name,
        in_specs=[_row(t, QKV_W), _halo(t, QKV_W), _row(t, Z_W, C_Z // Z_W), _row(t, LANES, C_BA // LANES),
                  _full((4, QKV_W)), _full((1, LANES)), _full((1, LANES)), _full((1, LANES))],
        out_specs=[_row(t, Z_W), pl.BlockSpec((1, HEADS, DK, DK), lambda i: (i, 0, 0, 0))],
        out_shape=[_sds((s, Z_W), BF16), _sds((nt, HEADS, DK, DK))],
        scratch_shapes=[pltpu.VMEM((HEADS, DK, DK), F32)], operands=(p, p, p, p, cq, a_row, dtb_row, nw_row))


def sg_fwd(p, lng, lnb, w_s, bs_t, name):
    s = p.shape[0]
    t = _tile(s, 512)

    def body(uv_ref, lng_ref, lnb_ref, ws_ref, bs_ref, o_ref):
        o_ref[...] = _sg_glue(uv_ref[...], lng_ref[...], lnb_ref[...], ws_ref[...], bs_ref[...]).astype(BF16)

    return pl.pallas_call(
        body, grid=(s // t,), name=name,
        in_specs=[_row(t, 2 * SG_W, C_UV // (2 * SG_W)), _full((1, SG_W)), _full((1, SG_W)),
                  _full((4, LANES, LANES)), _full((LANES, LANES))],
        out_specs=_row(t, SG_W), out_shape=_sds((s, SG_W), BF16), compiler_params=_params())(p, lng, lnb, w_s, bs_t)


def merge_fwd(oa, ob, p, x, wa, wb, wo, g1, b1, name):
    s = x.shape[0]
    t = _tile(s, 512)

    def body(oa_ref, ob_ref, ga_ref, gb_ref, x_ref, wa_ref, wb_ref, wo_ref, g_ref, b_ref, x1_ref):
        ya = _mm(oa_ref[...], wa_ref[...])
        yb = _mm(ob_ref[...], wb_ref[...])
        h = _merge_glue(ga_ref[...], gb_ref[...], ya, yb)
        x1_ref[...] = _res_ln_glue(x_ref[...], _mm(h, wo_ref[...]), g_ref[...], b_ref[...])

    return pl.pallas_call(
        body, grid=(s // t,), name=name,
        in_specs=[_row(t, Z_W), _row(t, SG_W), _row(t, D, C_GA // D), _row(t, D, C_GB // D), _row(t, D),
                  _full((Z_W, D)), _full((SG_W, D)), _full((D, D)), _full((1, D)), _full((1, D))],
        out_specs=_row(t, D), out_shape=_sds((s, D)), compiler_params=_params())(oa, ob, p, p, x, wa, wb, wo, g1, b1)


def _load_ffn_weights(wup_hbm, wdn_hbm, wup_v, wdn_v, up_slots, dn_slots):
    for n, k in enumerate(up_slots):
        pltpu.sync_copy(wup_hbm.at[k], wup_v.at[n])
    for n, k in enumerate(dn_slots):
        pltpu.sync_copy(wdn_hbm.at[k], wdn_v.at[pl.ds(n * DN_SHARD, DN_SHARD)])


def ffn_fwd(x1, wup4, cf4, wdn4, g2, b2, name, carry=None):
    s = x1.shape[0]
    t = _tile(s, 512)
    nt = s // t

    def body(x1_ref, halo_ref, wup_hbm, cf_ref, wdn_hbm, g_ref, b_ref, pre_ref, x2_ref, wup_v, wdn_v):
        i = pl.program_id(0)

        @pl.when(i == 0)
        def _():
            _load_ffn_weights(wup_hbm, wdn_hbm, wup_v, wdn_v, range(4), range(4))

        x1v = x1_ref[...]
        halo = jnp.where(i == 0, 0.0, halo_ref[...])
        x1cat = jnp.concatenate([halo, x1v], axis=0).astype(BF16)
        f = None
        for h in range(2):
            ua = jnp.dot(x1cat, wup_v[h], preferred_element_type=F32)
            ub = jnp.dot(x1cat, wup_v[2 + h], preferred_element_type=F32)
            cwa = tuple(cf_ref[h, k:k + 1, :] for k in range(3))
            cwb = tuple(cf_ref[2 + h, k:k + 1, :] for k in range(3))
            act = _ffn_glue(ua, ub, cwa, cwb)
            fh = _mm(act, wdn_v[h * FFN_HALF:(h + 1) * FFN_HALF, :])
            f = fh if f is None else f + fh
        pre = ALPHA * x1v + f
        pre_ref[...] = pre
        x2_ref[...] = _ln(pre, g_ref[...], b_ref[...])

    return _host_call(
        body, carry, nt, grid=(nt,), name=name,
        in_specs=[_row(t, D), _halo(t, D), ANY, _full((4, 3, FFN_HALF)), ANY, _full((1, D)), _full((1, D))],
        out_specs=[_row(t, D), _row(t, D)], out_shape=[_sds((s, D)), _sds((s, D))],
        scratch_shapes=[pltpu.VMEM((4, D, FFN_HALF), BF16), pltpu.VMEM((FFN, D), BF16)],
        operands=(x1, x1, wup4, cf4, wdn4, g2, b2))


def loss_ln_bwd(pre, tgt, g, b, name):
    s = pre.shape[0]
    t = _tile(s, 512)

    def body(pre_ref, t_ref, g_ref, b_ref, dpre_ref, dg_ref, db_ref, loss_ref):
        first = pl.program_id(0) == 0
        y, vjp = jax.vjp(_ln, pre_ref[...], g_ref[...], b_ref[...])
        e = y - t_ref[...]
        dpre, dg, db = vjp(e * (1.0 / D))
        dpre_ref[...] = dpre
        _acc(dg_ref, dg, first)
        _acc(db_ref, db, first)
        part = jnp.sum(jnp.sum(e * e, axis=1, keepdims=True), axis=0, keepdims=True) * (0.5 / D)
        _acc(loss_ref, jnp.broadcast_to(part, loss_ref.shape), first)

    return pl.pallas_call(
        body, grid=(s // t,), name=name, in_specs=[_row(t, D), _row(t, D), _full((1, D)), _full((1, D))],
        out_specs=[_row(t, D), _full((1, D)), _full((1, D)), _full((8, LANES))],
        out_shape=[_sds((s, D)), _sds((1, D)), _sds((1, D)), _sds((8, LANES))], compiler_params=_params())(pre, tgt, g, b)


def _acc(ref, val, first):
    @pl.when(first)
    def _():
        ref[...] = val

    @pl.when(jnp.logical_not(first))
    def _():
        ref[...] += val


def _acc_tn(acc_ref, a, b, first, seg):
    n = b.shape[1]
    for lo in range(0, n, seg):
        hi = min(lo + seg, n)
        _acc(acc_ref.at[:, lo:hi], _mm_tn(a, b[:, lo:hi]), first)


def ln_bwd(pre, dy, g, b, name):
    s = pre.shape[0]
    t = _tile(s, 512)

    def body(pre_ref, dy_ref, g_ref, b_ref, dpre_ref, dg_ref, db_ref):
        _, vjp = jax.vjp(_ln, pre_ref[...], g_ref[...], b_ref[...])
        dpre, dg, db = vjp(dy_ref[...])
        dpre_ref[...] = dpre
        first = pl.program_id(0) == 0
        _acc(dg_ref, dg, first)
        _acc(db_ref, db, first)

    return pl.pallas_call(
        body, grid=(s // t,), name=name, in_specs=[_row(t, D), _row(t, D), _full((1, D)), _full((1, D))],
        out_specs=[_row(t, D), _full((1, D)), _full((1, D))],
        out_shape=[_sds((s, D)), _sds((1, D)), _sds((1, D))], compiler_params=_params())(pre, dy, g, b)


def ffn_bwd(x1, df, acc_in, acc_scale, wup4, cf4, wdn4, h, name, carry=None):
    s = x1.shape[0]
    t = _tile(s)
    nt = s // t

    def body(x1_ref, halo_ref, df_ref, acc_ref, wup_hbm, cf_ref, wdn_hbm,
             dx1_ref, dwup_hbm, dcf_ref, dwdn_hbm, wup_v, wdn_v, dwup_v, dwdn_v, carry):
        i = pl.program_id(0)
        j = nt - 1 - i
        first = i == 0

        @pl.when(first)
        def _():
            _load_ffn_weights(wup_hbm, wdn_hbm, wup_v, wdn_v, (h, 2 + h), (2 * h, 2 * h + 1))
            carry[...] = jnp.zeros_like(carry)

        halo = jnp.where(j == 0, 0.0, halo_ref[...])
        x1cat = jnp.concatenate([halo, x1_ref[...]], axis=0).astype(BF16)
        ua = jnp.dot(x1cat, wup_v[0], preferred_element_type=F32)
        ub = jnp.dot(x1cat, wup_v[1], preferred_element_type=F32)
        cwa = tuple(cf_ref[h, k:k + 1, :] for k in range(3))
        cwb = tuple(cf_ref[2 + h, k:k + 1, :] for k in range(3))
        act, vjp = jax.vjp(_ffn_glue, ua, ub, cwa, cwb)
        dfb = df_ref[...].astype(BF16)
        dact = _mm_nt(dfb, wdn_v[...])
        _acc_tn(dwdn_v, act.astype(BF16), dfb, first, 512)
        dua, dub, dcwa, dcwb = vjp(dact)
        x1b = x1cat[HALO:]
        dups = []
        for n, du in enumerate((dua, dub)):
            dups.append(jnp.concatenate([du[HALO:t], du[t:] + carry[n]], axis=0).astype(BF16))
            carry[n] = du[:HALO]
            _acc(dwup_v.at[n], _mm_tn(x1b, dups[n]), first)
        for k in range(3):
            _acc(dcf_ref.at[0, k:k + 1, :], dcwa[k], first)
            _acc(dcf_ref.at[1, k:k + 1, :], dcwb[k], first)
        dx1_ref[...] = acc_scale * acc_ref[...] + _mm_nt(dups[0], wup_v[0]) + _mm_nt(dups[1], wup_v[1])

        @pl.when(i == nt - 1)
        def _():
            pltpu.sync_copy(dwup_v, dwup_hbm)
            pltpu.sync_copy(dwdn_v, dwdn_hbm)

    return _host_call(
        body, carry, nt, grid=(nt,), name=name,
        in_specs=[_row_rev(t, D, nt), _halo(t, D, nt), _row_rev(t, D, nt), _row_rev(t, D, nt),
                  ANY, _full((4, 3, FFN_HALF)), ANY],
        out_specs=[_row_rev(t, D, nt), ANY, _full((2, 3, FFN_HALF)), ANY],
        out_shape=[_sds((s, D)), _sds((2, D, FFN_HALF)), _sds((2, 3, FFN_HALF)), _sds((FFN_HALF, D))],
        scratch_shapes=[pltpu.VMEM((2, D, FFN_HALF), BF16), pltpu.VMEM((FFN_HALF, D), BF16),
                        pltpu.VMEM((2, D, FFN_HALF), F32), pltpu.VMEM((FFN_HALF, D), F32),
                        pltpu.VMEM((2, HALO, FFN_HALF), F32)],
        operands=(x1, x1, df, acc_in, wup4, cf4, wdn4))


def merge_bwd(oa, ob, p, x, dx1, wa, wb, wo, g1, b1, name, carry=None):
    s = x.shape[0]
    t = _tile(s)

    def body(oa_ref, ob_ref, ga_ref, gb_ref, x_ref, dx1_ref, wa_ref, wb_ref, wo_ref, g_ref, b_ref,
             doa_ref, dob_ref, dga_ref, dgb_ref, dx_ref, dwa_ref, dwb_ref, dwo_ref, dg_ref, db_ref):
        first = pl.program_id(0) == 0
        oa = oa_ref[...]
        ob = ob_ref[...]
        ya = _mm(oa, wa_ref[...])
        yb = _mm(ob, wb_ref[...])
        h, vjp1 = jax.vjp(_merge_glue, ga_ref[...], gb_ref[...], ya, yb)
        hb = h.astype(BF16)
        r = _mm(hb, wo_ref[...])
        _, vjp2 = jax.vjp(_res_ln_glue, x_ref[...], r, g_ref[...], b_ref[...])
        dx, dr, dg, db = vjp2(dx1_ref[...])
        dx_ref[...] = dx
        _acc(dg_ref, dg, first)
        _acc(db_ref, db, first)
        drb = dr.astype(BF16)
        dh = _mm_nt(drb, wo_ref[...])
        _acc(dwo_ref, _mm_tn(hb, drb), first)
        dga, dgb, dya, dyb = vjp1(dh)
        dga_ref[...] = dga.astype(BF16)
        dgb_ref[...] = dgb.astype(BF16)
        dyab = dya.astype(BF16)
        dybb = dyb.astype(BF16)
        doa_ref[...] = _mm_nt(dyab, wa_ref[...]).astype(BF16)
        dob_ref[...] = _mm_nt(dybb, wb_ref[...]).astype(BF16)
        _acc(dwa_ref, _mm_tn(oa, dyab), first)
        _acc(dwb_ref, _mm_tn(ob, dybb), first)

    return _host_call(
        body, carry, s // t, grid=(s // t,), name=name,
        in_specs=[_row(t, Z_W), _row(t, SG_W), _row(t, D, C_GA // D), _row(t, D, C_GB // D), _row(t, D), _row(t, D),
                  _full((Z_W, D)), _full((SG_W, D)), _full((D, D)), _full((1, D)), _full((1, D))],
        out_specs=[_row(t, Z_W), _row(t, SG_W), _row(t, D), _row(t, D), _row(t, D),
                   _full((Z_W, D)), _full((SG_W, D)), _full((D, D)), _full((1, D)), _full((1, D))],
        out_shape=[_sds((s, Z_W), BF16), _sds((s, SG_W), BF16), _sds((s, D), BF16), _sds((s, D), BF16), _sds((s, D)),
                   _sds((Z_W, D)), _sds((SG_W, D)), _sds((D, D)), _sds((1, D)), _sds((1, D))],
        scratch_shapes=[], operands=(oa, ob, p, p, x, dx1, wa, wb, wo, g1, b1))


def sg_bwd(p, dob, lng, lnb, w_s, bs_t, name, carry=None):
    s = p.shape[0]
    t = _tile(s, 512)

    def body(uv_ref, dob_ref, lng_ref, lnb_ref, ws_ref, bs_ref, duv_ref, dlng_ref, dlnb_ref, dws_ref, dbs_ref):
        first = pl.program_id(0) == 0
        _, vjp = jax.vjp(_sg_glue, uv_ref[...], lng_ref[...], lnb_ref[...], ws_ref[...], bs_ref[...])
        duv, dlng, dlnb, dws, dbs = vjp(dob_ref[...].astype(F32))
        duv_ref[...] = duv.astype(BF16)
        _acc(dlng_ref, dlng, first)
        _acc(dlnb_ref, dlnb, first)
        _acc(dws_ref, dws, first)
        _acc(dbs_ref, dbs, first)

    return _host_call(
        body, carry, s // t, grid=(s // t,), name=name,
        in_specs=[_row(t, 2 * SG_W, C_UV // (2 * SG_W)), _row(t, SG_W), _full((1, SG_W)), _full((1, SG_W)),
                  _full((4, LANES, LANES)), _full((LANES, LANES))],
        out_specs=[_row(t, 2 * SG_W), _full((1, SG_W)), _full((1, SG_W)), _full((4, LANES, LANES)), _full((LANES, LANES))],
        out_shape=[_sds((s, 2 * SG_W), BF16), _sds((1, SG_W)), _sds((1, SG_W)), _sds((4, LANES, LANES)), _sds((LANES, LANES))],
        scratch_shapes=[], operands=(p, dob, lng, lnb, w_s, bs_t))


def dn_bwd(p, sst, doa, cq, a_row, dtb_row, nw_row, name, carry=None):
    s = p.shape[0]
    t = _tile(s, DN_TILE)
    nt = s // t

    def body(qkv_ref, halo_ref, z_ref, ba_ref, sst_ref, doa_ref, cq_ref, a_ref, dtb_ref, nw_ref,
             dqkv_ref, dz_ref, dba_ref, dcq_ref, da_ref, ddtb_ref, dnw_ref, ds_scr, carry):
        i = pl.program_id(0)
        j = nt - 1 - i
        first = i == 0

        @pl.when(first)
        def _():
            ds_scr[...] = jnp.zeros_like(ds_scr)
            carry[...] = jnp.zeros_like(carry)

        halo = jnp.where(j == 0, 0.0, halo_ref[...])
        qkvcat = jnp.concatenate([halo, qkv_ref[...]], axis=0)
        cw = tuple(cq_ref[k:k + 1, :] for k in range(4))
        _, vjp = jax.vjp(_dn_glue, qkvcat, z_ref[...], ba_ref[...], sst_ref[0], cw, a_ref[...], dtb_ref[...], nw_ref[...])
        dqkvcat, dz, dba, ds_in, dcw, da, ddtb, dnw = vjp((doa_ref[...].astype(F32), ds_scr[...]))
        ds_scr[...] = ds_in
        dz_ref[...] = dz.astype(BF16)
        dba_ref[...] = dba.astype(BF16)
        dtile = dqkvcat[HALO:]
        dqkv_ref[...] = dtile.astype(BF16)
        dqkv_ref[t - HALO:t, :] = (dtile[t - HALO:] + carry[...]).astype(BF16)
        carry[...] = dqkvcat[:HALO]
        for k in range(4):
            _acc(dcq_ref.at[k:k + 1, :], dcw[k], first)
        _acc(da_ref, da, first)
        _acc(ddtb_ref, ddtb, first)
        _acc(dnw_ref, dnw, first)

    return _host_call(
        body, carry, nt, grid=(nt,), name=name,
        in_specs=[_row_rev(t, QKV_W, nt), _halo(t, QKV_W, nt), _row_rev(t, Z_W, nt, C_Z // Z_W),
                  _row_rev(t, LANES, nt, C_BA // LANES),
                  pl.BlockSpec((1, HEADS, DK, DK), lambda i: (nt - 1 - i, 0, 0, 0)), _row_rev(t, Z_W, nt),
                  _full((4, QKV_W)), _full((1, LANES)), _full((1, LANES)), _full((1, LANES))],
        out_specs=[_row_rev(t, QKV_W, nt), _row_rev(t, Z_W, nt), _row_rev(t, LANES, nt),
                   _full((4, QKV_W)), _full((1, LANES)), _full((1, LANES)), _full((1, LANES))],
        out_shape=[_sds((s, QKV_W), BF16), _sds((s, Z_W), BF16), _sds((s, LANES), BF16),
                   _sds((4, QKV_W)), _sds((1, LANES)), _sds((1, LANES)), _sds((1, LANES))],
        scratch_shapes=[pltpu.VMEM((HEADS, DK, DK), F32), pltpu.VMEM((HALO, QKV_W), F32)],
        operands=(p, p, p, p, sst, doa, cq, a_row, dtb_row, nw_row))


def proj_bwd(dps, dxd, w, name, carry=None):
    s = dxd.shape[0]
    t = _tile(s, 512)
    n = len(dps)

    def body(*refs):
        dp_refs, dxd_ref, w_hbm, dx_ref, w_v = refs[:n], refs[n], refs[n + 1], refs[n + 2], refs[n + 3]

        @pl.when(pl.program_id(0) == 0)
        def _():
            pltpu.sync_copy(w_hbm, w_v)

        dp = jnp.concatenate([r[...] for r in dp_refs], axis=1)
        dx_ref[...] = dxd_ref[...] + _mm_nt(dp, w_v[...])

    return _host_call(
        body, carry, s // t, grid=(s // t,), name=name,
        in_specs=[_row(t, dp.shape[1]) for dp in dps] + [_row(t, D), ANY],
        out_specs=[_row(t, D)], out_shape=[_sds((s, D))],
        scratch_shapes=[pltpu.VMEM((D, IN_COLS_PAD), BF16)], operands=(*dps, dxd, w))


def wgrad(x, dp, col, into, name):
    s, n = dp.shape
    tk = _tile(s, 1024)
    tn = next(c for c in (1024, 768, 512, 256, 128) if n % c == 0 and col % c == 0)
    block = col // tn

    def body(x_ref, dp_ref, *rest):
        o_ref = rest[-1]
        _acc(o_ref, _mm_tn(x_ref[...], dp_ref[...]), pl.program_id(1) == 0)

    operands = (x, dp) if into is None else (x, dp, into)
    return pl.pallas_call(
        body, grid=(n // tn, s // tk), name=name,
        in_specs=[pl.BlockSpec((tk, D), lambda j, k: (k, 0)), pl.BlockSpec((tk, tn), lambda j, k: (k, j))]
        + ([] if into is None else [ANY]),
        out_specs=pl.BlockSpec((D, tn), lambda j, k: (0, block + j)), out_shape=_sds((D, IN_COLS_PAD)),
        input_output_aliases={} if into is None else {2: 0},
        compiler_params=_params(2))(*operands)


def _rows_block(rows, cols):
    cap = max(HALO, (2 * 1024 * 1024) // (cols * 4))
    for cand in range(min(rows, cap) // HALO * HALO, HALO - 1, -HALO):
        if rows % cand == 0:
            return cand
    return rows


def adam_call(w, g, m, v, name):
    rows, cols = w.shape
    tr = _rows_block(rows, cols)
    c1 = 1.0 - ADAM_B1 ** ADAM_STEP
    c2 = 1.0 - ADAM_B2 ** ADAM_STEP

    def body(w_ref, g_ref, m_ref, v_ref, go_ref, d_ref, nm_ref, nv_ref):
        gv = g_ref[...]
        go_ref[...] = gv
        nm = ADAM_B1 * m_ref[...] + (1.0 - ADAM_B1) * gv
        nv = ADAM_B2 * v_ref[...] + (1.0 - ADAM_B2) * (gv * gv)
        d_ref[...] = -ADAM_LR * ((nm / c1) / (jnp.sqrt(nv / c2) + ADAM_EPS) + ADAM_WD * w_ref[...])
        nm_ref[...] = nm
        nv_ref[...] = nv

    spec = pl.BlockSpec((tr, cols), lambda i: (i, 0))
    return pl.pallas_call(
        body, grid=(rows // tr,), name=name, in_specs=[spec] * 4, out_specs=[spec] * 4,
        out_shape=[_sds((rows, cols))] * 4, compiler_params=_params())(w, g, m, v)


def _place():
    return lax.axis_index("x"), lax.axis_index("y"), lax.axis_index("c")


def _other_chips(x, y):
    return [(1 - x, y), (x, 1 - y), (1 - x, 1 - y)]


def _remote(src, dst, send_sem, recv_sem, to):
    return pltpu.make_async_remote_copy(src_ref=src, dst_ref=dst, send_sem=send_sem, recv_sem=recv_sem,
                                        device_id=to, device_id_type=MESH)


class _Gather:
    def __init__(self, ins, outs, send_sems, recv_sems, local_sems):
        self.ins, self.outs, self.n = ins, outs, len(ins)
        self.send_sems, self.recv_sems, self.local_sems = send_sems, recv_sems, local_sems
        self.x, self.y, self.c = _place()
        self.me = 2 * self.x + self.y
        self.chips = _other_chips(self.x, self.y)

    def _copy(self, t, k, slot, part, to, src=None):
        dst = self.outs[t].at[slot, part]
        return _remote(dst if src is None else src, dst, self.send_sems.at[6 * t + k], self.recv_sems.at[6 * t + k], to)

    def _mine(self):
        return [pltpu.make_async_copy(self.ins[t].at[p], self.outs[t].at[self.me, p], self.local_sems.at[2 * t + p])
                for t in range(self.n) for p in range(2)]

    def _first(self):
        return [self._copy(t, k, self.me, self.c, (cx, cy, self.c), src=self.ins[t].at[self.c])
                for k, (cx, cy) in enumerate(self.chips) for t in range(self.n)]

    def start(self):
        for cp in self._mine() + self._first():
            cp.start()

    def finish(self):
        x, y, c = self.x, self.y, self.c
        passed = []
        for k, (cx, cy) in enumerate(self.chips):
            for t in range(self.n):
                self._copy(t, k, 2 * cx + cy, c, (x, y, c)).wait_recv()
                passed.append(self._copy(t, 3 + k, 2 * cx + cy, c, (x, y, 1 - c)))
                passed[-1].start()
        for k, (cx, cy) in enumerate(self.chips):
            for t in range(self.n):
                self._copy(t, 3 + k, 2 * cx + cy, 1 - c, (x, y, c)).wait_recv()
        for cp in self._first() + passed:
            cp.wait_send()
        for cp in self._mine():
            cp.wait()


class _Carried(NamedTuple):
    ins: tuple
    out_shapes: tuple
    scratch: tuple
    make: Callable
    aliases: dict


def _host_call(body, carry, steps, *, grid, name, in_specs, out_specs, out_shape, scratch_shapes, operands):
    in_specs, out_specs, out_shape, scratch_shapes = list(in_specs), list(out_specs), list(out_shape), list(scratch_shapes)
    aliases = {}
    if carry is not None:
        n_in, n_out, n_scr = len(in_specs), len(out_specs), len(scratch_shapes)
        n_ci, n_co = len(carry.ins), len(carry.out_shapes)
        plain = body

        def body(*refs):
            ins, cins = refs[:n_in], refs[n_in:n_in + n_ci]
            outs = refs[n_in + n_ci:n_in + n_ci + n_out]
            couts = refs[n_in + n_ci + n_out:n_in + n_ci + n_out + n_co]
            rest = refs[n_in + n_ci + n_out + n_co:]
            exchange = carry.make(cins, couts, *rest[n_scr:])
            pl.when(pl.program_id(0) == 0)(exchange.start)
            plain(*ins, *outs, *rest[:n_scr])
            pl.when(pl.program_id(0) == steps - 1)(exchange.finish)

        aliases = {n_in + i: n_out + j for i, j in carry.aliases.items()}
        in_specs += [ANY] * n_ci
        out_specs += [ANY] * n_co
        out_shape += list(carry.out_shapes)
        scratch_shapes += list(carry.scratch)
        operands = tuple(operands) + tuple(carry.ins)
    return pl.pallas_call(
        body, grid=grid, name=name, in_specs=in_specs, out_specs=out_specs, out_shape=out_shape,
        scratch_shapes=scratch_shapes, input_output_aliases=aliases, compiler_params=_params(len(grid)))(*operands)


def exchange(carry, name):
    n_i, n_o = len(carry.ins), len(carry.out_shapes)

    def body(*refs):
        ex = carry.make(refs[:n_i], refs[n_i:n_i + n_o], *refs[n_i + n_o:])
        ex.start()
        ex.finish()

    return pl.pallas_call(
        body, name=name, in_specs=[ANY] * n_i, out_specs=[ANY] * n_o, out_shape=list(carry.out_shapes),
        scratch_shapes=list(carry.scratch), input_output_aliases=dict(carry.aliases),
        compiler_params=pltpu.CompilerParams(vmem_limit_bytes=VMEM_LIMIT))(*carry.ins)


def _dma_sems(*counts):
    return tuple(pltpu.SemaphoreType.DMA((n,)) for n in counts)


def carried_gather(shards):
    n = len(shards)
    return _Carried(tuple(shards), tuple(_sds((N_CHIPS,) + a.shape, a.dtype) for a in shards),
                    _dma_sems(6 * n, 6 * n, 2 * n), _Gather, {})


class _PairSwap:
    def __init__(self, ins, outs, send_sems, recv_sems):
        x, y, c = _place()
        self.copies = [_remote(ins[t].at[:, 1 - c], outs[t], send_sems.at[t], recv_sems.at[t], (x, y, 1 - c))
                       for t in range(len(ins))]

    def start(self):
        for cp in self.copies:
            cp.start()

    def finish(self):
        for cp in self.copies:
            cp.wait()


def carried_pair_swap(views):
    n = len(views)
    return _Carried(tuple(views), tuple(_sds((v.shape[0],) + v.shape[2:]) for v in views), _dma_sems(n, n), _PairSwap, {})


class _Scatter:
    def __init__(self, srcs, outs, send_sems, recv_sems, local_sems, pieces):
        self.srcs, self.outs, self.pieces, self.n = srcs, outs, pieces, len(pieces)
        self.send_sems, self.recv_sems, self.local_sems = send_sems, recv_sems, local_sems
        self.x, self.y, self.c = _place()
        self.me = 2 * self.x + self.y

    def _piece(self, t, k):
        idx, lead, cols = self.pieces[t][k]
        ref = self.srcs[idx].at[lead]
        return ref if cols is None else ref.at[:, pl.ds(cols[0], cols[1])]

    def _local(self, t, k):
        return pltpu.make_async_copy(self._piece(t, k), self.outs[t].at[k], self.local_sems.at[t])

    def _each_chip(self, mine, others):
        for k in range(N_CHIPS):
            pl.when(self.me == k)(functools.partial(mine, k))
            pl.when(self.me != k)(functools.partial(others, k))

    def start(self):
        def mine(k):
            for t in range(self.n):
                self._local(t, k).start()

        def others(k):
            for t in range(self.n):
                _remote(self._piece(t, k), self.outs[t].at[self.me], self.send_sems.at[N_CHIPS * t + k],
                        self.recv_sems.at[N_CHIPS * t + self.me], (k // 2, k % 2, self.c)).start()

        self._each_chip(mine, others)

    def finish(self):
        def mine(k):
            for t in range(self.n):
                self._local(t, k).wait()

        def others(k):
            for t in range(self.n):
                cp = _remote(self._piece(t, k), self.outs[t].at[k], self.send_sems.at[N_CHIPS * t + k],
                             self.recv_sems.at[N_CHIPS * t + k], (self.x, self.y, self.c))
                cp.wait_recv()
                cp.wait_send()

        self._each_chip(mine, others)


def carried_scatter(srcs, pieces, part_shapes):
    n = len(pieces)
    return _Carried(tuple(srcs), tuple(_sds((N_CHIPS,) + tuple(shp), srcs[0].dtype) for shp in part_shapes),
                    _dma_sems(N_CHIPS * n, N_CHIPS * n, n), functools.partial(_Scatter, pieces=pieces), {})


class _PairJoin:
    def __init__(self, ins, outs, send_sems, recv_sems, layer):
        self.ins, self.outs, self.layer, self.n = ins, outs, layer, len(ins)
        self.send_sems, self.recv_sems = send_sems, recv_sems
        self.x, self.y, self.c = _place()

    def _copy(self, t, half, to):
        return _remote(self.ins[t].at[self.layer, self.c], self.outs[t].at[self.layer, half],
                       self.send_sems.at[t], self.recv_sems.at[t], to)

    def start(self):
        for t in range(self.n):
            self._copy(t, self.c, (self.x, self.y, 1 - self.c)).start()

    def finish(self):
        for t in range(self.n):
            self._copy(t, self.c, (self.x, self.y, 1 - self.c)).wait_send()
            self._copy(t, 1 - self.c, (self.x, self.y, self.c)).wait_recv()


def carried_join(bufs, layer):
    n = len(bufs)
    return _Carried(tuple(bufs), tuple(_sds(b.shape) for b in bufs), _dma_sems(n, n),
                    functools.partial(_PairJoin, layer=layer), {t: t for t in range(n)})


def pair_add_half(mine, theirs, c_vec, name):
    g, _, h, b = mine.shape
    tr = _rows_block(h, b)

    def body(c_ref, a_ref, b_ref, o_ref):
        o_ref[...] = (a_ref[...] + b_ref[...]).astype(BF16)

    part = pl.BlockSpec((None, tr, b), lambda j, i, c: (j, i, 0))
    grid_spec = pltpu.PrefetchScalarGridSpec(
        num_scalar_prefetch=1, grid=(g, h // tr),
        in_specs=[pl.BlockSpec((None, None, tr, b), lambda j, i, c: (j, c[0], i, 0)), part], out_specs=part)
    return pl.pallas_call(body, grid_spec=grid_spec, name=name, out_shape=_sds((g, h, b), BF16),
                          compiler_params=_params(2))(c_vec, mine, theirs)


def chips_add_into(recv, into, layer, c_vec, name):
    n, h, b = recv.shape
    tr = _rows_block(h, b)

    def body(c_ref, r0, r1, r2, r3, *rest):
        rest[-1][...] = ((r0[...].astype(F32) + r1[...].astype(F32)) + r2[...].astype(F32)) + r3[...].astype(F32)

    grid_spec = pltpu.PrefetchScalarGridSpec(
        num_scalar_prefetch=1, grid=(h // tr,),
        in_specs=[pl.BlockSpec((None, tr, b), lambda i, c, k=k: (k, i, 0)) for k in range(n)]
        + ([] if into is None else [ANY]),
        out_specs=pl.BlockSpec((None, None, tr, b), lambda i, c: (layer, c[0], i, 0)))
    return pl.pallas_call(
        body, grid_spec=grid_spec, name=name, out_shape=_sds((2, 2, h, b)),
        input_output_aliases={} if into is None else {1 + n: 0},
        compiler_params=_params())(c_vec, *([recv] * n), *(() if into is None else (into,)))


N_DEV = 8


class _AllSum:
    def __init__(self, ins, outs, buf, total, send_sems, recv_sems, local_sem):
        self.v, self.out, self.buf, self.total = ins[0], outs[0], buf, total
        self.send_sems, self.recv_sems, self.local_sem = send_sems, recv_sems, local_sem
        self.x, self.y, self.c = _place()
        self.me, self.sibling = (self.x, self.y, self.c), (self.x, self.y, 1 - self.c)
        self.chips = _other_chips(self.x, self.y)

    def _slot(self, px, py, pc):
        return self.buf.at[4 * px + 2 * py + pc]

    def _copy(self, k, block, to):
        return _remote(self._slot(*block), self._slot(*block), self.send_sems.at[k], self.recv_sems.at[k], to)

    def _first(self):
        return [self._copy(0, self.me, self.sibling)] + [
            self._copy(1 + k, self.me, (*chip, self.c)) for k, chip in enumerate(self.chips)]

    def start(self):
        load = pltpu.make_async_copy(self.v, self._slot(*self.me), self.local_sem)
        load.start()
        load.wait()
        for cp in self._first():
            cp.start()

    def finish(self):
        c = self.c
        passed = [self._copy(4 + k, (*chip, c), self.sibling) for k, chip in enumerate(self.chips)]
        for k, chip in enumerate(self.chips):
            self._copy(1 + k, (*chip, c), self.me).wait_recv()
            passed[k].start()
        self._copy(0, self.sibling, self.me).wait_recv()
        for k, chip in enumerate(self.chips):
            self._copy(4 + k, (*chip, 1 - c), self.me).wait_recv()
        for cp in self._first() + passed:
            cp.wait_send()
        acc = self.buf[0]
        for d in range(1, N_DEV):
            acc = acc + self.buf[d]
        self.total[...] = acc
        store = pltpu.make_async_copy(self.total, self.out, self.local_sem)
        store.start()
        store.wait()


def carried_allsum(v):
    rows, lanes = v.shape
    scratch = (pltpu.VMEM((N_DEV, rows, lanes), F32), pltpu.VMEM((rows, lanes), F32)) + _dma_sems(7, 7) + (
        pltpu.SemaphoreType.DMA,)
    return _Carried((v,), (_sds((rows, lanes)),), scratch, _AllSum, {})


class _Both:
    def __init__(self, *exchanges):
        self.exchanges = exchanges

    def start(self):
        for ex in self.exchanges:
            ex.start()

    def finish(self):
        for ex in self.exchanges:
            ex.finish()


def carried_both(a, b):
    ai, ao, asc = len(a.ins), len(a.out_shapes), len(a.scratch)

    def make(ins, outs, *scratch):
        return _Both(a.make(ins[:ai], outs[:ao], *scratch[:asc]), b.make(ins[ai:], outs[ao:], *scratch[asc:]))

    aliases = {**a.aliases, **{ai + i: ao + j for i, j in b.aliases.items()}}
    return _Carried(a.ins + b.ins, a.out_shapes + b.out_shapes, a.scratch + b.scratch, make, aliases)


BIG = ("w_in", "w_branch_a", "w_branch_b", "w_out", "w_up", "w_down")
CONV = ("conv_qkv", "conv_ffn")
REPL =("a_log", "dt_bias", "dn_norm_w", "sg_ln_g", "sg_ln_b", "w_spatial", "b_spatial", "ln1_g", "ln1_b", "ln2_g", "ln2_b")


def _pad_rows(flat, mult):
    n = flat.shape[0]
    unit = mult * LANES
    total = -(-n // unit) * unit
    return jnp.pad(flat, (0, total - n)).reshape(total // LANES, LANES)


def _pack(arrs, mult):
    return _pad_rows(jnp.concatenate([a.reshape(-1) for a in arrs]), mult)


def _unpack(flat, shapes):
    out, off = [], 0
    for shp in shapes:
        n = math.prod(shp)
        out.append(flat[off:off + n].reshape(shp))
        off += n
    return out


def kernel(x, w_in, conv_qkv, a_log, dt_bias, dn_norm_w, w_branch_a, sg_ln_g, sg_ln_b, w_spatial, b_spatial, w_branch_b, w_out, ln1_g, ln1_b, w_up, conv_ffn, w_down, ln2_g, ln2_b, loss_target, m_w_in, m_conv_qkv, m_a_log, m_dt_bias, m_dn_norm_w, m_w_branch_a, m_sg_ln_g, m_sg_ln_b, m_w_spatial, m_b_spatial, m_w_branch_b, m_w_out, m_ln1_g, m_ln1_b, m_w_up, m_conv_ffn, m_w_down, m_ln2_g, m_ln2_b, v_w_in, v_conv_qkv, v_a_log, v_dt_bias, v_dn_norm_w, v_w_branch_a, v_sg_ln_g, v_sg_ln_b, v_w_spatial, v_b_spatial, v_w_branch_b, v_w_out, v_ln1_g, v_ln1_b, v_w_up, v_conv_ffn, v_w_down, v_ln2_g, v_ln2_b):
    names = ("w_in", "conv_qkv", "a_log", "dt_bias", "dn_norm_w", "w_branch_a", "sg_ln_g", "sg_ln_b", "w_spatial",
             "b_spatial", "w_branch_b", "w_out", "ln1_g", "ln1_b", "w_up", "conv_ffn", "w_down", "ln2_g", "ln2_b")
    w = dict(zip(names, (w_in, conv_qkv, a_log, dt_bias, dn_norm_w, w_branch_a, sg_ln_g, sg_ln_b, w_spatial,
                         b_spatial, w_branch_b, w_out, ln1_g, ln1_b, w_up, conv_ffn, w_down, ln2_g, ln2_b)))
    m = dict(zip(names, (m_w_in, m_conv_qkv, m_a_log, m_dt_bias, m_dn_norm_w, m_w_branch_a, m_sg_ln_g, m_sg_ln_b,
                         m_w_spatial, m_b_spatial, m_w_branch_b, m_w_out, m_ln1_g, m_ln1_b, m_w_up, m_conv_ffn,
                         m_w_down, m_ln2_g, m_ln2_b)))
    v = dict(zip(names, (v_w_in, v_conv_qkv, v_a_log, v_dt_bias, v_dn_norm_w, v_w_branch_a, v_sg_ln_g, v_sg_ln_b,
                         v_w_spatial, v_b_spatial, v_w_branch_b, v_w_out, v_ln1_g, v_ln1_b, v_w_up, v_conv_ffn,
                         v_w_down, v_ln2_g, v_ln2_b)))
    chip = 2 * lax.axis_index("x") + lax.axis_index("y")
    s = x.shape[1]
    xs = x.reshape(s, D)
    tgt = loss_target.reshape(s, D)

    big_names, conv_names = list(BIG), list(CONV)

    def in_two(name, l):
        rows, cols = w[name].shape[1:]
        return w[name][l].astype(BF16).reshape(2, rows // 2, cols)

    def whole(name, landed):
        rows, cols = w[name].shape[1:]
        return landed.reshape(N_CHIPS, rows, cols)

    first = exchange(carried_gather([in_two("w_in", 0)] + [w[n] for n in conv_names]), "gather_first")
    got = [{"w_in": whole("w_in", first[0])}, {}]
    conv_taps = dict(zip(conv_names, first[1:]))
    narrow, wide = ["w_branch_a", "w_branch_b", "w_out"], ["w_up", "w_down"]
    carried = {"proj_fwd0": (0, narrow), "dn_fwd0": (0, wide), "ffn_fwd0": (1, ["w_in"] + narrow), "dn_fwd1": (1, wide)}

    def carry(call):
        if call not in carried:
            return None
        l, which = carried[call]
        return carried_gather([in_two(n, l) for n in which])

    def land(call, landed):
        l, which = carried.get(call, (0, []))
        got[l].update({n: whole(n, a) for n, a in zip(which, landed)})

    def lane_row(vec, off):
        return jnp.zeros((1, LANES), F32).at[0, off:off + vec.shape[0]].set(vec)

    def side_by_side(blocks):
        return jnp.concatenate([blocks[k] for k in range(N_CHIPS)], axis=1)

    def small_params(l):
        return dict(
            cq=side_by_side(conv_taps["conv_qkv"][:, l]),
            a_row=lane_row(w["a_log"][l], HEADS), dtb_row=lane_row(w["dt_bias"][l], HEADS),
            nw_row=w["dn_norm_w"][l].reshape(1, DK),
            lng=w["sg_ln_g"][l].reshape(1, SG_W), lnb=w["sg_ln_b"][l].reshape(1, SG_W),
            w_s=w["w_spatial"][l], bs_t=jnp.zeros((LANES, LANES), F32).at[:, :4].set(w["b_spatial"][l].T),
            g1=w["ln1_g"][l].reshape(1, D), b1=w["ln1_b"][l].reshape(1, D),
            cf=conv_taps["conv_ffn"][:, l],
            g2=w["ln2_g"][l].reshape(1, D), b2=w["ln2_b"][l].reshape(1, D))

    layers, saved = [], []
    h_in = xs
    for l in range(DEPTH):
        p = small_params(l)
        wi = side_by_side(got[l]["w_in"])
        p["w_in"] = jnp.concatenate([wi[:, :2048], wi[:, 2056:3080], wi[:, 3080:5128], wi[:, 2048:2056],
                                     jnp.zeros((D, IN_COLS_PAD - 5128), BF16)], axis=1)
        proj, *landed = proj_fwd(h_in, p["w_in"], f"proj_fwd{l}", carry=carry(f"proj_fwd{l}"))
        land(f"proj_fwd{l}", landed)
        oa, sst, *landed = dn_fwd(proj, p["cq"], p["a_row"], p["dtb_row"], p["nw_row"], f"dn_fwd{l}",
                                  carry=carry(f"dn_fwd{l}"))
        land(f"dn_fwd{l}", landed)
        ob = sg_fwd(proj, p["lng"], p["lnb"], p["w_s"], p["bs_t"], f"sg_fwd{l}")
        p.update(wa=side_by_side(got[l]["w_branch_a"]), wb=side_by_side(got[l]["w_branch_b"]),
                 wo=got[l]["w_out"].reshape(D, D))
        x1 = merge_fwd(oa, ob, proj, h_in, p["wa"], p["wb"], p["wo"], p["g1"], p["b1"], f"merge_fwd{l}")
        pre2, x2, *landed = ffn_fwd(x1, got[l]["w_up"], p["cf"], got[l]["w_down"], p["g2"], p["b2"], f"ffn_fwd{l}",
                                    carry=carry(f"ffn_fwd{l}"))
        land(f"ffn_fwd{l}", landed)
        layers.append(p)
        saved.append(dict(x=h_in, proj=proj, oa=oa, ob=ob, sst=sst, x1=x1, pre2=pre2))
        h_in = x2


    small_names = conv_names + list(REPL)
    grads = {n: [None] * DEPTH for n in small_names}
    c_vec = jnp.stack([lax.axis_index("c")]).astype(jnp.int32)
    tags = ("w_in", "w_a", "w_b", "w_out", "w_up0", "w_up1", "w_dn0", "w_dn1")
    groups = (1, 1, 1, N_CHIPS, 2, 2, 2, 2)
    ab_cols = w["w_branch_a"].shape[2]
    pieces = [
        [(0, (k,), None) for k in range(N_CHIPS)],
        [(1, (0,), (k * ab_cols, ab_cols)) for k in range(N_CHIPS)],
        [(2, (0,), (k * ab_cols, ab_cols)) for k in range(N_CHIPS)],
        [(3, (k,), None) for k in range(N_CHIPS)],
        [(4 + k % 2, (k // 2,), None) for k in range(N_CHIPS)],
        [(6 + k // 2, (k % 2,), None) for k in range(N_CHIPS)],
    ]
    part_shapes = [(w[n].shape[1] // 2, w[n].shape[2]) for n in big_names]
    arrays_of = ((0,), (1,), (2,), (3,), (4, 5), (6, 7))
    rest, ffn_part = (0, 1, 2, 3), (4, 5)
    bufs = {}

    def arrays(which):
        return [i for t in which for i in arrays_of[t]]

    def views(which, arrs):
        return [a.reshape(groups[i], 2, a.size // a.shape[-1] // (2 * groups[i]), a.shape[-1])
                for i, a in zip(arrays(which), arrs)]

    def pair_sums(l, which, mine, theirs):
        ids = arrays(which)
        sums = [pair_add_half(m_, t_, c_vec, f"reduce_pair_add{l}_{tags[i]}") for i, m_, t_ in zip(ids, mine, theirs)]
        if ids[0] == 0:
            pin = sums[0][0]
            natural = jnp.concatenate([pin[:, :2048], pin[:, C_BA:C_BA + 8], pin[:, 2048:C_BA]], axis=1)
            sums[0] = jnp.stack(jnp.split(natural, N_CHIPS, axis=1))
        return sums

    def scatter_of(which, srcs):
        place = {i: j for j, i in enumerate(arrays(which))}
        return carried_scatter(srcs, [[(place[i], lead, cols) for i, lead, cols in pieces[t]] for t in which],
                               [part_shapes[t] for t in which])

    def chip_sums(l, which, recv):
        for t, r in zip(which, recv):
            n = big_names[t]
            bufs[n] = chips_add_into(r, bufs.get(n), l, c_vec, f"reduce_chips_add{l}_{n}")
        return [bufs[big_names[t]] for t in which]

    def keep(which, joined):
        bufs.update({big_names[t]: b for t, b in zip(which, joined)})

    above = None
    for l in reversed(range(DEPTH)):
        p, a = layers[l], saved[l]
        if l == DEPTH - 1:
            dpre2, dg2, db2, loss_part = loss_ln_bwd(a["pre2"], tgt, p["g2"], p["b2"], "loss_ln2_bwd")
            loss = lax.psum(loss_part[0, 0], ("x", "y", "c"))
        else:
            dpre2, dg2, db2 = ln_bwd(a["pre2"], dy, p["g2"], p["b2"], f"ln2_bwd{l}")
        dx1, dwup0, dcf0, dwdn0, *theirs = ffn_bwd(
            a["x1"], dpre2, dpre2, ALPHA, got[l]["w_up"], p["cf"], got[l]["w_down"], 0, f"ffn_bwd{l}a",
            carry=carried_pair_swap(above) if above else None)
        srcs = pair_sums(l + 1, rest, above, theirs) if above else None
        dx1, dwup1, dcf1, dwdn1, *recv = ffn_bwd(
            a["x1"], dpre2, dx1, 1.0, got[l]["w_up"], p["cf"], got[l]["w_down"], 1, f"ffn_bwd{l}b",
            carry=scatter_of(rest, srcs) if above else None)
        summed = chip_sums(l + 1, rest, recv) if above else None
        doa, dob, dga, dgb, dxd, dwa, dwb, dwo, dg1, db1, *joined = merge_bwd(
            a["oa"], a["ob"], a["proj"], a["x"], dx1, p["wa"], p["wb"], p["wo"], p["g1"], p["b1"], f"merge_bwd{l}",
            carry=carried_join(summed, l + 1) if above else None)
        keep(rest, joined)
        mine = views(ffn_part, [dwup0, dwup1, dwdn0, dwdn1])
        duv, dlng, dlnb, dws, dbs, *theirs = sg_bwd(a["proj"], dob, p["lng"], p["lnb"], p["w_s"], p["bs_t"], f"sg_bwd{l}",
                                                    carry=carried_pair_swap(mine))
        srcs = pair_sums(l, ffn_part, mine, theirs)
        dqkv, dz, dba, dcq, da, ddtb, dnw, *recv = dn_bwd(a["proj"], a["sst"], doa, p["cq"], p["a_row"], p["dtb_row"],
                                                          p["nw_row"], f"dn_bwd{l}", carry=scatter_of(ffn_part, srcs))
        summed = chip_sums(l, ffn_part, recv)
        dy, *joined = proj_bwd([dqkv, dz, duv, dga, dgb, dba], dxd, p["w_in"], f"proj_bwd{l}",
                               carry=carried_join(summed, l))
        keep(ffn_part, joined)
        dwi = None
        for tag, dp, col in (("qkv", dqkv, 0), ("z", dz, C_Z), ("uv", duv, C_UV), ("ga", dga, C_GA), ("gb", dgb, C_GB),
                             ("ba", dba, C_BA)):
            dwi = wgrad(a["x"], dp, col, dwi, f"wgrad_in{l}_{tag}")

        above = views(rest, [dwi, dwa, dwb, dwo])
        grads["conv_qkv"][l] = dcq
        grads["conv_ffn"][l] = jnp.concatenate([dcf0[0], dcf1[0], dcf0[1], dcf1[1]], axis=1)
        grads["a_log"][l] = da[0, HEADS:2 * HEADS]
        grads["dt_bias"][l] = ddtb[0, HEADS:2 * HEADS]
        grads["dn_norm_w"][l] = dnw[0]
        grads["sg_ln_g"][l] = dlng[0]
        grads["sg_ln_b"][l] = dlnb[0]
        grads["w_spatial"][l] = dws
        grads["b_spatial"][l] = dbs[:, :4].T
        grads["ln1_g"][l] = dg1[0]
        grads["ln1_b"][l] = db1[0]
        grads["ln2_g"][l] = dg2[0]
        grads["ln2_b"][l] = db2[0]
    grad_x = dy.reshape(x.shape)
    g_full = {n: jnp.stack(grads[n]) for n in small_names}

    theirs = exchange(carried_pair_swap(above), "reduce_pair")
    *recv, small = exchange(carried_both(scatter_of(rest, pair_sums(0, rest, above, theirs)),
                                         carried_allsum(_pack([g_full[n] for n in small_names], 8))), "reduce_chips")
    keep(rest, exchange(carried_join(chip_sums(0, rest, recv), 0), "reduce_join"))
    g_shard = {n: bufs[n].reshape(w[n].shape) for n in big_names}

    small_full = dict(zip(small_names, _unpack(small.reshape(-1), [g_full[n].shape for n in small_names])))
    for n in conv_names:
        width = w[n].shape[2]
        g_shard[n] = lax.dynamic_slice_in_dim(small_full[n], chip * width, width, axis=2)
    for n in REPL:
        g_shard[n] = small_full[n]

    delta, new_m, new_v = {}, {}, {}
    for n in big_names:
        shp = w[n].shape
        two_d = (shp[0] * shp[1], shp[2])
        g_, d_, m_, v_ = adam_call(w[n].reshape(two_d), g_shard[n].reshape(two_d), m[n].reshape(two_d), v[n].reshape(two_d), f"adam_{n}")
        g_shard[n], delta[n], new_m[n], new_v[n] = g_.reshape(shp), d_.reshape(shp), m_.reshape(shp), v_.reshape(shp)
    shapes = [w[n].shape for n in small_names]
    packs = [_pack([src[n] for n in small_names], 8) for src in (w, g_shard, m, v)]
    outs = adam_call(*packs, "adam_small")
    for dst, o in zip((delta, new_m, new_v), outs[1:]):
        dst.update(zip(small_names, _unpack(o.reshape(-1), shapes)))

    return (loss, grad_x, *[g_shard[n] for n in names], *[delta[n] for n in names],
            *[new_m[n] for n in names], *[new_v[n] for n in names])
```

```python
import functools
import math
from typing import Callable, NamedTuple

import jax
import jax.numpy as jnp
from jax import lax
from jax.experimental import pallas as pl
from jax.experimental.pallas import tpu as pltpu

F32 = jnp.float32
BF16 = jnp.bfloat16
HI = lax.Precision.HIGHEST
MID = lax.Precision.HIGH
MESH = pl.DeviceIdType.MESH

D = 1024
DEPTH = 2
HEADS = 4
DK = 128
CHUNK = 64
QKV_W = 1536
Z_W = 512
SG_W = 512
FFN = 2816
FFN_HALF = FFN // 2
N_CHIPS = 4
DN_SHARD = FFN // N_CHIPS
LN_EPS = 1e-5
RMS_EPS = 1e-6
L2_EPS = 1e-6
ALPHA = (2 * DEPTH) ** 0.25
ADAM_LR, ADAM_B1, ADAM_B2, ADAM_EPS, ADAM_WD, ADAM_STEP = 0.001, 0.9, 0.999, 1e-08, 0.01, 10

HALO = 16
LANES = 128
IN_COLS_PAD = 5248
C_Z, C_UV, C_GA, C_GB, C_BA = 1536, 2048, 3072, 4096, 5120
VMEM_LIMIT = 56 * 1024 * 1024
DN_TILE = 256


def _params(n_grid=1):
    return pltpu.CompilerParams(dimension_semantics=("arbitrary",) * n_grid, vmem_limit_bytes=VMEM_LIMIT)


def _mm(a, b):
    return jnp.dot(a.astype(BF16), b.astype(BF16), preferred_element_type=F32)


def _mm_nt(a, b):
    return lax.dot_general(a.astype(BF16), b.astype(BF16), (((1,), (1,)), ((), ())), preferred_element_type=F32)


def _mm_tn(a, b):
    return lax.dot_general(a.astype(BF16), b.astype(BF16), (((0,), (0,)), ((), ())), preferred_element_type=F32)


def _bdot(a, b, prec=MID):
    return lax.dot_general(a, b, (((2,), (1,)), ((0,), (0,))), precision=prec, preferred_element_type=F32)


def _bdot_nt(a, b, prec=MID):
    return lax.dot_general(a, b, (((2,), (2,)), ((0,), (0,))), precision=prec, preferred_element_type=F32)


def _bf16_dot(a, b, contract):
    return lax.dot_general(a.astype(BF16), b.astype(BF16), (contract, ((0,), (0,))), preferred_element_type=F32)


@jax.custom_vjp
def _fdot(a, b):
    return _bf16_dot(a, b, ((2,), (1,)))


def _fdot_fwd(a, b):
    return _fdot(a, b), (a, b)


def _fdot_bwd(res, ct):
    a, b = res
    return _bf16_dot(ct, b, ((2,), (2,))), _bf16_dot(a, ct, ((1,), (1,)))


_fdot.defvjp(_fdot_fwd, _fdot_bwd)


@jax.custom_vjp
def _fdot_nt(a, b):
    return _bf16_dot(a, b, ((2,), (2,)))


def _fdot_nt_fwd(a, b):
    return _fdot_nt(a, b), (a, b)


def _fdot_nt_bwd(res, ct):
    a, b = res
    return _bf16_dot(ct, b, ((2,), (1,))), _bf16_dot(ct, a, ((1,), (1,)))


_fdot_nt.defvjp(_fdot_nt_fwd, _fdot_nt_bwd)


@jax.custom_vjp
def _fdot_tn(a, b):
    return _bf16_dot(a, b, ((1,), (1,)))


def _fdot_tn_fwd(a, b):
    return _fdot_tn(a, b), (a, b)


def _fdot_tn_bwd(res, ct):
    a, b = res
    return _bf16_dot(b, ct, ((2,), (2,))), _bf16_dot(a, ct, ((2,), (1,)))


_fdot_tn.defvjp(_fdot_tn_fwd, _fdot_tn_bwd)


def _stack(parts):
    return jnp.concatenate([p[None] for p in parts], axis=0)


def _ln(x, g, b):
    mu = jnp.mean(x, axis=-1, keepdims=True)
    xc = x - mu
    var = jnp.mean(xc * xc, axis=-1, keepdims=True)
    return xc * lax.rsqrt(var + LN_EPS) * g + b


def _shift_rows(x, s):
    s = s % x.shape[0]
    return x if s == 0 else pltpu.roll(x, s, 0)


@jax.custom_vjp
def _conv(xcat, w):
    k_taps = len(w)
    y = None
    for k in range(k_taps):
        t = _shift_rows(xcat, k_taps - 1 - k)[HALO:] * w[k]
        y = t if y is None else y + t
    return y


def _conv_fwd(xcat, w):
    return _conv(xcat, w), (xcat, w)


def _conv_bwd(res, dy):
    xcat, w = res
    k_taps = len(w)
    dyp = jnp.concatenate([jnp.zeros((HALO, dy.shape[1]), dy.dtype), dy], axis=0)
    dx = None
    dws = []
    for k in range(k_taps):
        shifted = _shift_rows(dyp, -(k_taps - 1 - k))
        t = shifted * w[k]
        dx = t if dx is None else dx + t
        dws.append(jnp.sum(shifted * xcat, axis=0, keepdims=True))
    return dx, tuple(dws)


_conv.defvjp(_conv_fwd, _conv_bwd)


@jax.custom_vjp
def _tri_inv(l):
    n = l.shape[-1]
    r = lax.broadcasted_iota(jnp.int32, (n, n), 0)
    c = lax.broadcasted_iota(jnp.int32, (n, n), 1)
    eye = (r == c).astype(F32)
    p = eye - l
    lp = l
    steps = int(math.log2(n)) - 1
    for i in range(steps):
        dot = _bdot if i < 2 else functools.partial(_bf16_dot, contract=((2,), (1,)))
        lp = dot(lp, lp)
        p = p + dot(p, lp)
    return p


def _tri_inv_fwd(l):
    t = _tri_inv(l)
    return t, t


def _tri_inv_bwd(t, dt):
    tt = jnp.swapaxes(t, 1, 2)
    return (-_bdot(tt, _bdot(dt, tt)),)


_tri_inv.defvjp(_tri_inv_fwd, _tri_inv_bwd)


@jax.custom_vjp
def _tri_inv_saved(l, t):
    return t


def _tri_inv_saved_fwd(l, t):
    return t, t


def _tri_inv_saved_bwd(t, dt):
    return _tri_inv_bwd(t, dt) + (jnp.zeros_like(t),)


_tri_inv_saved.defvjp(_tri_inv_saved_fwd, _tri_inv_saved_bwd)


def _dn_glue(qkvcat, z, ba, s_in, cw, a_row, dtb_row, nw_row, t_saved=None):
    t_rows = z.shape[0]
    nc = t_rows // CHUNK
    nb = nc * HEADS

    qkv = jax.nn.silu(_conv(qkvcat, cw))

    def chunks(t, off):
        return _stack([t[n * CHUNK:(n + 1) * CHUNK, off + h * DK: off + (h + 1) * DK]
                       for n in range(nc) for h in range(HEADS)])

    q = chunks(qkv, 0)
    k = chunks(qkv, 512)
    v = chunks(qkv, 1024)
    q = q * lax.rsqrt(jnp.sum(q * q, axis=-1, keepdims=True) + L2_EPS) * (DK ** -0.5)
    k = k * lax.rsqrt(jnp.sum(k * k, axis=-1, keepdims=True) + L2_EPS)

    lane = lax.broadcasted_iota(jnp.int32, (LANES, HEADS * DK), 0)
    head_of_col = lax.broadcasted_iota(jnp.int32, (LANES, HEADS * DK), 1) // DK
    e_beta = (head_of_col == lane).astype(F32)
    e_g = (head_of_col + HEADS == lane).astype(F32)
    beta_l = jax.nn.sigmoid(ba)
    g_l = -jnp.exp(a_row) * jax.nn.softplus(ba + dtb_row)
    beta = chunks(jnp.dot(beta_l, e_beta, precision=MID, preferred_element_type=F32), 0)
    g = chunks(jnp.dot(g_l, e_g, precision=MID, preferred_element_type=F32), 0)

    r = lax.broadcasted_iota(jnp.int32, (CHUNK, CHUNK), 0)
    c = lax.broadcasted_iota(jnp.int32, (CHUNK, CHUNK), 1)
    causal = r >= c
    strict = r > c
    tril_b = jnp.broadcast_to(causal.astype(F32), (nb, CHUNK, CHUNK))
    gi_b = _bdot(tril_b, g, HI)
    gi = gi_b[:, :, :CHUNK]
    gj = jnp.swapaxes(gi, 1, 2)
    decay = jnp.where(causal, jnp.exp(jnp.where(causal, gi - gj, 0.0)), 0.0)
    kb = k * beta
    l_mat = jnp.where(strict, _fdot_nt(kb, k) * decay, 0.0)
    t_mat = _tri_inv(l_mat) if t_saved is None else _tri_inv_saved(l_mat, t_saved)
    e_gi = jnp.exp(gi_b)
    w_mat = _fdot(t_mat, kb * e_gi)
    u_mat = _fdot(t_mat, v * beta)
    a_qk = _fdot_nt(q, k) * decay
    q_g = q * e_gi
    gl_b = jnp.broadcast_to(jnp.sum(g, axis=1, keepdims=True), g.shape)
    k_d = k * jnp.exp(gl_b - gi_b)
    e_gl = jnp.exp(gl_b)
    g_last = jnp.concatenate([e_gl, e_gl], axis=1)

    state = s_in
    rows = []
    for n in range(nc):
        sl = slice(n * HEADS, (n + 1) * HEADS)
        u_new = u_mat[sl] - _fdot(w_mat[sl], state)
        o_n = _fdot(q_g[sl], state) + _fdot(a_qk[sl], u_new)
        state = state * g_last[sl] + _fdot_tn(k_d[sl], u_new)
        o_n = o_n * lax.rsqrt(jnp.mean(o_n * o_n, axis=-1, keepdims=True) + RMS_EPS) * nw_row
        z_n = _stack([z[n * CHUNK:(n + 1) * CHUNK, h * DK:(h + 1) * DK] for h in range(HEADS)])
        o_n = o_n * jax.nn.silu(z_n)
        rows.append(jnp.concatenate([o_n[h] for h in range(HEADS)], axis=-1))
    return jnp.concatenate(rows, axis=0), state, t_mat


def _sg_glue(uv, lng, lnb, w_s, bs_t):
    t_rows = uv.shape[0]
    y = jax.nn.gelu(uv)
    u = y[:, :SG_W]
    v = _ln(y[:, SG_W:], lng, lnb)
    r = lax.broadcasted_iota(jnp.int32, (LANES, LANES), 0)
    c = lax.broadcasted_iota(jnp.int32, (LANES, LANES), 1)
    wm = jnp.where(r >= c, w_s, 0.0)
    lane = lax.broadcasted_iota(jnp.int32, (LANES, SG_W), 0)
    group_of_col = lax.broadcasted_iota(jnp.int32, (LANES, SG_W), 1) // LANES
    e_grp = (group_of_col == lane).astype(F32)
    bias = jnp.dot(bs_t, e_grp, precision=HI, preferred_element_type=F32)
    outs = []
    for n in range(t_rows // LANES):
        vb = v[n * LANES:(n + 1) * LANES]
        vg = _stack([vb[:, g * LANES:(g + 1) * LANES] for g in range(4)])
        mg = _fdot(wm, vg)
        mixed = jnp.concatenate([mg[g] for g in range(4)], axis=-1) + bias
        outs.append(u[n * LANES:(n + 1) * LANES] * mixed)
    return jnp.concatenate(outs, axis=0)


def _merge_glue(ga, gb, ya, yb):
    return jax.nn.sigmoid(ga) * ya + jax.nn.sigmoid(gb) * yb


def _res_ln_glue(x, r, g, b):
    return _ln(ALPHA * x + r, g, b)


def _ffn_glue(ua, ub, cwa, cwb):
    return jax.nn.silu(_conv(ua, cwa)) * _conv(ub, cwb)


def _row(t, c, col=0):
    return pl.BlockSpec((t, c), lambda i: (i, col))


def _row_rev(t, c, nt, col=0):
    return pl.BlockSpec((t, c), lambda i: (nt - 1 - i, col))


def _halo(t, c, nt=None):
    per = t // HALO
    if nt is None:
        return pl.BlockSpec((HALO, c), lambda i: (jnp.maximum(i * per - 1, 0), 0))
    return pl.BlockSpec((HALO, c), lambda i: (jnp.maximum((nt - 1 - i) * per - 1, 0), 0))


def _full(shape):
    nd = len(shape)
    return pl.BlockSpec(shape, lambda i: (0,) * nd)


ANY = pl.BlockSpec(memory_space=pl.ANY)


def _sds(shape, dtype=F32):
    return jax.ShapeDtypeStruct(shape, dtype)


def _tile(s, want=256):
    for t in (want, 256, 128):
        if s % t == 0:
            return t
    raise ValueError(f"sequence length {s} is not a multiple of 128")


def proj_fwd(x, w, name, carry=None):
    s = x.shape[0]
    t = _tile(s, 512)
    nt = s // t
    segs = [(0, 2048), (2048, 3072), (3072, 4096), (4096, 5120), (5120, IN_COLS_PAD)]

    def body(x_ref, w_ref, p_ref):
        xb = x_ref[...].astype(BF16)
        for lo, hi in segs:
            p_ref[:, lo:hi] = jnp.dot(xb, w_ref[:, lo:hi], preferred_element_type=F32)

    return _host_call(
        body, carry, nt, grid=(nt,), name=name, in_specs=[_row(t, D), _full((D, IN_COLS_PAD))],
        out_specs=[_row(t, IN_COLS_PAD)], out_shape=[_sds((s, IN_COLS_PAD))], scratch_shapes=[], operands=(x, w))


def dn_fwd(p, cq, a_row, dtb_row, nw_row, name, carry=None):
    s = p.shape[0]
    t = _tile(s, DN_TILE)
    nt = s // t

    nb = HEADS * t // CHUNK

    def body(qkv_ref, halo_ref, z_ref, ba_ref, cq_ref, a_ref, dtb_ref, nw_ref, o_ref, sst_ref, tinv_ref, s_scr):
        i = pl.program_id(0)

        @pl.when(i == 0)
        def _():
            s_scr[...] = jnp.zeros_like(s_scr)

        halo = jnp.where(i == 0, 0.0, halo_ref[...])
        qkvcat = jnp.concatenate([halo, qkv_ref[...]], axis=0)
        cw = tuple(cq_ref[k:k + 1, :] for k in range(4))
        s_in = s_scr[...]
        sst_ref[0] = s_in
        o, s_out, t_mat = _dn_glue(qkvcat, z_ref[...], ba_ref[...], s_in, cw, a_ref[...], dtb_ref[...], nw_ref[...])
        o_ref[...] = o.astype(BF16)
        tinv_ref[0] = t_mat
        s_scr[...] = s_out

    return _host_call(
        body, carry, nt, grid=(nt,), name=name,
        in_specs=[_row(t, QKV_W), _halo(t, QKV_W), _row(t, Z_W, C_Z // Z_W), _row(t, LANES, C_BA // LANES),
                  _full((4, QKV_W)), _full((1, LANES)), _full((1, LANES)), _full((1, LANES))],
        out_specs=[_row(t, Z_W), pl.BlockSpec((1, HEADS, DK, DK), lambda i: (i, 0, 0, 0)),
                   pl.BlockSpec((1, nb, CHUNK, CHUNK), lambda i: (i, 0, 0, 0))],
        out_shape=[_sds((s, Z_W), BF16), _sds((nt, HEADS, DK, DK)), _sds((nt, nb, CHUNK, CHUNK))],
        scratch_shapes=[pltpu.VMEM((HEADS, DK, DK), F32)], operands=(p, p, p, p, cq, a_row, dtb_row, nw_row))


def sg_fwd(p, lng, lnb, w_s, bs_t, name):
    s = p.shape[0]
    t = _tile(s, 512)

    def body(uv_ref, lng_ref, lnb_ref, ws_ref, bs_ref, o_ref):
        o_ref[...] = _sg_glue(uv_ref[...], lng_ref[...], lnb_ref[...], ws_ref[...], bs_ref[...]).astype(BF16)

    return pl.pallas_call(
        body, grid=(s // t,), name=name,
        in_specs=[_row(t, 2 * SG_W, C_UV // (2 * SG_W)), _full((1, SG_W)), _full((1, SG_W)),
                  _full((4, LANES, LANES)), _full((LANES, LANES))],
        out_specs=_row(t, SG_W), out_shape=_sds((s, SG_W), BF16), compiler_params=_params())(p, lng, lnb, w_s, bs_t)


def merge_fwd(oa, ob, p, x, wa, wb, wo, g1, b1, name):
    s = x.shape[0]
    t = _tile(s, 512)

    def body(oa_ref, ob_ref, ga_ref, gb_ref, x_ref, wa_ref, wb_ref, wo_ref, g_ref, b_ref, x1_ref):
        ya = _mm(oa_ref[...], wa_ref[...])
        yb = _mm(ob_ref[...], wb_ref[...])
        h = _merge_glue(ga_ref[...], gb_ref[...], ya, yb)
        x1_ref[...] = _res_ln_glue(x_ref[...], _mm(h, wo_ref[...]), g_ref[...], b_ref[...])

    return pl.pallas_call(
        body, grid=(s // t,), name=name,
        in_specs=[_row(t, Z_W), _row(t, SG_W), _row(t, D, C_GA // D), _row(t, D, C_GB // D), _row(t, D),
                  _full((Z_W, D)), _full((SG_W, D)), _full((D, D)), _full((1, D)), _full((1, D))],
        out_specs=_row(t, D), out_shape=_sds((s, D)), compiler_params=_params())(oa, ob, p, p, x, wa, wb, wo, g1, b1)


def _load_ffn_weights(wup_hbm, wdn_hbm, wup_v, wdn_v, up_slots, dn_slots):
    for n, k in enumerate(up_slots):
        pltpu.sync_copy(wup_hbm.at[k], wup_v.at[n])
    for n, k in enumerate(dn_slots):
        pltpu.sync_copy(wdn_hbm.at[k], wdn_v.at[pl.ds(n * DN_SHARD, DN_SHARD)])


def ffn_fwd(x1, wup4, cf4, wdn4, g2, b2, name, carry=None):
    s = x1.shape[0]
    t = _tile(s, 512)
    nt = s // t

    def body(x1_ref, halo_ref, wup_hbm, cf_ref, wdn_hbm, g_ref, b_ref, pre_ref, x2_ref, wup_v, wdn_v):
        i = pl.program_id(0)

        @pl.when(i == 0)
        def _():
            _load_ffn_weights(wup_hbm, wdn_hbm, wup_v, wdn_v, range(4), range(4))

        x1v = x1_ref[...]
        halo = jnp.where(i == 0, 0.0, halo_ref[...])
        x1cat = jnp.concatenate([halo, x1v], axis=0).astype(BF16)
        f = None
        for h in range(2):
            ua = jnp.dot(x1cat, wup_v[h], preferred_element_type=F32)
            ub = jnp.dot(x1cat, wup_v[2 + h], preferred_element_type=F32)
            cwa = tuple(cf_ref[h, k:k + 1, :] for k in range(3))
            cwb = tuple(cf_ref[2 + h, k:k + 1, :] for k in range(3))
            act = _ffn_glue(ua, ub, cwa, cwb)
            fh = _mm(act, wdn_v[h * FFN_HALF:(h + 1) * FFN_HALF, :])
            f = fh if f is None else f + fh
        pre = ALPHA * x1v + f
        pre_ref[...] = pre
        x2_ref[...] = _ln(pre, g_ref[...], b_ref[...])

    return _host_call(
        body, carry, nt, grid=(nt,), name=name,
        in_specs=[_row(t, D), _halo(t, D), ANY, _full((4, 3, FFN_HALF)), ANY, _full((1, D)), _full((1, D))],
        out_specs=[_row(t, D), _row(t, D)], out_shape=[_sds((s, D)), _sds((s, D))],
        scratch_shapes=[pltpu.VMEM((4, D, FFN_HALF), BF16), pltpu.VMEM((FFN, D), BF16)],
        operands=(x1, x1, wup4, cf4, wdn4, g2, b2))


def loss_ln_bwd(pre, tgt, g, b, name):
    s = pre.shape[0]
    t = _tile(s, 512)

    def body(pre_ref, t_ref, g_ref, b_ref, dpre_ref, dg_ref, db_ref, loss_ref):
        first = pl.program_id(0) == 0
        y, vjp = jax.vjp(_ln, pre_ref[...], g_ref[...], b_ref[...])
        e = y - t_ref[...]
        dpre, dg, db = vjp(e * (1.0 / D))
        dpre_ref[...] = dpre
        _acc(dg_ref, dg, first)
        _acc(db_ref, db, first)
        part = jnp.sum(jnp.sum(e * e, axis=1, keepdims=True), axis=0, keepdims=True) * (0.5 / D)
        _acc(loss_ref, jnp.broadcast_to(part, loss_ref.shape), first)

    return pl.pallas_call(
        body, grid=(s // t,), name=name, in_specs=[_row(t, D), _row(t, D), _full((1, D)), _full((1, D))],
        out_specs=[_row(t, D), _full((1, D)), _full((1, D)), _full((8, LANES))],
        out_shape=[_sds((s, D)), _sds((1, D)), _sds((1, D)), _sds((8, LANES))], compiler_params=_params())(pre, tgt, g, b)


def _acc(ref, val, first):
    @pl.when(first)
    def _():
        ref[...] = val

    @pl.when(jnp.logical_not(first))
    def _():
        ref[...] += val


def _acc_tn(acc_ref, a, b, first, seg):
    n = b.shape[1]
    for lo in range(0, n, seg):
        hi = min(lo + seg, n)
        _acc(acc_ref.at[:, lo:hi], _mm_tn(a, b[:, lo:hi]), first)


def ln_bwd(pre, dy, g, b, name):
    s = pre.shape[0]
    t = _tile(s, 512)

    def body(pre_ref, dy_ref, g_ref, b_ref, dpre_ref, dg_ref, db_ref):
        _, vjp = jax.vjp(_ln, pre_ref[...], g_ref[...], b_ref[...])
        dpre, dg, db = vjp(dy_ref[...])
        dpre_ref[...] = dpre
        first = pl.program_id(0) == 0
        _acc(dg_ref, dg, first)
        _acc(db_ref, db, first)

    return pl.pallas_call(
        body, grid=(s // t,), name=name, in_specs=[_row(t, D), _row(t, D), _full((1, D)), _full((1, D))],
        out_specs=[_row(t, D), _full((1, D)), _full((1, D))],
        out_shape=[_sds((s, D)), _sds((1, D)), _sds((1, D))], compiler_params=_params())(pre, dy, g, b)


def ffn_bwd(x1, df, acc_in, acc_scale, wup4, cf4, wdn4, h, name, carry=None):
    s = x1.shape[0]
    t = _tile(s)
    nt = s // t

    def body(x1_ref, halo_ref, df_ref, acc_ref, wup_hbm, cf_ref, wdn_hbm,
             dx1_ref, dwup_hbm, dcf_ref, dwdn_hbm, wup_v, wdn_v, dwup_v, dwdn_v, carry):
        i = pl.program_id(0)
        j = nt - 1 - i
        first = i == 0

        @pl.when(first)
        def _():
            _load_ffn_weights(wup_hbm, wdn_hbm, wup_v, wdn_v, (h, 2 + h), (2 * h, 2 * h + 1))
            carry[...] = jnp.zeros_like(carry)

        halo = jnp.where(j == 0, 0.0, halo_ref[...])
        x1cat = jnp.concatenate([halo, x1_ref[...]], axis=0).astype(BF16)
        ua = jnp.dot(x1cat, wup_v[0], preferred_element_type=F32)
        ub = jnp.dot(x1cat, wup_v[1], preferred_element_type=F32)
        cwa = tuple(cf_ref[h, k:k + 1, :] for k in range(3))
        cwb = tuple(cf_ref[2 + h, k:k + 1, :] for k in range(3))
        act, vjp = jax.vjp(_ffn_glue, ua, ub, cwa, cwb)
        dfb = df_ref[...].astype(BF16)
        dact = _mm_nt(dfb, wdn_v[...])
        _acc_tn(dwdn_v, act.astype(BF16), dfb, first, 512)
        dua, dub, dcwa, dcwb = vjp(dact)
        x1b = x1cat[HALO:]
        dups = []
        for n, du in enumerate((dua, dub)):
            dups.append(jnp.concatenate([du[HALO:t], du[t:] + carry[n]], axis=0).astype(BF16))
            carry[n] = du[:HALO]
            _acc(dwup_v.at[n], _mm_tn(x1b, dups[n]), first)
        for k in range(3):
            _acc(dcf_ref.at[0, k:k + 1, :], dcwa[k], first)
            _acc(dcf_ref.at[1, k:k + 1, :], dcwb[k], first)
        dx1_ref[...] = acc_scale * acc_ref[...] + _mm_nt(dups[0], wup_v[0]) + _mm_nt(dups[1], wup_v[1])

        @pl.when(i == nt - 1)
        def _():
            pltpu.sync_copy(dwup_v, dwup_hbm)
            pltpu.sync_copy(dwdn_v, dwdn_hbm)

    return _host_call(
        body, carry, nt, grid=(nt,), name=name,
        in_specs=[_row_rev(t, D, nt), _halo(t, D, nt), _row_rev(t, D, nt), _row_rev(t, D, nt),
                  ANY, _full((4, 3, FFN_HALF)), ANY],
        out_specs=[_row_rev(t, D, nt), ANY, _full((2, 3, FFN_HALF)), ANY],
        out_shape=[_sds((s, D)), _sds((2, D, FFN_HALF)), _sds((2, 3, FFN_HALF)), _sds((FFN_HALF, D))],
        scratch_shapes=[pltpu.VMEM((2, D, FFN_HALF), BF16), pltpu.VMEM((FFN_HALF, D), BF16),
                        pltpu.VMEM((2, D, FFN_HALF), F32), pltpu.VMEM((FFN_HALF, D), F32),
                        pltpu.VMEM((2, HALO, FFN_HALF), F32)],
        operands=(x1, x1, df, acc_in, wup4, cf4, wdn4))


def merge_bwd(oa, ob, p, x, dx1, wa, wb, wo, g1, b1, name, carry=None):
    s = x.shape[0]
    t = _tile(s)

    def body(oa_ref, ob_ref, ga_ref, gb_ref, x_ref, dx1_ref, wa_ref, wb_ref, wo_ref, g_ref, b_ref,
             doa_ref, dob_ref, dga_ref, dgb_ref, dx_ref, dwa_ref, dwb_ref, dwo_ref, dg_ref, db_ref):
        first = pl.program_id(0) == 0
        oa = oa_ref[...]
        ob = ob_ref[...]
        ya = _mm(oa, wa_ref[...])
        yb = _mm(ob, wb_ref[...])
        h, vjp1 = jax.vjp(_merge_glue, ga_ref[...], gb_ref[...], ya, yb)
        hb = h.astype(BF16)
        r = _mm(hb, wo_ref[...])
        _, vjp2 = jax.vjp(_res_ln_glue, x_ref[...], r, g_ref[...], b_ref[...])
        dx, dr, dg, db = vjp2(dx1_ref[...])
        dx_ref[...] = dx
        _acc(dg_ref, dg, first)
        _acc(db_ref, db, first)
        drb = dr.astype(BF16)
        dh = _mm_nt(drb, wo_ref[...])
        _acc(dwo_ref, _mm_tn(hb, drb), first)
        dga, dgb, dya, dyb = vjp1(dh)
        dga_ref[...] = dga.astype(BF16)
        dgb_ref[...] = dgb.astype(BF16)
        dyab = dya.astype(BF16)
        dybb = dyb.astype(BF16)
        doa_ref[...] = _mm_nt(dyab, wa_ref[...]).astype(BF16)
        dob_ref[...] = _mm_nt(dybb, wb_ref[...]).astype(BF16)
        _acc(dwa_ref, _mm_tn(oa, dyab), first)
        _acc(dwb_ref, _mm_tn(ob, dybb), first)

    return _host_call(
        body, carry, s // t, grid=(s // t,), name=name,
        in_specs=[_row(t, Z_W), _row(t, SG_W), _row(t, D, C_GA // D), _row(t, D, C_GB // D), _row(t, D), _row(t, D),
                  _full((Z_W, D)), _full((SG_W, D)), _full((D, D)), _full((1, D)), _full((1, D))],
        out_specs=[_row(t, Z_W), _row(t, SG_W), _row(t, D), _row(t, D), _row(t, D),
                   _full((Z_W, D)), _full((SG_W, D)), _full((D, D)), _full((1, D)), _full((1, D))],
        out_shape=[_sds((s, Z_W), BF16), _sds((s, SG_W), BF16), _sds((s, D), BF16), _sds((s, D), BF16), _sds((s, D)),
                   _sds((Z_W, D)), _sds((SG_W, D)), _sds((D, D)), _sds((1, D)), _sds((1, D))],
        scratch_shapes=[], operands=(oa, ob, p, p, x, dx1, wa, wb, wo, g1, b1))


def sg_bwd(p, dob, lng, lnb, w_s, bs_t, name, carry=None):
    s = p.shape[0]
    t = _tile(s, 512)

    def body(uv_ref, dob_ref, lng_ref, lnb_ref, ws_ref, bs_ref, duv_ref, dlng_ref, dlnb_ref, dws_ref, dbs_ref):
        first = pl.program_id(0) == 0
        _, vjp = jax.vjp(_sg_glue, uv_ref[...], lng_ref[...], lnb_ref[...], ws_ref[...], bs_ref[...])
        duv, dlng, dlnb, dws, dbs = vjp(dob_ref[...].astype(F32))
        duv_ref[...] = duv.astype(BF16)
        _acc(dlng_ref, dlng, first)
        _acc(dlnb_ref, dlnb, first)
        _acc(dws_ref, dws, first)
        _acc(dbs_ref, dbs, first)

    return _host_call(
        body, carry, s // t, grid=(s // t,), name=name,
        in_specs=[_row(t, 2 * SG_W, C_UV // (2 * SG_W)), _row(t, SG_W), _full((1, SG_W)), _full((1, SG_W)),
                  _full((4, LANES, LANES)), _full((LANES, LANES))],
        out_specs=[_row(t, 2 * SG_W), _full((1, SG_W)), _full((1, SG_W)), _full((4, LANES, LANES)), _full((LANES, LANES))],
        out_shape=[_sds((s, 2 * SG_W), BF16), _sds((1, SG_W)), _sds((1, SG_W)), _sds((4, LANES, LANES)), _sds((LANES, LANES))],
        scratch_shapes=[], operands=(p, dob, lng, lnb, w_s, bs_t))


def dn_bwd(p, sst, tinv, doa, cq, a_row, dtb_row, nw_row, name, carry=None):
    s = p.shape[0]
    t = _tile(s, DN_TILE)
    nt = s // t

    def body(qkv_ref, halo_ref, z_ref, ba_ref, sst_ref, tinv_ref, doa_ref, cq_ref, a_ref, dtb_ref, nw_ref,
             dqkv_ref, dz_ref, dba_ref, dcq_ref, da_ref, ddtb_ref, dnw_ref, ds_scr, carry):
        i = pl.program_id(0)
        j = nt - 1 - i
        first = i == 0

        @pl.when(first)
        def _():
            ds_scr[...] = jnp.zeros_like(ds_scr)
            carry[...] = jnp.zeros_like(carry)

        halo = jnp.where(j == 0, 0.0, halo_ref[...])
        qkvcat = jnp.concatenate([halo, qkv_ref[...]], axis=0)
        cw = tuple(cq_ref[k:k + 1, :] for k in range(4))
        t_saved = tinv_ref[0]
        _, vjp = jax.vjp(lambda *args: _dn_glue(*args, t_saved=t_saved)[:2],
                         qkvcat, z_ref[...], ba_ref[...], sst_ref[0], cw, a_ref[...], dtb_ref[...], nw_ref[...])
        dqkvcat, dz, dba, ds_in, dcw, da, ddtb, dnw = vjp((doa_ref[...].astype(F32), ds_scr[...]))
        ds_scr[...] = ds_in
        dz_ref[...] = dz.astype(BF16)
        dba_ref[...] = dba.astype(BF16)
        dtile = dqkvcat[HALO:]
        dqkv_ref[...] = dtile.astype(BF16)
        dqkv_ref[t - HALO:t, :] = (dtile[t - HALO:] + carry[...]).astype(BF16)
        carry[...] = dqkvcat[:HALO]
        for k in range(4):
            _acc(dcq_ref.at[k:k + 1, :], dcw[k], first)
        _acc(da_ref, da, first)
        _acc(ddtb_ref, ddtb, first)
        _acc(dnw_ref, dnw, first)

    return _host_call(
        body, carry, nt, grid=(nt,), name=name,
        in_specs=[_row_rev(t, QKV_W, nt), _halo(t, QKV_W, nt), _row_rev(t, Z_W, nt, C_Z // Z_W),
                  _row_rev(t, LANES, nt, C_BA // LANES),
                  pl.BlockSpec((1, HEADS, DK, DK), lambda i: (nt - 1 - i, 0, 0, 0)),
                  pl.BlockSpec((1,) + tinv.shape[1:], lambda i: (nt - 1 - i, 0, 0, 0)), _row_rev(t, Z_W, nt),
                  _full((4, QKV_W)), _full((1, LANES)), _full((1, LANES)), _full((1, LANES))],
        out_specs=[_row_rev(t, QKV_W, nt), _row_rev(t, Z_W, nt), _row_rev(t, LANES, nt),
                   _full((4, QKV_W)), _full((1, LANES)), _full((1, LANES)), _full((1, LANES))],
        out_shape=[_sds((s, QKV_W), BF16), _sds((s, Z_W), BF16), _sds((s, LANES), BF16),
                   _sds((4, QKV_W)), _sds((1, LANES)), _sds((1, LANES)), _sds((1, LANES))],
        scratch_shapes=[pltpu.VMEM((HEADS, DK, DK), F32), pltpu.VMEM((HALO, QKV_W), F32)],
        operands=(p, p, p, p, sst, tinv, doa, cq, a_row, dtb_row, nw_row))


def proj_bwd(dps, dxd, w, name, carry=None):
    s = dxd.shape[0]
    t = _tile(s, 512)
    n = len(dps)

    def body(*refs):
        dp_refs, dxd_ref, w_hbm, dx_ref, w_v = refs[:n], refs[n], refs[n + 1], refs[n + 2], refs[n + 3]

        @pl.when(pl.program_id(0) == 0)
        def _():
            pltpu.sync_copy(w_hbm, w_v)

        dp = jnp.concatenate([r[...] for r in dp_refs], axis=1)
        dx_ref[...] = dxd_ref[...] + _mm_nt(dp, w_v[...])

    return _host_call(
        body, carry, s // t, grid=(s // t,), name=name,
        in_specs=[_row(t, dp.shape[1]) for dp in dps] + [_row(t, D), ANY],
        out_specs=[_row(t, D)], out_shape=[_sds((s, D))],
        scratch_shapes=[pltpu.VMEM((D, IN_COLS_PAD), BF16)], operands=(*dps, dxd, w))


def wgrad(x, dp, col, into, name):
    s, n = dp.shape
    tk = _tile(s, 1024)
    tn = next(c for c in (1024, 768, 512, 256, 128) if n % c == 0 and col % c == 0)
    block = col // tn

    def body(x_ref, dp_ref, *rest):
        o_ref = rest[-1]
        _acc(o_ref, _mm_tn(x_ref[...], dp_ref[...]), pl.program_id(1) == 0)

    operands = (x, dp) if into is None else (x, dp, into)
    return pl.pallas_call(
        body, grid=(n // tn, s // tk), name=name,
        in_specs=[pl.BlockSpec((tk, D), lambda j, k: (k, 0)), pl.BlockSpec((tk, tn), lambda j, k: (k, j))]
        + ([] if into is None else [ANY]),
        out_specs=pl.BlockSpec((D, tn), lambda j, k: (0, block + j)), out_shape=_sds((D, IN_COLS_PAD)),
        input_output_aliases={} if into is None else {2: 0},
        compiler_params=_params(2))(*operands)


def _rows_block(rows, cols):
    cap = max(HALO, (2 * 1024 * 1024) // (cols * 4))
    for cand in range(min(rows, cap) // HALO * HALO, HALO - 1, -HALO):
        if rows % cand == 0:
            return cand
    return rows


def adam_call(w, g, m, v, name):
    rows, cols = w.shape
    tr = _rows_block(rows, cols)
    c1 = 1.0 - ADAM_B1 ** ADAM_STEP
    c2 = 1.0 - ADAM_B2 ** ADAM_STEP

    def body(w_ref, g_ref, m_ref, v_ref, go_ref, d_ref, nm_ref, nv_ref):
        gv = g_ref[...]
        go_ref[...] = gv
        nm = ADAM_B1 * m_ref[...] + (1.0 - ADAM_B1) * gv
        nv = ADAM_B2 * v_ref[...] + (1.0 - ADAM_B2) * (gv * gv)
        d_ref[...] = -ADAM_LR * ((nm / c1) / (jnp.sqrt(nv / c2) + ADAM_EPS) + ADAM_WD * w_ref[...])
        nm_ref[...] = nm
        nv_ref[...] = nv

    spec = pl.BlockSpec((tr, cols), lambda i: (i, 0))
    return pl.pallas_call(
        body, grid=(rows // tr,), name=name, in_specs=[spec] * 4, out_specs=[spec] * 4,
        out_shape=[_sds((rows, cols))] * 4, compiler_params=_params())(w, g, m, v)


def _place():
    return lax.axis_index("x"), lax.axis_index("y"), lax.axis_index("c")


def _other_chips(x, y):
    return [(1 - x, y), (x, 1 - y), (1 - x, 1 - y)]


def _remote(src, dst, send_sem, recv_sem, to):
    return pltpu.make_async_remote_copy(src_ref=src, dst_ref=dst, send_sem=send_sem, recv_sem=recv_sem,
                                        device_id=to, device_id_type=MESH)


class _Gather:
    def __init__(self, ins, outs, send_sems, recv_sems, local_sems):
        self.ins, self.outs, self.n = ins, outs, len(ins)
        self.send_sems, self.recv_sems, self.local_sems = send_sems, recv_sems, local_sems
        self.x, self.y, self.c = _place()
        self.me = 2 * self.x + self.y
        self.chips = _other_chips(self.x, self.y)

    def _copy(self, t, k, slot, part, to, src=None):
        dst = self.outs[t].at[slot, part]
        return _remote(dst if src is None else src, dst, self.send_sems.at[6 * t + k], self.recv_sems.at[6 * t + k], to)

    def _mine(self):
        return [pltpu.make_async_copy(self.ins[t].at[p], self.outs[t].at[self.me, p], self.local_sems.at[2 * t + p])
                for t in range(self.n) for p in range(2)]

    def _first(self):
        return [self._copy(t, k, self.me, self.c, (cx, cy, self.c), src=self.ins[t].at[self.c])
                for k, (cx, cy) in enumerate(self.chips) for t in range(self.n)]

    def start(self):
        for cp in self._mine() + self._first():
            cp.start()

    def finish(self):
        x, y, c = self.x, self.y, self.c
        passed = []
        for k, (cx, cy) in enumerate(self.chips):
            for t in range(self.n):
                self._copy(t, k, 2 * cx + cy, c, (x, y, c)).wait_recv()
                passed.append(self._copy(t, 3 + k, 2 * cx + cy, c, (x, y, 1 - c)))
                passed[-1].start()
        for k, (cx, cy) in enumerate(self.chips):
            for t in range(self.n):
                self._copy(t, 3 + k, 2 * cx + cy, 1 - c, (x, y, c)).wait_recv()
        for cp in self._first() + passed:
            cp.wait_send()
        for cp in self._mine():
            cp.wait()


class _Carried(NamedTuple):
    ins: tuple
    out_shapes: tuple
    scratch: tuple
    make: Callable
    aliases: dict


def _host_call(body, carry, steps, *, grid, name, in_specs, out_specs, out_shape, scratch_shapes, operands):
    in_specs, out_specs, out_shape, scratch_shapes = list(in_specs), list(out_specs), list(out_shape), list(scratch_shapes)
    aliases = {}
    if carry is not None:
        n_in, n_out, n_scr = len(in_specs), len(out_specs), len(scratch_shapes)
        n_ci, n_co = len(carry.ins), len(carry.out_shapes)
        plain = body

        def body(*refs):
            ins, cins = refs[:n_in], refs[n_in:n_in + n_ci]
            outs = refs[n_in + n_ci:n_in + n_ci + n_out]
            couts = refs[n_in + n_ci + n_out:n_in + n_ci + n_out + n_co]
            rest = refs[n_in + n_ci + n_out + n_co:]
            exchange = carry.make(cins, couts, *rest[n_scr:])
            pl.when(pl.program_id(0) == 0)(exchange.start)
            plain(*ins, *outs, *rest[:n_scr])
            pl.when(pl.program_id(0) == steps - 1)(exchange.finish)

        aliases = {n_in + i: n_out + j for i, j in carry.aliases.items()}
        in_specs += [ANY] * n_ci
        out_specs += [ANY] * n_co
        out_shape += list(carry.out_shapes)
        scratch_shapes += list(carry.scratch)
        operands = tuple(operands) + tuple(carry.ins)
    return pl.pallas_call(
        body, grid=grid, name=name, in_specs=in_specs, out_specs=out_specs, out_shape=out_shape,
        scratch_shapes=scratch_shapes, input_output_aliases=aliases, compiler_params=_params(len(grid)))(*operands)


def exchange(carry, name):
    n_i, n_o = len(carry.ins), len(carry.out_shapes)

    def body(*refs):
        ex = carry.make(refs[:n_i], refs[n_i:n_i + n_o], *refs[n_i + n_o:])
        ex.start()
        ex.finish()

    return pl.pallas_call(
        body, name=name, in_specs=[ANY] * n_i, out_specs=[ANY] * n_o, out_shape=list(carry.out_shapes),
        scratch_shapes=list(carry.scratch), input_output_aliases=dict(carry.aliases),
        compiler_params=pltpu.CompilerParams(vmem_limit_bytes=VMEM_LIMIT))(*carry.ins)


def _dma_sems(*counts):
    return tuple(pltpu.SemaphoreType.DMA((n,)) for n in counts)


def carried_gather(shards):
    n = len(shards)
    return _Carried(tuple(shards), tuple(_sds((N_CHIPS,) + a.shape, a.dtype) for a in shards),
                    _dma_sems(6 * n, 6 * n, 2 * n), _Gather, {})


class _PairSwap:
    def __init__(self, ins, outs, send_sems, recv_sems):
        x, y, c = _place()
        self.copies = [_remote(ins[t].at[:, 1 - c], outs[t], send_sems.at[t], recv_sems.at[t], (x, y, 1 - c))
                       for t in range(len(ins))]

    def start(self):
        for cp in self.copies:
            cp.start()

    def finish(self):
        for cp in self.copies:
            cp.wait()


def carried_pair_swap(views):
    n = len(views)
    return _Carried(tuple(views), tuple(_sds((v.shape[0],) + v.shape[2:]) for v in views), _dma_sems(n, n), _PairSwap, {})


class _Scatter:
    def __init__(self, srcs, outs, send_sems, recv_sems, local_sems, pieces):
        self.srcs, self.outs, self.pieces, self.n = srcs, outs, pieces, len(pieces)
        self.send_sems, self.recv_sems, self.local_sems = send_sems, recv_sems, local_sems
        self.x, self.y, self.c = _place()
        self.me = 2 * self.x + self.y

    def _piece(self, t, k):
        idx, lead, cols = self.pieces[t][k]
        ref = self.srcs[idx].at[lead]
        return ref if cols is None else ref.at[:, pl.ds(cols[0], cols[1])]

    def _local(self, t, k):
        return pltpu.make_async_copy(self._piece(t, k), self.outs[t].at[k], self.local_sems.at[t])

    def _each_chip(self, mine, others):
        for k in range(N_CHIPS):
            pl.when(self.me == k)(functools.partial(mine, k))
            pl.when(self.me != k)(functools.partial(others, k))

    def start(self):
        def mine(k):
            for t in range(self.n):
                self._local(t, k).start()

        def others(k):
            for t in range(self.n):
                _remote(self._piece(t, k), self.outs[t].at[self.me], self.send_sems.at[N_CHIPS * t + k],
                        self.recv_sems.at[N_CHIPS * t + self.me], (k // 2, k % 2, self.c)).start()

        self._each_chip(mine, others)

    def finish(self):
        def mine(k):
            for t in range(self.n):
                self._local(t, k).wait()

        def others(k):
            for t in range(self.n):
                cp = _remote(self._piece(t, k), self.outs[t].at[k], self.send_sems.at[N_CHIPS * t + k],
                             self.recv_sems.at[N_CHIPS * t + k], (self.x, self.y, self.c))
                cp.wait_recv()
                cp.wait_send()

        self._each_chip(mine, others)


def carried_scatter(srcs, pieces, part_shapes):
    n = len(pieces)
    return _Carried(tuple(srcs), tuple(_sds((N_CHIPS,) + tuple(shp), srcs[0].dtype) for shp in part_shapes),
                    _dma_sems(N_CHIPS * n, N_CHIPS * n, n), functools.partial(_Scatter, pieces=pieces), {})


class _PairJoin:
    def __init__(self, ins, outs, send_sems, recv_sems, layer):
        self.ins, self.outs, self.layer, self.n = ins, outs, layer, len(ins)
        self.send_sems, self.recv_sems = send_sems, recv_sems
        self.x, self.y, self.c = _place()

    def _copy(self, t, half, to):
        return _remote(self.ins[t].at[self.layer, self.c], self.outs[t].at[self.layer, half],
                       self.send_sems.at[t], self.recv_sems.at[t], to)

    def start(self):
        for t in range(self.n):
            self._copy(t, self.c, (self.x, self.y, 1 - self.c)).start()

    def finish(self):
        for t in range(self.n):
            self._copy(t, self.c, (self.x, self.y, 1 - self.c)).wait_send()
            self._copy(t, 1 - self.c, (self.x, self.y, self.c)).wait_recv()


def carried_join(bufs, layer):
    n = len(bufs)
    return _Carried(tuple(bufs), tuple(_sds(b.shape) for b in bufs), _dma_sems(n, n),
                    functools.partial(_PairJoin, layer=layer), {t: t for t in range(n)})


def pair_add_half(mine, theirs, c_vec, name):
    g, _, h, b = mine.shape
    tr = _rows_block(h, b)

    def body(c_ref, a_ref, b_ref, o_ref):
        o_ref[...] = (a_ref[...] + b_ref[...]).astype(BF16)

    part = pl.BlockSpec((None, tr, b), lambda j, i, c: (j, i, 0))
    grid_spec = pltpu.PrefetchScalarGridSpec(
        num_scalar_prefetch=1, grid=(g, h // tr),
        in_specs=[pl.BlockSpec((None, None, tr, b), lambda j, i, c: (j, c[0], i, 0)), part], out_specs=part)
    return pl.pallas_call(body, grid_spec=grid_spec, name=name, out_shape=_sds((g, h, b), BF16),
                          compiler_params=_params(2))(c_vec, mine, theirs)


def chips_add_into(recv, into, layer, c_vec, name):
    n, h, b = recv.shape
    tr = _rows_block(h, b)

    def body(c_ref, r0, r1, r2, r3, *rest):
        rest[-1][...] = ((r0[...].astype(F32) + r1[...].astype(F32)) + r2[...].astype(F32)) + r3[...].astype(F32)

    grid_spec = pltpu.PrefetchScalarGridSpec(
        num_scalar_prefetch=1, grid=(h // tr,),
        in_specs=[pl.BlockSpec((None, tr, b), lambda i, c, k=k: (k, i, 0)) for k in range(n)]
        + ([] if into is None else [ANY]),
        out_specs=pl.BlockSpec((None, None, tr, b), lambda i, c: (layer, c[0], i, 0)))
    return pl.pallas_call(
        body, grid_spec=grid_spec, name=name, out_shape=_sds((2, 2, h, b)),
        input_output_aliases={} if into is None else {1 + n: 0},
        compiler_params=_params())(c_vec, *([recv] * n), *(() if into is None else (into,)))


N_DEV = 8


class _AllSum:
    def __init__(self, ins, outs, buf, total, send_sems, recv_sems, local_sem):
        self.v, self.out, self.buf, self.total = ins[0], outs[0], buf, total
        self.send_sems, self.recv_sems, self.local_sem = send_sems, recv_sems, local_sem
        self.x, self.y, self.c = _place()
        self.me, self.sibling = (self.x, self.y, self.c), (self.x, self.y, 1 - self.c)
        self.chips = _other_chips(self.x, self.y)

    def _slot(self, px, py, pc):
        return self.buf.at[4 * px + 2 * py + pc]

    def _copy(self, k, block, to):
        return _remote(self._slot(*block), self._slot(*block), self.send_sems.at[k], self.recv_sems.at[k], to)

    def _first(self):
        return [self._copy(0, self.me, self.sibling)] + [
            self._copy(1 + k, self.me, (*chip, self.c)) for k, chip in enumerate(self.chips)]

    def start(self):
        load = pltpu.make_async_copy(self.v, self._slot(*self.me), self.local_sem)
        load.start()
        load.wait()
        for cp in self._first():
            cp.start()

    def finish(self):
        c = self.c
        passed = [self._copy(4 + k, (*chip, c), self.sibling) for k, chip in enumerate(self.chips)]
        for k, chip in enumerate(self.chips):
            self._copy(1 + k, (*chip, c), self.me).wait_recv()
            passed[k].start()
        self._copy(0, self.sibling, self.me).wait_recv()
        for k, chip in enumerate(self.chips):
            self._copy(4 + k, (*chip, 1 - c), self.me).wait_recv()
        for cp in self._first() + passed:
            cp.wait_send()
        acc = self.buf[0]
        for d in range(1, N_DEV):
            acc = acc + self.buf[d]
        self.total[...] = acc
        store = pltpu.make_async_copy(self.total, self.out, self.local_sem)
        store.start()
        store.wait()


def carried_allsum(v):
    rows, lanes = v.shape
    scratch = (pltpu.VMEM((N_DEV, rows, lanes), F32), pltpu.VMEM((rows, lanes), F32)) + _dma_sems(7, 7) + (
        pltpu.SemaphoreType.DMA,)
    return _Carried((v,), (_sds((rows, lanes)),), scratch, _AllSum, {})


class _Both:
    def __init__(self, *exchanges):
        self.exchanges = exchanges

    def start(self):
        for ex in self.exchanges:
            ex.start()

    def finish(self):
        for ex in self.exchanges:
            ex.finish()


def carried_both(a, b):
    ai, ao, asc = len(a.ins), len(a.out_shapes), len(a.scratch)

    def make(ins, outs, *scratch):
        return _Both(a.make(ins[:ai], outs[:ao], *scratch[:asc]), b.make(ins[ai:], outs[ao:], *scratch[asc:]))

    aliases = {**a.aliases, **{ai + i: ao + j for i, j in b.aliases.items()}}
    return _Carried(a.ins + b.ins, a.out_shapes + b.out_shapes, a.scratch + b.scratch, make, aliases)


BIG = ("w_in", "w_branch_a", "w_branch_b", "w_out", "w_up", "w_down")
CONV = ("conv_qkv", "conv_ffn")
REPL =("a_log", "dt_bias", "dn_norm_w", "sg_ln_g", "sg_ln_b", "w_spatial", "b_spatial", "ln1_g", "ln1_b", "ln2_g", "ln2_b")


def _pad_rows(flat, mult):
    n = flat.shape[0]
    unit = mult * LANES
    total = -(-n // unit) * unit
    return jnp.pad(flat, (0, total - n)).reshape(total // LANES, LANES)


def _pack(arrs, mult):
    return _pad_rows(jnp.concatenate([a.reshape(-1) for a in arrs]), mult)


def _unpack(flat, shapes):
    out, off = [], 0
    for shp in shapes:
        n = math.prod(shp)
        out.append(flat[off:off + n].reshape(shp))
        off += n
    return out


def kernel(x, w_in, conv_qkv, a_log, dt_bias, dn_norm_w, w_branch_a, sg_ln_g, sg_ln_b, w_spatial, b_spatial, w_branch_b, w_out, ln1_g, ln1_b, w_up, conv_ffn, w_down, ln2_g, ln2_b, loss_target, m_w_in, m_conv_qkv, m_a_log, m_dt_bias, m_dn_norm_w, m_w_branch_a, m_sg_ln_g, m_sg_ln_b, m_w_spatial, m_b_spatial, m_w_branch_b, m_w_out, m_ln1_g, m_ln1_b, m_w_up, m_conv_ffn, m_w_down, m_ln2_g, m_ln2_b, v_w_in, v_conv_qkv, v_a_log, v_dt_bias, v_dn_norm_w, v_w_branch_a, v_sg_ln_g, v_sg_ln_b, v_w_spatial, v_b_spatial, v_w_branch_b, v_w_out, v_ln1_g, v_ln1_b, v_w_up, v_conv_ffn, v_w_down, v_ln2_g, v_ln2_b):
    names = ("w_in", "conv_qkv", "a_log", "dt_bias", "dn_norm_w", "w_branch_a", "sg_ln_g", "sg_ln_b", "w_spatial",
             "b_spatial", "w_branch_b", "w_out", "ln1_g", "ln1_b", "w_up", "conv_ffn", "w_down", "ln2_g", "ln2_b")
    w = dict(zip(names, (w_in, conv_qkv, a_log, dt_bias, dn_norm_w, w_branch_a, sg_ln_g, sg_ln_b, w_spatial,
                         b_spatial, w_branch_b, w_out, ln1_g, ln1_b, w_up, conv_ffn, w_down, ln2_g, ln2_b)))
    m = dict(zip(names, (m_w_in, m_conv_qkv, m_a_log, m_dt_bias, m_dn_norm_w, m_w_branch_a, m_sg_ln_g, m_sg_ln_b,
                         m_w_spatial, m_b_spatial, m_w_branch_b, m_w_out, m_ln1_g, m_ln1_b, m_w_up, m_conv_ffn,
                         m_w_down, m_ln2_g, m_ln2_b)))
    v = dict(zip(names, (v_w_in, v_conv_qkv, v_a_log, v_dt_bias, v_dn_norm_w, v_w_branch_a, v_sg_ln_g, v_sg_ln_b,
                         v_w_spatial, v_b_spatial, v_w_branch_b, v_w_out, v_ln1_g, v_ln1_b, v_w_up, v_conv_ffn,
                         v_w_down, v_ln2_g, v_ln2_b)))
    chip = 2 * lax.axis_index("x") + lax.axis_index("y")
    s = x.shape[1]
    xs = x.reshape(s, D)
    tgt = loss_target.reshape(s, D)

    big_names, conv_names = list(BIG), list(CONV)

    def in_two(name, l):
        rows, cols = w[name].shape[1:]
        return w[name][l].astype(BF16).reshape(2, rows // 2, cols)

    def whole(name, landed):
        rows, cols = w[name].shape[1:]
        return landed.reshape(N_CHIPS, rows, cols)

    first = exchange(carried_gather([in_two("w_in", 0)] + [w[n] for n in conv_names]), "gather_first")
    got = [{"w_in": whole("w_in", first[0])}, {}]
    conv_taps = dict(zip(conv_names, first[1:]))
    narrow, wide = ["w_branch_a", "w_branch_b", "w_out"], ["w_up", "w_down"]
    carried = {"proj_fwd0": (0, narrow), "dn_fwd0": (0, wide), "ffn_fwd0": (1, ["w_in"] + narrow), "dn_fwd1": (1, wide)}

    def carry(call):
        if call not in carried:
            return None
        l, which = carried[call]
        return carried_gather([in_two(n, l) for n in which])

    def land(call, landed):
        l, which = carried.get(call, (0, []))
        got[l].update({n: whole(n, a) for n, a in zip(which, landed)})

    def lane_row(vec, off):
        return jnp.zeros((1, LANES), F32).at[0, off:off + vec.shape[0]].set(vec)

    def side_by_side(blocks):
        return jnp.concatenate([blocks[k] for k in range(N_CHIPS)], axis=1)

    def small_params(l):
        return dict(
            cq=side_by_side(conv_taps["conv_qkv"][:, l]),
            a_row=lane_row(w["a_log"][l], HEADS), dtb_row=lane_row(w["dt_bias"][l], HEADS),
            nw_row=w["dn_norm_w"][l].reshape(1, DK),
            lng=w["sg_ln_g"][l].reshape(1, SG_W), lnb=w["sg_ln_b"][l].reshape(1, SG_W),
            w_s=w["w_spatial"][l], bs_t=jnp.zeros((LANES, LANES), F32).at[:, :4].set(w["b_spatial"][l].T),
            g1=w["ln1_g"][l].reshape(1, D), b1=w["ln1_b"][l].reshape(1, D),
            cf=conv_taps["conv_ffn"][:, l],
            g2=w["ln2_g"][l].reshape(1, D), b2=w["ln2_b"][l].reshape(1, D))

    layers, saved = [], []
    h_in = xs
    for l in range(DEPTH):
        p = small_params(l)
        wi = side_by_side(got[l]["w_in"])
        p["w_in"] = jnp.concatenate([wi[:, :2048], wi[:, 2056:3080], wi[:, 3080:5128], wi[:, 2048:2056],
                                     jnp.zeros((D, IN_COLS_PAD - 5128), BF16)], axis=1)
        proj, *landed = proj_fwd(h_in, p["w_in"], f"proj_fwd{l}", carry=carry(f"proj_fwd{l}"))
        land(f"proj_fwd{l}", landed)
        oa, sst, tinv, *landed = dn_fwd(proj, p["cq"], p["a_row"], p["dtb_row"], p["nw_row"], f"dn_fwd{l}",
                                  carry=carry(f"dn_fwd{l}"))
        land(f"dn_fwd{l}", landed)
        ob = sg_fwd(proj, p["lng"], p["lnb"], p["w_s"], p["bs_t"], f"sg_fwd{l}")
        p.update(wa=side_by_side(got[l]["w_branch_a"]), wb=side_by_side(got[l]["w_branch_b"]),
                 wo=got[l]["w_out"].reshape(D, D))
        x1 = merge_fwd(oa, ob, proj, h_in, p["wa"], p["wb"], p["wo"], p["g1"], p["b1"], f"merge_fwd{l}")
        pre2, x2, *landed = ffn_fwd(x1, got[l]["w_up"], p["cf"], got[l]["w_down"], p["g2"], p["b2"], f"ffn_fwd{l}",
                                    carry=carry(f"ffn_fwd{l}"))
        land(f"ffn_fwd{l}", landed)
        layers.append(p)
        saved.append(dict(x=h_in, proj=proj, oa=oa, ob=ob, sst=sst, tinv=tinv, x1=x1, pre2=pre2))
        h_in = x2


    small_names = conv_names + list(REPL)
    grads = {n: [None] * DEPTH for n in small_names}
    c_vec = jnp.stack([lax.axis_index("c")]).astype(jnp.int32)
    tags = ("w_in", "w_a", "w_b", "w_out", "w_up0", "w_up1", "w_dn0", "w_dn1")
    groups = (1, 1, 1, N_CHIPS, 2, 2, 2, 2)
    ab_cols = w["w_branch_a"].shape[2]
    pieces = [
        [(0, (k,), None) for k in range(N_CHIPS)],
        [(1, (0,), (k * ab_cols, ab_cols)) for k in range(N_CHIPS)],
        [(2, (0,), (k * ab_cols, ab_cols)) for k in range(N_CHIPS)],
        [(3, (k,), None) for k in range(N_CHIPS)],
        [(4 + k % 2, (k // 2,), None) for k in range(N_CHIPS)],
        [(6 + k // 2, (k % 2,), None) for k in range(N_CHIPS)],
    ]
    part_shapes = [(w[n].shape[1] // 2, w[n].shape[2]) for n in big_names]
    arrays_of = ((0,), (1,), (2,), (3,), (4, 5), (6, 7))
    rest, ffn_part = (0, 1, 2, 3), (4, 5)
    bufs = {}

    def arrays(which):
        return [i for t in which for i in arrays_of[t]]

    def views(which, arrs):
        return [a.reshape(groups[i], 2, a.size // a.shape[-1] // (2 * groups[i]), a.shape[-1])
                for i, a in zip(arrays(which), arrs)]

    def pair_sums(l, which, mine, theirs):
        ids = arrays(which)
        sums = [pair_add_half(m_, t_, c_vec, f"reduce_pair_add{l}_{tags[i]}") for i, m_, t_ in zip(ids, mine, theirs)]
        if ids[0] == 0:
            pin = sums[0][0]
            natural = jnp.concatenate([pin[:, :2048], pin[:, C_BA:C_BA + 8], pin[:, 2048:C_BA]], axis=1)
            sums[0] = jnp.stack(jnp.split(natural, N_CHIPS, axis=1))
        return sums

    def scatter_of(which, srcs):
        place = {i: j for j, i in enumerate(arrays(which))}
        return carried_scatter(srcs, [[(place[i], lead, cols) for i, lead, cols in pieces[t]] for t in which],
                               [part_shapes[t] for t in which])

    def chip_sums(l, which, recv):
        for t, r in zip(which, recv):
            n = big_names[t]
            bufs[n] = chips_add_into(r, bufs.get(n), l, c_vec, f"reduce_chips_add{l}_{n}")
        return [bufs[big_names[t]] for t in which]

    def keep(which, joined):
        bufs.update({big_names[t]: b for t, b in zip(which, joined)})

    above = None
    for l in reversed(range(DEPTH)):
        p, a = layers[l], saved[l]
        if l == DEPTH - 1:
            dpre2, dg2, db2, loss_part = loss_ln_bwd(a["pre2"], tgt, p["g2"], p["b2"], "loss_ln2_bwd")
            loss = lax.psum(loss_part[0, 0], ("x", "y", "c"))
        else:
            dpre2, dg2, db2 = ln_bwd(a["pre2"], dy, p["g2"], p["b2"], f"ln2_bwd{l}")
        dx1, dwup0, dcf0, dwdn0, *theirs = ffn_bwd(
            a["x1"], dpre2, dpre2, ALPHA, got[l]["w_up"], p["cf"], got[l]["w_down"], 0, f"ffn_bwd{l}a",
            carry=carried_pair_swap(above) if above else None)
        srcs = pair_sums(l + 1, rest, above, theirs) if above else None
        dx1, dwup1, dcf1, dwdn1, *recv = ffn_bwd(
            a["x1"], dpre2, dx1, 1.0, got[l]["w_up"], p["cf"], got[l]["w_down"], 1, f"ffn_bwd{l}b",
            carry=scatter_of(rest, srcs) if above else None)
        summed = chip_sums(l + 1, rest, recv) if above else None
        doa, dob, dga, dgb, dxd, dwa, dwb, dwo, dg1, db1, *joined = merge_bwd(
            a["oa"], a["ob"], a["proj"], a["x"], dx1, p["wa"], p["wb"], p["wo"], p["g1"], p["b1"], f"merge_bwd{l}",
            carry=carried_join(summed, l + 1) if above else None)
        keep(rest, joined)
        mine = views(ffn_part, [dwup0, dwup1, dwdn0, dwdn1])
        duv, dlng, dlnb, dws, dbs, *theirs = sg_bwd(a["proj"], dob, p["lng"], p["lnb"], p["w_s"], p["bs_t"], f"sg_bwd{l}",
                                                    carry=carried_pair_swap(mine))
        srcs = pair_sums(l, ffn_part, mine, theirs)
        dqkv, dz, dba, dcq, da, ddtb, dnw, *recv = dn_bwd(
            a["proj"], a["sst"], a["tinv"], doa, p["cq"], p["a_row"], p["dtb_row"], p["nw_row"], f"dn_bwd{l}",
            carry=scatter_of(ffn_part, srcs))
        summed = chip_sums(l, ffn_part, recv)
        dy, *joined = proj_bwd([dqkv, dz, duv, dga, dgb, dba], dxd, p["w_in"], f"proj_bwd{l}",
                               carry=carried_join(summed, l))
        keep(ffn_part, joined)
        dwi = None
        for tag, dp, col in (("qkv", dqkv, 0), ("z", dz, C_Z), ("uv", duv, C_UV), ("ga", dga, C_GA), ("gb", dgb, C_GB),
                             ("ba", dba, C_BA)):
            dwi = wgrad(a["x"], dp, col, dwi, f"wgrad_in{l}_{tag}")

        above = views(rest, [dwi, dwa, dwb, dwo])
        grads["conv_qkv"][l] = dcq
        grads["conv_ffn"][l] = jnp.concatenate([dcf0[0], dcf1[0], dcf0[1], dcf1[1]], axis=1)
        grads["a_log"][l] = da[0, HEADS:2 * HEADS]
        grads["dt_bias"][l] = ddtb[0, HEADS:2 * HEADS]
        grads["dn_norm_w"][l] = dnw[0]
        grads["sg_ln_g"][l] = dlng[0]
        grads["sg_ln_b"][l] = dlnb[0]
        grads["w_spatial"][l] = dws
        grads["b_spatial"][l] = dbs[:, :4].T
        grads["ln1_g"][l] = dg1[0]
        grads["ln1_b"][l] = db1[0]
        grads["ln2_g"][l] = dg2[0]
        grads["ln2_b"][l] = db2[0]
    grad_x = dy.reshape(x.shape)
    g_full = {n: jnp.stack(grads[n]) for n in small_names}

    theirs = exchange(carried_pair_swap(above), "reduce_pair")
    *recv, small = exchange(carried_both(scatter_of(rest, pair_sums(0, rest, above, theirs)),
                                         carried_allsum(_pack([g_full[n] for n in small_names], 8))), "reduce_chips")
    keep(rest, exchange(carried_join(chip_sums(0, rest, recv), 0), "reduce_join"))
    g_shard = {n: bufs[n].reshape(w[n].shape) for n in big_names}

    small_full = dict(zip(small_names, _unpack(small.reshape(-1), [g_full[n].shape for n in small_names])))
    for n in conv_names:
        width = w[n].shape[2]
        g_shard[n] = lax.dynamic_slice_in_dim(small_full[n], chip * width, width, axis=2)
    for n in REPL:
        g_shard[n] = small_full[n]

    delta, new_m, new_v = {}, {}, {}
    for n in big_names:
        shp = w[n].shape
        two_d = (shp[0] * shp[1], shp[2])
        g_, d_, m_, v_ = adam_call(w[n].reshape(two_d), g_shard[n].reshape(two_d), m[n].reshape(two_d), v[n].reshape(two_d), f"adam_{n}")
        g_shard[n], delta[n], new_m[n], new_v[n] = g_.reshape(shp), d_.reshape(shp), m_.reshape(shp), v_.reshape(shp)
    shapes = [w[n].shape for n in small_names]
    packs = [_pack([src[n] for n in small_names], 8) for src in (w, g_shard, m, v)]
    outs = adam_call(*packs, "adam_small")
    for dst, o in zip((delta, new_m, new_v), outs[1:]):
        dst.update(zip(small_names, _unpack(o.reshape(-1), shapes)))

    return (loss, grad_x, *[g_shard[n] for n in names], *[delta[n] for n in names],
            *[new_m[n] for n in names], *[new_v[n] for n in names])
```

```python
import functools
import math
from typing import Callable, NamedTuple

import jax
import jax.numpy as jnp
from jax import lax
from jax.experimental import pallas as pl
from jax.experimental.pallas import tpu as pltpu

F32 = jnp.float32
BF16 = jnp.bfloat16
HI = lax.Precision.HIGHEST
MID = lax.Precision.HIGH
MESH = pl.DeviceIdType.MESH

D = 1024
DEPTH = 2
HEADS = 4
DK = 128
CHUNK = 64
QKV_W = 1536
Z_W = 512
SG_W = 512
FFN = 2816
FFN_HALF = FFN // 2
N_CHIPS = 4
DN_SHARD = FFN // N_CHIPS
LN_EPS = 1e-5
RMS_EPS = 1e-6
L2_EPS = 1e-6
ALPHA = (2 * DEPTH) ** 0.25
ADAM_LR, ADAM_B1, ADAM_B2, ADAM_EPS, ADAM_WD, ADAM_STEP = 0.001, 0.9, 0.999, 1e-08, 0.01, 10

HALO = 16
LANES = 128
IN_COLS_PAD = 5248
C_Z, C_UV, C_GA, C_GB, C_BA = 1536, 2048, 3072, 4096, 5120
VMEM_LIMIT = 56 * 1024 * 1024
DN_TILE = 256


def _params(n_grid=1):
    return pltpu.CompilerParams(dimension_semantics=("arbitrary",) * n_grid, vmem_limit_bytes=VMEM_LIMIT)


def _mm(a, b):
    return jnp.dot(a.astype(BF16), b.astype(BF16), preferred_element_type=F32)


def _mm_nt(a, b):
    return lax.dot_general(a.astype(BF16), b.astype(BF16), (((1,), (1,)), ((), ())), preferred_element_type=F32)


def _mm_tn(a, b):
    return lax.dot_general(a.astype(BF16), b.astype(BF16), (((0,), (0,)), ((), ())), preferred_element_type=F32)


def _bdot(a, b, prec=MID):
    return lax.dot_general(a, b, (((2,), (1,)), ((0,), (0,))), precision=prec, preferred_element_type=F32)


def _bdot_nt(a, b, prec=MID):
    return lax.dot_general(a, b, (((2,), (2,)), ((0,), (0,))), precision=prec, preferred_element_type=F32)


def _bf16_dot(a, b, contract):
    return lax.dot_general(a.astype(BF16), b.astype(BF16), (contract, ((0,), (0,))), preferred_element_type=F32)


@jax.custom_vjp
def _fdot(a, b):
    return _bf16_dot(a, b, ((2,), (1,)))


def _fdot_fwd(a, b):
    return _fdot(a, b), (a, b)


def _fdot_bwd(res, ct):
    a, b = res
    return _bf16_dot(ct, b, ((2,), (2,))), _bf16_dot(a, ct, ((1,), (1,)))


_fdot.defvjp(_fdot_fwd, _fdot_bwd)


@jax.custom_vjp
def _fdot_nt(a, b):
    return _bf16_dot(a, b, ((2,), (2,)))


def _fdot_nt_fwd(a, b):
    return _fdot_nt(a, b), (a, b)


def _fdot_nt_bwd(res, ct):
    a, b = res
    return _bf16_dot(ct, b, ((2,), (1,))), _bf16_dot(ct, a, ((1,), (1,)))


_fdot_nt.defvjp(_fdot_nt_fwd, _fdot_nt_bwd)


@jax.custom_vjp
def _fdot_tn(a, b):
    return _bf16_dot(a, b, ((1,), (1,)))


def _fdot_tn_fwd(a, b):
    return _fdot_tn(a, b), (a, b)


def _fdot_tn_bwd(res, ct):
    a, b = res
    return _bf16_dot(b, ct, ((2,), (2,))), _bf16_dot(a, ct, ((2,), (1,)))


_fdot_tn.defvjp(_fdot_tn_fwd, _fdot_tn_bwd)


def _stack(parts):
    return jnp.concatenate([p[None] for p in parts], axis=0)


def _ln(x, g, b):
    mu = jnp.mean(x, axis=-1, keepdims=True)
    xc = x - mu
    var = jnp.mean(xc * xc, axis=-1, keepdims=True)
    return xc * lax.rsqrt(var + LN_EPS) * g + b


def _shift_rows(x, s):
    s = s % x.shape[0]
    return x if s == 0 else pltpu.roll(x, s, 0)


@jax.custom_vjp
def _conv(xcat, w):
    k_taps = len(w)
    y = None
    for k in range(k_taps):
        t = _shift_rows(xcat, k_taps - 1 - k)[HALO:] * w[k]
        y = t if y is None else y + t
    return y


def _conv_fwd(xcat, w):
    return _conv(xcat, w), (xcat, w)


def _conv_bwd(res, dy):
    xcat, w = res
    k_taps = len(w)
    dyp = jnp.concatenate([jnp.zeros((HALO, dy.shape[1]), dy.dtype), dy], axis=0)
    dx = None
    dws = []
    for k in range(k_taps):
        shifted = _shift_rows(dyp, -(k_taps - 1 - k))
        t = shifted * w[k]
        dx = t if dx is None else dx + t
        dws.append(jnp.sum(shifted * xcat, axis=0, keepdims=True))
    return dx, tuple(dws)


_conv.defvjp(_conv_fwd, _conv_bwd)


@jax.custom_vjp
def _tri_inv(l):
    n = l.shape[-1]
    r = lax.broadcasted_iota(jnp.int32, (n, n), 0)
    c = lax.broadcasted_iota(jnp.int32, (n, n), 1)
    eye = (r == c).astype(F32)
    p = eye - l
    lp = l
    steps = int(math.log2(n)) - 1
    for i in range(steps):
        dot = _bdot if i < 2 else functools.partial(_bf16_dot, contract=((2,), (1,)))
        lp = dot(lp, lp)
        p = p + dot(p, lp)
    return p


def _tri_inv_fwd(l):
    t = _tri_inv(l)
    return t, t


def _tri_inv_bwd(t, dt):
    tt = jnp.swapaxes(t, 1, 2)
    return (-_bdot(tt, _bdot(dt, tt)),)


_tri_inv.defvjp(_tri_inv_fwd, _tri_inv_bwd)


@jax.custom_vjp
def _tri_inv_saved(l, t):
    return t


def _tri_inv_saved_fwd(l, t):
    return t, t


def _tri_inv_saved_bwd(t, dt):
    return _tri_inv_bwd(t, dt) + (jnp.zeros_like(t),)


_tri_inv_saved.defvjp(_tri_inv_saved_fwd, _tri_inv_saved_bwd)


def _dn_glue(qkvcat, z, ba, s_in, cw, a_row, dtb_row, nw_row, t_saved=None):
    t_rows = z.shape[0]
    nc = t_rows // CHUNK
    nb = nc * HEADS

    qkv = jax.nn.silu(_conv(qkvcat, cw))

    def chunks(t, off):
        return _stack([t[n * CHUNK:(n + 1) * CHUNK, off + h * DK: off + (h + 1) * DK]
                       for n in range(nc) for h in range(HEADS)])

    q = chunks(qkv, 0)
    k = chunks(qkv, 512)
    v = chunks(qkv, 1024)
    q = q * lax.rsqrt(jnp.sum(q * q, axis=-1, keepdims=True) + L2_EPS) * (DK ** -0.5)
    k = k * lax.rsqrt(jnp.sum(k * k, axis=-1, keepdims=True) + L2_EPS)

    lane = lax.broadcasted_iota(jnp.int32, (LANES, HEADS * DK), 0)
    head_of_col = lax.broadcasted_iota(jnp.int32, (LANES, HEADS * DK), 1) // DK
    e_beta = (head_of_col == lane).astype(F32)
    e_g = (head_of_col + HEADS == lane).astype(F32)
    beta_l = jax.nn.sigmoid(ba)
    g_l = -jnp.exp(a_row) * jax.nn.softplus(ba + dtb_row)
    beta = chunks(jnp.dot(beta_l, e_beta, precision=MID, preferred_element_type=F32), 0)
    g = chunks(jnp.dot(g_l, e_g, precision=MID, preferred_element_type=F32), 0)

    r = lax.broadcasted_iota(jnp.int32, (CHUNK, CHUNK), 0)
    c = lax.broadcasted_iota(jnp.int32, (CHUNK, CHUNK), 1)
    causal = r >= c
    strict = r > c
    tril_b = jnp.broadcast_to(causal.astype(F32), (nb, CHUNK, CHUNK))
    gi_b = _bdot(tril_b, g, HI)
    gi = gi_b[:, :, :CHUNK]
    gj = jnp.swapaxes(gi, 1, 2)
    decay = jnp.where(causal, jnp.exp(jnp.where(causal, gi - gj, 0.0)), 0.0)
    kb = k * beta
    l_mat = jnp.where(strict, _fdot_nt(kb, k) * decay, 0.0)
    t_mat = _tri_inv(l_mat) if t_saved is None else _tri_inv_saved(l_mat, t_saved)
    e_gi = jnp.exp(gi_b)
    w_mat = _fdot(t_mat, kb * e_gi)
    u_mat = _fdot(t_mat, v * beta)
    a_qk = _fdot_nt(q, k) * decay
    q_g = q * e_gi
    gl_b = jnp.broadcast_to(jnp.sum(g, axis=1, keepdims=True), g.shape)
    k_d = k * jnp.exp(gl_b - gi_b)
    e_gl = jnp.exp(gl_b)
    g_last = jnp.concatenate([e_gl, e_gl], axis=1)

    state = s_in
    rows = []
    for n in range(nc):
        sl = slice(n * HEADS, (n + 1) * HEADS)
        u_new = u_mat[sl] - _fdot(w_mat[sl], state)
        o_n = _fdot(q_g[sl], state) + _fdot(a_qk[sl], u_new)
        state = state * g_last[sl] + _fdot_tn(k_d[sl], u_new)
        o_n = o_n * lax.rsqrt(jnp.mean(o_n * o_n, axis=-1, keepdims=True) + RMS_EPS) * nw_row
        z_n = _stack([z[n * CHUNK:(n + 1) * CHUNK, h * DK:(h + 1) * DK] for h in range(HEADS)])
        o_n = o_n * jax.nn.silu(z_n)
        rows.append(jnp.concatenate([o_n[h] for h in range(HEADS)], axis=-1))
    return jnp.concatenate(rows, axis=0), state, t_mat


def _sg_glue(uv, lng, lnb, w_s, bs_t):
    t_rows = uv.shape[0]
    y = jax.nn.gelu(uv)
    u = y[:, :SG_W]
    v = _ln(y[:, SG_W:], lng, lnb)
    r = lax.broadcasted_iota(jnp.int32, (LANES, LANES), 0)
    c = lax.broadcasted_iota(jnp.int32, (LANES, LANES), 1)
    wm = jnp.where(r >= c, w_s, 0.0)
    lane = lax.broadcasted_iota(jnp.int32, (LANES, SG_W), 0)
    group_of_col = lax.broadcasted_iota(jnp.int32, (LANES, SG_W), 1) // LANES
    e_grp = (group_of_col == lane).astype(F32)
    bias = jnp.dot(bs_t, e_grp, precision=HI, preferred_element_type=F32)
    outs = []
    for n in range(t_rows // LANES):
        vb = v[n * LANES:(n + 1) * LANES]
        vg = _stack([vb[:, g * LANES:(g + 1) * LANES] for g in range(4)])
        mg = _fdot(wm, vg)
        mixed = jnp.concatenate([mg[g] for g in range(4)], axis=-1) + bias
        outs.append(u[n * LANES:(n + 1) * LANES] * mixed)
    return jnp.concatenate(outs, axis=0)


def _merge_glue(ga, gb, ya, yb):
    return jax.nn.sigmoid(ga) * ya + jax.nn.sigmoid(gb) * yb


def _res_ln_glue(x, r, g, b):
    return _ln(ALPHA * x + r, g, b)


def _ffn_glue(ua, ub, cwa, cwb):
    return jax.nn.silu(_conv(ua, cwa)) * _conv(ub, cwb)


def _row(t, c, col=0):
    return pl.BlockSpec((t, c), lambda i: (i, col))


def _row_rev(t, c, nt, col=0):
    return pl.BlockSpec((t, c), lambda i: (nt - 1 - i, col))


def _halo(t, c, nt=None):
    per = t // HALO
    if nt is None:
        return pl.BlockSpec((HALO, c), lambda i: (jnp.maximum(i * per - 1, 0), 0))
    return pl.BlockSpec((HALO, c), lambda i: (jnp.maximum((nt - 1 - i) * per - 1, 0), 0))


def _full(shape):
    nd = len(shape)
    return pl.BlockSpec(shape, lambda i: (0,) * nd)


ANY = pl.BlockSpec(memory_space=pl.ANY)


def _sds(shape, dtype=F32):
    return jax.ShapeDtypeStruct(shape, dtype)


def _tile(s, want=256):
    for t in (want, 256, 128):
        if s % t == 0:
            return t
    raise ValueError(f"sequence length {s} is not a multiple of 128")


def proj_fwd(x, w, name, carry=None):
    s = x.shape[0]
    t = _tile(s, 512)
    nt = s // t
    segs = [(0, 2048), (2048, 3072), (3072, 4096), (4096, 5120), (5120, IN_COLS_PAD)]

    def body(x_ref, w_ref, p_ref):
        xb = x_ref[...].astype(BF16)
        for lo, hi in segs:
            p_ref[:, lo:hi] = jnp.dot(xb, w_ref[:, lo:hi], preferred_element_type=F32)

    return _host_call(
        body, carry, nt, grid=(nt,), name=name, in_specs=[_row(t, D), _full((D, IN_COLS_PAD))],
        out_specs=[_row(t, IN_COLS_PAD)], out_shape=[_sds((s, IN_COLS_PAD))], scratch_shapes=[], operands=(x, w))


def dn_fwd(p, cq, a_row, dtb_row, nw_row, name, carry=None):
    s = p.shape[0]
    t = _tile(s, DN_TILE)
    nt = s // t

    nb = HEADS * t // CHUNK

    def body(qkv_ref, halo_ref, z_ref, ba_ref, cq_ref, a_ref, dtb_ref, nw_ref, o_ref, sst_ref, tinv_ref, s_scr):
        i = pl.program_id(0)

        @pl.when(i == 0)
        def _():
            s_scr[...] = jnp.zeros_like(s_scr)

        halo = jnp.where(i == 0, 0.0, halo_ref[...])
        qkvcat = jnp.concatenate([halo, qkv_ref[...]], axis=0)
        cw = tuple(cq_ref[k:k + 1, :] for k in range(4))
        s_in = s_scr[...]
        sst_ref[0] = s_in
        o, s_out, t_mat = _dn_glue(qkvcat, z_ref[...], ba_ref[...], s_in, cw, a_ref[...], dtb_ref[...], nw_ref[...])
        o_ref[...] = o.astype(BF16)
        tinv_ref[0] = t_mat
        s_scr[...] = s_out

    return _host_call(
        body, carry, nt, grid=(nt,), name=name,
        in_specs=[_row(t, QKV_W), _halo(t, QKV_W), _row(t, Z_W, C_Z // Z_W), _row(t, LANES, C_BA // LANES),
                  _full((4, QKV_W)), _full((1, LANES)), _full((1, LANES)), _full((1, LANES))],
        out_specs=[_row(t, Z_W), pl.BlockSpec((1, HEADS, DK, DK), lambda i: (i, 0, 0, 0)),
                   pl.BlockSpec((1, nb, CHUNK, CHUNK), lambda i: (i, 0, 0, 0))],
        out_shape=[_sds((s, Z_W), BF16), _sds((nt, HEADS, DK, DK)), _sds((nt, nb, CHUNK, CHUNK))],
        scratch_shapes=[pltpu.VMEM((HEADS, DK, DK), F32)], operands=(p, p, p, p, cq, a_row, dtb_row, nw_row))


def sg_fwd(p, lng, lnb, w_s, bs_t, name):
    s = p.shape[0]
    t = _tile(s, 512)

    def body(uv_ref, lng_ref, lnb_ref, ws_ref, bs_ref, o_ref):
        o_ref[...] = _sg_glue(uv_ref[...], lng_ref[...], lnb_ref[...], ws_ref[...], bs_ref[...]).astype(BF16)

    return pl.pallas_call(
        body, grid=(s // t,), name=name,
        in_specs=[_row(t, 2 * SG_W, C_UV // (2 * SG_W)), _full((1, SG_W)), _full((1, SG_W)),
                  _full((4, LANES, LANES)), _full((LANES, LANES))],
        out_specs=_row(t, SG_W), out_shape=_sds((s, SG_W), BF16), compiler_params=_params())(p, lng, lnb, w_s, bs_t)


def merge_fwd(oa, ob, p, x, wa, wb, wo, g1, b1, name):
    s = x.shape[0]
    t = _tile(s, 512)

    def body(oa_ref, ob_ref, ga_ref, gb_ref, x_ref, wa_ref, wb_ref, wo_ref, g_ref, b_ref, x1_ref):
        ya = _mm(oa_ref[...], wa_ref[...])
        yb = _mm(ob_ref[...], wb_ref[...])
        h = _merge_glue(ga_ref[...], gb_ref[...], ya, yb)
        x1_ref[...] = _res_ln_glue(x_ref[...], _mm(h, wo_ref[...]), g_ref[...], b_ref[...])

    return pl.pallas_call(
        body, grid=(s // t,), name=name,
        in_specs=[_row(t, Z_W), _row(t, SG_W), _row(t, D, C_GA // D), _row(t, D, C_GB // D), _row(t, D),
                  _full((Z_W, D)), _full((SG_W, D)), _full((D, D)), _full((1, D)), _full((1, D))],
        out_specs=_row(t, D), out_shape=_sds((s, D)), compiler_params=_params())(oa, ob, p, p, x, wa, wb, wo, g1, b1)


def _load_ffn_weights(wup_hbm, wdn_hbm, wup_v, wdn_v, up_slots, dn_slots):
    for n, k in enumerate(up_slots):
        pltpu.sync_copy(wup_hbm.at[k], wup_v.at[n])
    for n, k in enumerate(dn_slots):
        pltpu.sync_copy(wdn_hbm.at[k], wdn_v.at[pl.ds(n * DN_SHARD, DN_SHARD)])


def ffn_fwd(x1, wup4, cf4, wdn4, g2, b2, name, carry=None):
    s = x1.shape[0]
    t = _tile(s, 512)
    nt = s // t

    def body(x1_ref, halo_ref, wup_hbm, cf_ref, wdn_hbm, g_ref, b_ref, pre_ref, x2_ref, wup_v, wdn_v):
        i = pl.program_id(0)

        @pl.when(i == 0)
        def _():
            _load_ffn_weights(wup_hbm, wdn_hbm, wup_v, wdn_v, range(4), range(4))

        x1v = x1_ref[...]
        halo = jnp.where(i == 0, 0.0, halo_ref[...])
        x1cat = jnp.concatenate([halo, x1v], axis=0).astype(BF16)
        f = None
        for h in range(2):
            ua = jnp.dot(x1cat, wup_v[h], preferred_element_type=F32)
            ub = jnp.dot(x1cat, wup_v[2 + h], preferred_element_type=F32)
            cwa = tuple(cf_ref[h, k:k + 1, :] for k in range(3))
            cwb = tuple(cf_ref[2 + h, k:k + 1, :] for k in range(3))
            act = _ffn_glue(ua, ub, cwa, cwb)
            fh = _mm(act, wdn_v[h * FFN_HALF:(h + 1) * FFN_HALF, :])
            f = fh if f is None else f + fh
        pre = ALPHA * x1v + f
        pre_ref[...] = pre
        x2_ref[...] = _ln(pre, g_ref[...], b_ref[...])

    return _host_call(
        body, carry, nt, grid=(nt,), name=name,
        in_specs=[_row(t, D), _halo(t, D), ANY, _full((4, 3, FFN_HALF)), ANY, _full((1, D)), _full((1, D))],
        out_specs=[_row(t, D), _row(t, D)], out_shape=[_sds((s, D)), _sds((s, D))],
        scratch_shapes=[pltpu.VMEM((4, D, FFN_HALF), BF16), pltpu.VMEM((FFN, D), BF16)],
        operands=(x1, x1, wup4, cf4, wdn4, g2, b2))


def loss_ln_bwd(pre, tgt, g, b, name):
    s = pre.shape[0]
    t = _tile(s, 512)

    def body(pre_ref, t_ref, g_ref, b_ref, dpre_ref, dg_ref, db_ref, loss_ref):
        first = pl.program_id(0) == 0
        y, vjp = jax.vjp(_ln, pre_ref[...], g_ref[...], b_ref[...])
        e = y - t_ref[...]
        dpre, dg, db = vjp(e * (1.0 / D))
        dpre_ref[...] = dpre
        _acc(dg_ref, dg, first)
        _acc(db_ref, db, first)
        part = jnp.sum(jnp.sum(e * e, axis=1, keepdims=True), axis=0, keepdims=True) * (0.5 / D)
        _acc(loss_ref, jnp.broadcast_to(part, loss_ref.shape), first)

    return pl.pallas_call(
        body, grid=(s // t,), name=name, in_specs=[_row(t, D), _row(t, D), _full((1, D)), _full((1, D))],
        out_specs=[_row(t, D), _full((1, D)), _full((1, D)), _full((8, LANES))],
        out_shape=[_sds((s, D)), _sds((1, D)), _sds((1, D)), _sds((8, LANES))], compiler_params=_params())(pre, tgt, g, b)


def _acc(ref, val, first):
    @pl.when(first)
    def _():
        ref[...] = val

    @pl.when(jnp.logical_not(first))
    def _():
        ref[...] += val


def _acc_tn(acc_ref, a, b, first, seg):
    n = b.shape[1]
    for lo in range(0, n, seg):
        hi = min(lo + seg, n)
        _acc(acc_ref.at[:, lo:hi], _mm_tn(a, b[:, lo:hi]), first)


def ln_bwd(pre, dy, g, b, name):
    s = pre.shape[0]
    t = _tile(s, 512)

    def body(pre_ref, dy_ref, g_ref, b_ref, dpre_ref, dg_ref, db_ref):
        _, vjp = jax.vjp(_ln, pre_ref[...], g_ref[...], b_ref[...])
        dpre, dg, db = vjp(dy_ref[...])
        dpre_ref[...] = dpre
        first = pl.program_id(0) == 0
        _acc(dg_ref, dg, first)
        _acc(db_ref, db, first)

    return pl.pallas_call(
        body, grid=(s // t,), name=name, in_specs=[_row(t, D), _row(t, D), _full((1, D)), _full((1, D))],
        out_specs=[_row(t, D), _full((1, D)), _full((1, D))],
        out_shape=[_sds((s, D)), _sds((1, D)), _sds((1, D))], compiler_params=_params())(pre, dy, g, b)


def ffn_bwd(x1, df, acc_in, acc_scale, wup4, cf4, wdn4, h, name, carry=None):
    s = x1.shape[0]
    t = _tile(s, 512)
    nt = s // t

    def body(x1_ref, halo_ref, df_ref, acc_ref, wup_hbm, cf_ref, wdn_hbm,
             dx1_ref, dcf_ref, dua_ref, dub_ref, act_ref, wup_v, wdn_v, carry):
        i = pl.program_id(0)
        j = nt - 1 - i
        first = i == 0

        @pl.when(first)
        def _():
            _load_ffn_weights(wup_hbm, wdn_hbm, wup_v, wdn_v, (h, 2 + h), (2 * h, 2 * h + 1))
            carry[...] = jnp.zeros_like(carry)

        halo = jnp.where(j == 0, 0.0, halo_ref[...])
        x1cat = jnp.concatenate([halo, x1_ref[...]], axis=0).astype(BF16)
        ua = jnp.dot(x1cat, wup_v[0], preferred_element_type=F32)
        ub = jnp.dot(x1cat, wup_v[1], preferred_element_type=F32)
        cwa = tuple(cf_ref[h, k:k + 1, :] for k in range(3))
        cwb = tuple(cf_ref[2 + h, k:k + 1, :] for k in range(3))
        act, vjp = jax.vjp(_ffn_glue, ua, ub, cwa, cwb)
        act_ref[...] = act.astype(BF16)
        dact = _mm_nt(df_ref[...], wdn_v[...])
        dua, dub, dcwa, dcwb = vjp(dact)
        dups = []
        for n, (du, out_ref) in enumerate(((dua, dua_ref), (dub, dub_ref))):
            dups.append(jnp.concatenate([du[HALO:t], du[t:] + carry[n]], axis=0).astype(BF16))
            carry[n] = du[:HALO]
            out_ref[...] = dups[n]
        for k in range(3):
            _acc(dcf_ref.at[0, k:k + 1, :], dcwa[k], first)
            _acc(dcf_ref.at[1, k:k + 1, :], dcwb[k], first)
        dx1_ref[...] = acc_scale * acc_ref[...] + _mm_nt(dups[0], wup_v[0]) + _mm_nt(dups[1], wup_v[1])

    hidden = _row_rev(t, FFN_HALF, nt)
    return _host_call(
        body, carry, nt, grid=(nt,), name=name,
        in_specs=[_row_rev(t, D, nt), _halo(t, D, nt), _row_rev(t, D, nt), _row_rev(t, D, nt),
                  ANY, _full((4, 3, FFN_HALF)), ANY],
        out_specs=[_row_rev(t, D, nt), _full((2, 3, FFN_HALF)), hidden, hidden, hidden],
        out_shape=[_sds((s, D)), _sds((2, 3, FFN_HALF))] + [_sds((s, FFN_HALF), BF16)] * 3,
        scratch_shapes=[pltpu.VMEM((2, D, FFN_HALF), BF16), pltpu.VMEM((FFN_HALF, D), BF16),
                        pltpu.VMEM((2, HALO, FFN_HALF), F32)],
        operands=(x1, x1, df, acc_in, wup4, cf4, wdn4))


def wgrad_mm(a, b, slots, slot, into, name):
    s, m = a.shape
    n = b.shape[1]
    tk = _tile(s, 1024)

    def body(a_ref, b_ref, *rest):
        _acc(rest[-1], _mm_tn(a_ref[...], b_ref[...]), pl.program_id(0) == 0)

    return pl.pallas_call(
        body, grid=(s // tk,), name=name,
        in_specs=[pl.BlockSpec((tk, m), lambda k: (k, 0)), pl.BlockSpec((tk, n), lambda k: (k, 0))]
        + ([] if into is None else [ANY]),
        out_specs=pl.BlockSpec((None, m, n), lambda k: (slot, 0, 0)), out_shape=_sds((slots, m, n)),
        input_output_aliases={} if into is None else {2: 0},
        compiler_params=_params())(*((a, b) if into is None else (a, b, into)))


def merge_bwd(oa, ob, p, x, dx1, wa, wb, wo, g1, b1, name, carry=None):
    s = x.shape[0]
    t = _tile(s)

    def body(oa_ref, ob_ref, ga_ref, gb_ref, x_ref, dx1_ref, wa_ref, wb_ref, wo_ref, g_ref, b_ref,
             doa_ref, dob_ref, dga_ref, dgb_ref, dx_ref, dwa_ref, dwb_ref, dwo_ref, dg_ref, db_ref):
        first = pl.program_id(0) == 0
        oa = oa_ref[...]
        ob = ob_ref[...]
        ya = _mm(oa, wa_ref[...])
        yb = _mm(ob, wb_ref[...])
        h, vjp1 = jax.vjp(_merge_glue, ga_ref[...], gb_ref[...], ya, yb)
        hb = h.astype(BF16)
        r = _mm(hb, wo_ref[...])
        _, vjp2 = jax.vjp(_res_ln_glue, x_ref[...], r, g_ref[...], b_ref[...])
        dx, dr, dg, db = vjp2(dx1_ref[...])
        dx_ref[...] = dx
        _acc(dg_ref, dg, first)
        _acc(db_ref, db, first)
        drb = dr.astype(BF16)
        dh = _mm_nt(drb, wo_ref[...])
        _acc(dwo_ref, _mm_tn(hb, drb), first)
        dga, dgb, dya, dyb = vjp1(dh)
        dga_ref[...] = dga.astype(BF16)
        dgb_ref[...] = dgb.astype(BF16)
        dyab = dya.astype(BF16)
        dybb = dyb.astype(BF16)
        doa_ref[...] = _mm_nt(dyab, wa_ref[...]).astype(BF16)
        dob_ref[...] = _mm_nt(dybb, wb_ref[...]).astype(BF16)
        _acc(dwa_ref, _mm_tn(oa, dyab), first)
        _acc(dwb_ref, _mm_tn(ob, dybb), first)

    return _host_call(
        body, carry, s // t, grid=(s // t,), name=name,
        in_specs=[_row(t, Z_W), _row(t, SG_W), _row(t, D, C_GA // D), _row(t, D, C_GB // D), _row(t, D), _row(t, D),
                  _full((Z_W, D)), _full((SG_W, D)), _full((D, D)), _full((1, D)), _full((1, D))],
        out_specs=[_row(t, Z_W), _row(t, SG_W), _row(t, D), _row(t, D), _row(t, D),
                   _full((Z_W, D)), _full((SG_W, D)), _full((D, D)), _full((1, D)), _full((1, D))],
        out_shape=[_sds((s, Z_W), BF16), _sds((s, SG_W), BF16), _sds((s, D), BF16), _sds((s, D), BF16), _sds((s, D)),
                   _sds((Z_W, D)), _sds((SG_W, D)), _sds((D, D)), _sds((1, D)), _sds((1, D))],
        scratch_shapes=[], operands=(oa, ob, p, p, x, dx1, wa, wb, wo, g1, b1))


def sg_bwd(p, dob, lng, lnb, w_s, bs_t, name, carry=None):
    s = p.shape[0]
    t = _tile(s, 512)

    def body(uv_ref, dob_ref, lng_ref, lnb_ref, ws_ref, bs_ref, duv_ref, dlng_ref, dlnb_ref, dws_ref, dbs_ref):
        first = pl.program_id(0) == 0
        _, vjp = jax.vjp(_sg_glue, uv_ref[...], lng_ref[...], lnb_ref[...], ws_ref[...], bs_ref[...])
        duv, dlng, dlnb, dws, dbs = vjp(dob_ref[...].astype(F32))
        duv_ref[...] = duv.astype(BF16)
        _acc(dlng_ref, dlng, first)
        _acc(dlnb_ref, dlnb, first)
        _acc(dws_ref, dws, first)
        _acc(dbs_ref, dbs, first)

    return _host_call(
        body, carry, s // t, grid=(s // t,), name=name,
        in_specs=[_row(t, 2 * SG_W, C_UV // (2 * SG_W)), _row(t, SG_W), _full((1, SG_W)), _full((1, SG_W)),
                  _full((4, LANES, LANES)), _full((LANES, LANES))],
        out_specs=[_row(t, 2 * SG_W), _full((1, SG_W)), _full((1, SG_W)), _full((4, LANES, LANES)), _full((LANES, LANES))],
        out_shape=[_sds((s, 2 * SG_W), BF16), _sds((1, SG_W)), _sds((1, SG_W)), _sds((4, LANES, LANES)), _sds((LANES, LANES))],
        scratch_shapes=[], operands=(p, dob, lng, lnb, w_s, bs_t))


def dn_bwd(p, sst, tinv, doa, cq, a_row, dtb_row, nw_row, name, carry=None):
    s = p.shape[0]
    t = _tile(s, DN_TILE)
    nt = s // t

    def body(qkv_ref, halo_ref, z_ref, ba_ref, sst_ref, tinv_ref, doa_ref, cq_ref, a_ref, dtb_ref, nw_ref,
             dqkv_ref, dz_ref, dba_ref, dcq_ref, da_ref, ddtb_ref, dnw_ref, ds_scr, carry):
        i = pl.program_id(0)
        j = nt - 1 - i
        first = i == 0

        @pl.when(first)
        def _():
            ds_scr[...] = jnp.zeros_like(ds_scr)
            carry[...] = jnp.zeros_like(carry)

        halo = jnp.where(j == 0, 0.0, halo_ref[...])
        qkvcat = jnp.concatenate([halo, qkv_ref[...]], axis=0)
        cw = tuple(cq_ref[k:k + 1, :] for k in range(4))
        t_saved = tinv_ref[0]
        _, vjp = jax.vjp(lambda *args: _dn_glue(*args, t_saved=t_saved)[:2],
                         qkvcat, z_ref[...], ba_ref[...], sst_ref[0], cw, a_ref[...], dtb_ref[...], nw_ref[...])
        dqkvcat, dz, dba, ds_in, dcw, da, ddtb, dnw = vjp((doa_ref[...].astype(F32), ds_scr[...]))
        ds_scr[...] = ds_in
        dz_ref[...] = dz.astype(BF16)
        dba_ref[...] = dba.astype(BF16)
        dtile = dqkvcat[HALO:]
        dqkv_ref[...] = dtile.astype(BF16)
        dqkv_ref[t - HALO:t, :] = (dtile[t - HALO:] + carry[...]).astype(BF16)
        carry[...] = dqkvcat[:HALO]
        for k in range(4):
            _acc(dcq_ref.at[k:k + 1, :], dcw[k], first)
        _acc(da_ref, da, first)
        _acc(ddtb_ref, ddtb, first)
        _acc(dnw_ref, dnw, first)

    return _host_call(
        body, carry, nt, grid=(nt,), name=name,
        in_specs=[_row_rev(t, QKV_W, nt), _halo(t, QKV_W, nt), _row_rev(t, Z_W, nt, C_Z // Z_W),
                  _row_rev(t, LANES, nt, C_BA // LANES),
                  pl.BlockSpec((1, HEADS, DK, DK), lambda i: (nt - 1 - i, 0, 0, 0)),
                  pl.BlockSpec((1,) + tinv.shape[1:], lambda i: (nt - 1 - i, 0, 0, 0)), _row_rev(t, Z_W, nt),
                  _full((4, QKV_W)), _full((1, LANES)), _full((1, LANES)), _full((1, LANES))],
        out_specs=[_row_rev(t, QKV_W, nt), _row_rev(t, Z_W, nt), _row_rev(t, LANES, nt),
                   _full((4, QKV_W)), _full((1, LANES)), _full((1, LANES)), _full((1, LANES))],
        out_shape=[_sds((s, QKV_W), BF16), _sds((s, Z_W), BF16), _sds((s, LANES), BF16),
                   _sds((4, QKV_W)), _sds((1, LANES)), _sds((1, LANES)), _sds((1, LANES))],
        scratch_shapes=[pltpu.VMEM((HEADS, DK, DK), F32), pltpu.VMEM((HALO, QKV_W), F32)],
        operands=(p, p, p, p, sst, tinv, doa, cq, a_row, dtb_row, nw_row))


def proj_bwd(dps, dxd, w, name, carry=None):
    s = dxd.shape[0]
    t = _tile(s, 512)
    n = len(dps)

    def body(*refs):
        dp_refs, dxd_ref, w_hbm, dx_ref, w_v = refs[:n], refs[n], refs[n + 1], refs[n + 2], refs[n + 3]

        @pl.when(pl.program_id(0) == 0)
        def _():
            pltpu.sync_copy(w_hbm, w_v)

        dp = jnp.concatenate([r[...] for r in dp_refs], axis=1)
        dx_ref[...] = dxd_ref[...] + _mm_nt(dp, w_v[...])

    return _host_call(
        body, carry, s // t, grid=(s // t,), name=name,
        in_specs=[_row(t, dp.shape[1]) for dp in dps] + [_row(t, D), ANY],
        out_specs=[_row(t, D)], out_shape=[_sds((s, D))],
        scratch_shapes=[pltpu.VMEM((D, IN_COLS_PAD), BF16)], operands=(*dps, dxd, w))


def wgrad(x, dp, col, into, name):
    s, n = dp.shape
    tk = _tile(s, 1024)
    tn = next(c for c in (1024, 768, 512, 256, 128) if n % c == 0 and col % c == 0)
    block = col // tn

    def body(x_ref, dp_ref, *rest):
        o_ref = rest[-1]
        _acc(o_ref, _mm_tn(x_ref[...], dp_ref[...]), pl.program_id(1) == 0)

    operands = (x, dp) if into is None else (x, dp, into)
    return pl.pallas_call(
        body, grid=(n // tn, s // tk), name=name,
        in_specs=[pl.BlockSpec((tk, D), lambda j, k: (k, 0)), pl.BlockSpec((tk, tn), lambda j, k: (k, j))]
        + ([] if into is None else [ANY]),
        out_specs=pl.BlockSpec((D, tn), lambda j, k: (0, block + j)), out_shape=_sds((D, IN_COLS_PAD)),
        input_output_aliases={} if into is None else {2: 0},
        compiler_params=_params(2))(*operands)


def _rows_block(rows, cols):
    cap = max(HALO, (2 * 1024 * 1024) // (cols * 4))
    for cand in range(min(rows, cap) // HALO * HALO, HALO - 1, -HALO):
        if rows % cand == 0:
            return cand
    return rows


def adam_call(w, g, m, v, name):
    rows, cols = w.shape
    tr = _rows_block(rows, cols)
    c1 = 1.0 - ADAM_B1 ** ADAM_STEP
    c2 = 1.0 - ADAM_B2 ** ADAM_STEP

    def body(w_ref, g_ref, m_ref, v_ref, go_ref, d_ref, nm_ref, nv_ref):
        gv = g_ref[...]
        go_ref[...] = gv
        nm = ADAM_B1 * m_ref[...] + (1.0 - ADAM_B1) * gv
        nv = ADAM_B2 * v_ref[...] + (1.0 - ADAM_B2) * (gv * gv)
        d_ref[...] = -ADAM_LR * ((nm / c1) / (jnp.sqrt(nv / c2) + ADAM_EPS) + ADAM_WD * w_ref[...])
        nm_ref[...] = nm
        nv_ref[...] = nv

    spec = pl.BlockSpec((tr, cols), lambda i: (i, 0))
    return pl.pallas_call(
        body, grid=(rows // tr,), name=name, in_specs=[spec] * 4, out_specs=[spec] * 4,
        out_shape=[_sds((rows, cols))] * 4, compiler_params=_params())(w, g, m, v)


def _place():
    return lax.axis_index("x"), lax.axis_index("y"), lax.axis_index("c")


def _other_chips(x, y):
    return [(1 - x, y), (x, 1 - y), (1 - x, 1 - y)]


def _remote(src, dst, send_sem, recv_sem, to):
    return pltpu.make_async_remote_copy(src_ref=src, dst_ref=dst, send_sem=send_sem, recv_sem=recv_sem,
                                        device_id=to, device_id_type=MESH)


class _Gather:
    def __init__(self, ins, outs, send_sems, recv_sems, local_sems):
        self.ins, self.outs, self.n = ins, outs, len(ins)
        self.send_sems, self.recv_sems, self.local_sems = send_sems, recv_sems, local_sems
        self.x, self.y, self.c = _place()
        self.me = 2 * self.x + self.y
        self.chips = _other_chips(self.x, self.y)

    def _copy(self, t, k, slot, part, to, src=None):
        dst = self.outs[t].at[slot, part]
        return _remote(dst if src is None else src, dst, self.send_sems.at[6 * t + k], self.recv_sems.at[6 * t + k], to)

    def _mine(self):
        return [pltpu.make_async_copy(self.ins[t].at[p], self.outs[t].at[self.me, p], self.local_sems.at[2 * t + p])
                for t in range(self.n) for p in range(2)]

    def _first(self):
        return [self._copy(t, k, self.me, self.c, (cx, cy, self.c), src=self.ins[t].at[self.c])
                for k, (cx, cy) in enumerate(self.chips) for t in range(self.n)]

    def start(self):
        for cp in self._mine() + self._first():
            cp.start()

    def finish(self):
        x, y, c = self.x, self.y, self.c
        passed = []
        for k, (cx, cy) in enumerate(self.chips):
            for t in range(self.n):
                self._copy(t, k, 2 * cx + cy, c, (x, y, c)).wait_recv()
                passed.append(self._copy(t, 3 + k, 2 * cx + cy, c, (x, y, 1 - c)))
                passed[-1].start()
        for k, (cx, cy) in enumerate(self.chips):
            for t in range(self.n):
                self._copy(t, 3 + k, 2 * cx + cy, 1 - c, (x, y, c)).wait_recv()
        for cp in self._first() + passed:
            cp.wait_send()
        for cp in self._mine():
            cp.wait()


class _Carried(NamedTuple):
    ins: tuple
    out_shapes: tuple
    scratch: tuple
    make: Callable
    aliases: dict


def _host_call(body, carry, steps, *, grid, name, in_specs, out_specs, out_shape, scratch_shapes, operands):
    in_specs, out_specs, out_shape, scratch_shapes = list(in_specs), list(out_specs), list(out_shape), list(scratch_shapes)
    aliases = {}
    if carry is not None:
        n_in, n_out, n_scr = len(in_specs), len(out_specs), len(scratch_shapes)
        n_ci, n_co = len(carry.ins), len(carry.out_shapes)
        plain = body

        def body(*refs):
            ins, cins = refs[:n_in], refs[n_in:n_in + n_ci]
            outs = refs[n_in + n_ci:n_in + n_ci + n_out]
            couts = refs[n_in + n_ci + n_out:n_in + n_ci + n_out + n_co]
            rest = refs[n_in + n_ci + n_out + n_co:]
            exchange = carry.make(cins, couts, *rest[n_scr:])
            pl.when(pl.program_id(0) == 0)(exchange.start)
            plain(*ins, *outs, *rest[:n_scr])
            pl.when(pl.program_id(0) == steps - 1)(exchange.finish)

        aliases = {n_in + i: n_out + j for i, j in carry.aliases.items()}
        in_specs += [ANY] * n_ci
        out_specs += [ANY] * n_co
        out_shape += list(carry.out_shapes)
        scratch_shapes += list(carry.scratch)
        operands = tuple(operands) + tuple(carry.ins)
    return pl.pallas_call(
        body, grid=grid, name=name, in_specs=in_specs, out_specs=out_specs, out_shape=out_shape,
        scratch_shapes=scratch_shapes, input_output_aliases=aliases, compiler_params=_params(len(grid)))(*operands)


def exchange(carry, name):
    n_i, n_o = len(carry.ins), len(carry.out_shapes)

    def body(*refs):
        ex = carry.make(refs[:n_i], refs[n_i:n_i + n_o], *refs[n_i + n_o:])
        ex.start()
        ex.finish()

    return pl.pallas_call(
        body, name=name, in_specs=[ANY] * n_i, out_specs=[ANY] * n_o, out_shape=list(carry.out_shapes),
        scratch_shapes=list(carry.scratch), input_output_aliases=dict(carry.aliases),
        compiler_params=pltpu.CompilerParams(vmem_limit_bytes=VMEM_LIMIT))(*carry.ins)


def _dma_sems(*counts):
    return tuple(pltpu.SemaphoreType.DMA((n,)) for n in counts)


def carried_gather(shards):
    n = len(shards)
    return _Carried(tuple(shards), tuple(_sds((N_CHIPS,) + a.shape, a.dtype) for a in shards),
                    _dma_sems(6 * n, 6 * n, 2 * n), _Gather, {})


class _PairSwap:
    def __init__(self, ins, outs, send_sems, recv_sems):
        x, y, c = _place()
        self.copies = [_remote(ins[t].at[:, 1 - c], outs[t], send_sems.at[t], recv_sems.at[t], (x, y, 1 - c))
                       for t in range(len(ins))]

    def start(self):
        for cp in self.copies:
            cp.start()

    def finish(self):
        for cp in self.copies:
            cp.wait()


def carried_pair_swap(views):
    n = len(views)
    return _Carried(tuple(views), tuple(_sds((v.shape[0],) + v.shape[2:]) for v in views), _dma_sems(n, n), _PairSwap, {})


class _Scatter:
    def __init__(self, srcs, outs, send_sems, recv_sems, local_sems, pieces):
        self.srcs, self.outs, self.pieces, self.n = srcs, outs, pieces, len(pieces)
        self.send_sems, self.recv_sems, self.local_sems = send_sems, recv_sems, local_sems
        self.x, self.y, self.c = _place()
        self.me = 2 * self.x + self.y

    def _piece(self, t, k):
        idx, lead, cols = self.pieces[t][k]
        ref = self.srcs[idx].at[lead]
        return ref if cols is None else ref.at[:, pl.ds(cols[0], cols[1])]

    def _local(self, t, k):
        return pltpu.make_async_copy(self._piece(t, k), self.outs[t].at[k], self.local_sems.at[t])

    def _each_chip(self, mine, others):
        for k in range(N_CHIPS):
            pl.when(self.me == k)(functools.partial(mine, k))
            pl.when(self.me != k)(functools.partial(others, k))

    def start(self):
        def mine(k):
            for t in range(self.n):
                self._local(t, k).start()

        def others(k):
            for t in range(self.n):
                _remote(self._piece(t, k), self.outs[t].at[self.me], self.send_sems.at[N_CHIPS * t + k],
                        self.recv_sems.at[N_CHIPS * t + self.me], (k // 2, k % 2, self.c)).start()

        self._each_chip(mine, others)

    def finish(self):
        def mine(k):
            for t in range(self.n):
                self._local(t, k).wait()

        def others(k):
            for t in range(self.n):
                cp = _remote(self._piece(t, k), self.outs[t].at[k], self.send_sems.at[N_CHIPS * t + k],
                             self.recv_sems.at[N_CHIPS * t + k], (self.x, self.y, self.c))
                cp.wait_recv()
                cp.wait_send()

        self._each_chip(mine, others)


def carried_scatter(srcs, pieces, part_shapes):
    n = len(pieces)
    return _Carried(tuple(srcs), tuple(_sds((N_CHIPS,) + tuple(shp), srcs[0].dtype) for shp in part_shapes),
                    _dma_sems(N_CHIPS * n, N_CHIPS * n, n), functools.partial(_Scatter, pieces=pieces), {})


class _PairJoin:
    def __init__(self, ins, outs, send_sems, recv_sems, layer):
        self.ins, self.outs, self.layer, self.n = ins, outs, layer, len(ins)
        self.send_sems, self.recv_sems = send_sems, recv_sems
        self.x, self.y, self.c = _place()

    def _copy(self, t, half, to):
        return _remote(self.ins[t].at[self.layer, self.c], self.outs[t].at[self.layer, half],
                       self.send_sems.at[t], self.recv_sems.at[t], to)

    def start(self):
        for t in range(self.n):
            self._copy(t, self.c, (self.x, self.y, 1 - self.c)).start()

    def finish(self):
        for t in range(self.n):
            self._copy(t, self.c, (self.x, self.y, 1 - self.c)).wait_send()
            self._copy(t, 1 - self.c, (self.x, self.y, self.c)).wait_recv()


def carried_join(bufs, layer):
    n = len(bufs)
    return _Carried(tuple(bufs), tuple(_sds(b.shape) for b in bufs), _dma_sems(n, n),
                    functools.partial(_PairJoin, layer=layer), {t: t for t in range(n)})


def pair_add_half(mine, theirs, c_vec, name):
    g, _, h, b = mine.shape
    tr = _rows_block(h, b)

    def body(c_ref, a_ref, b_ref, o_ref):
        o_ref[...] = (a_ref[...] + b_ref[...]).astype(BF16)

    part = pl.BlockSpec((None, tr, b), lambda j, i, c: (j, i, 0))
    grid_spec = pltpu.PrefetchScalarGridSpec(
        num_scalar_prefetch=1, grid=(g, h // tr),
        in_specs=[pl.BlockSpec((None, None, tr, b), lambda j, i, c: (j, c[0], i, 0)), part], out_specs=part)
    return pl.pallas_call(body, grid_spec=grid_spec, name=name, out_shape=_sds((g, h, b), BF16),
                          compiler_params=_params(2))(c_vec, mine, theirs)


def chips_add_into(recv, into, layer, c_vec, name):
    n, h, b = recv.shape
    tr = _rows_block(h, b)

    def body(c_ref, r0, r1, r2, r3, *rest):
        rest[-1][...] = ((r0[...].astype(F32) + r1[...].astype(F32)) + r2[...].astype(F32)) + r3[...].astype(F32)

    grid_spec = pltpu.PrefetchScalarGridSpec(
        num_scalar_prefetch=1, grid=(h // tr,),
        in_specs=[pl.BlockSpec((None, tr, b), lambda i, c, k=k: (k, i, 0)) for k in range(n)]
        + ([] if into is None else [ANY]),
        out_specs=pl.BlockSpec((None, None, tr, b), lambda i, c: (layer, c[0], i, 0)))
    return pl.pallas_call(
        body, grid_spec=grid_spec, name=name, out_shape=_sds((2, 2, h, b)),
        input_output_aliases={} if into is None else {1 + n: 0},
        compiler_params=_params())(c_vec, *([recv] * n), *(() if into is None else (into,)))


N_DEV = 8


class _AllSum:
    def __init__(self, ins, outs, buf, total, send_sems, recv_sems, local_sem):
        self.v, self.out, self.buf, self.total = ins[0], outs[0], buf, total
        self.send_sems, self.recv_sems, self.local_sem = send_sems, recv_sems, local_sem
        self.x, self.y, self.c = _place()
        self.me, self.sibling = (self.x, self.y, self.c), (self.x, self.y, 1 - self.c)
        self.chips = _other_chips(self.x, self.y)

    def _slot(self, px, py, pc):
        return self.buf.at[4 * px + 2 * py + pc]

    def _copy(self, k, block, to):
        return _remote(self._slot(*block), self._slot(*block), self.send_sems.at[k], self.recv_sems.at[k], to)

    def _first(self):
        return [self._copy(0, self.me, self.sibling)] + [
            self._copy(1 + k, self.me, (*chip, self.c)) for k, chip in enumerate(self.chips)]

    def start(self):
        load = pltpu.make_async_copy(self.v, self._slot(*self.me), self.local_sem)
        load.start()
        load.wait()
        for cp in self._first():
            cp.start()

    def finish(self):
        c = self.c
        passed = [self._copy(4 + k, (*chip, c), self.sibling) for k, chip in enumerate(self.chips)]
        for k, chip in enumerate(self.chips):
            self._copy(1 + k, (*chip, c), self.me).wait_recv()
            passed[k].start()
        self._copy(0, self.sibling, self.me).wait_recv()
        for k, chip in enumerate(self.chips):
            self._copy(4 + k, (*chip, 1 - c), self.me).wait_recv()
        for cp in self._first() + passed:
            cp.wait_send()
        acc = self.buf[0]
        for d in range(1, N_DEV):
            acc = acc + self.buf[d]
        self.total[...] = acc
        store = pltpu.make_async_copy(self.total, self.out, self.local_sem)
        store.start()
        store.wait()


def carried_allsum(v):
    rows, lanes = v.shape
    scratch = (pltpu.VMEM((N_DEV, rows, lanes), F32), pltpu.VMEM((rows, lanes), F32)) + _dma_sems(7, 7) + (
        pltpu.SemaphoreType.DMA,)
    return _Carried((v,), (_sds((rows, lanes)),), scratch, _AllSum, {})


class _Both:
    def __init__(self, *exchanges):
        self.exchanges = exchanges

    def start(self):
        for ex in self.exchanges:
            ex.start()

    def finish(self):
        for ex in self.exchanges:
            ex.finish()


def carried_both(a, b):
    ai, ao, asc = len(a.ins), len(a.out_shapes), len(a.scratch)

    def make(ins, outs, *scratch):
        return _Both(a.make(ins[:ai], outs[:ao], *scratch[:asc]), b.make(ins[ai:], outs[ao:], *scratch[asc:]))

    aliases = {**a.aliases, **{ai + i: ao + j for i, j in b.aliases.items()}}
    return _Carried(a.ins + b.ins, a.out_shapes + b.out_shapes, a.scratch + b.scratch, make, aliases)


BIG = ("w_in", "w_branch_a", "w_branch_b", "w_out", "w_up", "w_down")
CONV = ("conv_qkv", "conv_ffn")
REPL =("a_log", "dt_bias", "dn_norm_w", "sg_ln_g", "sg_ln_b", "w_spatial", "b_spatial", "ln1_g", "ln1_b", "ln2_g", "ln2_b")


def _pad_rows(flat, mult):
    n = flat.shape[0]
    unit = mult * LANES
    total = -(-n // unit) * unit
    return jnp.pad(flat, (0, total - n)).reshape(total // LANES, LANES)


def _pack(arrs, mult):
    return _pad_rows(jnp.concatenate([a.reshape(-1) for a in arrs]), mult)


def _unpack(flat, shapes):
    out, off = [], 0
    for shp in shapes:
        n = math.prod(shp)
        out.append(flat[off:off + n].reshape(shp))
        off += n
    return out


def kernel(x, w_in, conv_qkv, a_log, dt_bias, dn_norm_w, w_branch_a, sg_ln_g, sg_ln_b, w_spatial, b_spatial, w_branch_b, w_out, ln1_g, ln1_b, w_up, conv_ffn, w_down, ln2_g, ln2_b, loss_target, m_w_in, m_conv_qkv, m_a_log, m_dt_bias, m_dn_norm_w, m_w_branch_a, m_sg_ln_g, m_sg_ln_b, m_w_spatial, m_b_spatial, m_w_branch_b, m_w_out, m_ln1_g, m_ln1_b, m_w_up, m_conv_ffn, m_w_down, m_ln2_g, m_ln2_b, v_w_in, v_conv_qkv, v_a_log, v_dt_bias, v_dn_norm_w, v_w_branch_a, v_sg_ln_g, v_sg_ln_b, v_w_spatial, v_b_spatial, v_w_branch_b, v_w_out, v_ln1_g, v_ln1_b, v_w_up, v_conv_ffn, v_w_down, v_ln2_g, v_ln2_b):
    names = ("w_in", "conv_qkv", "a_log", "dt_bias", "dn_norm_w", "w_branch_a", "sg_ln_g", "sg_ln_b", "w_spatial",
             "b_spatial", "w_branch_b", "w_out", "ln1_g", "ln1_b", "w_up", "conv_ffn", "w_down", "ln2_g", "ln2_b")
    w = dict(zip(names, (w_in, conv_qkv, a_log, dt_bias, dn_norm_w, w_branch_a, sg_ln_g, sg_ln_b, w_spatial,
                         b_spatial, w_branch_b, w_out, ln1_g, ln1_b, w_up, conv_ffn, w_down, ln2_g, ln2_b)))
    m = dict(zip(names, (m_w_in, m_conv_qkv, m_a_log, m_dt_bias, m_dn_norm_w, m_w_branch_a, m_sg_ln_g, m_sg_ln_b,
                         m_w_spatial, m_b_spatial, m_w_branch_b, m_w_out, m_ln1_g, m_ln1_b, m_w_up, m_conv_ffn,
                         m_w_down, m_ln2_g, m_ln2_b)))
    v = dict(zip(names, (v_w_in, v_conv_qkv, v_a_log, v_dt_bias, v_dn_norm_w, v_w_branch_a, v_sg_ln_g, v_sg_ln_b,
                         v_w_spatial, v_b_spatial, v_w_branch_b, v_w_out, v_ln1_g, v_ln1_b, v_w_up, v_conv_ffn,
                         v_w_down, v_ln2_g, v_ln2_b)))
    chip = 2 * lax.axis_index("x") + lax.axis_index("y")
    s = x.shape[1]
    xs = x.reshape(s, D)
    tgt = loss_target.reshape(s, D)

    big_names, conv_names = list(BIG), list(CONV)

    def in_two(name, l):
        rows, cols = w[name].shape[1:]
        return w[name][l].astype(BF16).reshape(2, rows // 2, cols)

    def whole(name, landed):
        rows, cols = w[name].shape[1:]
        return landed.reshape(N_CHIPS, rows, cols)

    first = exchange(carried_gather([in_two("w_in", 0)] + [w[n] for n in conv_names]), "gather_first")
    got = [{"w_in": whole("w_in", first[0])}, {}]
    conv_taps = dict(zip(conv_names, first[1:]))
    narrow, wide = ["w_branch_a", "w_branch_b", "w_out"], ["w_up", "w_down"]
    carried = {"proj_fwd0": (0, narrow), "dn_fwd0": (0, wide), "ffn_fwd0": (1, ["w_in"] + narrow), "dn_fwd1": (1, wide)}

    def carry(call):
        if call not in carried:
            return None
        l, which = carried[call]
        return carried_gather([in_two(n, l) for n in which])

    def land(call, landed):
        l, which = carried.get(call, (0, []))
        got[l].update({n: whole(n, a) for n, a in zip(which, landed)})

    def lane_row(vec, off):
        return jnp.zeros((1, LANES), F32).at[0, off:off + vec.shape[0]].set(vec)

    def side_by_side(blocks):
        return jnp.concatenate([blocks[k] for k in range(N_CHIPS)], axis=1)

    def small_params(l):
        return dict(
            cq=side_by_side(conv_taps["conv_qkv"][:, l]),
            a_row=lane_row(w["a_log"][l], HEADS), dtb_row=lane_row(w["dt_bias"][l], HEADS),
            nw_row=w["dn_norm_w"][l].reshape(1, DK),
            lng=w["sg_ln_g"][l].reshape(1, SG_W), lnb=w["sg_ln_b"][l].reshape(1, SG_W),
            w_s=w["w_spatial"][l], bs_t=jnp.zeros((LANES, LANES), F32).at[:, :4].set(w["b_spatial"][l].T),
            g1=w["ln1_g"][l].reshape(1, D), b1=w["ln1_b"][l].reshape(1, D),
            cf=conv_taps["conv_ffn"][:, l],
            g2=w["ln2_g"][l].reshape(1, D), b2=w["ln2_b"][l].reshape(1, D))

    layers, saved = [], []
    h_in = xs
    for l in range(DEPTH):
        p = small_params(l)
        wi = side_by_side(got[l]["w_in"])
        p["w_in"] = jnp.concatenate([wi[:, :2048], wi[:, 2056:3080], wi[:, 3080:5128], wi[:, 2048:2056],
                                     jnp.zeros((D, IN_COLS_PAD - 5128), BF16)], axis=1)
        proj, *landed = proj_fwd(h_in, p["w_in"], f"proj_fwd{l}", carry=carry(f"proj_fwd{l}"))
        land(f"proj_fwd{l}", landed)
        oa, sst, tinv, *landed = dn_fwd(proj, p["cq"], p["a_row"], p["dtb_row"], p["nw_row"], f"dn_fwd{l}",
                                  carry=carry(f"dn_fwd{l}"))
        land(f"dn_fwd{l}", landed)
        ob = sg_fwd(proj, p["lng"], p["lnb"], p["w_s"], p["bs_t"], f"sg_fwd{l}")
        p.update(wa=side_by_side(got[l]["w_branch_a"]), wb=side_by_side(got[l]["w_branch_b"]),
                 wo=got[l]["w_out"].reshape(D, D))
        x1 = merge_fwd(oa, ob, proj, h_in, p["wa"], p["wb"], p["wo"], p["g1"], p["b1"], f"merge_fwd{l}")
        pre2, x2, *landed = ffn_fwd(x1, got[l]["w_up"], p["cf"], got[l]["w_down"], p["g2"], p["b2"], f"ffn_fwd{l}",
                                    carry=carry(f"ffn_fwd{l}"))
        land(f"ffn_fwd{l}", landed)
        layers.append(p)
        saved.append(dict(x=h_in, proj=proj, oa=oa, ob=ob, sst=sst, tinv=tinv, x1=x1, pre2=pre2))
        h_in = x2


    small_names = conv_names + list(REPL)
    grads = {n: [None] * DEPTH for n in small_names}
    c_vec = jnp.stack([lax.axis_index("c")]).astype(jnp.int32)
    tags = ("w_in", "w_a", "w_b", "w_out", "w_up0", "w_up1", "w_dn0", "w_dn1")
    groups = (1, 1, 1, N_CHIPS, 2, 2, 2, 2)
    ab_cols = w["w_branch_a"].shape[2]
    pieces = [
        [(0, (k,), None) for k in range(N_CHIPS)],
        [(1, (0,), (k * ab_cols, ab_cols)) for k in range(N_CHIPS)],
        [(2, (0,), (k * ab_cols, ab_cols)) for k in range(N_CHIPS)],
        [(3, (k,), None) for k in range(N_CHIPS)],
        [(4 + k % 2, (k // 2,), None) for k in range(N_CHIPS)],
        [(6 + k // 2, (k % 2,), None) for k in range(N_CHIPS)],
    ]
    part_shapes = [(w[n].shape[1] // 2, w[n].shape[2]) for n in big_names]
    arrays_of = ((0,), (1,), (2,), (3,), (4, 5), (6, 7))
    rest, ffn_part = (0, 1, 2, 3), (4, 5)
    bufs = {}

    def arrays(which):
        return [i for t in which for i in arrays_of[t]]

    def views(which, arrs):
        return [a.reshape(groups[i], 2, a.size // a.shape[-1] // (2 * groups[i]), a.shape[-1])
                for i, a in zip(arrays(which), arrs)]

    def pair_sums(l, which, mine, theirs):
        ids = arrays(which)
        sums = [pair_add_half(m_, t_, c_vec, f"reduce_pair_add{l}_{tags[i]}") for i, m_, t_ in zip(ids, mine, theirs)]
        if ids[0] == 0:
            pin = sums[0][0]
            natural = jnp.concatenate([pin[:, :2048], pin[:, C_BA:C_BA + 8], pin[:, 2048:C_BA]], axis=1)
            sums[0] = jnp.stack(jnp.split(natural, N_CHIPS, axis=1))
        return sums

    def scatter_of(which, srcs):
        place = {i: j for j, i in enumerate(arrays(which))}
        return carried_scatter(srcs, [[(place[i], lead, cols) for i, lead, cols in pieces[t]] for t in which],
                               [part_shapes[t] for t in which])

    def chip_sums(l, which, recv):
        for t, r in zip(which, recv):
            n = big_names[t]
            bufs[n] = chips_add_into(r, bufs.get(n), l, c_vec, f"reduce_chips_add{l}_{n}")
        return [bufs[big_names[t]] for t in which]

    def keep(which, joined):
        bufs.update({big_names[t]: b for t, b in zip(which, joined)})

    above = None
    for l in reversed(range(DEPTH)):
        p, a = layers[l], saved[l]
        if l == DEPTH - 1:
            dpre2, dg2, db2, loss_part = loss_ln_bwd(a["pre2"], tgt, p["g2"], p["b2"], "loss_ln2_bwd")
            loss = lax.psum(loss_part[0, 0], ("x", "y", "c"))
        else:
            dpre2, dg2, db2 = ln_bwd(a["pre2"], dy, p["g2"], p["b2"], f"ln2_bwd{l}")
        dx1, dcf0, *half0 = ffn_bwd(
            a["x1"], dpre2, dpre2, ALPHA, got[l]["w_up"], p["cf"], got[l]["w_down"], 0, f"ffn_bwd{l}a",
            carry=carried_pair_swap(above) if above else None)
        theirs = half0[3:]
        srcs = pair_sums(l + 1, rest, above, theirs) if above else None
        dx1, dcf1, *half1 = ffn_bwd(
            a["x1"], dpre2, dx1, 1.0, got[l]["w_up"], p["cf"], got[l]["w_down"], 1, f"ffn_bwd{l}b",
            carry=scatter_of(rest, srcs) if above else None)
        recv = half1[3:]
        summed = chip_sums(l + 1, rest, recv) if above else None
        ffn_grads = []
        for h, (dua, dub, act) in enumerate((half0[:3], half1[:3])):
            dwup = wgrad_mm(a["x1"], dua, 2, 0, None, f"wgrad_up{l}{'ab'[h]}_a")
            ffn_grads.append(wgrad_mm(a["x1"], dub, 2, 1, dwup, f"wgrad_up{l}{'ab'[h]}_b"))
        for h, (dua, dub, act) in enumerate((half0[:3], half1[:3])):
            ffn_grads.append(wgrad_mm(act, dpre2, 1, 0, None, f"wgrad_down{l}{'ab'[h]}")[0])
        dwup0, dwup1, dwdn0, dwdn1 = ffn_grads
        doa, dob, dga, dgb, dxd, dwa, dwb, dwo, dg1, db1, *joined = merge_bwd(
            a["oa"], a["ob"], a["proj"], a["x"], dx1, p["wa"], p["wb"], p["wo"], p["g1"], p["b1"], f"merge_bwd{l}",
            carry=carried_join(summed, l + 1) if above else None)
        keep(rest, joined)
        mine = views(ffn_part, [dwup0, dwup1, dwdn0, dwdn1])
        duv, dlng, dlnb, dws, dbs, *theirs = sg_bwd(a["proj"], dob, p["lng"], p["lnb"], p["w_s"], p["bs_t"], f"sg_bwd{l}",
                                                    carry=carried_pair_swap(mine))
        srcs = pair_sums(l, ffn_part, mine, theirs)
        dqkv, dz, dba, dcq, da, ddtb, dnw, *recv = dn_bwd(
            a["proj"], a["sst"], a["tinv"], doa, p["cq"], p["a_row"], p["dtb_row"], p["nw_row"], f"dn_bwd{l}",
            carry=scatter_of(ffn_part, srcs))
        summed = chip_sums(l, ffn_part, recv)
        dy, *joined = proj_bwd([dqkv, dz, duv, dga, dgb, dba], dxd, p["w_in"], f"proj_bwd{l}",
                               carry=carried_join(summed, l))
        keep(ffn_part, joined)
        dwi = None
        for tag, dp, col in (("qkv", dqkv, 0), ("z", dz, C_Z), ("uv", duv, C_UV), ("ga", dga, C_GA), ("gb", dgb, C_GB),
                             ("ba", dba, C_BA)):
            dwi = wgrad(a["x"], dp, col, dwi, f"wgrad_in{l}_{tag}")

        above = views(rest, [dwi, dwa, dwb, dwo])
        grads["conv_qkv"][l] = dcq
        grads["conv_ffn"][l] = jnp.concatenate([dcf0[0], dcf1[0], dcf0[1], dcf1[1]], axis=1)
        grads["a_log"][l] = da[0, HEADS:2 * HEADS]
        grads["dt_bias"][l] = ddtb[0, HEADS:2 * HEADS]
        grads["dn_norm_w"][l] = dnw[0]
        grads["sg_ln_g"][l] = dlng[0]
        grads["sg_ln_b"][l] = dlnb[0]
        grads["w_spatial"][l] = dws
        grads["b_spatial"][l] = dbs[:, :4].T
        grads["ln1_g"][l] = dg1[0]
        grads["ln1_b"][l] = db1[0]
        grads["ln2_g"][l] = dg2[0]
        grads["ln2_b"][l] = db2[0]
    grad_x = dy.reshape(x.shape)
    g_full = {n: jnp.stack(grads[n]) for n in small_names}

    theirs = exchange(carried_pair_swap(above), "reduce_pair")
    *recv, small = exchange(carried_both(scatter_of(rest, pair_sums(0, rest, above, theirs)),
                                         carried_allsum(_pack([g_full[n] for n in small_names], 8))), "reduce_chips")
    keep(rest, exchange(carried_join(chip_sums(0, rest, recv), 0), "reduce_join"))
    g_shard = {n: bufs[n].reshape(w[n].shape) for n in big_names}

    small_full = dict(zip(small_names, _unpack(small.reshape(-1), [g_full[n].shape for n in small_names])))
    for n in conv_names:
        width = w[n].shape[2]
        g_shard[n] = lax.dynamic_slice_in_dim(small_full[n], chip * width, width, axis=2)
    for n in REPL:
        g_shard[n] = small_full[n]

    delta, new_m, new_v = {}, {}, {}
    for n in big_names:
        shp = w[n].shape
        two_d = (shp[0] * shp[1], shp[2])
        g_, d_, m_, v_ = adam_call(w[n].reshape(two_d), g_shard[n].reshape(two_d), m[n].reshape(two_d), v[n].reshape(two_d), f"adam_{n}")
        g_shard[n], delta[n], new_m[n], new_v[n] = g_.reshape(shp), d_.reshape(shp), m_.reshape(shp), v_.reshape(shp)
    shapes = [w[n].shape for n in small_names]
    packs = [_pack([src[n] for n in small_names], 8) for src in (w, g_shard, m, v)]
    outs = adam_call(*packs, "adam_small")
    for dst, o in zip((delta, new_m, new_v), outs[1:]):
        dst.update(zip(small_names, _unpack(o.reshape(-1), shapes)))

    return (loss, grad_x, *[g_shard[n] for n in names], *[delta[n] for n in names],
            *[new_m[n] for n in names], *[new_v[n] for n in names])
```

```python
import functools
import math
from typing import Callable, NamedTuple

import jax
import jax.numpy as jnp
from jax import lax
from jax.experimental import pallas as pl
from jax.experimental.pallas import tpu as pltpu

F32 = jnp.float32
BF16 = jnp.bfloat16
HI = lax.Precision.HIGHEST
MID = lax.Precision.HIGH
MESH = pl.DeviceIdType.MESH

D = 1024
DEPTH = 2
HEADS = 4
DK = 128
CHUNK = 64
QKV_W = 1536
Z_W = 512
SG_W = 512
FFN = 2816
FFN_HALF = FFN // 2
N_CHIPS = 4
DN_SHARD = FFN // N_CHIPS
LN_EPS = 1e-5
RMS_EPS = 1e-6
L2_EPS = 1e-6
ALPHA = (2 * DEPTH) ** 0.25
ADAM_LR, ADAM_B1, ADAM_B2, ADAM_EPS, ADAM_WD, ADAM_STEP = 0.001, 0.9, 0.999, 1e-08, 0.01, 10

HALO = 16
LANES = 128
IN_COLS_PAD = 5248
C_Z, C_UV, C_GA, C_GB, C_BA = 1536, 2048, 3072, 4096, 5120
VMEM_LIMIT = 56 * 1024 * 1024
DN_TILE = 256


def _params(n_grid=1):
    return pltpu.CompilerParams(dimension_semantics=("arbitrary",) * n_grid, vmem_limit_bytes=VMEM_LIMIT)


def _mm(a, b):
    return jnp.dot(a.astype(BF16), b.astype(BF16), preferred_element_type=F32)


def _mm_nt(a, b):
    return lax.dot_general(a.astype(BF16), b.astype(BF16), (((1,), (1,)), ((), ())), preferred_element_type=F32)


def _mm_tn(a, b):
    return lax.dot_general(a.astype(BF16), b.astype(BF16), (((0,), (0,)), ((), ())), preferred_element_type=F32)


def _bdot(a, b, prec=MID):
    return lax.dot_general(a, b, (((2,), (1,)), ((0,), (0,))), precision=prec, preferred_element_type=F32)


def _bdot_nt(a, b, prec=MID):
    return lax.dot_general(a, b, (((2,), (2,)), ((0,), (0,))), precision=prec, preferred_element_type=F32)


def _bf16_dot(a, b, contract):
    return lax.dot_general(a.astype(BF16), b.astype(BF16), (contract, ((0,), (0,))), preferred_element_type=F32)


@jax.custom_vjp
def _fdot(a, b):
    return _bf16_dot(a, b, ((2,), (1,)))


def _fdot_fwd(a, b):
    return _fdot(a, b), (a, b)


def _fdot_bwd(res, ct):
    a, b = res
    return _bf16_dot(ct, b, ((2,), (2,))), _bf16_dot(a, ct, ((1,), (1,)))


_fdot.defvjp(_fdot_fwd, _fdot_bwd)


@jax.custom_vjp
def _fdot_nt(a, b):
    return _bf16_dot(a, b, ((2,), (2,)))


def _fdot_nt_fwd(a, b):
    return _fdot_nt(a, b), (a, b)


def _fdot_nt_bwd(res, ct):
    a, b = res
    return _bf16_dot(ct, b, ((2,), (1,))), _bf16_dot(ct, a, ((1,), (1,)))


_fdot_nt.defvjp(_fdot_nt_fwd, _fdot_nt_bwd)


@jax.custom_vjp
def _fdot_tn(a, b):
    return _bf16_dot(a, b, ((1,), (1,)))


def _fdot_tn_fwd(a, b):
    return _fdot_tn(a, b), (a, b)


def _fdot_tn_bwd(res, ct):
    a, b = res
    return _bf16_dot(b, ct, ((2,), (2,))), _bf16_dot(a, ct, ((2,), (1,)))


_fdot_tn.defvjp(_fdot_tn_fwd, _fdot_tn_bwd)


def _stack(parts):
    return jnp.concatenate([p[None] for p in parts], axis=0)


def _ln(x, g, b):
    mu = jnp.mean(x, axis=-1, keepdims=True)
    xc = x - mu
    var = jnp.mean(xc * xc, axis=-1, keepdims=True)
    return xc * lax.rsqrt(var + LN_EPS) * g + b


def _shift_rows(x, s):
    s = s % x.shape[0]
    return x if s == 0 else pltpu.roll(x, s, 0)


@jax.custom_vjp
def _conv(xcat, w):
    k_taps = len(w)
    y = None
    for k in range(k_taps):
        t = _shift_rows(xcat, k_taps - 1 - k)[HALO:] * w[k]
        y = t if y is None else y + t
    return y


def _conv_fwd(xcat, w):
    return _conv(xcat, w), (xcat, w)


def _conv_bwd(res, dy):
    xcat, w = res
    k_taps = len(w)
    dyp = jnp.concatenate([jnp.zeros((HALO, dy.shape[1]), dy.dtype), dy], axis=0)
    dx = None
    dws = []
    for k in range(k_taps):
        shifted = _shift_rows(dyp, -(k_taps - 1 - k))
        t = shifted * w[k]
        dx = t if dx is None else dx + t
        dws.append(jnp.sum(shifted * xcat, axis=0, keepdims=True))
    return dx, tuple(dws)


_conv.defvjp(_conv_fwd, _conv_bwd)


@jax.custom_vjp
def _tri_inv(l):
    n = l.shape[-1]
    r = lax.broadcasted_iota(jnp.int32, (n, n), 0)
    c = lax.broadcasted_iota(jnp.int32, (n, n), 1)
    eye = (r == c).astype(F32)
    p = eye - l
    lp = l
    steps = int(math.log2(n)) - 1
    for i in range(steps):
        dot = _bdot if i < 2 else functools.partial(_bf16_dot, contract=((2,), (1,)))
        lp = dot(lp, lp)
        p = p + dot(p, lp)
    return p


def _tri_inv_fwd(l):
    t = _tri_inv(l)
    return t, t


def _tri_inv_bwd(t, dt):
    tt = jnp.swapaxes(t, 1, 2)
    return (-_bdot(tt, _bdot(dt, tt)),)


_tri_inv.defvjp(_tri_inv_fwd, _tri_inv_bwd)


@jax.custom_vjp
def _tri_inv_saved(l, t):
    return t


def _tri_inv_saved_fwd(l, t):
    return t, t


def _tri_inv_saved_bwd(t, dt):
    return _tri_inv_bwd(t, dt) + (jnp.zeros_like(t),)


_tri_inv_saved.defvjp(_tri_inv_saved_fwd, _tri_inv_saved_bwd)


def _dn_glue(qkvcat, z, ba, s_in, cw, a_row, dtb_row, nw_row, t_saved=None):
    t_rows = z.shape[0]
    nc = t_rows // CHUNK
    nb = nc * HEADS

    qkv = jax.nn.silu(_conv(qkvcat, cw))

    def chunks(t, off):
        return _stack([t[n * CHUNK:(n + 1) * CHUNK, off + h * DK: off + (h + 1) * DK]
                       for n in range(nc) for h in range(HEADS)])

    q = chunks(qkv, 0)
    k = chunks(qkv, 512)
    v = chunks(qkv, 1024)
    q = q * lax.rsqrt(jnp.sum(q * q, axis=-1, keepdims=True) + L2_EPS) * (DK ** -0.5)
    k = k * lax.rsqrt(jnp.sum(k * k, axis=-1, keepdims=True) + L2_EPS)

    lane = lax.broadcasted_iota(jnp.int32, (LANES, HEADS * DK), 0)
    head_of_col = lax.broadcasted_iota(jnp.int32, (LANES, HEADS * DK), 1) // DK
    e_beta = (head_of_col == lane).astype(F32)
    e_g = (head_of_col + HEADS == lane).astype(F32)
    beta_l = jax.nn.sigmoid(ba)
    g_l = -jnp.exp(a_row) * jax.nn.softplus(ba + dtb_row)
    beta = chunks(jnp.dot(beta_l, e_beta, precision=MID, preferred_element_type=F32), 0)
    g = chunks(jnp.dot(g_l, e_g, precision=MID, preferred_element_type=F32), 0)

    r = lax.broadcasted_iota(jnp.int32, (CHUNK, CHUNK), 0)
    c = lax.broadcasted_iota(jnp.int32, (CHUNK, CHUNK), 1)
    causal = r >= c
    strict = r > c
    tril_b = jnp.broadcast_to(causal.astype(F32), (nb, CHUNK, CHUNK))
    gi_b = _bdot(tril_b, g, HI)
    gi = gi_b[:, :, :CHUNK]
    gj = jnp.swapaxes(gi, 1, 2)
    decay = jnp.where(causal, jnp.exp(jnp.where(causal, gi - gj, 0.0)), 0.0)
    kb = k * beta
    l_mat = jnp.where(strict, _fdot_nt(kb, k) * decay, 0.0)
    t_mat = _tri_inv(l_mat) if t_saved is None else _tri_inv_saved(l_mat, t_saved)
    e_gi = jnp.exp(gi_b)
    w_mat = _fdot(t_mat, kb * e_gi)
    u_mat = _fdot(t_mat, v * beta)
    a_qk = _fdot_nt(q, k) * decay
    q_g = q * e_gi
    gl_b = jnp.broadcast_to(jnp.sum(g, axis=1, keepdims=True), g.shape)
    k_d = k * jnp.exp(gl_b - gi_b)
    e_gl = jnp.exp(gl_b)
    g_last = jnp.concatenate([e_gl, e_gl], axis=1)

    state = s_in
    rows = []
    for n in range(nc):
        sl = slice(n * HEADS, (n + 1) * HEADS)
        u_new = u_mat[sl] - _fdot(w_mat[sl], state)
        o_n = _fdot(q_g[sl], state) + _fdot(a_qk[sl], u_new)
        state = state * g_last[sl] + _fdot_tn(k_d[sl], u_new)
        o_n = o_n * lax.rsqrt(jnp.mean(o_n * o_n, axis=-1, keepdims=True) + RMS_EPS) * nw_row
        z_n = _stack([z[n * CHUNK:(n + 1) * CHUNK, h * DK:(h + 1) * DK] for h in range(HEADS)])
        o_n = o_n * jax.nn.silu(z_n)
        rows.append(jnp.concatenate([o_n[h] for h in range(HEADS)], axis=-1))
    return jnp.concatenate(rows, axis=0), state, t_mat


def _sg_glue(uv, lng, lnb, w_s, bs_t):
    t_rows = uv.shape[0]
    y = jax.nn.gelu(uv)
    u = y[:, :SG_W]
    v = _ln(y[:, SG_W:], lng, lnb)
    r = lax.broadcasted_iota(jnp.int32, (LANES, LANES), 0)
    c = lax.broadcasted_iota(jnp.int32, (LANES, LANES), 1)
    wm = jnp.where(r >= c, w_s, 0.0)
    lane = lax.broadcasted_iota(jnp.int32, (LANES, SG_W), 0)
    group_of_col = lax.broadcasted_iota(jnp.int32, (LANES, SG_W), 1) // LANES
    e_grp = (group_of_col == lane).astype(F32)
    bias = jnp.dot(bs_t, e_grp, precision=HI, preferred_element_type=F32)
    outs = []
    for n in range(t_rows // LANES):
        vb = v[n * LANES:(n + 1) * LANES]
        vg = _stack([vb[:, g * LANES:(g + 1) * LANES] for g in range(4)])
        mg = _fdot(wm, vg)
        mixed = jnp.concatenate([mg[g] for g in range(4)], axis=-1) + bias
        outs.append(u[n * LANES:(n + 1) * LANES] * mixed)
    return jnp.concatenate(outs, axis=0)


def _merge_glue(ga, gb, ya, yb):
    return jax.nn.sigmoid(ga) * ya + jax.nn.sigmoid(gb) * yb


def _res_ln_glue(x, r, g, b):
    return _ln(ALPHA * x + r, g, b)


def _ffn_glue(ua, ub, cwa, cwb):
    return jax.nn.silu(_conv(ua, cwa)) * _conv(ub, cwb)


def _row(t, c, col=0):
    return pl.BlockSpec((t, c), lambda i: (i, col))


def _row_rev(t, c, nt, col=0):
    return pl.BlockSpec((t, c), lambda i: (nt - 1 - i, col))


def _halo(t, c, nt=None):
    per = t // HALO
    if nt is None:
        return pl.BlockSpec((HALO, c), lambda i: (jnp.maximum(i * per - 1, 0), 0))
    return pl.BlockSpec((HALO, c), lambda i: (jnp.maximum((nt - 1 - i) * per - 1, 0), 0))


def _full(shape):
    nd = len(shape)
    return pl.BlockSpec(shape, lambda i: (0,) * nd)


ANY = pl.BlockSpec(memory_space=pl.ANY)


def _sds(shape, dtype=F32):
    return jax.ShapeDtypeStruct(shape, dtype)


def _tile(s, want=256):
    for t in (want, 256, 128):
        if s % t == 0:
            return t
    raise ValueError(f"sequence length {s} is not a multiple of 128")


def proj_fwd(x, w, name, carry=None):
    s = x.shape[0]
    t = _tile(s, 512)
    nt = s // t
    segs = [(0, 2048), (2048, 3072), (3072, 4096), (4096, 5120), (5120, IN_COLS_PAD)]

    def body(x_ref, w_ref, p_ref):
        xb = x_ref[...].astype(BF16)
        for lo, hi in segs:
            p_ref[:, lo:hi] = jnp.dot(xb, w_ref[:, lo:hi], preferred_element_type=F32)

    return _host_call(
        body, carry, nt, grid=(nt,), name=name, in_specs=[_row(t, D), _full((D, IN_COLS_PAD))],
        out_specs=[_row(t, IN_COLS_PAD)], out_shape=[_sds((s, IN_COLS_PAD))], scratch_shapes=[], operands=(x, w))


def dn_fwd(p, cq, a_row, dtb_row, nw_row, name, carry=None):
    s = p.shape[0]
    t = _tile(s, DN_TILE)
    nt = s // t

    nb = HEADS * t // CHUNK

    def body(qkv_ref, halo_ref, z_ref, ba_ref, cq_ref, a_ref, dtb_ref, nw_ref, o_ref, sst_ref, tinv_ref, s_scr):
        i = pl.program_id(0)

        @pl.when(i == 0)
        def _():
            s_scr[...] = jnp.zeros_like(s_scr)

        halo = jnp.where(i == 0, 0.0, halo_ref[...])
        qkvcat = jnp.concatenate([halo, qkv_ref[...]], axis=0)
        cw = tuple(cq_ref[k:k + 1, :] for k in range(4))
        s_in = s_scr[...]
        sst_ref[0] = s_in
        o, s_out, t_mat = _dn_glue(qkvcat, z_ref[...], ba_ref[...], s_in, cw, a_ref[...], dtb_ref[...], nw_ref[...])
        o_ref[...] = o.astype(BF16)
        tinv_ref[0] = t_mat
        s_scr[...] = s_out

    return _host_call(
        body, carry, nt, grid=(nt,), name=name,
        in_specs=[_row(t, QKV_W), _halo(t, QKV_W), _row(t, Z_W, C_Z // Z_W), _row(t, LANES, C_BA // LANES),
                  _full((4, QKV_W)), _full((1, LANES)), _full((1, LANES)), _full((1, LANES))],
        out_specs=[_row(t, Z_W), pl.BlockSpec((1, HEADS, DK, DK), lambda i: (i, 0, 0, 0)),
                   pl.BlockSpec((1, nb, CHUNK, CHUNK), lambda i: (i, 0, 0, 0))],
        out_shape=[_sds((s, Z_W), BF16), _sds((nt, HEADS, DK, DK)), _sds((nt, nb, CHUNK, CHUNK))],
        scratch_shapes=[pltpu.VMEM((HEADS, DK, DK), F32)], operands=(p, p, p, p, cq, a_row, dtb_row, nw_row))


def sg_fwd(p, lng, lnb, w_s, bs_t, name):
    s = p.shape[0]
    t = _tile(s, 512)

    def body(uv_ref, lng_ref, lnb_ref, ws_ref, bs_ref, o_ref):
        o_ref[...] = _sg_glue(uv_ref[...], lng_ref[...], lnb_ref[...], ws_ref[...], bs_ref[...]).astype(BF16)

    return pl.pallas_call(
        body, grid=(s // t,), name=name,
        in_specs=[_row(t, 2 * SG_W, C_UV // (2 * SG_W)), _full((1, SG_W)), _full((1, SG_W)),
                  _full((4, LANES, LANES)), _full((LANES, LANES))],
        out_specs=_row(t, SG_W), out_shape=_sds((s, SG_W), BF16), compiler_params=_params())(p, lng, lnb, w_s, bs_t)


def merge_fwd(oa, ob, p, x, wa, wb, wo, g1, b1, name):
    s = x.shape[0]
    t = _tile(s, 512)

    def body(oa_ref, ob_ref, ga_ref, gb_ref, x_ref, wa_ref, wb_ref, wo_ref, g_ref, b_ref, x1_ref):
        ya = _mm(oa_ref[...], wa_ref[...])
        yb = _mm(ob_ref[...], wb_ref[...])
        h = _merge_glue(ga_ref[...], gb_ref[...], ya, yb)
        x1_ref[...] = _res_ln_glue(x_ref[...], _mm(h, wo_ref[...]), g_ref[...], b_ref[...])

    return pl.pallas_call(
        body, grid=(s // t,), name=name,
        in_specs=[_row(t, Z_W), _row(t, SG_W), _row(t, D, C_GA // D), _row(t, D, C_GB // D), _row(t, D),
                  _full((Z_W, D)), _full((SG_W, D)), _full((D, D)), _full((1, D)), _full((1, D))],
        out_specs=_row(t, D), out_shape=_sds((s, D)), compiler_params=_params())(oa, ob, p, p, x, wa, wb, wo, g1, b1)


def _load_ffn_weights(wup_hbm, wdn_hbm, wup_v, wdn_v, up_slots, dn_slots):
    for n, k in enumerate(up_slots):
        pltpu.sync_copy(wup_hbm.at[k], wup_v.at[n])
    for n, k in enumerate(dn_slots):
        pltpu.sync_copy(wdn_hbm.at[k], wdn_v.at[pl.ds(n * DN_SHARD, DN_SHARD)])


def ffn_fwd(x1, wup4, cf4, wdn4, g2, b2, name, carry=None):
    s = x1.shape[0]
    t = _tile(s, 512)
    nt = s // t

    def body(x1_ref, halo_ref, wup_hbm, cf_ref, wdn_hbm, g_ref, b_ref, pre_ref, x2_ref, wup_v, wdn_v):
        i = pl.program_id(0)

        @pl.when(i == 0)
        def _():
            _load_ffn_weights(wup_hbm, wdn_hbm, wup_v, wdn_v, range(4), range(4))

        x1v = x1_ref[...]
        halo = jnp.where(i == 0, 0.0, halo_ref[...])
        x1cat = jnp.concatenate([halo, x1v], axis=0).astype(BF16)
        f = None
        for h in range(2):
            ua = jnp.dot(x1cat, wup_v[h], preferred_element_type=F32)
            ub = jnp.dot(x1cat, wup_v[2 + h], preferred_element_type=F32)
            cwa = tuple(cf_ref[h, k:k + 1, :] for k in range(3))
            cwb = tuple(cf_ref[2 + h, k:k + 1, :] for k in range(3))
            act = _ffn_glue(ua, ub, cwa, cwb)
            fh = _mm(act, wdn_v[h * FFN_HALF:(h + 1) * FFN_HALF, :])
            f = fh if f is None else f + fh
        pre = ALPHA * x1v + f
        pre_ref[...] = pre
        x2_ref[...] = _ln(pre, g_ref[...], b_ref[...])

    return _host_call(
        body, carry, nt, grid=(nt,), name=name,
        in_specs=[_row(t, D), _halo(t, D), ANY, _full((4, 3, FFN_HALF)), ANY, _full((1, D)), _full((1, D))],
        out_specs=[_row(t, D), _row(t, D)], out_shape=[_sds((s, D)), _sds((s, D))],
        scratch_shapes=[pltpu.VMEM((4, D, FFN_HALF), BF16), pltpu.VMEM((FFN, D), BF16)],
        operands=(x1, x1, wup4, cf4, wdn4, g2, b2))


def loss_ln_bwd(pre, tgt, g, b, name):
    s = pre.shape[0]
    t = _tile(s, 512)

    def body(pre_ref, t_ref, g_ref, b_ref, dpre_ref, dg_ref, db_ref, loss_ref):
        first = pl.program_id(0) == 0
        y, vjp = jax.vjp(_ln, pre_ref[...], g_ref[...], b_ref[...])
        e = y - t_ref[...]
        dpre, dg, db = vjp(e * (1.0 / D))
        dpre_ref[...] = dpre
        _acc(dg_ref, dg, first)
        _acc(db_ref, db, first)
        part = jnp.sum(jnp.sum(e * e, axis=1, keepdims=True), axis=0, keepdims=True) * (0.5 / D)
        _acc(loss_ref, jnp.broadcast_to(part, loss_ref.shape), first)

    return pl.pallas_call(
        body, grid=(s // t,), name=name, in_specs=[_row(t, D), _row(t, D), _full((1, D)), _full((1, D))],
        out_specs=[_row(t, D), _full((1, D)), _full((1, D)), _full((8, LANES))],
        out_shape=[_sds((s, D)), _sds((1, D)), _sds((1, D)), _sds((8, LANES))], compiler_params=_params())(pre, tgt, g, b)


def _acc(ref, val, first):
    @pl.when(first)
    def _():
        ref[...] = val

    @pl.when(jnp.logical_not(first))
    def _():
        ref[...] += val


def _acc_tn(acc_ref, a, b, first, seg):
    n = b.shape[1]
    for lo in range(0, n, seg):
        hi = min(lo + seg, n)
        _acc(acc_ref.at[:, lo:hi], _mm_tn(a, b[:, lo:hi]), first)


def ln_bwd(pre, dy, g, b, name):
    s = pre.shape[0]
    t = _tile(s, 512)

    def body(pre_ref, dy_ref, g_ref, b_ref, dpre_ref, dg_ref, db_ref):
        _, vjp = jax.vjp(_ln, pre_ref[...], g_ref[...], b_ref[...])
        dpre, dg, db = vjp(dy_ref[...])
        dpre_ref[...] = dpre
        first = pl.program_id(0) == 0
        _acc(dg_ref, dg, first)
        _acc(db_ref, db, first)

    return pl.pallas_call(
        body, grid=(s // t,), name=name, in_specs=[_row(t, D), _row(t, D), _full((1, D)), _full((1, D))],
        out_specs=[_row(t, D), _full((1, D)), _full((1, D))],
        out_shape=[_sds((s, D)), _sds((1, D)), _sds((1, D))], compiler_params=_params())(pre, dy, g, b)


def ffn_bwd(x1, df, acc_in, acc_scale, wup4, cf4, wdn4, h, name, carry=None):
    s = x1.shape[0]
    t = _tile(s, 512)
    nt = s // t

    def body(x1_ref, halo_ref, df_ref, acc_ref, wup_hbm, cf_ref, wdn_hbm,
             dx1_ref, dcf_ref, dua_ref, dub_ref, act_ref, wup_v, wdn_v, carry):
        i = pl.program_id(0)
        j = nt - 1 - i
        first = i == 0

        @pl.when(first)
        def _():
            _load_ffn_weights(wup_hbm, wdn_hbm, wup_v, wdn_v, (h, 2 + h), (2 * h, 2 * h + 1))
            carry[...] = jnp.zeros_like(carry)

        halo = jnp.where(j == 0, 0.0, halo_ref[...])
        x1cat = jnp.concatenate([halo, x1_ref[...]], axis=0).astype(BF16)
        ua = jnp.dot(x1cat, wup_v[0], preferred_element_type=F32)
        ub = jnp.dot(x1cat, wup_v[1], preferred_element_type=F32)
        cwa = tuple(cf_ref[h, k:k + 1, :] for k in range(3))
        cwb = tuple(cf_ref[2 + h, k:k + 1, :] for k in range(3))
        act, vjp = jax.vjp(_ffn_glue, ua, ub, cwa, cwb)
        act_ref[...] = act.astype(BF16)
        dact = _mm_nt(df_ref[...], wdn_v[...])
        dua, dub, dcwa, dcwb = vjp(dact)
        dups = []
        for n, (du, out_ref) in enumerate(((dua, dua_ref), (dub, dub_ref))):
            dups.append(jnp.concatenate([du[HALO:t], du[t:] + carry[n]], axis=0).astype(BF16))
            carry[n] = du[:HALO]
            out_ref[...] = dups[n]
        for k in range(3):
            _acc(dcf_ref.at[0, k:k + 1, :], dcwa[k], first)
            _acc(dcf_ref.at[1, k:k + 1, :], dcwb[k], first)
        dx1_ref[...] = acc_scale * acc_ref[...] + _mm_nt(dups[0], wup_v[0]) + _mm_nt(dups[1], wup_v[1])

    hidden = _row_rev(t, FFN_HALF, nt)
    return _host_call(
        body, carry, nt, grid=(nt,), name=name,
        in_specs=[_row_rev(t, D, nt), _halo(t, D, nt), _row_rev(t, D, nt), _row_rev(t, D, nt),
                  ANY, _full((4, 3, FFN_HALF)), ANY],
        out_specs=[_row_rev(t, D, nt), _full((2, 3, FFN_HALF)), hidden, hidden, hidden],
        out_shape=[_sds((s, D)), _sds((2, 3, FFN_HALF))] + [_sds((s, FFN_HALF), BF16)] * 3,
        scratch_shapes=[pltpu.VMEM((2, D, FFN_HALF), BF16), pltpu.VMEM((FFN_HALF, D), BF16),
                        pltpu.VMEM((2, HALO, FFN_HALF), F32)],
        operands=(x1, x1, df, acc_in, wup4, cf4, wdn4))


def wgrad_mm(a, b, slots, slot, into, name):
    s, m = a.shape
    n = b.shape[1]
    tk = _tile(s, 1024)

    def body(a_ref, b_ref, *rest):
        _acc(rest[-1], _mm_tn(a_ref[...], b_ref[...]), pl.program_id(0) == 0)

    return pl.pallas_call(
        body, grid=(s // tk,), name=name,
        in_specs=[pl.BlockSpec((tk, m), lambda k: (k, 0)), pl.BlockSpec((tk, n), lambda k: (k, 0))]
        + ([] if into is None else [ANY]),
        out_specs=pl.BlockSpec((None, m, n), lambda k: (slot, 0, 0)), out_shape=_sds((slots, m, n)),
        input_output_aliases={} if into is None else {2: 0},
        compiler_params=_params())(*((a, b) if into is None else (a, b, into)))


def merge_bwd(oa, ob, p, x, dx1, wa, wb, wo, g1, b1, name, carry=None):
    s = x.shape[0]
    t = _tile(s)

    def body(oa_ref, ob_ref, ga_ref, gb_ref, x_ref, dx1_ref, wa_ref, wb_ref, wo_ref, g_ref, b_ref,
             doa_ref, dob_ref, dga_ref, dgb_ref, dx_ref, dg_ref, db_ref, h_ref, dr_ref, dya_ref, dyb_ref):
        first = pl.program_id(0) == 0
        oa = oa_ref[...]
        ob = ob_ref[...]
        ya = _mm(oa, wa_ref[...])
        yb = _mm(ob, wb_ref[...])
        h, vjp1 = jax.vjp(_merge_glue, ga_ref[...], gb_ref[...], ya, yb)
        hb = h.astype(BF16)
        r = _mm(hb, wo_ref[...])
        _, vjp2 = jax.vjp(_res_ln_glue, x_ref[...], r, g_ref[...], b_ref[...])
        dx, dr, dg, db = vjp2(dx1_ref[...])
        dx_ref[...] = dx
        _acc(dg_ref, dg, first)
        _acc(db_ref, db, first)
        drb = dr.astype(BF16)
        h_ref[...] = hb
        dr_ref[...] = drb
        dh = _mm_nt(drb, wo_ref[...])
        dga, dgb, dya, dyb = vjp1(dh)
        dga_ref[...] = dga.astype(BF16)
        dgb_ref[...] = dgb.astype(BF16)
        dyab = dya.astype(BF16)
        dybb = dyb.astype(BF16)
        dya_ref[...] = dyab
        dyb_ref[...] = dybb
        doa_ref[...] = _mm_nt(dyab, wa_ref[...]).astype(BF16)
        dob_ref[...] = _mm_nt(dybb, wb_ref[...]).astype(BF16)

    wide = _row(t, D)
    return _host_call(
        body, carry, s // t, grid=(s // t,), name=name,
        in_specs=[_row(t, Z_W), _row(t, SG_W), _row(t, D, C_GA // D), _row(t, D, C_GB // D), _row(t, D), _row(t, D),
                  _full((Z_W, D)), _full((SG_W, D)), _full((D, D)), _full((1, D)), _full((1, D))],
        out_specs=[_row(t, Z_W), _row(t, SG_W), wide, wide, wide, _full((1, D)), _full((1, D)), wide, wide, wide, wide],
        out_shape=[_sds((s, Z_W), BF16), _sds((s, SG_W), BF16), _sds((s, D), BF16), _sds((s, D), BF16), _sds((s, D)),
                   _sds((1, D)), _sds((1, D))] + [_sds((s, D), BF16)] * 4,
        scratch_shapes=[], operands=(oa, ob, p, p, x, dx1, wa, wb, wo, g1, b1))


def sg_bwd(p, dob, lng, lnb, w_s, bs_t, name, carry=None):
    s = p.shape[0]
    t = _tile(s, 512)

    def body(uv_ref, dob_ref, lng_ref, lnb_ref, ws_ref, bs_ref, duv_ref, dlng_ref, dlnb_ref, dws_ref, dbs_ref):
        first = pl.program_id(0) == 0
        _, vjp = jax.vjp(_sg_glue, uv_ref[...], lng_ref[...], lnb_ref[...], ws_ref[...], bs_ref[...])
        duv, dlng, dlnb, dws, dbs = vjp(dob_ref[...].astype(F32))
        duv_ref[...] = duv.astype(BF16)
        _acc(dlng_ref, dlng, first)
        _acc(dlnb_ref, dlnb, first)
        _acc(dws_ref, dws, first)
        _acc(dbs_ref, dbs, first)

    return _host_call(
        body, carry, s // t, grid=(s // t,), name=name,
        in_specs=[_row(t, 2 * SG_W, C_UV // (2 * SG_W)), _row(t, SG_W), _full((1, SG_W)), _full((1, SG_W)),
                  _full((4, LANES, LANES)), _full((LANES, LANES))],
        out_specs=[_row(t, 2 * SG_W), _full((1, SG_W)), _full((1, SG_W)), _full((4, LANES, LANES)), _full((LANES, LANES))],
        out_shape=[_sds((s, 2 * SG_W), BF16), _sds((1, SG_W)), _sds((1, SG_W)), _sds((4, LANES, LANES)), _sds((LANES, LANES))],
        scratch_shapes=[], operands=(p, dob, lng, lnb, w_s, bs_t))


def dn_bwd(p, sst, tinv, doa, cq, a_row, dtb_row, nw_row, name, carry=None):
    s = p.shape[0]
    t = _tile(s, DN_TILE)
    nt = s // t

    def body(qkv_ref, halo_ref, z_ref, ba_ref, sst_ref, tinv_ref, doa_ref, cq_ref, a_ref, dtb_ref, nw_ref,
             dqkv_ref, dz_ref, dba_ref, dcq_ref, da_ref, ddtb_ref, dnw_ref, ds_scr, carry):
        i = pl.program_id(0)
        j = nt - 1 - i
        first = i == 0

        @pl.when(first)
        def _():
            ds_scr[...] = jnp.zeros_like(ds_scr)
            carry[...] = jnp.zeros_like(carry)

        halo = jnp.where(j == 0, 0.0, halo_ref[...])
        qkvcat = jnp.concatenate([halo, qkv_ref[...]], axis=0)
        cw = tuple(cq_ref[k:k + 1, :] for k in range(4))
        t_saved = tinv_ref[0]
        _, vjp = jax.vjp(lambda *args: _dn_glue(*args, t_saved=t_saved)[:2],
                         qkvcat, z_ref[...], ba_ref[...], sst_ref[0], cw, a_ref[...], dtb_ref[...], nw_ref[...])
        dqkvcat, dz, dba, ds_in, dcw, da, ddtb, dnw = vjp((doa_ref[...].astype(F32), ds_scr[...]))
        ds_scr[...] = ds_in
        dz_ref[...] = dz.astype(BF16)
        dba_ref[...] = dba.astype(BF16)
        dtile = dqkvcat[HALO:]
        dqkv_ref[...] = dtile.astype(BF16)
        dqkv_ref[t - HALO:t, :] = (dtile[t - HALO:] + carry[...]).astype(BF16)
        carry[...] = dqkvcat[:HALO]
        for k in range(4):
            _acc(dcq_ref.at[k:k + 1, :], dcw[k], first)
        _acc(da_ref, da, first)
        _acc(ddtb_ref, ddtb, first)
        _acc(dnw_ref, dnw, first)

    return _host_call(
        body, carry, nt, grid=(nt,), name=name,
        in_specs=[_row_rev(t, QKV_W, nt), _halo(t, QKV_W, nt), _row_rev(t, Z_W, nt, C_Z // Z_W),
                  _row_rev(t, LANES, nt, C_BA // LANES),
                  pl.BlockSpec((1, HEADS, DK, DK), lambda i: (nt - 1 - i, 0, 0, 0)),
                  pl.BlockSpec((1,) + tinv.shape[1:], lambda i: (nt - 1 - i, 0, 0, 0)), _row_rev(t, Z_W, nt),
                  _full((4, QKV_W)), _full((1, LANES)), _full((1, LANES)), _full((1, LANES))],
        out_specs=[_row_rev(t, QKV_W, nt), _row_rev(t, Z_W, nt), _row_rev(t, LANES, nt),
                   _full((4, QKV_W)), _full((1, LANES)), _full((1, LANES)), _full((1, LANES))],
        out_shape=[_sds((s, QKV_W), BF16), _sds((s, Z_W), BF16), _sds((s, LANES), BF16),
                   _sds((4, QKV_W)), _sds((1, LANES)), _sds((1, LANES)), _sds((1, LANES))],
        scratch_shapes=[pltpu.VMEM((HEADS, DK, DK), F32), pltpu.VMEM((HALO, QKV_W), F32)],
        operands=(p, p, p, p, sst, tinv, doa, cq, a_row, dtb_row, nw_row))


def proj_bwd(dps, dxd, w, name, carry=None):
    s = dxd.shape[0]
    t = _tile(s, 512)
    n = len(dps)

    def body(*refs):
        dp_refs, dxd_ref, w_hbm, dx_ref, w_v = refs[:n], refs[n], refs[n + 1], refs[n + 2], refs[n + 3]

        @pl.when(pl.program_id(0) == 0)
        def _():
            pltpu.sync_copy(w_hbm, w_v)

        dp = jnp.concatenate([r[...] for r in dp_refs], axis=1)
        dx_ref[...] = dxd_ref[...] + _mm_nt(dp, w_v[...])

    return _host_call(
        body, carry, s // t, grid=(s // t,), name=name,
        in_specs=[_row(t, dp.shape[1]) for dp in dps] + [_row(t, D), ANY],
        out_specs=[_row(t, D)], out_shape=[_sds((s, D))],
        scratch_shapes=[pltpu.VMEM((D, IN_COLS_PAD), BF16)], operands=(*dps, dxd, w))


def wgrad(x, dp, col, into, name):
    s, n = dp.shape
    tk = _tile(s, 1024)
    tn = next(c for c in (1024, 768, 512, 256, 128) if n % c == 0 and col % c == 0)
    block = col // tn

    def body(x_ref, dp_ref, *rest):
        o_ref = rest[-1]
        _acc(o_ref, _mm_tn(x_ref[...], dp_ref[...]), pl.program_id(1) == 0)

    operands = (x, dp) if into is None else (x, dp, into)
    return pl.pallas_call(
        body, grid=(n // tn, s // tk), name=name,
        in_specs=[pl.BlockSpec((tk, D), lambda j, k: (k, 0)), pl.BlockSpec((tk, tn), lambda j, k: (k, j))]
        + ([] if into is None else [ANY]),
        out_specs=pl.BlockSpec((D, tn), lambda j, k: (0, block + j)), out_shape=_sds((D, IN_COLS_PAD)),
        input_output_aliases={} if into is None else {2: 0},
        compiler_params=_params(2))(*operands)


def _rows_block(rows, cols):
    cap = max(HALO, (2 * 1024 * 1024) // (cols * 4))
    for cand in range(min(rows, cap) // HALO * HALO, HALO - 1, -HALO):
        if rows % cand == 0:
            return cand
    return rows


def adam_call(w, g, m, v, name):
    rows, cols = w.shape
    tr = _rows_block(rows, cols)
    c1 = 1.0 - ADAM_B1 ** ADAM_STEP
    c2 = 1.0 - ADAM_B2 ** ADAM_STEP

    def body(w_ref, g_ref, m_ref, v_ref, go_ref, d_ref, nm_ref, nv_ref):
        gv = g_ref[...]
        go_ref[...] = gv
        nm = ADAM_B1 * m_ref[...] + (1.0 - ADAM_B1) * gv
        nv = ADAM_B2 * v_ref[...] + (1.0 - ADAM_B2) * (gv * gv)
        d_ref[...] = -ADAM_LR * ((nm / c1) / (jnp.sqrt(nv / c2) + ADAM_EPS) + ADAM_WD * w_ref[...])
        nm_ref[...] = nm
        nv_ref[...] = nv

    spec = pl.BlockSpec((tr, cols), lambda i: (i, 0))
    return pl.pallas_call(
        body, grid=(rows // tr,), name=name, in_specs=[spec] * 4, out_specs=[spec] * 4,
        out_shape=[_sds((rows, cols))] * 4, compiler_params=_params())(w, g, m, v)


def _place():
    return lax.axis_index("x"), lax.axis_index("y"), lax.axis_index("c")


def _other_chips(x, y):
    return [(1 - x, y), (x, 1 - y), (1 - x, 1 - y)]


def _remote(src, dst, send_sem, recv_sem, to):
    return pltpu.make_async_remote_copy(src_ref=src, dst_ref=dst, send_sem=send_sem, recv_sem=recv_sem,
                                        device_id=to, device_id_type=MESH)


class _Gather:
    def __init__(self, ins, outs, send_sems, recv_sems, local_sems):
        self.ins, self.outs, self.n = ins, outs, len(ins)
        self.send_sems, self.recv_sems, self.local_sems = send_sems, recv_sems, local_sems
        self.x, self.y, self.c = _place()
        self.me = 2 * self.x + self.y
        self.chips = _other_chips(self.x, self.y)

    def _copy(self, t, k, slot, part, to, src=None):
        dst = self.outs[t].at[slot, part]
        return _remote(dst if src is None else src, dst, self.send_sems.at[6 * t + k], self.recv_sems.at[6 * t + k], to)

    def _mine(self):
        return [pltpu.make_async_copy(self.ins[t].at[p], self.outs[t].at[self.me, p], self.local_sems.at[2 * t + p])
                for t in range(self.n) for p in range(2)]

    def _first(self):
        return [self._copy(t, k, self.me, self.c, (cx, cy, self.c), src=self.ins[t].at[self.c])
                for k, (cx, cy) in enumerate(self.chips) for t in range(self.n)]

    def start(self):
        for cp in self._mine() + self._first():
            cp.start()

    def finish(self):
        x, y, c = self.x, self.y, self.c
        passed = []
        for k, (cx, cy) in enumerate(self.chips):
            for t in range(self.n):
                self._copy(t, k, 2 * cx + cy, c, (x, y, c)).wait_recv()
                passed.append(self._copy(t, 3 + k, 2 * cx + cy, c, (x, y, 1 - c)))
                passed[-1].start()
        for k, (cx, cy) in enumerate(self.chips):
            for t in range(self.n):
                self._copy(t, 3 + k, 2 * cx + cy, 1 - c, (x, y, c)).wait_recv()
        for cp in self._first() + passed:
            cp.wait_send()
        for cp in self._mine():
            cp.wait()


class _Carried(NamedTuple):
    ins: tuple
    out_shapes: tuple
    scratch: tuple
    make: Callable
    aliases: dict


def _host_call(body, carry, steps, *, grid, name, in_specs, out_specs, out_shape, scratch_shapes, operands):
    in_specs, out_specs, out_shape, scratch_shapes = list(in_specs), list(out_specs), list(out_shape), list(scratch_shapes)
    aliases = {}
    if carry is not None:
        n_in, n_out, n_scr = len(in_specs), len(out_specs), len(scratch_shapes)
        n_ci, n_co = len(carry.ins), len(carry.out_shapes)
        plain = body

        def body(*refs):
            ins, cins = refs[:n_in], refs[n_in:n_in + n_ci]
            outs = refs[n_in + n_ci:n_in + n_ci + n_out]
            couts = refs[n_in + n_ci + n_out:n_in + n_ci + n_out + n_co]
            rest = refs[n_in + n_ci + n_out + n_co:]
            exchange = carry.make(cins, couts, *rest[n_scr:])
            pl.when(pl.program_id(0) == 0)(exchange.start)
            plain(*ins, *outs, *rest[:n_scr])
            pl.when(pl.program_id(0) == steps - 1)(exchange.finish)

        aliases = {n_in + i: n_out + j for i, j in carry.aliases.items()}
        in_specs += [ANY] * n_ci
        out_specs += [ANY] * n_co
        out_shape += list(carry.out_shapes)
        scratch_shapes += list(carry.scratch)
        operands = tuple(operands) + tuple(carry.ins)
    return pl.pallas_call(
        body, grid=grid, name=name, in_specs=in_specs, out_specs=out_specs, out_shape=out_shape,
        scratch_shapes=scratch_shapes, input_output_aliases=aliases, compiler_params=_params(len(grid)))(*operands)


def exchange(carry, name):
    n_i, n_o = len(carry.ins), len(carry.out_shapes)

    def body(*refs):
        ex = carry.make(refs[:n_i], refs[n_i:n_i + n_o], *refs[n_i + n_o:])
        ex.start()
        ex.finish()

    return pl.pallas_call(
        body, name=name, in_specs=[ANY] * n_i, out_specs=[ANY] * n_o, out_shape=list(carry.out_shapes),
        scratch_shapes=list(carry.scratch), input_output_aliases=dict(carry.aliases),
        compiler_params=pltpu.CompilerParams(vmem_limit_bytes=VMEM_LIMIT))(*carry.ins)


def _dma_sems(*counts):
    return tuple(pltpu.SemaphoreType.DMA((n,)) for n in counts)


def carried_gather(shards):
    n = len(shards)
    return _Carried(tuple(shards), tuple(_sds((N_CHIPS,) + a.shape, a.dtype) for a in shards),
                    _dma_sems(6 * n, 6 * n, 2 * n), _Gather, {})


class _PairSwap:
    def __init__(self, ins, outs, send_sems, recv_sems):
        x, y, c = _place()
        self.copies = [_remote(ins[t].at[:, 1 - c], outs[t], send_sems.at[t], recv_sems.at[t], (x, y, 1 - c))
                       for t in range(len(ins))]

    def start(self):
        for cp in self.copies:
            cp.start()

    def finish(self):
        for cp in self.copies:
            cp.wait()


def carried_pair_swap(views):
    n = len(views)
    return _Carried(tuple(views), tuple(_sds((v.shape[0],) + v.shape[2:]) for v in views), _dma_sems(n, n), _PairSwap, {})


class _Scatter:
    def __init__(self, srcs, outs, send_sems, recv_sems, local_sems, pieces):
        self.srcs, self.outs, self.pieces, self.n = srcs, outs, pieces, len(pieces)
        self.send_sems, self.recv_sems, self.local_sems = send_sems, recv_sems, local_sems
        self.x, self.y, self.c = _place()
        self.me = 2 * self.x + self.y

    def _piece(self, t, k):
        idx, lead, cols = self.pieces[t][k]
        ref = self.srcs[idx].at[lead]
        return ref if cols is None else ref.at[:, pl.ds(cols[0], cols[1])]

    def _local(self, t, k):
        return pltpu.make_async_copy(self._piece(t, k), self.outs[t].at[k], self.local_sems.at[t])

    def _each_chip(self, mine, others):
        for k in range(N_CHIPS):
            pl.when(self.me == k)(functools.partial(mine, k))
            pl.when(self.me != k)(functools.partial(others, k))

    def start(self):
        def mine(k):
            for t in range(self.n):
                self._local(t, k).start()

        def others(k):
            for t in range(self.n):
                _remote(self._piece(t, k), self.outs[t].at[self.me], self.send_sems.at[N_CHIPS * t + k],
                        self.recv_sems.at[N_CHIPS * t + self.me], (k // 2, k % 2, self.c)).start()

        self._each_chip(mine, others)

    def finish(self):
        def mine(k):
            for t in range(self.n):
                self._local(t, k).wait()

        def others(k):
            for t in range(self.n):
                cp = _remote(self._piece(t, k), self.outs[t].at[k], self.send_sems.at[N_CHIPS * t + k],
                             self.recv_sems.at[N_CHIPS * t + k], (self.x, self.y, self.c))
                cp.wait_recv()
                cp.wait_send()

        self._each_chip(mine, others)


def carried_scatter(srcs, pieces, part_shapes):
    n = len(pieces)
    return _Carried(tuple(srcs), tuple(_sds((N_CHIPS,) + tuple(shp), srcs[0].dtype) for shp in part_shapes),
                    _dma_sems(N_CHIPS * n, N_CHIPS * n, n), functools.partial(_Scatter, pieces=pieces), {})


class _PairJoin:
    def __init__(self, ins, outs, send_sems, recv_sems, layer):
        self.ins, self.outs, self.layer, self.n = ins, outs, layer, len(ins)
        self.send_sems, self.recv_sems = send_sems, recv_sems
        self.x, self.y, self.c = _place()

    def _copy(self, t, half, to):
        return _remote(self.ins[t].at[self.layer, self.c], self.outs[t].at[self.layer, half],
                       self.send_sems.at[t], self.recv_sems.at[t], to)

    def start(self):
        for t in range(self.n):
            self._copy(t, self.c, (self.x, self.y, 1 - self.c)).start()

    def finish(self):
        for t in range(self.n):
            self._copy(t, self.c, (self.x, self.y, 1 - self.c)).wait_send()
            self._copy(t, 1 - self.c, (self.x, self.y, self.c)).wait_recv()


def carried_join(bufs, layer):
    n = len(bufs)
    return _Carried(tuple(bufs), tuple(_sds(b.shape) for b in bufs), _dma_sems(n, n),
                    functools.partial(_PairJoin, layer=layer), {t: t for t in range(n)})


def pair_add_half(mine, theirs, c_vec, name):
    g, _, h, b = mine.shape
    tr = _rows_block(h, b)

    def body(c_ref, a_ref, b_ref, o_ref):
        o_ref[...] = (a_ref[...] + b_ref[...]).astype(BF16)

    part = pl.BlockSpec((None, tr, b), lambda j, i, c: (j, i, 0))
    grid_spec = pltpu.PrefetchScalarGridSpec(
        num_scalar_prefetch=1, grid=(g, h // tr),
        in_specs=[pl.BlockSpec((None, None, tr, b), lambda j, i, c: (j, c[0], i, 0)), part], out_specs=part)
    return pl.pallas_call(body, grid_spec=grid_spec, name=name, out_shape=_sds((g, h, b), BF16),
                          compiler_params=_params(2))(c_vec, mine, theirs)


def chips_add_into(recv, into, layer, c_vec, name):
    n, h, b = recv.shape
    tr = _rows_block(h, b)

    def body(c_ref, r0, r1, r2, r3, *rest):
        rest[-1][...] = ((r0[...].astype(F32) + r1[...].astype(F32)) + r2[...].astype(F32)) + r3[...].astype(F32)

    grid_spec = pltpu.PrefetchScalarGridSpec(
        num_scalar_prefetch=1, grid=(h // tr,),
        in_specs=[pl.BlockSpec((None, tr, b), lambda i, c, k=k: (k, i, 0)) for k in range(n)]
        + ([] if into is None else [ANY]),
        out_specs=pl.BlockSpec((None, None, tr, b), lambda i, c: (layer, c[0], i, 0)))
    return pl.pallas_call(
        body, grid_spec=grid_spec, name=name, out_shape=_sds((2, 2, h, b)),
        input_output_aliases={} if into is None else {1 + n: 0},
        compiler_params=_params())(c_vec, *([recv] * n), *(() if into is None else (into,)))


N_DEV = 8


class _AllSum:
    def __init__(self, ins, outs, buf, total, send_sems, recv_sems, local_sem):
        self.v, self.out, self.buf, self.total = ins[0], outs[0], buf, total
        self.send_sems, self.recv_sems, self.local_sem = send_sems, recv_sems, local_sem
        self.x, self.y, self.c = _place()
        self.me, self.sibling = (self.x, self.y, self.c), (self.x, self.y, 1 - self.c)
        self.chips = _other_chips(self.x, self.y)

    def _slot(self, px, py, pc):
        return self.buf.at[4 * px + 2 * py + pc]

    def _copy(self, k, block, to):
        return _remote(self._slot(*block), self._slot(*block), self.send_sems.at[k], self.recv_sems.at[k], to)

    def _first(self):
        return [self._copy(0, self.me, self.sibling)] + [
            self._copy(1 + k, self.me, (*chip, self.c)) for k, chip in enumerate(self.chips)]

    def start(self):
        load = pltpu.make_async_copy(self.v, self._slot(*self.me), self.local_sem)
        load.start()
        load.wait()
        for cp in self._first():
            cp.start()

    def finish(self):
        c = self.c
        passed = [self._copy(4 + k, (*chip, c), self.sibling) for k, chip in enumerate(self.chips)]
        for k, chip in enumerate(self.chips):
            self._copy(1 + k, (*chip, c), self.me).wait_recv()
            passed[k].start()
        self._copy(0, self.sibling, self.me).wait_recv()
        for k, chip in enumerate(self.chips):
            self._copy(4 + k, (*chip, 1 - c), self.me).wait_recv()
        for cp in self._first() + passed:
            cp.wait_send()
        acc = self.buf[0]
        for d in range(1, N_DEV):
            acc = acc + self.buf[d]
        self.total[...] = acc
        store = pltpu.make_async_copy(self.total, self.out, self.local_sem)
        store.start()
        store.wait()


def carried_allsum(v):
    rows, lanes = v.shape
    scratch = (pltpu.VMEM((N_DEV, rows, lanes), F32), pltpu.VMEM((rows, lanes), F32)) + _dma_sems(7, 7) + (
        pltpu.SemaphoreType.DMA,)
    return _Carried((v,), (_sds((rows, lanes)),), scratch, _AllSum, {})


class _Both:
    def __init__(self, *exchanges):
        self.exchanges = exchanges

    def start(self):
        for ex in self.exchanges:
            ex.start()

    def finish(self):
        for ex in self.exchanges:
            ex.finish()


def carried_both(a, b):
    ai, ao, asc = len(a.ins), len(a.out_shapes), len(a.scratch)

    def make(ins, outs, *scratch):
        return _Both(a.make(ins[:ai], outs[:ao], *scratch[:asc]), b.make(ins[ai:], outs[ao:], *scratch[asc:]))

    aliases = {**a.aliases, **{ai + i: ao + j for i, j in b.aliases.items()}}
    return _Carried(a.ins + b.ins, a.out_shapes + b.out_shapes, a.scratch + b.scratch, make, aliases)


BIG = ("w_in", "w_branch_a", "w_branch_b", "w_out", "w_up", "w_down")
CONV = ("conv_qkv", "conv_ffn")
REPL =("a_log", "dt_bias", "dn_norm_w", "sg_ln_g", "sg_ln_b", "w_spatial", "b_spatial", "ln1_g", "ln1_b", "ln2_g", "ln2_b")


def _pad_rows(flat, mult):
    n = flat.shape[0]
    unit = mult * LANES
    total = -(-n // unit) * unit
    return jnp.pad(flat, (0, total - n)).reshape(total // LANES, LANES)


def _pack(arrs, mult):
    return _pad_rows(jnp.concatenate([a.reshape(-1) for a in arrs]), mult)


def _unpack(flat, shapes):
    out, off = [], 0
    for shp in shapes:
        n = math.prod(shp)
        out.append(flat[off:off + n].reshape(shp))
        off += n
    return out


def kernel(x, w_in, conv_qkv, a_log, dt_bias, dn_norm_w, w_branch_a, sg_ln_g, sg_ln_b, w_spatial, b_spatial, w_branch_b, w_out, ln1_g, ln1_b, w_up, conv_ffn, w_down, ln2_g, ln2_b, loss_target, m_w_in, m_conv_qkv, m_a_log, m_dt_bias, m_dn_norm_w, m_w_branch_a, m_sg_ln_g, m_sg_ln_b, m_w_spatial, m_b_spatial, m_w_branch_b, m_w_out, m_ln1_g, m_ln1_b, m_w_up, m_conv_ffn, m_w_down, m_ln2_g, m_ln2_b, v_w_in, v_conv_qkv, v_a_log, v_dt_bias, v_dn_norm_w, v_w_branch_a, v_sg_ln_g, v_sg_ln_b, v_w_spatial, v_b_spatial, v_w_branch_b, v_w_out, v_ln1_g, v_ln1_b, v_w_up, v_conv_ffn, v_w_down, v_ln2_g, v_ln2_b):
    names = ("w_in", "conv_qkv", "a_log", "dt_bias", "dn_norm_w", "w_branch_a", "sg_ln_g", "sg_ln_b", "w_spatial",
             "b_spatial", "w_branch_b", "w_out", "ln1_g", "ln1_b", "w_up", "conv_ffn", "w_down", "ln2_g", "ln2_b")
    w = dict(zip(names, (w_in, conv_qkv, a_log, dt_bias, dn_norm_w, w_branch_a, sg_ln_g, sg_ln_b, w_spatial,
                         b_spatial, w_branch_b, w_out, ln1_g, ln1_b, w_up, conv_ffn, w_down, ln2_g, ln2_b)))
    m = dict(zip(names, (m_w_in, m_conv_qkv, m_a_log, m_dt_bias, m_dn_norm_w, m_w_branch_a, m_sg_ln_g, m_sg_ln_b,
                         m_w_spatial, m_b_spatial, m_w_branch_b, m_w_out, m_ln1_g, m_ln1_b, m_w_up, m_conv_ffn,
                         m_w_down, m_ln2_g, m_ln2_b)))
    v = dict(zip(names, (v_w_in, v_conv_qkv, v_a_log, v_dt_bias, v_dn_norm_w, v_w_branch_a, v_sg_ln_g, v_sg_ln_b,
                         v_w_spatial, v_b_spatial, v_w_branch_b, v_w_out, v_ln1_g, v_ln1_b, v_w_up, v_conv_ffn,
                         v_w_down, v_ln2_g, v_ln2_b)))
    chip = 2 * lax.axis_index("x") + lax.axis_index("y")
    s = x.shape[1]
    xs = x.reshape(s, D)
    tgt = loss_target.reshape(s, D)

    big_names, conv_names = list(BIG), list(CONV)

    def in_two(name, l):
        rows, cols = w[name].shape[1:]
        return w[name][l].astype(BF16).reshape(2, rows // 2, cols)

    def whole(name, landed):
        rows, cols = w[name].shape[1:]
        return landed.reshape(N_CHIPS, rows, cols)

    first = exchange(carried_gather([in_two("w_in", 0)] + [w[n] for n in conv_names]), "gather_first")
    got = [{"w_in": whole("w_in", first[0])}, {}]
    conv_taps = dict(zip(conv_names, first[1:]))
    narrow, wide = ["w_branch_a", "w_branch_b", "w_out"], ["w_up", "w_down"]
    carried = {"proj_fwd0": (0, narrow), "dn_fwd0": (0, wide), "ffn_fwd0": (1, ["w_in"] + narrow), "dn_fwd1": (1, wide)}

    def carry(call):
        if call not in carried:
            return None
        l, which = carried[call]
        return carried_gather([in_two(n, l) for n in which])

    def land(call, landed):
        l, which = carried.get(call, (0, []))
        got[l].update({n: whole(n, a) for n, a in zip(which, landed)})

    def lane_row(vec, off):
        return jnp.zeros((1, LANES), F32).at[0, off:off + vec.shape[0]].set(vec)

    def side_by_side(blocks):
        return jnp.concatenate([blocks[k] for k in range(N_CHIPS)], axis=1)

    def small_params(l):
        return dict(
            cq=side_by_side(conv_taps["conv_qkv"][:, l]),
            a_row=lane_row(w["a_log"][l], HEADS), dtb_row=lane_row(w["dt_bias"][l], HEADS),
            nw_row=w["dn_norm_w"][l].reshape(1, DK),
            lng=w["sg_ln_g"][l].reshape(1, SG_W), lnb=w["sg_ln_b"][l].reshape(1, SG_W),
            w_s=w["w_spatial"][l], bs_t=jnp.zeros((LANES, LANES), F32).at[:, :4].set(w["b_spatial"][l].T),
            g1=w["ln1_g"][l].reshape(1, D), b1=w["ln1_b"][l].reshape(1, D),
            cf=conv_taps["conv_ffn"][:, l],
            g2=w["ln2_g"][l].reshape(1, D), b2=w["ln2_b"][l].reshape(1, D))

    layers, saved = [], []
    h_in = xs
    for l in range(DEPTH):
        p = small_params(l)
        wi = side_by_side(got[l]["w_in"])
        p["w_in"] = jnp.concatenate([wi[:, :2048], wi[:, 2056:3080], wi[:, 3080:5128], wi[:, 2048:2056],
                                     jnp.zeros((D, IN_COLS_PAD - 5128), BF16)], axis=1)
        proj, *landed = proj_fwd(h_in, p["w_in"], f"proj_fwd{l}", carry=carry(f"proj_fwd{l}"))
        land(f"proj_fwd{l}", landed)
        oa, sst, tinv, *landed = dn_fwd(proj, p["cq"], p["a_row"], p["dtb_row"], p["nw_row"], f"dn_fwd{l}",
                                  carry=carry(f"dn_fwd{l}"))
        land(f"dn_fwd{l}", landed)
        ob = sg_fwd(proj, p["lng"], p["lnb"], p["w_s"], p["bs_t"], f"sg_fwd{l}")
        p.update(wa=side_by_side(got[l]["w_branch_a"]), wb=side_by_side(got[l]["w_branch_b"]),
                 wo=got[l]["w_out"].reshape(D, D))
        x1 = merge_fwd(oa, ob, proj, h_in, p["wa"], p["wb"], p["wo"], p["g1"], p["b1"], f"merge_fwd{l}")
        pre2, x2, *landed = ffn_fwd(x1, got[l]["w_up"], p["cf"], got[l]["w_down"], p["g2"], p["b2"], f"ffn_fwd{l}",
                                    carry=carry(f"ffn_fwd{l}"))
        land(f"ffn_fwd{l}", landed)
        layers.append(p)
        saved.append(dict(x=h_in, proj=proj, oa=oa, ob=ob, sst=sst, tinv=tinv, x1=x1, pre2=pre2))
        h_in = x2


    small_names = conv_names + list(REPL)
    grads = {n: [None] * DEPTH for n in small_names}
    c_vec = jnp.stack([lax.axis_index("c")]).astype(jnp.int32)
    tags = ("w_in", "w_a", "w_b", "w_out", "w_up0", "w_up1", "w_dn0", "w_dn1")
    groups = (1, 1, 1, N_CHIPS, 2, 2, 2, 2)
    ab_cols = w["w_branch_a"].shape[2]
    pieces = [
        [(0, (k,), None) for k in range(N_CHIPS)],
        [(1, (0,), (k * ab_cols, ab_cols)) for k in range(N_CHIPS)],
        [(2, (0,), (k * ab_cols, ab_cols)) for k in range(N_CHIPS)],
        [(3, (k,), None) for k in range(N_CHIPS)],
        [(4 + k % 2, (k // 2,), None) for k in range(N_CHIPS)],
        [(6 + k // 2, (k % 2,), None) for k in range(N_CHIPS)],
    ]
    part_shapes = [(w[n].shape[1] // 2, w[n].shape[2]) for n in big_names]
    arrays_of = ((0,), (1,), (2,), (3,), (4, 5), (6, 7))
    rest, ffn_part = (0, 1, 2, 3), (4, 5)
    bufs = {}

    def arrays(which):
        return [i for t in which for i in arrays_of[t]]

    def views(which, arrs):
        return [a.reshape(groups[i], 2, a.size // a.shape[-1] // (2 * groups[i]), a.shape[-1])
                for i, a in zip(arrays(which), arrs)]

    def pair_sums(l, which, mine, theirs):
        ids = arrays(which)
        sums = [pair_add_half(m_, t_, c_vec, f"reduce_pair_add{l}_{tags[i]}") for i, m_, t_ in zip(ids, mine, theirs)]
        if ids[0] == 0:
            pin = sums[0][0]
            natural = jnp.concatenate([pin[:, :2048], pin[:, C_BA:C_BA + 8], pin[:, 2048:C_BA]], axis=1)
            sums[0] = jnp.stack(jnp.split(natural, N_CHIPS, axis=1))
        return sums

    def scatter_of(which, srcs):
        place = {i: j for j, i in enumerate(arrays(which))}
        return carried_scatter(srcs, [[(place[i], lead, cols) for i, lead, cols in pieces[t]] for t in which],
                               [part_shapes[t] for t in which])

    def chip_sums(l, which, recv):
        for t, r in zip(which, recv):
            n = big_names[t]
            bufs[n] = chips_add_into(r, bufs.get(n), l, c_vec, f"reduce_chips_add{l}_{n}")
        return [bufs[big_names[t]] for t in which]

    def keep(which, joined):
        bufs.update({big_names[t]: b for t, b in zip(which, joined)})

    above = None
    for l in reversed(range(DEPTH)):
        p, a = layers[l], saved[l]
        if l == DEPTH - 1:
            dpre2, dg2, db2, loss_part = loss_ln_bwd(a["pre2"], tgt, p["g2"], p["b2"], "loss_ln2_bwd")
            loss = lax.psum(loss_part[0, 0], ("x", "y", "c"))
        else:
            dpre2, dg2, db2 = ln_bwd(a["pre2"], dy, p["g2"], p["b2"], f"ln2_bwd{l}")
        dx1, dcf0, *half0 = ffn_bwd(
            a["x1"], dpre2, dpre2, ALPHA, got[l]["w_up"], p["cf"], got[l]["w_down"], 0, f"ffn_bwd{l}a",
            carry=carried_pair_swap(above) if above else None)
        theirs = half0[3:]
        srcs = pair_sums(l + 1, rest, above, theirs) if above else None
        dx1, dcf1, *half1 = ffn_bwd(
            a["x1"], dpre2, dx1, 1.0, got[l]["w_up"], p["cf"], got[l]["w_down"], 1, f"ffn_bwd{l}b",
            carry=scatter_of(rest, srcs) if above else None)
        recv = half1[3:]
        summed = chip_sums(l + 1, rest, recv) if above else None
        ffn_grads = []
        for h, (dua, dub, act) in enumerate((half0[:3], half1[:3])):
            dwup = wgrad_mm(a["x1"], dua, 2, 0, None, f"wgrad_up{l}{'ab'[h]}_a")
            ffn_grads.append(wgrad_mm(a["x1"], dub, 2, 1, dwup, f"wgrad_up{l}{'ab'[h]}_b"))
        for h, (dua, dub, act) in enumerate((half0[:3], half1[:3])):
            ffn_grads.append(wgrad_mm(act, dpre2, 1, 0, None, f"wgrad_down{l}{'ab'[h]}")[0])
        dwup0, dwup1, dwdn0, dwdn1 = ffn_grads
        doa, dob, dga, dgb, dxd, dg1, db1, h_mid, d_r, d_ya, d_yb, *joined = merge_bwd(
            a["oa"], a["ob"], a["proj"], a["x"], dx1, p["wa"], p["wb"], p["wo"], p["g1"], p["b1"], f"merge_bwd{l}",
            carry=carried_join(summed, l + 1) if above else None)
        keep(rest, joined)
        dwa = wgrad_mm(a["oa"], d_ya, 1, 0, None, f"wgrad_a{l}")[0]
        dwb = wgrad_mm(a["ob"], d_yb, 1, 0, None, f"wgrad_b{l}")[0]
        dwo = wgrad_mm(h_mid, d_r, 1, 0, None, f"wgrad_out{l}")[0]
        mine = views(ffn_part, [dwup0, dwup1, dwdn0, dwdn1])
        duv, dlng, dlnb, dws, dbs, *theirs = sg_bwd(a["proj"], dob, p["lng"], p["lnb"], p["w_s"], p["bs_t"], f"sg_bwd{l}",
                                                    carry=carried_pair_swap(mine))
        srcs = pair_sums(l, ffn_part, mine, theirs)
        dqkv, dz, dba, dcq, da, ddtb, dnw, *recv = dn_bwd(
            a["proj"], a["sst"], a["tinv"], doa, p["cq"], p["a_row"], p["dtb_row"], p["nw_row"], f"dn_bwd{l}",
            carry=scatter_of(ffn_part, srcs))
        summed = chip_sums(l, ffn_part, recv)
        dy, *joined = proj_bwd([dqkv, dz, duv, dga, dgb, dba], dxd, p["w_in"], f"proj_bwd{l}",
                               carry=carried_join(summed, l))
        keep(ffn_part, joined)
        dwi = None
        for tag, dp, col in (("qkv", dqkv, 0), ("z", dz, C_Z), ("uv", duv, C_UV), ("ga", dga, C_GA), ("gb", dgb, C_GB),
                             ("ba", dba, C_BA)):
            dwi = wgrad(a["x"], dp, col, dwi, f"wgrad_in{l}_{tag}")

        above = views(rest, [dwi, dwa, dwb, dwo])
        grads["conv_qkv"][l] = dcq
        grads["conv_ffn"][l] = jnp.concatenate([dcf0[0], dcf1[0], dcf0[1], dcf1[1]], axis=1)
        grads["a_log"][l] = da[0, HEADS:2 * HEADS]
        grads["dt_bias"][l] = ddtb[0, HEADS:2 * HEADS]
        grads["dn_norm_w"][l] = dnw[0]
        grads["sg_ln_g"][l] = dlng[0]
        grads["sg_ln_b"][l] = dlnb[0]
        grads["w_spatial"][l] = dws
        grads["b_spatial"][l] = dbs[:, :4].T
        grads["ln1_g"][l] = dg1[0]
        grads["ln1_b"][l] = db1[0]
        grads["ln2_g"][l] = dg2[0]
        grads["ln2_b"][l] = db2[0]
    grad_x = dy.reshape(x.shape)
    g_full = {n: jnp.stack(grads[n]) for n in small_names}

    theirs = exchange(carried_pair_swap(above), "reduce_pair")
    *recv, small = exchange(carried_both(scatter_of(rest, pair_sums(0, rest, above, theirs)),
                                         carried_allsum(_pack([g_full[n] for n in small_names], 8))), "reduce_chips")
    keep(rest, exchange(carried_join(chip_sums(0, rest, recv), 0), "reduce_join"))
    g_shard = {n: bufs[n].reshape(w[n].shape) for n in big_names}

    small_full = dict(zip(small_names, _unpack(small.reshape(-1), [g_full[n].shape for n in small_names])))
    for n in conv_names:
        width = w[n].shape[2]
        g_shard[n] = lax.dynamic_slice_in_dim(small_full[n], chip * width, width, axis=2)
    for n in REPL:
        g_shard[n] = small_full[n]

    delta, new_m, new_v = {}, {}, {}
    for n in big_names:
        shp = w[n].shape
        two_d = (shp[0] * shp[1], shp[2])
        g_, d_, m_, v_ = adam_call(w[n].reshape(two_d), g_shard[n].reshape(two_d), m[n].reshape(two_d), v[n].reshape(two_d), f"adam_{n}")
        g_shard[n], delta[n], new_m[n], new_v[n] = g_.reshape(shp), d_.reshape(shp), m_.reshape(shp), v_.reshape(shp)
    shapes = [w[n].shape for n in small_names]
    packs = [_pack([src[n] for n in small_names], 8) for src in (w, g_shard, m, v)]
    outs = adam_call(*packs, "adam_small")
    for dst, o in zip((delta, new_m, new_v), outs[1:]):
        dst.update(zip(small_names, _unpack(o.reshape(-1), shapes)))

    return (loss, grad_x, *[g_shard[n] for n in names], *[delta[n] for n in names],
            *[new_m[n] for n in names], *[new_v[n] for n in names])
```

```python
import functools
import math
from typing import Callable, NamedTuple

import jax
import jax.numpy as jnp
from jax import lax
from jax.experimental import pallas as pl
from jax.experimental.pallas import tpu as pltpu

F32 = jnp.float32
BF16 = jnp.bfloat16
HI = lax.Precision.HIGHEST
MID = lax.Precision.HIGH
MESH = pl.DeviceIdType.MESH

D = 1024
DEPTH = 2
HEADS = 4
DK = 128
CHUNK = 64
QKV_W = 1536
Z_W = 512
SG_W = 512
FFN = 2816
FFN_HALF = FFN // 2
N_CHIPS = 4
DN_SHARD = FFN // N_CHIPS
LN_EPS = 1e-5
RMS_EPS = 1e-6
L2_EPS = 1e-6
ALPHA = (2 * DEPTH) ** 0.25
ADAM_LR, ADAM_B1, ADAM_B2, ADAM_EPS, ADAM_WD, ADAM_STEP = 0.001, 0.9, 0.999, 1e-08, 0.01, 10

HALO = 16
LANES = 128
IN_COLS_PAD = 5248
C_Z, C_UV, C_GA, C_GB, C_BA = 1536, 2048, 3072, 4096, 5120
VMEM_LIMIT = 56 * 1024 * 1024
WGRAD_TOKENS = 2048
DN_TILE = 256


def _params(n_grid=1):
    return pltpu.CompilerParams(dimension_semantics=("arbitrary",) * n_grid, vmem_limit_bytes=VMEM_LIMIT)


def _mm(a, b):
    return jnp.dot(a.astype(BF16), b.astype(BF16), preferred_element_type=F32)


def _mm_nt(a, b):
    return lax.dot_general(a.astype(BF16), b.astype(BF16), (((1,), (1,)), ((), ())), preferred_element_type=F32)


def _mm_tn(a, b):
    return lax.dot_general(a.astype(BF16), b.astype(BF16), (((0,), (0,)), ((), ())), preferred_element_type=F32)


def _bdot(a, b, prec=MID):
    return lax.dot_general(a, b, (((2,), (1,)), ((0,), (0,))), precision=prec, preferred_element_type=F32)


def _bdot_nt(a, b, prec=MID):
    return lax.dot_general(a, b, (((2,), (2,)), ((0,), (0,))), precision=prec, preferred_element_type=F32)


def _bf16_dot(a, b, contract):
    return lax.dot_general(a.astype(BF16), b.astype(BF16), (contract, ((0,), (0,))), preferred_element_type=F32)


@jax.custom_vjp
def _fdot(a, b):
    return _bf16_dot(a, b, ((2,), (1,)))


def _fdot_fwd(a, b):
    return _fdot(a, b), (a, b)


def _fdot_bwd(res, ct):
    a, b = res
    return _bf16_dot(ct, b, ((2,), (2,))), _bf16_dot(a, ct, ((1,), (1,)))


_fdot.defvjp(_fdot_fwd, _fdot_bwd)


@jax.custom_vjp
def _fdot_nt(a, b):
    return _bf16_dot(a, b, ((2,), (2,)))


def _fdot_nt_fwd(a, b):
    return _fdot_nt(a, b), (a, b)


def _fdot_nt_bwd(res, ct):
    a, b = res
    return _bf16_dot(ct, b, ((2,), (1,))), _bf16_dot(ct, a, ((1,), (1,)))


_fdot_nt.defvjp(_fdot_nt_fwd, _fdot_nt_bwd)


@jax.custom_vjp
def _fdot_tn(a, b):
    return _bf16_dot(a, b, ((1,), (1,)))


def _fdot_tn_fwd(a, b):
    return _fdot_tn(a, b), (a, b)


def _fdot_tn_bwd(res, ct):
    a, b = res
    return _bf16_dot(b, ct, ((2,), (2,))), _bf16_dot(a, ct, ((2,), (1,)))


_fdot_tn.defvjp(_fdot_tn_fwd, _fdot_tn_bwd)


def _stack(parts):
    return jnp.concatenate([p[None] for p in parts], axis=0)


def _ln(x, g, b):
    mu = jnp.mean(x, axis=-1, keepdims=True)
    xc = x - mu
    var = jnp.mean(xc * xc, axis=-1, keepdims=True)
    return xc * lax.rsqrt(var + LN_EPS) * g + b


def _shift_rows(x, s):
    s = s % x.shape[0]
    return x if s == 0 else pltpu.roll(x, s, 0)


@jax.custom_vjp
def _conv(xcat, w):
    k_taps = len(w)
    y = None
    for k in range(k_taps):
        t = _shift_rows(xcat, k_taps - 1 - k)[HALO:] * w[k]
        y = t if y is None else y + t
    return y


def _conv_fwd(xcat, w):
    return _conv(xcat, w), (xcat, w)


def _conv_bwd(res, dy):
    xcat, w = res
    k_taps = len(w)
    dyp = jnp.concatenate([jnp.zeros((HALO, dy.shape[1]), dy.dtype), dy], axis=0)
    dx = None
    dws = []
    for k in range(k_taps):
        shifted = _shift_rows(dyp, -(k_taps - 1 - k))
        t = shifted * w[k]
        dx = t if dx is None else dx + t
        dws.append(jnp.sum(shifted * xcat, axis=0, keepdims=True))
    return dx, tuple(dws)


_conv.defvjp(_conv_fwd, _conv_bwd)


@jax.custom_vjp
def _tri_inv(l):
    n = l.shape[-1]
    r = lax.broadcasted_iota(jnp.int32, (n, n), 0)
    c = lax.broadcasted_iota(jnp.int32, (n, n), 1)
    eye = (r == c).astype(F32)
    p = eye - l
    lp = l
    steps = int(math.log2(n)) - 1
    for i in range(steps):
        dot = _bdot if i < 2 else functools.partial(_bf16_dot, contract=((2,), (1,)))
        lp = dot(lp, lp)
        p = p + dot(p, lp)
    return p


def _tri_inv_fwd(l):
    t = _tri_inv(l)
    return t, t


def _tri_inv_bwd(t, dt):
    tt = jnp.swapaxes(t, 1, 2)
    return (-_bdot(tt, _bdot(dt, tt)),)


_tri_inv.defvjp(_tri_inv_fwd, _tri_inv_bwd)


@jax.custom_vjp
def _tri_inv_saved(l, t):
    return t


def _tri_inv_saved_fwd(l, t):
    return t, t


def _tri_inv_saved_bwd(t, dt):
    return _tri_inv_bwd(t, dt) + (jnp.zeros_like(t),)


_tri_inv_saved.defvjp(_tri_inv_saved_fwd, _tri_inv_saved_bwd)


def _dn_glue(qkvcat, z, ba, s_in, cw, a_row, dtb_row, nw_row, t_saved=None):
    t_rows = z.shape[0]
    nc = t_rows // CHUNK
    nb = nc * HEADS

    qkv = jax.nn.silu(_conv(qkvcat, cw))

    def chunks(t, off):
        return _stack([t[n * CHUNK:(n + 1) * CHUNK, off + h * DK: off + (h + 1) * DK]
                       for n in range(nc) for h in range(HEADS)])

    q = chunks(qkv, 0)
    k = chunks(qkv, 512)
    v = chunks(qkv, 1024)
    q = q * lax.rsqrt(jnp.sum(q * q, axis=-1, keepdims=True) + L2_EPS) * (DK ** -0.5)
    k = k * lax.rsqrt(jnp.sum(k * k, axis=-1, keepdims=True) + L2_EPS)

    lane = lax.broadcasted_iota(jnp.int32, (LANES, HEADS * DK), 0)
    head_of_col = lax.broadcasted_iota(jnp.int32, (LANES, HEADS * DK), 1) // DK
    e_beta = (head_of_col == lane).astype(F32)
    e_g = (head_of_col + HEADS == lane).astype(F32)
    beta_l = jax.nn.sigmoid(ba)
    g_l = -jnp.exp(a_row) * jax.nn.softplus(ba + dtb_row)
    beta = chunks(jnp.dot(beta_l, e_beta, precision=MID, preferred_element_type=F32), 0)
    g = chunks(jnp.dot(g_l, e_g, precision=MID, preferred_element_type=F32), 0)

    r = lax.broadcasted_iota(jnp.int32, (CHUNK, CHUNK), 0)
    c = lax.broadcasted_iota(jnp.int32, (CHUNK, CHUNK), 1)
    causal = r >= c
    strict = r > c
    tril_b = jnp.broadcast_to(causal.astype(F32), (nb, CHUNK, CHUNK))
    gi_b = _bdot(tril_b, g, HI)
    gi = gi_b[:, :, :CHUNK]
    gj = jnp.swapaxes(gi, 1, 2)
    decay = jnp.where(causal, jnp.exp(jnp.where(causal, gi - gj, 0.0)), 0.0)
    kb = k * beta
    l_mat = jnp.where(strict, _fdot_nt(kb, k) * decay, 0.0)
    t_mat = _tri_inv(l_mat) if t_saved is None else _tri_inv_saved(l_mat, t_saved)
    e_gi = jnp.exp(gi_b)
    w_mat = _fdot(t_mat, kb * e_gi)
    u_mat = _fdot(t_mat, v * beta)
    a_qk = _fdot_nt(q, k) * decay
    q_g = q * e_gi
    gl_b = jnp.broadcast_to(jnp.sum(g, axis=1, keepdims=True), g.shape)
    k_d = k * jnp.exp(gl_b - gi_b)
    e_gl = jnp.exp(gl_b)
    g_last = jnp.concatenate([e_gl, e_gl], axis=1)

    state = s_in
    rows = []
    for n in range(nc):
        sl = slice(n * HEADS, (n + 1) * HEADS)
        u_new = u_mat[sl] - _fdot(w_mat[sl], state)
        o_n = _fdot(q_g[sl], state) + _fdot(a_qk[sl], u_new)
        state = state * g_last[sl] + _fdot_tn(k_d[sl], u_new)
        o_n = o_n * lax.rsqrt(jnp.mean(o_n * o_n, axis=-1, keepdims=True) + RMS_EPS) * nw_row
        z_n = _stack([z[n * CHUNK:(n + 1) * CHUNK, h * DK:(h + 1) * DK] for h in range(HEADS)])
        o_n = o_n * jax.nn.silu(z_n)
        rows.append(jnp.concatenate([o_n[h] for h in range(HEADS)], axis=-1))
    return jnp.concatenate(rows, axis=0), state, t_mat


def _sg_glue(uv, lng, lnb, w_s, bs_t):
    t_rows = uv.shape[0]
    y = jax.nn.gelu(uv)
    u = y[:, :SG_W]
    v = _ln(y[:, SG_W:], lng, lnb)
    r = lax.broadcasted_iota(jnp.int32, (LANES, LANES), 0)
    c = lax.broadcasted_iota(jnp.int32, (LANES, LANES), 1)
    wm = jnp.where(r >= c, w_s, 0.0)
    lane = lax.broadcasted_iota(jnp.int32, (LANES, SG_W), 0)
    group_of_col = lax.broadcasted_iota(jnp.int32, (LANES, SG_W), 1) // LANES
    e_grp = (group_of_col == lane).astype(F32)
    bias = jnp.dot(bs_t, e_grp, precision=HI, preferred_element_type=F32)
    outs = []
    for n in range(t_rows // LANES):
        vb = v[n * LANES:(n + 1) * LANES]
        vg = _stack([vb[:, g * LANES:(g + 1) * LANES] for g in range(4)])
        mg = _fdot(wm, vg)
        mixed = jnp.concatenate([mg[g] for g in range(4)], axis=-1) + bias
        outs.append(u[n * LANES:(n + 1) * LANES] * mixed)
    return jnp.concatenate(outs, axis=0)


def _merge_glue(ga, gb, ya, yb):
    return jax.nn.sigmoid(ga) * ya + jax.nn.sigmoid(gb) * yb


def _res_ln_glue(x, r, g, b):
    return _ln(ALPHA * x + r, g, b)


def _ffn_glue(ua, ub, cwa, cwb):
    return jax.nn.silu(_conv(ua, cwa)) * _conv(ub, cwb)


def _row(t, c, col=0):
    return pl.BlockSpec((t, c), lambda i: (i, col))


def _row_rev(t, c, nt, col=0):
    return pl.BlockSpec((t, c), lambda i: (nt - 1 - i, col))


def _halo(t, c, nt=None):
    per = t // HALO
    if nt is None:
        return pl.BlockSpec((HALO, c), lambda i: (jnp.maximum(i * per - 1, 0), 0))
    return pl.BlockSpec((HALO, c), lambda i: (jnp.maximum((nt - 1 - i) * per - 1, 0), 0))


def _full(shape):
    nd = len(shape)
    return pl.BlockSpec(shape, lambda i: (0,) * nd)


ANY = pl.BlockSpec(memory_space=pl.ANY)


def _sds(shape, dtype=F32):
    return jax.ShapeDtypeStruct(shape, dtype)


def _tile(s, want=256):
    for t in (want, 256, 128):
        if s % t == 0:
            return t
    raise ValueError(f"sequence length {s} is not a multiple of 128")


def proj_fwd(x, w, name, carry=None):
    s = x.shape[0]
    t = _tile(s, 512)
    nt = s // t
    segs = [(0, 2048), (2048, 3072), (3072, 4096), (4096, 5120), (5120, IN_COLS_PAD)]

    def body(x_ref, w_ref, p_ref):
        xb = x_ref[...].astype(BF16)
        for lo, hi in segs:
            p_ref[:, lo:hi] = jnp.dot(xb, w_ref[:, lo:hi], preferred_element_type=F32)

    return _host_call(
        body, carry, nt, grid=(nt,), name=name, in_specs=[_row(t, D), _full((D, IN_COLS_PAD))],
        out_specs=[_row(t, IN_COLS_PAD)], out_shape=[_sds((s, IN_COLS_PAD))], scratch_shapes=[], operands=(x, w))


def dn_fwd(p, cq, a_row, dtb_row, nw_row, name, carry=None):
    s = p.shape[0]
    t = _tile(s, DN_TILE)
    nt = s // t

    nb = HEADS * t // CHUNK

    def body(qkv_ref, halo_ref, z_ref, ba_ref, cq_ref, a_ref, dtb_ref, nw_ref, o_ref, sst_ref, tinv_ref, s_scr):
        i = pl.program_id(0)

        @pl.when(i == 0)
        def _():
            s_scr[...] = jnp.zeros_like(s_scr)

        halo = jnp.where(i == 0, 0.0, halo_ref[...])
        qkvcat = jnp.concatenate([halo, qkv_ref[...]], axis=0)
        cw = tuple(cq_ref[k:k + 1, :] for k in range(4))
        s_in = s_scr[...]
        sst_ref[0] = s_in
        o, s_out, t_mat = _dn_glue(qkvcat, z_ref[...], ba_ref[...], s_in, cw, a_ref[...], dtb_ref[...], nw_ref[...])
        o_ref[...] = o.astype(BF16)
        tinv_ref[0] = t_mat
        s_scr[...] = s_out

    return _host_call(
        body, carry, nt, grid=(nt,), name=name,
        in_specs=[_row(t, QKV_W), _halo(t, QKV_W), _row(t, Z_W, C_Z // Z_W), _row(t, LANES, C_BA // LANES),
                  _full((4, QKV_W)), _full((1, LANES)), _full((1, LANES)), _full((1, LANES))],
        out_specs=[_row(t, Z_W), pl.BlockSpec((1, HEADS, DK, DK), lambda i: (i, 0, 0, 0)),
                   pl.BlockSpec((1, nb, CHUNK, CHUNK), lambda i: (i, 0, 0, 0))],
        out_shape=[_sds((s, Z_W), BF16), _sds((nt, HEADS, DK, DK)), _sds((nt, nb, CHUNK, CHUNK))],
        scratch_shapes=[pltpu.VMEM((HEADS, DK, DK), F32)], operands=(p, p, p, p, cq, a_row, dtb_row, nw_row))


def sg_fwd(p, lng, lnb, w_s, bs_t, name):
    s = p.shape[0]
    t = _tile(s, 512)

    def body(uv_ref, lng_ref, lnb_ref, ws_ref, bs_ref, o_ref):
        o_ref[...] = _sg_glue(uv_ref[...], lng_ref[...], lnb_ref[...], ws_ref[...], bs_ref[...]).astype(BF16)

    return pl.pallas_call(
        body, grid=(s // t,), name=name,
        in_specs=[_row(t, 2 * SG_W, C_UV // (2 * SG_W)), _full((1, SG_W)), _full((1, SG_W)),
                  _full((4, LANES, LANES)), _full((LANES, LANES))],
        out_specs=_row(t, SG_W), out_shape=_sds((s, SG_W), BF16), compiler_params=_params())(p, lng, lnb, w_s, bs_t)


def merge_fwd(oa, ob, p, x, wa, wb, wo, g1, b1, name):
    s = x.shape[0]
    t = _tile(s, 512)

    def body(oa_ref, ob_ref, ga_ref, gb_ref, x_ref, wa_ref, wb_ref, wo_ref, g_ref, b_ref, x1_ref):
        ya = _mm(oa_ref[...], wa_ref[...])
        yb = _mm(ob_ref[...], wb_ref[...])
        h = _merge_glue(ga_ref[...], gb_ref[...], ya, yb)
        x1_ref[...] = _res_ln_glue(x_ref[...], _mm(h, wo_ref[...]), g_ref[...], b_ref[...])

    return pl.pallas_call(
        body, grid=(s // t,), name=name,
        in_specs=[_row(t, Z_W), _row(t, SG_W), _row(t, D, C_GA // D), _row(t, D, C_GB // D), _row(t, D),
                  _full((Z_W, D)), _full((SG_W, D)), _full((D, D)), _full((1, D)), _full((1, D))],
        out_specs=_row(t, D), out_shape=_sds((s, D)), compiler_params=_params())(oa, ob, p, p, x, wa, wb, wo, g1, b1)


def _load_ffn_weights(wup_hbm, wdn_hbm, wup_v, wdn_v, up_slots, dn_slots):
    for n, k in enumerate(up_slots):
        pltpu.sync_copy(wup_hbm.at[k], wup_v.at[n])
    for n, k in enumerate(dn_slots):
        pltpu.sync_copy(wdn_hbm.at[k], wdn_v.at[pl.ds(n * DN_SHARD, DN_SHARD)])


def ffn_fwd(x1, wup4, cf4, wdn4, g2, b2, name, carry=None):
    s = x1.shape[0]
    t = _tile(s, 512)
    nt = s // t

    def body(x1_ref, halo_ref, wup_hbm, cf_ref, wdn_hbm, g_ref, b_ref, pre_ref, x2_ref, wup_v, wdn_v):
        i = pl.program_id(0)

        @pl.when(i == 0)
        def _():
            _load_ffn_weights(wup_hbm, wdn_hbm, wup_v, wdn_v, range(4), range(4))

        x1v = x1_ref[...]
        halo = jnp.where(i == 0, 0.0, halo_ref[...])
        x1cat = jnp.concatenate([halo, x1v], axis=0).astype(BF16)
        f = None
        for h in range(2):
            ua = jnp.dot(x1cat, wup_v[h], preferred_element_type=F32)
            ub = jnp.dot(x1cat, wup_v[2 + h], preferred_element_type=F32)
            cwa = tuple(cf_ref[h, k:k + 1, :] for k in range(3))
            cwb = tuple(cf_ref[2 + h, k:k + 1, :] for k in range(3))
            act = _ffn_glue(ua, ub, cwa, cwb)
            fh = _mm(act, wdn_v[h * FFN_HALF:(h + 1) * FFN_HALF, :])
            f = fh if f is None else f + fh
        pre = ALPHA * x1v + f
        pre_ref[...] = pre
        x2_ref[...] = _ln(pre, g_ref[...], b_ref[...])

    return _host_call(
        body, carry, nt, grid=(nt,), name=name,
        in_specs=[_row(t, D), _halo(t, D), ANY, _full((4, 3, FFN_HALF)), ANY, _full((1, D)), _full((1, D))],
        out_specs=[_row(t, D), _row(t, D)], out_shape=[_sds((s, D)), _sds((s, D))],
        scratch_shapes=[pltpu.VMEM((4, D, FFN_HALF), BF16), pltpu.VMEM((FFN, D), BF16)],
        operands=(x1, x1, wup4, cf4, wdn4, g2, b2))


def loss_ln_bwd(pre, tgt, g, b, name):
    s = pre.shape[0]
    t = _tile(s, 512)

    def body(pre_ref, t_ref, g_ref, b_ref, dpre_ref, dg_ref, db_ref, loss_ref):
        first = pl.program_id(0) == 0
        y, vjp = jax.vjp(_ln, pre_ref[...], g_ref[...], b_ref[...])
        e = y - t_ref[...]
        dpre, dg, db = vjp(e * (1.0 / D))
        dpre_ref[...] = dpre
        _acc(dg_ref, dg, first)
        _acc(db_ref, db, first)
        part = jnp.sum(jnp.sum(e * e, axis=1, keepdims=True), axis=0, keepdims=True) * (0.5 / D)
        _acc(loss_ref, jnp.broadcast_to(part, loss_ref.shape), first)

    return pl.pallas_call(
        body, grid=(s // t,), name=name, in_specs=[_row(t, D), _row(t, D), _full((1, D)), _full((1, D))],
        out_specs=[_row(t, D), _full((1, D)), _full((1, D)), _full((8, LANES))],
        out_shape=[_sds((s, D)), _sds((1, D)), _sds((1, D)), _sds((8, LANES))], compiler_params=_params())(pre, tgt, g, b)


def _acc(ref, val, first):
    @pl.when(first)
    def _():
        ref[...] = val

    @pl.when(jnp.logical_not(first))
    def _():
        ref[...] += val


def _acc_tn(acc_ref, a, b, first, seg):
    n = b.shape[1]
    for lo in range(0, n, seg):
        hi = min(lo + seg, n)
        _acc(acc_ref.at[:, lo:hi], _mm_tn(a, b[:, lo:hi]), first)


def ln_bwd(pre, dy, g, b, name):
    s = pre.shape[0]
    t = _tile(s, 512)

    def body(pre_ref, dy_ref, g_ref, b_ref, dpre_ref, dg_ref, db_ref):
        _, vjp = jax.vjp(_ln, pre_ref[...], g_ref[...], b_ref[...])
        dpre, dg, db = vjp(dy_ref[...])
        dpre_ref[...] = dpre
        first = pl.program_id(0) == 0
        _acc(dg_ref, dg, first)
        _acc(db_ref, db, first)

    return pl.pallas_call(
        body, grid=(s // t,), name=name, in_specs=[_row(t, D), _row(t, D), _full((1, D)), _full((1, D))],
        out_specs=[_row(t, D), _full((1, D)), _full((1, D))],
        out_shape=[_sds((s, D)), _sds((1, D)), _sds((1, D))], compiler_params=_params())(pre, dy, g, b)


def ffn_bwd(x1, df, acc_in, acc_scale, wup4, cf4, wdn4, h, name, carry=None):
    s = x1.shape[0]
    t = _tile(s, 512)
    nt = s // t

    def body(x1_ref, halo_ref, df_ref, acc_ref, wup_hbm, cf_ref, wdn_hbm,
             dx1_ref, dcf_ref, dua_ref, dub_ref, act_ref, wup_v, wdn_v, carry):
        i = pl.program_id(0)
        j = nt - 1 - i
        first = i == 0

        @pl.when(first)
        def _():
            _load_ffn_weights(wup_hbm, wdn_hbm, wup_v, wdn_v, (h, 2 + h), (2 * h, 2 * h + 1))
            carry[...] = jnp.zeros_like(carry)

        halo = jnp.where(j == 0, 0.0, halo_ref[...])
        x1cat = jnp.concatenate([halo, x1_ref[...]], axis=0).astype(BF16)
        ua = jnp.dot(x1cat, wup_v[0], preferred_element_type=F32)
        ub = jnp.dot(x1cat, wup_v[1], preferred_element_type=F32)
        cwa = tuple(cf_ref[h, k:k + 1, :] for k in range(3))
        cwb = tuple(cf_ref[2 + h, k:k + 1, :] for k in range(3))
        act, vjp = jax.vjp(_ffn_glue, ua, ub, cwa, cwb)
        act_ref[...] = act.astype(BF16)
        dact = _mm_nt(df_ref[...], wdn_v[...])
        dua, dub, dcwa, dcwb = vjp(dact)
        dups = []
        for n, (du, out_ref) in enumerate(((dua, dua_ref), (dub, dub_ref))):
            dups.append(jnp.concatenate([du[HALO:t], du[t:] + carry[n]], axis=0).astype(BF16))
            carry[n] = du[:HALO]
            out_ref[...] = dups[n]
        for k in range(3):
            _acc(dcf_ref.at[0, k:k + 1, :], dcwa[k], first)
            _acc(dcf_ref.at[1, k:k + 1, :], dcwb[k], first)
        dx1_ref[...] = acc_scale * acc_ref[...] + _mm_nt(dups[0], wup_v[0]) + _mm_nt(dups[1], wup_v[1])

    hidden = _row_rev(t, FFN_HALF, nt)
    return _host_call(
        body, carry, nt, grid=(nt,), name=name,
        in_specs=[_row_rev(t, D, nt), _halo(t, D, nt), _row_rev(t, D, nt), _row_rev(t, D, nt),
                  ANY, _full((4, 3, FFN_HALF)), ANY],
        out_specs=[_row_rev(t, D, nt), _full((2, 3, FFN_HALF)), hidden, hidden, hidden],
        out_shape=[_sds((s, D)), _sds((2, 3, FFN_HALF))] + [_sds((s, FFN_HALF), BF16)] * 3,
        scratch_shapes=[pltpu.VMEM((2, D, FFN_HALF), BF16), pltpu.VMEM((FFN_HALF, D), BF16),
                        pltpu.VMEM((2, HALO, FFN_HALF), F32)],
        operands=(x1, x1, df, acc_in, wup4, cf4, wdn4))


def wgrad_mm(a, b, slots, slot, into, name):
    s, m = a.shape
    n = b.shape[1]
    tk = _tile(s, WGRAD_TOKENS)

    def body(a_ref, b_ref, *rest):
        _acc(rest[-1], _mm_tn(a_ref[...], b_ref[...]), pl.program_id(0) == 0)

    return pl.pallas_call(
        body, grid=(s // tk,), name=name,
        in_specs=[pl.BlockSpec((tk, m), lambda k: (k, 0)), pl.BlockSpec((tk, n), lambda k: (k, 0))]
        + ([] if into is None else [ANY]),
        out_specs=pl.BlockSpec((None, m, n), lambda k: (slot, 0, 0)), out_shape=_sds((slots, m, n)),
        input_output_aliases={} if into is None else {2: 0},
        compiler_params=_params())(*((a, b) if into is None else (a, b, into)))


def merge_bwd(oa, ob, p, x, dx1, wa, wb, wo, g1, b1, name, carry=None):
    s = x.shape[0]
    t = _tile(s)

    def body(oa_ref, ob_ref, ga_ref, gb_ref, x_ref, dx1_ref, wa_ref, wb_ref, wo_ref, g_ref, b_ref,
             doa_ref, dob_ref, dga_ref, dgb_ref, dx_ref, dg_ref, db_ref, h_ref, dr_ref, dya_ref, dyb_ref):
        first = pl.program_id(0) == 0
        oa = oa_ref[...]
        ob = ob_ref[...]
        ya = _mm(oa, wa_ref[...])
        yb = _mm(ob, wb_ref[...])
        h, vjp1 = jax.vjp(_merge_glue, ga_ref[...], gb_ref[...], ya, yb)
        hb = h.astype(BF16)
        r = _mm(hb, wo_ref[...])
        _, vjp2 = jax.vjp(_res_ln_glue, x_ref[...], r, g_ref[...], b_ref[...])
        dx, dr, dg, db = vjp2(dx1_ref[...])
        dx_ref[...] = dx
        _acc(dg_ref, dg, first)
        _acc(db_ref, db, first)
        drb = dr.astype(BF16)
        h_ref[...] = hb
        dr_ref[...] = drb
        dh = _mm_nt(drb, wo_ref[...])
        dga, dgb, dya, dyb = vjp1(dh)
        dga_ref[...] = dga.astype(BF16)
        dgb_ref[...] = dgb.astype(BF16)
        dyab = dya.astype(BF16)
        dybb = dyb.astype(BF16)
        dya_ref[...] = dyab
        dyb_ref[...] = dybb
        doa_ref[...] = _mm_nt(dyab, wa_ref[...]).astype(BF16)
        dob_ref[...] = _mm_nt(dybb, wb_ref[...]).astype(BF16)

    wide = _row(t, D)
    return _host_call(
        body, carry, s // t, grid=(s // t,), name=name,
        in_specs=[_row(t, Z_W), _row(t, SG_W), _row(t, D, C_GA // D), _row(t, D, C_GB // D), _row(t, D), _row(t, D),
                  _full((Z_W, D)), _full((SG_W, D)), _full((D, D)), _full((1, D)), _full((1, D))],
        out_specs=[_row(t, Z_W), _row(t, SG_W), wide, wide, wide, _full((1, D)), _full((1, D)), wide, wide, wide, wide],
        out_shape=[_sds((s, Z_W), BF16), _sds((s, SG_W), BF16), _sds((s, D), BF16), _sds((s, D), BF16), _sds((s, D)),
                   _sds((1, D)), _sds((1, D))] + [_sds((s, D), BF16)] * 4,
        scratch_shapes=[], operands=(oa, ob, p, p, x, dx1, wa, wb, wo, g1, b1))


def sg_bwd(p, dob, lng, lnb, w_s, bs_t, name, carry=None):
    s = p.shape[0]
    t = _tile(s, 512)

    def body(uv_ref, dob_ref, lng_ref, lnb_ref, ws_ref, bs_ref, duv_ref, dlng_ref, dlnb_ref, dws_ref, dbs_ref):
        first = pl.program_id(0) == 0
        _, vjp = jax.vjp(_sg_glue, uv_ref[...], lng_ref[...], lnb_ref[...], ws_ref[...], bs_ref[...])
        duv, dlng, dlnb, dws, dbs = vjp(dob_ref[...].astype(F32))
        duv_ref[...] = duv.astype(BF16)
        _acc(dlng_ref, dlng, first)
        _acc(dlnb_ref, dlnb, first)
        _acc(dws_ref, dws, first)
        _acc(dbs_ref, dbs, first)

    return _host_call(
        body, carry, s // t, grid=(s // t,), name=name,
        in_specs=[_row(t, 2 * SG_W, C_UV // (2 * SG_W)), _row(t, SG_W), _full((1, SG_W)), _full((1, SG_W)),
                  _full((4, LANES, LANES)), _full((LANES, LANES))],
        out_specs=[_row(t, 2 * SG_W), _full((1, SG_W)), _full((1, SG_W)), _full((4, LANES, LANES)), _full((LANES, LANES))],
        out_shape=[_sds((s, 2 * SG_W), BF16), _sds((1, SG_W)), _sds((1, SG_W)), _sds((4, LANES, LANES)), _sds((LANES, LANES))],
        scratch_shapes=[], operands=(p, dob, lng, lnb, w_s, bs_t))


def dn_bwd(p, sst, tinv, doa, cq, a_row, dtb_row, nw_row, name, carry=None):
    s = p.shape[0]
    t = _tile(s, DN_TILE)
    nt = s // t

    def body(qkv_ref, halo_ref, z_ref, ba_ref, sst_ref, tinv_ref, doa_ref, cq_ref, a_ref, dtb_ref, nw_ref,
             dqkv_ref, dz_ref, dba_ref, dcq_ref, da_ref, ddtb_ref, dnw_ref, ds_scr, carry):
        i = pl.program_id(0)
        j = nt - 1 - i
        first = i == 0

        @pl.when(first)
        def _():
            ds_scr[...] = jnp.zeros_like(ds_scr)
            carry[...] = jnp.zeros_like(carry)

        halo = jnp.where(j == 0, 0.0, halo_ref[...])
        qkvcat = jnp.concatenate([halo, qkv_ref[...]], axis=0)
        cw = tuple(cq_ref[k:k + 1, :] for k in range(4))
        t_saved = tinv_ref[0]
        _, vjp = jax.vjp(lambda *args: _dn_glue(*args, t_saved=t_saved)[:2],
                         qkvcat, z_ref[...], ba_ref[...], sst_ref[0], cw, a_ref[...], dtb_ref[...], nw_ref[...])
        dqkvcat, dz, dba, ds_in, dcw, da, ddtb, dnw = vjp((doa_ref[...].astype(F32), ds_scr[...]))
        ds_scr[...] = ds_in
        dz_ref[...] = dz.astype(BF16)
        dba_ref[...] = dba.astype(BF16)
        dtile = dqkvcat[HALO:]
        dqkv_ref[...] = dtile.astype(BF16)
        dqkv_ref[t - HALO:t, :] = (dtile[t - HALO:] + carry[...]).astype(BF16)
        carry[...] = dqkvcat[:HALO]
        for k in range(4):
            _acc(dcq_ref.at[k:k + 1, :], dcw[k], first)
        _acc(da_ref, da, first)
        _acc(ddtb_ref, ddtb, first)
        _acc(dnw_ref, dnw, first)

    return _host_call(
        body, carry, nt, grid=(nt,), name=name,
        in_specs=[_row_rev(t, QKV_W, nt), _halo(t, QKV_W, nt), _row_rev(t, Z_W, nt, C_Z // Z_W),
                  _row_rev(t, LANES, nt, C_BA // LANES),
                  pl.BlockSpec((1, HEADS, DK, DK), lambda i: (nt - 1 - i, 0, 0, 0)),
                  pl.BlockSpec((1,) + tinv.shape[1:], lambda i: (nt - 1 - i, 0, 0, 0)), _row_rev(t, Z_W, nt),
                  _full((4, QKV_W)), _full((1, LANES)), _full((1, LANES)), _full((1, LANES))],
        out_specs=[_row_rev(t, QKV_W, nt), _row_rev(t, Z_W, nt), _row_rev(t, LANES, nt),
                   _full((4, QKV_W)), _full((1, LANES)), _full((1, LANES)), _full((1, LANES))],
        out_shape=[_sds((s, QKV_W), BF16), _sds((s, Z_W), BF16), _sds((s, LANES), BF16),
                   _sds((4, QKV_W)), _sds((1, LANES)), _sds((1, LANES)), _sds((1, LANES))],
        scratch_shapes=[pltpu.VMEM((HEADS, DK, DK), F32), pltpu.VMEM((HALO, QKV_W), F32)],
        operands=(p, p, p, p, sst, tinv, doa, cq, a_row, dtb_row, nw_row))


def proj_bwd(dps, dxd, w, name, carry=None):
    s = dxd.shape[0]
    t = _tile(s, 512)
    n = len(dps)

    def body(*refs):
        dp_refs, dxd_ref, w_hbm, dx_ref, w_v = refs[:n], refs[n], refs[n + 1], refs[n + 2], refs[n + 3]

        @pl.when(pl.program_id(0) == 0)
        def _():
            pltpu.sync_copy(w_hbm, w_v)

        dp = jnp.concatenate([r[...] for r in dp_refs], axis=1)
        dx_ref[...] = dxd_ref[...] + _mm_nt(dp, w_v[...])

    return _host_call(
        body, carry, s // t, grid=(s // t,), name=name,
        in_specs=[_row(t, dp.shape[1]) for dp in dps] + [_row(t, D), ANY],
        out_specs=[_row(t, D)], out_shape=[_sds((s, D))],
        scratch_shapes=[pltpu.VMEM((D, IN_COLS_PAD), BF16)], operands=(*dps, dxd, w))


def wgrad(x, dp, col, into, name):
    s, n = dp.shape
    tk = _tile(s, WGRAD_TOKENS)
    tn = next(c for c in (1024, 768, 512, 256, 128) if n % c == 0 and col % c == 0)
    block = col // tn

    def body(x_ref, dp_ref, *rest):
        o_ref = rest[-1]
        _acc(o_ref, _mm_tn(x_ref[...], dp_ref[...]), pl.program_id(1) == 0)

    operands = (x, dp) if into is None else (x, dp, into)
    return pl.pallas_call(
        body, grid=(n // tn, s // tk), name=name,
        in_specs=[pl.BlockSpec((tk, D), lambda j, k: (k, 0)), pl.BlockSpec((tk, tn), lambda j, k: (k, j))]
        + ([] if into is None else [ANY]),
        out_specs=pl.BlockSpec((D, tn), lambda j, k: (0, block + j)), out_shape=_sds((D, IN_COLS_PAD)),
        input_output_aliases={} if into is None else {2: 0},
        compiler_params=_params(2))(*operands)


def _rows_block(rows, cols):
    cap = max(HALO, (2 * 1024 * 1024) // (cols * 4))
    for cand in range(min(rows, cap) // HALO * HALO, HALO - 1, -HALO):
        if rows % cand == 0:
            return cand
    return rows


def adam_call(w, g, m, v, name):
    rows, cols = w.shape
    tr = _rows_block(rows, cols)
    c1 = 1.0 - ADAM_B1 ** ADAM_STEP
    c2 = 1.0 - ADAM_B2 ** ADAM_STEP

    def body(w_ref, g_ref, m_ref, v_ref, go_ref, d_ref, nm_ref, nv_ref):
        gv = g_ref[...]
        go_ref[...] = gv
        nm = ADAM_B1 * m_ref[...] + (1.0 - ADAM_B1) * gv
        nv = ADAM_B2 * v_ref[...] + (1.0 - ADAM_B2) * (gv * gv)
        d_ref[...] = -ADAM_LR * ((nm / c1) / (jnp.sqrt(nv / c2) + ADAM_EPS) + ADAM_WD * w_ref[...])
        nm_ref[...] = nm
        nv_ref[...] = nv

    spec = pl.BlockSpec((tr, cols), lambda i: (i, 0))
    return pl.pallas_call(
        body, grid=(rows // tr,), name=name, in_specs=[spec] * 4, out_specs=[spec] * 4,
        out_shape=[_sds((rows, cols))] * 4, compiler_params=_params())(w, g, m, v)


def _place():
    return lax.axis_index("x"), lax.axis_index("y"), lax.axis_index("c")


def _other_chips(x, y):
    return [(1 - x, y), (x, 1 - y), (1 - x, 1 - y)]


def _remote(src, dst, send_sem, recv_sem, to):
    return pltpu.make_async_remote_copy(src_ref=src, dst_ref=dst, send_sem=send_sem, recv_sem=recv_sem,
                                        device_id=to, device_id_type=MESH)


class _Gather:
    def __init__(self, ins, outs, send_sems, recv_sems, local_sems):
        self.ins, self.outs, self.n = ins, outs, len(ins)
        self.send_sems, self.recv_sems, self.local_sems = send_sems, recv_sems, local_sems
        self.x, self.y, self.c = _place()
        self.me = 2 * self.x + self.y
        self.chips = _other_chips(self.x, self.y)

    def _copy(self, t, k, slot, part, to, src=None):
        dst = self.outs[t].at[slot, part]
        return _remote(dst if src is None else src, dst, self.send_sems.at[6 * t + k], self.recv_sems.at[6 * t + k], to)

    def _mine(self):
        return [pltpu.make_async_copy(self.ins[t].at[p], self.outs[t].at[self.me, p], self.local_sems.at[2 * t + p])
                for t in range(self.n) for p in range(2)]

    def _first(self):
        return [self._copy(t, k, self.me, self.c, (cx, cy, self.c), src=self.ins[t].at[self.c])
                for k, (cx, cy) in enumerate(self.chips) for t in range(self.n)]

    def start(self):
        for cp in self._mine() + self._first():
            cp.start()

    def finish(self):
        x, y, c = self.x, self.y, self.c
        passed = []
        for k, (cx, cy) in enumerate(self.chips):
            for t in range(self.n):
                self._copy(t, k, 2 * cx + cy, c, (x, y, c)).wait_recv()
                passed.append(self._copy(t, 3 + k, 2 * cx + cy, c, (x, y, 1 - c)))
                passed[-1].start()
        for k, (cx, cy) in enumerate(self.chips):
            for t in range(self.n):
                self._copy(t, 3 + k, 2 * cx + cy, 1 - c, (x, y, c)).wait_recv()
        for cp in self._first() + passed:
            cp.wait_send()
        for cp in self._mine():
            cp.wait()


class _Carried(NamedTuple):
    ins: tuple
    out_shapes: tuple
    scratch: tuple
    make: Callable
    aliases: dict


def _host_call(body, carry, steps, *, grid, name, in_specs, out_specs, out_shape, scratch_shapes, operands):
    in_specs, out_specs, out_shape, scratch_shapes = list(in_specs), list(out_specs), list(out_shape), list(scratch_shapes)
    aliases = {}
    if carry is not None:
        n_in, n_out, n_scr = len(in_specs), len(out_specs), len(scratch_shapes)
        n_ci, n_co = len(carry.ins), len(carry.out_shapes)
        plain = body

        def body(*refs):
            ins, cins = refs[:n_in], refs[n_in:n_in + n_ci]
            outs = refs[n_in + n_ci:n_in + n_ci + n_out]
            couts = refs[n_in + n_ci + n_out:n_in + n_ci + n_out + n_co]
            rest = refs[n_in + n_ci + n_out + n_co:]
            exchange = carry.make(cins, couts, *rest[n_scr:])
            pl.when(pl.program_id(0) == 0)(exchange.start)
            plain(*ins, *outs, *rest[:n_scr])
            pl.when(pl.program_id(0) == steps - 1)(exchange.finish)

        aliases = {n_in + i: n_out + j for i, j in carry.aliases.items()}
        in_specs += [ANY] * n_ci
        out_specs += [ANY] * n_co
        out_shape += list(carry.out_shapes)
        scratch_shapes += list(carry.scratch)
        operands = tuple(operands) + tuple(carry.ins)
    return pl.pallas_call(
        body, grid=grid, name=name, in_specs=in_specs, out_specs=out_specs, out_shape=out_shape,
        scratch_shapes=scratch_shapes, input_output_aliases=aliases, compiler_params=_params(len(grid)))(*operands)


def exchange(carry, name):
    n_i, n_o = len(carry.ins), len(carry.out_shapes)

    def body(*refs):
        ex = carry.make(refs[:n_i], refs[n_i:n_i + n_o], *refs[n_i + n_o:])
        ex.start()
        ex.finish()

    return pl.pallas_call(
        body, name=name, in_specs=[ANY] * n_i, out_specs=[ANY] * n_o, out_shape=list(carry.out_shapes),
        scratch_shapes=list(carry.scratch), input_output_aliases=dict(carry.aliases),
        compiler_params=pltpu.CompilerParams(vmem_limit_bytes=VMEM_LIMIT))(*carry.ins)


def _dma_sems(*counts):
    return tuple(pltpu.SemaphoreType.DMA((n,)) for n in counts)


def carried_gather(shards):
    n = len(shards)
    return _Carried(tuple(shards), tuple(_sds((N_CHIPS,) + a.shape, a.dtype) for a in shards),
                    _dma_sems(6 * n, 6 * n, 2 * n), _Gather, {})


class _PairSwap:
    def __init__(self, ins, outs, send_sems, recv_sems):
        x, y, c = _place()
        self.copies = [_remote(ins[t].at[:, 1 - c], outs[t], send_sems.at[t], recv_sems.at[t], (x, y, 1 - c))
                       for t in range(len(ins))]

    def start(self):
        for cp in self.copies:
            cp.start()

    def finish(self):
        for cp in self.copies:
            cp.wait()


def carried_pair_swap(views):
    n = len(views)
    return _Carried(tuple(views), tuple(_sds((v.shape[0],) + v.shape[2:]) for v in views), _dma_sems(n, n), _PairSwap, {})


class _Scatter:
    def __init__(self, srcs, outs, send_sems, recv_sems, local_sems, pieces):
        self.srcs, self.outs, self.pieces, self.n = srcs, outs, pieces, len(pieces)
        self.send_sems, self.recv_sems, self.local_sems = send_sems, recv_sems, local_sems
        self.x, self.y, self.c = _place()
        self.me = 2 * self.x + self.y

    def _piece(self, t, k):
        idx, lead, cols = self.pieces[t][k]
        ref = self.srcs[idx].at[lead]
        return ref if cols is None else ref.at[:, pl.ds(cols[0], cols[1])]

    def _local(self, t, k):
        return pltpu.make_async_copy(self._piece(t, k), self.outs[t].at[k], self.local_sems.at[t])

    def _each_chip(self, mine, others):
        for k in range(N_CHIPS):
            pl.when(self.me == k)(functools.partial(mine, k))
            pl.when(self.me != k)(functools.partial(others, k))

    def start(self):
        def mine(k):
            for t in range(self.n):
                self._local(t, k).start()

        def others(k):
            for t in range(self.n):
                _remote(self._piece(t, k), self.outs[t].at[self.me], self.send_sems.at[N_CHIPS * t + k],
                        self.recv_sems.at[N_CHIPS * t + self.me], (k // 2, k % 2, self.c)).start()

        self._each_chip(mine, others)

    def finish(self):
        def mine(k):
            for t in range(self.n):
                self._local(t, k).wait()

        def others(k):
            for t in range(self.n):
                cp = _remote(self._piece(t, k), self.outs[t].at[k], self.send_sems.at[N_CHIPS * t + k],
                             self.recv_sems.at[N_CHIPS * t + k], (self.x, self.y, self.c))
                cp.wait_recv()
                cp.wait_send()

        self._each_chip(mine, others)


def carried_scatter(srcs, pieces, part_shapes):
    n = len(pieces)
    return _Carried(tuple(srcs), tuple(_sds((N_CHIPS,) + tuple(shp), srcs[0].dtype) for shp in part_shapes),
                    _dma_sems(N_CHIPS * n, N_CHIPS * n, n), functools.partial(_Scatter, pieces=pieces), {})


class _PairJoin:
    def __init__(self, ins, outs, send_sems, recv_sems, layer):
        self.ins, self.outs, self.layer, self.n = ins, outs, layer, len(ins)
        self.send_sems, self.recv_sems = send_sems, recv_sems
        self.x, self.y, self.c = _place()

    def _copy(self, t, half, to):
        return _remote(self.ins[t].at[self.layer, self.c], self.outs[t].at[self.layer, half],
                       self.send_sems.at[t], self.recv_sems.at[t], to)

    def start(self):
        for t in range(self.n):
            self._copy(t, self.c, (self.x, self.y, 1 - self.c)).start()

    def finish(self):
        for t in range(self.n):
            self._copy(t, self.c, (self.x, self.y, 1 - self.c)).wait_send()
            self._copy(t, 1 - self.c, (self.x, self.y, self.c)).wait_recv()


def carried_join(bufs, layer):
    n = len(bufs)
    return _Carried(tuple(bufs), tuple(_sds(b.shape) for b in bufs), _dma_sems(n, n),
                    functools.partial(_PairJoin, layer=layer), {t: t for t in range(n)})


def pair_add_half(mine, theirs, c_vec, name):
    g, _, h, b = mine.shape
    tr = _rows_block(h, b)

    def body(c_ref, a_ref, b_ref, o_ref):
        o_ref[...] = (a_ref[...] + b_ref[...]).astype(BF16)

    part = pl.BlockSpec((None, tr, b), lambda j, i, c: (j, i, 0))
    grid_spec = pltpu.PrefetchScalarGridSpec(
        num_scalar_prefetch=1, grid=(g, h // tr),
        in_specs=[pl.BlockSpec((None, None, tr, b), lambda j, i, c: (j, c[0], i, 0)), part], out_specs=part)
    return pl.pallas_call(body, grid_spec=grid_spec, name=name, out_shape=_sds((g, h, b), BF16),
                          compiler_params=_params(2))(c_vec, mine, theirs)


def chips_add_into(recv, into, layer, c_vec, name):
    n, h, b = recv.shape
    tr = _rows_block(h, b)

    def body(c_ref, r0, r1, r2, r3, *rest):
        rest[-1][...] = ((r0[...].astype(F32) + r1[...].astype(F32)) + r2[...].astype(F32)) + r3[...].astype(F32)

    grid_spec = pltpu.PrefetchScalarGridSpec(
        num_scalar_prefetch=1, grid=(h // tr,),
        in_specs=[pl.BlockSpec((None, tr, b), lambda i, c, k=k: (k, i, 0)) for k in range(n)]
        + ([] if into is None else [ANY]),
        out_specs=pl.BlockSpec((None, None, tr, b), lambda i, c: (layer, c[0], i, 0)))
    return pl.pallas_call(
        body, grid_spec=grid_spec, name=name, out_shape=_sds((2, 2, h, b)),
        input_output_aliases={} if into is None else {1 + n: 0},
        compiler_params=_params())(c_vec, *([recv] * n), *(() if into is None else (into,)))


N_DEV = 8


class _AllSum:
    def __init__(self, ins, outs, buf, total, send_sems, recv_sems, local_sem):
        self.v, self.out, self.buf, self.total = ins[0], outs[0], buf, total
        self.send_sems, self.recv_sems, self.local_sem = send_sems, recv_sems, local_sem
        self.x, self.y, self.c = _place()
        self.me, self.sibling = (self.x, self.y, self.c), (self.x, self.y, 1 - self.c)
        self.chips = _other_chips(self.x, self.y)

    def _slot(self, px, py, pc):
        return self.buf.at[4 * px + 2 * py + pc]

    def _copy(self, k, block, to):
        return _remote(self._slot(*block), self._slot(*block), self.send_sems.at[k], self.recv_sems.at[k], to)

    def _first(self):
        return [self._copy(0, self.me, self.sibling)] + [
            self._copy(1 + k, self.me, (*chip, self.c)) for k, chip in enumerate(self.chips)]

    def start(self):
        load = pltpu.make_async_copy(self.v, self._slot(*self.me), self.local_sem)
        load.start()
        load.wait()
        for cp in self._first():
            cp.start()

    def finish(self):
        c = self.c
        passed = [self._copy(4 + k, (*chip, c), self.sibling) for k, chip in enumerate(self.chips)]
        for k, chip in enumerate(self.chips):
            self._copy(1 + k, (*chip, c), self.me).wait_recv()
            passed[k].start()
        self._copy(0, self.sibling, self.me).wait_recv()
        for k, chip in enumerate(self.chips):
            self._copy(4 + k, (*chip, 1 - c), self.me).wait_recv()
        for cp in self._first() + passed:
            cp.wait_send()
        acc = self.buf[0]
        for d in range(1, N_DEV):
            acc = acc + self.buf[d]
        self.total[...] = acc
        store = pltpu.make_async_copy(self.total, self.out, self.local_sem)
        store.start()
        store.wait()


def carried_allsum(v):
    rows, lanes = v.shape
    scratch = (pltpu.VMEM((N_DEV, rows, lanes), F32), pltpu.VMEM((rows, lanes), F32)) + _dma_sems(7, 7) + (
        pltpu.SemaphoreType.DMA,)
    return _Carried((v,), (_sds((rows, lanes)),), scratch, _AllSum, {})


class _Both:
    def __init__(self, *exchanges):
        self.exchanges = exchanges

    def start(self):
        for ex in self.exchanges:
            ex.start()

    def finish(self):
        for ex in self.exchanges:
            ex.finish()


def carried_both(a, b):
    ai, ao, asc = len(a.ins), len(a.out_shapes), len(a.scratch)

    def make(ins, outs, *scratch):
        return _Both(a.make(ins[:ai], outs[:ao], *scratch[:asc]), b.make(ins[ai:], outs[ao:], *scratch[asc:]))

    aliases = {**a.aliases, **{ai + i: ao + j for i, j in b.aliases.items()}}
    return _Carried(a.ins + b.ins, a.out_shapes + b.out_shapes, a.scratch + b.scratch, make, aliases)


BIG = ("w_in", "w_branch_a", "w_branch_b", "w_out", "w_up", "w_down")
CONV = ("conv_qkv", "conv_ffn")
REPL =("a_log", "dt_bias", "dn_norm_w", "sg_ln_g", "sg_ln_b", "w_spatial", "b_spatial", "ln1_g", "ln1_b", "ln2_g", "ln2_b")


def _pad_rows(flat, mult):
    n = flat.shape[0]
    unit = mult * LANES
    total = -(-n // unit) * unit
    return jnp.pad(flat, (0, total - n)).reshape(total // LANES, LANES)


def _pack(arrs, mult):
    return _pad_rows(jnp.concatenate([a.reshape(-1) for a in arrs]), mult)


def _unpack(flat, shapes):
    out, off = [], 0
    for shp in shapes:
        n = math.prod(shp)
        out.append(flat[off:off + n].reshape(shp))
        off += n
    return out


def kernel(x, w_in, conv_qkv, a_log, dt_bias, dn_norm_w, w_branch_a, sg_ln_g, sg_ln_b, w_spatial, b_spatial, w_branch_b, w_out, ln1_g, ln1_b, w_up, conv_ffn, w_down, ln2_g, ln2_b, loss_target, m_w_in, m_conv_qkv, m_a_log, m_dt_bias, m_dn_norm_w, m_w_branch_a, m_sg_ln_g, m_sg_ln_b, m_w_spatial, m_b_spatial, m_w_branch_b, m_w_out, m_ln1_g, m_ln1_b, m_w_up, m_conv_ffn, m_w_down, m_ln2_g, m_ln2_b, v_w_in, v_conv_qkv, v_a_log, v_dt_bias, v_dn_norm_w, v_w_branch_a, v_sg_ln_g, v_sg_ln_b, v_w_spatial, v_b_spatial, v_w_branch_b, v_w_out, v_ln1_g, v_ln1_b, v_w_up, v_conv_ffn, v_w_down, v_ln2_g, v_ln2_b):
    names = ("w_in", "conv_qkv", "a_log", "dt_bias", "dn_norm_w", "w_branch_a", "sg_ln_g", "sg_ln_b", "w_spatial",
             "b_spatial", "w_branch_b", "w_out", "ln1_g", "ln1_b", "w_up", "conv_ffn", "w_down", "ln2_g", "ln2_b")
    w = dict(zip(names, (w_in, conv_qkv, a_log, dt_bias, dn_norm_w, w_branch_a, sg_ln_g, sg_ln_b, w_spatial,
                         b_spatial, w_branch_b, w_out, ln1_g, ln1_b, w_up, conv_ffn, w_down, ln2_g, ln2_b)))
    m = dict(zip(names, (m_w_in, m_conv_qkv, m_a_log, m_dt_bias, m_dn_norm_w, m_w_branch_a, m_sg_ln_g, m_sg_ln_b,
                         m_w_spatial, m_b_spatial, m_w_branch_b, m_w_out, m_ln1_g, m_ln1_b, m_w_up, m_conv_ffn,
                         m_w_down, m_ln2_g, m_ln2_b)))
    v = dict(zip(names, (v_w_in, v_conv_qkv, v_a_log, v_dt_bias, v_dn_norm_w, v_w_branch_a, v_sg_ln_g, v_sg_ln_b,
                         v_w_spatial, v_b_spatial, v_w_branch_b, v_w_out, v_ln1_g, v_ln1_b, v_w_up, v_conv_ffn,
                         v_w_down, v_ln2_g, v_ln2_b)))
    chip = 2 * lax.axis_index("x") + lax.axis_index("y")
    s = x.shape[1]
    xs = x.reshape(s, D)
    tgt = loss_target.reshape(s, D)

    big_names, conv_names = list(BIG), list(CONV)

    def in_two(name, l):
        rows, cols = w[name].shape[1:]
        return w[name][l].astype(BF16).reshape(2, rows // 2, cols)

    def whole(name, landed):
        rows, cols = w[name].shape[1:]
        return landed.reshape(N_CHIPS, rows, cols)

    first = exchange(carried_gather([in_two("w_in", 0)] + [w[n] for n in conv_names]), "gather_first")
    got = [{"w_in": whole("w_in", first[0])}, {}]
    conv_taps = dict(zip(conv_names, first[1:]))
    narrow, wide = ["w_branch_a", "w_branch_b", "w_out"], ["w_up", "w_down"]
    carried = {"proj_fwd0": (0, narrow), "dn_fwd0": (0, wide), "ffn_fwd0": (1, ["w_in"] + narrow), "dn_fwd1": (1, wide)}

    def carry(call):
        if call not in carried:
            return None
        l, which = carried[call]
        return carried_gather([in_two(n, l) for n in which])

    def land(call, landed):
        l, which = carried.get(call, (0, []))
        got[l].update({n: whole(n, a) for n, a in zip(which, landed)})

    def lane_row(vec, off):
        return jnp.zeros((1, LANES), F32).at[0, off:off + vec.shape[0]].set(vec)

    def side_by_side(blocks):
        return jnp.concatenate([blocks[k] for k in range(N_CHIPS)], axis=1)

    def small_params(l):
        return dict(
            cq=side_by_side(conv_taps["conv_qkv"][:, l]),
            a_row=lane_row(w["a_log"][l], HEADS), dtb_row=lane_row(w["dt_bias"][l], HEADS),
            nw_row=w["dn_norm_w"][l].reshape(1, DK),
            lng=w["sg_ln_g"][l].reshape(1, SG_W), lnb=w["sg_ln_b"][l].reshape(1, SG_W),
            w_s=w["w_spatial"][l], bs_t=jnp.zeros((LANES, LANES), F32).at[:, :4].set(w["b_spatial"][l].T),
            g1=w["ln1_g"][l].reshape(1, D), b1=w["ln1_b"][l].reshape(1, D),
            cf=conv_taps["conv_ffn"][:, l],
            g2=w["ln2_g"][l].reshape(1, D), b2=w["ln2_b"][l].reshape(1, D))

    layers, saved = [], []
    h_in = xs
    for l in range(DEPTH):
        p = small_params(l)
        wi = side_by_side(got[l]["w_in"])
        p["w_in"] = jnp.concatenate([wi[:, :2048], wi[:, 2056:3080], wi[:, 3080:5128], wi[:, 2048:2056],
                                     jnp.zeros((D, IN_COLS_PAD - 5128), BF16)], axis=1)
        proj, *landed = proj_fwd(h_in, p["w_in"], f"proj_fwd{l}", carry=carry(f"proj_fwd{l}"))
        land(f"proj_fwd{l}", landed)
        oa, sst, tinv, *landed = dn_fwd(proj, p["cq"], p["a_row"], p["dtb_row"], p["nw_row"], f"dn_fwd{l}",
                                  carry=carry(f"dn_fwd{l}"))
        land(f"dn_fwd{l}", landed)
        ob = sg_fwd(proj, p["lng"], p["lnb"], p["w_s"], p["bs_t"], f"sg_fwd{l}")
        p.update(wa=side_by_side(got[l]["w_branch_a"]), wb=side_by_side(got[l]["w_branch_b"]),
                 wo=got[l]["w_out"].reshape(D, D))
        x1 = merge_fwd(oa, ob, proj, h_in, p["wa"], p["wb"], p["wo"], p["g1"], p["b1"], f"merge_fwd{l}")
        pre2, x2, *landed = ffn_fwd(x1, got[l]["w_up"], p["cf"], got[l]["w_down"], p["g2"], p["b2"], f"ffn_fwd{l}",
                                    carry=carry(f"ffn_fwd{l}"))
        land(f"ffn_fwd{l}", landed)
        layers.append(p)
        saved.append(dict(x=h_in, proj=proj, oa=oa, ob=ob, sst=sst, tinv=tinv, x1=x1, pre2=pre2))
        h_in = x2


    small_names = conv_names + list(REPL)
    grads = {n: [None] * DEPTH for n in small_names}
    c_vec = jnp.stack([lax.axis_index("c")]).astype(jnp.int32)
    tags = ("w_in", "w_a", "w_b", "w_out", "w_up0", "w_up1", "w_dn0", "w_dn1")
    groups = (1, 1, 1, N_CHIPS, 2, 2, 2, 2)
    ab_cols = w["w_branch_a"].shape[2]
    pieces = [
        [(0, (k,), None) for k in range(N_CHIPS)],
        [(1, (0,), (k * ab_cols, ab_cols)) for k in range(N_CHIPS)],
        [(2, (0,), (k * ab_cols, ab_cols)) for k in range(N_CHIPS)],
        [(3, (k,), None) for k in range(N_CHIPS)],
        [(4 + k % 2, (k // 2,), None) for k in range(N_CHIPS)],
        [(6 + k // 2, (k % 2,), None) for k in range(N_CHIPS)],
    ]
    part_shapes = [(w[n].shape[1] // 2, w[n].shape[2]) for n in big_names]
    arrays_of = ((0,), (1,), (2,), (3,), (4, 5), (6, 7))
    rest, ffn_part = (0, 1, 2, 3), (4, 5)
    bufs = {}

    def arrays(which):
        return [i for t in which for i in arrays_of[t]]

    def views(which, arrs):
        return [a.reshape(groups[i], 2, a.size // a.shape[-1] // (2 * groups[i]), a.shape[-1])
                for i, a in zip(arrays(which), arrs)]

    def pair_sums(l, which, mine, theirs):
        ids = arrays(which)
        sums = [pair_add_half(m_, t_, c_vec, f"reduce_pair_add{l}_{tags[i]}") for i, m_, t_ in zip(ids, mine, theirs)]
        if ids[0] == 0:
            pin = sums[0][0]
            natural = jnp.concatenate([pin[:, :2048], pin[:, C_BA:C_BA + 8], pin[:, 2048:C_BA]], axis=1)
            sums[0] = jnp.stack(jnp.split(natural, N_CHIPS, axis=1))
        return sums

    def scatter_of(which, srcs):
        place = {i: j for j, i in enumerate(arrays(which))}
        return carried_scatter(srcs, [[(place[i], lead, cols) for i, lead, cols in pieces[t]] for t in which],
                               [part_shapes[t] for t in which])

    def chip_sums(l, which, recv):
        for t, r in zip(which, recv):
            n = big_names[t]
            bufs[n] = chips_add_into(r, bufs.get(n), l, c_vec, f"reduce_chips_add{l}_{n}")
        return [bufs[big_names[t]] for t in which]

    def keep(which, joined):
        bufs.update({big_names[t]: b for t, b in zip(which, joined)})

    above = None
    for l in reversed(range(DEPTH)):
        p, a = layers[l], saved[l]
        if l == DEPTH - 1:
            dpre2, dg2, db2, loss_part = loss_ln_bwd(a["pre2"], tgt, p["g2"], p["b2"], "loss_ln2_bwd")
            loss = lax.psum(loss_part[0, 0], ("x", "y", "c"))
        else:
            dpre2, dg2, db2 = ln_bwd(a["pre2"], dy, p["g2"], p["b2"], f"ln2_bwd{l}")
        dx1, dcf0, *half0 = ffn_bwd(
            a["x1"], dpre2, dpre2, ALPHA, got[l]["w_up"], p["cf"], got[l]["w_down"], 0, f"ffn_bwd{l}a",
            carry=carried_pair_swap(above) if above else None)
        theirs = half0[3:]
        srcs = pair_sums(l + 1, rest, above, theirs) if above else None
        dx1, dcf1, *half1 = ffn_bwd(
            a["x1"], dpre2, dx1, 1.0, got[l]["w_up"], p["cf"], got[l]["w_down"], 1, f"ffn_bwd{l}b",
            carry=scatter_of(rest, srcs) if above else None)
        recv = half1[3:]
        summed = chip_sums(l + 1, rest, recv) if above else None
        ffn_grads = []
        for h, (dua, dub, act) in enumerate((half0[:3], half1[:3])):
            dwup = wgrad_mm(a["x1"], dua, 2, 0, None, f"wgrad_up{l}{'ab'[h]}_a")
            ffn_grads.append(wgrad_mm(a["x1"], dub, 2, 1, dwup, f"wgrad_up{l}{'ab'[h]}_b"))
        for h, (dua, dub, act) in enumerate((half0[:3], half1[:3])):
            ffn_grads.append(wgrad_mm(act, dpre2, 1, 0, None, f"wgrad_down{l}{'ab'[h]}")[0])
        dwup0, dwup1, dwdn0, dwdn1 = ffn_grads
        doa, dob, dga, dgb, dxd, dg1, db1, h_mid, d_r, d_ya, d_yb, *joined = merge_bwd(
            a["oa"], a["ob"], a["proj"], a["x"], dx1, p["wa"], p["wb"], p["wo"], p["g1"], p["b1"], f"merge_bwd{l}",
            carry=carried_join(summed, l + 1) if above else None)
        keep(rest, joined)
        dwa = wgrad_mm(a["oa"], d_ya, 1, 0, None, f"wgrad_a{l}")[0]
        dwb = wgrad_mm(a["ob"], d_yb, 1, 0, None, f"wgrad_b{l}")[0]
        dwo = wgrad_mm(h_mid, d_r, 1, 0, None, f"wgrad_out{l}")[0]
        mine = views(ffn_part, [dwup0, dwup1, dwdn0, dwdn1])
        duv, dlng, dlnb, dws, dbs, *theirs = sg_bwd(a["proj"], dob, p["lng"], p["lnb"], p["w_s"], p["bs_t"], f"sg_bwd{l}",
                                                    carry=carried_pair_swap(mine))
        srcs = pair_sums(l, ffn_part, mine, theirs)
        dqkv, dz, dba, dcq, da, ddtb, dnw, *recv = dn_bwd(
            a["proj"], a["sst"], a["tinv"], doa, p["cq"], p["a_row"], p["dtb_row"], p["nw_row"], f"dn_bwd{l}",
            carry=scatter_of(ffn_part, srcs))
        summed = chip_sums(l, ffn_part, recv)
        dy, *joined = proj_bwd([dqkv, dz, duv, dga, dgb, dba], dxd, p["w_in"], f"proj_bwd{l}",
                               carry=carried_join(summed, l))
        keep(ffn_part, joined)
        dwi = None
        for tag, dp, col in (("qkv", dqkv, 0), ("z", dz, C_Z), ("uv", duv, C_UV), ("ga", dga, C_GA), ("gb", dgb, C_GB),
                             ("ba", dba, C_BA)):
            dwi = wgrad(a["x"], dp, col, dwi, f"wgrad_in{l}_{tag}")

        above = views(rest, [dwi, dwa, dwb, dwo])
        grads["conv_qkv"][l] = dcq
        grads["conv_ffn"][l] = jnp.concatenate([dcf0[0], dcf1[0], dcf0[1], dcf1[1]], axis=1)
        grads["a_log"][l] = da[0, HEADS:2 * HEADS]
        grads["dt_bias"][l] = ddtb[0, HEADS:2 * HEADS]
        grads["dn_norm_w"][l] = dnw[0]
        grads["sg_ln_g"][l] = dlng[0]
        grads["sg_ln_b"][l] = dlnb[0]
        grads["w_spatial"][l] = dws
        grads["b_spatial"][l] = dbs[:, :4].T
        grads["ln1_g"][l] = dg1[0]
        grads["ln1_b"][l] = db1[0]
        grads["ln2_g"][l] = dg2[0]
        grads["ln2_b"][l] = db2[0]
    grad_x = dy.reshape(x.shape)
    g_full = {n: jnp.stack(grads[n]) for n in small_names}

    theirs = exchange(carried_pair_swap(above), "reduce_pair")
    *recv, small = exchange(carried_both(scatter_of(rest, pair_sums(0, rest, above, theirs)),
                                         carried_allsum(_pack([g_full[n] for n in small_names], 8))), "reduce_chips")
    keep(rest, exchange(carried_join(chip_sums(0, rest, recv), 0), "reduce_join"))
    g_shard = {n: bufs[n].reshape(w[n].shape) for n in big_names}

    small_full = dict(zip(small_names, _unpack(small.reshape(-1), [g_full[n].shape for n in small_names])))
    for n in conv_names:
        width = w[n].shape[2]
        g_shard[n] = lax.dynamic_slice_in_dim(small_full[n], chip * width, width, axis=2)
    for n in REPL:
        g_shard[n] = small_full[n]

    delta, new_m, new_v = {}, {}, {}
    for n in big_names:
        shp = w[n].shape
        two_d = (shp[0] * shp[1], shp[2])
        g_, d_, m_, v_ = adam_call(w[n].reshape(two_d), g_shard[n].reshape(two_d), m[n].reshape(two_d), v[n].reshape(two_d), f"adam_{n}")
        g_shard[n], delta[n], new_m[n], new_v[n] = g_.reshape(shp), d_.reshape(shp), m_.reshape(shp), v_.reshape(shp)
    shapes = [w[n].shape for n in small_names]
    packs = [_pack([src[n] for n in small_names], 8) for src in (w, g_shard, m, v)]
    outs = adam_call(*packs, "adam_small")
    for dst, o in zip((delta, new_m, new_v), outs[1:]):
        dst.update(zip(small_names, _unpack(o.reshape(-1), shapes)))

    return (loss, grad_x, *[g_shard[n] for n in names], *[delta[n] for n in names],
            *[new_m[n] for n in names], *[new_v[n] for n in names])
```

```python
import functools
import math
from typing import Callable, NamedTuple

import jax
import jax.numpy as jnp
from jax import lax
from jax.experimental import pallas as pl
from jax.experimental.pallas import tpu as pltpu

F32 = jnp.float32
BF16 = jnp.bfloat16
HI = lax.Precision.HIGHEST
MID = lax.Precision.HIGH
MESH = pl.DeviceIdType.MESH

D = 1024
DEPTH = 2
HEADS = 4
DK = 128
CHUNK = 64
QKV_W = 1536
Z_W = 512
SG_W = 512
FFN = 2816
FFN_HALF = FFN // 2
N_CHIPS = 4
DN_SHARD = FFN // N_CHIPS
LN_EPS = 1e-5
RMS_EPS = 1e-6
L2_EPS = 1e-6
ALPHA = (2 * DEPTH) ** 0.25
ADAM_LR, ADAM_B1, ADAM_B2, ADAM_EPS, ADAM_WD, ADAM_STEP = 0.001, 0.9, 0.999, 1e-08, 0.01, 10

HALO = 16
LANES = 128
IN_COLS_PAD = 5248
C_Z, C_UV, C_GA, C_GB, C_BA = 1536, 2048, 3072, 4096, 5120
VMEM_LIMIT = 56 * 1024 * 1024
WGRAD_TOKENS = 2048
DN_TILE = 256


def _params(n_grid=1):
    return pltpu.CompilerParams(dimension_semantics=("arbitrary",) * n_grid, vmem_limit_bytes=VMEM_LIMIT)


def _mm(a, b):
    return jnp.dot(a.astype(BF16), b.astype(BF16), preferred_element_type=F32)


def _mm_nt(a, b):
    return lax.dot_general(a.astype(BF16), b.astype(BF16), (((1,), (1,)), ((), ())), preferred_element_type=F32)


def _mm_tn(a, b):
    return lax.dot_general(a.astype(BF16), b.astype(BF16), (((0,), (0,)), ((), ())), preferred_element_type=F32)


def _bdot(a, b, prec=MID):
    return lax.dot_general(a, b, (((2,), (1,)), ((0,), (0,))), precision=prec, preferred_element_type=F32)


def _bdot_nt(a, b, prec=MID):
    return lax.dot_general(a, b, (((2,), (2,)), ((0,), (0,))), precision=prec, preferred_element_type=F32)


def _bf16_dot(a, b, contract):
    return lax.dot_general(a.astype(BF16), b.astype(BF16), (contract, ((0,), (0,))), preferred_element_type=F32)


@jax.custom_vjp
def _fdot(a, b):
    return _bf16_dot(a, b, ((2,), (1,)))


def _fdot_fwd(a, b):
    return _fdot(a, b), (a, b)


def _fdot_bwd(res, ct):
    a, b = res
    return _bf16_dot(ct, b, ((2,), (2,))), _bf16_dot(a, ct, ((1,), (1,)))


_fdot.defvjp(_fdot_fwd, _fdot_bwd)


@jax.custom_vjp
def _fdot_nt(a, b):
    return _bf16_dot(a, b, ((2,), (2,)))


def _fdot_nt_fwd(a, b):
    return _fdot_nt(a, b), (a, b)


def _fdot_nt_bwd(res, ct):
    a, b = res
    return _bf16_dot(ct, b, ((2,), (1,))), _bf16_dot(ct, a, ((1,), (1,)))


_fdot_nt.defvjp(_fdot_nt_fwd, _fdot_nt_bwd)


@jax.custom_vjp
def _fdot_tn(a, b):
    return _bf16_dot(a, b, ((1,), (1,)))


def _fdot_tn_fwd(a, b):
    return _fdot_tn(a, b), (a, b)


def _fdot_tn_bwd(res, ct):
    a, b = res
    return _bf16_dot(b, ct, ((2,), (2,))), _bf16_dot(a, ct, ((2,), (1,)))


_fdot_tn.defvjp(_fdot_tn_fwd, _fdot_tn_bwd)


def _stack(parts):
    return jnp.concatenate([p[None] for p in parts], axis=0)


def _ln(x, g, b):
    mu = jnp.mean(x, axis=-1, keepdims=True)
    xc = x - mu
    var = jnp.mean(xc * xc, axis=-1, keepdims=True)
    return xc * lax.rsqrt(var + LN_EPS) * g + b


def _shift_rows(x, s):
    s = s % x.shape[0]
    return x if s == 0 else pltpu.roll(x, s, 0)


@jax.custom_vjp
def _conv(xcat, w):
    k_taps = len(w)
    y = None
    for k in range(k_taps):
        t = _shift_rows(xcat, k_taps - 1 - k)[HALO:] * w[k]
        y = t if y is None else y + t
    return y


def _conv_fwd(xcat, w):
    return _conv(xcat, w), (xcat, w)


def _conv_bwd(res, dy):
    xcat, w = res
    k_taps = len(w)
    dyp = jnp.concatenate([jnp.zeros((HALO, dy.shape[1]), dy.dtype), dy], axis=0)
    dx = None
    dws = []
    for k in range(k_taps):
        shifted = _shift_rows(dyp, -(k_taps - 1 - k))
        t = shifted * w[k]
        dx = t if dx is None else dx + t
        dws.append(jnp.sum(shifted * xcat, axis=0, keepdims=True))
    return dx, tuple(dws)


_conv.defvjp(_conv_fwd, _conv_bwd)


@jax.custom_vjp
def _tri_inv(l):
    n = l.shape[-1]
    r = lax.broadcasted_iota(jnp.int32, (n, n), 0)
    c = lax.broadcasted_iota(jnp.int32, (n, n), 1)
    eye = (r == c).astype(F32)
    p = eye - l
    lp = l
    steps = int(math.log2(n)) - 1
    for i in range(steps):
        dot = _bdot if i < 2 else functools.partial(_bf16_dot, contract=((2,), (1,)))
        lp = dot(lp, lp)
        p = p + dot(p, lp)
    return p


def _tri_inv_fwd(l):
    t = _tri_inv(l)
    return t, t


def _tri_inv_bwd(t, dt):
    tt = jnp.swapaxes(t, 1, 2)
    return (-_bdot(tt, _bdot(dt, tt)),)


_tri_inv.defvjp(_tri_inv_fwd, _tri_inv_bwd)


@jax.custom_vjp
def _tri_inv_saved(l, t):
    return t


def _tri_inv_saved_fwd(l, t):
    return t, t


def _tri_inv_saved_bwd(t, dt):
    return _tri_inv_bwd(t, dt) + (jnp.zeros_like(t),)


_tri_inv_saved.defvjp(_tri_inv_saved_fwd, _tri_inv_saved_bwd)


def _dn_glue(qkvcat, z, ba, s_in, cw, a_row, dtb_row, nw_row, t_saved=None):
    t_rows = z.shape[0]
    nc = t_rows // CHUNK
    nb = nc * HEADS

    qkv = jax.nn.silu(_conv(qkvcat, cw))

    def chunks(t, off):
        return _stack([t[n * CHUNK:(n + 1) * CHUNK, off + h * DK: off + (h + 1) * DK]
                       for n in range(nc) for h in range(HEADS)])

    q = chunks(qkv, 0)
    k = chunks(qkv, 512)
    v = chunks(qkv, 1024)
    q = q * lax.rsqrt(jnp.sum(q * q, axis=-1, keepdims=True) + L2_EPS) * (DK ** -0.5)
    k = k * lax.rsqrt(jnp.sum(k * k, axis=-1, keepdims=True) + L2_EPS)

    lane = lax.broadcasted_iota(jnp.int32, (LANES, HEADS * DK), 0)
    head_of_col = lax.broadcasted_iota(jnp.int32, (LANES, HEADS * DK), 1) // DK
    e_beta = (head_of_col == lane).astype(F32)
    e_g = (head_of_col + HEADS == lane).astype(F32)
    beta_l = jax.nn.sigmoid(ba)
    g_l = -jnp.exp(a_row) * jax.nn.softplus(ba + dtb_row)
    beta = chunks(jnp.dot(beta_l, e_beta, precision=MID, preferred_element_type=F32), 0)
    g = chunks(jnp.dot(g_l, e_g, precision=MID, preferred_element_type=F32), 0)

    r = lax.broadcasted_iota(jnp.int32, (CHUNK, CHUNK), 0)
    c = lax.broadcasted_iota(jnp.int32, (CHUNK, CHUNK), 1)
    causal = r >= c
    strict = r > c
    tril_b = jnp.broadcast_to(causal.astype(F32), (nb, CHUNK, CHUNK))
    gi_b = _bdot(tril_b, g, HI)
    gi = gi_b[:, :, :CHUNK]
    gj = jnp.swapaxes(gi, 1, 2)
    decay = jnp.where(causal, jnp.exp(jnp.where(causal, gi - gj, 0.0)), 0.0)
    kb = k * beta
    l_mat = jnp.where(strict, _fdot_nt(kb, k) * decay, 0.0)
    t_mat = _tri_inv(l_mat) if t_saved is None else _tri_inv_saved(l_mat, t_saved)
    e_gi = jnp.exp(gi_b)
    w_mat = _fdot(t_mat, kb * e_gi)
    u_mat = _fdot(t_mat, v * beta)
    a_qk = _fdot_nt(q, k) * decay
    q_g = q * e_gi
    gl_b = jnp.broadcast_to(jnp.sum(g, axis=1, keepdims=True), g.shape)
    k_d = k * jnp.exp(gl_b - gi_b)
    e_gl = jnp.exp(gl_b)
    g_last = jnp.concatenate([e_gl, e_gl], axis=1)

    state = s_in
    rows = []
    for n in range(nc):
        sl = slice(n * HEADS, (n + 1) * HEADS)
        u_new = u_mat[sl] - _fdot(w_mat[sl], state)
        o_n = _fdot(q_g[sl], state) + _fdot(a_qk[sl], u_new)
        state = state * g_last[sl] + _fdot_tn(k_d[sl], u_new)
        o_n = o_n * lax.rsqrt(jnp.mean(o_n * o_n, axis=-1, keepdims=True) + RMS_EPS) * nw_row
        z_n = _stack([z[n * CHUNK:(n + 1) * CHUNK, h * DK:(h + 1) * DK] for h in range(HEADS)])
        o_n = o_n * jax.nn.silu(z_n)
        rows.append(jnp.concatenate([o_n[h] for h in range(HEADS)], axis=-1))
    return jnp.concatenate(rows, axis=0), state, t_mat


def _sg_glue(uv, lng, lnb, w_s, bs_t):
    t_rows = uv.shape[0]
    y = jax.nn.gelu(uv)
    u = y[:, :SG_W]
    v = _ln(y[:, SG_W:], lng, lnb)
    r = lax.broadcasted_iota(jnp.int32, (LANES, LANES), 0)
    c = lax.broadcasted_iota(jnp.int32, (LANES, LANES), 1)
    wm = jnp.where(r >= c, w_s, 0.0)
    lane = lax.broadcasted_iota(jnp.int32, (LANES, SG_W), 0)
    group_of_col = lax.broadcasted_iota(jnp.int32, (LANES, SG_W), 1) // LANES
    e_grp = (group_of_col == lane).astype(F32)
    bias = jnp.dot(bs_t, e_grp, precision=HI, preferred_element_type=F32)
    outs = []
    for n in range(t_rows // LANES):
        vb = v[n * LANES:(n + 1) * LANES]
        vg = _stack([vb[:, g * LANES:(g + 1) * LANES] for g in range(4)])
        mg = _fdot(wm, vg)
        mixed = jnp.concatenate([mg[g] for g in range(4)], axis=-1) + bias
        outs.append(u[n * LANES:(n + 1) * LANES] * mixed)
    return jnp.concatenate(outs, axis=0)


def _merge_glue(ga, gb, ya, yb):
    return jax.nn.sigmoid(ga) * ya + jax.nn.sigmoid(gb) * yb


def _res_ln_glue(x, r, g, b):
    return _ln(ALPHA * x + r, g, b)


def _ffn_glue(ua, ub, cwa, cwb):
    return jax.nn.silu(_conv(ua, cwa)) * _conv(ub, cwb)


def _row(t, c, col=0):
    return pl.BlockSpec((t, c), lambda i: (i, col))


def _row_rev(t, c, nt, col=0):
    return pl.BlockSpec((t, c), lambda i: (nt - 1 - i, col))


def _halo(t, c, nt=None):
    per = t // HALO
    if nt is None:
        return pl.BlockSpec((HALO, c), lambda i: (jnp.maximum(i * per - 1, 0), 0))
    return pl.BlockSpec((HALO, c), lambda i: (jnp.maximum((nt - 1 - i) * per - 1, 0), 0))


def _full(shape):
    nd = len(shape)
    return pl.BlockSpec(shape, lambda i: (0,) * nd)


ANY = pl.BlockSpec(memory_space=pl.ANY)


def _sds(shape, dtype=F32):
    return jax.ShapeDtypeStruct(shape, dtype)


def _tile(s, want=256):
    for t in (want, 256, 128):
        if s % t == 0:
            return t
    raise ValueError(f"sequence length {s} is not a multiple of 128")


def proj_fwd(x, w, name, carry=None):
    s = x.shape[0]
    t = _tile(s, 512)
    nt = s // t
    segs = [(0, 2048), (2048, 3072), (3072, 4096), (4096, 5120), (5120, IN_COLS_PAD)]

    def body(x_ref, w_ref, p_ref):
        xb = x_ref[...].astype(BF16)
        for lo, hi in segs:
            p_ref[:, lo:hi] = jnp.dot(xb, w_ref[:, lo:hi], preferred_element_type=F32)

    return _host_call(
        body, carry, nt, grid=(nt,), name=name, in_specs=[_row(t, D), _full((D, IN_COLS_PAD))],
        out_specs=[_row(t, IN_COLS_PAD)], out_shape=[_sds((s, IN_COLS_PAD))], scratch_shapes=[], operands=(x, w))


def dn_fwd(p, cq, a_row, dtb_row, nw_row, name, carry=None):
    s = p.shape[0]
    t = _tile(s, DN_TILE)
    nt = s // t

    nb = HEADS * t // CHUNK

    def body(qkv_ref, halo_ref, z_ref, ba_ref, cq_ref, a_ref, dtb_ref, nw_ref, o_ref, sst_ref, tinv_ref, s_scr):
        i = pl.program_id(0)

        @pl.when(i == 0)
        def _():
            s_scr[...] = jnp.zeros_like(s_scr)

        halo = jnp.where(i == 0, 0.0, halo_ref[...])
        qkvcat = jnp.concatenate([halo, qkv_ref[...]], axis=0)
        cw = tuple(cq_ref[k:k + 1, :] for k in range(4))
        s_in = s_scr[...]
        sst_ref[0] = s_in
        o, s_out, t_mat = _dn_glue(qkvcat, z_ref[...], ba_ref[...], s_in, cw, a_ref[...], dtb_ref[...], nw_ref[...])
        o_ref[...] = o.astype(BF16)
        tinv_ref[0] = t_mat
        s_scr[...] = s_out

    return _host_call(
        body, carry, nt, grid=(nt,), name=name,
        in_specs=[_row(t, QKV_W), _halo(t, QKV_W), _row(t, Z_W, C_Z // Z_W), _row(t, LANES, C_BA // LANES),
                  _full((4, QKV_W)), _full((1, LANES)), _full((1, LANES)), _full((1, LANES))],
        out_specs=[_row(t, Z_W), pl.BlockSpec((1, HEADS, DK, DK), lambda i: (i, 0, 0, 0)),
                   pl.BlockSpec((1, nb, CHUNK, CHUNK), lambda i: (i, 0, 0, 0))],
        out_shape=[_sds((s, Z_W), BF16), _sds((nt, HEADS, DK, DK)), _sds((nt, nb, CHUNK, CHUNK))],
        scratch_shapes=[pltpu.VMEM((HEADS, DK, DK), F32)], operands=(p, p, p, p, cq, a_row, dtb_row, nw_row))


def sg_fwd(p, lng, lnb, w_s, bs_t, name):
    s = p.shape[0]
    t = _tile(s, 512)

    def body(uv_ref, lng_ref, lnb_ref, ws_ref, bs_ref, o_ref):
        o_ref[...] = _sg_glue(uv_ref[...], lng_ref[...], lnb_ref[...], ws_ref[...], bs_ref[...]).astype(BF16)

    return pl.pallas_call(
        body, grid=(s // t,), name=name,
        in_specs=[_row(t, 2 * SG_W, C_UV // (2 * SG_W)), _full((1, SG_W)), _full((1, SG_W)),
                  _full((4, LANES, LANES)), _full((LANES, LANES))],
        out_specs=_row(t, SG_W), out_shape=_sds((s, SG_W), BF16), compiler_params=_params())(p, lng, lnb, w_s, bs_t)


def merge_fwd(oa, ob, p, x, wa, wb, wo, g1, b1, name):
    s = x.shape[0]
    t = _tile(s, 512)

    def body(oa_ref, ob_ref, ga_ref, gb_ref, x_ref, wa_ref, wb_ref, wo_ref, g_ref, b_ref, x1_ref):
        ya = _mm(oa_ref[...], wa_ref[...])
        yb = _mm(ob_ref[...], wb_ref[...])
        h = _merge_glue(ga_ref[...], gb_ref[...], ya, yb)
        x1_ref[...] = _res_ln_glue(x_ref[...], _mm(h, wo_ref[...]), g_ref[...], b_ref[...])

    return pl.pallas_call(
        body, grid=(s // t,), name=name,
        in_specs=[_row(t, Z_W), _row(t, SG_W), _row(t, D, C_GA // D), _row(t, D, C_GB // D), _row(t, D),
                  _full((Z_W, D)), _full((SG_W, D)), _full((D, D)), _full((1, D)), _full((1, D))],
        out_specs=_row(t, D), out_shape=_sds((s, D)), compiler_params=_params())(oa, ob, p, p, x, wa, wb, wo, g1, b1)


def _load_ffn_weights(wup_hbm, wdn_hbm, wup_v, wdn_v, up_slots, dn_slots):
    for n, k in enumerate(up_slots):
        pltpu.sync_copy(wup_hbm.at[k], wup_v.at[n])
    for n, k in enumerate(dn_slots):
        pltpu.sync_copy(wdn_hbm.at[k], wdn_v.at[pl.ds(n * DN_SHARD, DN_SHARD)])


def ffn_fwd(x1, wup4, cf4, wdn4, g2, b2, name, carry=None):
    s = x1.shape[0]
    t = _tile(s, 512)
    nt = s // t

    def body(x1_ref, halo_ref, wup_hbm, cf_ref, wdn_hbm, g_ref, b_ref, pre_ref, x2_ref, wup_v, wdn_v):
        i = pl.program_id(0)

        @pl.when(i == 0)
        def _():
            _load_ffn_weights(wup_hbm, wdn_hbm, wup_v, wdn_v, range(4), range(4))

        x1v = x1_ref[...]
        halo = jnp.where(i == 0, 0.0, halo_ref[...])
        x1cat = jnp.concatenate([halo, x1v], axis=0).astype(BF16)
        f = None
        for h in range(2):
            ua = jnp.dot(x1cat, wup_v[h], preferred_element_type=F32)
            ub = jnp.dot(x1cat, wup_v[2 + h], preferred_element_type=F32)
            cwa = tuple(cf_ref[h, k:k + 1, :] for k in range(3))
            cwb = tuple(cf_ref[2 + h, k:k + 1, :] for k in range(3))
            act = _ffn_glue(ua, ub, cwa, cwb)
            fh = _mm(act, wdn_v[h * FFN_HALF:(h + 1) * FFN_HALF, :])
            f = fh if f is None else f + fh
        pre = ALPHA * x1v + f
        pre_ref[...] = pre
        x2_ref[...] = _ln(pre, g_ref[...], b_ref[...])

    return _host_call(
        body, carry, nt, grid=(nt,), name=name,
        in_specs=[_row(t, D), _halo(t, D), ANY, _full((4, 3, FFN_HALF)), ANY, _full((1, D)), _full((1, D))],
        out_specs=[_row(t, D), _row(t, D)], out_shape=[_sds((s, D)), _sds((s, D))],
        scratch_shapes=[pltpu.VMEM((4, D, FFN_HALF), BF16), pltpu.VMEM((FFN, D), BF16)],
        operands=(x1, x1, wup4, cf4, wdn4, g2, b2))


def loss_ln_bwd(pre, tgt, g, b, name):
    s = pre.shape[0]
    t = _tile(s, 512)

    def body(pre_ref, t_ref, g_ref, b_ref, dpre_ref, dg_ref, db_ref, loss_ref):
        first = pl.program_id(0) == 0
        y, vjp = jax.vjp(_ln, pre_ref[...], g_ref[...], b_ref[...])
        e = y - t_ref[...]
        dpre, dg, db = vjp(e * (1.0 / D))
        dpre_ref[...] = dpre
        _acc(dg_ref, dg, first)
        _acc(db_ref, db, first)
        part = jnp.sum(jnp.sum(e * e, axis=1, keepdims=True), axis=0, keepdims=True) * (0.5 / D)
        _acc(loss_ref, jnp.broadcast_to(part, loss_ref.shape), first)

    return pl.pallas_call(
        body, grid=(s // t,), name=name, in_specs=[_row(t, D), _row(t, D), _full((1, D)), _full((1, D))],
        out_specs=[_row(t, D), _full((1, D)), _full((1, D)), _full((8, LANES))],
        out_shape=[_sds((s, D)), _sds((1, D)), _sds((1, D)), _sds((8, LANES))], compiler_params=_params())(pre, tgt, g, b)


def _acc(ref, val, first):
    @pl.when(first)
    def _():
        ref[...] = val

    @pl.when(jnp.logical_not(first))
    def _():
        ref[...] += val


def _acc_tn(acc_ref, a, b, first, seg):
    n = b.shape[1]
    for lo in range(0, n, seg):
        hi = min(lo + seg, n)
        _acc(acc_ref.at[:, lo:hi], _mm_tn(a, b[:, lo:hi]), first)


def ln_bwd(pre, dy, g, b, name):
    s = pre.shape[0]
    t = _tile(s, 512)

    def body(pre_ref, dy_ref, g_ref, b_ref, dpre_ref, dg_ref, db_ref):
        _, vjp = jax.vjp(_ln, pre_ref[...], g_ref[...], b_ref[...])
        dpre, dg, db = vjp(dy_ref[...])
        dpre_ref[...] = dpre
        first = pl.program_id(0) == 0
        _acc(dg_ref, dg, first)
        _acc(db_ref, db, first)

    return pl.pallas_call(
        body, grid=(s // t,), name=name, in_specs=[_row(t, D), _row(t, D), _full((1, D)), _full((1, D))],
        out_specs=[_row(t, D), _full((1, D)), _full((1, D))],
        out_shape=[_sds((s, D)), _sds((1, D)), _sds((1, D))], compiler_params=_params())(pre, dy, g, b)


def ffn_bwd(x1, df, acc_in, acc_scale, wup4, cf4, wdn4, h, name, carry=None):
    s = x1.shape[0]
    t = _tile(s, 512)
    nt = s // t

    def body(x1_ref, halo_ref, df_ref, acc_ref, wup_hbm, cf_ref, wdn_hbm,
             dx1_ref, dcf_ref, dua_ref, dub_ref, act_ref, wup_v, wdn_v, carry):
        i = pl.program_id(0)
        j = nt - 1 - i
        first = i == 0

        @pl.when(first)
        def _():
            _load_ffn_weights(wup_hbm, wdn_hbm, wup_v, wdn_v, (h, 2 + h), (2 * h, 2 * h + 1))
            carry[...] = jnp.zeros_like(carry)

        halo = jnp.where(j == 0, 0.0, halo_ref[...])
        x1cat = jnp.concatenate([halo, x1_ref[...]], axis=0).astype(BF16)
        ua = jnp.dot(x1cat, wup_v[0], preferred_element_type=F32)
        ub = jnp.dot(x1cat, wup_v[1], preferred_element_type=F32)
        cwa = tuple(cf_ref[h, k:k + 1, :] for k in range(3))
        cwb = tuple(cf_ref[2 + h, k:k + 1, :] for k in range(3))
        act, vjp = jax.vjp(_ffn_glue, ua, ub, cwa, cwb)
        act_ref[...] = act.astype(BF16)
        dact = _mm_nt(df_ref[...], wdn_v[...])
        dua, dub, dcwa, dcwb = vjp(dact)
        dups = []
        for n, (du, out_ref) in enumerate(((dua, dua_ref), (dub, dub_ref))):
            dups.append(jnp.concatenate([du[HALO:t], du[t:] + carry[n]], axis=0).astype(BF16))
            carry[n] = du[:HALO]
            out_ref[...] = dups[n]
        for k in range(3):
            _acc(dcf_ref.at[0, k:k + 1, :], dcwa[k], first)
            _acc(dcf_ref.at[1, k:k + 1, :], dcwb[k], first)
        dx1_ref[...] = acc_scale * acc_ref[...] + _mm_nt(dups[0], wup_v[0]) + _mm_nt(dups[1], wup_v[1])

    hidden = _row_rev(t, FFN_HALF, nt)
    return _host_call(
        body, carry, nt, grid=(nt,), name=name,
        in_specs=[_row_rev(t, D, nt), _halo(t, D, nt), _row_rev(t, D, nt), _row_rev(t, D, nt),
                  ANY, _full((4, 3, FFN_HALF)), ANY],
        out_specs=[_row_rev(t, D, nt), _full((2, 3, FFN_HALF)), hidden, hidden, hidden],
        out_shape=[_sds((s, D)), _sds((2, 3, FFN_HALF))] + [_sds((s, FFN_HALF), BF16)] * 3,
        scratch_shapes=[pltpu.VMEM((2, D, FFN_HALF), BF16), pltpu.VMEM((FFN_HALF, D), BF16),
                        pltpu.VMEM((2, HALO, FFN_HALF), F32)],
        operands=(x1, x1, df, acc_in, wup4, cf4, wdn4))


def wgrad_mm(a, b, slots, slot, into, name):
    s, m = a.shape
    n = b.shape[1]
    tk = _tile(s, WGRAD_TOKENS)

    def body(a_ref, b_ref, *rest):
        _acc(rest[-1], _mm_tn(a_ref[...], b_ref[...]), pl.program_id(0) == 0)

    return pl.pallas_call(
        body, grid=(s // tk,), name=name,
        in_specs=[pl.BlockSpec((tk, m), lambda k: (k, 0)), pl.BlockSpec((tk, n), lambda k: (k, 0))]
        + ([] if into is None else [ANY]),
        out_specs=pl.BlockSpec((None, m, n), lambda k: (slot, 0, 0)), out_shape=_sds((slots, m, n)),
        input_output_aliases={} if into is None else {2: 0},
        compiler_params=_params())(*((a, b) if into is None else (a, b, into)))


def merge_bwd(oa, ob, p, x, dx1, wa, wb, wo, g1, b1, name, carry=None):
    s = x.shape[0]
    t = _tile(s)

    def body(oa_ref, ob_ref, ga_ref, gb_ref, x_ref, dx1_ref, wa_ref, wb_ref, wo_ref, g_ref, b_ref,
             doa_ref, dob_ref, dga_ref, dgb_ref, dx_ref, dg_ref, db_ref, h_ref, dr_ref, dya_ref, dyb_ref):
        first = pl.program_id(0) == 0
        oa = oa_ref[...]
        ob = ob_ref[...]
        ya = _mm(oa, wa_ref[...])
        yb = _mm(ob, wb_ref[...])
        h, vjp1 = jax.vjp(_merge_glue, ga_ref[...], gb_ref[...], ya, yb)
        hb = h.astype(BF16)
        r = _mm(hb, wo_ref[...])
        _, vjp2 = jax.vjp(_res_ln_glue, x_ref[...], r, g_ref[...], b_ref[...])
        dx, dr, dg, db = vjp2(dx1_ref[...])
        dx_ref[...] = dx
        _acc(dg_ref, dg, first)
        _acc(db_ref, db, first)
        drb = dr.astype(BF16)
        h_ref[...] = hb
        dr_ref[...] = drb
        dh = _mm_nt(drb, wo_ref[...])
        dga, dgb, dya, dyb = vjp1(dh)
        dga_ref[...] = dga.astype(BF16)
        dgb_ref[...] = dgb.astype(BF16)
        dyab = dya.astype(BF16)
        dybb = dyb.astype(BF16)
        dya_ref[...] = dyab
        dyb_ref[...] = dybb
        doa_ref[...] = _mm_nt(dyab, wa_ref[...]).astype(BF16)
        dob_ref[...] = _mm_nt(dybb, wb_ref[...]).astype(BF16)

    wide = _row(t, D)
    return _host_call(
        body, carry, s // t, grid=(s // t,), name=name,
        in_specs=[_row(t, Z_W), _row(t, SG_W), _row(t, D, C_GA // D), _row(t, D, C_GB // D), _row(t, D), _row(t, D),
                  _full((Z_W, D)), _full((SG_W, D)), _full((D, D)), _full((1, D)), _full((1, D))],
        out_specs=[_row(t, Z_W), _row(t, SG_W), wide, wide, wide, _full((1, D)), _full((1, D)), wide, wide, wide, wide],
        out_shape=[_sds((s, Z_W), BF16), _sds((s, SG_W), BF16), _sds((s, D), BF16), _sds((s, D), BF16), _sds((s, D)),
                   _sds((1, D)), _sds((1, D))] + [_sds((s, D), BF16)] * 4,
        scratch_shapes=[], operands=(oa, ob, p, p, x, dx1, wa, wb, wo, g1, b1))


def sg_bwd(p, dob, lng, lnb, w_s, bs_t, name, carry=None):
    s = p.shape[0]
    t = _tile(s, 512)

    def body(uv_ref, dob_ref, lng_ref, lnb_ref, ws_ref, bs_ref, duv_ref, dlng_ref, dlnb_ref, dws_ref, dbs_ref):
        first = pl.program_id(0) == 0
        _, vjp = jax.vjp(_sg_glue, uv_ref[...], lng_ref[...], lnb_ref[...], ws_ref[...], bs_ref[...])
        duv, dlng, dlnb, dws, dbs = vjp(dob_ref[...].astype(F32))
        duv_ref[...] = duv.astype(BF16)
        _acc(dlng_ref, dlng, first)
        _acc(dlnb_ref, dlnb, first)
        _acc(dws_ref, dws, first)
        _acc(dbs_ref, dbs, first)

    return _host_call(
        body, carry, s // t, grid=(s // t,), name=name,
        in_specs=[_row(t, 2 * SG_W, C_UV // (2 * SG_W)), _row(t, SG_W), _full((1, SG_W)), _full((1, SG_W)),
                  _full((4, LANES, LANES)), _full((LANES, LANES))],
        out_specs=[_row(t, 2 * SG_W), _full((1, SG_W)), _full((1, SG_W)), _full((4, LANES, LANES)), _full((LANES, LANES))],
        out_shape=[_sds((s, 2 * SG_W), BF16), _sds((1, SG_W)), _sds((1, SG_W)), _sds((4, LANES, LANES)), _sds((LANES, LANES))],
        scratch_shapes=[], operands=(p, dob, lng, lnb, w_s, bs_t))


def dn_bwd(p, sst, tinv, doa, cq, a_row, dtb_row, nw_row, name, carry=None):
    s = p.shape[0]
    t = _tile(s, DN_TILE)
    nt = s // t

    def body(qkv_ref, halo_ref, z_ref, ba_ref, sst_ref, tinv_ref, doa_ref, cq_ref, a_ref, dtb_ref, nw_ref,
             dqkv_ref, dz_ref, dba_ref, dcq_ref, da_ref, ddtb_ref, dnw_ref, ds_scr, carry):
        i = pl.program_id(0)
        j = nt - 1 - i
        first = i == 0

        @pl.when(first)
        def _():
            ds_scr[...] = jnp.zeros_like(ds_scr)
            carry[...] = jnp.zeros_like(carry)

        halo = jnp.where(j == 0, 0.0, halo_ref[...])
        qkvcat = jnp.concatenate([halo, qkv_ref[...]], axis=0)
        cw = tuple(cq_ref[k:k + 1, :] for k in range(4))
        t_saved = tinv_ref[0]
        _, vjp = jax.vjp(lambda *args: _dn_glue(*args, t_saved=t_saved)[:2],
                         qkvcat, z_ref[...], ba_ref[...], sst_ref[0], cw, a_ref[...], dtb_ref[...], nw_ref[...])
        dqkvcat, dz, dba, ds_in, dcw, da, ddtb, dnw = vjp((doa_ref[...].astype(F32), ds_scr[...]))
        ds_scr[...] = ds_in
        dz_ref[...] = dz.astype(BF16)
        dba_ref[...] = dba.astype(BF16)
        dtile = dqkvcat[HALO:]
        dqkv_ref[...] = dtile.astype(BF16)
        dqkv_ref[t - HALO:t, :] = (dtile[t - HALO:] + carry[...]).astype(BF16)
        carry[...] = dqkvcat[:HALO]
        for k in range(4):
            _acc(dcq_ref.at[k:k + 1, :], dcw[k], first)
        _acc(da_ref, da, first)
        _acc(ddtb_ref, ddtb, first)
        _acc(dnw_ref, dnw, first)

    return _host_call(
        body, carry, nt, grid=(nt,), name=name,
        in_specs=[_row_rev(t, QKV_W, nt), _halo(t, QKV_W, nt), _row_rev(t, Z_W, nt, C_Z // Z_W),
                  _row_rev(t, LANES, nt, C_BA // LANES),
                  pl.BlockSpec((1, HEADS, DK, DK), lambda i: (nt - 1 - i, 0, 0, 0)),
                  pl.BlockSpec((1,) + tinv.shape[1:], lambda i: (nt - 1 - i, 0, 0, 0)), _row_rev(t, Z_W, nt),
                  _full((4, QKV_W)), _full((1, LANES)), _full((1, LANES)), _full((1, LANES))],
        out_specs=[_row_rev(t, QKV_W, nt), _row_rev(t, Z_W, nt), _row_rev(t, LANES, nt),
                   _full((4, QKV_W)), _full((1, LANES)), _full((1, LANES)), _full((1, LANES))],
        out_shape=[_sds((s, QKV_W), BF16), _sds((s, Z_W), BF16), _sds((s, LANES), BF16),
                   _sds((4, QKV_W)), _sds((1, LANES)), _sds((1, LANES)), _sds((1, LANES))],
        scratch_shapes=[pltpu.VMEM((HEADS, DK, DK), F32), pltpu.VMEM((HALO, QKV_W), F32)],
        operands=(p, p, p, p, sst, tinv, doa, cq, a_row, dtb_row, nw_row))


def proj_bwd(dps, dxd, w, name, carry=None):
    s = dxd.shape[0]
    t = _tile(s, 512)
    n = len(dps)

    def body(*refs):
        dp_refs, dxd_ref, w_hbm, dx_ref, w_v = refs[:n], refs[n], refs[n + 1], refs[n + 2], refs[n + 3]

        @pl.when(pl.program_id(0) == 0)
        def _():
            pltpu.sync_copy(w_hbm, w_v)

        dp = jnp.concatenate([r[...] for r in dp_refs], axis=1)
        dx_ref[...] = dxd_ref[...] + _mm_nt(dp, w_v[...])

    return _host_call(
        body, carry, s // t, grid=(s // t,), name=name,
        in_specs=[_row(t, dp.shape[1]) for dp in dps] + [_row(t, D), ANY],
        out_specs=[_row(t, D)], out_shape=[_sds((s, D))],
        scratch_shapes=[pltpu.VMEM((D, IN_COLS_PAD), BF16)], operands=(*dps, dxd, w))


def wgrad(x, dp, col, into, name):
    s, n = dp.shape
    tk = _tile(s, WGRAD_TOKENS)
    tn = next(c for c in (1024, 768, 512, 256, 128) if n % c == 0 and col % c == 0)
    block = col // tn

    def body(x_ref, dp_ref, *rest):
        o_ref = rest[-1]
        _acc(o_ref, _mm_tn(x_ref[...], dp_ref[...]), pl.program_id(1) == 0)

    operands = (x, dp) if into is None else (x, dp, into)
    return pl.pallas_call(
        body, grid=(n // tn, s // tk), name=name,
        in_specs=[pl.BlockSpec((tk, D), lambda j, k: (k, 0)), pl.BlockSpec((tk, tn), lambda j, k: (k, j))]
        + ([] if into is None else [ANY]),
        out_specs=pl.BlockSpec((D, tn), lambda j, k: (0, block + j)), out_shape=_sds((D, IN_COLS_PAD)),
        input_output_aliases={} if into is None else {2: 0},
        compiler_params=_params(2))(*operands)


def _rows_block(rows, cols):
    cap = max(HALO, (2 * 1024 * 1024) // (cols * 4))
    for cand in range(min(rows, cap) // HALO * HALO, HALO - 1, -HALO):
        if rows % cand == 0:
            return cand
    return rows


def adam_call(w, g, m, v, name):
    rows, cols = w.shape
    tr = _rows_block(rows, cols)
    c1 = 1.0 - ADAM_B1 ** ADAM_STEP
    c2 = 1.0 - ADAM_B2 ** ADAM_STEP

    def body(w_ref, g_ref, m_ref, v_ref, go_ref, d_ref, nm_ref, nv_ref):
        gv = g_ref[...]
        go_ref[...] = gv
        nm = ADAM_B1 * m_ref[...] + (1.0 - ADAM_B1) * gv
        nv = ADAM_B2 * v_ref[...] + (1.0 - ADAM_B2) * (gv * gv)
        d_ref[...] = -ADAM_LR * ((nm / c1) / (jnp.sqrt(nv / c2) + ADAM_EPS) + ADAM_WD * w_ref[...])
        nm_ref[...] = nm
        nv_ref[...] = nv

    spec = pl.BlockSpec((tr, cols), lambda i: (i, 0))
    return pl.pallas_call(
        body, grid=(rows // tr,), name=name, in_specs=[spec] * 4, out_specs=[spec] * 4,
        out_shape=[_sds((rows, cols))] * 4, compiler_params=_params())(w, g, m, v)


def _place():
    return lax.axis_index("x"), lax.axis_index("y"), lax.axis_index("c")


def _other_chips(x, y):
    return [(1 - x, y), (x, 1 - y), (1 - x, 1 - y)]


def _remote(src, dst, send_sem, recv_sem, to):
    return pltpu.make_async_remote_copy(src_ref=src, dst_ref=dst, send_sem=send_sem, recv_sem=recv_sem,
                                        device_id=to, device_id_type=MESH)


class _Gather:
    def __init__(self, ins, outs, send_sems, recv_sems, local_sems):
        self.ins, self.outs, self.n = ins, outs, len(ins)
        self.send_sems, self.recv_sems, self.local_sems = send_sems, recv_sems, local_sems
        self.x, self.y, self.c = _place()
        self.me = 2 * self.x + self.y
        self.chips = _other_chips(self.x, self.y)

    def _copy(self, t, k, slot, part, to, src=None):
        dst = self.outs[t].at[slot, part]
        return _remote(dst if src is None else src, dst, self.send_sems.at[6 * t + k], self.recv_sems.at[6 * t + k], to)

    def _mine(self):
        return [pltpu.make_async_copy(self.ins[t].at[p], self.outs[t].at[self.me, p], self.local_sems.at[2 * t + p])
                for t in range(self.n) for p in range(2)]

    def _first(self):
        return [self._copy(t, k, self.me, self.c, (cx, cy, self.c), src=self.ins[t].at[self.c])
                for k, (cx, cy) in enumerate(self.chips) for t in range(self.n)]

    def start(self):
        for cp in self._mine() + self._first():
            cp.start()

    def finish(self):
        x, y, c = self.x, self.y, self.c
        passed = []
        for k, (cx, cy) in enumerate(self.chips):
            for t in range(self.n):
                self._copy(t, k, 2 * cx + cy, c, (x, y, c)).wait_recv()
                passed.append(self._copy(t, 3 + k, 2 * cx + cy, c, (x, y, 1 - c)))
                passed[-1].start()
        for k, (cx, cy) in enumerate(self.chips):
            for t in range(self.n):
                self._copy(t, 3 + k, 2 * cx + cy, 1 - c, (x, y, c)).wait_recv()
        for cp in self._first() + passed:
            cp.wait_send()
        for cp in self._mine():
            cp.wait()


class _Carried(NamedTuple):
    ins: tuple
    out_shapes: tuple
    scratch: tuple
    make: Callable
    aliases: dict


def _host_call(body, carry, steps, *, grid, name, in_specs, out_specs, out_shape, scratch_shapes, operands):
    in_specs, out_specs, out_shape, scratch_shapes = list(in_specs), list(out_specs), list(out_shape), list(scratch_shapes)
    aliases = {}
    if carry is not None:
        n_in, n_out, n_scr = len(in_specs), len(out_specs), len(scratch_shapes)
        n_ci, n_co = len(carry.ins), len(carry.out_shapes)
        plain = body

        def body(*refs):
            ins, cins = refs[:n_in], refs[n_in:n_in + n_ci]
            outs = refs[n_in + n_ci:n_in + n_ci + n_out]
            couts = refs[n_in + n_ci + n_out:n_in + n_ci + n_out + n_co]
            rest = refs[n_in + n_ci + n_out + n_co:]
            exchange = carry.make(cins, couts, *rest[n_scr:])
            pl.when(pl.program_id(0) == 0)(exchange.start)
            plain(*ins, *outs, *rest[:n_scr])
            pl.when(pl.program_id(0) == steps - 1)(exchange.finish)

        aliases = {n_in + i: n_out + j for i, j in carry.aliases.items()}
        in_specs += [ANY] * n_ci
        out_specs += [ANY] * n_co
        out_shape += list(carry.out_shapes)
        scratch_shapes += list(carry.scratch)
        operands = tuple(operands) + tuple(carry.ins)
    return pl.pallas_call(
        body, grid=grid, name=name, in_specs=in_specs, out_specs=out_specs, out_shape=out_shape,
        scratch_shapes=scratch_shapes, input_output_aliases=aliases, compiler_params=_params(len(grid)))(*operands)


def exchange(carry, name):
    n_i, n_o = len(carry.ins), len(carry.out_shapes)

    def body(*refs):
        ex = carry.make(refs[:n_i], refs[n_i:n_i + n_o], *refs[n_i + n_o:])
        ex.start()
        ex.finish()

    return pl.pallas_call(
        body, name=name, in_specs=[ANY] * n_i, out_specs=[ANY] * n_o, out_shape=list(carry.out_shapes),
        scratch_shapes=list(carry.scratch), input_output_aliases=dict(carry.aliases),
        compiler_params=pltpu.CompilerParams(vmem_limit_bytes=VMEM_LIMIT))(*carry.ins)


def _dma_sems(*counts):
    return tuple(pltpu.SemaphoreType.DMA((n,)) for n in counts)


def carried_gather(shards):
    n = len(shards)
    return _Carried(tuple(shards), tuple(_sds((N_CHIPS,) + a.shape, a.dtype) for a in shards),
                    _dma_sems(6 * n, 6 * n, 2 * n), _Gather, {})


class _PairSwap:
    def __init__(self, ins, outs, send_sems, recv_sems):
        x, y, c = _place()
        self.copies = [_remote(ins[t].at[:, 1 - c], outs[t], send_sems.at[t], recv_sems.at[t], (x, y, 1 - c))
                       for t in range(len(ins))]

    def start(self):
        for cp in self.copies:
            cp.start()

    def finish(self):
        for cp in self.copies:
            cp.wait()


def carried_pair_swap(views):
    n = len(views)
    return _Carried(tuple(views), tuple(_sds((v.shape[0],) + v.shape[2:]) for v in views), _dma_sems(n, n), _PairSwap, {})


class _Scatter:
    def __init__(self, srcs, outs, send_sems, recv_sems, local_sems, pieces):
        self.srcs, self.outs, self.pieces, self.n = srcs, outs, pieces, len(pieces)
        self.send_sems, self.recv_sems, self.local_sems = send_sems, recv_sems, local_sems
        self.x, self.y, self.c = _place()
        self.me = 2 * self.x + self.y

    def _piece(self, t, k):
        idx, lead, cols = self.pieces[t][k]
        ref = self.srcs[idx].at[lead]
        return ref if cols is None else ref.at[:, pl.ds(cols[0], cols[1])]

    def _local(self, t, k):
        return pltpu.make_async_copy(self._piece(t, k), self.outs[t].at[k], self.local_sems.at[t])

    def _each_chip(self, mine, others):
        for k in range(N_CHIPS):
            pl.when(self.me == k)(functools.partial(mine, k))
            pl.when(self.me != k)(functools.partial(others, k))

    def start(self):
        def mine(k):
            for t in range(self.n):
                self._local(t, k).start()

        def others(k):
            for t in range(self.n):
                _remote(self._piece(t, k), self.outs[t].at[self.me], self.send_sems.at[N_CHIPS * t + k],
                        self.recv_sems.at[N_CHIPS * t + self.me], (k // 2, k % 2, self.c)).start()

        self._each_chip(mine, others)

    def finish(self):
        def mine(k):
            for t in range(self.n):
                self._local(t, k).wait()

        def others(k):
            for t in range(self.n):
                cp = _remote(self._piece(t, k), self.outs[t].at[k], self.send_sems.at[N_CHIPS * t + k],
                             self.recv_sems.at[N_CHIPS * t + k], (self.x, self.y, self.c))
                cp.wait_recv()
                cp.wait_send()

        self._each_chip(mine, others)


def carried_scatter(srcs, pieces, part_shapes):
    n = len(pieces)
    return _Carried(tuple(srcs), tuple(_sds((N_CHIPS,) + tuple(shp), srcs[0].dtype) for shp in part_shapes),
                    _dma_sems(N_CHIPS * n, N_CHIPS * n, n), functools.partial(_Scatter, pieces=pieces), {})


class _PairJoin:
    def __init__(self, ins, outs, send_sems, recv_sems, layer):
        self.ins, self.outs, self.layer, self.n = ins, outs, layer, len(ins)
        self.send_sems, self.recv_sems = send_sems, recv_sems
        self.x, self.y, self.c = _place()

    def _copy(self, t, half, to):
        return _remote(self.ins[t].at[self.layer, self.c], self.outs[t].at[self.layer, half],
                       self.send_sems.at[t], self.recv_sems.at[t], to)

    def start(self):
        for t in range(self.n):
            self._copy(t, self.c, (self.x, self.y, 1 - self.c)).start()

    def finish(self):
        for t in range(self.n):
            self._copy(t, self.c, (self.x, self.y, 1 - self.c)).wait_send()
            self._copy(t, 1 - self.c, (self.x, self.y, self.c)).wait_recv()


def carried_join(bufs, layer):
    n = len(bufs)
    return _Carried(tuple(bufs), tuple(_sds(b.shape) for b in bufs), _dma_sems(n, n),
                    functools.partial(_PairJoin, layer=layer), {t: t for t in range(n)})


def pair_add_half(mine, theirs, c_vec, name):
    g, _, h, b = mine.shape
    tr = _rows_block(h, b)

    def body(c_ref, a_ref, b_ref, o_ref):
        o_ref[...] = (a_ref[...] + b_ref[...]).astype(BF16)

    part = pl.BlockSpec((None, tr, b), lambda j, i, c: (j, i, 0))
    grid_spec = pltpu.PrefetchScalarGridSpec(
        num_scalar_prefetch=1, grid=(g, h // tr),
        in_specs=[pl.BlockSpec((None, None, tr, b), lambda j, i, c: (j, c[0], i, 0)), part], out_specs=part)
    return pl.pallas_call(body, grid_spec=grid_spec, name=name, out_shape=_sds((g, h, b), BF16),
                          compiler_params=_params(2))(c_vec, mine, theirs)


def chips_add_into(recv, into, layer, c_vec, name):
    n, h, b = recv.shape
    tr = _rows_block(h, b)

    def body(c_ref, r0, r1, r2, r3, *rest):
        rest[-1][...] = ((r0[...].astype(F32) + r1[...].astype(F32)) + r2[...].astype(F32)) + r3[...].astype(F32)

    grid_spec = pltpu.PrefetchScalarGridSpec(
        num_scalar_prefetch=1, grid=(h // tr,),
        in_specs=[pl.BlockSpec((None, tr, b), lambda i, c, k=k: (k, i, 0)) for k in range(n)]
        + ([] if into is None else [ANY]),
        out_specs=pl.BlockSpec((None, None, tr, b), lambda i, c: (layer, c[0], i, 0)))
    return pl.pallas_call(
        body, grid_spec=grid_spec, name=name, out_shape=_sds((2, 2, h, b)),
        input_output_aliases={} if into is None else {1 + n: 0},
        compiler_params=_params())(c_vec, *([recv] * n), *(() if into is None else (into,)))


N_DEV = 8


class _AllSum:
    def __init__(self, ins, outs, buf, total, send_sems, recv_sems, local_sem):
        self.v, self.out, self.buf, self.total = ins[0], outs[0], buf, total
        self.send_sems, self.recv_sems, self.local_sem = send_sems, recv_sems, local_sem
        self.x, self.y, self.c = _place()
        self.me, self.sibling = (self.x, self.y, self.c), (self.x, self.y, 1 - self.c)
        self.chips = _other_chips(self.x, self.y)

    def _slot(self, px, py, pc):
        return self.buf.at[4 * px + 2 * py + pc]

    def _copy(self, k, block, to):
        return _remote(self._slot(*block), self._slot(*block), self.send_sems.at[k], self.recv_sems.at[k], to)

    def _first(self):
        return [self._copy(0, self.me, self.sibling)] + [
            self._copy(1 + k, self.me, (*chip, self.c)) for k, chip in enumerate(self.chips)]

    def start(self):
        load = pltpu.make_async_copy(self.v, self._slot(*self.me), self.local_sem)
        load.start()
        load.wait()
        for cp in self._first():
            cp.start()

    def finish(self):
        c = self.c
        passed = [self._copy(4 + k, (*chip, c), self.sibling) for k, chip in enumerate(self.chips)]
        for k, chip in enumerate(self.chips):
            self._copy(1 + k, (*chip, c), self.me).wait_recv()
            passed[k].start()
        self._copy(0, self.sibling, self.me).wait_recv()
        for k, chip in enumerate(self.chips):
            self._copy(4 + k, (*chip, 1 - c), self.me).wait_recv()
        for cp in self._first() + passed:
            cp.wait_send()
        acc = self.buf[0]
        for d in range(1, N_DEV):
            acc = acc + self.buf[d]
        self.total[...] = acc
        store = pltpu.make_async_copy(self.total, self.out, self.local_sem)
        store.start()
        store.wait()


def carried_allsum(v):
    rows, lanes = v.shape
    scratch = (pltpu.VMEM((N_DEV, rows, lanes), F32), pltpu.VMEM((rows, lanes), F32)) + _dma_sems(7, 7) + (
        pltpu.SemaphoreType.DMA,)
    return _Carried((v,), (_sds((rows, lanes)),), scratch, _AllSum, {})


class _Both:
    def __init__(self, *exchanges):
        self.exchanges = exchanges

    def start(self):
        for ex in self.exchanges:
            ex.start()

    def finish(self):
        for ex in self.exchanges:
            ex.finish()


def carried_both(a, b):
    ai, ao, asc = len(a.ins), len(a.out_shapes), len(a.scratch)

    def make(ins, outs, *scratch):
        return _Both(a.make(ins[:ai], outs[:ao], *scratch[:asc]), b.make(ins[ai:], outs[ao:], *scratch[asc:]))

    aliases = {**a.aliases, **{ai + i: ao + j for i, j in b.aliases.items()}}
    return _Carried(a.ins + b.ins, a.out_shapes + b.out_shapes, a.scratch + b.scratch, make, aliases)


BIG = ("w_in", "w_branch_a", "w_branch_b", "w_out", "w_up", "w_down")
CONV = ("conv_qkv", "conv_ffn")
REPL =("a_log", "dt_bias", "dn_norm_w", "sg_ln_g", "sg_ln_b", "w_spatial", "b_spatial", "ln1_g", "ln1_b", "ln2_g", "ln2_b")


def _pad_rows(flat, mult):
    n = flat.shape[0]
    unit = mult * LANES
    total = -(-n // unit) * unit
    return jnp.pad(flat, (0, total - n)).reshape(total // LANES, LANES)


def _pack(arrs, mult):
    return _pad_rows(jnp.concatenate([a.reshape(-1) for a in arrs]), mult)


def _unpack(flat, shapes):
    out, off = [], 0
    for shp in shapes:
        n = math.prod(shp)
        out.append(flat[off:off + n].reshape(shp))
        off += n
    return out


def kernel(x, w_in, conv_qkv, a_log, dt_bias, dn_norm_w, w_branch_a, sg_ln_g, sg_ln_b, w_spatial, b_spatial, w_branch_b, w_out, ln1_g, ln1_b, w_up, conv_ffn, w_down, ln2_g, ln2_b, loss_target, m_w_in, m_conv_qkv, m_a_log, m_dt_bias, m_dn_norm_w, m_w_branch_a, m_sg_ln_g, m_sg_ln_b, m_w_spatial, m_b_spatial, m_w_branch_b, m_w_out, m_ln1_g, m_ln1_b, m_w_up, m_conv_ffn, m_w_down, m_ln2_g, m_ln2_b, v_w_in, v_conv_qkv, v_a_log, v_dt_bias, v_dn_norm_w, v_w_branch_a, v_sg_ln_g, v_sg_ln_b, v_w_spatial, v_b_spatial, v_w_branch_b, v_w_out, v_ln1_g, v_ln1_b, v_w_up, v_conv_ffn, v_w_down, v_ln2_g, v_ln2_b):
    names = ("w_in", "conv_qkv", "a_log", "dt_bias", "dn_norm_w", "w_branch_a", "sg_ln_g", "sg_ln_b", "w_spatial",
             "b_spatial", "w_branch_b", "w_out", "ln1_g", "ln1_b", "w_up", "conv_ffn", "w_down", "ln2_g", "ln2_b")
    w = dict(zip(names, (w_in, conv_qkv, a_log, dt_bias, dn_norm_w, w_branch_a, sg_ln_g, sg_ln_b, w_spatial,
                         b_spatial, w_branch_b, w_out, ln1_g, ln1_b, w_up, conv_ffn, w_down, ln2_g, ln2_b)))
    m = dict(zip(names, (m_w_in, m_conv_qkv, m_a_log, m_dt_bias, m_dn_norm_w, m_w_branch_a, m_sg_ln_g, m_sg_ln_b,
                         m_w_spatial, m_b_spatial, m_w_branch_b, m_w_out, m_ln1_g, m_ln1_b, m_w_up, m_conv_ffn,
                         m_w_down, m_ln2_g, m_ln2_b)))
    v = dict(zip(names, (v_w_in, v_conv_qkv, v_a_log, v_dt_bias, v_dn_norm_w, v_w_branch_a, v_sg_ln_g, v_sg_ln_b,
                         v_w_spatial, v_b_spatial, v_w_branch_b, v_w_out, v_ln1_g, v_ln1_b, v_w_up, v_conv_ffn,
                         v_w_down, v_ln2_g, v_ln2_b)))
    chip = 2 * lax.axis_index("x") + lax.axis_index("y")
    s = x.shape[1]
    xs = x.reshape(s, D)
    tgt = loss_target.reshape(s, D)

    big_names, conv_names = list(BIG), list(CONV)

    def in_two(name, l):
        rows, cols = w[name].shape[1:]
        return w[name][l].astype(BF16).reshape(2, rows // 2, cols)

    def whole(name, landed):
        rows, cols = w[name].shape[1:]
        return landed.reshape(N_CHIPS, rows, cols)

    first = exchange(carried_gather([in_two("w_in", 0)] + [w[n] for n in conv_names]), "gather_first")
    got = [{"w_in": whole("w_in", first[0])}, {}]
    conv_taps = dict(zip(conv_names, first[1:]))
    narrow, wide = ["w_branch_a", "w_branch_b", "w_out"], ["w_up", "w_down"]
    carried = {"proj_fwd0": (0, narrow), "dn_fwd0": (0, wide), "ffn_fwd0": (1, ["w_in"] + narrow), "dn_fwd1": (1, wide)}

    def carry(call):
        if call not in carried:
            return None
        l, which = carried[call]
        return carried_gather([in_two(n, l) for n in which])

    def land(call, landed):
        l, which = carried.get(call, (0, []))
        got[l].update({n: whole(n, a) for n, a in zip(which, landed)})

    def lane_row(vec, off):
        return jnp.zeros((1, LANES), F32).at[0, off:off + vec.shape[0]].set(vec)

    def side_by_side(blocks):
        return jnp.concatenate([blocks[k] for k in range(N_CHIPS)], axis=1)

    def small_params(l):
        return dict(
            cq=side_by_side(conv_taps["conv_qkv"][:, l]),
            a_row=lane_row(w["a_log"][l], HEADS), dtb_row=lane_row(w["dt_bias"][l], HEADS),
            nw_row=w["dn_norm_w"][l].reshape(1, DK),
            lng=w["sg_ln_g"][l].reshape(1, SG_W), lnb=w["sg_ln_b"][l].reshape(1, SG_W),
            w_s=w["w_spatial"][l], bs_t=jnp.zeros((LANES, LANES), F32).at[:, :4].set(w["b_spatial"][l].T),
            g1=w["ln1_g"][l].reshape(1, D), b1=w["ln1_b"][l].reshape(1, D),
            cf=conv_taps["conv_ffn"][:, l],
            g2=w["ln2_g"][l].reshape(1, D), b2=w["ln2_b"][l].reshape(1, D))

    layers, saved = [], []
    h_in = xs
    for l in range(DEPTH):
        p = small_params(l)
        wi = side_by_side(got[l]["w_in"])
        p["w_in"] = jnp.concatenate([wi[:, :2048], wi[:, 2056:3080], wi[:, 3080:5128], wi[:, 2048:2056],
                                     jnp.zeros((D, IN_COLS_PAD - 5128), BF16)], axis=1)
        proj, *landed = proj_fwd(h_in, p["w_in"], f"proj_fwd{l}", carry=carry(f"proj_fwd{l}"))
        land(f"proj_fwd{l}", landed)
        oa, sst, tinv, *landed = dn_fwd(proj, p["cq"], p["a_row"], p["dtb_row"], p["nw_row"], f"dn_fwd{l}",
                                  carry=carry(f"dn_fwd{l}"))
        land(f"dn_fwd{l}", landed)
        ob = sg_fwd(proj, p["lng"], p["lnb"], p["w_s"], p["bs_t"], f"sg_fwd{l}")
        p.update(wa=side_by_side(got[l]["w_branch_a"]), wb=side_by_side(got[l]["w_branch_b"]),
                 wo=got[l]["w_out"].reshape(D, D))
        x1 = merge_fwd(oa, ob, proj, h_in, p["wa"], p["wb"], p["wo"], p["g1"], p["b1"], f"merge_fwd{l}")
        pre2, x2, *landed = ffn_fwd(x1, got[l]["w_up"], p["cf"], got[l]["w_down"], p["g2"], p["b2"], f"ffn_fwd{l}",
                                    carry=carry(f"ffn_fwd{l}"))
        land(f"ffn_fwd{l}", landed)
        layers.append(p)
        saved.append(dict(x=h_in, proj=proj, oa=oa, ob=ob, sst=sst, tinv=tinv, x1=x1, pre2=pre2))
        h_in = x2


    small_names = conv_names + list(REPL)
    grads = {n: [None] * DEPTH for n in small_names}
    c_vec = jnp.stack([lax.axis_index("c")]).astype(jnp.int32)
    tags = ("w_in", "w_a", "w_b", "w_out", "w_up0", "w_up1", "w_dn0", "w_dn1")
    groups = (1, 1, 1, N_CHIPS, 2, 2, 2, 2)
    ab_cols = w["w_branch_a"].shape[2]
    pieces = [
        [(0, (k,), None) for k in range(N_CHIPS)],
        [(1, (0,), (k * ab_cols, ab_cols)) for k in range(N_CHIPS)],
        [(2, (0,), (k * ab_cols, ab_cols)) for k in range(N_CHIPS)],
        [(3, (k,), None) for k in range(N_CHIPS)],
        [(4 + k % 2, (k // 2,), None) for k in range(N_CHIPS)],
        [(6 + k // 2, (k % 2,), None) for k in range(N_CHIPS)],
    ]
    part_shapes = [(w[n].shape[1] // 2, w[n].shape[2]) for n in big_names]
    arrays_of = ((0,), (1,), (2,), (3,), (4, 5), (6, 7))
    rest, ffn_part = (0, 1, 2, 3), (4, 5)
    bufs = {}

    def arrays(which):
        return [i for t in which for i in arrays_of[t]]

    def views(which, arrs):
        return [a.reshape(groups[i], 2, a.size // a.shape[-1] // (2 * groups[i]), a.shape[-1])
                for i, a in zip(arrays(which), arrs)]

    def pair_sums(l, which, mine, theirs):
        ids = arrays(which)
        sums = [pair_add_half(m_, t_, c_vec, f"reduce_pair_add{l}_{tags[i]}") for i, m_, t_ in zip(ids, mine, theirs)]
        if ids[0] == 0:
            pin = sums[0][0]
            natural = jnp.concatenate([pin[:, :2048], pin[:, C_BA:C_BA + 8], pin[:, 2048:C_BA]], axis=1)
            sums[0] = jnp.stack(jnp.split(natural, N_CHIPS, axis=1))
        return sums

    def scatter_of(which, srcs):
        place = {i: j for j, i in enumerate(arrays(which))}
        return carried_scatter(srcs, [[(place[i], lead, cols) for i, lead, cols in pieces[t]] for t in which],
                               [part_shapes[t] for t in which])

    def chip_sums(l, which, recv):
        for t, r in zip(which, recv):
            n = big_names[t]
            bufs[n] = chips_add_into(r, bufs.get(n), l, c_vec, f"reduce_chips_add{l}_{n}")
        return [bufs[big_names[t]] for t in which]

    def keep(which, joined):
        bufs.update({big_names[t]: b for t, b in zip(which, joined)})

    above = None
    for l in reversed(range(DEPTH)):
        p, a = layers[l], saved[l]
        if l == DEPTH - 1:
            dpre2, dg2, db2, loss_part = loss_ln_bwd(a["pre2"], tgt, p["g2"], p["b2"], "loss_ln2_bwd")
            loss = lax.psum(loss_part[0, 0], ("x", "y", "c"))
        else:
            dpre2, dg2, db2 = ln_bwd(a["pre2"], dy, p["g2"], p["b2"], f"ln2_bwd{l}")
        dx1, dcf0, *half0 = ffn_bwd(
            a["x1"], dpre2, dpre2, ALPHA, got[l]["w_up"], p["cf"], got[l]["w_down"], 0, f"ffn_bwd{l}a",
            carry=carried_pair_swap(above) if above else None)
        theirs = half0[3:]
        srcs = pair_sums(l + 1, rest, above, theirs) if above else None
        dx1, dcf1, *half1 = ffn_bwd(
            a["x1"], dpre2, dx1, 1.0, got[l]["w_up"], p["cf"], got[l]["w_down"], 1, f"ffn_bwd{l}b",
            carry=scatter_of(rest, srcs) if above else None)
        recv = half1[3:]
        summed = chip_sums(l + 1, rest, recv) if above else None
        ffn_grads = []
        x1_bf16 = a["x1"].astype(BF16)
        for h, (dua, dub, act) in enumerate((half0[:3], half1[:3])):
            dwup = wgrad_mm(x1_bf16, dua, 2, 0, None, f"wgrad_up{l}{'ab'[h]}_a")
            ffn_grads.append(wgrad_mm(x1_bf16, dub, 2, 1, dwup, f"wgrad_up{l}{'ab'[h]}_b"))
        for h, (dua, dub, act) in enumerate((half0[:3], half1[:3])):
            ffn_grads.append(wgrad_mm(act, dpre2, 1, 0, None, f"wgrad_down{l}{'ab'[h]}")[0])
        dwup0, dwup1, dwdn0, dwdn1 = ffn_grads
        doa, dob, dga, dgb, dxd, dg1, db1, h_mid, d_r, d_ya, d_yb, *joined = merge_bwd(
            a["oa"], a["ob"], a["proj"], a["x"], dx1, p["wa"], p["wb"], p["wo"], p["g1"], p["b1"], f"merge_bwd{l}",
            carry=carried_join(summed, l + 1) if above else None)
        keep(rest, joined)
        dwa = wgrad_mm(a["oa"], d_ya, 1, 0, None, f"wgrad_a{l}")[0]
        dwb = wgrad_mm(a["ob"], d_yb, 1, 0, None, f"wgrad_b{l}")[0]
        dwo = wgrad_mm(h_mid, d_r, 1, 0, None, f"wgrad_out{l}")[0]
        mine = views(ffn_part, [dwup0, dwup1, dwdn0, dwdn1])
        duv, dlng, dlnb, dws, dbs, *theirs = sg_bwd(a["proj"], dob, p["lng"], p["lnb"], p["w_s"], p["bs_t"], f"sg_bwd{l}",
                                                    carry=carried_pair_swap(mine))
        srcs = pair_sums(l, ffn_part, mine, theirs)
        dqkv, dz, dba, dcq, da, ddtb, dnw, *recv = dn_bwd(
            a["proj"], a["sst"], a["tinv"], doa, p["cq"], p["a_row"], p["dtb_row"], p["nw_row"], f"dn_bwd{l}",
            carry=scatter_of(ffn_part, srcs))
        summed = chip_sums(l, ffn_part, recv)
        dy, *joined = proj_bwd([dqkv, dz, duv, dga, dgb, dba], dxd, p["w_in"], f"proj_bwd{l}",
                               carry=carried_join(summed, l))
        keep(ffn_part, joined)
        dwi = None
        x_bf16 = a["x"].astype(BF16)
        for tag, dp, col in (("qkv", dqkv, 0), ("z", dz, C_Z), ("uv", duv, C_UV), ("ga", dga, C_GA), ("gb", dgb, C_GB),
                             ("ba", dba, C_BA)):
            dwi = wgrad(x_bf16, dp, col, dwi, f"wgrad_in{l}_{tag}")

        above = views(rest, [dwi, dwa, dwb, dwo])
        grads["conv_qkv"][l] = dcq
        grads["conv_ffn"][l] = jnp.concatenate([dcf0[0], dcf1[0], dcf0[1], dcf1[1]], axis=1)
        grads["a_log"][l] = da[0, HEADS:2 * HEADS]
        grads["dt_bias"][l] = ddtb[0, HEADS:2 * HEADS]
        grads["dn_norm_w"][l] = dnw[0]
        grads["sg_ln_g"][l] = dlng[0]
        grads["sg_ln_b"][l] = dlnb[0]
        grads["w_spatial"][l] = dws
        grads["b_spatial"][l] = dbs[:, :4].T
        grads["ln1_g"][l] = dg1[0]
        grads["ln1_b"][l] = db1[0]
        grads["ln2_g"][l] = dg2[0]
        grads["ln2_b"][l] = db2[0]
    grad_x = dy.reshape(x.shape)
    g_full = {n: jnp.stack(grads[n]) for n in small_names}

    theirs = exchange(carried_pair_swap(above), "reduce_pair")
    *recv, small = exchange(carried_both(scatter_of(rest, pair_sums(0, rest, above, theirs)),
                                         carried_allsum(_pack([g_full[n] for n in small_names], 8))), "reduce_chips")
    keep(rest, exchange(carried_join(chip_sums(0, rest, recv), 0), "reduce_join"))
    g_shard = {n: bufs[n].reshape(w[n].shape) for n in big_names}

    small_full = dict(zip(small_names, _unpack(small.reshape(-1), [g_full[n].shape for n in small_names])))
    for n in conv_names:
        width = w[n].shape[2]
        g_shard[n] = lax.dynamic_slice_in_dim(small_full[n], chip * width, width, axis=2)
    for n in REPL:
        g_shard[n] = small_full[n]

    delta, new_m, new_v = {}, {}, {}
    for n in big_names:
        shp = w[n].shape
        two_d = (shp[0] * shp[1], shp[2])
        g_, d_, m_, v_ = adam_call(w[n].reshape(two_d), g_shard[n].reshape(two_d), m[n].reshape(two_d), v[n].reshape(two_d), f"adam_{n}")
        g_shard[n], delta[n], new_m[n], new_v[n] = g_.reshape(shp), d_.reshape(shp), m_.reshape(shp), v_.reshape(shp)
    shapes = [w[n].shape for n in small_names]
    packs = [_pack([src[n] for n in small_names], 8) for src in (w, g_shard, m, v)]
    outs = adam_call(*packs, "adam_small")
    for dst, o in zip((delta, new_m, new_v), outs[1:]):
        dst.update(zip(small_names, _unpack(o.reshape(-1), shapes)))

    return (loss, grad_x, *[g_shard[n] for n in names], *[delta[n] for n in names],
            *[new_m[n] for n in names], *[new_v[n] for n in names])
```
